```python
import math
import jax, jax.numpy as jnp
from jax import lax
import numpy as np

D_MODEL = 2048
BATCH = 2
SEQ = 4096
DEPTH = 2
DEC_BATCH = 32
DEC_SEQ = 8
PAST_LEN = 16384
PAGE_SIZE = 128

N_EVEN = (DEPTH + 1) // 2
N_ODD = DEPTH // 2
MIX_WIDTH = D_MODEL
POOL_WIDTH = D_MODEL // 2
POOL_WINDOWS = (2, 4, 8, 16)
POOL_GROUP = POOL_WIDTH // len(POOL_WINDOWS)
POOL_STATE = max(POOL_WINDOWS) - 1
CONV_WIDTH = D_MODEL // 2
CONV_K = 3
CONV_STATE = CONV_K - 1
EVEN_IN = POOL_WIDTH + 3 * CONV_WIDTH
GM_WIDTH = D_MODEL // 2
GM_GROUPS = 8
GM_GROUP = GM_WIDTH // GM_GROUPS
CHUNK = 128
HEAD_DIM = 64
N_HEADS = (D_MODEL // 2) // HEAD_DIM
N_KV = max(1, N_HEADS // 8)
WINDOW = 128
N_BUCKETS = 32
ODD_IN = 2 * GM_WIDTH + (N_HEADS + 2 * N_KV) * HEAD_DIM
N_EXPERTS = 16
N_EXPERT_GROUPS = 4
EXP_PER_GROUP = N_EXPERTS // N_EXPERT_GROUPS
TOP_K = 2
D_EXPERT = D_MODEL // 4
EPS = 1e-6
NEG_INF = -1e30

kernel_name = 'hybrid_pool_conv_gmlp_swa_moe_step'


def rmsnorm(x, g):
    xf = x.astype(jnp.float32)
    y = xf * lax.rsqrt(jnp.mean(xf * xf, axis=-1, keepdims=True) + EPS)
    return (y * g.astype(jnp.float32)).astype(x.dtype)


def layernorm(x, g, b):
    xf = x.astype(jnp.float32)
    xc = xf - jnp.mean(xf, axis=-1, keepdims=True)
    y = xc * lax.rsqrt(jnp.mean(xc * xc, axis=-1, keepdims=True) + EPS)
    return (y * g.astype(jnp.float32) + b.astype(jnp.float32)).astype(x.dtype)


def _t5_bucket(dist):
    max_exact = N_BUCKETS // 2
    d = np.maximum(dist, 1)
    large = max_exact + (np.log(d / max_exact) / np.log(WINDOW / max_exact)
                         * (N_BUCKETS - max_exact)).astype(np.int64)
    large = np.minimum(large, N_BUCKETS - 1)
    return np.where(dist < max_exact, dist, large).astype(np.int32)


def pool_mix(p, p_prev, start, w_pool, pool_scale):
    n, t = p.shape[:2]
    ext = jnp.concatenate([p_prev, p], axis=1).astype(jnp.float32)
    cs = jnp.pad(jnp.cumsum(ext, axis=1), ((0, 0), (1, 0), (0, 0)))
    pos = start + jnp.arange(t)
    outs = []
    for gi, w in enumerate(POOL_WINDOWS):
        sl = slice(gi * POOL_GROUP, (gi + 1) * POOL_GROUP)
        hi = cs[:, POOL_STATE + 1:POOL_STATE + 1 + t, sl]
        lo = cs[:, POOL_STATE + 1 - w:POOL_STATE + 1 - w + t, sl]
        cnt = jnp.minimum(pos + 1, w).astype(jnp.float32)[None, :, None]
        outs.append((hi - lo) / cnt - ext[:, POOL_STATE:, sl])
    d = jnp.stack(outs, axis=2)
    y = jnp.einsum('ntgc,gce->ntge', d, w_pool).reshape(n, t, POOL_WIDTH)
    return (y * pool_scale).astype(p.dtype)


def short_conv(z, z_prev, conv_w):
    t = z.shape[1]
    ext = jnp.concatenate([z_prev, z], axis=1)
    return sum(conv_w[j] * ext[:, j:j + t] for j in range(CONV_K))


def chunk_spatial(u, v, w_s, b_s):
    n, t = v.shape[:2]
    nc = -(-t // CHUNK)
    pad = nc * CHUNK - t
    vp = jnp.pad(v, ((0, 0), (0, pad), (0, 0))).reshape(n, nc, CHUNK, GM_GROUPS, GM_GROUP)
    ws = w_s * jnp.tril(jnp.ones((CHUNK, CHUNK), w_s.dtype))
    mixed = jnp.einsum('gts,ncsgd->nctgd', ws, vp) + b_s.T[None, None, :, :, None]
    mixed = mixed.reshape(n, nc * CHUNK, GM_WIDTH)[:, :t]
    return u * mixed


def swa_attention(q, k, v, k_prev, v_prev, start, sinks, rel_table):
    n, t = q.shape[:2]
    grp = N_HEADS // N_KV
    nb = -(-t // WINDOW)
    pad = nb * WINDOW - t
    qb = jnp.pad(q, ((0, 0), (0, pad), (0, 0), (0, 0))).reshape(n, nb, WINDOW, N_KV, grp, HEAD_DIM)

    def band(new, prev):
        ext = jnp.concatenate([prev, jnp.pad(new, ((0, 0), (0, pad), (0, 0), (0, 0)))], axis=1)
        ext = ext.reshape(n, nb + 1, WINDOW, N_KV, HEAD_DIM)
        return jnp.concatenate([ext[:, :-1], ext[:, 1:]], axis=2)

    kb = band(k, k_prev)
    vb = band(v, v_prev)
    s = jnp.einsum('nbqhgd,nbshd->nbhgqs', qb, kb,
                   preferred_element_type=jnp.float32) * (HEAD_DIM ** -0.5)
    dist = WINDOW + np.arange(WINDOW)[:, None] - np.arange(2 * WINDOW)[None, :]
    bias = jnp.take(rel_table.astype(jnp.float32), _t5_bucket(np.clip(dist, 0, WINDOW - 1)), axis=0)
    bias = jnp.transpose(bias, (2, 0, 1)).reshape(N_KV, grp, WINDOW, 2 * WINDOW)
    kpos = start - WINDOW + WINDOW * np.arange(nb)[:, None] + np.arange(2 * WINDOW)[None, :]
    valid = ((dist >= 0) & (dist < WINDOW))[None] & (kpos >= 0)[:, None, :]
    s = jnp.where(valid[None, :, None, None], s + bias, NEG_INF)
    sink = sinks.astype(jnp.float32).reshape(N_KV, grp)[:, :, None]
    m = jnp.maximum(s.max(axis=-1), sink)
    pr = jnp.exp(s - m[..., None])
    pr = pr / (pr.sum(axis=-1) + jnp.exp(sink - m))[..., None]
    o = jnp.einsum('nbhgqs,nbshd->nbqhgd', pr, vb.astype(jnp.float32))
    return o.reshape(n, nb * WINDOW, N_HEADS * HEAD_DIM)[:, :t].astype(q.dtype)


def even_mixer(h, pool_prev, conv_prev, start, w_in, w_out, w_pool, pool_scale, conv_w):
    proj = jnp.einsum('ntd,de->nte', h, w_in)
    p, xin, gb, gc = jnp.split(proj, [POOL_WIDTH, POOL_WIDTH + CONV_WIDTH,
                                      POOL_WIDTH + 2 * CONV_WIDTH], axis=-1)
    ya = pool_mix(p, pool_prev, start, w_pool, pool_scale)
    z = gc * xin
    yb = gb * short_conv(z, conv_prev, conv_w)
    y = jnp.einsum('nte,ed->ntd', jnp.concatenate([ya, yb], axis=-1), w_out)
    new_pool = jnp.concatenate([pool_prev, p], axis=1)[:, -POOL_STATE:]
    new_conv = jnp.concatenate([conv_prev, z], axis=1)[:, -CONV_STATE:]
    return y, new_pool, new_conv


def odd_mixer(h, k_prev, v_prev, start, w_in, w_out, gm_g, gm_b, gm_ws, gm_bs, sinks, rel_table):
    n, t, _ = h.shape
    proj = jnp.einsum('ntd,de->nte', h, w_in)
    s1 = GM_WIDTH
    s2 = 2 * GM_WIDTH
    s3 = s2 + N_HEADS * HEAD_DIM
    s4 = s3 + N_KV * HEAD_DIM
    u, vg, q, k, va = jnp.split(proj, [s1, s2, s3, s4], axis=-1)
    vg = layernorm(vg, gm_g, gm_b)
    yc = chunk_spatial(u, vg, gm_ws, gm_bs)
    q = q.reshape(n, t, N_HEADS, HEAD_DIM)
    k = k.reshape(n, t, N_KV, HEAD_DIM)
    va = va.reshape(n, t, N_KV, HEAD_DIM)
    yd = swa_attention(q, k, va, k_prev, v_prev, start, sinks, rel_table)
    y = jnp.einsum('nte,ed->ntd', jnp.concatenate([yc, yd], axis=-1), w_out)
    new_k = jnp.concatenate([k_prev, k], axis=1)[:, -WINDOW:]
    new_v = jnp.concatenate([v_prev, va], axis=1)[:, -WINDOW:]
    return y, new_k, new_v, vg


def moe(h, w_router, b_router, w_gate, w_up, w_down):
    n, t, d = h.shape
    ht = h.reshape(n * t, d)
    logits = jnp.einsum('td,de->te', ht, w_router, preferred_element_type=jnp.float32)
    s = jax.nn.sigmoid(logits)
    sg = (s + b_router.astype(jnp.float32)).reshape(-1, N_EXPERT_GROUPS, EXP_PER_GROUP)
    gscore = lax.top_k(sg, TOP_K)[0].sum(axis=-1)
    gsel = jnp.argmax(gscore, axis=-1)
    in_group = jnp.einsum('tg,tge->te', jax.nn.one_hot(gsel, N_EXPERT_GROUPS, dtype=jnp.float32), sg)
    _, idx = lax.top_k(in_group, TOP_K)
    experts = gsel[:, None] * EXP_PER_GROUP + idx
    wsel = jnp.take_along_axis(s, experts, axis=1)
    wsel = wsel / wsel.sum(axis=-1, keepdims=True)
    gate = jnp.einsum('tk,tke->te', wsel, jax.nn.one_hot(experts, N_EXPERTS, dtype=jnp.float32))
    a = jnp.einsum('td,edf->tef', ht, w_gate)
    b = jnp.einsum('td,edf->tef', ht, w_up)
    hid = jax.nn.silu(a) * b * gate[:, :, None].astype(ht.dtype)
    out = jnp.einsum('tef,efd->td', hid, w_down)
    return out.reshape(n, t, d).astype(h.dtype)


def trunk(x, c, start, pool_prev, conv_prev, k_prev, v_prev, weights):
    (w_ada, b_ada, norm_g, final_norm_g, w_in_even, w_out_even, w_pool, pool_scale, conv_w,
     w_in_odd, w_out_odd, gm_norm_g, gm_norm_b, gm_w_s, gm_b_s, attn_sinks, rel_bias,
     w_router, b_router, w_gate, w_up, w_down) = weights
    pools, convs, ks, vs, gms = [], [], [], [], []
    for layer in range(DEPTH):
        mod = jnp.einsum('nd,de->ne', jax.nn.silu(c), w_ada[layer]) + b_ada[layer]
        sh1, sc1, g1, sh2, sc2, g2 = jnp.split(mod[:, None, :], 6, axis=-1)
        h = rmsnorm(x, norm_g[layer, 0]) * (1 + sc1) + sh1
        i = layer // 2
        if layer % 2 == 0:
            y, p_new, c_new = even_mixer(h, pool_prev[i], conv_prev[i], start, w_in_even[i],
                                         w_out_even[i], w_pool[i], pool_scale[i], conv_w[i])
            pools.append(p_new)
            convs.append(c_new)
        else:
            y, k_new, v_new, gm_v = odd_mixer(h, k_prev[i], v_prev[i], start, w_in_odd[i],
                                              w_out_odd[i], gm_norm_g[i], gm_norm_b[i],
                                              gm_w_s[i], gm_b_s[i], attn_sinks[i], rel_bias)
            ks.append(k_new)
            vs.append(v_new)
            gms.append(gm_v)
        x = x + g1 * y
        h = rmsnorm(x, norm_g[layer, 1]) * (1 + sc2) + sh2
        x = x + g2 * moe(h, w_router, b_router, w_gate[layer], w_up[layer], w_down[layer])
    return (rmsnorm(x, final_norm_g), jnp.stack(pools), jnp.stack(convs),
            jnp.stack(ks), jnp.stack(vs), jnp.stack(gms))


def setup_inputs(seed: int = 0) -> dict:
    key = jax.random.key(seed)
    keys = iter(jax.random.split(key, 32))

    def nrm(shape, scale):
        return jax.random.normal(next(keys), shape, jnp.float32) * scale

    d = D_MODEL
    return {
        'x_prompt': nrm((BATCH, SEQ, d), 1.0),
        'x_sample': nrm((DEC_BATCH, DEC_SEQ, d), 1.0),
        'state_pool': nrm((N_EVEN, DEC_BATCH, POOL_STATE, POOL_WIDTH), 1.0),
        'state_conv': nrm((N_EVEN, DEC_BATCH, CONV_STATE, CONV_WIDTH), 1.0),
        'cache_swa_k': nrm((N_ODD, DEC_BATCH, WINDOW, N_KV, HEAD_DIM), 1.0),
        'cache_swa_v': nrm((N_ODD, DEC_BATCH, WINDOW, N_KV, HEAD_DIM), 1.0),
        'c_prompt': nrm((BATCH, d), 1.0),
        'c_sample': nrm((DEC_BATCH, d), 1.0),
        'w_ada': nrm((DEPTH, d, 6 * d), 0.5 * d ** -0.5),
        'b_ada': nrm((DEPTH, 6 * d), 0.02),
        'norm_g': 1.0 + nrm((DEPTH, 2, d), 0.05),
        'final_norm_g': 1.0 + nrm((d,), 0.05),
        'w_in_even': nrm((N_EVEN, d, EVEN_IN), d ** -0.5),
        'w_out_even': nrm((N_EVEN, MIX_WIDTH, d), MIX_WIDTH ** -0.5),
        'w_pool': nrm((N_EVEN, len(POOL_WINDOWS), POOL_GROUP, POOL_GROUP), POOL_GROUP ** -0.5),
        'pool_scale': 1.0 + nrm((N_EVEN, POOL_WIDTH), 0.1),
        'conv_w': nrm((N_EVEN, CONV_K, CONV_WIDTH), CONV_K ** -0.5),
        'w_in_odd': nrm((N_ODD, d, ODD_IN), d ** -0.5),
        'w_out_odd': nrm((N_ODD, MIX_WIDTH, d), MIX_WIDTH ** -0.5),
        'gm_norm_g': 1.0 + nrm((N_ODD, GM_WIDTH), 0.05),
        'gm_norm_b': nrm((N_ODD, GM_WIDTH), 0.02),
        'gm_w_s': nrm((N_ODD, GM_GROUPS, CHUNK, CHUNK), CHUNK ** -0.5),
        'gm_b_s': 1.0 + nrm((N_ODD, GM_GROUPS, CHUNK), 0.1),
        'attn_sinks': nrm((N_ODD, N_HEADS), 0.5),
        'rel_bias': nrm((N_BUCKETS, N_HEADS), 0.1),
        'w_router': nrm((d, N_EXPERTS), d ** -0.5),
        'b_router': nrm((N_EXPERTS,), 0.01),
        'w_gate': nrm((DEPTH, N_EXPERTS, d, D_EXPERT), d ** -0.5),
        'w_up': nrm((DEPTH, N_EXPERTS, d, D_EXPERT), d ** -0.5),
        'w_down': nrm((DEPTH, N_EXPERTS, D_EXPERT, d), D_EXPERT ** -0.5),
    }


def reference(x_prompt, x_sample, state_pool, state_conv, cache_swa_k, cache_swa_v, c_prompt,
              c_sample, w_ada, b_ada, norm_g, final_norm_g, w_in_even, w_out_even, w_pool,
              pool_scale, conv_w, w_in_odd, w_out_odd, gm_norm_g, gm_norm_b, gm_w_s, gm_b_s,
              attn_sinks, rel_bias, w_router, b_router, w_gate, w_up, w_down):
    weights = (w_ada, b_ada, norm_g, final_norm_g, w_in_even, w_out_even, w_pool, pool_scale,
               conv_w, w_in_odd, w_out_odd, gm_norm_g, gm_norm_b, gm_w_s, gm_b_s, attn_sinks,
               rel_bias, w_router, b_router, w_gate, w_up, w_down)
    dt = x_prompt.dtype
    nbp = x_prompt.shape[0]
    pool0 = jnp.zeros((N_EVEN, nbp, POOL_STATE, POOL_WIDTH), dt)
    conv0 = jnp.zeros((N_EVEN, nbp, CONV_STATE, CONV_WIDTH), dt)
    kv0 = jnp.zeros((N_ODD, nbp, WINDOW, N_KV, HEAD_DIM), dt)
    y_prompt, pool_p, conv_p, k_p, v_p, _ = trunk(x_prompt, c_prompt, 0, pool0, conv0, kv0, kv0,
                                                  weights)
    y_sample, pool_s, conv_s, k_s, v_s, gm_s = trunk(x_sample, c_sample, PAST_LEN, state_pool,
                                                     state_conv, cache_swa_k, cache_swa_v, weights)
    return (y_prompt, y_sample, pool_p, pool_s, conv_p, conv_s, k_p, k_s, v_p, v_s, gm_s)
```

```python
import functools

import numpy as np
import jax
import jax.numpy as jnp
from jax import lax
from jax.experimental import pallas as pl
from jax.experimental.pallas import tpu as pltpu

F32 = jnp.float32
BF16 = jnp.bfloat16
I32 = jnp.int32
U32 = jnp.uint32

D_MODEL = 2048
POOL_WINDOWS = (2, 4, 8, 16)
POOL_WIDTH = 1024
POOL_GROUP = 256
POOL_STATE = 15
CONV_WIDTH = 1024
CONV_K = 3
GM_WIDTH = 1024
GM_GROUPS = 8
GM_GROUP = 128
CHUNK = 128
HEAD_DIM = 64
N_HEADS = 16
N_KV = 2
WINDOW = 128
N_BUCKETS = 32
N_EXPERTS = 16
N_EXPERT_GROUPS = 4
EXP_PER_GROUP = 4
TOP_K = 2
D_EXPERT = 512
EPS = 1e-6
NEG_INF = -1e30
PAST_LEN = 16384

V7X_VMEM_BYTES = 64 * 1024 * 1024
V7X_SUBLANES = 8
V7X_LANES = 128
VMEM_LIMIT = 56 * 1024 * 1024

ROW_TILE = 256
POOL_HALO = 16
CONV_HALO = 8
MOE_TILE = 256


def _cparams(n_axes):
    return pltpu.CompilerParams(dimension_semantics=("arbitrary",) * n_axes,
                                vmem_limit_bytes=VMEM_LIMIT)


def _rms(x, g):
    return x * lax.rsqrt(jnp.mean(x * x, axis=-1, keepdims=True) + EPS) * g


def _adaln_kernel(c_ref, w_ref, b_ref, o_ref):
    c = c_ref[...]
    a = (c * jax.nn.sigmoid(c)).astype(BF16)
    o_ref[0] = jnp.dot(a, w_ref[0].astype(BF16), preferred_element_type=F32) + b_ref[0]


def _adaln(c_all, w_ada, b_ada):
    depth, d, n6 = w_ada.shape
    m = c_all.shape[0]
    tn = 1024
    return pl.pallas_call(
        _adaln_kernel,
        grid=(depth, n6 // tn),
        in_specs=[pl.BlockSpec((m, d), lambda l, j: (0, 0)),
                  pl.BlockSpec((1, d, tn), lambda l, j: (l, 0, j)),
                  pl.BlockSpec((1, 1, tn), lambda l, j: (l, 0, j))],
        out_specs=pl.BlockSpec((1, m, tn), lambda l, j: (l, 0, j)),
        out_shape=jax.ShapeDtypeStruct((depth, m, n6), F32),
        compiler_params=_cparams(2),
        name="adaln",
    )(c_all, w_ada, b_ada.reshape(depth, 1, n6))


def _norm_mod_kernel(x_ref, g_ref, sc_ref, sh_ref, h_ref):
    h = _rms(x_ref[...], g_ref[...]) * (1.0 + sc_ref[0]) + sh_ref[0]
    h_ref[...] = h.astype(BF16)


def _mod_spec(mod, tiles_per_mod):
    _, mrows, d = mod.shape
    return pl.BlockSpec((1, mrows, d), lambda i: (i // tiles_per_mod, 0, 0))


def _norm_mod(x2, g, sc, sh, tiles_per_mod):
    rows, d = x2.shape
    return pl.pallas_call(
        _norm_mod_kernel,
        grid=(rows // ROW_TILE,),
        in_specs=[pl.BlockSpec((ROW_TILE, d), lambda i: (i, 0)),
                  pl.BlockSpec((1, d), lambda i: (0, 0)),
                  _mod_spec(sc, tiles_per_mod), _mod_spec(sh, tiles_per_mod)],
        out_specs=pl.BlockSpec((ROW_TILE, d), lambda i: (i, 0)),
        out_shape=jax.ShapeDtypeStruct((rows, d), BF16),
        compiler_params=_cparams(1),
        name="norm_mod",
    )(x2, g.reshape(1, d), sc, sh)


def _pool_kernel(h_ref, w_ref, wp_ref, ps_ref, st_ref, ya_ref, ns_ref, wbf, wpbf, carry,
                 *, nb, tm, tiles_per_seq, start):
    i = pl.program_id(0)
    t = i % tiles_per_seq
    c = POOL_WIDTH
    halo = POOL_HALO

    @pl.when(i == 0)
    def _():
        wbf[...] = w_ref[...].astype(BF16)
        wpbf[...] = wp_ref[...].astype(BF16)

    @pl.when(t == 0)
    def _():
        carry[...] = st_ref[...]

    p = jnp.dot(h_ref[...], wbf[...], preferred_element_type=F32)
    p3 = p.reshape(nb, tm, c)
    ext3 = jnp.concatenate([carry[...], p3], axis=1)
    tail = ext3[:, tm:tm + halo, :]
    ns_ref[...] = tail
    carry[...] = tail
    ext = ext3.reshape(nb * (halo + tm), c)
    pos = start + t * tm + lax.broadcasted_iota(I32, (1, tm, 1), 1)
    outs = []
    for gi, w in enumerate(POOL_WINDOWS):
        sl = slice(gi * POOL_GROUP, (gi + 1) * POOL_GROUP)
        acc = ext[:, sl]
        shift = 1
        while shift < w:
            acc = acc + pltpu.roll(acc, shift, 0)
            shift *= 2
        win = acc.reshape(nb, halo + tm, POOL_GROUP)[:, halo:, :]
        cnt = jnp.minimum(pos + 1, w).astype(F32)
        dgrp = win / cnt - p3[:, :, sl]
        outs.append(jnp.dot(dgrp.reshape(nb * tm, POOL_GROUP).astype(BF16), wpbf[gi],
                            preferred_element_type=F32))
    y = jnp.concatenate(outs, axis=-1) * ps_ref[...]
    ya_ref[...] = y.astype(BF16)


def _pool_mixer(h2, w_in, w_pool, pool_scale, state, nb, tm, start):
    rows, d = h2.shape
    nseq = state.shape[0]
    tiles_per_seq = (rows // nseq) // tm
    seq_blocks = nseq // nb
    c = POOL_WIDTH
    st = jnp.pad(state, ((0, 0), (POOL_HALO - POOL_STATE, 0), (0, 0)))
    kern = functools.partial(_pool_kernel, nb=nb, tm=tm, tiles_per_seq=tiles_per_seq, start=start)
    ya, ns = pl.pallas_call(
        kern,
        grid=(seq_blocks * tiles_per_seq,),
        in_specs=[pl.BlockSpec((nb * tm, d), lambda i: (i, 0)),
                  pl.BlockSpec((d, c), lambda i: (0, 0)),
                  pl.BlockSpec((len(POOL_WINDOWS), POOL_GROUP, POOL_GROUP), lambda i: (0, 0, 0)),
                  pl.BlockSpec((1, c), lambda i: (0, 0)),
                  pl.BlockSpec((nb, POOL_HALO, c), lambda i: (i // tiles_per_seq, 0, 0))],
        out_specs=[pl.BlockSpec((nb * tm, c), lambda i: (i, 0)),
                   pl.BlockSpec((nb, POOL_HALO, c), lambda i: (i // tiles_per_seq, 0, 0))],
        out_shape=[jax.ShapeDtypeStruct((rows, c), BF16),
                   jax.ShapeDtypeStruct((nseq, POOL_HALO, c), F32)],
        scratch_shapes=[pltpu.VMEM((d, c), BF16),
                        pltpu.VMEM((len(POOL_WINDOWS), POOL_GROUP, POOL_GROUP), BF16),
                        pltpu.VMEM((nb, POOL_HALO, c), F32)],
        compiler_params=_cparams(1),
        name="pool_mixer",
    )(h2, w_in, w_pool, pool_scale.reshape(1, c), st)
    return ya, ns[:, POOL_HALO - POOL_STATE:, :]


def _conv_kernel(h_ref, wx_ref, wb_ref, wc_ref, cw_ref, st_ref, yb_ref, ns_ref,
                 wxbf, wbbf, wcbf, carry, *, nb, tm, tiles_per_seq):
    i = pl.program_id(1)
    t = i % tiles_per_seq
    tc = wxbf.shape[1]
    halo = CONV_HALO

    @pl.when(i == 0)
    def _():
        wxbf[...] = wx_ref[...].astype(BF16)
        wbbf[...] = wb_ref[...].astype(BF16)
        wcbf[...] = wc_ref[...].astype(BF16)

    @pl.when(t == 0)
    def _():
        carry[...] = st_ref[...]

    h = h_ref[...]
    xin = jnp.dot(h, wxbf[...], preferred_element_type=F32)
    gb = jnp.dot(h, wbbf[...], preferred_element_type=F32)
    gc = jnp.dot(h, wcbf[...], preferred_element_type=F32)
    z3 = (gc * xin).reshape(nb, tm, tc)
    ext3 = jnp.concatenate([carry[...], z3], axis=1)
    tail = ext3[:, tm:tm + halo, :]
    ns_ref[...] = tail
    carry[...] = tail
    ext = ext3.reshape(nb * (halo + tm), tc)
    cw = cw_ref[...]
    conv = cw[0:1, :] * pltpu.roll(ext, 2, 0) + cw[1:2, :] * pltpu.roll(ext, 1, 0) + cw[2:3, :] * ext
    conv = conv.reshape(nb, halo + tm, tc)[:, halo:, :].reshape(nb * tm, tc)
    yb_ref[...] = (gb * conv).astype(BF16)


def _conv_mixer(h2, w_in, conv_w, state, nb, tm):
    rows, d = h2.shape
    nseq = state.shape[0]
    tiles_per_seq = (rows // nseq) // tm
    seq_blocks = nseq // nb
    c = CONV_WIDTH
    tc = 512
    cb = c // tc
    base = POOL_WIDTH // tc
    st = jnp.pad(state, ((0, 0), (CONV_HALO - (CONV_K - 1), 0), (0, 0)))
    kern = functools.partial(_conv_kernel, nb=nb, tm=tm, tiles_per_seq=tiles_per_seq)
    yb, ns = pl.pallas_call(
        kern,
        grid=(cb, seq_blocks * tiles_per_seq),
        in_specs=[pl.BlockSpec((nb * tm, d), lambda j, i: (i, 0)),
                  pl.BlockSpec((d, tc), lambda j, i: (0, base + j)),
                  pl.BlockSpec((d, tc), lambda j, i: (0, base + cb + j)),
                  pl.BlockSpec((d, tc), lambda j, i: (0, base + 2 * cb + j)),
                  pl.BlockSpec((CONV_K, tc), lambda j, i: (0, j)),
                  pl.BlockSpec((nb, CONV_HALO, tc), lambda j, i: (i // tiles_per_seq, 0, j))],
        out_specs=[pl.BlockSpec((nb * tm, tc), lambda j, i: (i, j)),
                   pl.BlockSpec((nb, CONV_HALO, tc), lambda j, i: (i // tiles_per_seq, 0, j))],
        out_shape=[jax.ShapeDtypeStruct((rows, c), BF16),
                   jax.ShapeDtypeStruct((nseq, CONV_HALO, c), F32)],
        scratch_shapes=[pltpu.VMEM((d, tc), BF16)] * 3 + [pltpu.VMEM((nb, CONV_HALO, tc), F32)],
        compiler_params=_cparams(2),
        name="conv_mixer",
    )(h2, w_in, w_in, w_in, conv_w, st)
    return yb, ns[:, CONV_HALO - (CONV_K - 1):, :]


def _pack_bf16_pairs(v):
    c = v.shape[1] // 2
    r = v.astype(BF16).astype(F32)
    lo = pltpu.bitcast(r[:, :c], U32)
    hi = pltpu.bitcast(r[:, c:], U32)
    return (hi & jnp.uint32(0xFFFF0000)) | (lo >> 16)


def _unpack_bf16_pairs(w):
    lo = pltpu.bitcast(w << 16, F32).astype(BF16)
    hi = pltpu.bitcast(w & jnp.uint32(0xFFFF0000), F32).astype(BF16)
    return lo, hi


def _route(log_t, bias_col):
    s = jax.nn.sigmoid(log_t)
    sg = s + bias_col
    sr = [s[e:e + 1, :] for e in range(N_EXPERTS)]
    gr = [sg[e:e + 1, :] for e in range(N_EXPERTS)]
    top2 = []
    for e in range(N_EXPERTS):
        g0 = (e // EXP_PER_GROUP) * EXP_PER_GROUP
        rank = jnp.zeros_like(gr[e], dtype=I32)
        for j in range(g0, g0 + EXP_PER_GROUP):
            if j == e:
                continue
            ahead = (gr[j] >= gr[e]) if j < e else (gr[j] > gr[e])
            rank = rank + ahead.astype(I32)
        top2.append(rank < TOP_K)
    gscore = []
    for g in range(N_EXPERT_GROUPS):
        acc = jnp.zeros_like(gr[0])
        for e in range(g * EXP_PER_GROUP, (g + 1) * EXP_PER_GROUP):
            acc = acc + jnp.where(top2[e], gr[e], 0.0)
        gscore.append(acc)
    best = gscore[0]
    gsel = jnp.zeros_like(best, dtype=I32)
    for g in range(1, N_EXPERT_GROUPS):
        upd = gscore[g] > best
        gsel = jnp.where(upd, g, gsel)
        best = jnp.where(upd, gscore[g], best)
    sel = [top2[e] & (gsel == e // EXP_PER_GROUP) for e in range(N_EXPERTS)]
    wsum = jnp.zeros_like(best)
    for e in range(N_EXPERTS):
        wsum = wsum + jnp.where(sel[e], sr[e], 0.0)
    return sel, sr, wsum


def _outproj_kernel(ya_ref, yb_ref, x_ref, g1_ref, sc_ref, sh_ref, ng_ref, wo_ref, wr_ref, br_ref,
                    x1_ref, hp_ref, ri_ref, rw_ref, cnt_ref, wobf, cnt_acc):
    i = pl.program_id(0)
    rows = x_ref.shape[0]
    half = ya_ref.shape[1]

    @pl.when(i == 0)
    def _():
        wobf[...] = wo_ref[...].astype(BF16)
        cnt_acc[...] = jnp.zeros_like(cnt_acc)

    y = (jnp.dot(ya_ref[...], wobf[:half, :], preferred_element_type=F32)
         + jnp.dot(yb_ref[...], wobf[half:, :], preferred_element_type=F32))
    x1 = x_ref[...] + g1_ref[0] * y
    x1_ref[...] = x1
    h2 = _rms(x1, ng_ref[...]) * (1.0 + sc_ref[0]) + sh_ref[0]
    hp_ref[...] = _pack_bf16_pairs(h2)

    log_t = lax.dot_general(wr_ref[...], h2, (((1,), (1,)), ((), ())),
                            precision=lax.Precision.HIGHEST, preferred_element_type=F32)
    sel, sr, wsum = _route(log_t, br_ref[...])

    selm = jnp.concatenate([m.astype(F32) for m in sel], axis=0)
    src = lax.broadcasted_iota(I32, (rows, rows), 0)
    dst = lax.broadcasted_iota(I32, (rows, rows), 1)
    before = (src < dst).astype(BF16)
    ranks = jnp.dot(selm.astype(BF16), before, preferred_element_type=F32)
    slot = (cnt_acc[...][:, 0:1] + ranks).astype(I32)
    cnt_new = cnt_acc[...] + jnp.sum(selm, axis=1, keepdims=True)
    cnt_acc[...] = cnt_new
    cnt_ref[...] = cnt_new

    zi = jnp.zeros((1, rows), I32)
    zf = jnp.zeros((1, rows), F32)
    found = jnp.zeros((1, rows), jnp.bool_)
    e_a, e_b, r_a, r_b, w_a, w_b = zi, zi, zi, zi, zf, zf
    for e in range(N_EXPERTS):
        is_a = sel[e] & jnp.logical_not(found)
        is_b = sel[e] & found
        gate = sr[e] / wsum
        e_a = jnp.where(is_a, e, e_a)
        e_b = jnp.where(is_b, e, e_b)
        r_a = jnp.where(is_a, slot[e:e + 1, :], r_a)
        r_b = jnp.where(is_b, slot[e:e + 1, :], r_b)
        w_a = jnp.where(is_a, gate, w_a)
        w_b = jnp.where(is_b, gate, w_b)
        found = found | sel[e]
    ri_ref[0] = jnp.concatenate([e_a, e_b, r_a, r_b, zi, zi, zi, zi], axis=0)
    wmat = jnp.concatenate([w_a, w_b, jnp.zeros((V7X_LANES - 2, rows), F32)], axis=0)
    rw_ref[...] = wmat.T


def _outproj_router(ya, yb, x2, g1, sc, sh, ng, w_out, w_router, b_router, tiles_per_mod):
    rows_all, d = x2.shape
    half = ya.shape[1]
    nt = rows_all // ROW_TILE
    row_spec = lambda w: pl.BlockSpec((ROW_TILE, w), lambda i: (i, 0))
    const = lambda shape: pl.BlockSpec(shape, lambda i: (0,) * len(shape))
    return pl.pallas_call(
        _outproj_kernel,
        grid=(nt,),
        in_specs=[row_spec(half), row_spec(half), row_spec(d),
                  _mod_spec(g1, tiles_per_mod), _mod_spec(sc, tiles_per_mod), _mod_spec(sh, tiles_per_mod),
                  const((1, d)), const((d, d)), const((N_EXPERTS, d)), const((N_EXPERTS, 1))],
        out_specs=[row_spec(d), row_spec(d // 2),
                   pl.BlockSpec((1, V7X_SUBLANES, ROW_TILE), lambda i: (i, 0, 0)),
                   row_spec(V7X_LANES),
                   const((N_EXPERTS, V7X_LANES))],
        out_shape=[jax.ShapeDtypeStruct((rows_all, d), F32),
                   jax.ShapeDtypeStruct((rows_all, d // 2), U32),
                   jax.ShapeDtypeStruct((nt, V7X_SUBLANES, ROW_TILE), I32),
                   jax.ShapeDtypeStruct((rows_all, V7X_LANES), F32),
                   jax.ShapeDtypeStruct((N_EXPERTS, V7X_LANES), F32)],
        scratch_shapes=[pltpu.VMEM((d, d), BF16), pltpu.VMEM((N_EXPERTS, V7X_LANES), F32)],
        compiler_params=_cparams(1),
        name="outproj_router",
    )(ya, yb, x2, g1, sc, sh, ng.reshape(1, d), w_out, w_router.T, b_router.reshape(N_EXPERTS, 1))


def _dispatch_kernel(pos_ref, hp_ref, xs_in_ref, xs_ref, sem, *, n_tok):
    del xs_in_ref
    i = pl.program_id(0)
    rows = hp_ref.shape[0]
    base = i * rows

    def row_copy(r, dst):
        return pltpu.make_async_copy(hp_ref.at[pl.ds(r, 1), :], xs_ref.at[pl.ds(dst, 1), :], sem)

    def issue(r, carry):
        row_copy(r, pos_ref[base + r]).start()
        row_copy(r, pos_ref[n_tok + base + r]).start()
        return carry

    lax.fori_loop(0, rows, issue, 0)

    def drain(r, carry):
        row_copy(0, 0).wait()
        row_copy(0, 0).wait()
        return carry

    lax.fori_loop(0, rows, drain, 0)


def _dispatch(pos, hp, n_rows_sorted):
    n_tok, c = hp.shape
    xs0 = jnp.zeros((n_rows_sorted, c), U32)
    kern = functools.partial(_dispatch_kernel, n_tok=n_tok)
    return pl.pallas_call(
        kern,
        grid_spec=pltpu.PrefetchScalarGridSpec(
            num_scalar_prefetch=1,
            grid=(n_tok // ROW_TILE,),
            in_specs=[pl.BlockSpec((ROW_TILE, c), lambda i, p: (i, 0)),
                      pl.BlockSpec(memory_space=pl.ANY)],
            out_specs=pl.BlockSpec(memory_space=pl.ANY),
            scratch_shapes=[pltpu.SemaphoreType.DMA(())]),
        out_shape=jax.ShapeDtypeStruct((n_rows_sorted, c), U32),
        input_output_aliases={2: 0},
        compiler_params=_cparams(1),
        name="moe_dispatch",
    )(pos, hp, xs0)


def _experts_kernel(te_ref, tv_ref, xs_ref, wg_ref, wu_ref, wd_ref, ys_ref, wgbf, wubf, wdbf):
    i = pl.program_id(0)
    prev = te_ref[jnp.maximum(i - 1, 0)]
    changed = (i == 0) | (te_ref[i] != prev)
    half = xs_ref.shape[1]

    @pl.when(changed)
    def _():
        wgbf[...] = wg_ref[0].astype(BF16)
        wubf[...] = wu_ref[0].astype(BF16)
        wdbf[...] = wd_ref[0].astype(BF16)

    @pl.when(tv_ref[i] > 0)
    def _():
        lo, hi = _unpack_bf16_pairs(xs_ref[...])
        a = (jnp.dot(lo, wgbf[:half, :], preferred_element_type=F32)
             + jnp.dot(hi, wgbf[half:, :], preferred_element_type=F32))
        b = (jnp.dot(lo, wubf[:half, :], preferred_element_type=F32)
             + jnp.dot(hi, wubf[half:, :], preferred_element_type=F32))
        hid = (a * jax.nn.sigmoid(a)) * b
        ys_ref[...] = jnp.dot(hid.astype(BF16), wdbf[...], preferred_element_type=F32)

    @pl.when(tv_ref[i] == 0)
    def _():
        ys_ref[...] = jnp.zeros_like(ys_ref)


def _experts(tile_expert, tile_valid, xs, w_gate, w_up, w_down):
    n_rows, half = xs.shape
    _, d, f = w_gate.shape
    nt = n_rows // MOE_TILE
    return pl.pallas_call(
        _experts_kernel,
        grid_spec=pltpu.PrefetchScalarGridSpec(
            num_scalar_prefetch=2,
            grid=(nt,),
            in_specs=[pl.BlockSpec((MOE_TILE, half), lambda i, te, tv: (i, 0)),
                      pl.BlockSpec((1, d, f), lambda i, te, tv: (te[i], 0, 0)),
                      pl.BlockSpec((1, d, f), lambda i, te, tv: (te[i], 0, 0)),
                      pl.BlockSpec((1, f, d), lambda i, te, tv: (te[i], 0, 0))],
            out_specs=pl.BlockSpec((MOE_TILE, d), lambda i, te, tv: (i, 0)),
            scratch_shapes=[pltpu.VMEM((d, f), BF16), pltpu.VMEM((d, f), BF16), pltpu.VMEM((f, d), BF16)]),
        out_shape=jax.ShapeDtypeStruct((n_rows, d), F32),
        compiler_params=_cparams(1),
        name="moe_experts",
    )(tile_expert, tile_valid, xs, w_gate, w_up, w_down)


def _combine_kernel(pos_ref, ys_ref, x1_ref, rw_ref, g2_ref, ng_ref, sc_ref, sh_ref, *rest, n_tok, final):
    if final:
        x2_ref, buf_a, buf_b, sem = rest
        hn_ref = None
    else:
        x2_ref, hn_ref, buf_a, buf_b, sem = rest
    i = pl.program_id(0)
    rows = x1_ref.shape[0]
    base = i * rows

    def row_copy(src, buf, r):
        return pltpu.make_async_copy(ys_ref.at[pl.ds(src, 1), :], buf.at[pl.ds(r, 1), :], sem)

    def issue(r, carry):
        row_copy(pos_ref[base + r], buf_a, r).start()
        row_copy(pos_ref[n_tok + base + r], buf_b, r).start()
        return carry

    lax.fori_loop(0, rows, issue, 0)

    def drain(r, carry):
        row_copy(0, buf_a, 0).wait()
        row_copy(0, buf_b, 0).wait()
        return carry

    lax.fori_loop(0, rows, drain, 0)

    rw = rw_ref[...]
    moe = rw[:, 0:1] * buf_a[...] + rw[:, 1:2] * buf_b[...]
    x2 = x1_ref[...] + g2_ref[0] * moe
    if final:
        x2_ref[...] = _rms(x2, ng_ref[...])
    else:
        x2_ref[...] = x2
        hn_ref[...] = (_rms(x2, ng_ref[...]) * (1.0 + sc_ref[0]) + sh_ref[0]).astype(BF16)


def _combine(pos, ys, x1, rw, g2, ng, sc, sh, tiles_per_mod, final):
    n_tok, d = x1.shape
    kern = functools.partial(_combine_kernel, n_tok=n_tok, final=final)
    row_spec = lambda w: pl.BlockSpec((ROW_TILE, w), lambda i, p: (i, 0))
    mod_spec = lambda m: pl.BlockSpec((1, m.shape[1], d), lambda i, p: (i // tiles_per_mod, 0, 0))
    out_shape = [jax.ShapeDtypeStruct((n_tok, d), F32)]
    out_specs = [row_spec(d)]
    if not final:
        out_shape.append(jax.ShapeDtypeStruct((n_tok, d), BF16))
        out_specs.append(row_spec(d))
    return pl.pallas_call(
        kern,
        grid_spec=pltpu.PrefetchScalarGridSpec(
            num_scalar_prefetch=1,
            grid=(n_tok // ROW_TILE,),
            in_specs=[pl.BlockSpec(memory_space=pl.ANY), row_spec(d), row_spec(V7X_LANES),
                      mod_spec(g2), pl.BlockSpec((1, d), lambda i, p: (0, 0)), mod_spec(sc), mod_spec(sh)],
            out_specs=out_specs,
            scratch_shapes=[pltpu.VMEM((ROW_TILE, d), F32), pltpu.VMEM((ROW_TILE, d), F32),
                            pltpu.SemaphoreType.DMA(())]),
        out_shape=out_shape,
        compiler_params=_cparams(1),
        name="moe_combine_final" if final else "moe_combine",
    )(pos, ys, x1, rw, g2, ng.reshape(1, d), sc, sh)


def _gmlp_kernel(h_ref, w_ref, lg_ref, lb_ref, ws_ref, bs_ref, yc_ref, gv_ref, wbf, wsbf, *, blk):
    i = pl.program_id(0)
    rows = h_ref.shape[0]
    c = GM_WIDTH
    ell = ws_ref.shape[1]

    @pl.when(i == 0)
    def _():
        wbf[...] = w_ref[...].astype(BF16)
        r = lax.broadcasted_iota(I32, (ell, ell), 0)
        s = lax.broadcasted_iota(I32, (ell, ell), 1)
        keep = (r >= s) & ((r // blk) == (s // blk))
        for g in range(GM_GROUPS):
            wsbf[g] = jnp.where(keep, ws_ref[g], 0.0).astype(BF16)

    uv = jnp.dot(h_ref[...], wbf[...], preferred_element_type=F32)
    u = uv[:, :c]
    v = uv[:, c:]
    vc = v - jnp.mean(v, axis=-1, keepdims=True)
    vn = vc * lax.rsqrt(jnp.mean(vc * vc, axis=-1, keepdims=True) + EPS) * lg_ref[...] + lb_ref[...]
    gv_ref[...] = vn
    vb = vn.astype(BF16)
    bs = bs_ref[...]
    for ch in range(rows // ell):
        rs = slice(ch * ell, (ch + 1) * ell)
        outs = []
        for g in range(GM_GROUPS):
            cs = slice(g * GM_GROUP, (g + 1) * GM_GROUP)
            mixed = jnp.dot(wsbf[g], vb[rs, cs], preferred_element_type=F32) + bs[:, g:g + 1]
            outs.append(u[rs, cs] * mixed)
        yc_ref[rs, :] = jnp.concatenate(outs, axis=-1).astype(BF16)


def _gmlp_mixer(h2, w_in, ln_g, ln_b, ws, bs_t, blk):
    rows, d = h2.shape
    c = GM_WIDTH
    ell = ws.shape[1]
    kern = functools.partial(_gmlp_kernel, blk=blk)
    return pl.pallas_call(
        kern,
        grid=(rows // ROW_TILE,),
        in_specs=[pl.BlockSpec((ROW_TILE, d), lambda i: (i, 0)),
                  pl.BlockSpec((d, 2 * c), lambda i: (0, 0)),
                  pl.BlockSpec((1, c), lambda i: (0, 0)),
                  pl.BlockSpec((1, c), lambda i: (0, 0)),
                  pl.BlockSpec((GM_GROUPS, ell, ell), lambda i: (0, 0, 0)),
                  pl.BlockSpec((ell, GM_GROUPS), lambda i: (0, 0))],
        out_specs=[pl.BlockSpec((ROW_TILE, c), lambda i: (i, 0)),
                   pl.BlockSpec((ROW_TILE, c), lambda i: (i, 0))],
        out_shape=[jax.ShapeDtypeStruct((rows, c), BF16),
                   jax.ShapeDtypeStruct((rows, c), F32)],
        scratch_shapes=[pltpu.VMEM((d, 2 * c), BF16), pltpu.VMEM((GM_GROUPS, ell, ell), BF16)],
        compiler_params=_cparams(1),
        name="gmlp_mixer",
    )(h2, w_in, ln_g.reshape(1, c), ln_b.reshape(1, c), ws, bs_t)


def _swa_kernel(h_ref, w_ref, kp_ref, vp_ref, bias_ref, sink_ref, yd_ref, k_ref, v_ref,
                wbf, kprev, vprev, *, tq, blocks_per_seq, has_cache):
    i = pl.program_id(0)
    rows = h_ref.shape[0]
    nq = N_HEADS * HEAD_DIM
    nkv = N_KV * HEAD_DIM
    n_blocks = rows // tq

    @pl.when(i == 0)
    def _():
        wbf[...] = w_ref[...].astype(BF16)
        kprev[...] = jnp.zeros_like(kprev)
        vprev[...] = jnp.zeros_like(vprev)

    qkv = jnp.dot(h_ref[...], wbf[...], preferred_element_type=F32)
    k_new = qkv[:, nq:nq + nkv]
    v_new = qkv[:, nq + nkv:]
    k_ref[...] = k_new
    v_ref[...] = v_new
    lane = lax.broadcasted_iota(I32, (2 * WINDOW, nkv), 1)
    lo_half = lane < HEAD_DIM
    col = lax.broadcasted_iota(I32, (tq, 2 * WINDOW), 1)
    for blk in range(n_blocks):
        rs = slice(blk * tq, (blk + 1) * tq)
        if has_cache:
            k_old = kp_ref[blk]
            v_old = vp_ref[blk]
            first = None
        else:
            seq_blk = (i * n_blocks + blk) % blocks_per_seq
            first = seq_blk == 0
            k_old = kprev[...]
            v_old = vprev[...]
        if tq < WINDOW:
            pad = jnp.zeros((WINDOW - tq, nkv), F32)
            k_cur = jnp.concatenate([k_new[rs], pad], axis=0)
            v_cur = jnp.concatenate([v_new[rs], pad], axis=0)
        else:
            k_cur = k_new[rs]
            v_cur = v_new[rs]
        if not has_cache:
            kprev[...] = k_cur
            vprev[...] = v_cur
        kcat = jnp.concatenate([k_old, k_cur], axis=0)
        vcat = jnp.concatenate([v_old, v_cur], axis=0)
        kswap = pltpu.roll(kcat, HEAD_DIM, 1)
        vswap = pltpu.roll(vcat, HEAD_DIM, 1)
        outs = []
        for pair in range(N_HEADS // 2):
            hk = (2 * pair) // (N_HEADS // N_KV)
            k_lo, k_hi = (kcat, kswap) if hk == 0 else (kswap, kcat)
            v_lo, v_hi = (vcat, vswap) if hk == 0 else (vswap, vcat)
            kbd = jnp.concatenate([jnp.where(lo_half, k_lo, 0.0), jnp.where(lo_half, 0.0, k_hi)],
                                  axis=0).astype(BF16)
            vbd = jnp.concatenate([jnp.where(lo_half, v_lo, 0.0), jnp.where(lo_half, 0.0, v_hi)],
                                  axis=0).astype(BF16)
            qp = qkv[rs, pair * 2 * HEAD_DIM:(pair + 1) * 2 * HEAD_DIM].astype(BF16)
            s2 = lax.dot_general(qp, kbd, (((1,), (1,)), ((), ())),
                                 preferred_element_type=F32) * (HEAD_DIM ** -0.5)
            probs = []
            for sub in range(2):
                head = 2 * pair + sub
                s = s2[:, sub * 2 * WINDOW:(sub + 1) * 2 * WINDOW] + bias_ref[head, :tq, :]
                if first is not None:
                    s = jnp.where(first & (col < WINDOW), NEG_INF, s)
                sink = sink_ref[head:head + 1, 0:1]
                m = jnp.maximum(jnp.max(s, axis=-1, keepdims=True), sink)
                pr = jnp.exp(s - m)
                pr = pr / (jnp.sum(pr, axis=-1, keepdims=True) + jnp.exp(sink - m))
                probs.append(pr.astype(BF16))
            outs.append(jnp.dot(jnp.concatenate(probs, axis=-1), vbd, preferred_element_type=F32))
        yd_ref[rs, :] = jnp.concatenate(outs, axis=-1).astype(BF16)


def _swa_mixer(h2, w_in, k_cache, v_cache, bias, sinks, tq, blocks_per_seq):
    rows, d = h2.shape
    nq = N_HEADS * HEAD_DIM
    nkv = N_KV * HEAD_DIM
    nw = nq + 2 * nkv
    has_cache = k_cache is not None
    n_blocks = ROW_TILE // tq
    if not has_cache:
        k_cache = jnp.zeros((n_blocks, WINDOW, nkv), F32)
        v_cache = k_cache
    kern = functools.partial(_swa_kernel, tq=tq, blocks_per_seq=blocks_per_seq, has_cache=has_cache)
    cache_spec = pl.BlockSpec((n_blocks, WINDOW, nkv), lambda i: (i if has_cache else 0, 0, 0))
    return pl.pallas_call(
        kern,
        grid=(rows // ROW_TILE,),
        in_specs=[pl.BlockSpec((ROW_TILE, d), lambda i: (i, 0)),
                  pl.BlockSpec((d, nw), lambda i: (0, 0)),
                  cache_spec, cache_spec,
                  pl.BlockSpec((N_HEADS, WINDOW, 2 * WINDOW), lambda i: (0, 0, 0)),
                  pl.BlockSpec((N_HEADS, V7X_LANES), lambda i: (0, 0))],
        out_specs=[pl.BlockSpec((ROW_TILE, nq), lambda i: (i, 0)),
                   pl.BlockSpec((ROW_TILE, nkv), lambda i: (i, 0)),
                   pl.BlockSpec((ROW_TILE, nkv), lambda i: (i, 0))],
        out_shape=[jax.ShapeDtypeStruct((rows, nq), BF16),
                   jax.ShapeDtypeStruct((rows, nkv), F32),
                   jax.ShapeDtypeStruct((rows, nkv), F32)],
        scratch_shapes=[pltpu.VMEM((d, nw), BF16), pltpu.VMEM((WINDOW, nkv), F32),
                        pltpu.VMEM((WINDOW, nkv), F32)],
        compiler_params=_cparams(1),
        name="swa_mixer",
    )(h2, w_in, k_cache, v_cache, bias, sinks)


def _t5_bucket(dist):
    max_exact = N_BUCKETS // 2
    dd = np.maximum(dist, 1)
    large = max_exact + (np.log(dd / max_exact) / np.log(WINDOW / max_exact)
                         * (N_BUCKETS - max_exact)).astype(np.int64)
    large = np.minimum(large, N_BUCKETS - 1)
    return np.where(dist < max_exact, dist, large).astype(np.int32)


def _attention_bias(rel_bias):
    dist = WINDOW + np.arange(WINDOW)[:, None] - np.arange(2 * WINDOW)[None, :]
    bucket = _t5_bucket(np.clip(dist, 0, WINDOW - 1))
    valid = (dist >= 0) & (dist < WINDOW)
    bias = jnp.transpose(jnp.take(rel_bias.astype(F32), bucket, axis=0), (2, 0, 1))
    return jnp.where(valid[None], bias, NEG_INF)


def _routing_plan(ri_list, cnt_list, n_rows_sorted):
    counts = [c[:, 0].astype(I32) for c in cnt_list]
    total = sum(counts)
    padded = ((total + MOE_TILE - 1) // MOE_TILE) * MOE_TILE
    starts = jnp.cumsum(padded) - padded
    pos_a, pos_b = [], []
    offset = jnp.zeros_like(total)
    for ri, cnt in zip(ri_list, counts):
        e_a, e_b, r_a, r_b = (ri[:, k, :].reshape(-1) for k in range(4))
        pos_a.append(starts[e_a] + offset[e_a] + r_a)
        pos_b.append(starts[e_b] + offset[e_b] + r_b)
        offset = offset + cnt
    pos = jnp.concatenate(pos_a + pos_b).astype(I32)
    nt = n_rows_sorted // MOE_TILE
    tile_start = jnp.arange(nt, dtype=I32) * MOE_TILE
    ends = starts + padded
    tile_expert = jnp.sum((tile_start[:, None] >= ends[None, :]).astype(I32), axis=1)
    used = tile_expert < N_EXPERTS
    last_expert = jnp.max(jnp.where(padded > 0, jnp.arange(N_EXPERTS, dtype=I32), 0))
    tile_expert = jnp.where(used, tile_expert, last_expert).astype(I32)
    return pos, tile_expert, used.astype(I32)


def _moe(ri_list, cnt_list, hp_list, x1_list, rw_list, g2_list, ng, sc_list, sh_list, tpm_list,
         w_gate, w_up, w_down, final):
    n_tok = sum(h.shape[0] for h in hp_list)
    n_rows_sorted = ((TOP_K * n_tok + N_EXPERTS * (MOE_TILE - 1)) // MOE_TILE + 1) * MOE_TILE
    pos, tile_expert, tile_valid = _routing_plan(ri_list, cnt_list, n_rows_sorted)
    hp = jnp.concatenate(hp_list, axis=0)
    xs = _dispatch(pos, hp, n_rows_sorted)
    ys = _experts(tile_expert, tile_valid, xs, w_gate, w_up, w_down)
    outs = []
    off = 0
    for x1, rw, g2, sc, sh, tpm in zip(x1_list, rw_list, g2_list, sc_list, sh_list, tpm_list):
        n = x1.shape[0]
        pos_t = jnp.concatenate([pos[off:off + n], pos[n_tok + off:n_tok + off + n]])
        outs.append(_combine(pos_t, ys, x1, rw, g2, ng, sc, sh, tpm, final))
        off += n
    return outs


def kernel(x_prompt, x_sample, state_pool, state_conv, cache_swa_k, cache_swa_v, c_prompt, c_sample, w_ada, b_ada, norm_g, final_norm_g, w_in_even, w_out_even, w_pool, pool_scale, conv_w, w_in_odd, w_out_odd, gm_norm_g, gm_norm_b, gm_w_s, gm_b_s, attn_sinks, rel_bias, w_router, b_router, w_gate, w_up, w_down):
    d = D_MODEL
    bp, tp, _ = x_prompt.shape
    bs, ts, _ = x_sample.shape
    rows_s = bs * ts
    assert rows_s == ROW_TILE and tp % ROW_TILE == 0 and PAST_LEN % CHUNK == 0

    n_c = bp + bs
    c_pad = (-n_c) % V7X_SUBLANES
    c_all = jnp.concatenate([c_prompt, c_sample, jnp.zeros((c_pad, d), F32)], axis=0)
    mod = _adaln(c_all, w_ada, b_ada)

    def mods(layer):
        parts = jnp.split(mod[layer], 6, axis=-1)
        pm = [p[:bp].reshape(bp, 1, d) for p in parts]
        sm = [jnp.repeat(p[bp:bp + bs], ts, axis=0).reshape(1, rows_s, d) for p in parts]
        return pm, sm

    tpm_p = tp // ROW_TILE
    tpm_s = 1
    xp = x_prompt.reshape(bp * tp, d)
    xs_ = x_sample.reshape(rows_s, d)

    pm, sm = mods(0)
    hp0 = _norm_mod(xp, norm_g[0, 0], pm[1], pm[0], tpm_p)
    hs0 = _norm_mod(xs_, norm_g[0, 0], sm[1], sm[0], tpm_s)
    zero_pool = jnp.zeros((bp, POOL_STATE, POOL_WIDTH), F32)
    zero_conv = jnp.zeros((bp, CONV_K - 1, CONV_WIDTH), F32)
    ya_p, pool_p = _pool_mixer(hp0, w_in_even[0], w_pool[0], pool_scale[0], zero_pool, 1, ROW_TILE, 0)
    ya_s, pool_s = _pool_mixer(hs0, w_in_even[0], w_pool[0], pool_scale[0], state_pool[0], bs, ts, PAST_LEN)
    yb_p, conv_p = _conv_mixer(hp0, w_in_even[0], conv_w[0], zero_conv, 1, ROW_TILE)
    yb_s, conv_s = _conv_mixer(hs0, w_in_even[0], conv_w[0], state_conv[0], bs, ts)
    x1p, hpp, rip, rwp, cntp = _outproj_router(ya_p, yb_p, xp, pm[2], pm[4], pm[3], norm_g[0, 1],
                                               w_out_even[0], w_router, b_router, tpm_p)
    x1s, hps, ris, rws, cnts = _outproj_router(ya_s, yb_s, xs_, sm[2], sm[4], sm[3], norm_g[0, 1],
                                               w_out_even[0], w_router, b_router, tpm_s)
    pm1, sm1 = mods(1)
    (x2p, h1p), (x2s, h1s) = _moe([rip, ris], [cntp, cnts], [hpp, hps], [x1p, x1s], [rwp, rws],
                                  [pm[5], sm[5]], norm_g[1, 0], [pm1[1], sm1[1]], [pm1[0], sm1[0]],
                                  [tpm_p, tpm_s], w_gate[0], w_up[0], w_down[0], final=False)

    nuv = 2 * GM_WIDTH
    w_uv = w_in_odd[0][:, :nuv]
    w_qkv = w_in_odd[0][:, nuv:]
    bs_t = gm_b_s[0].T
    ws_s = jnp.tile(gm_w_s[0][:, :ts, :ts], (1, bs, bs))
    bs_s = jnp.tile(bs_t[:ts], (bs, 1))
    yc_p, _ = _gmlp_mixer(h1p, w_uv, gm_norm_g[0], gm_norm_b[0], gm_w_s[0], bs_t, CHUNK)
    yc_s, gv_s = _gmlp_mixer(h1s, w_uv, gm_norm_g[0], gm_norm_b[0], ws_s, bs_s, ts)
    bias = _attention_bias(rel_bias)
    sinks = jnp.broadcast_to(attn_sinks[0].reshape(N_HEADS, 1), (N_HEADS, V7X_LANES))
    nkv = N_KV * HEAD_DIM
    yd_p, k_p, v_p = _swa_mixer(h1p, w_qkv, None, None, bias, sinks, WINDOW, tp // WINDOW)
    yd_s, k_s, v_s = _swa_mixer(h1s, w_qkv, cache_swa_k[0].reshape(bs, WINDOW, nkv),
                                cache_swa_v[0].reshape(bs, WINDOW, nkv), bias, sinks, ts, 1)
    x1p, hpp, rip, rwp, cntp = _outproj_router(yc_p, yd_p, x2p, pm1[2], pm1[4], pm1[3], norm_g[1, 1],
                                               w_out_odd[0], w_router, b_router, tpm_p)
    x1s, hps, ris, rws, cnts = _outproj_router(yc_s, yd_s, x2s, sm1[2], sm1[4], sm1[3], norm_g[1, 1],
                                               w_out_odd[0], w_router, b_router, tpm_s)
    (yp,), (ys_out,) = _moe([rip, ris], [cntp, cnts], [hpp, hps], [x1p, x1s], [rwp, rws],
                            [pm1[5], sm1[5]], final_norm_g, [pm1[1], sm1[1]], [pm1[0], sm1[0]],
                            [tpm_p, tpm_s], w_gate[1], w_up[1], w_down[1], final=True)

    k_p4 = k_p.reshape(bp, tp, N_KV, HEAD_DIM)[:, -WINDOW:]
    v_p4 = v_p.reshape(bp, tp, N_KV, HEAD_DIM)[:, -WINDOW:]
    k_s4 = jnp.concatenate([cache_swa_k[0], k_s.reshape(bs, ts, N_KV, HEAD_DIM)], axis=1)[:, -WINDOW:]
    v_s4 = jnp.concatenate([cache_swa_v[0], v_s.reshape(bs, ts, N_KV, HEAD_DIM)], axis=1)[:, -WINDOW:]
    return (yp.reshape(bp, tp, d), ys_out.reshape(bs, ts, d),
            pool_p[None], pool_s[None], conv_p[None], conv_s[None],
            k_p4[None], k_s4[None], v_p4[None], v_s4[None],
            gv_s.reshape(bs, ts, GM_WIDTH)[None])
```

```python
import functools

import numpy as np
import jax
import jax.numpy as jnp
from jax import lax
from jax.experimental import pallas as pl
from jax.experimental.pallas import tpu as pltpu

F32 = jnp.float32
BF16 = jnp.bfloat16
I32 = jnp.int32
U32 = jnp.uint32

D_MODEL = 2048
POOL_WINDOWS = (2, 4, 8, 16)
POOL_WIDTH = 1024
POOL_GROUP = 256
POOL_STATE = 15
CONV_WIDTH = 1024
CONV_K = 3
GM_WIDTH = 1024
GM_GROUPS = 8
GM_GROUP = 128
CHUNK = 128
HEAD_DIM = 64
N_HEADS = 16
N_KV = 2
WINDOW = 128
N_BUCKETS = 32
N_EXPERTS = 16
N_EXPERT_GROUPS = 4
EXP_PER_GROUP = 4
TOP_K = 2
EPS = 1e-6
NEG_INF = -1e30
PAST_LEN = 16384

V7X_SUBLANES = 8
V7X_LANES = 128
VMEM_LIMIT = 56 * 1024 * 1024

ROW_TILE = 256
ROUTER_TILE = 1024
POOL_HALO = 16
CONV_HALO = 8
MOE_TILE = 256
TAB_EXPERT, TAB_VALID, TAB_LAST_TILE, TAB_NUSED = 0, 1, 2, 3


def _cparams(n_axes):
    return pltpu.CompilerParams(dimension_semantics=("arbitrary",) * n_axes,
                                vmem_limit_bytes=VMEM_LIMIT)


def _rms(x, g):
    return x * lax.rsqrt(jnp.mean(x * x, axis=-1, keepdims=True) + EPS) * g


def _mod_spec(mod, layer, part):
    nrow = ROW_TILE if mod.shape[1] == ROW_TILE else V7X_SUBLANES
    return pl.BlockSpec((1, nrow, D_MODEL), lambda *_: (layer, 0, part))


def _mod_rows(m_ref, seq):
    if m_ref.shape[1] == V7X_SUBLANES:
        return m_ref[0, pl.ds(seq, 1), :]
    return m_ref[0]


def _adaln_kernel(c_ref, w_ref, b_ref, o_ref):
    c = c_ref[...]
    a = (c * jax.nn.sigmoid(c)).astype(BF16)
    o_ref[0] = jnp.dot(a, w_ref[0].astype(BF16), preferred_element_type=F32) + b_ref[0]


def _adaln(c_all, w_ada, b_ada):
    depth, d, n6 = w_ada.shape
    m = c_all.shape[0]
    tn = 1024
    return pl.pallas_call(
        _adaln_kernel,
        grid=(depth, n6 // tn),
        in_specs=[pl.BlockSpec((m, d), lambda l, j: (0, 0)),
                  pl.BlockSpec((1, d, tn), lambda l, j: (l, 0, j)),
                  pl.BlockSpec((1, 1, tn), lambda l, j: (l, 0, j))],
        out_specs=pl.BlockSpec((1, m, tn), lambda l, j: (l, 0, j)),
        out_shape=jax.ShapeDtypeStruct((depth, m, n6), F32),
        compiler_params=_cparams(2),
        name="adaln",
    )(c_all, w_ada, b_ada.reshape(depth, 1, n6))


def _norm_mod_kernel(x_ref, g_ref, sc_ref, sh_ref, h_ref, *, tiles_per_seq):
    seq = pl.program_id(0) // tiles_per_seq
    h = _rms(x_ref[...], g_ref[...]) * (1.0 + _mod_rows(sc_ref, seq)) + _mod_rows(sh_ref, seq)
    h_ref[...] = h.astype(BF16)


def _norm_mod(x2, g, mod, layer, tiles_per_seq):
    rows, d = x2.shape
    return pl.pallas_call(
        functools.partial(_norm_mod_kernel, tiles_per_seq=tiles_per_seq),
        grid=(rows // ROW_TILE,),
        in_specs=[pl.BlockSpec((ROW_TILE, d), lambda i: (i, 0)),
                  pl.BlockSpec((1, d), lambda i: (0, 0)),
                  _mod_spec(mod, layer, 1), _mod_spec(mod, layer, 0)],
        out_specs=pl.BlockSpec((ROW_TILE, d), lambda i: (i, 0)),
        out_shape=jax.ShapeDtypeStruct((rows, d), BF16),
        compiler_params=_cparams(1),
        name="norm_mod",
    )(x2, g.reshape(1, d), mod, mod)


def _pool_kernel(h_ref, w_ref, wp_ref, ps_ref, st_ref, ya_ref, ns_ref, wbf, wpbf, carry,
                 *, nb, tm, tiles_per_seq, start):
    i = pl.program_id(0)
    t = i % tiles_per_seq
    c = POOL_WIDTH
    halo = POOL_HALO

    @pl.when(i == 0)
    def _():
        wbf[...] = w_ref[...].astype(BF16)
        wpbf[...] = wp_ref[...].astype(BF16)

    @pl.when(t == 0)
    def _():
        carry[...] = st_ref[...]

    p = jnp.dot(h_ref[...], wbf[...], preferred_element_type=F32)
    p3 = p.reshape(nb, tm, c)
    ext3 = jnp.concatenate([carry[...], p3], axis=1)
    tail = ext3[:, tm:tm + halo, :]
    ns_ref[...] = tail
    carry[...] = tail
    ext = ext3.reshape(nb * (halo + tm), c)
    pos = start + t * tm + lax.broadcasted_iota(I32, (1, tm, 1), 1)
    outs = []
    for gi, w in enumerate(POOL_WINDOWS):
        sl = slice(gi * POOL_GROUP, (gi + 1) * POOL_GROUP)
        acc = ext[:, sl]
        shift = 1
        while shift < w:
            acc = acc + pltpu.roll(acc, shift, 0)
            shift *= 2
        win = acc.reshape(nb, halo + tm, POOL_GROUP)[:, halo:, :]
        cnt = jnp.minimum(pos + 1, w).astype(F32)
        dgrp = win / cnt - p3[:, :, sl]
        outs.append(jnp.dot(dgrp.reshape(nb * tm, POOL_GROUP).astype(BF16), wpbf[gi],
                            preferred_element_type=F32))
    y = jnp.concatenate(outs, axis=-1) * ps_ref[...]
    ya_ref[...] = y.astype(BF16)


def _pool_mixer(h2, w_in, w_pool, pool_scale, state, nb, tm, start):
    rows, d = h2.shape
    nseq = state.shape[0]
    tiles_per_seq = (rows // nseq) // tm
    seq_blocks = nseq // nb
    c = POOL_WIDTH
    st = jnp.pad(state, ((0, 0), (POOL_HALO - POOL_STATE, 0), (0, 0)))
    kern = functools.partial(_pool_kernel, nb=nb, tm=tm, tiles_per_seq=tiles_per_seq, start=start)
    ya, ns = pl.pallas_call(
        kern,
        grid=(seq_blocks * tiles_per_seq,),
        in_specs=[pl.BlockSpec((nb * tm, d), lambda i: (i, 0)),
                  pl.BlockSpec((d, c), lambda i: (0, 0)),
                  pl.BlockSpec((len(POOL_WINDOWS), POOL_GROUP, POOL_GROUP), lambda i: (0, 0, 0)),
                  pl.BlockSpec((1, c), lambda i: (0, 0)),
                  pl.BlockSpec((nb, POOL_HALO, c), lambda i: (i // tiles_per_seq, 0, 0))],
        out_specs=[pl.BlockSpec((nb * tm, c), lambda i: (i, 0)),
                   pl.BlockSpec((nb, POOL_HALO, c), lambda i: (i // tiles_per_seq, 0, 0))],
        out_shape=[jax.ShapeDtypeStruct((rows, c), BF16),
                   jax.ShapeDtypeStruct((nseq, POOL_HALO, c), F32)],
        scratch_shapes=[pltpu.VMEM((d, c), BF16),
                        pltpu.VMEM((len(POOL_WINDOWS), POOL_GROUP, POOL_GROUP), BF16),
                        pltpu.VMEM((nb, POOL_HALO, c), F32)],
        compiler_params=_cparams(1),
        name="pool_mixer",
    )(h2, w_in, w_pool, pool_scale.reshape(1, c), st)
    return ya, ns[:, POOL_HALO - POOL_STATE:, :]


def _conv_kernel(h_ref, wx_ref, wb_ref, wc_ref, cw_ref, st_ref, yb_ref, ns_ref,
                 wxbf, wbbf, wcbf, carry, *, nb, tm, tiles_per_seq):
    i = pl.program_id(1)
    t = i % tiles_per_seq
    tc = wxbf.shape[1]
    halo = CONV_HALO

    @pl.when(i == 0)
    def _():
        wxbf[...] = wx_ref[...].astype(BF16)
        wbbf[...] = wb_ref[...].astype(BF16)
        wcbf[...] = wc_ref[...].astype(BF16)

    @pl.when(t == 0)
    def _():
        carry[...] = st_ref[...]

    h = h_ref[...]
    xin = jnp.dot(h, wxbf[...], preferred_element_type=F32)
    gb = jnp.dot(h, wbbf[...], preferred_element_type=F32)
    gc = jnp.dot(h, wcbf[...], preferred_element_type=F32)
    z3 = (gc * xin).reshape(nb, tm, tc)
    ext3 = jnp.concatenate([carry[...], z3], axis=1)
    tail = ext3[:, tm:tm + halo, :]
    ns_ref[...] = tail
    carry[...] = tail
    ext = ext3.reshape(nb * (halo + tm), tc)
    cw = cw_ref[...]
    conv = cw[0:1, :] * pltpu.roll(ext, 2, 0) + cw[1:2, :] * pltpu.roll(ext, 1, 0) + cw[2:3, :] * ext
    conv = conv.reshape(nb, halo + tm, tc)[:, halo:, :].reshape(nb * tm, tc)
    yb_ref[...] = (gb * conv).astype(BF16)


def _conv_mixer(h2, w_in, conv_w, state, nb, tm):
    rows, d = h2.shape
    nseq = state.shape[0]
    tiles_per_seq = (rows // nseq) // tm
    seq_blocks = nseq // nb
    c = CONV_WIDTH
    tc = 512
    cb = c // tc
    base = POOL_WIDTH // tc
    st = jnp.pad(state, ((0, 0), (CONV_HALO - (CONV_K - 1), 0), (0, 0)))
    kern = functools.partial(_conv_kernel, nb=nb, tm=tm, tiles_per_seq=tiles_per_seq)
    yb, ns = pl.pallas_call(
        kern,
        grid=(cb, seq_blocks * tiles_per_seq),
        in_specs=[pl.BlockSpec((nb * tm, d), lambda j, i: (i, 0)),
                  pl.BlockSpec((d, tc), lambda j, i: (0, base + j)),
                  pl.BlockSpec((d, tc), lambda j, i: (0, base + cb + j)),
                  pl.BlockSpec((d, tc), lambda j, i: (0, base + 2 * cb + j)),
                  pl.BlockSpec((CONV_K, tc), lambda j, i: (0, j)),
                  pl.BlockSpec((nb, CONV_HALO, tc), lambda j, i: (i // tiles_per_seq, 0, j))],
        out_specs=[pl.BlockSpec((nb * tm, tc), lambda j, i: (i, j)),
                   pl.BlockSpec((nb, CONV_HALO, tc), lambda j, i: (i // tiles_per_seq, 0, j))],
        out_shape=[jax.ShapeDtypeStruct((rows, c), BF16),
                   jax.ShapeDtypeStruct((nseq, CONV_HALO, c), F32)],
        scratch_shapes=[pltpu.VMEM((d, tc), BF16)] * 3 + [pltpu.VMEM((nb, CONV_HALO, tc), F32)],
        compiler_params=_cparams(2),
        name="conv_mixer",
    )(h2, w_in, w_in, w_in, conv_w, st)
    return yb, ns[:, CONV_HALO - (CONV_K - 1):, :]


def _pack_bf16_pairs(v):
    c = v.shape[1] // 2
    r = v.astype(BF16).astype(F32)
    lo = pltpu.bitcast(r[:, :c], U32)
    hi = pltpu.bitcast(r[:, c:], U32)
    return (hi & jnp.uint32(0xFFFF0000)) | (lo >> 16)


def _unpack_bf16_pairs(w):
    lo = pltpu.bitcast(w << 16, F32).astype(BF16)
    hi = pltpu.bitcast(w & jnp.uint32(0xFFFF0000), F32).astype(BF16)
    return lo, hi


def _outproj_kernel(ya_ref, yb_ref, x_ref, g1_ref, sc_ref, sh_ref, ng_ref, wo_ref,
                    x1_ref, hp_ref, wobf, *, tiles_per_seq):
    i = pl.program_id(0)
    seq = i // tiles_per_seq
    half = ya_ref.shape[1]

    @pl.when(i == 0)
    def _():
        wobf[...] = wo_ref[...].astype(BF16)

    y = (jnp.dot(ya_ref[...], wobf[:half, :], preferred_element_type=F32)
         + jnp.dot(yb_ref[...], wobf[half:, :], preferred_element_type=F32))
    x1 = x_ref[...] + _mod_rows(g1_ref, seq) * y
    x1_ref[...] = x1
    h2 = _rms(x1, ng_ref[...]) * (1.0 + _mod_rows(sc_ref, seq)) + _mod_rows(sh_ref, seq)
    hp_ref[...] = _pack_bf16_pairs(h2)


def _outproj(ya, yb, x2, mod, layer, ng, w_out, tiles_per_seq):
    rows_all, d = x2.shape
    half = ya.shape[1]
    row_spec = lambda w: pl.BlockSpec((ROW_TILE, w), lambda i: (i, 0))
    return pl.pallas_call(
        functools.partial(_outproj_kernel, tiles_per_seq=tiles_per_seq),
        grid=(rows_all // ROW_TILE,),
        in_specs=[row_spec(half), row_spec(half), row_spec(d),
                  _mod_spec(mod, layer, 2), _mod_spec(mod, layer, 4), _mod_spec(mod, layer, 3),
                  pl.BlockSpec((1, d), lambda i: (0, 0)), pl.BlockSpec((d, d), lambda i: (0, 0))],
        out_specs=[row_spec(d), row_spec(d // 2)],
        out_shape=[jax.ShapeDtypeStruct((rows_all, d), F32),
                   jax.ShapeDtypeStruct((rows_all, d // 2), U32)],
        scratch_shapes=[pltpu.VMEM((d, d), BF16)],
        compiler_params=_cparams(1),
        name="outproj",
    )(ya, yb, x2, mod, mod, mod, ng.reshape(1, d), w_out)


def _router_kernel(hpp_ref, hps_ref, wr_ref, br_ref, pos_ref, rw_ref, tab_ref,
                   cnt_acc, totals, starts, padded, *, nt_p, rows_s):
    ph = pl.program_id(0)
    t = pl.program_id(1)
    last = nt_p
    r = hpp_ref.shape[0]
    half = hpp_ref.shape[1]
    ne = N_EXPERTS
    sub = lax.broadcasted_iota(I32, (ne, V7X_LANES), 0)

    @pl.when(t == 0)
    def _():
        cnt_acc[...] = jnp.zeros_like(cnt_acc)

    @pl.when((ph == 0) & (t == 0))
    def _():
        starts[...] = jnp.zeros_like(starts)
        padded[...] = jnp.zeros_like(padded)

    @pl.when((ph == 1) & (t == 0))
    def _():
        pad = jnp.floor((totals[...] + (MOE_TILE - 1.0)) * (1.0 / MOE_TILE)) * MOE_TILE
        run = pad
        k = 1
        while k < ne:
            run = run + jnp.where(sub >= k, pltpu.roll(run, k, 0), 0.0)
            k *= 2
        padded[...] = pad
        starts[...] = run - pad

    is_s = t == last
    w_s = jnp.concatenate([hps_ref[...], jnp.zeros((r - rows_s, half), U32)], axis=0)
    w = jnp.where(is_s, w_s, hpp_ref[...])
    lo, hi = _unpack_bf16_pairs(w)
    wr = wr_ref[...].astype(BF16)
    nt_dims = (((1,), (1,)), ((), ()))
    log_t = (lax.dot_general(wr[:, :half], lo, nt_dims, preferred_element_type=F32)
             + lax.dot_general(wr[:, half:], hi, nt_dims, preferred_element_type=F32))

    s = jax.nn.sigmoid(log_t)
    sg = s + br_ref[...]
    eid = lax.broadcasted_iota(I32, (ne, r), 0)
    within = eid % EXP_PER_GROUP
    grp = eid // EXP_PER_GROUP

    def group_rot(x, k):
        return jnp.where(within + k < EXP_PER_GROUP,
                         pltpu.roll(x, ne - k, 0), pltpu.roll(x, EXP_PER_GROUP - k, 0))

    rank = jnp.zeros((ne, r), I32)
    for k in range(1, EXP_PER_GROUP):
        mate = group_rot(sg, k)
        wrapped = within + k >= EXP_PER_GROUP
        ahead = (mate > sg) | (wrapped & (mate == sg))
        rank = rank + ahead.astype(I32)
    top2 = rank < TOP_K
    kept = jnp.where(top2, sg, 0.0)
    gscore = kept
    for k in range(1, EXP_PER_GROUP):
        gscore = gscore + group_rot(kept, k)
    win = None
    for k in range(1, N_EXPERT_GROUPS):
        other = pltpu.roll(gscore, EXP_PER_GROUP * k, 0)
        beats = (gscore > other) | ((grp < k) & (gscore == other))
        win = beats if win is None else (win & beats)
    n_valid = jnp.where(is_s, rows_s, r)
    tok = lax.broadcasted_iota(I32, (ne, r), 1)
    sel = top2 & win & (tok < n_valid)
    picked = jnp.where(sel, s, 0.0)
    wsum = jnp.sum(picked, axis=0, keepdims=True)
    gate = picked / jnp.where(tok[0:1, :] < n_valid, wsum, 1.0)

    src = lax.broadcasted_iota(I32, (r, r), 0)
    dst = lax.broadcasted_iota(I32, (r, r), 1)
    before = (src < dst).astype(BF16)
    selb = sel.astype(F32)
    ranks = jnp.dot(selb.astype(BF16), before, preferred_element_type=F32)
    slot = (starts[...][:, 0:1] + cnt_acc[...][:, 0:1] + ranks).astype(I32)
    cnt_new = cnt_acc[...] + jnp.sum(selb, axis=1, keepdims=True)
    cnt_acc[...] = cnt_new

    @pl.when((ph == 0) & (t == last))
    def _():
        totals[...] = cnt_new

    e_a = jnp.min(jnp.where(sel, eid, ne), axis=0, keepdims=True)
    e_b = jnp.max(jnp.where(sel, eid, -1), axis=0, keepdims=True)
    is_a = sel & (eid == e_a)
    is_b = sel & (eid == e_b)
    pos_a = jnp.sum(jnp.where(is_a, slot, 0), axis=0, keepdims=True)
    pos_b = jnp.sum(jnp.where(is_b, slot, 0), axis=0, keepdims=True)
    w_a = jnp.sum(jnp.where(is_a, gate, 0.0), axis=0, keepdims=True)
    w_b = jnp.sum(jnp.where(is_b, gate, 0.0), axis=0, keepdims=True)
    pos_ref[0] = jnp.concatenate([pos_a, pos_b], axis=0)
    wmat = jnp.concatenate([w_a, w_b, jnp.zeros((V7X_LANES - 2, r), F32)], axis=0)
    rw_ref[...] = wmat.T

    @pl.when((ph == 1) & (t == last))
    def _():
        ends = starts[...] + padded[...]
        lane = lax.broadcasted_iota(I32, (ne, V7X_LANES), 1)
        tile_start = (lane * MOE_TILE).astype(F32)
        te = jnp.sum((tile_start >= ends).astype(I32), axis=0, keepdims=True)
        valid = te < ne
        last_e = jnp.max(jnp.where(padded[...] > 0.0, sub, 0), axis=0, keepdims=True)
        te = jnp.where(valid, te, last_e)
        n_used = jnp.sum(valid.astype(I32), axis=1, keepdims=True) + jnp.zeros((1, V7X_LANES), I32)
        last_tile = jnp.where(padded[...] > 0.0, ends - MOE_TILE, -1.0).astype(I32)
        last_tile_row = jnp.sum(jnp.where(sub == lane, last_tile, 0), axis=0, keepdims=True)
        zero = jnp.zeros((1, V7X_LANES), I32)
        tab_ref[...] = jnp.concatenate([te, valid.astype(I32), last_tile_row, n_used,
                                        zero, zero, zero, zero], axis=0)


def _router(hp_p, hp_s, w_router, b_router):
    n_p, half = hp_p.shape
    rows_s = hp_s.shape[0]
    r = ROUTER_TILE
    nt_p = n_p // r
    nt = nt_p + 1
    kern = functools.partial(_router_kernel, nt_p=nt_p, rows_s=rows_s)
    pos, rw, tab = pl.pallas_call(
        kern,
        grid=(2, nt),
        in_specs=[pl.BlockSpec((r, half), lambda p, t: (jnp.minimum(t, nt_p - 1), 0)),
                  pl.BlockSpec((rows_s, half), lambda p, t: (0, 0)),
                  pl.BlockSpec((N_EXPERTS, 2 * half), lambda p, t: (0, 0)),
                  pl.BlockSpec((N_EXPERTS, 1), lambda p, t: (0, 0))],
        out_specs=[pl.BlockSpec((1, TOP_K, r), lambda p, t: (p * t, 0, 0)),
                   pl.BlockSpec((r, V7X_LANES), lambda p, t: (p * t, 0)),
                   pl.BlockSpec((V7X_SUBLANES, V7X_LANES), lambda p, t: (0, 0))],
        out_shape=[jax.ShapeDtypeStruct((nt, TOP_K, r), I32),
                   jax.ShapeDtypeStruct((nt * r, V7X_LANES), F32),
                   jax.ShapeDtypeStruct((V7X_SUBLANES, V7X_LANES), I32)],
        scratch_shapes=[pltpu.VMEM((N_EXPERTS, V7X_LANES), F32)] * 4,
        compiler_params=_cparams(2),
        name="router",
    )(hp_p, hp_s, w_router.T, b_router.reshape(N_EXPERTS, 1))
    return pos.reshape(-1), rw, tab.reshape(-1)


def _pos_index(tok0):
    return (tok0 // ROUTER_TILE) * (TOP_K * ROUTER_TILE) + tok0 % ROUTER_TILE


def _dispatch_kernel(pos_ref, tab_ref, hpp_ref, hps_ref, xs_ref, zbuf, sem, *, n_p_steps):
    i = pl.program_id(0)
    rows = hpp_ref.shape[0]

    @pl.when(i == 0)
    def _():
        zbuf[...] = jnp.zeros_like(zbuf)

        def fill(e):
            first = pl.multiple_of(tab_ref[TAB_LAST_TILE * V7X_LANES + e], MOE_TILE)
            return pltpu.make_async_copy(zbuf, xs_ref.at[pl.ds(first, MOE_TILE), :], sem)

        for e in range(N_EXPERTS):
            @pl.when(tab_ref[TAB_LAST_TILE * V7X_LANES + e] >= 0)
            def _():
                fill(e).start()
        for e in range(N_EXPERTS):
            @pl.when(tab_ref[TAB_LAST_TILE * V7X_LANES + e] >= 0)
            def _():
                fill(e).wait()

        def tail(j):
            first = pl.multiple_of(j * MOE_TILE, MOE_TILE)
            return pltpu.make_async_copy(zbuf, xs_ref.at[pl.ds(first, MOE_TILE), :], sem)

        def tail_start(j, carry):
            tail(j).start()
            return carry

        def tail_wait(j, carry):
            tail(j).wait()
            return carry

        n_used = tab_ref[TAB_NUSED * V7X_LANES]
        n_tiles = xs_ref.shape[0] // MOE_TILE
        lax.fori_loop(n_used, n_tiles, tail_start, 0)
        lax.fori_loop(n_used, n_tiles, tail_wait, 0)

    base = _pos_index(i * rows)

    def scatter(src_ref):
        def row_copy(r, dst):
            return pltpu.make_async_copy(src_ref.at[pl.ds(r, 1), :], xs_ref.at[pl.ds(dst, 1), :], sem)

        def issue(r, carry):
            row_copy(r, pos_ref[base + r]).start()
            row_copy(r, pos_ref[base + ROUTER_TILE + r]).start()
            return carry

        lax.fori_loop(0, rows, issue, 0, unroll=8)
        block = pltpu.make_async_copy(src_ref, xs_ref.at[pl.ds(0, rows), :], sem)
        for _ in range(TOP_K):
            block.wait()

    @pl.when(i < n_p_steps)
    def _():
        scatter(hpp_ref)

    @pl.when(i == n_p_steps)
    def _():
        scatter(hps_ref)


def _dispatch(pos, tab, hp_p, hp_s, n_rows_sorted):
    n_p, c = hp_p.shape
    n_p_steps = n_p // ROW_TILE
    assert hp_s.shape[0] == ROW_TILE
    kern = functools.partial(_dispatch_kernel, n_p_steps=n_p_steps)
    return pl.pallas_call(
        kern,
        grid_spec=pltpu.PrefetchScalarGridSpec(
            num_scalar_prefetch=2,
            grid=(n_p_steps + 1,),
            in_specs=[pl.BlockSpec((ROW_TILE, c), lambda i, p, t: (jnp.minimum(i, n_p_steps - 1), 0)),
                      pl.BlockSpec((ROW_TILE, c), lambda i, p, t: (0, 0))],
            out_specs=pl.BlockSpec(memory_space=pl.ANY),
            scratch_shapes=[pltpu.VMEM((MOE_TILE, c), U32), pltpu.SemaphoreType.DMA(())]),
        out_shape=jax.ShapeDtypeStruct((n_rows_sorted, c), U32),
        compiler_params=_cparams(1),
        name="moe_dispatch",
    )(pos, tab, hp_p, hp_s)


def _experts_kernel(tab_ref, xs_ref, wg_ref, wu_ref, wd_ref, ys_ref, wgbf, wubf, wdbf):
    i = pl.program_id(0)
    expert = tab_ref[TAB_EXPERT * V7X_LANES + i]
    prev = tab_ref[TAB_EXPERT * V7X_LANES + jnp.maximum(i - 1, 0)]
    half = xs_ref.shape[1]

    @pl.when((i == 0) | (expert != prev))
    def _():
        wgbf[...] = wg_ref[0].astype(BF16)
        wubf[...] = wu_ref[0].astype(BF16)
        wdbf[...] = wd_ref[0].astype(BF16)

    @pl.when(tab_ref[TAB_VALID * V7X_LANES + i] > 0)
    def _():
        lo, hi = _unpack_bf16_pairs(xs_ref[...])
        a = (jnp.dot(lo, wgbf[:half, :], preferred_element_type=F32)
             + jnp.dot(hi, wgbf[half:, :], preferred_element_type=F32))
        b = (jnp.dot(lo, wubf[:half, :], preferred_element_type=F32)
             + jnp.dot(hi, wubf[half:, :], preferred_element_type=F32))
        hid = (a * jax.nn.sigmoid(a)) * b
        ys_ref[...] = jnp.dot(hid.astype(BF16), wdbf[...], preferred_element_type=F32)

    @pl.when(tab_ref[TAB_VALID * V7X_LANES + i] == 0)
    def _():
        ys_ref[...] = jnp.zeros_like(ys_ref)


def _experts(tab, xs, w_gate, w_up, w_down):
    n_rows, half = xs.shape
    _, d, f = w_gate.shape
    nt = n_rows // MOE_TILE
    assert nt <= V7X_LANES

    def tile(i, tab_ref):
        return jnp.minimum(i, tab_ref[TAB_NUSED * V7X_LANES] - 1)

    def expert(i, tab_ref):
        return tab_ref[TAB_EXPERT * V7X_LANES + i]

    return pl.pallas_call(
        _experts_kernel,
        grid_spec=pltpu.PrefetchScalarGridSpec(
            num_scalar_prefetch=1,
            grid=(nt,),
            in_specs=[pl.BlockSpec((MOE_TILE, half), lambda i, t: (tile(i, t), 0)),
                      pl.BlockSpec((1, d, f), lambda i, t: (expert(i, t), 0, 0)),
                      pl.BlockSpec((1, d, f), lambda i, t: (expert(i, t), 0, 0)),
                      pl.BlockSpec((1, f, d), lambda i, t: (expert(i, t), 0, 0))],
            out_specs=pl.BlockSpec((MOE_TILE, d), lambda i, t: (i, 0)),
            scratch_shapes=[pltpu.VMEM((d, f), BF16), pltpu.VMEM((d, f), BF16), pltpu.VMEM((f, d), BF16)]),
        out_shape=jax.ShapeDtypeStruct((n_rows, d), F32),
        compiler_params=_cparams(1),
        name="moe_experts",
    )(tab, xs, w_gate, w_up, w_down)


def _combine_kernel(pos_ref, ys_ref, x1_ref, rw_ref, g2_ref, ng_ref, sc_ref, sh_ref, *rest,
                    tok0, tiles_per_seq, final):
    if final:
        x2_ref, buf_a, buf_b, sem = rest
        hn_ref = None
    else:
        x2_ref, hn_ref, buf_a, buf_b, sem = rest
    i = pl.program_id(0)
    seq = i // tiles_per_seq
    rows = x1_ref.shape[0]
    base = _pos_index(tok0 + i * rows)

    def row_copy(src, buf, r):
        return pltpu.make_async_copy(ys_ref.at[pl.ds(src, 1), :], buf.at[pl.ds(r, 1), :], sem)

    def issue(r, carry):
        row_copy(pos_ref[base + r], buf_a, r).start()
        row_copy(pos_ref[base + ROUTER_TILE + r], buf_b, r).start()
        return carry

    lax.fori_loop(0, rows, issue, 0, unroll=8)
    pltpu.make_async_copy(ys_ref.at[pl.ds(0, rows), :], buf_a, sem).wait()
    pltpu.make_async_copy(ys_ref.at[pl.ds(0, rows), :], buf_b, sem).wait()

    rw = rw_ref[...]
    moe = rw[:, 0:1] * buf_a[...] + rw[:, 1:2] * buf_b[...]
    x2 = x1_ref[...] + _mod_rows(g2_ref, seq) * moe
    if final:
        x2_ref[...] = _rms(x2, ng_ref[...])
    else:
        x2_ref[...] = x2
        hn_ref[...] = (_rms(x2, ng_ref[...]) * (1.0 + _mod_rows(sc_ref, seq))
                       + _mod_rows(sh_ref, seq)).astype(BF16)


def _combine(pos, ys, x1, rw, tok0, mod, layer, ng, tiles_per_seq, final):
    n_tok, d = x1.shape
    kern = functools.partial(_combine_kernel, tok0=tok0, tiles_per_seq=tiles_per_seq, final=final)
    rw_off = tok0 // ROW_TILE
    row_spec = lambda w: pl.BlockSpec((ROW_TILE, w), lambda i, p: (i, 0))
    out_shape = [jax.ShapeDtypeStruct((n_tok, d), F32)]
    out_specs = [row_spec(d)]
    if not final:
        out_shape.append(jax.ShapeDtypeStruct((n_tok, d), BF16))
        out_specs.append(row_spec(d))
    nxt = min(layer + 1, mod.shape[0] - 1)
    return pl.pallas_call(
        kern,
        grid_spec=pltpu.PrefetchScalarGridSpec(
            num_scalar_prefetch=1,
            grid=(n_tok // ROW_TILE,),
            in_specs=[pl.BlockSpec(memory_space=pl.ANY), row_spec(d),
                      pl.BlockSpec((ROW_TILE, V7X_LANES), lambda i, p: (rw_off + i, 0)),
                      _mod_spec(mod, layer, 5), pl.BlockSpec((1, d), lambda i, p: (0, 0)),
                      _mod_spec(mod, nxt, 1), _mod_spec(mod, nxt, 0)],
            out_specs=out_specs,
            scratch_shapes=[pltpu.VMEM((ROW_TILE, d), F32), pltpu.VMEM((ROW_TILE, d), F32),
                            pltpu.SemaphoreType.DMA(())]),
        out_shape=out_shape,
        compiler_params=_cparams(1),
        name="moe_combine_final" if final else "moe_combine",
    )(pos, ys, x1, rw, mod, ng.reshape(1, d), mod, mod)


def _moe(hp_p, hp_s, x1_p, x1_s, mod_p, mod_s, layer, ng, tps_p, w_router, b_router,
         w_gate, w_up, w_down, final):
    n_p = hp_p.shape[0]
    n_tok = n_p + hp_s.shape[0]
    max_rows = TOP_K * n_tok + N_EXPERTS * (MOE_TILE - 1)
    n_rows_sorted = -(-max_rows // MOE_TILE) * MOE_TILE
    pos, rw, tab = _router(hp_p, hp_s, w_router, b_router)
    xs = _dispatch(pos, tab, hp_p, hp_s, n_rows_sorted)
    ys = _experts(tab, xs, w_gate, w_up, w_down)
    out_p = _combine(pos, ys, x1_p, rw, 0, mod_p, layer, ng, tps_p, final)
    out_s = _combine(pos, ys, x1_s, rw, n_p, mod_s, layer, ng, 1, final)
    return out_p, out_s


def _gmlp_kernel(h_ref, w_ref, lg_ref, lb_ref, ws_ref, bs_ref, yc_ref, *rest, ell, blk, emit_v):
    if emit_v:
        gv_ref, wbf, wsbf = rest
    else:
        wbf, wsbf = rest
    i = pl.program_id(0)
    rows = h_ref.shape[0]
    c = GM_WIDTH

    @pl.when(i == 0)
    def _():
        wbf[...] = w_ref[...].astype(BF16)
        r = lax.broadcasted_iota(I32, (ell, ell), 0)
        s = lax.broadcasted_iota(I32, (ell, ell), 1)
        keep = (r >= s) & ((r // blk) == (s // blk))
        rsel = (lax.broadcasted_iota(I32, (ell, CHUNK), 0) % blk
                == lax.broadcasted_iota(I32, (ell, CHUNK), 1)).astype(BF16)
        csel = (lax.broadcasted_iota(I32, (CHUNK, ell), 1) % blk
                == lax.broadcasted_iota(I32, (CHUNK, ell), 0)).astype(BF16)
        for g in range(GM_GROUPS):
            wchunk = ws_ref[g].astype(BF16)
            if blk == ell:
                full = wchunk
            else:
                rowsp = jnp.dot(rsel, wchunk, preferred_element_type=F32).astype(BF16)
                full = jnp.dot(rowsp, csel, preferred_element_type=F32).astype(BF16)
            wsbf[g] = jnp.where(keep, full, jnp.zeros_like(full))

    uv = jnp.dot(h_ref[...], wbf[...], preferred_element_type=F32)
    u = uv[:, :c]
    v = uv[:, c:]
    vc = v - jnp.mean(v, axis=-1, keepdims=True)
    vn = vc * lax.rsqrt(jnp.mean(vc * vc, axis=-1, keepdims=True) + EPS) * lg_ref[...] + lb_ref[...]
    if emit_v:
        gv_ref[...] = vn
    vb = vn.astype(BF16)
    bs = bs_ref[...]
    for ch in range(rows // ell):
        rs = slice(ch * ell, (ch + 1) * ell)
        outs = []
        for g in range(GM_GROUPS):
            cs = slice(g * GM_GROUP, (g + 1) * GM_GROUP)
            mixed = jnp.dot(wsbf[g], vb[rs, cs], preferred_element_type=F32)
            mixed = (mixed.reshape(ell // blk, blk, GM_GROUP) + bs[:blk, g:g + 1][None]).reshape(ell, GM_GROUP)
            outs.append(u[rs, cs] * mixed)
        yc_ref[rs, :] = jnp.concatenate(outs, axis=-1).astype(BF16)


def _gmlp_mixer(h2, w_in, ln_g, ln_b, ws, bs_t, ell, blk, emit_v):
    rows, d = h2.shape
    c = GM_WIDTH
    kern = functools.partial(_gmlp_kernel, ell=ell, blk=blk, emit_v=emit_v)
    out_specs = [pl.BlockSpec((ROW_TILE, c), lambda i: (i, 0))]
    out_shape = [jax.ShapeDtypeStruct((rows, c), BF16)]
    if emit_v:
        out_specs.append(pl.BlockSpec((ROW_TILE, c), lambda i: (i, 0)))
        out_shape.append(jax.ShapeDtypeStruct((rows, c), F32))
    return pl.pallas_call(
        kern,
        grid=(rows // ROW_TILE,),
        in_specs=[pl.BlockSpec((ROW_TILE, d), lambda i: (i, 0)),
                  pl.BlockSpec((d, 2 * c), lambda i: (0, 0)),
                  pl.BlockSpec((1, c), lambda i: (0, 0)),
                  pl.BlockSpec((1, c), lambda i: (0, 0)),
                  pl.BlockSpec((GM_GROUPS, CHUNK, CHUNK), lambda i: (0, 0, 0)),
                  pl.BlockSpec((CHUNK, GM_GROUPS), lambda i: (0, 0))],
        out_specs=out_specs,
        out_shape=out_shape,
        scratch_shapes=[pltpu.VMEM((d, 2 * c), BF16), pltpu.VMEM((GM_GROUPS, ell, ell), BF16)],
        compiler_params=_cparams(1),
        name="gmlp_mixer",
    )(h2, w_in, ln_g.reshape(1, c), ln_b.reshape(1, c), ws, bs_t)


def _swa_kernel(h_ref, wq_ref, wkv_ref, kp_ref, vp_ref, bias_ref, sink_ref, yd_ref, k_ref, v_ref,
                wbf, kprev, vprev, *, tq, blocks_per_seq, has_cache):
    i = pl.program_id(0)
    rows = h_ref.shape[0]
    nq = N_HEADS * HEAD_DIM
    nkv = N_KV * HEAD_DIM
    n_blocks = rows // tq
    pair_w = 2 * HEAD_DIM
    pairs_per_kv = N_HEADS // N_KV // 2

    @pl.when(i == 0)
    def _():
        wbf[:, :nq] = wq_ref[...].astype(BF16)
        wbf[:, nq:] = wkv_ref[...].astype(BF16)
        kprev[...] = jnp.zeros_like(kprev)
        vprev[...] = jnp.zeros_like(vprev)

    qkv = jnp.dot(h_ref[...], wbf[...], preferred_element_type=F32)
    k_new = qkv[:, nq:nq + nkv]
    v_new = qkv[:, nq + nkv:]
    k_ref[...] = k_new
    v_ref[...] = v_new
    lane = lax.broadcasted_iota(I32, (2 * WINDOW, nkv), 1)
    lo_half = lane < HEAD_DIM
    col = lax.broadcasted_iota(I32, (tq, 2 * WINDOW), 1)
    nt_dims = (((1,), (1,)), ((), ()))

    def block_diag(a, a_swapped, hk):
        lo, hi = (a, a_swapped) if hk == 0 else (a_swapped, a)
        return jnp.concatenate([jnp.where(lo_half, lo, 0.0), jnp.where(lo_half, 0.0, hi)],
                               axis=0).astype(BF16)

    scores, vbds = [], []
    for blk in range(n_blocks):
        rs = slice(blk * tq, (blk + 1) * tq)
        if has_cache:
            k_old = kp_ref[blk]
            v_old = vp_ref[blk]
            first = None
        else:
            first = (i * n_blocks + blk) % blocks_per_seq == 0
            k_old = kprev[...]
            v_old = vprev[...]
        if tq < WINDOW:
            pad = jnp.zeros((WINDOW - tq, nkv), F32)
            k_cur = jnp.concatenate([k_new[rs], pad], axis=0)
            v_cur = jnp.concatenate([v_new[rs], pad], axis=0)
        else:
            k_cur = k_new[rs]
            v_cur = v_new[rs]
        if not has_cache:
            kprev[...] = k_cur
            vprev[...] = v_cur
        kcat = jnp.concatenate([k_old, k_cur], axis=0)
        vcat = jnp.concatenate([v_old, v_cur], axis=0)
        kswap = pltpu.roll(kcat, HEAD_DIM, 1)
        vswap = pltpu.roll(vcat, HEAD_DIM, 1)
        per_head = []
        for hk in range(N_KV):
            kbd = block_diag(kcat, kswap, hk)
            vbds.append(block_diag(vcat, vswap, hk))
            p0 = hk * pairs_per_kv
            qs = jnp.concatenate([qkv[rs, (p0 + pp) * pair_w:(p0 + pp + 1) * pair_w]
                                  for pp in range(pairs_per_kv)], axis=0).astype(BF16)
            s4 = lax.dot_general(qs, kbd, nt_dims, preferred_element_type=F32) * (HEAD_DIM ** -0.5)
            for pp in range(pairs_per_kv):
                for sub in range(2):
                    s = s4[pp * tq:(pp + 1) * tq, sub * 2 * WINDOW:(sub + 1) * 2 * WINDOW]
                    if first is not None:
                        s = jnp.where(first & (col < WINDOW), NEG_INF, s)
                    per_head.append(s)
        scores.append(jnp.concatenate(per_head, axis=0))

    s_all = jnp.stack(scores, axis=0) + bias_ref[...][None]
    sink = sink_ref[...][None]
    m = jnp.maximum(jnp.max(s_all, axis=-1, keepdims=True), sink)
    pr = jnp.exp(s_all - m)
    pr = pr / (jnp.sum(pr, axis=-1, keepdims=True) + jnp.exp(sink - m))

    for blk in range(n_blocks):
        outs = []
        for hk in range(N_KV):
            p4 = []
            for pp in range(pairs_per_kv):
                h0 = 2 * (hk * pairs_per_kv + pp)
                p4.append(jnp.concatenate([pr[blk, h0 * tq:(h0 + 1) * tq, :],
                                           pr[blk, (h0 + 1) * tq:(h0 + 2) * tq, :]], axis=-1))
            o4 = jnp.dot(jnp.concatenate(p4, axis=0).astype(BF16), vbds[blk * N_KV + hk],
                         preferred_element_type=F32)
            outs.extend(o4[pp * tq:(pp + 1) * tq, :] for pp in range(pairs_per_kv))
        yd_ref[blk * tq:(blk + 1) * tq, :] = jnp.concatenate(outs, axis=-1).astype(BF16)


def _swa_mixer(h2, w_in, k_cache, v_cache, bias, sinks, tq, blocks_per_seq):
    rows, d = h2.shape
    nq = N_HEADS * HEAD_DIM
    nkv = N_KV * HEAD_DIM
    nw = nq + 2 * nkv
    q_blk = (w_in.shape[1] - nw) // nq
    kv_blk = (w_in.shape[1] - 2 * nkv) // (2 * nkv)
    assert q_blk * nq + nw == w_in.shape[1] and kv_blk * 2 * nkv + 2 * nkv == w_in.shape[1]
    has_cache = k_cache is not None
    n_blocks = ROW_TILE // tq
    if not has_cache:
        k_cache = jnp.zeros((n_blocks, WINDOW, nkv), F32)
        v_cache = k_cache
    kern = functools.partial(_swa_kernel, tq=tq, blocks_per_seq=blocks_per_seq, has_cache=has_cache)
    cache_spec = pl.BlockSpec((n_blocks, WINDOW, nkv), lambda i: (i if has_cache else 0, 0, 0))
    return pl.pallas_call(
        kern,
        grid=(rows // ROW_TILE,),
        in_specs=[pl.BlockSpec((ROW_TILE, d), lambda i: (i, 0)),
                  pl.BlockSpec((d, nq), lambda i: (0, q_blk)),
                  pl.BlockSpec((d, 2 * nkv), lambda i: (0, kv_blk)),
                  cache_spec, cache_spec,
                  pl.BlockSpec((N_HEADS * tq, 2 * WINDOW), lambda i: (0, 0)),
                  pl.BlockSpec((N_HEADS * tq, 1), lambda i: (0, 0))],
        out_specs=[pl.BlockSpec((ROW_TILE, nq), lambda i: (i, 0)),
                   pl.BlockSpec((ROW_TILE, nkv), lambda i: (i, 0)),
                   pl.BlockSpec((ROW_TILE, nkv), lambda i: (i, 0))],
        out_shape=[jax.ShapeDtypeStruct((rows, nq), BF16),
                   jax.ShapeDtypeStruct((rows, nkv), F32),
                   jax.ShapeDtypeStruct((rows, nkv), F32)],
        scratch_shapes=[pltpu.VMEM((d, nw), BF16), pltpu.VMEM((WINDOW, nkv), F32),
                        pltpu.VMEM((WINDOW, nkv), F32)],
        compiler_params=_cparams(1),
        name="swa_mixer",
    )(h2, w_in, w_in, k_cache, v_cache, bias, sinks)


def _t5_bucket(dist):
    max_exact = N_BUCKETS // 2
    dd = np.maximum(dist, 1)
    large = max_exact + (np.log(dd / max_exact) / np.log(WINDOW / max_exact)
                         * (N_BUCKETS - max_exact)).astype(np.int64)
    large = np.minimum(large, N_BUCKETS - 1)
    return np.where(dist < max_exact, dist, large).astype(np.int32)


def _attention_bias(rel_bias):
    by_dist = jnp.take(rel_bias.astype(F32), _t5_bucket(np.arange(WINDOW)), axis=0).T
    neg = jnp.full((N_HEADS, WINDOW), NEG_INF, F32)
    line = jnp.concatenate([neg, by_dist[:, ::-1], neg[:, :WINDOW - 1]], axis=1)
    return jnp.stack([line[:, WINDOW - 1 - q:3 * WINDOW - 1 - q] for q in range(WINDOW)], axis=1)


def kernel(x_prompt, x_sample, state_pool, state_conv, cache_swa_k, cache_swa_v, c_prompt, c_sample, w_ada, b_ada, norm_g, final_norm_g, w_in_even, w_out_even, w_pool, pool_scale, conv_w, w_in_odd, w_out_odd, gm_norm_g, gm_norm_b, gm_w_s, gm_b_s, attn_sinks, rel_bias, w_router, b_router, w_gate, w_up, w_down):
    d = D_MODEL
    bp, tp, _ = x_prompt.shape
    bs, ts, _ = x_sample.shape
    rows_s = bs * ts
    assert rows_s == ROW_TILE and tp % ROUTER_TILE == 0 and PAST_LEN % CHUNK == 0
    assert bp <= V7X_SUBLANES and CHUNK % ts == 0

    n_c = bp + bs
    c_pad = (-n_c) % V7X_SUBLANES
    c_all = jnp.concatenate([c_prompt, c_sample, jnp.zeros((c_pad, d), F32)], axis=0)
    mod_p = _adaln(c_all, w_ada, b_ada)
    mod_s = jnp.repeat(mod_p[:, bp:bp + bs], ts, axis=1)

    tps_p = tp // ROW_TILE
    xp = x_prompt.reshape(bp * tp, d)
    xs_ = x_sample.reshape(rows_s, d)
    w_in0, w_in1 = w_in_even[0], w_in_odd[0]

    hp0 = _norm_mod(xp, norm_g[0, 0], mod_p, 0, tps_p)
    hs0 = _norm_mod(xs_, norm_g[0, 0], mod_s, 0, 1)
    zero_pool = jnp.zeros((bp, POOL_STATE, POOL_WIDTH), F32)
    zero_conv = jnp.zeros((bp, CONV_K - 1, CONV_WIDTH), F32)
    ya_p, pool_p = _pool_mixer(hp0, w_in0, w_pool[0], pool_scale[0], zero_pool, 1, ROW_TILE, 0)
    ya_s, pool_s = _pool_mixer(hs0, w_in0, w_pool[0], pool_scale[0], state_pool[0], bs, ts, PAST_LEN)
    yb_p, conv_p = _conv_mixer(hp0, w_in0, conv_w[0], zero_conv, 1, ROW_TILE)
    yb_s, conv_s = _conv_mixer(hs0, w_in0, conv_w[0], state_conv[0], bs, ts)
    x1p, hpp = _outproj(ya_p, yb_p, xp, mod_p, 0, norm_g[0, 1], w_out_even[0], tps_p)
    x1s, hps = _outproj(ya_s, yb_s, xs_, mod_s, 0, norm_g[0, 1], w_out_even[0], 1)
    (x2p, h1p), (x2s, h1s) = _moe(hpp, hps, x1p, x1s, mod_p, mod_s, 0, norm_g[1, 0], tps_p,
                                  w_router, b_router, w_gate[0], w_up[0], w_down[0], final=False)

    bs_t = gm_b_s[0].T
    (yc_p,) = _gmlp_mixer(h1p, w_in1, gm_norm_g[0], gm_norm_b[0], gm_w_s[0], bs_t, CHUNK, CHUNK, False)
    yc_s, gv_s = _gmlp_mixer(h1s, w_in1, gm_norm_g[0], gm_norm_b[0], gm_w_s[0], bs_t, rows_s, ts, True)
    bias = _attention_bias(rel_bias)
    nkv = N_KV * HEAD_DIM
    yd_p, k_p, v_p = _swa_mixer(h1p, w_in1, None, None, bias.reshape(N_HEADS * WINDOW, 2 * WINDOW),
                                jnp.repeat(attn_sinks[0], WINDOW).reshape(-1, 1), WINDOW, tp // WINDOW)
    yd_s, k_s, v_s = _swa_mixer(h1s, w_in1, cache_swa_k[0].reshape(bs, WINDOW, nkv),
                                cache_swa_v[0].reshape(bs, WINDOW, nkv),
                                bias[:, :ts, :].reshape(N_HEADS * ts, 2 * WINDOW),
                                jnp.repeat(attn_sinks[0], ts).reshape(-1, 1), ts, 1)
    x1p, hpp = _outproj(yc_p, yd_p, x2p, mod_p, 1, norm_g[1, 1], w_out_odd[0], tps_p)
    x1s, hps = _outproj(yc_s, yd_s, x2s, mod_s, 1, norm_g[1, 1], w_out_odd[0], 1)
    (yp,), (ys_out,) = _moe(hpp, hps, x1p, x1s, mod_p, mod_s, 1, final_norm_g, tps_p,
                            w_router, b_router, w_gate[1], w_up[1], w_down[1], final=True)

    k_p4 = k_p.reshape(bp, tp, N_KV, HEAD_DIM)[:, -WINDOW:]
    v_p4 = v_p.reshape(bp, tp, N_KV, HEAD_DIM)[:, -WINDOW:]
    k_s4 = jnp.concatenate([cache_swa_k[0], k_s.reshape(bs, ts, N_KV, HEAD_DIM)], axis=1)[:, -WINDOW:]
    v_s4 = jnp.concatenate([cache_swa_v[0], v_s.reshape(bs, ts, N_KV, HEAD_DIM)], axis=1)[:, -WINDOW:]
    return (yp.reshape(bp, tp, d), ys_out.reshape(bs, ts, d),
            pool_p[None], pool_s[None], conv_p[None], conv_s[None],
            k_p4[None], k_s4[None], v_p4[None], v_s4[None],
            gv_s.reshape(bs, ts, GM_WIDTH)[None])
```

```python
import functools

import numpy as np
import jax
import jax.numpy as jnp
from jax import lax
from jax.experimental import pallas as pl
from jax.experimental.pallas import tpu as pltpu

F32 = jnp.float32
BF16 = jnp.bfloat16
I32 = jnp.int32
U32 = jnp.uint32

D_MODEL = 2048
POOL_WINDOWS = (2, 4, 8, 16)
POOL_WIDTH = 1024
POOL_GROUP = 256
POOL_STATE = 15
CONV_WIDTH = 1024
CONV_K = 3
GM_WIDTH = 1024
GM_GROUPS = 8
GM_GROUP = 128
CHUNK = 128
HEAD_DIM = 64
N_HEADS = 16
N_KV = 2
WINDOW = 128
N_BUCKETS = 32
N_EXPERTS = 16
N_EXPERT_GROUPS = 4
EXP_PER_GROUP = 4
TOP_K = 2
EPS = 1e-6
NEG_INF = -1e30
PAST_LEN = 16384

V7X_SUBLANES = 8
V7X_LANES = 128
VMEM_LIMIT = 56 * 1024 * 1024

ROW_TILE = 256
ROUTER_TILE = 1024
POOL_HALO = 16
CONV_HALO = 8
MOE_TILE = 256
TAB_EXPERT, TAB_VALID, TAB_LAST_TILE, TAB_NUSED, TAB_NEXT = 0, 1, 2, 3, 4


def _cparams(n_axes):
    return pltpu.CompilerParams(dimension_semantics=("arbitrary",) * n_axes,
                                vmem_limit_bytes=VMEM_LIMIT)


def _rms(x, g):
    return x * lax.rsqrt(jnp.mean(x * x, axis=-1, keepdims=True) + EPS) * g


def _mod_spec(mod, layer, part):
    nrow = ROW_TILE if mod.shape[1] == ROW_TILE else V7X_SUBLANES
    return pl.BlockSpec((1, nrow, D_MODEL), lambda *_: (layer, 0, part))


def _mod_rows(m_ref, seq):
    if m_ref.shape[1] == V7X_SUBLANES:
        return m_ref[0, pl.ds(seq, 1), :]
    return m_ref[0]


def _adaln_kernel(c_ref, w_ref, b_ref, o_ref):
    c = c_ref[...]
    a = (c * jax.nn.sigmoid(c)).astype(BF16)
    o_ref[0] = jnp.dot(a, w_ref[0].astype(BF16), preferred_element_type=F32) + b_ref[0]


def _adaln(c_all, w_ada, b_ada):
    depth, d, n6 = w_ada.shape
    m = c_all.shape[0]
    tn = 1024
    return pl.pallas_call(
        _adaln_kernel,
        grid=(depth, n6 // tn),
        in_specs=[pl.BlockSpec((m, d), lambda l, j: (0, 0)),
                  pl.BlockSpec((1, d, tn), lambda l, j: (l, 0, j)),
                  pl.BlockSpec((1, 1, tn), lambda l, j: (l, 0, j))],
        out_specs=pl.BlockSpec((1, m, tn), lambda l, j: (l, 0, j)),
        out_shape=jax.ShapeDtypeStruct((depth, m, n6), F32),
        compiler_params=_cparams(2),
        name="adaln",
    )(c_all, w_ada, b_ada.reshape(depth, 1, n6))


def _norm_mod_kernel(x_ref, g_ref, sc_ref, sh_ref, h_ref, *, tiles_per_seq):
    seq = pl.program_id(0) // tiles_per_seq
    h = _rms(x_ref[...], g_ref[...]) * (1.0 + _mod_rows(sc_ref, seq)) + _mod_rows(sh_ref, seq)
    h_ref[...] = h.astype(BF16)


def _norm_mod(x2, g, mod, layer, tiles_per_seq):
    rows, d = x2.shape
    return pl.pallas_call(
        functools.partial(_norm_mod_kernel, tiles_per_seq=tiles_per_seq),
        grid=(rows // ROW_TILE,),
        in_specs=[pl.BlockSpec((ROW_TILE, d), lambda i: (i, 0)),
                  pl.BlockSpec((1, d), lambda i: (0, 0)),
                  _mod_spec(mod, layer, 1), _mod_spec(mod, layer, 0)],
        out_specs=pl.BlockSpec((ROW_TILE, d), lambda i: (i, 0)),
        out_shape=jax.ShapeDtypeStruct((rows, d), BF16),
        compiler_params=_cparams(1),
        name="norm_mod",
    )(x2, g.reshape(1, d), mod, mod)


def _pool_kernel(h_ref, w_ref, wp_ref, ps_ref, st_ref, ya_ref, ns_ref, wbf, wpbf, carry,
                 *, nb, tm, tiles_per_seq, start):
    i = pl.program_id(0)
    t = i % tiles_per_seq
    c = POOL_WIDTH
    halo = POOL_HALO

    @pl.when(i == 0)
    def _():
        wbf[...] = w_ref[...].astype(BF16)
        wpbf[...] = wp_ref[...].astype(BF16)

    @pl.when(t == 0)
    def _():
        carry[...] = st_ref[...]

    p = jnp.dot(h_ref[...], wbf[...], preferred_element_type=F32)
    p3 = p.reshape(nb, tm, c)
    ext3 = jnp.concatenate([carry[...], p3], axis=1)
    tail = ext3[:, tm:tm + halo, :]
    ns_ref[...] = tail
    carry[...] = tail
    ext = ext3.reshape(nb * (halo + tm), c)
    pos = start + t * tm + lax.broadcasted_iota(I32, (1, tm, 1), 1)
    outs = []
    for gi, w in enumerate(POOL_WINDOWS):
        sl = slice(gi * POOL_GROUP, (gi + 1) * POOL_GROUP)
        acc = ext[:, sl]
        shift = 1
        while shift < w:
            acc = acc + pltpu.roll(acc, shift, 0)
            shift *= 2
        win = acc.reshape(nb, halo + tm, POOL_GROUP)[:, halo:, :]
        cnt = jnp.minimum(pos + 1, w).astype(F32)
        dgrp = win / cnt - p3[:, :, sl]
        outs.append(jnp.dot(dgrp.reshape(nb * tm, POOL_GROUP).astype(BF16), wpbf[gi],
                            preferred_element_type=F32))
    y = jnp.concatenate(outs, axis=-1) * ps_ref[...]
    ya_ref[...] = y.astype(BF16)


def _pool_mixer(h2, w_in, w_pool, pool_scale, state, nb, tm, start):
    rows, d = h2.shape
    nseq = state.shape[0]
    tiles_per_seq = (rows // nseq) // tm
    seq_blocks = nseq // nb
    c = POOL_WIDTH
    st = jnp.pad(state, ((0, 0), (POOL_HALO - POOL_STATE, 0), (0, 0)))
    kern = functools.partial(_pool_kernel, nb=nb, tm=tm, tiles_per_seq=tiles_per_seq, start=start)
    ya, ns = pl.pallas_call(
        kern,
        grid=(seq_blocks * tiles_per_seq,),
        in_specs=[pl.BlockSpec((nb * tm, d), lambda i: (i, 0)),
                  pl.BlockSpec((d, c), lambda i: (0, 0)),
                  pl.BlockSpec((len(POOL_WINDOWS), POOL_GROUP, POOL_GROUP), lambda i: (0, 0, 0)),
                  pl.BlockSpec((1, c), lambda i: (0, 0)),
                  pl.BlockSpec((nb, POOL_HALO, c), lambda i: (i // tiles_per_seq, 0, 0))],
        out_specs=[pl.BlockSpec((nb * tm, c), lambda i: (i, 0)),
                   pl.BlockSpec((nb, POOL_HALO, c), lambda i: (i // tiles_per_seq, 0, 0))],
        out_shape=[jax.ShapeDtypeStruct((rows, c), BF16),
                   jax.ShapeDtypeStruct((nseq, POOL_HALO, c), F32)],
        scratch_shapes=[pltpu.VMEM((d, c), BF16),
                        pltpu.VMEM((len(POOL_WINDOWS), POOL_GROUP, POOL_GROUP), BF16),
                        pltpu.VMEM((nb, POOL_HALO, c), F32)],
        compiler_params=_cparams(1),
        name="pool_mixer",
    )(h2, w_in, w_pool, pool_scale.reshape(1, c), st)
    return ya, ns[:, POOL_HALO - POOL_STATE:, :]


def _conv_kernel(h_ref, wx_ref, wb_ref, wc_ref, cw_ref, st_ref, yb_ref, ns_ref,
                 wxbf, wbbf, wcbf, carry, *, nb, tm, tiles_per_seq):
    i = pl.program_id(1)
    t = i % tiles_per_seq
    tc = wxbf.shape[1]
    halo = CONV_HALO

    @pl.when(i == 0)
    def _():
        wxbf[...] = wx_ref[...].astype(BF16)
        wbbf[...] = wb_ref[...].astype(BF16)
        wcbf[...] = wc_ref[...].astype(BF16)

    @pl.when(t == 0)
    def _():
        carry[...] = st_ref[...]

    h = h_ref[...]
    xin = jnp.dot(h, wxbf[...], preferred_element_type=F32)
    gb = jnp.dot(h, wbbf[...], preferred_element_type=F32)
    gc = jnp.dot(h, wcbf[...], preferred_element_type=F32)
    z3 = (gc * xin).reshape(nb, tm, tc)
    ext3 = jnp.concatenate([carry[...], z3], axis=1)
    tail = ext3[:, tm:tm + halo, :]
    ns_ref[...] = tail
    carry[...] = tail
    ext = ext3.reshape(nb * (halo + tm), tc)
    cw = cw_ref[...]
    conv = cw[0:1, :] * pltpu.roll(ext, 2, 0) + cw[1:2, :] * pltpu.roll(ext, 1, 0) + cw[2:3, :] * ext
    conv = conv.reshape(nb, halo + tm, tc)[:, halo:, :].reshape(nb * tm, tc)
    yb_ref[...] = (gb * conv).astype(BF16)


def _conv_mixer(h2, w_in, conv_w, state, nb, tm):
    rows, d = h2.shape
    nseq = state.shape[0]
    tiles_per_seq = (rows // nseq) // tm
    seq_blocks = nseq // nb
    c = CONV_WIDTH
    tc = 512
    cb = c // tc
    base = POOL_WIDTH // tc
    st = jnp.pad(state, ((0, 0), (CONV_HALO - (CONV_K - 1), 0), (0, 0)))
    kern = functools.partial(_conv_kernel, nb=nb, tm=tm, tiles_per_seq=tiles_per_seq)
    yb, ns = pl.pallas_call(
        kern,
        grid=(cb, seq_blocks * tiles_per_seq),
        in_specs=[pl.BlockSpec((nb * tm, d), lambda j, i: (i, 0)),
                  pl.BlockSpec((d, tc), lambda j, i: (0, base + j)),
                  pl.BlockSpec((d, tc), lambda j, i: (0, base + cb + j)),
                  pl.BlockSpec((d, tc), lambda j, i: (0, base + 2 * cb + j)),
                  pl.BlockSpec((CONV_K, tc), lambda j, i: (0, j)),
                  pl.BlockSpec((nb, CONV_HALO, tc), lambda j, i: (i // tiles_per_seq, 0, j))],
        out_specs=[pl.BlockSpec((nb * tm, tc), lambda j, i: (i, j)),
                   pl.BlockSpec((nb, CONV_HALO, tc), lambda j, i: (i // tiles_per_seq, 0, j))],
        out_shape=[jax.ShapeDtypeStruct((rows, c), BF16),
                   jax.ShapeDtypeStruct((nseq, CONV_HALO, c), F32)],
        scratch_shapes=[pltpu.VMEM((d, tc), BF16)] * 3 + [pltpu.VMEM((nb, CONV_HALO, tc), F32)],
        compiler_params=_cparams(2),
        name="conv_mixer",
    )(h2, w_in, w_in, w_in, conv_w, st)
    return yb, ns[:, CONV_HALO - (CONV_K - 1):, :]


def _pack_bf16_pairs(v):
    c = v.shape[1] // 2
    r = v.astype(BF16).astype(F32)
    lo = pltpu.bitcast(r[:, :c], U32)
    hi = pltpu.bitcast(r[:, c:], U32)
    return (hi & jnp.uint32(0xFFFF0000)) | (lo >> 16)


def _unpack_bf16_pairs(w):
    lo = pltpu.bitcast(w << 16, F32).astype(BF16)
    hi = pltpu.bitcast(w & jnp.uint32(0xFFFF0000), F32).astype(BF16)
    return lo, hi


def _outproj_kernel(ya_ref, yb_ref, x_ref, g1_ref, sc_ref, sh_ref, ng_ref, wo_ref,
                    x1_ref, hp_ref, wobf, *, tiles_per_seq):
    i = pl.program_id(0)
    seq = i // tiles_per_seq
    half = ya_ref.shape[1]

    @pl.when(i == 0)
    def _():
        wobf[...] = wo_ref[...].astype(BF16)

    y = (jnp.dot(ya_ref[...], wobf[:half, :], preferred_element_type=F32)
         + jnp.dot(yb_ref[...], wobf[half:, :], preferred_element_type=F32))
    x1 = x_ref[...] + _mod_rows(g1_ref, seq) * y
    x1_ref[...] = x1
    h2 = _rms(x1, ng_ref[...]) * (1.0 + _mod_rows(sc_ref, seq)) + _mod_rows(sh_ref, seq)
    hp_ref[...] = _pack_bf16_pairs(h2)


def _outproj(ya, yb, x2, mod, layer, ng, w_out, tiles_per_seq):
    rows_all, d = x2.shape
    half = ya.shape[1]
    row_spec = lambda w: pl.BlockSpec((ROW_TILE, w), lambda i: (i, 0))
    return pl.pallas_call(
        functools.partial(_outproj_kernel, tiles_per_seq=tiles_per_seq),
        grid=(rows_all // ROW_TILE,),
        in_specs=[row_spec(half), row_spec(half), row_spec(d),
                  _mod_spec(mod, layer, 2), _mod_spec(mod, layer, 4), _mod_spec(mod, layer, 3),
                  pl.BlockSpec((1, d), lambda i: (0, 0)), pl.BlockSpec((d, d), lambda i: (0, 0))],
        out_specs=[row_spec(d), row_spec(d // 2)],
        out_shape=[jax.ShapeDtypeStruct((rows_all, d), F32),
                   jax.ShapeDtypeStruct((rows_all, d // 2), U32)],
        scratch_shapes=[pltpu.VMEM((d, d), BF16)],
        compiler_params=_cparams(1),
        name="outproj",
    )(ya, yb, x2, mod, mod, mod, ng.reshape(1, d), w_out)


def _router_kernel(hpp_ref, hps_ref, wr_ref, br_ref, pos_ref, rw_ref, tab_ref,
                   cnt_acc, totals, starts, padded, *, nt_p, rows_s):
    ph = pl.program_id(0)
    t = pl.program_id(1)
    last = nt_p
    r = hpp_ref.shape[0]
    half = hpp_ref.shape[1]
    ne = N_EXPERTS
    sub = lax.broadcasted_iota(I32, (ne, V7X_LANES), 0)

    @pl.when(t == 0)
    def _():
        cnt_acc[...] = jnp.zeros_like(cnt_acc)

    @pl.when((ph == 0) & (t == 0))
    def _():
        starts[...] = jnp.zeros_like(starts)
        padded[...] = jnp.zeros_like(padded)

    @pl.when((ph == 1) & (t == 0))
    def _():
        pad = jnp.floor((totals[...] + (MOE_TILE - 1.0)) * (1.0 / MOE_TILE)) * MOE_TILE
        run = pad
        k = 1
        while k < ne:
            run = run + jnp.where(sub >= k, pltpu.roll(run, k, 0), 0.0)
            k *= 2
        padded[...] = pad
        starts[...] = run - pad

    is_s = t == last
    w_s = jnp.concatenate([hps_ref[...], jnp.zeros((r - rows_s, half), U32)], axis=0)
    w = jnp.where(is_s, w_s, hpp_ref[...])
    lo, hi = _unpack_bf16_pairs(w)
    wr = wr_ref[...].astype(BF16)
    nt_dims = (((1,), (1,)), ((), ()))
    log_t = (lax.dot_general(wr[:, :half], lo, nt_dims, preferred_element_type=F32)
             + lax.dot_general(wr[:, half:], hi, nt_dims, preferred_element_type=F32))

    s = jax.nn.sigmoid(log_t)
    sg = s + br_ref[...]
    eid = lax.broadcasted_iota(I32, (ne, r), 0)
    within = eid % EXP_PER_GROUP
    grp = eid // EXP_PER_GROUP

    def group_rot(x, k):
        return jnp.where(within + k < EXP_PER_GROUP,
                         pltpu.roll(x, ne - k, 0), pltpu.roll(x, EXP_PER_GROUP - k, 0))

    rank = jnp.zeros((ne, r), I32)
    for k in range(1, EXP_PER_GROUP):
        mate = group_rot(sg, k)
        wrapped = within + k >= EXP_PER_GROUP
        ahead = (mate > sg) | (wrapped & (mate == sg))
        rank = rank + ahead.astype(I32)
    top2 = rank < TOP_K
    kept = jnp.where(top2, sg, 0.0)
    gscore = kept
    for k in range(1, EXP_PER_GROUP):
        gscore = gscore + group_rot(kept, k)
    win = None
    for k in range(1, N_EXPERT_GROUPS):
        other = pltpu.roll(gscore, EXP_PER_GROUP * k, 0)
        beats = (gscore > other) | ((grp < k) & (gscore == other))
        win = beats if win is None else (win & beats)
    n_valid = jnp.where(is_s, rows_s, r)
    tok = lax.broadcasted_iota(I32, (ne, r), 1)
    sel = top2 & win & (tok < n_valid)
    picked = jnp.where(sel, s, 0.0)
    wsum = jnp.sum(picked, axis=0, keepdims=True)
    gate = picked / jnp.where(tok[0:1, :] < n_valid, wsum, 1.0)

    src = lax.broadcasted_iota(I32, (r, r), 0)
    dst = lax.broadcasted_iota(I32, (r, r), 1)
    before = (src < dst).astype(BF16)
    selb = sel.astype(F32)
    ranks = jnp.dot(selb.astype(BF16), before, preferred_element_type=F32)
    slot = (starts[...][:, 0:1] + cnt_acc[...][:, 0:1] + ranks).astype(I32)
    cnt_new = cnt_acc[...] + jnp.sum(selb, axis=1, keepdims=True)
    cnt_acc[...] = cnt_new

    @pl.when((ph == 0) & (t == last))
    def _():
        totals[...] = cnt_new

    e_a = jnp.min(jnp.where(sel, eid, ne), axis=0, keepdims=True)
    e_b = jnp.max(jnp.where(sel, eid, -1), axis=0, keepdims=True)
    is_a = sel & (eid == e_a)
    is_b = sel & (eid == e_b)
    pos_a = jnp.sum(jnp.where(is_a, slot, 0), axis=0, keepdims=True)
    pos_b = jnp.sum(jnp.where(is_b, slot, 0), axis=0, keepdims=True)
    w_a = jnp.sum(jnp.where(is_a, gate, 0.0), axis=0, keepdims=True)
    w_b = jnp.sum(jnp.where(is_b, gate, 0.0), axis=0, keepdims=True)
    pos_ref[0] = jnp.concatenate([pos_a, pos_b], axis=0)
    wmat = jnp.concatenate([w_a, w_b, jnp.zeros((V7X_LANES - 2, r), F32)], axis=0)
    rw_ref[...] = wmat.T

    @pl.when((ph == 1) & (t == last))
    def _():
        ends = starts[...] + padded[...]
        lane = lax.broadcasted_iota(I32, (ne, V7X_LANES), 1)
        tile_start = (lane * MOE_TILE).astype(F32)
        te = jnp.sum((tile_start >= ends).astype(I32), axis=0, keepdims=True)
        valid = te < ne
        last_e = jnp.max(jnp.where(padded[...] > 0.0, sub, 0), axis=0, keepdims=True)
        te = jnp.where(valid, te, last_e)
        n_used = jnp.sum(valid.astype(I32), axis=1, keepdims=True) + jnp.zeros((1, V7X_LANES), I32)
        last_tile = jnp.where(padded[...] > 0.0, ends - MOE_TILE, -1.0).astype(I32)
        last_tile_row = jnp.sum(jnp.where(sub == lane, last_tile, 0), axis=0, keepdims=True)
        later = jnp.min(jnp.where((sub > te) & (padded[...] > 0.0), sub, ne), axis=0, keepdims=True)
        next_e = jnp.where(later < ne, later, -1)
        zero = jnp.zeros((1, V7X_LANES), I32)
        tab_ref[...] = jnp.concatenate([te, valid.astype(I32), last_tile_row, n_used, next_e,
                                        zero, zero, zero], axis=0)


def _router(hp_p, hp_s, w_router, b_router):
    n_p, half = hp_p.shape
    rows_s = hp_s.shape[0]
    r = ROUTER_TILE
    nt_p = n_p // r
    nt = nt_p + 1
    kern = functools.partial(_router_kernel, nt_p=nt_p, rows_s=rows_s)
    pos, rw, tab = pl.pallas_call(
        kern,
        grid=(2, nt),
        in_specs=[pl.BlockSpec((r, half), lambda p, t: (jnp.minimum(t, nt_p - 1), 0)),
                  pl.BlockSpec((rows_s, half), lambda p, t: (0, 0)),
                  pl.BlockSpec((N_EXPERTS, 2 * half), lambda p, t: (0, 0)),
                  pl.BlockSpec((N_EXPERTS, 1), lambda p, t: (0, 0))],
        out_specs=[pl.BlockSpec((1, TOP_K, r), lambda p, t: (p * t, 0, 0)),
                   pl.BlockSpec((r, V7X_LANES), lambda p, t: (p * t, 0)),
                   pl.BlockSpec((V7X_SUBLANES, V7X_LANES), lambda p, t: (0, 0))],
        out_shape=[jax.ShapeDtypeStruct((nt, TOP_K, r), I32),
                   jax.ShapeDtypeStruct((nt * r, V7X_LANES), F32),
                   jax.ShapeDtypeStruct((V7X_SUBLANES, V7X_LANES), I32)],
        scratch_shapes=[pltpu.VMEM((N_EXPERTS, V7X_LANES), F32)] * 4,
        compiler_params=_cparams(2),
        name="router",
    )(hp_p, hp_s, w_router.T, b_router.reshape(N_EXPERTS, 1))
    return pos.reshape(-1), rw, tab.reshape(-1)


def _pos_index(tok0):
    return (tok0 // ROUTER_TILE) * (TOP_K * ROUTER_TILE) + tok0 % ROUTER_TILE


def _dispatch_kernel(pos_ref, tab_ref, hpp_ref, hps_ref, xs_ref, zbuf, sem, *, n_p_steps):
    i = pl.program_id(0)
    rows = hpp_ref.shape[0]

    @pl.when(i == 0)
    def _():
        zbuf[...] = jnp.zeros_like(zbuf)

        def fill(e):
            first = pl.multiple_of(tab_ref[TAB_LAST_TILE * V7X_LANES + e], MOE_TILE)
            return pltpu.make_async_copy(zbuf, xs_ref.at[pl.ds(first, MOE_TILE), :], sem)

        for e in range(N_EXPERTS):
            @pl.when(tab_ref[TAB_LAST_TILE * V7X_LANES + e] >= 0)
            def _():
                fill(e).start()
        for e in range(N_EXPERTS):
            @pl.when(tab_ref[TAB_LAST_TILE * V7X_LANES + e] >= 0)
            def _():
                fill(e).wait()

        def tail(j):
            first = pl.multiple_of(j * MOE_TILE, MOE_TILE)
            return pltpu.make_async_copy(zbuf, xs_ref.at[pl.ds(first, MOE_TILE), :], sem)

        def tail_start(j, carry):
            tail(j).start()
            return carry

        def tail_wait(j, carry):
            tail(j).wait()
            return carry

        n_used = tab_ref[TAB_NUSED * V7X_LANES]
        n_tiles = xs_ref.shape[0] // MOE_TILE
        lax.fori_loop(n_used, n_tiles, tail_start, 0)
        lax.fori_loop(n_used, n_tiles, tail_wait, 0)

    base = _pos_index(i * rows)

    def scatter(src_ref):
        def row_copy(r, dst):
            return pltpu.make_async_copy(src_ref.at[pl.ds(r, 1), :], xs_ref.at[pl.ds(dst, 1), :], sem)

        def issue(r, carry):
            row_copy(r, pos_ref[base + r]).start()
            row_copy(r, pos_ref[base + ROUTER_TILE + r]).start()
            return carry

        lax.fori_loop(0, rows, issue, 0, unroll=8)
        block = pltpu.make_async_copy(src_ref, xs_ref.at[pl.ds(0, rows), :], sem)
        for _ in range(TOP_K):
            block.wait()

    @pl.when(i < n_p_steps)
    def _():
        scatter(hpp_ref)

    @pl.when(i == n_p_steps)
    def _():
        scatter(hps_ref)


def _dispatch(pos, tab, hp_p, hp_s, n_rows_sorted):
    n_p, c = hp_p.shape
    n_p_steps = n_p // ROW_TILE
    assert hp_s.shape[0] == ROW_TILE
    kern = functools.partial(_dispatch_kernel, n_p_steps=n_p_steps)
    return pl.pallas_call(
        kern,
        grid_spec=pltpu.PrefetchScalarGridSpec(
            num_scalar_prefetch=2,
            grid=(n_p_steps + 1,),
            in_specs=[pl.BlockSpec((ROW_TILE, c), lambda i, p, t: (jnp.minimum(i, n_p_steps - 1), 0)),
                      pl.BlockSpec((ROW_TILE, c), lambda i, p, t: (0, 0))],
            out_specs=pl.BlockSpec(memory_space=pl.ANY),
            scratch_shapes=[pltpu.VMEM((MOE_TILE, c), U32), pltpu.SemaphoreType.DMA(())]),
        out_shape=jax.ShapeDtypeStruct((n_rows_sorted, c), U32),
        compiler_params=_cparams(1),
        name="moe_dispatch",
    )(pos, tab, hp_p, hp_s)


def _experts_kernel(tab_ref, xs_ref, wg_hbm, wu_hbm, wd_hbm, ys_ref,
                    wg32, wu32, wd32, wgbf, wubf, wdbf, slot_ref, sems, *, layer):
    i = pl.program_id(0)
    expert = tab_ref[TAB_EXPERT * V7X_LANES + i]
    prev = tab_ref[TAB_EXPERT * V7X_LANES + jnp.maximum(i - 1, 0)]
    upcoming = tab_ref[TAB_NEXT * V7X_LANES + i]
    changed = (i == 0) | (expert != prev)
    half = xs_ref.shape[1]

    def weight_copies(e, slot):
        return (pltpu.make_async_copy(wg_hbm.at[layer, e], wg32.at[slot], sems.at[0, slot]),
                pltpu.make_async_copy(wu_hbm.at[layer, e], wu32.at[slot], sems.at[1, slot]),
                pltpu.make_async_copy(wd_hbm.at[layer, e], wd32.at[slot], sems.at[2, slot]))

    @pl.when(i == 0)
    def _():
        slot_ref[0] = 0
        for cp in weight_copies(expert, 0):
            cp.start()

    @pl.when(changed & (i > 0))
    def _():
        slot_ref[0] = 1 - slot_ref[0]

    for slot in range(2):
        @pl.when(changed & (slot_ref[0] == slot))
        def _():
            for cp in weight_copies(expert, slot):
                cp.wait()
            wgbf[...] = wg32[slot].astype(BF16)
            wubf[...] = wu32[slot].astype(BF16)
            wdbf[...] = wd32[slot].astype(BF16)

            @pl.when(upcoming >= 0)
            def _():
                for cp in weight_copies(upcoming, 1 - slot):
                    cp.start()

    @pl.when(tab_ref[TAB_VALID * V7X_LANES + i] > 0)
    def _():
        lo, hi = _unpack_bf16_pairs(xs_ref[...])
        a = (jnp.dot(lo, wgbf[:half, :], preferred_element_type=F32)
             + jnp.dot(hi, wgbf[half:, :], preferred_element_type=F32))
        b = (jnp.dot(lo, wubf[:half, :], preferred_element_type=F32)
             + jnp.dot(hi, wubf[half:, :], preferred_element_type=F32))
        hid = (a * jax.nn.sigmoid(a)) * b
        ys_ref[...] = jnp.dot(hid.astype(BF16), wdbf[...], preferred_element_type=F32)

    @pl.when(tab_ref[TAB_VALID * V7X_LANES + i] == 0)
    def _():
        ys_ref[...] = jnp.zeros_like(ys_ref)


def _experts(tab, xs, w_gate, w_up, w_down, layer):
    n_rows, half = xs.shape
    _, _, d, f = w_gate.shape
    nt = n_rows // MOE_TILE
    assert nt <= V7X_LANES

    def tile(i, tab_ref):
        return jnp.minimum(i, tab_ref[TAB_NUSED * V7X_LANES] - 1)

    hbm = pl.BlockSpec(memory_space=pl.ANY)
    return pl.pallas_call(
        functools.partial(_experts_kernel, layer=layer),
        grid_spec=pltpu.PrefetchScalarGridSpec(
            num_scalar_prefetch=1,
            grid=(nt,),
            in_specs=[pl.BlockSpec((MOE_TILE, half), lambda i, t: (tile(i, t), 0)), hbm, hbm, hbm],
            out_specs=pl.BlockSpec((MOE_TILE, d), lambda i, t: (i, 0)),
            scratch_shapes=[pltpu.VMEM((2, d, f), F32), pltpu.VMEM((2, d, f), F32), pltpu.VMEM((2, f, d), F32),
                            pltpu.VMEM((d, f), BF16), pltpu.VMEM((d, f), BF16), pltpu.VMEM((f, d), BF16),
                            pltpu.SMEM((1,), I32), pltpu.SemaphoreType.DMA((3, 2))]),
        out_shape=jax.ShapeDtypeStruct((n_rows, d), F32),
        compiler_params=_cparams(1),
        name="moe_experts",
    )(tab, xs, w_gate, w_up, w_down)


def _combine_kernel(pos_ref, ys_ref, x1_ref, rw_ref, g2_ref, ng_ref, sc_ref, sh_ref, *rest,
                    tok0, tiles_per_seq, n_steps, final):
    if final:
        x2_ref, buf, sems = rest
        hn_ref = None
    else:
        x2_ref, hn_ref, buf, sems = rest
    i = pl.program_id(0)
    seq = i // tiles_per_seq
    rows = x1_ref.shape[0]

    def gather(step, slot):
        base = _pos_index(tok0 + step * rows)

        def issue(r, carry):
            for k in range(TOP_K):
                src = pos_ref[base + k * ROUTER_TILE + r]
                pltpu.make_async_copy(ys_ref.at[pl.ds(src, 1), :], buf.at[slot, k, pl.ds(r, 1), :],
                                      sems.at[slot]).start()
            return carry

        lax.fori_loop(0, rows, issue, 0, unroll=8)

    @pl.when(i == 0)
    def _():
        gather(0, 0)

    @pl.when(i + 1 < n_steps)
    def _():
        gather(i + 1, (i + 1) % 2)

    slot = i % 2
    for k in range(TOP_K):
        pltpu.make_async_copy(ys_ref.at[pl.ds(0, rows), :], buf.at[slot, k], sems.at[slot]).wait()

    rw = rw_ref[...]
    moe = rw[:, 0:1] * buf[slot, 0] + rw[:, 1:2] * buf[slot, 1]
    x2 = x1_ref[...] + _mod_rows(g2_ref, seq) * moe
    if final:
        x2_ref[...] = _rms(x2, ng_ref[...])
    else:
        x2_ref[...] = x2
        hn_ref[...] = (_rms(x2, ng_ref[...]) * (1.0 + _mod_rows(sc_ref, seq))
                       + _mod_rows(sh_ref, seq)).astype(BF16)


def _combine(pos, ys, x1, rw, tok0, mod, layer, ng, tiles_per_seq, final):
    n_tok, d = x1.shape
    kern = functools.partial(_combine_kernel, tok0=tok0, tiles_per_seq=tiles_per_seq,
                             n_steps=n_tok // ROW_TILE, final=final)
    rw_off = tok0 // ROW_TILE
    row_spec = lambda w: pl.BlockSpec((ROW_TILE, w), lambda i, p: (i, 0))
    out_shape = [jax.ShapeDtypeStruct((n_tok, d), F32)]
    out_specs = [row_spec(d)]
    if not final:
        out_shape.append(jax.ShapeDtypeStruct((n_tok, d), BF16))
        out_specs.append(row_spec(d))
    nxt = min(layer + 1, mod.shape[0] - 1)
    return pl.pallas_call(
        kern,
        grid_spec=pltpu.PrefetchScalarGridSpec(
            num_scalar_prefetch=1,
            grid=(n_tok // ROW_TILE,),
            in_specs=[pl.BlockSpec(memory_space=pl.ANY), row_spec(d),
                      pl.BlockSpec((ROW_TILE, V7X_LANES), lambda i, p: (rw_off + i, 0)),
                      _mod_spec(mod, layer, 5), pl.BlockSpec((1, d), lambda i, p: (0, 0)),
                      _mod_spec(mod, nxt, 1), _mod_spec(mod, nxt, 0)],
            out_specs=out_specs,
            scratch_shapes=[pltpu.VMEM((2, TOP_K, ROW_TILE, d), F32), pltpu.SemaphoreType.DMA((2,))]),
        out_shape=out_shape,
        compiler_params=_cparams(1),
        name="moe_combine_final" if final else "moe_combine",
    )(pos, ys, x1, rw, mod, ng.reshape(1, d), mod, mod)


def _moe(hp_p, hp_s, x1_p, x1_s, mod_p, mod_s, layer, ng, tps_p, w_router, b_router,
         w_gate, w_up, w_down, final):
    n_p = hp_p.shape[0]
    n_tok = n_p + hp_s.shape[0]
    max_rows = TOP_K * n_tok + N_EXPERTS * (MOE_TILE - 1)
    n_rows_sorted = -(-max_rows // MOE_TILE) * MOE_TILE
    pos, rw, tab = _router(hp_p, hp_s, w_router, b_router)
    xs = _dispatch(pos, tab, hp_p, hp_s, n_rows_sorted)
    ys = _experts(tab, xs, w_gate, w_up, w_down, layer)
    out_p = _combine(pos, ys, x1_p, rw, 0, mod_p, layer, ng, tps_p, final)
    out_s = _combine(pos, ys, x1_s, rw, n_p, mod_s, layer, ng, 1, final)
    return out_p, out_s


def _gmlp_kernel(h_ref, w_ref, lg_ref, lb_ref, ws_ref, bs_ref, yc_ref, *rest, ell, blk, emit_v):
    if emit_v:
        gv_ref, wbf, wsbf = rest
    else:
        wbf, wsbf = rest
    i = pl.program_id(0)
    rows = h_ref.shape[0]
    c = GM_WIDTH

    @pl.when(i == 0)
    def _():
        wbf[...] = w_ref[...].astype(BF16)
        r = lax.broadcasted_iota(I32, (ell, ell), 0)
        s = lax.broadcasted_iota(I32, (ell, ell), 1)
        keep = (r >= s) & ((r // blk) == (s // blk))
        rsel = (lax.broadcasted_iota(I32, (ell, CHUNK), 0) % blk
                == lax.broadcasted_iota(I32, (ell, CHUNK), 1)).astype(BF16)
        csel = (lax.broadcasted_iota(I32, (CHUNK, ell), 1) % blk
                == lax.broadcasted_iota(I32, (CHUNK, ell), 0)).astype(BF16)
        for g in range(GM_GROUPS):
            wchunk = ws_ref[g].astype(BF16)
            if blk == ell:
                full = wchunk
            else:
                rowsp = jnp.dot(rsel, wchunk, preferred_element_type=F32).astype(BF16)
                full = jnp.dot(rowsp, csel, preferred_element_type=F32).astype(BF16)
            wsbf[g] = jnp.where(keep, full, jnp.zeros_like(full))

    uv = jnp.dot(h_ref[...], wbf[...], preferred_element_type=F32)
    u = uv[:, :c]
    v = uv[:, c:]
    vc = v - jnp.mean(v, axis=-1, keepdims=True)
    vn = vc * lax.rsqrt(jnp.mean(vc * vc, axis=-1, keepdims=True) + EPS) * lg_ref[...] + lb_ref[...]
    if emit_v:
        gv_ref[...] = vn
    vb = vn.astype(BF16)
    bs = bs_ref[...]
    for ch in range(rows // ell):
        rs = slice(ch * ell, (ch + 1) * ell)
        outs = []
        for g in range(GM_GROUPS):
            cs = slice(g * GM_GROUP, (g + 1) * GM_GROUP)
            mixed = jnp.dot(wsbf[g], vb[rs, cs], preferred_element_type=F32)
            mixed = (mixed.reshape(ell // blk, blk, GM_GROUP) + bs[:blk, g:g + 1][None]).reshape(ell, GM_GROUP)
            outs.append(u[rs, cs] * mixed)
        yc_ref[rs, :] = jnp.concatenate(outs, axis=-1).astype(BF16)


def _gmlp_mixer(h2, w_in, ln_g, ln_b, ws, bs_t, ell, blk, emit_v):
    rows, d = h2.shape
    c = GM_WIDTH
    kern = functools.partial(_gmlp_kernel, ell=ell, blk=blk, emit_v=emit_v)
    out_specs = [pl.BlockSpec((ROW_TILE, c), lambda i: (i, 0))]
    out_shape = [jax.ShapeDtypeStruct((rows, c), BF16)]
    if emit_v:
        out_specs.append(pl.BlockSpec((ROW_TILE, c), lambda i: (i, 0)))
        out_shape.append(jax.ShapeDtypeStruct((rows, c), F32))
    return pl.pallas_call(
        kern,
        grid=(rows // ROW_TILE,),
        in_specs=[pl.BlockSpec((ROW_TILE, d), lambda i: (i, 0)),
                  pl.BlockSpec((d, 2 * c), lambda i: (0, 0)),
                  pl.BlockSpec((1, c), lambda i: (0, 0)),
                  pl.BlockSpec((1, c), lambda i: (0, 0)),
                  pl.BlockSpec((GM_GROUPS, CHUNK, CHUNK), lambda i: (0, 0, 0)),
                  pl.BlockSpec((CHUNK, GM_GROUPS), lambda i: (0, 0))],
        out_specs=out_specs,
        out_shape=out_shape,
        scratch_shapes=[pltpu.VMEM((d, 2 * c), BF16), pltpu.VMEM((GM_GROUPS, ell, ell), BF16)],
        compiler_params=_cparams(1),
        name="gmlp_mixer",
    )(h2, w_in, ln_g.reshape(1, c), ln_b.reshape(1, c), ws, bs_t)


def _swa_kernel(h_ref, wq_ref, wkv_ref, kp_ref, vp_ref, bias_ref, sink_ref, yd_ref, k_ref, v_ref,
                wbf, kprev, vprev, *, tq, blocks_per_seq, has_cache):
    i = pl.program_id(0)
    rows = h_ref.shape[0]
    nq = N_HEADS * HEAD_DIM
    nkv = N_KV * HEAD_DIM
    n_blocks = rows // tq
    pair_w = 2 * HEAD_DIM
    pairs_per_kv = N_HEADS // N_KV // 2

    @pl.when(i == 0)
    def _():
        wbf[:, :nq] = wq_ref[...].astype(BF16)
        wbf[:, nq:] = wkv_ref[...].astype(BF16)
        kprev[...] = jnp.zeros_like(kprev)
        vprev[...] = jnp.zeros_like(vprev)

    qkv = jnp.dot(h_ref[...], wbf[...], preferred_element_type=F32)
    k_new = qkv[:, nq:nq + nkv]
    v_new = qkv[:, nq + nkv:]
    k_ref[...] = k_new
    v_ref[...] = v_new
    lane = lax.broadcasted_iota(I32, (2 * WINDOW, nkv), 1)
    lo_half = lane < HEAD_DIM
    col = lax.broadcasted_iota(I32, (tq, 2 * WINDOW), 1)
    nt_dims = (((1,), (1,)), ((), ()))

    def block_diag(a, a_swapped, hk):
        lo, hi = (a, a_swapped) if hk == 0 else (a_swapped, a)
        return jnp.concatenate([jnp.where(lo_half, lo, 0.0), jnp.where(lo_half, 0.0, hi)],
                               axis=0).astype(BF16)

    scores, vbds = [], []
    for blk in range(n_blocks):
        rs = slice(blk * tq, (blk + 1) * tq)
        if has_cache:
            k_old = kp_ref[blk]
            v_old = vp_ref[blk]
            first = None
        else:
            first = (i * n_blocks + blk) % blocks_per_seq == 0
            k_old = kprev[...]
            v_old = vprev[...]
        if tq < WINDOW:
            pad = jnp.zeros((WINDOW - tq, nkv), F32)
            k_cur = jnp.concatenate([k_new[rs], pad], axis=0)
            v_cur = jnp.concatenate([v_new[rs], pad], axis=0)
        else:
            k_cur = k_new[rs]
            v_cur = v_new[rs]
        if not has_cache:
            kprev[...] = k_cur
            vprev[...] = v_cur
        kcat = jnp.concatenate([k_old, k_cur], axis=0)
        vcat = jnp.concatenate([v_old, v_cur], axis=0)
        kswap = pltpu.roll(kcat, HEAD_DIM, 1)
        vswap = pltpu.roll(vcat, HEAD_DIM, 1)
        per_head = []
        for hk in range(N_KV):
            kbd = block_diag(kcat, kswap, hk)
            vbds.append(block_diag(vcat, vswap, hk))
            p0 = hk * pairs_per_kv
            qs = jnp.concatenate([qkv[rs, (p0 + pp) * pair_w:(p0 + pp + 1) * pair_w]
                                  for pp in range(pairs_per_kv)], axis=0).astype(BF16)
            s4 = lax.dot_general(qs, kbd, nt_dims, preferred_element_type=F32) * (HEAD_DIM ** -0.5)
            for pp in range(pairs_per_kv):
                for sub in range(2):
                    s = s4[pp * tq:(pp + 1) * tq, sub * 2 * WINDOW:(sub + 1) * 2 * WINDOW]
                    if first is not None:
                        s = jnp.where(first & (col < WINDOW), NEG_INF, s)
                    per_head.append(s)
        scores.append(jnp.concatenate(per_head, axis=0))

    s_all = jnp.stack(scores, axis=0) + bias_ref[...][None]
    sink = sink_ref[...][None]
    m = jnp.maximum(jnp.max(s_all, axis=-1, keepdims=True), sink)
    pr = jnp.exp(s_all - m)
    pr = pr / (jnp.sum(pr, axis=-1, keepdims=True) + jnp.exp(sink - m))

    for blk in range(n_blocks):
        outs = []
        for hk in range(N_KV):
            p4 = []
            for pp in range(pairs_per_kv):
                h0 = 2 * (hk * pairs_per_kv + pp)
                p4.append(jnp.concatenate([pr[blk, h0 * tq:(h0 + 1) * tq, :],
                                           pr[blk, (h0 + 1) * tq:(h0 + 2) * tq, :]], axis=-1))
            o4 = jnp.dot(jnp.concatenate(p4, axis=0).astype(BF16), vbds[blk * N_KV + hk],
                         preferred_element_type=F32)
            outs.extend(o4[pp * tq:(pp + 1) * tq, :] for pp in range(pairs_per_kv))
        yd_ref[blk * tq:(blk + 1) * tq, :] = jnp.concatenate(outs, axis=-1).astype(BF16)


def _swa_mixer(h2, w_in, k_cache, v_cache, bias, sinks, tq, blocks_per_seq):
    rows, d = h2.shape
    nq = N_HEADS * HEAD_DIM
    nkv = N_KV * HEAD_DIM
    nw = nq + 2 * nkv
    q_blk = (w_in.shape[1] - nw) // nq
    kv_blk = (w_in.shape[1] - 2 * nkv) // (2 * nkv)
    assert q_blk * nq + nw == w_in.shape[1] and kv_blk * 2 * nkv + 2 * nkv == w_in.shape[1]
    has_cache = k_cache is not None
    n_blocks = ROW_TILE // tq
    if not has_cache:
        k_cache = jnp.zeros((n_blocks, WINDOW, nkv), F32)
        v_cache = k_cache
    kern = functools.partial(_swa_kernel, tq=tq, blocks_per_seq=blocks_per_seq, has_cache=has_cache)
    cache_spec = pl.BlockSpec((n_blocks, WINDOW, nkv), lambda i: (i if has_cache else 0, 0, 0))
    return pl.pallas_call(
        kern,
        grid=(rows // ROW_TILE,),
        in_specs=[pl.BlockSpec((ROW_TILE, d), lambda i: (i, 0)),
                  pl.BlockSpec((d, nq), lambda i: (0, q_blk)),
                  pl.BlockSpec((d, 2 * nkv), lambda i: (0, kv_blk)),
                  cache_spec, cache_spec,
                  pl.BlockSpec((N_HEADS * tq, 2 * WINDOW), lambda i: (0, 0)),
                  pl.BlockSpec((N_HEADS * tq, 1), lambda i: (0, 0))],
        out_specs=[pl.BlockSpec((ROW_TILE, nq), lambda i: (i, 0)),
                   pl.BlockSpec((ROW_TILE, nkv), lambda i: (i, 0)),
                   pl.BlockSpec((ROW_TILE, nkv), lambda i: (i, 0))],
        out_shape=[jax.ShapeDtypeStruct((rows, nq), BF16),
                   jax.ShapeDtypeStruct((rows, nkv), F32),
                   jax.ShapeDtypeStruct((rows, nkv), F32)],
        scratch_shapes=[pltpu.VMEM((d, nw), BF16), pltpu.VMEM((WINDOW, nkv), F32),
                        pltpu.VMEM((WINDOW, nkv), F32)],
        compiler_params=_cparams(1),
        name="swa_mixer",
    )(h2, w_in, w_in, k_cache, v_cache, bias, sinks)


def _t5_bucket(dist):
    max_exact = N_BUCKETS // 2
    dd = np.maximum(dist, 1)
    large = max_exact + (np.log(dd / max_exact) / np.log(WINDOW / max_exact)
                         * (N_BUCKETS - max_exact)).astype(np.int64)
    large = np.minimum(large, N_BUCKETS - 1)
    return np.where(dist < max_exact, dist, large).astype(np.int32)


def _attention_bias(rel_bias):
    by_dist = jnp.take(rel_bias.astype(F32), _t5_bucket(np.arange(WINDOW)), axis=0).T
    neg = jnp.full((N_HEADS, WINDOW), NEG_INF, F32)
    line = jnp.concatenate([neg, by_dist[:, ::-1], neg[:, :WINDOW - 1]], axis=1)
    return jnp.stack([line[:, WINDOW - 1 - q:3 * WINDOW - 1 - q] for q in range(WINDOW)], axis=1)


def kernel(x_prompt, x_sample, state_pool, state_conv, cache_swa_k, cache_swa_v, c_prompt, c_sample, w_ada, b_ada, norm_g, final_norm_g, w_in_even, w_out_even, w_pool, pool_scale, conv_w, w_in_odd, w_out_odd, gm_norm_g, gm_norm_b, gm_w_s, gm_b_s, attn_sinks, rel_bias, w_router, b_router, w_gate, w_up, w_down):
    d = D_MODEL
    bp, tp, _ = x_prompt.shape
    bs, ts, _ = x_sample.shape
    rows_s = bs * ts
    assert rows_s == ROW_TILE and tp % ROUTER_TILE == 0 and PAST_LEN % CHUNK == 0
    assert bp <= V7X_SUBLANES and CHUNK % ts == 0

    n_c = bp + bs
    c_pad = (-n_c) % V7X_SUBLANES
    c_all = jnp.concatenate([c_prompt, c_sample, jnp.zeros((c_pad, d), F32)], axis=0)
    mod_p = _adaln(c_all, w_ada, b_ada)
    mod_s = jnp.repeat(mod_p[:, bp:bp + bs], ts, axis=1)

    tps_p = tp // ROW_TILE
    xp = x_prompt.reshape(bp * tp, d)
    xs_ = x_sample.reshape(rows_s, d)
    w_in0, w_in1 = w_in_even[0], w_in_odd[0]

    hp0 = _norm_mod(xp, norm_g[0, 0], mod_p, 0, tps_p)
    hs0 = _norm_mod(xs_, norm_g[0, 0], mod_s, 0, 1)
    zero_pool = jnp.zeros((bp, POOL_STATE, POOL_WIDTH), F32)
    zero_conv = jnp.zeros((bp, CONV_K - 1, CONV_WIDTH), F32)
    ya_p, pool_p = _pool_mixer(hp0, w_in0, w_pool[0], pool_scale[0], zero_pool, 1, ROW_TILE, 0)
    ya_s, pool_s = _pool_mixer(hs0, w_in0, w_pool[0], pool_scale[0], state_pool[0], bs, ts, PAST_LEN)
    yb_p, conv_p = _conv_mixer(hp0, w_in0, conv_w[0], zero_conv, 1, ROW_TILE)
    yb_s, conv_s = _conv_mixer(hs0, w_in0, conv_w[0], state_conv[0], bs, ts)
    x1p, hpp = _outproj(ya_p, yb_p, xp, mod_p, 0, norm_g[0, 1], w_out_even[0], tps_p)
    x1s, hps = _outproj(ya_s, yb_s, xs_, mod_s, 0, norm_g[0, 1], w_out_even[0], 1)
    (x2p, h1p), (x2s, h1s) = _moe(hpp, hps, x1p, x1s, mod_p, mod_s, 0, norm_g[1, 0], tps_p,
                                  w_router, b_router, w_gate, w_up, w_down, final=False)

    bs_t = gm_b_s[0].T
    (yc_p,) = _gmlp_mixer(h1p, w_in1, gm_norm_g[0], gm_norm_b[0], gm_w_s[0], bs_t, CHUNK, CHUNK, False)
    yc_s, gv_s = _gmlp_mixer(h1s, w_in1, gm_norm_g[0], gm_norm_b[0], gm_w_s[0], bs_t, rows_s, ts, True)
    bias = _attention_bias(rel_bias)
    nkv = N_KV * HEAD_DIM
    yd_p, k_p, v_p = _swa_mixer(h1p, w_in1, None, None, bias.reshape(N_HEADS * WINDOW, 2 * WINDOW),
                                jnp.repeat(attn_sinks[0], WINDOW).reshape(-1, 1), WINDOW, tp // WINDOW)
    yd_s, k_s, v_s = _swa_mixer(h1s, w_in1, cache_swa_k[0].reshape(bs, WINDOW, nkv),
                                cache_swa_v[0].reshape(bs, WINDOW, nkv),
                                bias[:, :ts, :].reshape(N_HEADS * ts, 2 * WINDOW),
                                jnp.repeat(attn_sinks[0], ts).reshape(-1, 1), ts, 1)
    x1p, hpp = _outproj(yc_p, yd_p, x2p, mod_p, 1, norm_g[1, 1], w_out_odd[0], tps_p)
    x1s, hps = _outproj(yc_s, yd_s, x2s, mod_s, 1, norm_g[1, 1], w_out_odd[0], 1)
    (yp,), (ys_out,) = _moe(hpp, hps, x1p, x1s, mod_p, mod_s, 1, final_norm_g, tps_p,
                            w_router, b_router, w_gate, w_up, w_down, final=True)

    k_p4 = k_p.reshape(bp, tp, N_KV, HEAD_DIM)[:, -WINDOW:]
    v_p4 = v_p.reshape(bp, tp, N_KV, HEAD_DIM)[:, -WINDOW:]
    k_s4 = jnp.concatenate([cache_swa_k[0], k_s.reshape(bs, ts, N_KV, HEAD_DIM)], axis=1)[:, -WINDOW:]
    v_s4 = jnp.concatenate([cache_swa_v[0], v_s.reshape(bs, ts, N_KV, HEAD_DIM)], axis=1)[:, -WINDOW:]
    return (yp.reshape(bp, tp, d), ys_out.reshape(bs, ts, d),
            pool_p[None], pool_s[None], conv_p[None], conv_s[None],
            k_p4[None], k_s4[None], v_p4[None], v_s4[None],
            gv_s.reshape(bs, ts, GM_WIDTH)[None])
```

```python
import functools

import numpy as np
import jax
import jax.numpy as jnp
from jax import lax
from jax.experimental import pallas as pl
from jax.experimental.pallas import tpu as pltpu

F32 = jnp.float32
BF16 = jnp.bfloat16
I32 = jnp.int32
U32 = jnp.uint32

D_MODEL = 2048
POOL_WINDOWS = (2, 4, 8, 16)
POOL_WIDTH = 1024
POOL_GROUP = 256
POOL_STATE = 15
CONV_WIDTH = 1024
CONV_K = 3
GM_WIDTH = 1024
GM_GROUPS = 8
GM_GROUP = 128
CHUNK = 128
HEAD_DIM = 64
N_HEADS = 16
N_KV = 2
WINDOW = 128
N_BUCKETS = 32
N_EXPERTS = 16
N_EXPERT_GROUPS = 4
EXP_PER_GROUP = 4
TOP_K = 2
EPS = 1e-6
NEG_INF = -1e30
PAST_LEN = 16384

V7X_SUBLANES = 8
V7X_LANES = 128
VMEM_LIMIT = 56 * 1024 * 1024

ROW_TILE = 256
ROUTER_TILE = 1024
POOL_HALO = 16
CONV_HALO = 8
MOE_TILE = 256
TAB_EXPERT, TAB_VALID, TAB_LAST_TILE, TAB_NUSED, TAB_NEXT = 0, 1, 2, 3, 4


def _cparams(n_axes):
    return pltpu.CompilerParams(dimension_semantics=("arbitrary",) * n_axes,
                                vmem_limit_bytes=VMEM_LIMIT)


def _rms(x, g):
    return x * lax.rsqrt(jnp.mean(x * x, axis=-1, keepdims=True) + EPS) * g


def _mod_spec(mod, layer, part):
    nrow = ROW_TILE if mod.shape[1] == ROW_TILE else V7X_SUBLANES
    return pl.BlockSpec((1, nrow, D_MODEL), lambda *_: (layer, 0, part))


def _mod_rows(m_ref, seq):
    if m_ref.shape[1] == V7X_SUBLANES:
        return m_ref[0, pl.ds(seq, 1), :]
    return m_ref[0]


def _adaln_kernel(c_ref, w_ref, b_ref, o_ref):
    c = c_ref[...]
    a = (c * jax.nn.sigmoid(c)).astype(BF16)
    o_ref[0] = jnp.dot(a, w_ref[0].astype(BF16), preferred_element_type=F32) + b_ref[0]


def _adaln(c_all, w_ada, b_ada):
    depth, d, n6 = w_ada.shape
    m = c_all.shape[0]
    tn = 1024
    return pl.pallas_call(
        _adaln_kernel,
        grid=(depth, n6 // tn),
        in_specs=[pl.BlockSpec((m, d), lambda l, j: (0, 0)),
                  pl.BlockSpec((1, d, tn), lambda l, j: (l, 0, j)),
                  pl.BlockSpec((1, 1, tn), lambda l, j: (l, 0, j))],
        out_specs=pl.BlockSpec((1, m, tn), lambda l, j: (l, 0, j)),
        out_shape=jax.ShapeDtypeStruct((depth, m, n6), F32),
        compiler_params=_cparams(2),
        name="adaln",
    )(c_all, w_ada, b_ada.reshape(depth, 1, n6))


def _norm_mod_kernel(x_ref, g_ref, sc_ref, sh_ref, h_ref, *, tiles_per_seq):
    seq = pl.program_id(0) // tiles_per_seq
    h = _rms(x_ref[...], g_ref[...]) * (1.0 + _mod_rows(sc_ref, seq)) + _mod_rows(sh_ref, seq)
    h_ref[...] = h.astype(BF16)


def _norm_mod(x2, g, mod, layer, tiles_per_seq):
    rows, d = x2.shape
    return pl.pallas_call(
        functools.partial(_norm_mod_kernel, tiles_per_seq=tiles_per_seq),
        grid=(rows // ROW_TILE,),
        in_specs=[pl.BlockSpec((ROW_TILE, d), lambda i: (i, 0)),
                  pl.BlockSpec((1, d), lambda i: (0, 0)),
                  _mod_spec(mod, layer, 1), _mod_spec(mod, layer, 0)],
        out_specs=pl.BlockSpec((ROW_TILE, d), lambda i: (i, 0)),
        out_shape=jax.ShapeDtypeStruct((rows, d), BF16),
        compiler_params=_cparams(1),
        name="norm_mod",
    )(x2, g.reshape(1, d), mod, mod)


def _pool_kernel(h_ref, w_ref, wp_ref, ps_ref, st_ref, ya_ref, ns_ref, wbf, wpbf, carry,
                 *, nb, tm, tiles_per_seq, start):
    i = pl.program_id(0)
    t = i % tiles_per_seq
    c = POOL_WIDTH
    halo = POOL_HALO

    @pl.when(i == 0)
    def _():
        wbf[...] = w_ref[...].astype(BF16)
        wpbf[...] = wp_ref[...].astype(BF16)

    @pl.when(t == 0)
    def _():
        carry[...] = st_ref[...]

    p = jnp.dot(h_ref[...], wbf[...], preferred_element_type=F32)
    p3 = p.reshape(nb, tm, c)
    ext3 = jnp.concatenate([carry[...], p3], axis=1)
    tail = ext3[:, tm:tm + halo, :]
    ns_ref[...] = tail
    carry[...] = tail
    ext = ext3.reshape(nb * (halo + tm), c)
    pos = start + t * tm + lax.broadcasted_iota(I32, (1, tm, 1), 1)
    outs = []
    for gi, w in enumerate(POOL_WINDOWS):
        sl = slice(gi * POOL_GROUP, (gi + 1) * POOL_GROUP)
        acc = ext[:, sl]
        shift = 1
        while shift < w:
            acc = acc + pltpu.roll(acc, shift, 0)
            shift *= 2
        win = acc.reshape(nb, halo + tm, POOL_GROUP)[:, halo:, :]
        cnt = jnp.minimum(pos + 1, w).astype(F32)
        dgrp = win / cnt - p3[:, :, sl]
        outs.append(jnp.dot(dgrp.reshape(nb * tm, POOL_GROUP).astype(BF16), wpbf[gi],
                            preferred_element_type=F32))
    y = jnp.concatenate(outs, axis=-1) * ps_ref[...]
    ya_ref[...] = y.astype(BF16)


def _pool_mixer(h2, w_in, w_pool, pool_scale, state, nb, tm, start):
    rows, d = h2.shape
    nseq = state.shape[0]
    tiles_per_seq = (rows // nseq) // tm
    seq_blocks = nseq // nb
    c = POOL_WIDTH
    st = jnp.pad(state, ((0, 0), (POOL_HALO - POOL_STATE, 0), (0, 0)))
    kern = functools.partial(_pool_kernel, nb=nb, tm=tm, tiles_per_seq=tiles_per_seq, start=start)
    ya, ns = pl.pallas_call(
        kern,
        grid=(seq_blocks * tiles_per_seq,),
        in_specs=[pl.BlockSpec((nb * tm, d), lambda i: (i, 0)),
                  pl.BlockSpec((d, c), lambda i: (0, 0)),
                  pl.BlockSpec((len(POOL_WINDOWS), POOL_GROUP, POOL_GROUP), lambda i: (0, 0, 0)),
                  pl.BlockSpec((1, c), lambda i: (0, 0)),
                  pl.BlockSpec((nb, POOL_HALO, c), lambda i: (i // tiles_per_seq, 0, 0))],
        out_specs=[pl.BlockSpec((nb * tm, c), lambda i: (i, 0)),
                   pl.BlockSpec((nb, POOL_HALO, c), lambda i: (i // tiles_per_seq, 0, 0))],
        out_shape=[jax.ShapeDtypeStruct((rows, c), BF16),
                   jax.ShapeDtypeStruct((nseq, POOL_HALO, c), F32)],
        scratch_shapes=[pltpu.VMEM((d, c), BF16),
                        pltpu.VMEM((len(POOL_WINDOWS), POOL_GROUP, POOL_GROUP), BF16),
                        pltpu.VMEM((nb, POOL_HALO, c), F32)],
        compiler_params=_cparams(1),
        name="pool_mixer",
    )(h2, w_in, w_pool, pool_scale.reshape(1, c), st)
    return ya, ns[:, POOL_HALO - POOL_STATE:, :]


def _conv_kernel(h_ref, wx_ref, wb_ref, wc_ref, cw_ref, st_ref, yb_ref, ns_ref,
                 wxbf, wbbf, wcbf, carry, *, nb, tm, tiles_per_seq):
    i = pl.program_id(1)
    t = i % tiles_per_seq
    tc = wxbf.shape[1]
    halo = CONV_HALO

    @pl.when(i == 0)
    def _():
        wxbf[...] = wx_ref[...].astype(BF16)
        wbbf[...] = wb_ref[...].astype(BF16)
        wcbf[...] = wc_ref[...].astype(BF16)

    @pl.when(t == 0)
    def _():
        carry[...] = st_ref[...]

    h = h_ref[...]
    xin = jnp.dot(h, wxbf[...], preferred_element_type=F32)
    gb = jnp.dot(h, wbbf[...], preferred_element_type=F32)
    gc = jnp.dot(h, wcbf[...], preferred_element_type=F32)
    z3 = (gc * xin).reshape(nb, tm, tc)
    ext3 = jnp.concatenate([carry[...], z3], axis=1)
    tail = ext3[:, tm:tm + halo, :]
    ns_ref[...] = tail
    carry[...] = tail
    ext = ext3.reshape(nb * (halo + tm), tc)
    cw = cw_ref[...]
    conv = cw[0:1, :] * pltpu.roll(ext, 2, 0) + cw[1:2, :] * pltpu.roll(ext, 1, 0) + cw[2:3, :] * ext
    conv = conv.reshape(nb, halo + tm, tc)[:, halo:, :].reshape(nb * tm, tc)
    yb_ref[...] = (gb * conv).astype(BF16)


def _conv_mixer(h2, w_in, conv_w, state, nb, tm):
    rows, d = h2.shape
    nseq = state.shape[0]
    tiles_per_seq = (rows // nseq) // tm
    seq_blocks = nseq // nb
    c = CONV_WIDTH
    tc = 512
    cb = c // tc
    base = POOL_WIDTH // tc
    st = jnp.pad(state, ((0, 0), (CONV_HALO - (CONV_K - 1), 0), (0, 0)))
    kern = functools.partial(_conv_kernel, nb=nb, tm=tm, tiles_per_seq=tiles_per_seq)
    yb, ns = pl.pallas_call(
        kern,
        grid=(cb, seq_blocks * tiles_per_seq),
        in_specs=[pl.BlockSpec((nb * tm, d), lambda j, i: (i, 0)),
                  pl.BlockSpec((d, tc), lambda j, i: (0, base + j)),
                  pl.BlockSpec((d, tc), lambda j, i: (0, base + cb + j)),
                  pl.BlockSpec((d, tc), lambda j, i: (0, base + 2 * cb + j)),
                  pl.BlockSpec((CONV_K, tc), lambda j, i: (0, j)),
                  pl.BlockSpec((nb, CONV_HALO, tc), lambda j, i: (i // tiles_per_seq, 0, j))],
        out_specs=[pl.BlockSpec((nb * tm, tc), lambda j, i: (i, j)),
                   pl.BlockSpec((nb, CONV_HALO, tc), lambda j, i: (i // tiles_per_seq, 0, j))],
        out_shape=[jax.ShapeDtypeStruct((rows, c), BF16),
                   jax.ShapeDtypeStruct((nseq, CONV_HALO, c), F32)],
        scratch_shapes=[pltpu.VMEM((d, tc), BF16)] * 3 + [pltpu.VMEM((nb, CONV_HALO, tc), F32)],
        compiler_params=_cparams(2),
        name="conv_mixer",
    )(h2, w_in, w_in, w_in, conv_w, st)
    return yb, ns[:, CONV_HALO - (CONV_K - 1):, :]


def _pack_bf16_pairs(v):
    c = v.shape[1] // 2
    r = v.astype(BF16).astype(F32)
    lo = pltpu.bitcast(r[:, :c], U32)
    hi = pltpu.bitcast(r[:, c:], U32)
    return (hi & jnp.uint32(0xFFFF0000)) | (lo >> 16)


def _unpack_bf16_pairs(w):
    lo = pltpu.bitcast(w << 16, F32).astype(BF16)
    hi = pltpu.bitcast(w & jnp.uint32(0xFFFF0000), F32).astype(BF16)
    return lo, hi


def _outproj_kernel(ya_ref, yb_ref, x_ref, g1_ref, sc_ref, sh_ref, ng_ref, wo_ref,
                    x1_ref, hp_ref, wobf, *, tiles_per_seq):
    i = pl.program_id(0)
    seq = i // tiles_per_seq
    half = ya_ref.shape[1]

    @pl.when(i == 0)
    def _():
        wobf[...] = wo_ref[...].astype(BF16)

    y = (jnp.dot(ya_ref[...], wobf[:half, :], preferred_element_type=F32)
         + jnp.dot(yb_ref[...], wobf[half:, :], preferred_element_type=F32))
    x1 = x_ref[...] + _mod_rows(g1_ref, seq) * y
    x1_ref[...] = x1
    h2 = _rms(x1, ng_ref[...]) * (1.0 + _mod_rows(sc_ref, seq)) + _mod_rows(sh_ref, seq)
    hp_ref[...] = _pack_bf16_pairs(h2)


def _outproj(ya, yb, x2, mod, layer, ng, w_out, tiles_per_seq):
    rows_all, d = x2.shape
    half = ya.shape[1]
    row_spec = lambda w: pl.BlockSpec((ROW_TILE, w), lambda i: (i, 0))
    return pl.pallas_call(
        functools.partial(_outproj_kernel, tiles_per_seq=tiles_per_seq),
        grid=(rows_all // ROW_TILE,),
        in_specs=[row_spec(half), row_spec(half), row_spec(d),
                  _mod_spec(mod, layer, 2), _mod_spec(mod, layer, 4), _mod_spec(mod, layer, 3),
                  pl.BlockSpec((1, d), lambda i: (0, 0)), pl.BlockSpec((d, d), lambda i: (0, 0))],
        out_specs=[row_spec(d), row_spec(d // 2)],
        out_shape=[jax.ShapeDtypeStruct((rows_all, d), F32),
                   jax.ShapeDtypeStruct((rows_all, d // 2), U32)],
        scratch_shapes=[pltpu.VMEM((d, d), BF16)],
        compiler_params=_cparams(1),
        name="outproj",
    )(ya, yb, x2, mod, mod, mod, ng.reshape(1, d), w_out)


def _router_kernel(hpp_ref, hps_ref, wr_ref, br_ref, pos_ref, rw_ref, tab_ref,
                   cnt_acc, totals, starts, padded, *, nt_p, rows_s):
    ph = pl.program_id(0)
    t = pl.program_id(1)
    last = nt_p
    r = hpp_ref.shape[0]
    half = hpp_ref.shape[1]
    ne = N_EXPERTS
    sub = lax.broadcasted_iota(I32, (ne, V7X_LANES), 0)

    @pl.when(t == 0)
    def _():
        cnt_acc[...] = jnp.zeros_like(cnt_acc)

    @pl.when((ph == 0) & (t == 0))
    def _():
        starts[...] = jnp.zeros_like(starts)
        padded[...] = jnp.zeros_like(padded)

    @pl.when((ph == 1) & (t == 0))
    def _():
        pad = jnp.floor((totals[...] + (MOE_TILE - 1.0)) * (1.0 / MOE_TILE)) * MOE_TILE
        run = pad
        k = 1
        while k < ne:
            run = run + jnp.where(sub >= k, pltpu.roll(run, k, 0), 0.0)
            k *= 2
        padded[...] = pad
        starts[...] = run - pad

    is_s = t == last
    w_s = jnp.concatenate([hps_ref[...], jnp.zeros((r - rows_s, half), U32)], axis=0)
    w = jnp.where(is_s, w_s, hpp_ref[...])
    lo, hi = _unpack_bf16_pairs(w)
    wr = wr_ref[...].astype(BF16)
    nt_dims = (((1,), (1,)), ((), ()))
    log_t = (lax.dot_general(wr[:, :half], lo, nt_dims, preferred_element_type=F32)
             + lax.dot_general(wr[:, half:], hi, nt_dims, preferred_element_type=F32))

    s = jax.nn.sigmoid(log_t)
    sg = s + br_ref[...]
    eid = lax.broadcasted_iota(I32, (ne, r), 0)
    within = eid % EXP_PER_GROUP
    grp = eid // EXP_PER_GROUP

    def group_rot(x, k):
        return jnp.where(within + k < EXP_PER_GROUP,
                         pltpu.roll(x, ne - k, 0), pltpu.roll(x, EXP_PER_GROUP - k, 0))

    rank = jnp.zeros((ne, r), I32)
    for k in range(1, EXP_PER_GROUP):
        mate = group_rot(sg, k)
        wrapped = within + k >= EXP_PER_GROUP
        ahead = (mate > sg) | (wrapped & (mate == sg))
        rank = rank + ahead.astype(I32)
    top2 = rank < TOP_K
    kept = jnp.where(top2, sg, 0.0)
    gscore = kept
    for k in range(1, EXP_PER_GROUP):
        gscore = gscore + group_rot(kept, k)
    win = None
    for k in range(1, N_EXPERT_GROUPS):
        other = pltpu.roll(gscore, EXP_PER_GROUP * k, 0)
        beats = (gscore > other) | ((grp < k) & (gscore == other))
        win = beats if win is None else (win & beats)
    n_valid = jnp.where(is_s, rows_s, r)
    tok = lax.broadcasted_iota(I32, (ne, r), 1)
    sel = top2 & win & (tok < n_valid)
    picked = jnp.where(sel, s, 0.0)
    wsum = jnp.sum(picked, axis=0, keepdims=True)
    gate = picked / jnp.where(tok[0:1, :] < n_valid, wsum, 1.0)

    src = lax.broadcasted_iota(I32, (r, r), 0)
    dst = lax.broadcasted_iota(I32, (r, r), 1)
    before = (src < dst).astype(BF16)
    selb = sel.astype(F32)
    ranks = jnp.dot(selb.astype(BF16), before, preferred_element_type=F32)
    slot = (starts[...][:, 0:1] + cnt_acc[...][:, 0:1] + ranks).astype(I32)
    cnt_new = cnt_acc[...] + jnp.sum(selb, axis=1, keepdims=True)
    cnt_acc[...] = cnt_new

    @pl.when((ph == 0) & (t == last))
    def _():
        totals[...] = cnt_new

    e_a = jnp.min(jnp.where(sel, eid, ne), axis=0, keepdims=True)
    e_b = jnp.max(jnp.where(sel, eid, -1), axis=0, keepdims=True)
    is_a = sel & (eid == e_a)
    is_b = sel & (eid == e_b)
    pos_a = jnp.sum(jnp.where(is_a, slot, 0), axis=0, keepdims=True)
    pos_b = jnp.sum(jnp.where(is_b, slot, 0), axis=0, keepdims=True)
    w_a = jnp.sum(jnp.where(is_a, gate, 0.0), axis=0, keepdims=True)
    w_b = jnp.sum(jnp.where(is_b, gate, 0.0), axis=0, keepdims=True)
    pos_ref[0] = jnp.concatenate([pos_a, pos_b], axis=0)
    wmat = jnp.concatenate([w_a, w_b, jnp.zeros((V7X_LANES - 2, r), F32)], axis=0)
    rw_ref[...] = wmat.T

    @pl.when((ph == 1) & (t == last))
    def _():
        ends = starts[...] + padded[...]
        lane = lax.broadcasted_iota(I32, (ne, V7X_LANES), 1)
        tile_start = (lane * MOE_TILE).astype(F32)
        te = jnp.sum((tile_start >= ends).astype(I32), axis=0, keepdims=True)
        valid = te < ne
        last_e = jnp.max(jnp.where(padded[...] > 0.0, sub, 0), axis=0, keepdims=True)
        te = jnp.where(valid, te, last_e)
        n_used = jnp.sum(valid.astype(I32), axis=1, keepdims=True) + jnp.zeros((1, V7X_LANES), I32)
        last_tile = jnp.where(padded[...] > 0.0, ends - MOE_TILE, -1.0).astype(I32)
        last_tile_row = jnp.sum(jnp.where(sub == lane, last_tile, 0), axis=0, keepdims=True)
        later = jnp.min(jnp.where((sub > te) & (padded[...] > 0.0), sub, ne), axis=0, keepdims=True)
        next_e = jnp.where(later < ne, later, -1)
        zero = jnp.zeros((1, V7X_LANES), I32)
        tab_ref[...] = jnp.concatenate([te, valid.astype(I32), last_tile_row, n_used, next_e,
                                        zero, zero, zero], axis=0)


def _router(hp_p, hp_s, w_router, b_router):
    n_p, half = hp_p.shape
    rows_s = hp_s.shape[0]
    r = ROUTER_TILE
    nt_p = n_p // r
    nt = nt_p + 1
    kern = functools.partial(_router_kernel, nt_p=nt_p, rows_s=rows_s)
    pos, rw, tab = pl.pallas_call(
        kern,
        grid=(2, nt),
        in_specs=[pl.BlockSpec((r, half), lambda p, t: (jnp.minimum(t, nt_p - 1), 0)),
                  pl.BlockSpec((rows_s, half), lambda p, t: (0, 0)),
                  pl.BlockSpec((N_EXPERTS, 2 * half), lambda p, t: (0, 0)),
                  pl.BlockSpec((N_EXPERTS, 1), lambda p, t: (0, 0))],
        out_specs=[pl.BlockSpec((1, TOP_K, r), lambda p, t: (p * t, 0, 0)),
                   pl.BlockSpec((r, V7X_LANES), lambda p, t: (p * t, 0)),
                   pl.BlockSpec((V7X_SUBLANES, V7X_LANES), lambda p, t: (0, 0))],
        out_shape=[jax.ShapeDtypeStruct((nt, TOP_K, r), I32),
                   jax.ShapeDtypeStruct((nt * r, V7X_LANES), F32),
                   jax.ShapeDtypeStruct((V7X_SUBLANES, V7X_LANES), I32)],
        scratch_shapes=[pltpu.VMEM((N_EXPERTS, V7X_LANES), F32)] * 4,
        compiler_params=_cparams(2),
        name="router",
    )(hp_p, hp_s, w_router.T, b_router.reshape(N_EXPERTS, 1))
    return pos.reshape(-1), rw, tab.reshape(-1)


def _pos_index(tok0):
    return (tok0 // ROUTER_TILE) * (TOP_K * ROUTER_TILE) + tok0 % ROUTER_TILE


def _dispatch_kernel(pos_ref, tab_ref, hpp_ref, hps_ref, xs_ref, zbuf, sem, *, n_p_steps):
    i = pl.program_id(0)
    rows = hpp_ref.shape[0]

    @pl.when(i == 0)
    def _():
        zbuf[...] = jnp.zeros_like(zbuf)

        def fill(e):
            first = pl.multiple_of(tab_ref[TAB_LAST_TILE * V7X_LANES + e], MOE_TILE)
            return pltpu.make_async_copy(zbuf, xs_ref.at[pl.ds(first, MOE_TILE), :], sem)

        for e in range(N_EXPERTS):
            @pl.when(tab_ref[TAB_LAST_TILE * V7X_LANES + e] >= 0)
            def _():
                fill(e).start()
        for e in range(N_EXPERTS):
            @pl.when(tab_ref[TAB_LAST_TILE * V7X_LANES + e] >= 0)
            def _():
                fill(e).wait()

        def tail(j):
            first = pl.multiple_of(j * MOE_TILE, MOE_TILE)
            return pltpu.make_async_copy(zbuf, xs_ref.at[pl.ds(first, MOE_TILE), :], sem)

        def tail_start(j, carry):
            tail(j).start()
            return carry

        def tail_wait(j, carry):
            tail(j).wait()
            return carry

        n_used = tab_ref[TAB_NUSED * V7X_LANES]
        n_tiles = xs_ref.shape[0] // MOE_TILE
        lax.fori_loop(n_used, n_tiles, tail_start, 0)
        lax.fori_loop(n_used, n_tiles, tail_wait, 0)

    base = _pos_index(i * rows)

    def scatter(src_ref):
        def row_copy(r, dst):
            return pltpu.make_async_copy(src_ref.at[pl.ds(r, 1), :], xs_ref.at[pl.ds(dst, 1), :], sem)

        def issue(r, carry):
            row_copy(r, pos_ref[base + r]).start()
            row_copy(r, pos_ref[base + ROUTER_TILE + r]).start(priority=1)
            return carry

        lax.fori_loop(0, rows, issue, 0, unroll=8)
        block = pltpu.make_async_copy(src_ref, xs_ref.at[pl.ds(0, rows), :], sem)
        for _ in range(TOP_K):
            block.wait()

    @pl.when(i < n_p_steps)
    def _():
        scatter(hpp_ref)

    @pl.when(i == n_p_steps)
    def _():
        scatter(hps_ref)


def _dispatch(pos, tab, hp_p, hp_s, n_rows_sorted):
    n_p, c = hp_p.shape
    n_p_steps = n_p // ROW_TILE
    assert hp_s.shape[0] == ROW_TILE
    kern = functools.partial(_dispatch_kernel, n_p_steps=n_p_steps)
    return pl.pallas_call(
        kern,
        grid_spec=pltpu.PrefetchScalarGridSpec(
            num_scalar_prefetch=2,
            grid=(n_p_steps + 1,),
            in_specs=[pl.BlockSpec((ROW_TILE, c), lambda i, p, t: (jnp.minimum(i, n_p_steps - 1), 0)),
                      pl.BlockSpec((ROW_TILE, c), lambda i, p, t: (0, 0))],
            out_specs=pl.BlockSpec(memory_space=pl.ANY),
            scratch_shapes=[pltpu.VMEM((MOE_TILE, c), U32), pltpu.SemaphoreType.DMA(())]),
        out_shape=jax.ShapeDtypeStruct((n_rows_sorted, c), U32),
        compiler_params=_cparams(1),
        name="moe_dispatch",
    )(pos, tab, hp_p, hp_s)


def _experts_kernel(tab_ref, xs_ref, wg_hbm, wu_hbm, wd_hbm, ys_ref,
                    wg32, wu32, wd32, wgbf, wubf, wdbf, slot_ref, sems, *, layer):
    i = pl.program_id(0)
    expert = tab_ref[TAB_EXPERT * V7X_LANES + i]
    prev = tab_ref[TAB_EXPERT * V7X_LANES + jnp.maximum(i - 1, 0)]
    upcoming = tab_ref[TAB_NEXT * V7X_LANES + i]
    changed = (i == 0) | (expert != prev)
    half = xs_ref.shape[1]

    def weight_copies(e, slot):
        return (pltpu.make_async_copy(wg_hbm.at[layer, e], wg32.at[slot], sems.at[0, slot]),
                pltpu.make_async_copy(wu_hbm.at[layer, e], wu32.at[slot], sems.at[1, slot]),
                pltpu.make_async_copy(wd_hbm.at[layer, e], wd32.at[slot], sems.at[2, slot]))

    @pl.when(i == 0)
    def _():
        slot_ref[0] = 0
        for cp in weight_copies(expert, 0):
            cp.start()

    @pl.when(changed & (i > 0))
    def _():
        slot_ref[0] = 1 - slot_ref[0]

    for slot in range(2):
        @pl.when(changed & (slot_ref[0] == slot))
        def _():
            for cp in weight_copies(expert, slot):
                cp.wait()
            wgbf[...] = wg32[slot].astype(BF16)
            wubf[...] = wu32[slot].astype(BF16)
            wdbf[...] = wd32[slot].astype(BF16)

            @pl.when(upcoming >= 0)
            def _():
                for cp in weight_copies(upcoming, 1 - slot):
                    cp.start(priority=1)

    @pl.when(tab_ref[TAB_VALID * V7X_LANES + i] > 0)
    def _():
        lo, hi = _unpack_bf16_pairs(xs_ref[...])
        a = (jnp.dot(lo, wgbf[:half, :], preferred_element_type=F32)
             + jnp.dot(hi, wgbf[half:, :], preferred_element_type=F32))
        b = (jnp.dot(lo, wubf[:half, :], preferred_element_type=F32)
             + jnp.dot(hi, wubf[half:, :], preferred_element_type=F32))
        hid = (a * jax.nn.sigmoid(a)) * b
        ys_ref[...] = jnp.dot(hid.astype(BF16), wdbf[...], preferred_element_type=F32)

    @pl.when(tab_ref[TAB_VALID * V7X_LANES + i] == 0)
    def _():
        ys_ref[...] = jnp.zeros_like(ys_ref)


def _experts(tab, xs, w_gate, w_up, w_down, layer):
    n_rows, half = xs.shape
    _, _, d, f = w_gate.shape
    nt = n_rows // MOE_TILE
    assert nt <= V7X_LANES

    def tile(i, tab_ref):
        return jnp.minimum(i, tab_ref[TAB_NUSED * V7X_LANES] - 1)

    hbm = pl.BlockSpec(memory_space=pl.ANY)
    return pl.pallas_call(
        functools.partial(_experts_kernel, layer=layer),
        grid_spec=pltpu.PrefetchScalarGridSpec(
            num_scalar_prefetch=1,
            grid=(nt,),
            in_specs=[pl.BlockSpec((MOE_TILE, half), lambda i, t: (tile(i, t), 0)), hbm, hbm, hbm],
            out_specs=pl.BlockSpec((MOE_TILE, d), lambda i, t: (i, 0)),
            scratch_shapes=[pltpu.VMEM((2, d, f), F32), pltpu.VMEM((2, d, f), F32), pltpu.VMEM((2, f, d), F32),
                            pltpu.VMEM((d, f), BF16), pltpu.VMEM((d, f), BF16), pltpu.VMEM((f, d), BF16),
                            pltpu.SMEM((1,), I32), pltpu.SemaphoreType.DMA((3, 2))]),
        out_shape=jax.ShapeDtypeStruct((n_rows, d), F32),
        compiler_params=_cparams(1),
        name="moe_experts",
    )(tab, xs, w_gate, w_up, w_down)


def _combine_kernel(pos_ref, ys_ref, x1_ref, rw_ref, g2_ref, ng_ref, sc_ref, sh_ref, *rest,
                    tok0, tiles_per_seq, n_steps, final):
    if final:
        x2_ref, buf, sems = rest
        hn_ref = None
    else:
        x2_ref, hn_ref, buf, sems = rest
    i = pl.program_id(0)
    seq = i // tiles_per_seq
    rows = x1_ref.shape[0]

    def gather(step, slot):
        base = _pos_index(tok0 + step * rows)

        def issue(r, carry):
            for k in range(TOP_K):
                src = pos_ref[base + k * ROUTER_TILE + r]
                pltpu.make_async_copy(ys_ref.at[pl.ds(src, 1), :], buf.at[slot, k, pl.ds(r, 1), :],
                                      sems.at[slot]).start(priority=k)
            return carry

        lax.fori_loop(0, rows, issue, 0, unroll=8)

    @pl.when(i == 0)
    def _():
        gather(0, 0)

    @pl.when(i + 1 < n_steps)
    def _():
        gather(i + 1, (i + 1) % 2)

    slot = i % 2
    for k in range(TOP_K):
        pltpu.make_async_copy(ys_ref.at[pl.ds(0, rows), :], buf.at[slot, k], sems.at[slot]).wait()

    rw = rw_ref[...]
    moe = rw[:, 0:1] * buf[slot, 0] + rw[:, 1:2] * buf[slot, 1]
    x2 = x1_ref[...] + _mod_rows(g2_ref, seq) * moe
    if final:
        x2_ref[...] = _rms(x2, ng_ref[...])
    else:
        x2_ref[...] = x2
        hn_ref[...] = (_rms(x2, ng_ref[...]) * (1.0 + _mod_rows(sc_ref, seq))
                       + _mod_rows(sh_ref, seq)).astype(BF16)


def _combine(pos, ys, x1, rw, tok0, mod, layer, ng, tiles_per_seq, final):
    n_tok, d = x1.shape
    kern = functools.partial(_combine_kernel, tok0=tok0, tiles_per_seq=tiles_per_seq,
                             n_steps=n_tok // ROW_TILE, final=final)
    rw_off = tok0 // ROW_TILE
    row_spec = lambda w: pl.BlockSpec((ROW_TILE, w), lambda i, p: (i, 0))
    out_shape = [jax.ShapeDtypeStruct((n_tok, d), F32)]
    out_specs = [row_spec(d)]
    if not final:
        out_shape.append(jax.ShapeDtypeStruct((n_tok, d), BF16))
        out_specs.append(row_spec(d))
    nxt = min(layer + 1, mod.shape[0] - 1)
    return pl.pallas_call(
        kern,
        grid_spec=pltpu.PrefetchScalarGridSpec(
            num_scalar_prefetch=1,
            grid=(n_tok // ROW_TILE,),
            in_specs=[pl.BlockSpec(memory_space=pl.ANY), row_spec(d),
                      pl.BlockSpec((ROW_TILE, V7X_LANES), lambda i, p: (rw_off + i, 0)),
                      _mod_spec(mod, layer, 5), pl.BlockSpec((1, d), lambda i, p: (0, 0)),
                      _mod_spec(mod, nxt, 1), _mod_spec(mod, nxt, 0)],
            out_specs=out_specs,
            scratch_shapes=[pltpu.VMEM((2, TOP_K, ROW_TILE, d), F32), pltpu.SemaphoreType.DMA((2,))]),
        out_shape=out_shape,
        compiler_params=_cparams(1),
        name="moe_combine_final" if final else "moe_combine",
    )(pos, ys, x1, rw, mod, ng.reshape(1, d), mod, mod)


def _moe(hp_p, hp_s, x1_p, x1_s, mod_p, mod_s, layer, ng, tps_p, w_router, b_router,
         w_gate, w_up, w_down, final):
    n_p = hp_p.shape[0]
    n_tok = n_p + hp_s.shape[0]
    max_rows = TOP_K * n_tok + N_EXPERTS * (MOE_TILE - 1)
    n_rows_sorted = -(-max_rows // MOE_TILE) * MOE_TILE
    pos, rw, tab = _router(hp_p, hp_s, w_router, b_router)
    xs = _dispatch(pos, tab, hp_p, hp_s, n_rows_sorted)
    ys = _experts(tab, xs, w_gate, w_up, w_down, layer)
    out_p = _combine(pos, ys, x1_p, rw, 0, mod_p, layer, ng, tps_p, final)
    out_s = _combine(pos, ys, x1_s, rw, n_p, mod_s, layer, ng, 1, final)
    return out_p, out_s


def _gmlp_kernel(h_ref, w_ref, lg_ref, lb_ref, ws_ref, bs_ref, yc_ref, *rest, ell, blk, emit_v):
    if emit_v:
        gv_ref, wbf, wsbf = rest
    else:
        wbf, wsbf = rest
    i = pl.program_id(0)
    rows = h_ref.shape[0]
    c = GM_WIDTH

    @pl.when(i == 0)
    def _():
        wbf[...] = w_ref[...].astype(BF16)
        r = lax.broadcasted_iota(I32, (ell, ell), 0)
        s = lax.broadcasted_iota(I32, (ell, ell), 1)
        keep = (r >= s) & ((r // blk) == (s // blk))
        rsel = (lax.broadcasted_iota(I32, (ell, CHUNK), 0) % blk
                == lax.broadcasted_iota(I32, (ell, CHUNK), 1)).astype(BF16)
        csel = (lax.broadcasted_iota(I32, (CHUNK, ell), 1) % blk
                == lax.broadcasted_iota(I32, (CHUNK, ell), 0)).astype(BF16)
        for g in range(GM_GROUPS):
            wchunk = ws_ref[g].astype(BF16)
            if blk == ell:
                full = wchunk
            else:
                rowsp = jnp.dot(rsel, wchunk, preferred_element_type=F32).astype(BF16)
                full = jnp.dot(rowsp, csel, preferred_element_type=F32).astype(BF16)
            wsbf[g] = jnp.where(keep, full, jnp.zeros_like(full))

    uv = jnp.dot(h_ref[...], wbf[...], preferred_element_type=F32)
    u = uv[:, :c]
    v = uv[:, c:]
    vc = v - jnp.mean(v, axis=-1, keepdims=True)
    vn = vc * lax.rsqrt(jnp.mean(vc * vc, axis=-1, keepdims=True) + EPS) * lg_ref[...] + lb_ref[...]
    if emit_v:
        gv_ref[...] = vn
    vb = vn.astype(BF16)
    bs = bs_ref[...]
    for ch in range(rows // ell):
        rs = slice(ch * ell, (ch + 1) * ell)
        outs = []
        for g in range(GM_GROUPS):
            cs = slice(g * GM_GROUP, (g + 1) * GM_GROUP)
            mixed = jnp.dot(wsbf[g], vb[rs, cs], preferred_element_type=F32)
            mixed = (mixed.reshape(ell // blk, blk, GM_GROUP) + bs[:blk, g:g + 1][None]).reshape(ell, GM_GROUP)
            outs.append(u[rs, cs] * mixed)
        yc_ref[rs, :] = jnp.concatenate(outs, axis=-1).astype(BF16)


def _gmlp_mixer(h2, w_in, ln_g, ln_b, ws, bs_t, ell, blk, emit_v):
    rows, d = h2.shape
    c = GM_WIDTH
    kern = functools.partial(_gmlp_kernel, ell=ell, blk=blk, emit_v=emit_v)
    out_specs = [pl.BlockSpec((ROW_TILE, c), lambda i: (i, 0))]
    out_shape = [jax.ShapeDtypeStruct((rows, c), BF16)]
    if emit_v:
        out_specs.append(pl.BlockSpec((ROW_TILE, c), lambda i: (i, 0)))
        out_shape.append(jax.ShapeDtypeStruct((rows, c), F32))
    return pl.pallas_call(
        kern,
        grid=(rows // ROW_TILE,),
        in_specs=[pl.BlockSpec((ROW_TILE, d), lambda i: (i, 0)),
                  pl.BlockSpec((d, 2 * c), lambda i: (0, 0)),
                  pl.BlockSpec((1, c), lambda i: (0, 0)),
                  pl.BlockSpec((1, c), lambda i: (0, 0)),
                  pl.BlockSpec((GM_GROUPS, CHUNK, CHUNK), lambda i: (0, 0, 0)),
                  pl.BlockSpec((CHUNK, GM_GROUPS), lambda i: (0, 0))],
        out_specs=out_specs,
        out_shape=out_shape,
        scratch_shapes=[pltpu.VMEM((d, 2 * c), BF16), pltpu.VMEM((GM_GROUPS, ell, ell), BF16)],
        compiler_params=_cparams(1),
        name="gmlp_mixer",
    )(h2, w_in, ln_g.reshape(1, c), ln_b.reshape(1, c), ws, bs_t)


PAIR_W = 2 * HEAD_DIM
PAIRS_PER_KV = N_HEADS // N_KV // 2
NT_DIMS = (((1,), (1,)), ((), ()))


def _swa_project(i, h_ref, wq_ref, wkv_ref, wbf):
    nq = N_HEADS * HEAD_DIM

    @pl.when(i == 0)
    def _():
        wbf[:, :nq] = wq_ref[...].astype(BF16)
        wbf[:, nq:] = wkv_ref[...].astype(BF16)

    return jnp.dot(h_ref[...], wbf[...], preferred_element_type=F32)


def _pair_block_diag(a, a_swapped, hk, axis):
    dim_axis = 1 - axis
    low = lax.broadcasted_iota(I32, a.shape, dim_axis) < HEAD_DIM
    lo, hi = (a, a_swapped) if hk == 0 else (a_swapped, a)
    return jnp.concatenate([jnp.where(low, lo, 0.0), jnp.where(low, 0.0, hi)], axis=axis).astype(BF16)


def _stack_pairs(qkv, rs, hk):
    p0 = hk * PAIRS_PER_KV
    return jnp.concatenate([qkv[rs, (p0 + pp) * PAIR_W:(p0 + pp + 1) * PAIR_W]
                            for pp in range(PAIRS_PER_KV)], axis=0).astype(BF16)


def _swa_cached_kernel(h_ref, wq_ref, wkv_ref, kp_ref, vp_ref, bias_ref, sink_ref, yd_ref, k_ref, v_ref,
                       wbf, *, tq):
    i = pl.program_id(0)
    rows = h_ref.shape[0]
    nq = N_HEADS * HEAD_DIM
    nkv = N_KV * HEAD_DIM
    n_blocks = rows // tq
    qkv = _swa_project(i, h_ref, wq_ref, wkv_ref, wbf)
    k_new = qkv[:, nq:nq + nkv]
    v_new = qkv[:, nq + nkv:]
    k_ref[...] = k_new
    v_ref[...] = v_new
    pad = jnp.zeros((WINDOW - tq, nkv), F32)

    scores, vbds = [], []
    for blk in range(n_blocks):
        rs = slice(blk * tq, (blk + 1) * tq)
        kcat = jnp.concatenate([kp_ref[blk], k_new[rs], pad], axis=0)
        vcat = jnp.concatenate([vp_ref[blk], v_new[rs], pad], axis=0)
        kswap = pltpu.roll(kcat, HEAD_DIM, 1)
        vswap = pltpu.roll(vcat, HEAD_DIM, 1)
        per_head = []
        for hk in range(N_KV):
            kbd = _pair_block_diag(kcat, kswap, hk, 0)
            vbds.append(_pair_block_diag(vcat, vswap, hk, 0))
            s4 = lax.dot_general(_stack_pairs(qkv, rs, hk), kbd, NT_DIMS,
                                 preferred_element_type=F32) * (HEAD_DIM ** -0.5)
            for pp in range(PAIRS_PER_KV):
                for sub in range(2):
                    per_head.append(s4[pp * tq:(pp + 1) * tq, sub * 2 * WINDOW:(sub + 1) * 2 * WINDOW])
        scores.append(jnp.concatenate(per_head, axis=0))

    s_all = jnp.stack(scores, axis=0) + bias_ref[...][None]
    sink = sink_ref[...][None]
    m = jnp.maximum(jnp.max(s_all, axis=-1, keepdims=True), sink)
    pr = jnp.exp(s_all - m)
    pr = pr / (jnp.sum(pr, axis=-1, keepdims=True) + jnp.exp(sink - m))

    for blk in range(n_blocks):
        outs = []
        for hk in range(N_KV):
            p4 = []
            for pp in range(PAIRS_PER_KV):
                h0 = 2 * (hk * PAIRS_PER_KV + pp)
                p4.append(jnp.concatenate([pr[blk, h0 * tq:(h0 + 1) * tq, :],
                                           pr[blk, (h0 + 1) * tq:(h0 + 2) * tq, :]], axis=-1))
            o4 = jnp.dot(jnp.concatenate(p4, axis=0).astype(BF16), vbds[blk * N_KV + hk],
                         preferred_element_type=F32)
            outs.extend(o4[pp * tq:(pp + 1) * tq, :] for pp in range(PAIRS_PER_KV))
        yd_ref[blk * tq:(blk + 1) * tq, :] = jnp.concatenate(outs, axis=-1).astype(BF16)


def _swa_stream_kernel(h_ref, wq_ref, wkv_ref, bias_ref, sink_ref, yd_ref, k_ref, v_ref,
                       wbf, kprev, vprev_t, *, blocks_per_seq):
    i = pl.program_id(0)
    rows = h_ref.shape[0]
    nq = N_HEADS * HEAD_DIM
    nkv = N_KV * HEAD_DIM
    tq = WINDOW
    n_blocks = rows // tq

    @pl.when(i == 0)
    def _():
        kprev[...] = jnp.zeros_like(kprev)
        vprev_t[...] = jnp.zeros_like(vprev_t)

    qkv = _swa_project(i, h_ref, wq_ref, wkv_ref, wbf)
    k_new = qkv[:, nq:nq + nkv]
    v_new = qkv[:, nq + nkv:]
    k_ref[...] = k_new
    v_ref[...] = v_new
    v_new_t = v_new.T
    key = lax.broadcasted_iota(I32, (2, 2 * WINDOW, PAIRS_PER_KV * tq), 1)

    for blk in range(n_blocks):
        rs = slice(blk * tq, (blk + 1) * tq)
        first = (i * n_blocks + blk) % blocks_per_seq == 0
        k_cur = k_new[rs]
        v_cur_t = v_new_t[:, rs]
        kcat = jnp.concatenate([kprev[...], k_cur], axis=0)
        vcat_t = jnp.concatenate([vprev_t[...], v_cur_t], axis=1)
        kprev[...] = k_cur
        vprev_t[...] = v_cur_t
        kswap = pltpu.roll(kcat, HEAD_DIM, 1)
        vswap_t = pltpu.roll(vcat_t, HEAD_DIM, 0)
        outs = []
        for hk in range(N_KV):
            kbd = _pair_block_diag(kcat, kswap, hk, 0)
            vbd_t = _pair_block_diag(vcat_t, vswap_t, hk, 1)
            st = lax.dot_general(kbd, _stack_pairs(qkv, rs, hk), NT_DIMS,
                                 preferred_element_type=F32) * (HEAD_DIM ** -0.5)
            s3 = st.reshape(2, 2 * WINDOW, PAIRS_PER_KV * tq) + bias_ref[hk]
            s3 = jnp.where(first & (key < WINDOW), NEG_INF, s3)
            sink = sink_ref[hk]
            m = jnp.maximum(jnp.max(s3, axis=1, keepdims=True), sink)
            pr = jnp.exp(s3 - m)
            pr = pr / (jnp.sum(pr, axis=1, keepdims=True) + jnp.exp(sink - m))
            o_t = jnp.dot(vbd_t, pr.reshape(4 * WINDOW, PAIRS_PER_KV * tq).astype(BF16),
                          preferred_element_type=F32)
            o4 = o_t.T
            outs.extend(o4[pp * tq:(pp + 1) * tq, :] for pp in range(PAIRS_PER_KV))
        yd_ref[rs, :] = jnp.concatenate(outs, axis=-1).astype(BF16)


def _swa_weight_specs(w_in, d):
    nq = N_HEADS * HEAD_DIM
    nkv = N_KV * HEAD_DIM
    nw = nq + 2 * nkv
    q_blk = (w_in.shape[1] - nw) // nq
    kv_blk = (w_in.shape[1] - 2 * nkv) // (2 * nkv)
    assert q_blk * nq + nw == w_in.shape[1] and kv_blk * 2 * nkv + 2 * nkv == w_in.shape[1]
    return [pl.BlockSpec((d, nq), lambda i: (0, q_blk)), pl.BlockSpec((d, 2 * nkv), lambda i: (0, kv_blk))]


def _swa_outputs(rows):
    nq = N_HEADS * HEAD_DIM
    nkv = N_KV * HEAD_DIM
    specs = [pl.BlockSpec((ROW_TILE, nq), lambda i: (i, 0)),
             pl.BlockSpec((ROW_TILE, nkv), lambda i: (i, 0)),
             pl.BlockSpec((ROW_TILE, nkv), lambda i: (i, 0))]
    shapes = [jax.ShapeDtypeStruct((rows, nq), BF16),
              jax.ShapeDtypeStruct((rows, nkv), F32),
              jax.ShapeDtypeStruct((rows, nkv), F32)]
    return specs, shapes


def _swa_cached_mixer(h2, w_in, k_cache, v_cache, bias, sinks, tq):
    rows, d = h2.shape
    nkv = N_KV * HEAD_DIM
    nw = N_HEADS * HEAD_DIM + 2 * nkv
    n_blocks = ROW_TILE // tq
    cache_spec = pl.BlockSpec((n_blocks, WINDOW, nkv), lambda i: (i, 0, 0))
    out_specs, out_shape = _swa_outputs(rows)
    return pl.pallas_call(
        functools.partial(_swa_cached_kernel, tq=tq),
        grid=(rows // ROW_TILE,),
        in_specs=[pl.BlockSpec((ROW_TILE, d), lambda i: (i, 0))] + _swa_weight_specs(w_in, d)
        + [cache_spec, cache_spec,
           pl.BlockSpec((N_HEADS * tq, 2 * WINDOW), lambda i: (0, 0)),
           pl.BlockSpec((N_HEADS * tq, 1), lambda i: (0, 0))],
        out_specs=out_specs,
        out_shape=out_shape,
        scratch_shapes=[pltpu.VMEM((d, nw), BF16)],
        compiler_params=_cparams(1),
        name="swa_cached",
    )(h2, w_in, w_in, k_cache, v_cache, bias, sinks)


def _swa_stream_mixer(h2, w_in, bias_t, sinks_t, blocks_per_seq):
    rows, d = h2.shape
    nkv = N_KV * HEAD_DIM
    nw = N_HEADS * HEAD_DIM + 2 * nkv
    lanes = PAIRS_PER_KV * WINDOW
    out_specs, out_shape = _swa_outputs(rows)
    return pl.pallas_call(
        functools.partial(_swa_stream_kernel, blocks_per_seq=blocks_per_seq),
        grid=(rows // ROW_TILE,),
        in_specs=[pl.BlockSpec((ROW_TILE, d), lambda i: (i, 0))] + _swa_weight_specs(w_in, d)
        + [pl.BlockSpec((N_KV, 2, 2 * WINDOW, lanes), lambda i: (0, 0, 0, 0)),
           pl.BlockSpec((N_KV, 2, 1, lanes), lambda i: (0, 0, 0, 0))],
        out_specs=out_specs,
        out_shape=out_shape,
        scratch_shapes=[pltpu.VMEM((d, nw), BF16), pltpu.VMEM((WINDOW, nkv), F32),
                        pltpu.VMEM((nkv, WINDOW), F32)],
        compiler_params=_cparams(1),
        name="swa_stream",
    )(h2, w_in, w_in, bias_t, sinks_t)


def _t5_bucket(dist):
    max_exact = N_BUCKETS // 2
    dd = np.maximum(dist, 1)
    large = max_exact + (np.log(dd / max_exact) / np.log(WINDOW / max_exact)
                         * (N_BUCKETS - max_exact)).astype(np.int64)
    large = np.minimum(large, N_BUCKETS - 1)
    return np.where(dist < max_exact, dist, large).astype(np.int32)


def _attention_bias(rel_bias):
    by_dist = jnp.take(rel_bias.astype(F32), _t5_bucket(np.arange(WINDOW)), axis=0).T
    neg = jnp.full((N_HEADS, WINDOW), NEG_INF, F32)
    line = jnp.concatenate([neg, by_dist[:, ::-1], neg[:, :WINDOW - 1]], axis=1)
    return jnp.stack([line[:, WINDOW - 1 - q:3 * WINDOW - 1 - q] for q in range(WINDOW)], axis=1)


def kernel(x_prompt, x_sample, state_pool, state_conv, cache_swa_k, cache_swa_v, c_prompt, c_sample, w_ada, b_ada, norm_g, final_norm_g, w_in_even, w_out_even, w_pool, pool_scale, conv_w, w_in_odd, w_out_odd, gm_norm_g, gm_norm_b, gm_w_s, gm_b_s, attn_sinks, rel_bias, w_router, b_router, w_gate, w_up, w_down):
    d = D_MODEL
    bp, tp, _ = x_prompt.shape
    bs, ts, _ = x_sample.shape
    rows_s = bs * ts
    assert rows_s == ROW_TILE and tp % ROUTER_TILE == 0 and PAST_LEN % CHUNK == 0
    assert bp <= V7X_SUBLANES and CHUNK % ts == 0

    n_c = bp + bs
    c_pad = (-n_c) % V7X_SUBLANES
    c_all = jnp.concatenate([c_prompt, c_sample, jnp.zeros((c_pad, d), F32)], axis=0)
    mod_p = _adaln(c_all, w_ada, b_ada)
    mod_s = jnp.repeat(mod_p[:, bp:bp + bs], ts, axis=1)

    tps_p = tp // ROW_TILE
    xp = x_prompt.reshape(bp * tp, d)
    xs_ = x_sample.reshape(rows_s, d)
    w_in0, w_in1 = w_in_even[0], w_in_odd[0]

    hp0 = _norm_mod(xp, norm_g[0, 0], mod_p, 0, tps_p)
    hs0 = _norm_mod(xs_, norm_g[0, 0], mod_s, 0, 1)
    zero_pool = jnp.zeros((bp, POOL_STATE, POOL_WIDTH), F32)
    zero_conv = jnp.zeros((bp, CONV_K - 1, CONV_WIDTH), F32)
    ya_p, pool_p = _pool_mixer(hp0, w_in0, w_pool[0], pool_scale[0], zero_pool, 1, ROW_TILE, 0)
    ya_s, pool_s = _pool_mixer(hs0, w_in0, w_pool[0], pool_scale[0], state_pool[0], bs, ts, PAST_LEN)
    yb_p, conv_p = _conv_mixer(hp0, w_in0, conv_w[0], zero_conv, 1, ROW_TILE)
    yb_s, conv_s = _conv_mixer(hs0, w_in0, conv_w[0], state_conv[0], bs, ts)
    x1p, hpp = _outproj(ya_p, yb_p, xp, mod_p, 0, norm_g[0, 1], w_out_even[0], tps_p)
    x1s, hps = _outproj(ya_s, yb_s, xs_, mod_s, 0, norm_g[0, 1], w_out_even[0], 1)
    (x2p, h1p), (x2s, h1s) = _moe(hpp, hps, x1p, x1s, mod_p, mod_s, 0, norm_g[1, 0], tps_p,
                                  w_router, b_router, w_gate, w_up, w_down, final=False)

    bs_t = gm_b_s[0].T
    (yc_p,) = _gmlp_mixer(h1p, w_in1, gm_norm_g[0], gm_norm_b[0], gm_w_s[0], bs_t, CHUNK, CHUNK, False)
    yc_s, gv_s = _gmlp_mixer(h1s, w_in1, gm_norm_g[0], gm_norm_b[0], gm_w_s[0], bs_t, rows_s, ts, True)
    bias = _attention_bias(rel_bias)
    nkv = N_KV * HEAD_DIM
    bias_t = jnp.transpose(bias.reshape(N_KV, PAIRS_PER_KV, 2, WINDOW, 2 * WINDOW), (0, 2, 4, 1, 3))
    bias_t = bias_t.reshape(N_KV, 2, 2 * WINDOW, PAIRS_PER_KV * WINDOW)
    sinks_t = jnp.transpose(attn_sinks[0].reshape(N_KV, PAIRS_PER_KV, 2), (0, 2, 1))
    sinks_t = jnp.repeat(sinks_t, WINDOW, axis=-1).reshape(N_KV, 2, 1, PAIRS_PER_KV * WINDOW)
    yd_p, k_p, v_p = _swa_stream_mixer(h1p, w_in1, bias_t, sinks_t, tp // WINDOW)
    yd_s, k_s, v_s = _swa_cached_mixer(h1s, w_in1, cache_swa_k[0].reshape(bs, WINDOW, nkv),
                                       cache_swa_v[0].reshape(bs, WINDOW, nkv),
                                       bias[:, :ts, :].reshape(N_HEADS * ts, 2 * WINDOW),
                                       jnp.repeat(attn_sinks[0], ts).reshape(-1, 1), ts)
    x1p, hpp = _outproj(yc_p, yd_p, x2p, mod_p, 1, norm_g[1, 1], w_out_odd[0], tps_p)
    x1s, hps = _outproj(yc_s, yd_s, x2s, mod_s, 1, norm_g[1, 1], w_out_odd[0], 1)
    (yp,), (ys_out,) = _moe(hpp, hps, x1p, x1s, mod_p, mod_s, 1, final_norm_g, tps_p,
                            w_router, b_router, w_gate, w_up, w_down, final=True)

    k_p4 = k_p.reshape(bp, tp, N_KV, HEAD_DIM)[:, -WINDOW:]
    v_p4 = v_p.reshape(bp, tp, N_KV, HEAD_DIM)[:, -WINDOW:]
    k_s4 = jnp.concatenate([cache_swa_k[0], k_s.reshape(bs, ts, N_KV, HEAD_DIM)], axis=1)[:, -WINDOW:]
    v_s4 = jnp.concatenate([cache_swa_v[0], v_s.reshape(bs, ts, N_KV, HEAD_DIM)], axis=1)[:, -WINDOW:]
    return (yp.reshape(bp, tp, d), ys_out.reshape(bs, ts, d),
            pool_p[None], pool_s[None], conv_p[None], conv_s[None],
            k_p4[None], k_s4[None], v_p4[None], v_s4[None],
            gv_s.reshape(bs, ts, GM_WIDTH)[None])
```

```python
import functools

import numpy as np
import jax
import jax.numpy as jnp
from jax import lax
from jax.experimental import pallas as pl
from jax.experimental.pallas import tpu as pltpu

F32 = jnp.float32
BF16 = jnp.bfloat16
I32 = jnp.int32
U32 = jnp.uint32

D_MODEL = 2048
POOL_WINDOWS = (2, 4, 8, 16)
POOL_WIDTH = 1024
POOL_GROUP = 256
POOL_STATE = 15
CONV_WIDTH = 1024
CONV_K = 3
GM_WIDTH = 1024
GM_GROUPS = 8
GM_GROUP = 128
CHUNK = 128
HEAD_DIM = 64
N_HEADS = 16
N_KV = 2
WINDOW = 128
N_BUCKETS = 32
N_EXPERTS = 16
N_EXPERT_GROUPS = 4
EXP_PER_GROUP = 4
TOP_K = 2
EPS = 1e-6
NEG_INF = -1e30
PAST_LEN = 16384

V7X_SUBLANES = 8
V7X_LANES = 128
VMEM_LIMIT = 56 * 1024 * 1024

ROW_TILE = 256
ROUTER_TILE = 1024
POOL_HALO = 16
CONV_HALO = 8
MOE_TILE = 256
TAB_EXPERT, TAB_VALID, TAB_LAST_TILE, TAB_NUSED, TAB_NEXT = 0, 1, 2, 3, 4


def _cparams(n_axes):
    return pltpu.CompilerParams(dimension_semantics=("arbitrary",) * n_axes,
                                vmem_limit_bytes=VMEM_LIMIT)


def _rms(x, g):
    return x * lax.rsqrt(jnp.mean(x * x, axis=-1, keepdims=True) + EPS) * g


def _mod_spec(mod, layer, part):
    nrow = ROW_TILE if mod.shape[1] == ROW_TILE else V7X_SUBLANES
    return pl.BlockSpec((1, nrow, D_MODEL), lambda *_: (layer, 0, part))


def _mod_rows(m_ref, seq):
    if m_ref.shape[1] == V7X_SUBLANES:
        return m_ref[0, pl.ds(seq, 1), :]
    return m_ref[0]


def _adaln_kernel(c_ref, w_ref, b_ref, o_ref):
    c = c_ref[...]
    a = (c * jax.nn.sigmoid(c)).astype(BF16)
    o_ref[0] = jnp.dot(a, w_ref[0].astype(BF16), preferred_element_type=F32) + b_ref[0]


def _adaln(c_all, w_ada, b_ada):
    depth, d, n6 = w_ada.shape
    m = c_all.shape[0]
    tn = 1024
    return pl.pallas_call(
        _adaln_kernel,
        grid=(depth, n6 // tn),
        in_specs=[pl.BlockSpec((m, d), lambda l, j: (0, 0)),
                  pl.BlockSpec((1, d, tn), lambda l, j: (l, 0, j)),
                  pl.BlockSpec((1, 1, tn), lambda l, j: (l, 0, j))],
        out_specs=pl.BlockSpec((1, m, tn), lambda l, j: (l, 0, j)),
        out_shape=jax.ShapeDtypeStruct((depth, m, n6), F32),
        compiler_params=_cparams(2),
        name="adaln",
    )(c_all, w_ada, b_ada.reshape(depth, 1, n6))


def _pool_kernel(x_ref, g_ref, sc_ref, sh_ref, w_ref, wp_ref, ps_ref, st_ref, h_ref, ya_ref, ns_ref,
                 wbf, wpbf, carry, *, nb, tm, tiles_per_seq, start):
    i = pl.program_id(0)
    t = i % tiles_per_seq
    seq = i // tiles_per_seq
    c = POOL_WIDTH
    halo = POOL_HALO

    @pl.when(i == 0)
    def _():
        wbf[...] = w_ref[...].astype(BF16)
        wpbf[...] = wp_ref[...].astype(BF16)

    @pl.when(t == 0)
    def _():
        carry[...] = st_ref[...]

    h = (_rms(x_ref[...], g_ref[...]) * (1.0 + _mod_rows(sc_ref, seq)) + _mod_rows(sh_ref, seq)).astype(BF16)
    h_ref[...] = h
    p = jnp.dot(h, wbf[...], preferred_element_type=F32)
    p3 = p.reshape(nb, tm, c)
    ext3 = jnp.concatenate([carry[...], p3], axis=1)
    tail = ext3[:, tm:tm + halo, :]
    ns_ref[...] = tail
    carry[...] = tail
    ext = ext3.reshape(nb * (halo + tm), c)
    pos = start + t * tm + lax.broadcasted_iota(I32, (1, tm, 1), 1)
    outs = []
    for gi, w in enumerate(POOL_WINDOWS):
        sl = slice(gi * POOL_GROUP, (gi + 1) * POOL_GROUP)
        acc = ext[:, sl]
        shift = 1
        while shift < w:
            acc = acc + pltpu.roll(acc, shift, 0)
            shift *= 2
        win = acc.reshape(nb, halo + tm, POOL_GROUP)[:, halo:, :]
        cnt = jnp.minimum(pos + 1, w).astype(F32)
        dgrp = win / cnt - p3[:, :, sl]
        outs.append(jnp.dot(dgrp.reshape(nb * tm, POOL_GROUP).astype(BF16), wpbf[gi],
                            preferred_element_type=F32))
    y = jnp.concatenate(outs, axis=-1) * ps_ref[...]
    ya_ref[...] = y.astype(BF16)


def _pool_mixer(x2, g, mod, layer, w_in, w_pool, pool_scale, state, nb, tm, start):
    rows, d = x2.shape
    nseq = state.shape[0]
    tiles_per_seq = (rows // nseq) // tm
    seq_blocks = nseq // nb
    c = POOL_WIDTH
    st = jnp.pad(state, ((0, 0), (POOL_HALO - POOL_STATE, 0), (0, 0)))
    kern = functools.partial(_pool_kernel, nb=nb, tm=tm, tiles_per_seq=tiles_per_seq, start=start)
    h2, ya, ns = pl.pallas_call(
        kern,
        grid=(seq_blocks * tiles_per_seq,),
        in_specs=[pl.BlockSpec((nb * tm, d), lambda i: (i, 0)),
                  pl.BlockSpec((1, d), lambda i: (0, 0)),
                  _mod_spec(mod, layer, 1), _mod_spec(mod, layer, 0),
                  pl.BlockSpec((d, c), lambda i: (0, 0)),
                  pl.BlockSpec((len(POOL_WINDOWS), POOL_GROUP, POOL_GROUP), lambda i: (0, 0, 0)),
                  pl.BlockSpec((1, c), lambda i: (0, 0)),
                  pl.BlockSpec((nb, POOL_HALO, c), lambda i: (i // tiles_per_seq, 0, 0))],
        out_specs=[pl.BlockSpec((nb * tm, d), lambda i: (i, 0)),
                   pl.BlockSpec((nb * tm, c), lambda i: (i, 0)),
                   pl.BlockSpec((nb, POOL_HALO, c), lambda i: (i // tiles_per_seq, 0, 0))],
        out_shape=[jax.ShapeDtypeStruct((rows, d), BF16),
                   jax.ShapeDtypeStruct((rows, c), BF16),
                   jax.ShapeDtypeStruct((nseq, POOL_HALO, c), F32)],
        scratch_shapes=[pltpu.VMEM((d, c), BF16),
                        pltpu.VMEM((len(POOL_WINDOWS), POOL_GROUP, POOL_GROUP), BF16),
                        pltpu.VMEM((nb, POOL_HALO, c), F32)],
        compiler_params=_cparams(1),
        name="pool_mixer",
    )(x2, g.reshape(1, d), mod, mod, w_in, w_pool, pool_scale.reshape(1, c), st)
    return h2, ya, ns[:, POOL_HALO - POOL_STATE:, :]


def _conv_kernel(h_ref, wx_ref, wb_ref, wc_ref, cw_ref, st_ref, yb_ref, ns_ref,
                 wxbf, wbbf, wcbf, carry, *, nb, tm, tiles_per_seq):
    i = pl.program_id(1)
    t = i % tiles_per_seq
    tc = wxbf.shape[1]
    halo = CONV_HALO

    @pl.when(i == 0)
    def _():
        wxbf[...] = wx_ref[...].astype(BF16)
        wbbf[...] = wb_ref[...].astype(BF16)
        wcbf[...] = wc_ref[...].astype(BF16)

    @pl.when(t == 0)
    def _():
        carry[...] = st_ref[...]

    h = h_ref[...]
    xin = jnp.dot(h, wxbf[...], preferred_element_type=F32)
    gb = jnp.dot(h, wbbf[...], preferred_element_type=F32)
    gc = jnp.dot(h, wcbf[...], preferred_element_type=F32)
    z3 = (gc * xin).reshape(nb, tm, tc)
    ext3 = jnp.concatenate([carry[...], z3], axis=1)
    tail = ext3[:, tm:tm + halo, :]
    ns_ref[...] = tail
    carry[...] = tail
    ext = ext3.reshape(nb * (halo + tm), tc)
    cw = cw_ref[...]
    conv = cw[0:1, :] * pltpu.roll(ext, 2, 0) + cw[1:2, :] * pltpu.roll(ext, 1, 0) + cw[2:3, :] * ext
    conv = conv.reshape(nb, halo + tm, tc)[:, halo:, :].reshape(nb * tm, tc)
    yb_ref[...] = (gb * conv).astype(BF16)


def _conv_mixer(h2, w_in, conv_w, state, nb, tm):
    rows, d = h2.shape
    nseq = state.shape[0]
    tiles_per_seq = (rows // nseq) // tm
    seq_blocks = nseq // nb
    c = CONV_WIDTH
    tc = 512
    cb = c // tc
    base = POOL_WIDTH // tc
    st = jnp.pad(state, ((0, 0), (CONV_HALO - (CONV_K - 1), 0), (0, 0)))
    kern = functools.partial(_conv_kernel, nb=nb, tm=tm, tiles_per_seq=tiles_per_seq)
    yb, ns = pl.pallas_call(
        kern,
        grid=(cb, seq_blocks * tiles_per_seq),
        in_specs=[pl.BlockSpec((nb * tm, d), lambda j, i: (i, 0)),
                  pl.BlockSpec((d, tc), lambda j, i: (0, base + j)),
                  pl.BlockSpec((d, tc), lambda j, i: (0, base + cb + j)),
                  pl.BlockSpec((d, tc), lambda j, i: (0, base + 2 * cb + j)),
                  pl.BlockSpec((CONV_K, tc), lambda j, i: (0, j)),
                  pl.BlockSpec((nb, CONV_HALO, tc), lambda j, i: (i // tiles_per_seq, 0, j))],
        out_specs=[pl.BlockSpec((nb * tm, tc), lambda j, i: (i, j)),
                   pl.BlockSpec((nb, CONV_HALO, tc), lambda j, i: (i // tiles_per_seq, 0, j))],
        out_shape=[jax.ShapeDtypeStruct((rows, c), BF16),
                   jax.ShapeDtypeStruct((nseq, CONV_HALO, c), F32)],
        scratch_shapes=[pltpu.VMEM((d, tc), BF16)] * 3 + [pltpu.VMEM((nb, CONV_HALO, tc), F32)],
        compiler_params=_cparams(2),
        name="conv_mixer",
    )(h2, w_in, w_in, w_in, conv_w, st)
    return yb, ns[:, CONV_HALO - (CONV_K - 1):, :]


def _pack_bf16_pairs(v):
    c = v.shape[1] // 2
    r = v.astype(BF16).astype(F32)
    lo = pltpu.bitcast(r[:, :c], U32)
    hi = pltpu.bitcast(r[:, c:], U32)
    return (hi & jnp.uint32(0xFFFF0000)) | (lo >> 16)


def _unpack_pairs_f32(w):
    return pltpu.bitcast(w << 16, F32), pltpu.bitcast(w & jnp.uint32(0xFFFF0000), F32)


def _unpack_bf16_pairs(w):
    lo, hi = _unpack_pairs_f32(w)
    return lo.astype(BF16), hi.astype(BF16)


def _outproj_kernel(ya_ref, yb_ref, x_ref, g1_ref, sc_ref, sh_ref, ng_ref, wo_ref,
                    x1_ref, hp_ref, wobf, *, tiles_per_seq):
    i = pl.program_id(0)
    seq = i // tiles_per_seq
    half = ya_ref.shape[1]

    @pl.when(i == 0)
    def _():
        wobf[...] = wo_ref[...].astype(BF16)

    y = (jnp.dot(ya_ref[...], wobf[:half, :], preferred_element_type=F32)
         + jnp.dot(yb_ref[...], wobf[half:, :], preferred_element_type=F32))
    x1 = x_ref[...] + _mod_rows(g1_ref, seq) * y
    x1_ref[...] = x1
    h2 = _rms(x1, ng_ref[...]) * (1.0 + _mod_rows(sc_ref, seq)) + _mod_rows(sh_ref, seq)
    hp_ref[...] = _pack_bf16_pairs(h2)


def _outproj(ya, yb, x2, mod, layer, ng, w_out, tiles_per_seq):
    rows_all, d = x2.shape
    half = ya.shape[1]
    row_spec = lambda w: pl.BlockSpec((ROW_TILE, w), lambda i: (i, 0))
    return pl.pallas_call(
        functools.partial(_outproj_kernel, tiles_per_seq=tiles_per_seq),
        grid=(rows_all // ROW_TILE,),
        in_specs=[row_spec(half), row_spec(half), row_spec(d),
                  _mod_spec(mod, layer, 2), _mod_spec(mod, layer, 4), _mod_spec(mod, layer, 3),
                  pl.BlockSpec((1, d), lambda i: (0, 0)), pl.BlockSpec((d, d), lambda i: (0, 0))],
        out_specs=[row_spec(d), row_spec(d // 2)],
        out_shape=[jax.ShapeDtypeStruct((rows_all, d), F32),
                   jax.ShapeDtypeStruct((rows_all, d // 2), U32)],
        scratch_shapes=[pltpu.VMEM((d, d), BF16)],
        compiler_params=_cparams(1),
        name="outproj",
    )(ya, yb, x2, mod, mod, mod, ng.reshape(1, d), w_out)


def _router_kernel(hpp_ref, hps_ref, wr_ref, br_ref, pos_ref, rw_ref, tab_ref,
                   cnt_acc, totals, starts, padded, *, nt_p, rows_s):
    ph = pl.program_id(0)
    t = pl.program_id(1)
    last = nt_p
    r = hpp_ref.shape[0]
    half = hpp_ref.shape[1]
    ne = N_EXPERTS
    sub = lax.broadcasted_iota(I32, (ne, V7X_LANES), 0)

    @pl.when(t == 0)
    def _():
        cnt_acc[...] = jnp.zeros_like(cnt_acc)

    @pl.when((ph == 0) & (t == 0))
    def _():
        starts[...] = jnp.zeros_like(starts)
        padded[...] = jnp.zeros_like(padded)

    @pl.when((ph == 1) & (t == 0))
    def _():
        pad = jnp.floor((totals[...] + (MOE_TILE - 1.0)) * (1.0 / MOE_TILE)) * MOE_TILE
        run = pad
        k = 1
        while k < ne:
            run = run + jnp.where(sub >= k, pltpu.roll(run, k, 0), 0.0)
            k *= 2
        padded[...] = pad
        starts[...] = run - pad

    is_s = t == last
    w_s = jnp.concatenate([hps_ref[...], jnp.zeros((r - rows_s, half), U32)], axis=0)
    w = jnp.where(is_s, w_s, hpp_ref[...])
    lo, hi = _unpack_bf16_pairs(w)
    wr = wr_ref[...].astype(BF16)
    nt_dims = (((1,), (1,)), ((), ()))
    log_t = (lax.dot_general(wr[:, :half], lo, nt_dims, preferred_element_type=F32)
             + lax.dot_general(wr[:, half:], hi, nt_dims, preferred_element_type=F32))

    s = jax.nn.sigmoid(log_t)
    sg = s + br_ref[...]
    eid = lax.broadcasted_iota(I32, (ne, r), 0)
    within = eid % EXP_PER_GROUP
    grp = eid // EXP_PER_GROUP

    def group_rot(x, k):
        return jnp.where(within + k < EXP_PER_GROUP,
                         pltpu.roll(x, ne - k, 0), pltpu.roll(x, EXP_PER_GROUP - k, 0))

    rank = jnp.zeros((ne, r), I32)
    for k in range(1, EXP_PER_GROUP):
        mate = group_rot(sg, k)
        wrapped = within + k >= EXP_PER_GROUP
        ahead = (mate > sg) | (wrapped & (mate == sg))
        rank = rank + ahead.astype(I32)
    top2 = rank < TOP_K
    kept = jnp.where(top2, sg, 0.0)
    gscore = kept
    for k in range(1, EXP_PER_GROUP):
        gscore = gscore + group_rot(kept, k)
    win = None
    for k in range(1, N_EXPERT_GROUPS):
        other = pltpu.roll(gscore, EXP_PER_GROUP * k, 0)
        beats = (gscore > other) | ((grp < k) & (gscore == other))
        win = beats if win is None else (win & beats)
    n_valid = jnp.where(is_s, rows_s, r)
    tok = lax.broadcasted_iota(I32, (ne, r), 1)
    sel = top2 & win & (tok < n_valid)
    picked = jnp.where(sel, s, 0.0)
    wsum = jnp.sum(picked, axis=0, keepdims=True)
    gate = picked / jnp.where(tok[0:1, :] < n_valid, wsum, 1.0)

    src = lax.broadcasted_iota(I32, (r, r), 0)
    dst = lax.broadcasted_iota(I32, (r, r), 1)
    before = (src < dst).astype(BF16)
    selb = sel.astype(F32)
    ranks = jnp.dot(selb.astype(BF16), before, preferred_element_type=F32)
    slot = (starts[...][:, 0:1] + cnt_acc[...][:, 0:1] + ranks).astype(I32)
    cnt_new = cnt_acc[...] + jnp.sum(selb, axis=1, keepdims=True)
    cnt_acc[...] = cnt_new

    @pl.when((ph == 0) & (t == last))
    def _():
        totals[...] = cnt_new

    e_a = jnp.min(jnp.where(sel, eid, ne), axis=0, keepdims=True)
    e_b = jnp.max(jnp.where(sel, eid, -1), axis=0, keepdims=True)
    is_a = sel & (eid == e_a)
    is_b = sel & (eid == e_b)
    pos_a = jnp.sum(jnp.where(is_a, slot, 0), axis=0, keepdims=True)
    pos_b = jnp.sum(jnp.where(is_b, slot, 0), axis=0, keepdims=True)
    w_a = jnp.sum(jnp.where(is_a, gate, 0.0), axis=0, keepdims=True)
    w_b = jnp.sum(jnp.where(is_b, gate, 0.0), axis=0, keepdims=True)
    pos_ref[0] = jnp.concatenate([pos_a, pos_b], axis=0)
    wmat = jnp.concatenate([w_a, w_b, jnp.zeros((V7X_LANES - 2, r), F32)], axis=0)
    rw_ref[...] = wmat.T

    @pl.when((ph == 1) & (t == last))
    def _():
        ends = starts[...] + padded[...]
        lane = lax.broadcasted_iota(I32, (ne, V7X_LANES), 1)
        tile_start = (lane * MOE_TILE).astype(F32)
        te = jnp.sum((tile_start >= ends).astype(I32), axis=0, keepdims=True)
        valid = te < ne
        last_e = jnp.max(jnp.where(padded[...] > 0.0, sub, 0), axis=0, keepdims=True)
        te = jnp.where(valid, te, last_e)
        n_used = jnp.sum(valid.astype(I32), axis=1, keepdims=True) + jnp.zeros((1, V7X_LANES), I32)
        last_tile = jnp.where(padded[...] > 0.0, ends - MOE_TILE, -1.0).astype(I32)
        last_tile_row = jnp.sum(jnp.where(sub == lane, last_tile, 0), axis=0, keepdims=True)
        later = jnp.min(jnp.where((sub > te) & (padded[...] > 0.0), sub, ne), axis=0, keepdims=True)
        next_e = jnp.where(later < ne, later, -1)
        zero = jnp.zeros((1, V7X_LANES), I32)
        tab_ref[...] = jnp.concatenate([te, valid.astype(I32), last_tile_row, n_used, next_e,
                                        zero, zero, zero], axis=0)


def _router(hp_p, hp_s, w_router, b_router):
    n_p, half = hp_p.shape
    rows_s = hp_s.shape[0]
    r = ROUTER_TILE
    nt_p = n_p // r
    nt = nt_p + 1
    kern = functools.partial(_router_kernel, nt_p=nt_p, rows_s=rows_s)
    pos, rw, tab = pl.pallas_call(
        kern,
        grid=(2, nt),
        in_specs=[pl.BlockSpec((r, half), lambda p, t: (jnp.minimum(t, nt_p - 1), 0)),
                  pl.BlockSpec((rows_s, half), lambda p, t: (0, 0)),
                  pl.BlockSpec((N_EXPERTS, 2 * half), lambda p, t: (0, 0)),
                  pl.BlockSpec((N_EXPERTS, 1), lambda p, t: (0, 0))],
        out_specs=[pl.BlockSpec((1, TOP_K, r), lambda p, t: (p * t, 0, 0)),
                   pl.BlockSpec((r, V7X_LANES), lambda p, t: (p * t, 0)),
                   pl.BlockSpec((V7X_SUBLANES, V7X_LANES), lambda p, t: (0, 0))],
        out_shape=[jax.ShapeDtypeStruct((nt, TOP_K, r), I32),
                   jax.ShapeDtypeStruct((nt * r, V7X_LANES), F32),
                   jax.ShapeDtypeStruct((V7X_SUBLANES, V7X_LANES), I32)],
        scratch_shapes=[pltpu.VMEM((N_EXPERTS, V7X_LANES), F32)] * 4,
        compiler_params=_cparams(2),
        name="router",
    )(hp_p, hp_s, w_router.T, b_router.reshape(N_EXPERTS, 1))
    return pos.reshape(-1), rw, tab.reshape(-1)


def _pos_index(tok0):
    return (tok0 // ROUTER_TILE) * (TOP_K * ROUTER_TILE) + tok0 % ROUTER_TILE


def _dispatch_kernel(pos_ref, tab_ref, hpp_ref, hps_ref, xs_ref, zbuf, sem, *, n_p_steps):
    i = pl.program_id(0)
    rows = hpp_ref.shape[0]

    @pl.when(i == 0)
    def _():
        zbuf[...] = jnp.zeros_like(zbuf)

        def fill(e):
            first = pl.multiple_of(tab_ref[TAB_LAST_TILE * V7X_LANES + e], MOE_TILE)
            return pltpu.make_async_copy(zbuf, xs_ref.at[pl.ds(first, MOE_TILE), :], sem)

        for e in range(N_EXPERTS):
            @pl.when(tab_ref[TAB_LAST_TILE * V7X_LANES + e] >= 0)
            def _():
                fill(e).start()
        for e in range(N_EXPERTS):
            @pl.when(tab_ref[TAB_LAST_TILE * V7X_LANES + e] >= 0)
            def _():
                fill(e).wait()

        def tail(j):
            first = pl.multiple_of(j * MOE_TILE, MOE_TILE)
            return pltpu.make_async_copy(zbuf, xs_ref.at[pl.ds(first, MOE_TILE), :], sem)

        def tail_start(j, carry):
            tail(j).start()
            return carry

        def tail_wait(j, carry):
            tail(j).wait()
            return carry

        n_used = tab_ref[TAB_NUSED * V7X_LANES]
        n_tiles = xs_ref.shape[0] // MOE_TILE
        lax.fori_loop(n_used, n_tiles, tail_start, 0)
        lax.fori_loop(n_used, n_tiles, tail_wait, 0)

    base = _pos_index(i * rows)

    def scatter(src_ref):
        def row_copy(r, dst):
            return pltpu.make_async_copy(src_ref.at[pl.ds(r, 1), :], xs_ref.at[pl.ds(dst, 1), :], sem)

        def issue(r, carry):
            row_copy(r, pos_ref[base + r]).start()
            row_copy(r, pos_ref[base + ROUTER_TILE + r]).start(priority=1)
            return carry

        lax.fori_loop(0, rows, issue, 0, unroll=8)
        block = pltpu.make_async_copy(src_ref, xs_ref.at[pl.ds(0, rows), :], sem)
        for _ in range(TOP_K):
            block.wait()

    @pl.when(i < n_p_steps)
    def _():
        scatter(hpp_ref)

    @pl.when(i == n_p_steps)
    def _():
        scatter(hps_ref)


def _dispatch(pos, tab, hp_p, hp_s, n_rows_sorted):
    n_p, c = hp_p.shape
    n_p_steps = n_p // ROW_TILE
    assert hp_s.shape[0] == ROW_TILE
    kern = functools.partial(_dispatch_kernel, n_p_steps=n_p_steps)
    return pl.pallas_call(
        kern,
        grid_spec=pltpu.PrefetchScalarGridSpec(
            num_scalar_prefetch=2,
            grid=(n_p_steps + 1,),
            in_specs=[pl.BlockSpec((ROW_TILE, c), lambda i, p, t: (jnp.minimum(i, n_p_steps - 1), 0)),
                      pl.BlockSpec((ROW_TILE, c), lambda i, p, t: (0, 0))],
            out_specs=pl.BlockSpec(memory_space=pl.ANY),
            scratch_shapes=[pltpu.VMEM((MOE_TILE, c), U32), pltpu.SemaphoreType.DMA(())]),
        out_shape=jax.ShapeDtypeStruct((n_rows_sorted, c), U32),
        compiler_params=_cparams(1),
        name="moe_dispatch",
    )(pos, tab, hp_p, hp_s)


def _experts_kernel(tab_ref, xs_ref, wg_hbm, wu_hbm, wd_hbm, ys_ref,
                    wg32, wu32, wd32, wgbf, wubf, wdbf, slot_ref, sems, *, layer):
    i = pl.program_id(0)
    expert = tab_ref[TAB_EXPERT * V7X_LANES + i]
    prev = tab_ref[TAB_EXPERT * V7X_LANES + jnp.maximum(i - 1, 0)]
    upcoming = tab_ref[TAB_NEXT * V7X_LANES + i]
    changed = (i == 0) | (expert != prev)
    half = xs_ref.shape[1]

    def weight_copies(e, slot):
        return (pltpu.make_async_copy(wg_hbm.at[layer, e], wg32.at[slot], sems.at[0, slot]),
                pltpu.make_async_copy(wu_hbm.at[layer, e], wu32.at[slot], sems.at[1, slot]),
                pltpu.make_async_copy(wd_hbm.at[layer, e], wd32.at[slot], sems.at[2, slot]))

    @pl.when(i == 0)
    def _():
        slot_ref[0] = 0
        for cp in weight_copies(expert, 0):
            cp.start()

    @pl.when(changed & (i > 0))
    def _():
        slot_ref[0] = 1 - slot_ref[0]

    for slot in range(2):
        @pl.when(changed & (slot_ref[0] == slot))
        def _():
            for cp in weight_copies(expert, slot):
                cp.wait()
            wgbf[...] = wg32[slot].astype(BF16)
            wubf[...] = wu32[slot].astype(BF16)
            wdbf[...] = wd32[slot].astype(BF16)

            @pl.when(upcoming >= 0)
            def _():
                for cp in weight_copies(upcoming, 1 - slot):
                    cp.start(priority=1)

    @pl.when(tab_ref[TAB_VALID * V7X_LANES + i] > 0)
    def _():
        lo, hi = _unpack_bf16_pairs(xs_ref[...])
        a = (jnp.dot(lo, wgbf[:half, :], preferred_element_type=F32)
             + jnp.dot(hi, wgbf[half:, :], preferred_element_type=F32))
        b = (jnp.dot(lo, wubf[:half, :], preferred_element_type=F32)
             + jnp.dot(hi, wubf[half:, :], preferred_element_type=F32))
        hid = (a * jax.nn.sigmoid(a)) * b
        ys_ref[...] = _pack_bf16_pairs(jnp.dot(hid.astype(BF16), wdbf[...], preferred_element_type=F32))

    @pl.when(tab_ref[TAB_VALID * V7X_LANES + i] == 0)
    def _():
        ys_ref[...] = jnp.zeros_like(ys_ref)


def _experts(tab, xs, w_gate, w_up, w_down, layer):
    n_rows, half = xs.shape
    _, _, d, f = w_gate.shape
    nt = n_rows // MOE_TILE
    assert nt <= V7X_LANES

    def tile(i, tab_ref):
        return jnp.minimum(i, tab_ref[TAB_NUSED * V7X_LANES] - 1)

    hbm = pl.BlockSpec(memory_space=pl.ANY)
    return pl.pallas_call(
        functools.partial(_experts_kernel, layer=layer),
        grid_spec=pltpu.PrefetchScalarGridSpec(
            num_scalar_prefetch=1,
            grid=(nt,),
            in_specs=[pl.BlockSpec((MOE_TILE, half), lambda i, t: (tile(i, t), 0)), hbm, hbm, hbm],
            out_specs=pl.BlockSpec((MOE_TILE, half), lambda i, t: (i, 0)),
            scratch_shapes=[pltpu.VMEM((2, d, f), F32), pltpu.VMEM((2, d, f), F32), pltpu.VMEM((2, f, d), F32),
                            pltpu.VMEM((d, f), BF16), pltpu.VMEM((d, f), BF16), pltpu.VMEM((f, d), BF16),
                            pltpu.SMEM((1,), I32), pltpu.SemaphoreType.DMA((3, 2))]),
        out_shape=jax.ShapeDtypeStruct((n_rows, half), U32),
        compiler_params=_cparams(1),
        name="moe_experts",
    )(tab, xs, w_gate, w_up, w_down)


def _combine_kernel(pos_ref, ys_ref, x1_ref, rw_ref, g2_ref, ng_ref, sc_ref, sh_ref, *rest,
                    tok0, tiles_per_seq, n_steps, final):
    if final:
        x2_ref, buf, sems = rest
        hn_ref = None
    else:
        x2_ref, hn_ref, buf, sems = rest
    i = pl.program_id(0)
    seq = i // tiles_per_seq
    rows = x1_ref.shape[0]

    def gather(step, slot):
        base = _pos_index(tok0 + step * rows)

        def issue(r, carry):
            for k in range(TOP_K):
                src = pos_ref[base + k * ROUTER_TILE + r]
                pltpu.make_async_copy(ys_ref.at[pl.ds(src, 1), :], buf.at[slot, k, pl.ds(r, 1), :],
                                      sems.at[slot]).start(priority=k)
            return carry

        lax.fori_loop(0, rows, issue, 0, unroll=8)

    @pl.when(i == 0)
    def _():
        gather(0, 0)

    @pl.when(i + 1 < n_steps)
    def _():
        gather(i + 1, (i + 1) % 2)

    slot = i % 2
    for k in range(TOP_K):
        pltpu.make_async_copy(ys_ref.at[pl.ds(0, rows), :], buf.at[slot, k], sems.at[slot]).wait()

    rw = rw_ref[...]
    lo_a, hi_a = _unpack_pairs_f32(buf[slot, 0])
    lo_b, hi_b = _unpack_pairs_f32(buf[slot, 1])
    w_a = rw[:, 0:1]
    w_b = rw[:, 1:2]
    moe = jnp.concatenate([w_a * lo_a + w_b * lo_b, w_a * hi_a + w_b * hi_b], axis=-1)
    x2 = x1_ref[...] + _mod_rows(g2_ref, seq) * moe
    if final:
        x2_ref[...] = _rms(x2, ng_ref[...])
    else:
        x2_ref[...] = x2
        hn_ref[...] = (_rms(x2, ng_ref[...]) * (1.0 + _mod_rows(sc_ref, seq))
                       + _mod_rows(sh_ref, seq)).astype(BF16)


def _combine(pos, ys, x1, rw, tok0, mod, layer, ng, tiles_per_seq, final):
    n_tok, d = x1.shape
    kern = functools.partial(_combine_kernel, tok0=tok0, tiles_per_seq=tiles_per_seq,
                             n_steps=n_tok // ROW_TILE, final=final)
    rw_off = tok0 // ROW_TILE
    row_spec = lambda w: pl.BlockSpec((ROW_TILE, w), lambda i, p: (i, 0))
    out_shape = [jax.ShapeDtypeStruct((n_tok, d), F32)]
    out_specs = [row_spec(d)]
    if not final:
        out_shape.append(jax.ShapeDtypeStruct((n_tok, d), BF16))
        out_specs.append(row_spec(d))
    nxt = min(layer + 1, mod.shape[0] - 1)
    return pl.pallas_call(
        kern,
        grid_spec=pltpu.PrefetchScalarGridSpec(
            num_scalar_prefetch=1,
            grid=(n_tok // ROW_TILE,),
            in_specs=[pl.BlockSpec(memory_space=pl.ANY), row_spec(d),
                      pl.BlockSpec((ROW_TILE, V7X_LANES), lambda i, p: (rw_off + i, 0)),
                      _mod_spec(mod, layer, 5), pl.BlockSpec((1, d), lambda i, p: (0, 0)),
                      _mod_spec(mod, nxt, 1), _mod_spec(mod, nxt, 0)],
            out_specs=out_specs,
            scratch_shapes=[pltpu.VMEM((2, TOP_K, ROW_TILE, d // 2), U32), pltpu.SemaphoreType.DMA((2,))]),
        out_shape=out_shape,
        compiler_params=_cparams(1),
        name="moe_combine_final" if final else "moe_combine",
    )(pos, ys, x1, rw, mod, ng.reshape(1, d), mod, mod)


def _moe(hp_p, hp_s, x1_p, x1_s, mod_p, mod_s, layer, ng, tps_p, w_router, b_router,
         w_gate, w_up, w_down, final):
    n_p = hp_p.shape[0]
    n_tok = n_p + hp_s.shape[0]
    max_rows = TOP_K * n_tok + N_EXPERTS * (MOE_TILE - 1)
    n_rows_sorted = -(-max_rows // MOE_TILE) * MOE_TILE
    pos, rw, tab = _router(hp_p, hp_s, w_router, b_router)
    xs = _dispatch(pos, tab, hp_p, hp_s, n_rows_sorted)
    ys = _experts(tab, xs, w_gate, w_up, w_down, layer)
    out_p = _combine(pos, ys, x1_p, rw, 0, mod_p, layer, ng, tps_p, final)
    out_s = _combine(pos, ys, x1_s, rw, n_p, mod_s, layer, ng, 1, final)
    return out_p, out_s


def _gmlp_kernel(h_ref, w_ref, lg_ref, lb_ref, ws_ref, bs_ref, yc_ref, *rest, ell, blk, emit_v):
    if emit_v:
        gv_ref, wbf, wsbf = rest
    else:
        wbf, wsbf = rest
    i = pl.program_id(0)
    rows = h_ref.shape[0]
    c = GM_WIDTH

    @pl.when(i == 0)
    def _():
        wbf[...] = w_ref[...].astype(BF16)
        r = lax.broadcasted_iota(I32, (ell, ell), 0)
        s = lax.broadcasted_iota(I32, (ell, ell), 1)
        keep = (r >= s) & ((r // blk) == (s // blk))
        rsel = (lax.broadcasted_iota(I32, (ell, CHUNK), 0) % blk
                == lax.broadcasted_iota(I32, (ell, CHUNK), 1)).astype(BF16)
        csel = (lax.broadcasted_iota(I32, (CHUNK, ell), 1) % blk
                == lax.broadcasted_iota(I32, (CHUNK, ell), 0)).astype(BF16)
        for g in range(GM_GROUPS):
            wchunk = ws_ref[g].astype(BF16)
            if blk == ell:
                full = wchunk
            else:
                rowsp = jnp.dot(rsel, wchunk, preferred_element_type=F32).astype(BF16)
                full = jnp.dot(rowsp, csel, preferred_element_type=F32).astype(BF16)
            wsbf[g] = jnp.where(keep, full, jnp.zeros_like(full))

    uv = jnp.dot(h_ref[...], wbf[...], preferred_element_type=F32)
    u = uv[:, :c]
    v = uv[:, c:]
    vc = v - jnp.mean(v, axis=-1, keepdims=True)
    vn = vc * lax.rsqrt(jnp.mean(vc * vc, axis=-1, keepdims=True) + EPS) * lg_ref[...] + lb_ref[...]
    if emit_v:
        gv_ref[...] = vn
    vb = vn.astype(BF16)
    bs = bs_ref[...]
    for ch in range(rows // ell):
        rs = slice(ch * ell, (ch + 1) * ell)
        outs = []
        for g in range(GM_GROUPS):
            cs = slice(g * GM_GROUP, (g + 1) * GM_GROUP)
            mixed = jnp.dot(wsbf[g], vb[rs, cs], preferred_element_type=F32)
            mixed = (mixed.reshape(ell // blk, blk, GM_GROUP) + bs[:blk, g:g + 1][None]).reshape(ell, GM_GROUP)
            outs.append(u[rs, cs] * mixed)
        yc_ref[rs, :] = jnp.concatenate(outs, axis=-1).astype(BF16)


def _gmlp_mixer(h2, w_in, ln_g, ln_b, ws, bs_t, ell, blk, emit_v):
    rows, d = h2.shape
    c = GM_WIDTH
    kern = functools.partial(_gmlp_kernel, ell=ell, blk=blk, emit_v=emit_v)
    out_specs = [pl.BlockSpec((ROW_TILE, c), lambda i: (i, 0))]
    out_shape = [jax.ShapeDtypeStruct((rows, c), BF16)]
    if emit_v:
        out_specs.append(pl.BlockSpec((ROW_TILE, c), lambda i: (i, 0)))
        out_shape.append(jax.ShapeDtypeStruct((rows, c), F32))
    return pl.pallas_call(
        kern,
        grid=(rows // ROW_TILE,),
        in_specs=[pl.BlockSpec((ROW_TILE, d), lambda i: (i, 0)),
                  pl.BlockSpec((d, 2 * c), lambda i: (0, 0)),
                  pl.BlockSpec((1, c), lambda i: (0, 0)),
                  pl.BlockSpec((1, c), lambda i: (0, 0)),
                  pl.BlockSpec((GM_GROUPS, CHUNK, CHUNK), lambda i: (0, 0, 0)),
                  pl.BlockSpec((CHUNK, GM_GROUPS), lambda i: (0, 0))],
        out_specs=out_specs,
        out_shape=out_shape,
        scratch_shapes=[pltpu.VMEM((d, 2 * c), BF16), pltpu.VMEM((GM_GROUPS, ell, ell), BF16)],
        compiler_params=_cparams(1),
        name="gmlp_mixer",
    )(h2, w_in, ln_g.reshape(1, c), ln_b.reshape(1, c), ws, bs_t)


PAIR_W = 2 * HEAD_DIM
PAIRS_PER_KV = N_HEADS // N_KV // 2
NT_DIMS = (((1,), (1,)), ((), ()))


def _swa_project(i, h_ref, wq_ref, wkv_ref, wbf):
    nq = N_HEADS * HEAD_DIM

    @pl.when(i == 0)
    def _():
        wbf[:, :nq] = wq_ref[...].astype(BF16)
        wbf[:, nq:] = wkv_ref[...].astype(BF16)

    return jnp.dot(h_ref[...], wbf[...], preferred_element_type=F32)


def _pair_block_diag(a, a_swapped, hk, axis):
    dim_axis = 1 - axis
    low = lax.broadcasted_iota(I32, a.shape, dim_axis) < HEAD_DIM
    lo, hi = (a, a_swapped) if hk == 0 else (a_swapped, a)
    return jnp.concatenate([jnp.where(low, lo, 0.0), jnp.where(low, 0.0, hi)], axis=axis).astype(BF16)


def _stack_pairs(qkv, rs, hk):
    p0 = hk * PAIRS_PER_KV
    return jnp.concatenate([qkv[rs, (p0 + pp) * PAIR_W:(p0 + pp + 1) * PAIR_W]
                            for pp in range(PAIRS_PER_KV)], axis=0).astype(BF16)


def _swa_cached_kernel(h_ref, wq_ref, wkv_ref, kp_ref, vp_ref, bias_ref, sink_ref, yd_ref, k_ref, v_ref,
                       wbf, *, tq):
    i = pl.program_id(0)
    rows = h_ref.shape[0]
    nq = N_HEADS * HEAD_DIM
    nkv = N_KV * HEAD_DIM
    n_blocks = rows // tq
    qkv = _swa_project(i, h_ref, wq_ref, wkv_ref, wbf)
    k_new = qkv[:, nq:nq + nkv]
    v_new = qkv[:, nq + nkv:]
    k_ref[...] = k_new
    v_ref[...] = v_new
    pad = jnp.zeros((WINDOW - tq, nkv), F32)

    scores, vbds = [], []
    for blk in range(n_blocks):
        rs = slice(blk * tq, (blk + 1) * tq)
        kcat = jnp.concatenate([kp_ref[blk], k_new[rs], pad], axis=0)
        vcat = jnp.concatenate([vp_ref[blk], v_new[rs], pad], axis=0)
        kswap = pltpu.roll(kcat, HEAD_DIM, 1)
        vswap = pltpu.roll(vcat, HEAD_DIM, 1)
        per_head = []
        for hk in range(N_KV):
            kbd = _pair_block_diag(kcat, kswap, hk, 0)
            vbds.append(_pair_block_diag(vcat, vswap, hk, 0))
            s4 = lax.dot_general(_stack_pairs(qkv, rs, hk), kbd, NT_DIMS,
                                 preferred_element_type=F32) * (HEAD_DIM ** -0.5)
            for pp in range(PAIRS_PER_KV):
                for sub in range(2):
                    per_head.append(s4[pp * tq:(pp + 1) * tq, sub * 2 * WINDOW:(sub + 1) * 2 * WINDOW])
        scores.append(jnp.concatenate(per_head, axis=0))

    s_all = jnp.stack(scores, axis=0) + bias_ref[...][None]
    sink = sink_ref[...][None]
    m = jnp.maximum(jnp.max(s_all, axis=-1, keepdims=True), sink)
    pr = jnp.exp(s_all - m)
    pr = pr / (jnp.sum(pr, axis=-1, keepdims=True) + jnp.exp(sink - m))

    for blk in range(n_blocks):
        outs = []
        for hk in range(N_KV):
            p4 = []
            for pp in range(PAIRS_PER_KV):
                h0 = 2 * (hk * PAIRS_PER_KV + pp)
                p4.append(jnp.concatenate([pr[blk, h0 * tq:(h0 + 1) * tq, :],
                                           pr[blk, (h0 + 1) * tq:(h0 + 2) * tq, :]], axis=-1))
            o4 = jnp.dot(jnp.concatenate(p4, axis=0).astype(BF16), vbds[blk * N_KV + hk],
                         preferred_element_type=F32)
            outs.extend(o4[pp * tq:(pp + 1) * tq, :] for pp in range(PAIRS_PER_KV))
        yd_ref[blk * tq:(blk + 1) * tq, :] = jnp.concatenate(outs, axis=-1).astype(BF16)


def _swa_stream_kernel(h_ref, wq_ref, wkv_ref, bias_ref, sink_ref, yd_ref, k_ref, v_ref,
                       wbf, kprev, vprev_t, *, blocks_per_seq):
    i = pl.program_id(0)
    rows = h_ref.shape[0]
    nq = N_HEADS * HEAD_DIM
    nkv = N_KV * HEAD_DIM
    tq = WINDOW
    n_blocks = rows // tq

    @pl.when(i == 0)
    def _():
        kprev[...] = jnp.zeros_like(kprev)
        vprev_t[...] = jnp.zeros_like(vprev_t)

    qkv = _swa_project(i, h_ref, wq_ref, wkv_ref, wbf)
    k_new = qkv[:, nq:nq + nkv]
    v_new = qkv[:, nq + nkv:]
    k_ref[...] = k_new
    v_ref[...] = v_new
    v_new_t = v_new.T
    key = lax.broadcasted_iota(I32, (2, 2 * WINDOW, PAIRS_PER_KV * tq), 1)

    for blk in range(n_blocks):
        rs = slice(blk * tq, (blk + 1) * tq)
        first = (i * n_blocks + blk) % blocks_per_seq == 0
        k_cur = k_new[rs]
        v_cur_t = v_new_t[:, rs]
        kcat = jnp.concatenate([kprev[...], k_cur], axis=0)
        vcat_t = jnp.concatenate([vprev_t[...], v_cur_t], axis=1)
        kprev[...] = k_cur
        vprev_t[...] = v_cur_t
        kswap = pltpu.roll(kcat, HEAD_DIM, 1)
        vswap_t = pltpu.roll(vcat_t, HEAD_DIM, 0)
        outs = []
        for hk in range(N_KV):
            kbd = _pair_block_diag(kcat, kswap, hk, 0)
            vbd_t = _pair_block_diag(vcat_t, vswap_t, hk, 1)
            st = lax.dot_general(kbd, _stack_pairs(qkv, rs, hk), NT_DIMS,
                                 preferred_element_type=F32) * (HEAD_DIM ** -0.5)
            s3 = st.reshape(2, 2 * WINDOW, PAIRS_PER_KV * tq) + bias_ref[hk]
            s3 = jnp.where(first & (key < WINDOW), NEG_INF, s3)
            sink = sink_ref[hk]
            m = jnp.maximum(jnp.max(s3, axis=1, keepdims=True), sink)
            pr = jnp.exp(s3 - m)
            pr = pr / (jnp.sum(pr, axis=1, keepdims=True) + jnp.exp(sink - m))
            o_t = jnp.dot(vbd_t, pr.reshape(4 * WINDOW, PAIRS_PER_KV * tq).astype(BF16),
                          preferred_element_type=F32)
            o4 = o_t.T
            outs.extend(o4[pp * tq:(pp + 1) * tq, :] for pp in range(PAIRS_PER_KV))
        yd_ref[rs, :] = jnp.concatenate(outs, axis=-1).astype(BF16)


def _swa_weight_specs(w_in, d):
    nq = N_HEADS * HEAD_DIM
    nkv = N_KV * HEAD_DIM
    nw = nq + 2 * nkv
    q_blk = (w_in.shape[1] - nw) // nq
    kv_blk = (w_in.shape[1] - 2 * nkv) // (2 * nkv)
    assert q_blk * nq + nw == w_in.shape[1] and kv_blk * 2 * nkv + 2 * nkv == w_in.shape[1]
    return [pl.BlockSpec((d, nq), lambda i: (0, q_blk)), pl.BlockSpec((d, 2 * nkv), lambda i: (0, kv_blk))]


def _swa_outputs(rows):
    nq = N_HEADS * HEAD_DIM
    nkv = N_KV * HEAD_DIM
    specs = [pl.BlockSpec((ROW_TILE, nq), lambda i: (i, 0)),
             pl.BlockSpec((ROW_TILE, nkv), lambda i: (i, 0)),
             pl.BlockSpec((ROW_TILE, nkv), lambda i: (i, 0))]
    shapes = [jax.ShapeDtypeStruct((rows, nq), BF16),
              jax.ShapeDtypeStruct((rows, nkv), F32),
              jax.ShapeDtypeStruct((rows, nkv), F32)]
    return specs, shapes


def _swa_cached_mixer(h2, w_in, k_cache, v_cache, bias, sinks, tq):
    rows, d = h2.shape
    nkv = N_KV * HEAD_DIM
    nw = N_HEADS * HEAD_DIM + 2 * nkv
    n_blocks = ROW_TILE // tq
    cache_spec = pl.BlockSpec((n_blocks, WINDOW, nkv), lambda i: (i, 0, 0))
    out_specs, out_shape = _swa_outputs(rows)
    return pl.pallas_call(
        functools.partial(_swa_cached_kernel, tq=tq),
        grid=(rows // ROW_TILE,),
        in_specs=[pl.BlockSpec((ROW_TILE, d), lambda i: (i, 0))] + _swa_weight_specs(w_in, d)
        + [cache_spec, cache_spec,
           pl.BlockSpec((N_HEADS * tq, 2 * WINDOW), lambda i: (0, 0)),
           pl.BlockSpec((N_HEADS * tq, 1), lambda i: (0, 0))],
        out_specs=out_specs,
        out_shape=out_shape,
        scratch_shapes=[pltpu.VMEM((d, nw), BF16)],
        compiler_params=_cparams(1),
        name="swa_cached",
    )(h2, w_in, w_in, k_cache, v_cache, bias, sinks)


def _swa_stream_mixer(h2, w_in, bias_t, sinks_t, blocks_per_seq):
    rows, d = h2.shape
    nkv = N_KV * HEAD_DIM
    nw = N_HEADS * HEAD_DIM + 2 * nkv
    lanes = PAIRS_PER_KV * WINDOW
    out_specs, out_shape = _swa_outputs(rows)
    return pl.pallas_call(
        functools.partial(_swa_stream_kernel, blocks_per_seq=blocks_per_seq),
        grid=(rows // ROW_TILE,),
        in_specs=[pl.BlockSpec((ROW_TILE, d), lambda i: (i, 0))] + _swa_weight_specs(w_in, d)
        + [pl.BlockSpec((N_KV, 2, 2 * WINDOW, lanes), lambda i: (0, 0, 0, 0)),
           pl.BlockSpec((N_KV, 2, 1, lanes), lambda i: (0, 0, 0, 0))],
        out_specs=out_specs,
        out_shape=out_shape,
        scratch_shapes=[pltpu.VMEM((d, nw), BF16), pltpu.VMEM((WINDOW, nkv), F32),
                        pltpu.VMEM((nkv, WINDOW), F32)],
        compiler_params=_cparams(1),
        name="swa_stream",
    )(h2, w_in, w_in, bias_t, sinks_t)


def _t5_bucket(dist):
    max_exact = N_BUCKETS // 2
    dd = np.maximum(dist, 1)
    large = max_exact + (np.log(dd / max_exact) / np.log(WINDOW / max_exact)
                         * (N_BUCKETS - max_exact)).astype(np.int64)
    large = np.minimum(large, N_BUCKETS - 1)
    return np.where(dist < max_exact, dist, large).astype(np.int32)


def _attention_bias(rel_bias):
    by_dist = jnp.take(rel_bias.astype(F32), _t5_bucket(np.arange(WINDOW)), axis=0).T
    neg = jnp.full((N_HEADS, WINDOW), NEG_INF, F32)
    line = jnp.concatenate([neg, by_dist[:, ::-1], neg[:, :WINDOW - 1]], axis=1)
    return jnp.stack([line[:, WINDOW - 1 - q:3 * WINDOW - 1 - q] for q in range(WINDOW)], axis=1)


def kernel(x_prompt, x_sample, state_pool, state_conv, cache_swa_k, cache_swa_v, c_prompt, c_sample, w_ada, b_ada, norm_g, final_norm_g, w_in_even, w_out_even, w_pool, pool_scale, conv_w, w_in_odd, w_out_odd, gm_norm_g, gm_norm_b, gm_w_s, gm_b_s, attn_sinks, rel_bias, w_router, b_router, w_gate, w_up, w_down):
    d = D_MODEL
    bp, tp, _ = x_prompt.shape
    bs, ts, _ = x_sample.shape
    rows_s = bs * ts
    assert rows_s == ROW_TILE and tp % ROUTER_TILE == 0 and PAST_LEN % CHUNK == 0
    assert bp <= V7X_SUBLANES and CHUNK % ts == 0

    n_c = bp + bs
    c_pad = (-n_c) % V7X_SUBLANES
    c_all = jnp.concatenate([c_prompt, c_sample, jnp.zeros((c_pad, d), F32)], axis=0)
    mod_p = _adaln(c_all, w_ada, b_ada)
    mod_s = jnp.repeat(mod_p[:, bp:bp + bs], ts, axis=1)

    tps_p = tp // ROW_TILE
    xp = x_prompt.reshape(bp * tp, d)
    xs_ = x_sample.reshape(rows_s, d)
    w_in0, w_in1 = w_in_even[0], w_in_odd[0]

    zero_pool = jnp.zeros((bp, POOL_STATE, POOL_WIDTH), F32)
    zero_conv = jnp.zeros((bp, CONV_K - 1, CONV_WIDTH), F32)
    hp0, ya_p, pool_p = _pool_mixer(xp, norm_g[0, 0], mod_p, 0, w_in0, w_pool[0], pool_scale[0],
                                    zero_pool, 1, ROW_TILE, 0)
    hs0, ya_s, pool_s = _pool_mixer(xs_, norm_g[0, 0], mod_s, 0, w_in0, w_pool[0], pool_scale[0],
                                    state_pool[0], bs, ts, PAST_LEN)
    yb_p, conv_p = _conv_mixer(hp0, w_in0, conv_w[0], zero_conv, 1, ROW_TILE)
    yb_s, conv_s = _conv_mixer(hs0, w_in0, conv_w[0], state_conv[0], bs, ts)
    x1p, hpp = _outproj(ya_p, yb_p, xp, mod_p, 0, norm_g[0, 1], w_out_even[0], tps_p)
    x1s, hps = _outproj(ya_s, yb_s, xs_, mod_s, 0, norm_g[0, 1], w_out_even[0], 1)
    (x2p, h1p), (x2s, h1s) = _moe(hpp, hps, x1p, x1s, mod_p, mod_s, 0, norm_g[1, 0], tps_p,
                                  w_router, b_router, w_gate, w_up, w_down, final=False)

    bs_t = gm_b_s[0].T
    (yc_p,) = _gmlp_mixer(h1p, w_in1, gm_norm_g[0], gm_norm_b[0], gm_w_s[0], bs_t, CHUNK, CHUNK, False)
    yc_s, gv_s = _gmlp_mixer(h1s, w_in1, gm_norm_g[0], gm_norm_b[0], gm_w_s[0], bs_t, rows_s, ts, True)
    bias = _attention_bias(rel_bias)
    nkv = N_KV * HEAD_DIM
    bias_t = jnp.transpose(bias.reshape(N_KV, PAIRS_PER_KV, 2, WINDOW, 2 * WINDOW), (0, 2, 4, 1, 3))
    bias_t = bias_t.reshape(N_KV, 2, 2 * WINDOW, PAIRS_PER_KV * WINDOW)
    sinks_t = jnp.transpose(attn_sinks[0].reshape(N_KV, PAIRS_PER_KV, 2), (0, 2, 1))
    sinks_t = jnp.repeat(sinks_t, WINDOW, axis=-1).reshape(N_KV, 2, 1, PAIRS_PER_KV * WINDOW)
    yd_p, k_p, v_p = _swa_stream_mixer(h1p, w_in1, bias_t, sinks_t, tp // WINDOW)
    yd_s, k_s, v_s = _swa_cached_mixer(h1s, w_in1, cache_swa_k[0].reshape(bs, WINDOW, nkv),
                                       cache_swa_v[0].reshape(bs, WINDOW, nkv),
                                       bias[:, :ts, :].reshape(N_HEADS * ts, 2 * WINDOW),
                                       jnp.repeat(attn_sinks[0], ts).reshape(-1, 1), ts)
    x1p, hpp = _outproj(yc_p, yd_p, x2p, mod_p, 1, norm_g[1, 1], w_out_odd[0], tps_p)
    x1s, hps = _outproj(yc_s, yd_s, x2s, mod_s, 1, norm_g[1, 1], w_out_odd[0], 1)
    (yp,), (ys_out,) = _moe(hpp, hps, x1p, x1s, mod_p, mod_s, 1, final_norm_g, tps_p,
                            w_router, b_router, w_gate, w_up, w_down, final=True)

    k_p4 = k_p.reshape(bp, tp, N_KV, HEAD_DIM)[:, -WINDOW:]
    v_p4 = v_p.reshape(bp, tp, N_KV, HEAD_DIM)[:, -WINDOW:]
    k_s4 = jnp.concatenate([cache_swa_k[0], k_s.reshape(bs, ts, N_KV, HEAD_DIM)], axis=1)[:, -WINDOW:]
    v_s4 = jnp.concatenate([cache_swa_v[0], v_s.reshape(bs, ts, N_KV, HEAD_DIM)], axis=1)[:, -WINDOW:]
    return (yp.reshape(bp, tp, d), ys_out.reshape(bs, ts, d),
            pool_p[None], pool_s[None], conv_p[None], conv_s[None],
            k_p4[None], k_s4[None], v_p4[None], v_s4[None],
            gv_s.reshape(bs, ts, GM_WIDTH)[None])
```

```python
import functools

import numpy as np
import jax
import jax.numpy as jnp
from jax import lax
from jax.experimental import pallas as pl
from jax.experimental.pallas import tpu as pltpu

F32 = jnp.float32
BF16 = jnp.bfloat16
I32 = jnp.int32
U32 = jnp.uint32

D_MODEL = 2048
POOL_WINDOWS = (2, 4, 8, 16)
POOL_WIDTH = 1024
POOL_GROUP = 256
POOL_STATE = 15
CONV_WIDTH = 1024
CONV_K = 3
GM_WIDTH = 1024
GM_GROUPS = 8
GM_GROUP = 128
CHUNK = 128
HEAD_DIM = 64
N_HEADS = 16
N_KV = 2
WINDOW = 128
N_BUCKETS = 32
N_EXPERTS = 16
N_EXPERT_GROUPS = 4
EXP_PER_GROUP = 4
TOP_K = 2
EPS = 1e-6
NEG_INF = -1e30
PAST_LEN = 16384

V7X_SUBLANES = 8
V7X_LANES = 128
VMEM_LIMIT = 56 * 1024 * 1024

ROW_TILE = 256
ROUTER_TILE = 1024
POOL_HALO = 16
CONV_HALO = 8
MOE_TILE = 256
TAB_EXPERT, TAB_VALID, TAB_LAST_TILE, TAB_NUSED, TAB_NEXT = 0, 1, 2, 3, 4


def _cparams(n_axes):
    return pltpu.CompilerParams(dimension_semantics=("arbitrary",) * n_axes,
                                vmem_limit_bytes=VMEM_LIMIT)


def _rms(x, g):
    return x * lax.rsqrt(jnp.mean(x * x, axis=-1, keepdims=True) + EPS) * g


def _mod_spec(mod, layer, part):
    nrow = ROW_TILE if mod.shape[1] == ROW_TILE else V7X_SUBLANES
    return pl.BlockSpec((1, nrow, D_MODEL), lambda *_: (layer, 0, part))


def _mod_rows(m_ref, seq):
    if m_ref.shape[1] == V7X_SUBLANES:
        return m_ref[0, pl.ds(seq, 1), :]
    return m_ref[0]


def _adaln_kernel(c_ref, w_ref, b_ref, o_ref):
    c = c_ref[...]
    a = (c * jax.nn.sigmoid(c)).astype(BF16)
    o_ref[0] = jnp.dot(a, w_ref[0].astype(BF16), preferred_element_type=F32) + b_ref[0]


def _adaln(c_all, w_ada, b_ada):
    depth, d, n6 = w_ada.shape
    m = c_all.shape[0]
    tn = 1024
    return pl.pallas_call(
        _adaln_kernel,
        grid=(depth, n6 // tn),
        in_specs=[pl.BlockSpec((m, d), lambda l, j: (0, 0)),
                  pl.BlockSpec((1, d, tn), lambda l, j: (l, 0, j)),
                  pl.BlockSpec((1, 1, tn), lambda l, j: (l, 0, j))],
        out_specs=pl.BlockSpec((1, m, tn), lambda l, j: (l, 0, j)),
        out_shape=jax.ShapeDtypeStruct((depth, m, n6), F32),
        compiler_params=_cparams(2),
        name="adaln",
    )(c_all, w_ada, b_ada.reshape(depth, 1, n6))


def _pool_kernel(x_ref, g_ref, sc_ref, sh_ref, w_ref, wp_ref, ps_ref, st_ref, h_ref, ya_ref, ns_ref,
                 wbf, wpbf, carry, *, nb, tm, tiles_per_seq, start):
    i = pl.program_id(0)
    t = i % tiles_per_seq
    seq = i // tiles_per_seq
    c = POOL_WIDTH
    halo = POOL_HALO

    @pl.when(i == 0)
    def _():
        wbf[...] = w_ref[...].astype(BF16)
        wpbf[...] = wp_ref[...].astype(BF16)

    @pl.when(t == 0)
    def _():
        carry[...] = st_ref[...]

    h = (_rms(x_ref[...], g_ref[...]) * (1.0 + _mod_rows(sc_ref, seq)) + _mod_rows(sh_ref, seq)).astype(BF16)
    h_ref[...] = h
    p = jnp.dot(h, wbf[...], preferred_element_type=F32)
    p3 = p.reshape(nb, tm, c)
    ext3 = jnp.concatenate([carry[...], p3], axis=1)
    tail = ext3[:, tm:tm + halo, :]
    ns_ref[...] = tail
    carry[...] = tail
    ext = ext3.reshape(nb * (halo + tm), c)
    pos = start + t * tm + lax.broadcasted_iota(I32, (1, tm, 1), 1)
    outs = []
    for gi, w in enumerate(POOL_WINDOWS):
        sl = slice(gi * POOL_GROUP, (gi + 1) * POOL_GROUP)
        acc = ext[:, sl]
        shift = 1
        while shift < w:
            acc = acc + pltpu.roll(acc, shift, 0)
            shift *= 2
        win = acc.reshape(nb, halo + tm, POOL_GROUP)[:, halo:, :]
        cnt = jnp.minimum(pos + 1, w).astype(F32)
        dgrp = win / cnt - p3[:, :, sl]
        outs.append(jnp.dot(dgrp.reshape(nb * tm, POOL_GROUP).astype(BF16), wpbf[gi],
                            preferred_element_type=F32))
    y = jnp.concatenate(outs, axis=-1) * ps_ref[...]
    ya_ref[...] = y.astype(BF16)


def _pool_mixer(x2, g, mod, layer, w_in, w_pool, pool_scale, state, nb, tm, start):
    rows, d = x2.shape
    nseq = state.shape[0]
    tiles_per_seq = (rows // nseq) // tm
    seq_blocks = nseq // nb
    c = POOL_WIDTH
    st = jnp.pad(state, ((0, 0), (POOL_HALO - POOL_STATE, 0), (0, 0)))
    kern = functools.partial(_pool_kernel, nb=nb, tm=tm, tiles_per_seq=tiles_per_seq, start=start)
    h2, ya, ns = pl.pallas_call(
        kern,
        grid=(seq_blocks * tiles_per_seq,),
        in_specs=[pl.BlockSpec((nb * tm, d), lambda i: (i, 0)),
                  pl.BlockSpec((1, d), lambda i: (0, 0)),
                  _mod_spec(mod, layer, 1), _mod_spec(mod, layer, 0),
                  pl.BlockSpec((d, c), lambda i: (0, 0)),
                  pl.BlockSpec((len(POOL_WINDOWS), POOL_GROUP, POOL_GROUP), lambda i: (0, 0, 0)),
                  pl.BlockSpec((1, c), lambda i: (0, 0)),
                  pl.BlockSpec((nb, POOL_HALO, c), lambda i: (i // tiles_per_seq, 0, 0))],
        out_specs=[pl.BlockSpec((nb * tm, d), lambda i: (i, 0)),
                   pl.BlockSpec((nb * tm, c), lambda i: (i, 0)),
                   pl.BlockSpec((nb, POOL_HALO, c), lambda i: (i // tiles_per_seq, 0, 0))],
        out_shape=[jax.ShapeDtypeStruct((rows, d), BF16),
                   jax.ShapeDtypeStruct((rows, c), BF16),
                   jax.ShapeDtypeStruct((nseq, POOL_HALO, c), F32)],
        scratch_shapes=[pltpu.VMEM((d, c), BF16),
                        pltpu.VMEM((len(POOL_WINDOWS), POOL_GROUP, POOL_GROUP), BF16),
                        pltpu.VMEM((nb, POOL_HALO, c), F32)],
        compiler_params=_cparams(1),
        name="pool_mixer",
    )(x2, g.reshape(1, d), mod, mod, w_in, w_pool, pool_scale.reshape(1, c), st)
    return h2, ya, ns[:, POOL_HALO - POOL_STATE:, :]


def _conv_kernel(h_ref, wx_ref, wb_ref, wc_ref, cw_ref, st_ref, yb_ref, ns_ref,
                 wxbf, wbbf, wcbf, carry, *, nb, tm, tiles_per_seq):
    i = pl.program_id(1)
    t = i % tiles_per_seq
    tc = wxbf.shape[1]
    halo = CONV_HALO

    @pl.when(i == 0)
    def _():
        wxbf[...] = wx_ref[...].astype(BF16)
        wbbf[...] = wb_ref[...].astype(BF16)
        wcbf[...] = wc_ref[...].astype(BF16)

    @pl.when(t == 0)
    def _():
        carry[...] = st_ref[...]

    h = h_ref[...]
    xin = jnp.dot(h, wxbf[...], preferred_element_type=F32)
    gb = jnp.dot(h, wbbf[...], preferred_element_type=F32)
    gc = jnp.dot(h, wcbf[...], preferred_element_type=F32)
    z3 = (gc * xin).reshape(nb, tm, tc)
    ext3 = jnp.concatenate([carry[...], z3], axis=1)
    tail = ext3[:, tm:tm + halo, :]
    ns_ref[...] = tail
    carry[...] = tail
    ext = ext3.reshape(nb * (halo + tm), tc)
    cw = cw_ref[...]
    conv = cw[0:1, :] * pltpu.roll(ext, 2, 0) + cw[1:2, :] * pltpu.roll(ext, 1, 0) + cw[2:3, :] * ext
    conv = conv.reshape(nb, halo + tm, tc)[:, halo:, :].reshape(nb * tm, tc)
    yb_ref[...] = (gb * conv).astype(BF16)


def _conv_mixer(h2, w_in, conv_w, state, nb, tm):
    rows, d = h2.shape
    nseq = state.shape[0]
    tiles_per_seq = (rows // nseq) // tm
    seq_blocks = nseq // nb
    c = CONV_WIDTH
    tc = 512
    cb = c // tc
    base = POOL_WIDTH // tc
    st = jnp.pad(state, ((0, 0), (CONV_HALO - (CONV_K - 1), 0), (0, 0)))
    kern = functools.partial(_conv_kernel, nb=nb, tm=tm, tiles_per_seq=tiles_per_seq)
    yb, ns = pl.pallas_call(
        kern,
        grid=(cb, seq_blocks * tiles_per_seq),
        in_specs=[pl.BlockSpec((nb * tm, d), lambda j, i: (i, 0)),
                  pl.BlockSpec((d, tc), lambda j, i: (0, base + j)),
                  pl.BlockSpec((d, tc), lambda j, i: (0, base + cb + j)),
                  pl.BlockSpec((d, tc), lambda j, i: (0, base + 2 * cb + j)),
                  pl.BlockSpec((CONV_K, tc), lambda j, i: (0, j)),
                  pl.BlockSpec((nb, CONV_HALO, tc), lambda j, i: (i // tiles_per_seq, 0, j))],
        out_specs=[pl.BlockSpec((nb * tm, tc), lambda j, i: (i, j)),
                   pl.BlockSpec((nb, CONV_HALO, tc), lambda j, i: (i // tiles_per_seq, 0, j))],
        out_shape=[jax.ShapeDtypeStruct((rows, c), BF16),
                   jax.ShapeDtypeStruct((nseq, CONV_HALO, c), F32)],
        scratch_shapes=[pltpu.VMEM((d, tc), BF16)] * 3 + [pltpu.VMEM((nb, CONV_HALO, tc), F32)],
        compiler_params=_cparams(2),
        name="conv_mixer",
    )(h2, w_in, w_in, w_in, conv_w, st)
    return yb, ns[:, CONV_HALO - (CONV_K - 1):, :]


def _pack_bf16_pairs(v):
    c = v.shape[1] // 2
    r = v.astype(BF16).astype(F32)
    lo = pltpu.bitcast(r[:, :c], U32)
    hi = pltpu.bitcast(r[:, c:], U32)
    return (hi & jnp.uint32(0xFFFF0000)) | (lo >> 16)


def _store_token_tiles(ref, v):
    rows = v.shape[0]
    for j in range(V7X_SUBLANES):
        ref[pl.ds(j, rows, stride=V7X_SUBLANES), :] = v[:, j * V7X_LANES:(j + 1) * V7X_LANES]


def _load_token_tiles(ref):
    rows = ref.shape[0] // V7X_SUBLANES
    return jnp.concatenate([ref[pl.ds(j, rows, stride=V7X_SUBLANES), :] for j in range(V7X_SUBLANES)],
                           axis=-1)


def _unpack_pairs_f32(w):
    return pltpu.bitcast(w << 16, F32), pltpu.bitcast(w & jnp.uint32(0xFFFF0000), F32)


def _unpack_bf16_pairs(w):
    lo, hi = _unpack_pairs_f32(w)
    return lo.astype(BF16), hi.astype(BF16)


def _outproj_kernel(ya_ref, yb_ref, x_ref, g1_ref, sc_ref, sh_ref, ng_ref, wo_ref,
                    x1_ref, hp_ref, hpt_ref, wobf, *, tiles_per_seq):
    i = pl.program_id(0)
    seq = i // tiles_per_seq
    half = ya_ref.shape[1]

    @pl.when(i == 0)
    def _():
        wobf[...] = wo_ref[...].astype(BF16)

    y = (jnp.dot(ya_ref[...], wobf[:half, :], preferred_element_type=F32)
         + jnp.dot(yb_ref[...], wobf[half:, :], preferred_element_type=F32))
    x1 = x_ref[...] + _mod_rows(g1_ref, seq) * y
    x1_ref[...] = x1
    h2 = _rms(x1, ng_ref[...]) * (1.0 + _mod_rows(sc_ref, seq)) + _mod_rows(sh_ref, seq)
    packed = _pack_bf16_pairs(h2)
    hp_ref[...] = packed
    _store_token_tiles(hpt_ref, packed)


def _outproj(ya, yb, x2, mod, layer, ng, w_out, tiles_per_seq):
    rows_all, d = x2.shape
    half = ya.shape[1]
    row_spec = lambda w: pl.BlockSpec((ROW_TILE, w), lambda i: (i, 0))
    return pl.pallas_call(
        functools.partial(_outproj_kernel, tiles_per_seq=tiles_per_seq),
        grid=(rows_all // ROW_TILE,),
        in_specs=[row_spec(half), row_spec(half), row_spec(d),
                  _mod_spec(mod, layer, 2), _mod_spec(mod, layer, 4), _mod_spec(mod, layer, 3),
                  pl.BlockSpec((1, d), lambda i: (0, 0)), pl.BlockSpec((d, d), lambda i: (0, 0))],
        out_specs=[row_spec(d), row_spec(d // 2),
                   pl.BlockSpec((ROW_TILE * V7X_SUBLANES, V7X_LANES), lambda i: (i, 0))],
        out_shape=[jax.ShapeDtypeStruct((rows_all, d), F32),
                   jax.ShapeDtypeStruct((rows_all, d // 2), U32),
                   jax.ShapeDtypeStruct((rows_all * V7X_SUBLANES, V7X_LANES), U32)],
        scratch_shapes=[pltpu.VMEM((d, d), BF16)],
        compiler_params=_cparams(1),
        name="outproj",
    )(ya, yb, x2, mod, mod, mod, ng.reshape(1, d), w_out)


def _router_kernel(hpp_ref, hps_ref, wr_ref, br_ref, pos_ref, rw_ref, tab_ref,
                   cnt_acc, totals, starts, padded, *, nt_p, rows_s):
    ph = pl.program_id(0)
    t = pl.program_id(1)
    last = nt_p
    r = hpp_ref.shape[0]
    half = hpp_ref.shape[1]
    ne = N_EXPERTS
    sub = lax.broadcasted_iota(I32, (ne, V7X_LANES), 0)

    @pl.when(t == 0)
    def _():
        cnt_acc[...] = jnp.zeros_like(cnt_acc)

    @pl.when((ph == 0) & (t == 0))
    def _():
        starts[...] = jnp.zeros_like(starts)
        padded[...] = jnp.zeros_like(padded)

    @pl.when((ph == 1) & (t == 0))
    def _():
        pad = jnp.floor((totals[...] + (MOE_TILE - 1.0)) * (1.0 / MOE_TILE)) * MOE_TILE
        run = pad
        k = 1
        while k < ne:
            run = run + jnp.where(sub >= k, pltpu.roll(run, k, 0), 0.0)
            k *= 2
        padded[...] = pad
        starts[...] = run - pad

    is_s = t == last
    w_s = jnp.concatenate([hps_ref[...], jnp.zeros((r - rows_s, half), U32)], axis=0)
    w = jnp.where(is_s, w_s, hpp_ref[...])
    lo, hi = _unpack_bf16_pairs(w)
    wr = wr_ref[...].astype(BF16)
    nt_dims = (((1,), (1,)), ((), ()))
    log_t = (lax.dot_general(wr[:, :half], lo, nt_dims, preferred_element_type=F32)
             + lax.dot_general(wr[:, half:], hi, nt_dims, preferred_element_type=F32))

    s = jax.nn.sigmoid(log_t)
    sg = s + br_ref[...]
    eid = lax.broadcasted_iota(I32, (ne, r), 0)
    within = eid % EXP_PER_GROUP
    grp = eid // EXP_PER_GROUP

    def group_rot(x, k):
        return jnp.where(within + k < EXP_PER_GROUP,
                         pltpu.roll(x, ne - k, 0), pltpu.roll(x, EXP_PER_GROUP - k, 0))

    rank = jnp.zeros((ne, r), I32)
    for k in range(1, EXP_PER_GROUP):
        mate = group_rot(sg, k)
        wrapped = within + k >= EXP_PER_GROUP
        ahead = (mate > sg) | (wrapped & (mate == sg))
        rank = rank + ahead.astype(I32)
    top2 = rank < TOP_K
    kept = jnp.where(top2, sg, 0.0)
    gscore = kept
    for k in range(1, EXP_PER_GROUP):
        gscore = gscore + group_rot(kept, k)
    win = None
    for k in range(1, N_EXPERT_GROUPS):
        other = pltpu.roll(gscore, EXP_PER_GROUP * k, 0)
        beats = (gscore > other) | ((grp < k) & (gscore == other))
        win = beats if win is None else (win & beats)
    n_valid = jnp.where(is_s, rows_s, r)
    tok = lax.broadcasted_iota(I32, (ne, r), 1)
    sel = top2 & win & (tok < n_valid)
    picked = jnp.where(sel, s, 0.0)
    wsum = jnp.sum(picked, axis=0, keepdims=True)
    gate = picked / jnp.where(tok[0:1, :] < n_valid, wsum, 1.0)

    src = lax.broadcasted_iota(I32, (r, r), 0)
    dst = lax.broadcasted_iota(I32, (r, r), 1)
    before = (src < dst).astype(BF16)
    selb = sel.astype(F32)
    ranks = jnp.dot(selb.astype(BF16), before, preferred_element_type=F32)
    slot = (starts[...][:, 0:1] + cnt_acc[...][:, 0:1] + ranks).astype(I32)
    cnt_new = cnt_acc[...] + jnp.sum(selb, axis=1, keepdims=True)
    cnt_acc[...] = cnt_new

    @pl.when((ph == 0) & (t == last))
    def _():
        totals[...] = cnt_new

    e_a = jnp.min(jnp.where(sel, eid, ne), axis=0, keepdims=True)
    e_b = jnp.max(jnp.where(sel, eid, -1), axis=0, keepdims=True)
    is_a = sel & (eid == e_a)
    is_b = sel & (eid == e_b)
    pos_a = jnp.sum(jnp.where(is_a, slot, 0), axis=0, keepdims=True)
    pos_b = jnp.sum(jnp.where(is_b, slot, 0), axis=0, keepdims=True)
    w_a = jnp.sum(jnp.where(is_a, gate, 0.0), axis=0, keepdims=True)
    w_b = jnp.sum(jnp.where(is_b, gate, 0.0), axis=0, keepdims=True)
    pos_ref[0] = jnp.concatenate([pos_a, pos_b], axis=0)
    wmat = jnp.concatenate([w_a, w_b, jnp.zeros((V7X_LANES - 2, r), F32)], axis=0)
    rw_ref[...] = wmat.T

    @pl.when((ph == 1) & (t == last))
    def _():
        ends = starts[...] + padded[...]
        lane = lax.broadcasted_iota(I32, (ne, V7X_LANES), 1)
        tile_start = (lane * MOE_TILE).astype(F32)
        te = jnp.sum((tile_start >= ends).astype(I32), axis=0, keepdims=True)
        valid = te < ne
        last_e = jnp.max(jnp.where(padded[...] > 0.0, sub, 0), axis=0, keepdims=True)
        te = jnp.where(valid, te, last_e)
        n_used = jnp.sum(valid.astype(I32), axis=1, keepdims=True) + jnp.zeros((1, V7X_LANES), I32)
        last_tile = jnp.where(padded[...] > 0.0, ends - MOE_TILE, -1.0).astype(I32)
        last_tile_row = jnp.sum(jnp.where(sub == lane, last_tile, 0), axis=0, keepdims=True)
        later = jnp.min(jnp.where((sub > te) & (padded[...] > 0.0), sub, ne), axis=0, keepdims=True)
        next_e = jnp.where(later < ne, later, -1)
        zero = jnp.zeros((1, V7X_LANES), I32)
        tab_ref[...] = jnp.concatenate([te, valid.astype(I32), last_tile_row, n_used, next_e,
                                        zero, zero, zero], axis=0)


def _router(hp_p, hp_s, w_router, b_router):
    n_p, half = hp_p.shape
    rows_s = hp_s.shape[0]
    r = ROUTER_TILE
    nt_p = n_p // r
    nt = nt_p + 1
    kern = functools.partial(_router_kernel, nt_p=nt_p, rows_s=rows_s)
    pos, rw, tab = pl.pallas_call(
        kern,
        grid=(2, nt),
        in_specs=[pl.BlockSpec((r, half), lambda p, t: (jnp.minimum(t, nt_p - 1), 0)),
                  pl.BlockSpec((rows_s, half), lambda p, t: (0, 0)),
                  pl.BlockSpec((N_EXPERTS, 2 * half), lambda p, t: (0, 0)),
                  pl.BlockSpec((N_EXPERTS, 1), lambda p, t: (0, 0))],
        out_specs=[pl.BlockSpec((1, TOP_K, r), lambda p, t: (p * t, 0, 0)),
                   pl.BlockSpec((r, V7X_LANES), lambda p, t: (p * t, 0)),
                   pl.BlockSpec((V7X_SUBLANES, V7X_LANES), lambda p, t: (0, 0))],
        out_shape=[jax.ShapeDtypeStruct((nt, TOP_K, r), I32),
                   jax.ShapeDtypeStruct((nt * r, V7X_LANES), F32),
                   jax.ShapeDtypeStruct((V7X_SUBLANES, V7X_LANES), I32)],
        scratch_shapes=[pltpu.VMEM((N_EXPERTS, V7X_LANES), F32)] * 4,
        compiler_params=_cparams(2),
        name="router",
    )(hp_p, hp_s, w_router.T, b_router.reshape(N_EXPERTS, 1))
    return pos.reshape(-1), rw, tab.reshape(-1)


def _pos_index(tok0):
    return (tok0 // ROUTER_TILE) * (TOP_K * ROUTER_TILE) + tok0 % ROUTER_TILE


def _tokens(ref, first, n=1):
    start = pl.multiple_of(first * V7X_SUBLANES, V7X_SUBLANES)
    return ref.at[pl.ds(start, n * V7X_SUBLANES), :]


def _dispatch_kernel(pos_ref, tab_ref, hpp_ref, hps_ref, xs_ref, zbuf, sem, *, n_p_steps):
    i = pl.program_id(0)
    rows = hpp_ref.shape[0] // V7X_SUBLANES

    @pl.when(i == 0)
    def _():
        zbuf[...] = jnp.zeros_like(zbuf)

        def fill(e):
            first = pl.multiple_of(tab_ref[TAB_LAST_TILE * V7X_LANES + e], MOE_TILE)
            return pltpu.make_async_copy(zbuf, _tokens(xs_ref, first, MOE_TILE), sem)

        for e in range(N_EXPERTS):
            @pl.when(tab_ref[TAB_LAST_TILE * V7X_LANES + e] >= 0)
            def _():
                fill(e).start()
        for e in range(N_EXPERTS):
            @pl.when(tab_ref[TAB_LAST_TILE * V7X_LANES + e] >= 0)
            def _():
                fill(e).wait()

        def tail(j):
            first = pl.multiple_of(j * MOE_TILE, MOE_TILE)
            return pltpu.make_async_copy(zbuf, _tokens(xs_ref, first, MOE_TILE), sem)

        def tail_start(j, carry):
            tail(j).start()
            return carry

        def tail_wait(j, carry):
            tail(j).wait()
            return carry

        n_used = tab_ref[TAB_NUSED * V7X_LANES]
        n_tiles = xs_ref.shape[0] // (MOE_TILE * V7X_SUBLANES)
        lax.fori_loop(n_used, n_tiles, tail_start, 0)
        lax.fori_loop(n_used, n_tiles, tail_wait, 0)

    base = _pos_index(i * rows)

    def scatter(src_ref):
        def row_copy(r, dst):
            return pltpu.make_async_copy(_tokens(src_ref, r), _tokens(xs_ref, dst), sem)

        def issue(r, carry):
            row_copy(r, pos_ref[base + r]).start()
            row_copy(r, pos_ref[base + ROUTER_TILE + r]).start(priority=1)
            return carry

        lax.fori_loop(0, rows, issue, 0, unroll=8)
        block = pltpu.make_async_copy(src_ref, _tokens(xs_ref, 0, rows), sem)
        for _ in range(TOP_K):
            block.wait()

    @pl.when(i < n_p_steps)
    def _():
        scatter(hpp_ref)

    @pl.when(i == n_p_steps)
    def _():
        scatter(hps_ref)


def _dispatch(pos, tab, hpt_p, hpt_s, n_rows_sorted):
    sub = V7X_SUBLANES
    n_p_steps = hpt_p.shape[0] // (ROW_TILE * sub)
    assert hpt_s.shape[0] == ROW_TILE * sub
    kern = functools.partial(_dispatch_kernel, n_p_steps=n_p_steps)
    blk = (ROW_TILE * sub, V7X_LANES)
    return pl.pallas_call(
        kern,
        grid_spec=pltpu.PrefetchScalarGridSpec(
            num_scalar_prefetch=2,
            grid=(n_p_steps + 1,),
            in_specs=[pl.BlockSpec(blk, lambda i, p, t: (jnp.minimum(i, n_p_steps - 1), 0)),
                      pl.BlockSpec(blk, lambda i, p, t: (0, 0))],
            out_specs=pl.BlockSpec(memory_space=pl.ANY),
            scratch_shapes=[pltpu.VMEM((MOE_TILE * sub, V7X_LANES), U32), pltpu.SemaphoreType.DMA(())]),
        out_shape=jax.ShapeDtypeStruct((n_rows_sorted * sub, V7X_LANES), U32),
        compiler_params=_cparams(1),
        name="moe_dispatch",
    )(pos, tab, hpt_p, hpt_s)


def _experts_kernel(tab_ref, xs_ref, wg_hbm, wu_hbm, wd_hbm, ys_ref,
                    wg32, wu32, wd32, wgbf, wubf, wdbf, slot_ref, sems, *, layer):
    i = pl.program_id(0)
    expert = tab_ref[TAB_EXPERT * V7X_LANES + i]
    prev = tab_ref[TAB_EXPERT * V7X_LANES + jnp.maximum(i - 1, 0)]
    upcoming = tab_ref[TAB_NEXT * V7X_LANES + i]
    changed = (i == 0) | (expert != prev)
    half = V7X_SUBLANES * V7X_LANES

    def weight_copies(e, slot):
        return (pltpu.make_async_copy(wg_hbm.at[layer, e], wg32.at[slot], sems.at[0, slot]),
                pltpu.make_async_copy(wu_hbm.at[layer, e], wu32.at[slot], sems.at[1, slot]),
                pltpu.make_async_copy(wd_hbm.at[layer, e], wd32.at[slot], sems.at[2, slot]))

    @pl.when(i == 0)
    def _():
        slot_ref[0] = 0
        for cp in weight_copies(expert, 0):
            cp.start()

    @pl.when(changed & (i > 0))
    def _():
        slot_ref[0] = 1 - slot_ref[0]

    for slot in range(2):
        @pl.when(changed & (slot_ref[0] == slot))
        def _():
            for cp in weight_copies(expert, slot):
                cp.wait()
            wgbf[...] = wg32[slot].astype(BF16)
            wubf[...] = wu32[slot].astype(BF16)
            wdbf[...] = wd32[slot].astype(BF16)

            @pl.when(upcoming >= 0)
            def _():
                for cp in weight_copies(upcoming, 1 - slot):
                    cp.start(priority=1)

    @pl.when(tab_ref[TAB_VALID * V7X_LANES + i] > 0)
    def _():
        lo, hi = _unpack_bf16_pairs(_load_token_tiles(xs_ref))
        a = (jnp.dot(lo, wgbf[:half, :], preferred_element_type=F32)
             + jnp.dot(hi, wgbf[half:, :], preferred_element_type=F32))
        b = (jnp.dot(lo, wubf[:half, :], preferred_element_type=F32)
             + jnp.dot(hi, wubf[half:, :], preferred_element_type=F32))
        hid = (a * jax.nn.sigmoid(a)) * b
        y = jnp.dot(hid.astype(BF16), wdbf[...], preferred_element_type=F32)
        _store_token_tiles(ys_ref, _pack_bf16_pairs(y))

    @pl.when(tab_ref[TAB_VALID * V7X_LANES + i] == 0)
    def _():
        ys_ref[...] = jnp.zeros_like(ys_ref)


def _experts(tab, xs, w_gate, w_up, w_down, layer):
    sub = V7X_SUBLANES
    _, _, d, f = w_gate.shape
    nt = xs.shape[0] // (MOE_TILE * sub)
    assert nt <= V7X_LANES and d == 2 * sub * V7X_LANES
    blk = (MOE_TILE * sub, V7X_LANES)

    def tile(i, tab_ref):
        return jnp.minimum(i, tab_ref[TAB_NUSED * V7X_LANES] - 1)

    hbm = pl.BlockSpec(memory_space=pl.ANY)
    return pl.pallas_call(
        functools.partial(_experts_kernel, layer=layer),
        grid_spec=pltpu.PrefetchScalarGridSpec(
            num_scalar_prefetch=1,
            grid=(nt,),
            in_specs=[pl.BlockSpec(blk, lambda i, t: (tile(i, t), 0)), hbm, hbm, hbm],
            out_specs=pl.BlockSpec(blk, lambda i, t: (i, 0)),
            scratch_shapes=[pltpu.VMEM((2, d, f), F32), pltpu.VMEM((2, d, f), F32), pltpu.VMEM((2, f, d), F32),
                            pltpu.VMEM((d, f), BF16), pltpu.VMEM((d, f), BF16), pltpu.VMEM((f, d), BF16),
                            pltpu.SMEM((1,), I32), pltpu.SemaphoreType.DMA((3, 2))]),
        out_shape=jax.ShapeDtypeStruct(xs.shape, U32),
        compiler_params=_cparams(1),
        name="moe_experts",
    )(tab, xs, w_gate, w_up, w_down)


def _combine_kernel(pos_ref, ys_ref, x1_ref, rw_ref, g2_ref, ng_ref, sc_ref, sh_ref, *rest,
                    tok0, tiles_per_seq, n_steps, final):
    if final:
        x2_ref, buf, sems = rest
        hn_ref = None
    else:
        x2_ref, hn_ref, buf, sems = rest
    i = pl.program_id(0)
    seq = i // tiles_per_seq
    rows = x1_ref.shape[0]

    def gather(step, slot):
        base = _pos_index(tok0 + step * rows)

        def issue(r, carry):
            for k in range(TOP_K):
                src = pos_ref[base + k * ROUTER_TILE + r]
                pltpu.make_async_copy(_tokens(ys_ref, src), _tokens(buf.at[slot, k], r),
                                      sems.at[slot]).start(priority=k)
            return carry

        lax.fori_loop(0, rows, issue, 0, unroll=8)

    @pl.when(i == 0)
    def _():
        gather(0, 0)

    @pl.when(i + 1 < n_steps)
    def _():
        gather(i + 1, (i + 1) % 2)

    slot = i % 2
    for k in range(TOP_K):
        pltpu.make_async_copy(_tokens(ys_ref, 0, rows), buf.at[slot, k], sems.at[slot]).wait()

    rw = rw_ref[...]
    lo_a, hi_a = _unpack_pairs_f32(_load_token_tiles(buf.at[slot, 0]))
    lo_b, hi_b = _unpack_pairs_f32(_load_token_tiles(buf.at[slot, 1]))
    w_a = rw[:, 0:1]
    w_b = rw[:, 1:2]
    moe = jnp.concatenate([w_a * lo_a + w_b * lo_b, w_a * hi_a + w_b * hi_b], axis=-1)
    x2 = x1_ref[...] + _mod_rows(g2_ref, seq) * moe
    if final:
        x2_ref[...] = _rms(x2, ng_ref[...])
    else:
        x2_ref[...] = x2
        hn_ref[...] = (_rms(x2, ng_ref[...]) * (1.0 + _mod_rows(sc_ref, seq))
                       + _mod_rows(sh_ref, seq)).astype(BF16)


def _combine(pos, ys, x1, rw, tok0, mod, layer, ng, tiles_per_seq, final):
    n_tok, d = x1.shape
    kern = functools.partial(_combine_kernel, tok0=tok0, tiles_per_seq=tiles_per_seq,
                             n_steps=n_tok // ROW_TILE, final=final)
    rw_off = tok0 // ROW_TILE
    row_spec = lambda w: pl.BlockSpec((ROW_TILE, w), lambda i, p: (i, 0))
    out_shape = [jax.ShapeDtypeStruct((n_tok, d), F32)]
    out_specs = [row_spec(d)]
    if not final:
        out_shape.append(jax.ShapeDtypeStruct((n_tok, d), BF16))
        out_specs.append(row_spec(d))
    nxt = min(layer + 1, mod.shape[0] - 1)
    return pl.pallas_call(
        kern,
        grid_spec=pltpu.PrefetchScalarGridSpec(
            num_scalar_prefetch=1,
            grid=(n_tok // ROW_TILE,),
            in_specs=[pl.BlockSpec(memory_space=pl.ANY), row_spec(d),
                      pl.BlockSpec((ROW_TILE, V7X_LANES), lambda i, p: (rw_off + i, 0)),
                      _mod_spec(mod, layer, 5), pl.BlockSpec((1, d), lambda i, p: (0, 0)),
                      _mod_spec(mod, nxt, 1), _mod_spec(mod, nxt, 0)],
            out_specs=out_specs,
            scratch_shapes=[pltpu.VMEM((2, TOP_K, ROW_TILE * V7X_SUBLANES, V7X_LANES), U32),
                            pltpu.SemaphoreType.DMA((2,))]),
        out_shape=out_shape,
        compiler_params=_cparams(1),
        name="moe_combine_final" if final else "moe_combine",
    )(pos, ys, x1, rw, mod, ng.reshape(1, d), mod, mod)


def _moe(out_p, out_s, mod_p, mod_s, layer, ng, tps_p, w_router, b_router, w_gate, w_up, w_down, final):
    x1_p, hp_p, hpt_p = out_p
    x1_s, hp_s, hpt_s = out_s
    n_p = hp_p.shape[0]
    n_tok = n_p + hp_s.shape[0]
    max_rows = TOP_K * n_tok + N_EXPERTS * (MOE_TILE - 1)
    n_rows_sorted = -(-max_rows // MOE_TILE) * MOE_TILE
    pos, rw, tab = _router(hp_p, hp_s, w_router, b_router)
    xs = _dispatch(pos, tab, hpt_p, hpt_s, n_rows_sorted)
    ys = _experts(tab, xs, w_gate, w_up, w_down, layer)
    out_p = _combine(pos, ys, x1_p, rw, 0, mod_p, layer, ng, tps_p, final)
    out_s = _combine(pos, ys, x1_s, rw, n_p, mod_s, layer, ng, 1, final)
    return out_p, out_s


def _gmlp_kernel(h_ref, w_ref, lg_ref, lb_ref, ws_ref, bs_ref, yc_ref, *rest, ell, blk, emit_v):
    if emit_v:
        gv_ref, wbf, wsbf = rest
    else:
        wbf, wsbf = rest
    i = pl.program_id(0)
    rows = h_ref.shape[0]
    c = GM_WIDTH

    @pl.when(i == 0)
    def _():
        wbf[...] = w_ref[...].astype(BF16)
        r = lax.broadcasted_iota(I32, (ell, ell), 0)
        s = lax.broadcasted_iota(I32, (ell, ell), 1)
        keep = (r >= s) & ((r // blk) == (s // blk))
        rsel = (lax.broadcasted_iota(I32, (ell, CHUNK), 0) % blk
                == lax.broadcasted_iota(I32, (ell, CHUNK), 1)).astype(BF16)
        csel = (lax.broadcasted_iota(I32, (CHUNK, ell), 1) % blk
                == lax.broadcasted_iota(I32, (CHUNK, ell), 0)).astype(BF16)
        for g in range(GM_GROUPS):
            wchunk = ws_ref[g].astype(BF16)
            if blk == ell:
                full = wchunk
            else:
                rowsp = jnp.dot(rsel, wchunk, preferred_element_type=F32).astype(BF16)
                full = jnp.dot(rowsp, csel, preferred_element_type=F32).astype(BF16)
            wsbf[g] = jnp.where(keep, full, jnp.zeros_like(full))

    uv = jnp.dot(h_ref[...], wbf[...], preferred_element_type=F32)
    u = uv[:, :c]
    v = uv[:, c:]
    vc = v - jnp.mean(v, axis=-1, keepdims=True)
    vn = vc * lax.rsqrt(jnp.mean(vc * vc, axis=-1, keepdims=True) + EPS) * lg_ref[...] + lb_ref[...]
    if emit_v:
        gv_ref[...] = vn
    vb = vn.astype(BF16)
    bs = bs_ref[...]
    for ch in range(rows // ell):
        rs = slice(ch * ell, (ch + 1) * ell)
        outs = []
        for g in range(GM_GROUPS):
            cs = slice(g * GM_GROUP, (g + 1) * GM_GROUP)
            mixed = jnp.dot(wsbf[g], vb[rs, cs], preferred_element_type=F32)
            mixed = (mixed.reshape(ell // blk, blk, GM_GROUP) + bs[:blk, g:g + 1][None]).reshape(ell, GM_GROUP)
            outs.append(u[rs, cs] * mixed)
        yc_ref[rs, :] = jnp.concatenate(outs, axis=-1).astype(BF16)


def _gmlp_mixer(h2, w_in, ln_g, ln_b, ws, bs_t, ell, blk, emit_v):
    rows, d = h2.shape
    c = GM_WIDTH
    kern = functools.partial(_gmlp_kernel, ell=ell, blk=blk, emit_v=emit_v)
    out_specs = [pl.BlockSpec((ROW_TILE, c), lambda i: (i, 0))]
    out_shape = [jax.ShapeDtypeStruct((rows, c), BF16)]
    if emit_v:
        out_specs.append(pl.BlockSpec((ROW_TILE, c), lambda i: (i, 0)))
        out_shape.append(jax.ShapeDtypeStruct((rows, c), F32))
    return pl.pallas_call(
        kern,
        grid=(rows // ROW_TILE,),
        in_specs=[pl.BlockSpec((ROW_TILE, d), lambda i: (i, 0)),
                  pl.BlockSpec((d, 2 * c), lambda i: (0, 0)),
                  pl.BlockSpec((1, c), lambda i: (0, 0)),
                  pl.BlockSpec((1, c), lambda i: (0, 0)),
                  pl.BlockSpec((GM_GROUPS, CHUNK, CHUNK), lambda i: (0, 0, 0)),
                  pl.BlockSpec((CHUNK, GM_GROUPS), lambda i: (0, 0))],
        out_specs=out_specs,
        out_shape=out_shape,
        scratch_shapes=[pltpu.VMEM((d, 2 * c), BF16), pltpu.VMEM((GM_GROUPS, ell, ell), BF16)],
        compiler_params=_cparams(1),
        name="gmlp_mixer",
    )(h2, w_in, ln_g.reshape(1, c), ln_b.reshape(1, c), ws, bs_t)


PAIR_W = 2 * HEAD_DIM
PAIRS_PER_KV = N_HEADS // N_KV // 2
NT_DIMS = (((1,), (1,)), ((), ()))


def _swa_project(i, h_ref, wq_ref, wkv_ref, wbf):
    nq = N_HEADS * HEAD_DIM

    @pl.when(i == 0)
    def _():
        wbf[:, :nq] = wq_ref[...].astype(BF16)
        wbf[:, nq:] = wkv_ref[...].astype(BF16)

    return jnp.dot(h_ref[...], wbf[...], preferred_element_type=F32)


def _pair_block_diag(a, a_swapped, hk, axis):
    dim_axis = 1 - axis
    low = lax.broadcasted_iota(I32, a.shape, dim_axis) < HEAD_DIM
    lo, hi = (a, a_swapped) if hk == 0 else (a_swapped, a)
    return jnp.concatenate([jnp.where(low, lo, 0.0), jnp.where(low, 0.0, hi)], axis=axis).astype(BF16)


def _stack_pairs(qkv, rs, hk):
    p0 = hk * PAIRS_PER_KV
    return jnp.concatenate([qkv[rs, (p0 + pp) * PAIR_W:(p0 + pp + 1) * PAIR_W]
                            for pp in range(PAIRS_PER_KV)], axis=0).astype(BF16)


def _swa_cached_kernel(h_ref, wq_ref, wkv_ref, kp_ref, vp_ref, bias_ref, sink_ref, yd_ref, k_ref, v_ref,
                       wbf, *, tq):
    i = pl.program_id(0)
    rows = h_ref.shape[0]
    nq = N_HEADS * HEAD_DIM
    nkv = N_KV * HEAD_DIM
    n_blocks = rows // tq
    qkv = _swa_project(i, h_ref, wq_ref, wkv_ref, wbf)
    k_new = qkv[:, nq:nq + nkv]
    v_new = qkv[:, nq + nkv:]
    k_ref[...] = k_new
    v_ref[...] = v_new
    pad = jnp.zeros((WINDOW - tq, nkv), F32)

    scores, vbds = [], []
    for blk in range(n_blocks):
        rs = slice(blk * tq, (blk + 1) * tq)
        kcat = jnp.concatenate([kp_ref[blk], k_new[rs], pad], axis=0)
        vcat = jnp.concatenate([vp_ref[blk], v_new[rs], pad], axis=0)
        kswap = pltpu.roll(kcat, HEAD_DIM, 1)
        vswap = pltpu.roll(vcat, HEAD_DIM, 1)
        per_head = []
        for hk in range(N_KV):
            kbd = _pair_block_diag(kcat, kswap, hk, 0)
            vbds.append(_pair_block_diag(vcat, vswap, hk, 0))
            s4 = lax.dot_general(_stack_pairs(qkv, rs, hk), kbd, NT_DIMS,
                                 preferred_element_type=F32) * (HEAD_DIM ** -0.5)
            for pp in range(PAIRS_PER_KV):
                for sub in range(2):
                    per_head.append(s4[pp * tq:(pp + 1) * tq, sub * 2 * WINDOW:(sub + 1) * 2 * WINDOW])
        scores.append(jnp.concatenate(per_head, axis=0))

    s_all = jnp.stack(scores, axis=0) + bias_ref[...][None]
    sink = sink_ref[...][None]
    m = jnp.maximum(jnp.max(s_all, axis=-1, keepdims=True), sink)
    pr = jnp.exp(s_all - m)
    pr = pr / (jnp.sum(pr, axis=-1, keepdims=True) + jnp.exp(sink - m))

    for blk in range(n_blocks):
        outs = []
        for hk in range(N_KV):
            p4 = []
            for pp in range(PAIRS_PER_KV):
                h0 = 2 * (hk * PAIRS_PER_KV + pp)
                p4.append(jnp.concatenate([pr[blk, h0 * tq:(h0 + 1) * tq, :],
                                           pr[blk, (h0 + 1) * tq:(h0 + 2) * tq, :]], axis=-1))
            o4 = jnp.dot(jnp.concatenate(p4, axis=0).astype(BF16), vbds[blk * N_KV + hk],
                         preferred_element_type=F32)
            outs.extend(o4[pp * tq:(pp + 1) * tq, :] for pp in range(PAIRS_PER_KV))
        yd_ref[blk * tq:(blk + 1) * tq, :] = jnp.concatenate(outs, axis=-1).astype(BF16)


def _swa_stream_kernel(h_ref, wq_ref, wkv_ref, bias_ref, sink_ref, yd_ref, k_ref, v_ref,
                       wbf, kprev, vprev_t, *, blocks_per_seq):
    i = pl.program_id(0)
    rows = h_ref.shape[0]
    nq = N_HEADS * HEAD_DIM
    nkv = N_KV * HEAD_DIM
    tq = WINDOW
    n_blocks = rows // tq

    @pl.when(i == 0)
    def _():
        kprev[...] = jnp.zeros_like(kprev)
        vprev_t[...] = jnp.zeros_like(vprev_t)

    qkv = _swa_project(i, h_ref, wq_ref, wkv_ref, wbf)
    k_new = qkv[:, nq:nq + nkv]
    v_new = qkv[:, nq + nkv:]
    k_ref[...] = k_new
    v_ref[...] = v_new
    v_new_t = v_new.T
    key = lax.broadcasted_iota(I32, (2, 2 * WINDOW, PAIRS_PER_KV * tq), 1)

    for blk in range(n_blocks):
        rs = slice(blk * tq, (blk + 1) * tq)
        first = (i * n_blocks + blk) % blocks_per_seq == 0
        k_cur = k_new[rs]
        v_cur_t = v_new_t[:, rs]
        kcat = jnp.concatenate([kprev[...], k_cur], axis=0)
        vcat_t = jnp.concatenate([vprev_t[...], v_cur_t], axis=1)
        kprev[...] = k_cur
        vprev_t[...] = v_cur_t
        kswap = pltpu.roll(kcat, HEAD_DIM, 1)
        vswap_t = pltpu.roll(vcat_t, HEAD_DIM, 0)
        outs = []
        for hk in range(N_KV):
            kbd = _pair_block_diag(kcat, kswap, hk, 0)
            vbd_t = _pair_block_diag(vcat_t, vswap_t, hk, 1)
            st = lax.dot_general(kbd, _stack_pairs(qkv, rs, hk), NT_DIMS,
                                 preferred_element_type=F32) * (HEAD_DIM ** -0.5)
            s3 = st.reshape(2, 2 * WINDOW, PAIRS_PER_KV * tq) + bias_ref[hk]
            s3 = jnp.where(first & (key < WINDOW), NEG_INF, s3)
            sink = sink_ref[hk]
            m = jnp.maximum(jnp.max(s3, axis=1, keepdims=True), sink)
            pr = jnp.exp(s3 - m)
            pr = pr / (jnp.sum(pr, axis=1, keepdims=True) + jnp.exp(sink - m))
            o_t = jnp.dot(vbd_t, pr.reshape(4 * WINDOW, PAIRS_PER_KV * tq).astype(BF16),
                          preferred_element_type=F32)
            o4 = o_t.T
            outs.extend(o4[pp * tq:(pp + 1) * tq, :] for pp in range(PAIRS_PER_KV))
        yd_ref[rs, :] = jnp.concatenate(outs, axis=-1).astype(BF16)


def _swa_weight_specs(w_in, d):
    nq = N_HEADS * HEAD_DIM
    nkv = N_KV * HEAD_DIM
    nw = nq + 2 * nkv
    q_blk = (w_in.shape[1] - nw) // nq
    kv_blk = (w_in.shape[1] - 2 * nkv) // (2 * nkv)
    assert q_blk * nq + nw == w_in.shape[1] and kv_blk * 2 * nkv + 2 * nkv == w_in.shape[1]
    return [pl.BlockSpec((d, nq), lambda i: (0, q_blk)), pl.BlockSpec((d, 2 * nkv), lambda i: (0, kv_blk))]


def _swa_outputs(rows):
    nq = N_HEADS * HEAD_DIM
    nkv = N_KV * HEAD_DIM
    specs = [pl.BlockSpec((ROW_TILE, nq), lambda i: (i, 0)),
             pl.BlockSpec((ROW_TILE, nkv), lambda i: (i, 0)),
             pl.BlockSpec((ROW_TILE, nkv), lambda i: (i, 0))]
    shapes = [jax.ShapeDtypeStruct((rows, nq), BF16),
              jax.ShapeDtypeStruct((rows, nkv), F32),
              jax.ShapeDtypeStruct((rows, nkv), F32)]
    return specs, shapes


def _swa_cached_mixer(h2, w_in, k_cache, v_cache, bias, sinks, tq):
    rows, d = h2.shape
    nkv = N_KV * HEAD_DIM
    nw = N_HEADS * HEAD_DIM + 2 * nkv
    n_blocks = ROW_TILE // tq
    cache_spec = pl.BlockSpec((n_blocks, WINDOW, nkv), lambda i: (i, 0, 0))
    out_specs, out_shape = _swa_outputs(rows)
    return pl.pallas_call(
        functools.partial(_swa_cached_kernel, tq=tq),
        grid=(rows // ROW_TILE,),
        in_specs=[pl.BlockSpec((ROW_TILE, d), lambda i: (i, 0))] + _swa_weight_specs(w_in, d)
        + [cache_spec, cache_spec,
           pl.BlockSpec((N_HEADS * tq, 2 * WINDOW), lambda i: (0, 0)),
           pl.BlockSpec((N_HEADS * tq, 1), lambda i: (0, 0))],
        out_specs=out_specs,
        out_shape=out_shape,
        scratch_shapes=[pltpu.VMEM((d, nw), BF16)],
        compiler_params=_cparams(1),
        name="swa_cached",
    )(h2, w_in, w_in, k_cache, v_cache, bias, sinks)


def _swa_stream_mixer(h2, w_in, bias_t, sinks_t, blocks_per_seq):
    rows, d = h2.shape
    nkv = N_KV * HEAD_DIM
    nw = N_HEADS * HEAD_DIM + 2 * nkv
    lanes = PAIRS_PER_KV * WINDOW
    out_specs, out_shape = _swa_outputs(rows)
    return pl.pallas_call(
        functools.partial(_swa_stream_kernel, blocks_per_seq=blocks_per_seq),
        grid=(rows // ROW_TILE,),
        in_specs=[pl.BlockSpec((ROW_TILE, d), lambda i: (i, 0))] + _swa_weight_specs(w_in, d)
        + [pl.BlockSpec((N_KV, 2, 2 * WINDOW, lanes), lambda i: (0, 0, 0, 0)),
           pl.BlockSpec((N_KV, 2, 1, lanes), lambda i: (0, 0, 0, 0))],
        out_specs=out_specs,
        out_shape=out_shape,
        scratch_shapes=[pltpu.VMEM((d, nw), BF16), pltpu.VMEM((WINDOW, nkv), F32),
                        pltpu.VMEM((nkv, WINDOW), F32)],
        compiler_params=_cparams(1),
        name="swa_stream",
    )(h2, w_in, w_in, bias_t, sinks_t)


def _t5_bucket(dist):
    max_exact = N_BUCKETS // 2
    dd = np.maximum(dist, 1)
    large = max_exact + (np.log(dd / max_exact) / np.log(WINDOW / max_exact)
                         * (N_BUCKETS - max_exact)).astype(np.int64)
    large = np.minimum(large, N_BUCKETS - 1)
    return np.where(dist < max_exact, dist, large).astype(np.int32)


def _attention_bias(rel_bias):
    by_dist = jnp.take(rel_bias.astype(F32), _t5_bucket(np.arange(WINDOW)), axis=0).T
    neg = jnp.full((N_HEADS, WINDOW), NEG_INF, F32)
    line = jnp.concatenate([neg, by_dist[:, ::-1], neg[:, :WINDOW - 1]], axis=1)
    return jnp.stack([line[:, WINDOW - 1 - q:3 * WINDOW - 1 - q] for q in range(WINDOW)], axis=1)


def kernel(x_prompt, x_sample, state_pool, state_conv, cache_swa_k, cache_swa_v, c_prompt, c_sample, w_ada, b_ada, norm_g, final_norm_g, w_in_even, w_out_even, w_pool, pool_scale, conv_w, w_in_odd, w_out_odd, gm_norm_g, gm_norm_b, gm_w_s, gm_b_s, attn_sinks, rel_bias, w_router, b_router, w_gate, w_up, w_down):
    d = D_MODEL
    bp, tp, _ = x_prompt.shape
    bs, ts, _ = x_sample.shape
    rows_s = bs * ts
    assert rows_s == ROW_TILE and tp % ROUTER_TILE == 0 and PAST_LEN % CHUNK == 0
    assert bp <= V7X_SUBLANES and CHUNK % ts == 0

    n_c = bp + bs
    c_pad = (-n_c) % V7X_SUBLANES
    c_all = jnp.concatenate([c_prompt, c_sample, jnp.zeros((c_pad, d), F32)], axis=0)
    mod_p = _adaln(c_all, w_ada, b_ada)
    mod_s = jnp.repeat(mod_p[:, bp:bp + bs], ts, axis=1)

    tps_p = tp // ROW_TILE
    xp = x_prompt.reshape(bp * tp, d)
    xs_ = x_sample.reshape(rows_s, d)
    w_in0, w_in1 = w_in_even[0], w_in_odd[0]

    zero_pool = jnp.zeros((bp, POOL_STATE, POOL_WIDTH), F32)
    zero_conv = jnp.zeros((bp, CONV_K - 1, CONV_WIDTH), F32)
    hp0, ya_p, pool_p = _pool_mixer(xp, norm_g[0, 0], mod_p, 0, w_in0, w_pool[0], pool_scale[0],
                                    zero_pool, 1, ROW_TILE, 0)
    hs0, ya_s, pool_s = _pool_mixer(xs_, norm_g[0, 0], mod_s, 0, w_in0, w_pool[0], pool_scale[0],
                                    state_pool[0], bs, ts, PAST_LEN)
    yb_p, conv_p = _conv_mixer(hp0, w_in0, conv_w[0], zero_conv, 1, ROW_TILE)
    yb_s, conv_s = _conv_mixer(hs0, w_in0, conv_w[0], state_conv[0], bs, ts)
    out_p = _outproj(ya_p, yb_p, xp, mod_p, 0, norm_g[0, 1], w_out_even[0], tps_p)
    out_s = _outproj(ya_s, yb_s, xs_, mod_s, 0, norm_g[0, 1], w_out_even[0], 1)
    (x2p, h1p), (x2s, h1s) = _moe(out_p, out_s, mod_p, mod_s, 0, norm_g[1, 0], tps_p,
                                  w_router, b_router, w_gate, w_up, w_down, final=False)

    bs_t = gm_b_s[0].T
    (yc_p,) = _gmlp_mixer(h1p, w_in1, gm_norm_g[0], gm_norm_b[0], gm_w_s[0], bs_t, CHUNK, CHUNK, False)
    yc_s, gv_s = _gmlp_mixer(h1s, w_in1, gm_norm_g[0], gm_norm_b[0], gm_w_s[0], bs_t, rows_s, ts, True)
    bias = _attention_bias(rel_bias)
    nkv = N_KV * HEAD_DIM
    bias_t = jnp.transpose(bias.reshape(N_KV, PAIRS_PER_KV, 2, WINDOW, 2 * WINDOW), (0, 2, 4, 1, 3))
    bias_t = bias_t.reshape(N_KV, 2, 2 * WINDOW, PAIRS_PER_KV * WINDOW)
    sinks_t = jnp.transpose(attn_sinks[0].reshape(N_KV, PAIRS_PER_KV, 2), (0, 2, 1))
    sinks_t = jnp.repeat(sinks_t, WINDOW, axis=-1).reshape(N_KV, 2, 1, PAIRS_PER_KV * WINDOW)
    yd_p, k_p, v_p = _swa_stream_mixer(h1p, w_in1, bias_t, sinks_t, tp // WINDOW)
    yd_s, k_s, v_s = _swa_cached_mixer(h1s, w_in1, cache_swa_k[0].reshape(bs, WINDOW, nkv),
                                       cache_swa_v[0].reshape(bs, WINDOW, nkv),
                                       bias[:, :ts, :].reshape(N_HEADS * ts, 2 * WINDOW),
                                       jnp.repeat(attn_sinks[0], ts).reshape(-1, 1), ts)
    out_p = _outproj(yc_p, yd_p, x2p, mod_p, 1, norm_g[1, 1], w_out_odd[0], tps_p)
    out_s = _outproj(yc_s, yd_s, x2s, mod_s, 1, norm_g[1, 1], w_out_odd[0], 1)
    (yp,), (ys_out,) = _moe(out_p, out_s, mod_p, mod_s, 1, final_norm_g, tps_p,
                            w_router, b_router, w_gate, w_up, w_down, final=True)

    k_p4 = k_p.reshape(bp, tp, N_KV, HEAD_DIM)[:, -WINDOW:]
    v_p4 = v_p.reshape(bp, tp, N_KV, HEAD_DIM)[:, -WINDOW:]
    k_s4 = jnp.concatenate([cache_swa_k[0], k_s.reshape(bs, ts, N_KV, HEAD_DIM)], axis=1)[:, -WINDOW:]
    v_s4 = jnp.concatenate([cache_swa_v[0], v_s.reshape(bs, ts, N_KV, HEAD_DIM)], axis=1)[:, -WINDOW:]
    return (yp.reshape(bp, tp, d), ys_out.reshape(bs, ts, d),
            pool_p[None], pool_s[None], conv_p[None], conv_s[None],
            k_p4[None], k_s4[None], v_p4[None], v_s4[None],
            gv_s.reshape(bs, ts, GM_WIDTH)[None])
```

```python
import functools

import numpy as np
import jax
import jax.numpy as jnp
from jax import lax
from jax.experimental import pallas as pl
from jax.experimental.pallas import tpu as pltpu

F32 = jnp.float32
BF16 = jnp.bfloat16
I32 = jnp.int32
U32 = jnp.uint32

D_MODEL = 2048
POOL_WINDOWS = (2, 4, 8, 16)
POOL_WIDTH = 1024
POOL_GROUP = 256
POOL_STATE = 15
CONV_WIDTH = 1024
CONV_K = 3
GM_WIDTH = 1024
GM_GROUPS = 8
GM_GROUP = 128
CHUNK = 128
HEAD_DIM = 64
N_HEADS = 16
N_KV = 2
WINDOW = 128
N_BUCKETS = 32
N_EXPERTS = 16
N_EXPERT_GROUPS = 4
EXP_PER_GROUP = 4
TOP_K = 2
EPS = 1e-6
NEG_INF = -1e30
PAST_LEN = 16384

V7X_SUBLANES = 8
V7X_LANES = 128
VMEM_LIMIT = 56 * 1024 * 1024

ROW_TILE = 256
MATMUL_TILE = 512
ROUTER_TILE = 1024
POOL_HALO = 16
CONV_HALO = 8
MOE_TILE = 256
TAB_EXPERT, TAB_VALID, TAB_LAST_TILE, TAB_NUSED, TAB_NEXT = 0, 1, 2, 3, 4


def _cparams(n_axes):
    return pltpu.CompilerParams(dimension_semantics=("arbitrary",) * n_axes,
                                vmem_limit_bytes=VMEM_LIMIT)


def _rms(x, g):
    return x * lax.rsqrt(jnp.mean(x * x, axis=-1, keepdims=True) + EPS) * g


def _mod_spec(mod, layer, part):
    nrow = ROW_TILE if mod.shape[1] == ROW_TILE else V7X_SUBLANES
    return pl.BlockSpec((1, nrow, D_MODEL), lambda *_: (layer, 0, part))


def _mod_rows(m_ref, seq):
    if m_ref.shape[1] == V7X_SUBLANES:
        return m_ref[0, pl.ds(seq, 1), :]
    return m_ref[0]


def _adaln_kernel(c_ref, w_ref, b_ref, o_ref):
    c = c_ref[...]
    a = (c * jax.nn.sigmoid(c)).astype(BF16)
    o_ref[0] = jnp.dot(a, w_ref[0].astype(BF16), preferred_element_type=F32) + b_ref[0]


def _adaln(c_all, w_ada, b_ada):
    depth, d, n6 = w_ada.shape
    m = c_all.shape[0]
    tn = 1024
    return pl.pallas_call(
        _adaln_kernel,
        grid=(depth, n6 // tn),
        in_specs=[pl.BlockSpec((m, d), lambda l, j: (0, 0)),
                  pl.BlockSpec((1, d, tn), lambda l, j: (l, 0, j)),
                  pl.BlockSpec((1, 1, tn), lambda l, j: (l, 0, j))],
        out_specs=pl.BlockSpec((1, m, tn), lambda l, j: (l, 0, j)),
        out_shape=jax.ShapeDtypeStruct((depth, m, n6), F32),
        compiler_params=_cparams(2),
        name="adaln",
    )(c_all, w_ada, b_ada.reshape(depth, 1, n6))


def _pool_kernel(x_ref, g_ref, sc_ref, sh_ref, w_ref, wp_ref, ps_ref, st_ref, h_ref, ya_ref, ns_ref,
                 wbf, wpbf, carry, *, nb, tm, tiles_per_seq, start):
    i = pl.program_id(0)
    t = i % tiles_per_seq
    seq = i // tiles_per_seq
    c = POOL_WIDTH
    halo = POOL_HALO

    @pl.when(i == 0)
    def _():
        wbf[...] = w_ref[...].astype(BF16)
        wpbf[...] = wp_ref[...].astype(BF16)

    @pl.when(t == 0)
    def _():
        carry[...] = st_ref[...]

    h = (_rms(x_ref[...], g_ref[...]) * (1.0 + _mod_rows(sc_ref, seq)) + _mod_rows(sh_ref, seq)).astype(BF16)
    h_ref[...] = h
    p = jnp.dot(h, wbf[...], preferred_element_type=F32)
    p3 = p.reshape(nb, tm, c)
    ext3 = jnp.concatenate([carry[...], p3], axis=1)
    tail = ext3[:, tm:tm + halo, :]
    ns_ref[...] = tail
    carry[...] = tail
    ext = ext3.reshape(nb * (halo + tm), c)
    pos = start + t * tm + lax.broadcasted_iota(I32, (1, tm, 1), 1)
    outs = []
    for gi, w in enumerate(POOL_WINDOWS):
        sl = slice(gi * POOL_GROUP, (gi + 1) * POOL_GROUP)
        acc = ext[:, sl]
        shift = 1
        while shift < w:
            acc = acc + pltpu.roll(acc, shift, 0)
            shift *= 2
        win = acc.reshape(nb, halo + tm, POOL_GROUP)[:, halo:, :]
        cnt = jnp.minimum(pos + 1, w).astype(F32)
        dgrp = win / cnt - p3[:, :, sl]
        outs.append(jnp.dot(dgrp.reshape(nb * tm, POOL_GROUP).astype(BF16), wpbf[gi],
                            preferred_element_type=F32))
    y = jnp.concatenate(outs, axis=-1) * ps_ref[...]
    ya_ref[...] = y.astype(BF16)


def _pool_mixer(x2, g, mod, layer, w_in, w_pool, pool_scale, state, nb, tm, start):
    rows, d = x2.shape
    nseq = state.shape[0]
    tiles_per_seq = (rows // nseq) // tm
    seq_blocks = nseq // nb
    c = POOL_WIDTH
    st = jnp.pad(state, ((0, 0), (POOL_HALO - POOL_STATE, 0), (0, 0)))
    kern = functools.partial(_pool_kernel, nb=nb, tm=tm, tiles_per_seq=tiles_per_seq, start=start)
    h2, ya, ns = pl.pallas_call(
        kern,
        grid=(seq_blocks * tiles_per_seq,),
        in_specs=[pl.BlockSpec((nb * tm, d), lambda i: (i, 0)),
                  pl.BlockSpec((1, d), lambda i: (0, 0)),
                  _mod_spec(mod, layer, 1), _mod_spec(mod, layer, 0),
                  pl.BlockSpec((d, c), lambda i: (0, 0)),
                  pl.BlockSpec((len(POOL_WINDOWS), POOL_GROUP, POOL_GROUP), lambda i: (0, 0, 0)),
                  pl.BlockSpec((1, c), lambda i: (0, 0)),
                  pl.BlockSpec((nb, POOL_HALO, c), lambda i: (i // tiles_per_seq, 0, 0))],
        out_specs=[pl.BlockSpec((nb * tm, d), lambda i: (i, 0)),
                   pl.BlockSpec((nb * tm, c), lambda i: (i, 0)),
                   pl.BlockSpec((nb, POOL_HALO, c), lambda i: (i // tiles_per_seq, 0, 0))],
        out_shape=[jax.ShapeDtypeStruct((rows, d), BF16),
                   jax.ShapeDtypeStruct((rows, c), BF16),
                   jax.ShapeDtypeStruct((nseq, POOL_HALO, c), F32)],
        scratch_shapes=[pltpu.VMEM((d, c), BF16),
                        pltpu.VMEM((len(POOL_WINDOWS), POOL_GROUP, POOL_GROUP), BF16),
                        pltpu.VMEM((nb, POOL_HALO, c), F32)],
        compiler_params=_cparams(1),
        name="pool_mixer",
    )(x2, g.reshape(1, d), mod, mod, w_in, w_pool, pool_scale.reshape(1, c), st)
    return h2, ya, ns[:, POOL_HALO - POOL_STATE:, :]


def _conv_kernel(h_ref, wx_ref, wb_ref, wc_ref, cw_ref, st_ref, yb_ref, ns_ref,
                 wxbf, wbbf, wcbf, carry, *, nb, tm, tiles_per_seq):
    i = pl.program_id(1)
    t = i % tiles_per_seq
    tc = wxbf.shape[1]
    halo = CONV_HALO

    @pl.when(i == 0)
    def _():
        wxbf[...] = wx_ref[...].astype(BF16)
        wbbf[...] = wb_ref[...].astype(BF16)
        wcbf[...] = wc_ref[...].astype(BF16)

    @pl.when(t == 0)
    def _():
        carry[...] = st_ref[...]

    h = h_ref[...]
    xin = jnp.dot(h, wxbf[...], preferred_element_type=F32)
    gb = jnp.dot(h, wbbf[...], preferred_element_type=F32)
    gc = jnp.dot(h, wcbf[...], preferred_element_type=F32)
    z3 = (gc * xin).reshape(nb, tm, tc)
    ext3 = jnp.concatenate([carry[...], z3], axis=1)
    tail = ext3[:, tm:tm + halo, :]
    ns_ref[...] = tail
    carry[...] = tail
    ext = ext3.reshape(nb * (halo + tm), tc)
    cw = cw_ref[...]
    conv = cw[0:1, :] * pltpu.roll(ext, 2, 0) + cw[1:2, :] * pltpu.roll(ext, 1, 0) + cw[2:3, :] * ext
    conv = conv.reshape(nb, halo + tm, tc)[:, halo:, :].reshape(nb * tm, tc)
    yb_ref[...] = (gb * conv).astype(BF16)


def _conv_mixer(h2, w_in, conv_w, state, nb, tm):
    rows, d = h2.shape
    nseq = state.shape[0]
    tiles_per_seq = (rows // nseq) // tm
    seq_blocks = nseq // nb
    c = CONV_WIDTH
    tc = 512
    cb = c // tc
    base = POOL_WIDTH // tc
    st = jnp.pad(state, ((0, 0), (CONV_HALO - (CONV_K - 1), 0), (0, 0)))
    kern = functools.partial(_conv_kernel, nb=nb, tm=tm, tiles_per_seq=tiles_per_seq)
    yb, ns = pl.pallas_call(
        kern,
        grid=(cb, seq_blocks * tiles_per_seq),
        in_specs=[pl.BlockSpec((nb * tm, d), lambda j, i: (i, 0)),
                  pl.BlockSpec((d, tc), lambda j, i: (0, base + j)),
                  pl.BlockSpec((d, tc), lambda j, i: (0, base + cb + j)),
                  pl.BlockSpec((d, tc), lambda j, i: (0, base + 2 * cb + j)),
                  pl.BlockSpec((CONV_K, tc), lambda j, i: (0, j)),
                  pl.BlockSpec((nb, CONV_HALO, tc), lambda j, i: (i // tiles_per_seq, 0, j))],
        out_specs=[pl.BlockSpec((nb * tm, tc), lambda j, i: (i, j)),
                   pl.BlockSpec((nb, CONV_HALO, tc), lambda j, i: (i // tiles_per_seq, 0, j))],
        out_shape=[jax.ShapeDtypeStruct((rows, c), BF16),
                   jax.ShapeDtypeStruct((nseq, CONV_HALO, c), F32)],
        scratch_shapes=[pltpu.VMEM((d, tc), BF16)] * 3 + [pltpu.VMEM((nb, CONV_HALO, tc), F32)],
        compiler_params=_cparams(2),
        name="conv_mixer",
    )(h2, w_in, w_in, w_in, conv_w, st)
    return yb, ns[:, CONV_HALO - (CONV_K - 1):, :]


def _pack_bf16_pairs(v):
    c = v.shape[1] // 2
    r = v.astype(BF16).astype(F32)
    lo = pltpu.bitcast(r[:, :c], U32)
    hi = pltpu.bitcast(r[:, c:], U32)
    return (hi & jnp.uint32(0xFFFF0000)) | (lo >> 16)


def _store_token_tiles(ref, v):
    rows = v.shape[0]
    for j in range(V7X_SUBLANES):
        ref[pl.ds(j, rows, stride=V7X_SUBLANES), :] = v[:, j * V7X_LANES:(j + 1) * V7X_LANES]


def _load_token_tiles(ref):
    rows = ref.shape[0] // V7X_SUBLANES
    return jnp.concatenate([ref[pl.ds(j, rows, stride=V7X_SUBLANES), :] for j in range(V7X_SUBLANES)],
                           axis=-1)


def _unpack_pairs_f32(w):
    return pltpu.bitcast(w << 16, F32), pltpu.bitcast(w & jnp.uint32(0xFFFF0000), F32)


def _unpack_bf16_pairs(w):
    lo, hi = _unpack_pairs_f32(w)
    return lo.astype(BF16), hi.astype(BF16)


def _outproj_kernel(ya_ref, yb_ref, x_ref, g1_ref, sc_ref, sh_ref, ng_ref, wo_ref,
                    x1_ref, hp_ref, hpt_ref, wobf, *, tiles_per_seq):
    i = pl.program_id(0)
    seq = i // tiles_per_seq
    half = ya_ref.shape[1]

    @pl.when(i == 0)
    def _():
        wobf[...] = wo_ref[...].astype(BF16)

    y = (jnp.dot(ya_ref[...], wobf[:half, :], preferred_element_type=F32)
         + jnp.dot(yb_ref[...], wobf[half:, :], preferred_element_type=F32))
    x1 = x_ref[...] + _mod_rows(g1_ref, seq) * y
    x1_ref[...] = x1
    h2 = _rms(x1, ng_ref[...]) * (1.0 + _mod_rows(sc_ref, seq)) + _mod_rows(sh_ref, seq)
    packed = _pack_bf16_pairs(h2)
    hp_ref[...] = packed
    _store_token_tiles(hpt_ref, packed)


def _outproj(ya, yb, x2, mod, layer, ng, w_out, seq_rows):
    rows_all, d = x2.shape
    half = ya.shape[1]
    rt = min(rows_all, MATMUL_TILE)
    row_spec = lambda w: pl.BlockSpec((rt, w), lambda i: (i, 0))
    return pl.pallas_call(
        functools.partial(_outproj_kernel, tiles_per_seq=seq_rows // rt),
        grid=(rows_all // rt,),
        in_specs=[row_spec(half), row_spec(half), row_spec(d),
                  _mod_spec(mod, layer, 2), _mod_spec(mod, layer, 4), _mod_spec(mod, layer, 3),
                  pl.BlockSpec((1, d), lambda i: (0, 0)),
                  pl.BlockSpec((d, d), lambda i: (0, 0), pipeline_mode=pl.Buffered(1))],
        out_specs=[row_spec(d), row_spec(d // 2),
                   pl.BlockSpec((rt * V7X_SUBLANES, V7X_LANES), lambda i: (i, 0))],
        out_shape=[jax.ShapeDtypeStruct((rows_all, d), F32),
                   jax.ShapeDtypeStruct((rows_all, d // 2), U32),
                   jax.ShapeDtypeStruct((rows_all * V7X_SUBLANES, V7X_LANES), U32)],
        scratch_shapes=[pltpu.VMEM((d, d), BF16)],
        compiler_params=_cparams(1),
        name="outproj",
    )(ya, yb, x2, mod, mod, mod, ng.reshape(1, d), w_out)


def _router_kernel(hpp_ref, hps_ref, wr_ref, br_ref, pos_ref, rw_ref, tab_ref,
                   cnt_acc, totals, starts, padded, *, nt_p, rows_s):
    ph = pl.program_id(0)
    t = pl.program_id(1)
    last = nt_p
    r = hpp_ref.shape[0]
    half = hpp_ref.shape[1]
    ne = N_EXPERTS
    sub = lax.broadcasted_iota(I32, (ne, V7X_LANES), 0)

    @pl.when(t == 0)
    def _():
        cnt_acc[...] = jnp.zeros_like(cnt_acc)

    @pl.when((ph == 0) & (t == 0))
    def _():
        starts[...] = jnp.zeros_like(starts)
        padded[...] = jnp.zeros_like(padded)

    @pl.when((ph == 1) & (t == 0))
    def _():
        pad = jnp.floor((totals[...] + (MOE_TILE - 1.0)) * (1.0 / MOE_TILE)) * MOE_TILE
        run = pad
        k = 1
        while k < ne:
            run = run + jnp.where(sub >= k, pltpu.roll(run, k, 0), 0.0)
            k *= 2
        padded[...] = pad
        starts[...] = run - pad

    is_s = t == last
    w_s = jnp.concatenate([hps_ref[...], jnp.zeros((r - rows_s, half), U32)], axis=0)
    w = jnp.where(is_s, w_s, hpp_ref[...])
    lo, hi = _unpack_bf16_pairs(w)
    wr = wr_ref[...].astype(BF16)
    nt_dims = (((1,), (1,)), ((), ()))
    log_t = (lax.dot_general(wr[:, :half], lo, nt_dims, preferred_element_type=F32)
             + lax.dot_general(wr[:, half:], hi, nt_dims, preferred_element_type=F32))

    s = jax.nn.sigmoid(log_t)
    sg = s + br_ref[...]
    eid = lax.broadcasted_iota(I32, (ne, r), 0)
    within = eid % EXP_PER_GROUP
    grp = eid // EXP_PER_GROUP

    def group_rot(x, k):
        return jnp.where(within + k < EXP_PER_GROUP,
                         pltpu.roll(x, ne - k, 0), pltpu.roll(x, EXP_PER_GROUP - k, 0))

    rank = jnp.zeros((ne, r), I32)
    for k in range(1, EXP_PER_GROUP):
        mate = group_rot(sg, k)
        wrapped = within + k >= EXP_PER_GROUP
        ahead = (mate > sg) | (wrapped & (mate == sg))
        rank = rank + ahead.astype(I32)
    top2 = rank < TOP_K
    kept = jnp.where(top2, sg, 0.0)
    gscore = kept
    for k in range(1, EXP_PER_GROUP):
        gscore = gscore + group_rot(kept, k)
    win = None
    for k in range(1, N_EXPERT_GROUPS):
        other = pltpu.roll(gscore, EXP_PER_GROUP * k, 0)
        beats = (gscore > other) | ((grp < k) & (gscore == other))
        win = beats if win is None else (win & beats)
    n_valid = jnp.where(is_s, rows_s, r)
    tok = lax.broadcasted_iota(I32, (ne, r), 1)
    sel = top2 & win & (tok < n_valid)
    picked = jnp.where(sel, s, 0.0)
    wsum = jnp.sum(picked, axis=0, keepdims=True)
    gate = picked / jnp.where(tok[0:1, :] < n_valid, wsum, 1.0)

    src = lax.broadcasted_iota(I32, (r, r), 0)
    dst = lax.broadcasted_iota(I32, (r, r), 1)
    before = (src < dst).astype(BF16)
    selb = sel.astype(F32)
    ranks = jnp.dot(selb.astype(BF16), before, preferred_element_type=F32)
    slot = (starts[...][:, 0:1] + cnt_acc[...][:, 0:1] + ranks).astype(I32)
    cnt_new = cnt_acc[...] + jnp.sum(selb, axis=1, keepdims=True)
    cnt_acc[...] = cnt_new

    @pl.when((ph == 0) & (t == last))
    def _():
        totals[...] = cnt_new

    e_a = jnp.min(jnp.where(sel, eid, ne), axis=0, keepdims=True)
    e_b = jnp.max(jnp.where(sel, eid, -1), axis=0, keepdims=True)
    is_a = sel & (eid == e_a)
    is_b = sel & (eid == e_b)
    pos_a = jnp.sum(jnp.where(is_a, slot, 0), axis=0, keepdims=True)
    pos_b = jnp.sum(jnp.where(is_b, slot, 0), axis=0, keepdims=True)
    w_a = jnp.sum(jnp.where(is_a, gate, 0.0), axis=0, keepdims=True)
    w_b = jnp.sum(jnp.where(is_b, gate, 0.0), axis=0, keepdims=True)
    pos_ref[0] = jnp.concatenate([pos_a, pos_b], axis=0)
    wmat = jnp.concatenate([w_a, w_b, jnp.zeros((V7X_LANES - 2, r), F32)], axis=0)
    rw_ref[...] = wmat.T

    @pl.when((ph == 1) & (t == last))
    def _():
        ends = starts[...] + padded[...]
        lane = lax.broadcasted_iota(I32, (ne, V7X_LANES), 1)
        tile_start = (lane * MOE_TILE).astype(F32)
        te = jnp.sum((tile_start >= ends).astype(I32), axis=0, keepdims=True)
        valid = te < ne
        last_e = jnp.max(jnp.where(padded[...] > 0.0, sub, 0), axis=0, keepdims=True)
        te = jnp.where(valid, te, last_e)
        n_used = jnp.sum(valid.astype(I32), axis=1, keepdims=True) + jnp.zeros((1, V7X_LANES), I32)
        last_tile = jnp.where(padded[...] > 0.0, ends - MOE_TILE, -1.0).astype(I32)
        last_tile_row = jnp.sum(jnp.where(sub == lane, last_tile, 0), axis=0, keepdims=True)
        later = jnp.min(jnp.where((sub > te) & (padded[...] > 0.0), sub, ne), axis=0, keepdims=True)
        next_e = jnp.where(later < ne, later, -1)
        zero = jnp.zeros((1, V7X_LANES), I32)
        tab_ref[...] = jnp.concatenate([te, valid.astype(I32), last_tile_row, n_used, next_e,
                                        zero, zero, zero], axis=0)


def _router(hp_p, hp_s, w_router, b_router):
    n_p, half = hp_p.shape
    rows_s = hp_s.shape[0]
    r = ROUTER_TILE
    nt_p = n_p // r
    nt = nt_p + 1
    kern = functools.partial(_router_kernel, nt_p=nt_p, rows_s=rows_s)
    pos, rw, tab = pl.pallas_call(
        kern,
        grid=(2, nt),
        in_specs=[pl.BlockSpec((r, half), lambda p, t: (jnp.minimum(t, nt_p - 1), 0)),
                  pl.BlockSpec((rows_s, half), lambda p, t: (0, 0)),
                  pl.BlockSpec((N_EXPERTS, 2 * half), lambda p, t: (0, 0)),
                  pl.BlockSpec((N_EXPERTS, 1), lambda p, t: (0, 0))],
        out_specs=[pl.BlockSpec((1, TOP_K, r), lambda p, t: (p * t, 0, 0)),
                   pl.BlockSpec((r, V7X_LANES), lambda p, t: (p * t, 0)),
                   pl.BlockSpec((V7X_SUBLANES, V7X_LANES), lambda p, t: (0, 0))],
        out_shape=[jax.ShapeDtypeStruct((nt, TOP_K, r), I32),
                   jax.ShapeDtypeStruct((nt * r, V7X_LANES), F32),
                   jax.ShapeDtypeStruct((V7X_SUBLANES, V7X_LANES), I32)],
        scratch_shapes=[pltpu.VMEM((N_EXPERTS, V7X_LANES), F32)] * 4,
        compiler_params=_cparams(2),
        name="router",
    )(hp_p, hp_s, w_router.T, b_router.reshape(N_EXPERTS, 1))
    return pos.reshape(-1), rw, tab.reshape(-1)


def _pos_index(tok0):
    return (tok0 // ROUTER_TILE) * (TOP_K * ROUTER_TILE) + tok0 % ROUTER_TILE


def _tokens(ref, first, n=1):
    start = pl.multiple_of(first * V7X_SUBLANES, V7X_SUBLANES)
    return ref.at[pl.ds(start, n * V7X_SUBLANES), :]


def _dispatch_kernel(pos_ref, tab_ref, hpp_ref, hps_ref, xs_ref, zbuf, sem, *, n_p_steps):
    i = pl.program_id(0)
    rows = hpp_ref.shape[0] // V7X_SUBLANES

    @pl.when(i == 0)
    def _():
        zbuf[...] = jnp.zeros_like(zbuf)

        def fill(e):
            first = pl.multiple_of(tab_ref[TAB_LAST_TILE * V7X_LANES + e], MOE_TILE)
            return pltpu.make_async_copy(zbuf, _tokens(xs_ref, first, MOE_TILE), sem)

        for e in range(N_EXPERTS):
            @pl.when(tab_ref[TAB_LAST_TILE * V7X_LANES + e] >= 0)
            def _():
                fill(e).start()
        for e in range(N_EXPERTS):
            @pl.when(tab_ref[TAB_LAST_TILE * V7X_LANES + e] >= 0)
            def _():
                fill(e).wait()

        def tail(j):
            first = pl.multiple_of(j * MOE_TILE, MOE_TILE)
            return pltpu.make_async_copy(zbuf, _tokens(xs_ref, first, MOE_TILE), sem)

        def tail_start(j, carry):
            tail(j).start()
            return carry

        def tail_wait(j, carry):
            tail(j).wait()
            return carry

        n_used = tab_ref[TAB_NUSED * V7X_LANES]
        n_tiles = xs_ref.shape[0] // (MOE_TILE * V7X_SUBLANES)
        lax.fori_loop(n_used, n_tiles, tail_start, 0)
        lax.fori_loop(n_used, n_tiles, tail_wait, 0)

    base = _pos_index(i * rows)

    def scatter(src_ref):
        def row_copy(r, dst):
            return pltpu.make_async_copy(_tokens(src_ref, r), _tokens(xs_ref, dst), sem)

        def issue(r, carry):
            row_copy(r, pos_ref[base + r]).start()
            row_copy(r, pos_ref[base + ROUTER_TILE + r]).start(priority=1)
            return carry

        lax.fori_loop(0, rows, issue, 0, unroll=8)
        block = pltpu.make_async_copy(src_ref, _tokens(xs_ref, 0, rows), sem)
        for _ in range(TOP_K):
            block.wait()

    @pl.when(i < n_p_steps)
    def _():
        scatter(hpp_ref)

    @pl.when(i == n_p_steps)
    def _():
        scatter(hps_ref)


def _dispatch(pos, tab, hpt_p, hpt_s, n_rows_sorted):
    sub = V7X_SUBLANES
    n_p_steps = hpt_p.shape[0] // (ROW_TILE * sub)
    assert hpt_s.shape[0] == ROW_TILE * sub
    kern = functools.partial(_dispatch_kernel, n_p_steps=n_p_steps)
    blk = (ROW_TILE * sub, V7X_LANES)
    return pl.pallas_call(
        kern,
        grid_spec=pltpu.PrefetchScalarGridSpec(
            num_scalar_prefetch=2,
            grid=(n_p_steps + 1,),
            in_specs=[pl.BlockSpec(blk, lambda i, p, t: (jnp.minimum(i, n_p_steps - 1), 0)),
                      pl.BlockSpec(blk, lambda i, p, t: (0, 0))],
            out_specs=pl.BlockSpec(memory_space=pl.ANY),
            scratch_shapes=[pltpu.VMEM((MOE_TILE * sub, V7X_LANES), U32), pltpu.SemaphoreType.DMA(())]),
        out_shape=jax.ShapeDtypeStruct((n_rows_sorted * sub, V7X_LANES), U32),
        compiler_params=_cparams(1),
        name="moe_dispatch",
    )(pos, tab, hpt_p, hpt_s)


def _experts_kernel(tab_ref, xs_ref, wg_hbm, wu_hbm, wd_hbm, ys_ref,
                    wg32, wu32, wd32, wgbf, wubf, wdbf, slot_ref, sems, *, layer):
    i = pl.program_id(0)
    expert = tab_ref[TAB_EXPERT * V7X_LANES + i]
    prev = tab_ref[TAB_EXPERT * V7X_LANES + jnp.maximum(i - 1, 0)]
    upcoming = tab_ref[TAB_NEXT * V7X_LANES + i]
    changed = (i == 0) | (expert != prev)
    half = V7X_SUBLANES * V7X_LANES

    def weight_copies(e, slot):
        return (pltpu.make_async_copy(wg_hbm.at[layer, e], wg32.at[slot], sems.at[0, slot]),
                pltpu.make_async_copy(wu_hbm.at[layer, e], wu32.at[slot], sems.at[1, slot]),
                pltpu.make_async_copy(wd_hbm.at[layer, e], wd32.at[slot], sems.at[2, slot]))

    @pl.when(i == 0)
    def _():
        slot_ref[0] = 0
        for cp in weight_copies(expert, 0):
            cp.start()

    @pl.when(changed & (i > 0))
    def _():
        slot_ref[0] = 1 - slot_ref[0]

    for slot in range(2):
        @pl.when(changed & (slot_ref[0] == slot))
        def _():
            for cp in weight_copies(expert, slot):
                cp.wait()
            wgbf[...] = wg32[slot].astype(BF16)
            wubf[...] = wu32[slot].astype(BF16)
            wdbf[...] = wd32[slot].astype(BF16)

            @pl.when(upcoming >= 0)
            def _():
                for cp in weight_copies(upcoming, 1 - slot):
                    cp.start(priority=1)

    @pl.when(tab_ref[TAB_VALID * V7X_LANES + i] > 0)
    def _():
        lo, hi = _unpack_bf16_pairs(_load_token_tiles(xs_ref))
        a = (jnp.dot(lo, wgbf[:half, :], preferred_element_type=F32)
             + jnp.dot(hi, wgbf[half:, :], preferred_element_type=F32))
        b = (jnp.dot(lo, wubf[:half, :], preferred_element_type=F32)
             + jnp.dot(hi, wubf[half:, :], preferred_element_type=F32))
        hid = (a * jax.nn.sigmoid(a)) * b
        y = jnp.dot(hid.astype(BF16), wdbf[...], preferred_element_type=F32)
        _store_token_tiles(ys_ref, _pack_bf16_pairs(y))

    @pl.when(tab_ref[TAB_VALID * V7X_LANES + i] == 0)
    def _():
        ys_ref[...] = jnp.zeros_like(ys_ref)


def _experts(tab, xs, w_gate, w_up, w_down, layer):
    sub = V7X_SUBLANES
    _, _, d, f = w_gate.shape
    nt = xs.shape[0] // (MOE_TILE * sub)
    assert nt <= V7X_LANES and d == 2 * sub * V7X_LANES
    blk = (MOE_TILE * sub, V7X_LANES)

    def tile(i, tab_ref):
        return jnp.minimum(i, tab_ref[TAB_NUSED * V7X_LANES] - 1)

    hbm = pl.BlockSpec(memory_space=pl.ANY)
    return pl.pallas_call(
        functools.partial(_experts_kernel, layer=layer),
        grid_spec=pltpu.PrefetchScalarGridSpec(
            num_scalar_prefetch=1,
            grid=(nt,),
            in_specs=[pl.BlockSpec(blk, lambda i, t: (tile(i, t), 0)), hbm, hbm, hbm],
            out_specs=pl.BlockSpec(blk, lambda i, t: (i, 0)),
            scratch_shapes=[pltpu.VMEM((2, d, f), F32), pltpu.VMEM((2, d, f), F32), pltpu.VMEM((2, f, d), F32),
                            pltpu.VMEM((d, f), BF16), pltpu.VMEM((d, f), BF16), pltpu.VMEM((f, d), BF16),
                            pltpu.SMEM((1,), I32), pltpu.SemaphoreType.DMA((3, 2))]),
        out_shape=jax.ShapeDtypeStruct(xs.shape, U32),
        compiler_params=_cparams(1),
        name="moe_experts",
    )(tab, xs, w_gate, w_up, w_down)


def _combine_kernel(pos_ref, ys_ref, x1_ref, rw_ref, g2_ref, ng_ref, sc_ref, sh_ref, *rest,
                    tok0, tiles_per_seq, n_steps, final):
    if final:
        x2_ref, buf, sems = rest
        hn_ref = None
    else:
        x2_ref, hn_ref, buf, sems = rest
    i = pl.program_id(0)
    seq = i // tiles_per_seq
    rows = x1_ref.shape[0]

    def gather(step, slot):
        base = _pos_index(tok0 + step * rows)

        def issue(r, carry):
            for k in range(TOP_K):
                src = pos_ref[base + k * ROUTER_TILE + r]
                pltpu.make_async_copy(_tokens(ys_ref, src), _tokens(buf.at[slot, k], r),
                                      sems.at[slot]).start(priority=k)
            return carry

        lax.fori_loop(0, rows, issue, 0, unroll=8)

    @pl.when(i == 0)
    def _():
        gather(0, 0)

    @pl.when(i + 1 < n_steps)
    def _():
        gather(i + 1, (i + 1) % 2)

    slot = i % 2
    for k in range(TOP_K):
        pltpu.make_async_copy(_tokens(ys_ref, 0, rows), buf.at[slot, k], sems.at[slot]).wait()

    rw = rw_ref[...]
    lo_a, hi_a = _unpack_pairs_f32(_load_token_tiles(buf.at[slot, 0]))
    lo_b, hi_b = _unpack_pairs_f32(_load_token_tiles(buf.at[slot, 1]))
    w_a = rw[:, 0:1]
    w_b = rw[:, 1:2]
    moe = jnp.concatenate([w_a * lo_a + w_b * lo_b, w_a * hi_a + w_b * hi_b], axis=-1)
    x2 = x1_ref[...] + _mod_rows(g2_ref, seq) * moe
    if final:
        x2_ref[...] = _rms(x2, ng_ref[...])
    else:
        x2_ref[...] = x2
        hn_ref[...] = (_rms(x2, ng_ref[...]) * (1.0 + _mod_rows(sc_ref, seq))
                       + _mod_rows(sh_ref, seq)).astype(BF16)


def _combine(pos, ys, x1, rw, tok0, mod, layer, ng, tiles_per_seq, final):
    n_tok, d = x1.shape
    kern = functools.partial(_combine_kernel, tok0=tok0, tiles_per_seq=tiles_per_seq,
                             n_steps=n_tok // ROW_TILE, final=final)
    rw_off = tok0 // ROW_TILE
    row_spec = lambda w: pl.BlockSpec((ROW_TILE, w), lambda i, p: (i, 0))
    out_shape = [jax.ShapeDtypeStruct((n_tok, d), F32)]
    out_specs = [row_spec(d)]
    if not final:
        out_shape.append(jax.ShapeDtypeStruct((n_tok, d), BF16))
        out_specs.append(row_spec(d))
    nxt = min(layer + 1, mod.shape[0] - 1)
    return pl.pallas_call(
        kern,
        grid_spec=pltpu.PrefetchScalarGridSpec(
            num_scalar_prefetch=1,
            grid=(n_tok // ROW_TILE,),
            in_specs=[pl.BlockSpec(memory_space=pl.ANY), row_spec(d),
                      pl.BlockSpec((ROW_TILE, V7X_LANES), lambda i, p: (rw_off + i, 0)),
                      _mod_spec(mod, layer, 5), pl.BlockSpec((1, d), lambda i, p: (0, 0)),
                      _mod_spec(mod, nxt, 1), _mod_spec(mod, nxt, 0)],
            out_specs=out_specs,
            scratch_shapes=[pltpu.VMEM((2, TOP_K, ROW_TILE * V7X_SUBLANES, V7X_LANES), U32),
                            pltpu.SemaphoreType.DMA((2,))]),
        out_shape=out_shape,
        compiler_params=_cparams(1),
        name="moe_combine_final" if final else "moe_combine",
    )(pos, ys, x1, rw, mod, ng.reshape(1, d), mod, mod)


def _moe(out_p, out_s, mod_p, mod_s, layer, ng, tps_p, w_router, b_router, w_gate, w_up, w_down, final):
    x1_p, hp_p, hpt_p = out_p
    x1_s, hp_s, hpt_s = out_s
    n_p = hp_p.shape[0]
    n_tok = n_p + hp_s.shape[0]
    max_rows = TOP_K * n_tok + N_EXPERTS * (MOE_TILE - 1)
    n_rows_sorted = -(-max_rows // MOE_TILE) * MOE_TILE
    pos, rw, tab = _router(hp_p, hp_s, w_router, b_router)
    xs = _dispatch(pos, tab, hpt_p, hpt_s, n_rows_sorted)
    ys = _experts(tab, xs, w_gate, w_up, w_down, layer)
    out_p = _combine(pos, ys, x1_p, rw, 0, mod_p, layer, ng, tps_p, final)
    out_s = _combine(pos, ys, x1_s, rw, n_p, mod_s, layer, ng, 1, final)
    return out_p, out_s


def _gmlp_kernel(h_ref, w_ref, lg_ref, lb_ref, ws_ref, bs_ref, yc_ref, *rest, ell, blk, emit_v):
    if emit_v:
        gv_ref, wbf, wsbf = rest
    else:
        wbf, wsbf = rest
    i = pl.program_id(0)
    rows = h_ref.shape[0]
    c = GM_WIDTH

    @pl.when(i == 0)
    def _():
        wbf[...] = w_ref[...].astype(BF16)
        r = lax.broadcasted_iota(I32, (ell, ell), 0)
        s = lax.broadcasted_iota(I32, (ell, ell), 1)
        keep = (r >= s) & ((r // blk) == (s // blk))
        rsel = (lax.broadcasted_iota(I32, (ell, CHUNK), 0) % blk
                == lax.broadcasted_iota(I32, (ell, CHUNK), 1)).astype(BF16)
        csel = (lax.broadcasted_iota(I32, (CHUNK, ell), 1) % blk
                == lax.broadcasted_iota(I32, (CHUNK, ell), 0)).astype(BF16)
        for g in range(GM_GROUPS):
            wchunk = ws_ref[g].astype(BF16)
            if blk == ell:
                full = wchunk
            else:
                rowsp = jnp.dot(rsel, wchunk, preferred_element_type=F32).astype(BF16)
                full = jnp.dot(rowsp, csel, preferred_element_type=F32).astype(BF16)
            wsbf[g] = jnp.where(keep, full, jnp.zeros_like(full))

    uv = jnp.dot(h_ref[...], wbf[...], preferred_element_type=F32)
    u = uv[:, :c]
    v = uv[:, c:]
    vc = v - jnp.mean(v, axis=-1, keepdims=True)
    vn = vc * lax.rsqrt(jnp.mean(vc * vc, axis=-1, keepdims=True) + EPS) * lg_ref[...] + lb_ref[...]
    if emit_v:
        gv_ref[...] = vn
    vb = vn.astype(BF16)
    bs = bs_ref[...]
    for ch in range(rows // ell):
        rs = slice(ch * ell, (ch + 1) * ell)
        outs = []
        for g in range(GM_GROUPS):
            cs = slice(g * GM_GROUP, (g + 1) * GM_GROUP)
            mixed = jnp.dot(wsbf[g], vb[rs, cs], preferred_element_type=F32)
            mixed = (mixed.reshape(ell // blk, blk, GM_GROUP) + bs[:blk, g:g + 1][None]).reshape(ell, GM_GROUP)
            outs.append(u[rs, cs] * mixed)
        yc_ref[rs, :] = jnp.concatenate(outs, axis=-1).astype(BF16)


def _gmlp_mixer(h2, w_in, ln_g, ln_b, ws, bs_t, ell, blk, emit_v):
    rows, d = h2.shape
    c = GM_WIDTH
    kern = functools.partial(_gmlp_kernel, ell=ell, blk=blk, emit_v=emit_v)
    rt = min(rows, MATMUL_TILE)
    out_specs = [pl.BlockSpec((rt, c), lambda i: (i, 0))]
    out_shape = [jax.ShapeDtypeStruct((rows, c), BF16)]
    if emit_v:
        out_specs.append(pl.BlockSpec((rt, c), lambda i: (i, 0)))
        out_shape.append(jax.ShapeDtypeStruct((rows, c), F32))
    return pl.pallas_call(
        kern,
        grid=(rows // rt,),
        in_specs=[pl.BlockSpec((rt, d), lambda i: (i, 0)),
                  pl.BlockSpec((d, 2 * c), lambda i: (0, 0), pipeline_mode=pl.Buffered(1)),
                  pl.BlockSpec((1, c), lambda i: (0, 0)),
                  pl.BlockSpec((1, c), lambda i: (0, 0)),
                  pl.BlockSpec((GM_GROUPS, CHUNK, CHUNK), lambda i: (0, 0, 0)),
                  pl.BlockSpec((CHUNK, GM_GROUPS), lambda i: (0, 0))],
        out_specs=out_specs,
        out_shape=out_shape,
        scratch_shapes=[pltpu.VMEM((d, 2 * c), BF16), pltpu.VMEM((GM_GROUPS, ell, ell), BF16)],
        compiler_params=_cparams(1),
        name="gmlp_mixer",
    )(h2, w_in, ln_g.reshape(1, c), ln_b.reshape(1, c), ws, bs_t)


PAIR_W = 2 * HEAD_DIM
PAIRS_PER_KV = N_HEADS // N_KV // 2
NT_DIMS = (((1,), (1,)), ((), ()))


def _swa_project(i, h_ref, wq_ref, wkv_ref, wbf):
    nq = N_HEADS * HEAD_DIM

    @pl.when(i == 0)
    def _():
        wbf[:, :nq] = wq_ref[...].astype(BF16)
        wbf[:, nq:] = wkv_ref[...].astype(BF16)

    return jnp.dot(h_ref[...], wbf[...], preferred_element_type=F32)


def _pair_block_diag(a, a_swapped, hk, axis):
    dim_axis = 1 - axis
    low = lax.broadcasted_iota(I32, a.shape, dim_axis) < HEAD_DIM
    lo, hi = (a, a_swapped) if hk == 0 else (a_swapped, a)
    return jnp.concatenate([jnp.where(low, lo, 0.0), jnp.where(low, 0.0, hi)], axis=axis).astype(BF16)


def _stack_pairs(qkv, rs, hk):
    p0 = hk * PAIRS_PER_KV
    return jnp.concatenate([qkv[rs, (p0 + pp) * PAIR_W:(p0 + pp + 1) * PAIR_W]
                            for pp in range(PAIRS_PER_KV)], axis=0).astype(BF16)


def _swa_cached_kernel(h_ref, wq_ref, wkv_ref, kp_ref, vp_ref, bias_ref, sink_ref, yd_ref, k_ref, v_ref,
                       wbf, *, tq):
    i = pl.program_id(0)
    rows = h_ref.shape[0]
    nq = N_HEADS * HEAD_DIM
    nkv = N_KV * HEAD_DIM
    n_blocks = rows // tq
    qkv = _swa_project(i, h_ref, wq_ref, wkv_ref, wbf)
    k_new = qkv[:, nq:nq + nkv]
    v_new = qkv[:, nq + nkv:]
    k_ref[...] = k_new
    v_ref[...] = v_new
    pad = jnp.zeros((WINDOW - tq, nkv), F32)

    scores, vbds = [], []
    for blk in range(n_blocks):
        rs = slice(blk * tq, (blk + 1) * tq)
        kcat = jnp.concatenate([kp_ref[blk], k_new[rs], pad], axis=0)
        vcat = jnp.concatenate([vp_ref[blk], v_new[rs], pad], axis=0)
        kswap = pltpu.roll(kcat, HEAD_DIM, 1)
        vswap = pltpu.roll(vcat, HEAD_DIM, 1)
        per_head = []
        for hk in range(N_KV):
            kbd = _pair_block_diag(kcat, kswap, hk, 0)
            vbds.append(_pair_block_diag(vcat, vswap, hk, 0))
            s4 = lax.dot_general(_stack_pairs(qkv, rs, hk), kbd, NT_DIMS,
                                 preferred_element_type=F32) * (HEAD_DIM ** -0.5)
            for pp in range(PAIRS_PER_KV):
                for sub in range(2):
                    per_head.append(s4[pp * tq:(pp + 1) * tq, sub * 2 * WINDOW:(sub + 1) * 2 * WINDOW])
        scores.append(jnp.concatenate(per_head, axis=0))

    s_all = jnp.stack(scores, axis=0) + bias_ref[...][None]
    sink = sink_ref[...][None]
    m = jnp.maximum(jnp.max(s_all, axis=-1, keepdims=True), sink)
    pr = jnp.exp(s_all - m)
    pr = pr / (jnp.sum(pr, axis=-1, keepdims=True) + jnp.exp(sink - m))

    for blk in range(n_blocks):
        outs = []
        for hk in range(N_KV):
            p4 = []
            for pp in range(PAIRS_PER_KV):
                h0 = 2 * (hk * PAIRS_PER_KV + pp)
                p4.append(jnp.concatenate([pr[blk, h0 * tq:(h0 + 1) * tq, :],
                                           pr[blk, (h0 + 1) * tq:(h0 + 2) * tq, :]], axis=-1))
            o4 = jnp.dot(jnp.concatenate(p4, axis=0).astype(BF16), vbds[blk * N_KV + hk],
                         preferred_element_type=F32)
            outs.extend(o4[pp * tq:(pp + 1) * tq, :] for pp in range(PAIRS_PER_KV))
        yd_ref[blk * tq:(blk + 1) * tq, :] = jnp.concatenate(outs, axis=-1).astype(BF16)


def _swa_stream_kernel(h_ref, wq_ref, wkv_ref, bias_ref, sink_ref, yd_ref, k_ref, v_ref,
                       wbf, kprev, vprev_t, *, blocks_per_seq):
    i = pl.program_id(0)
    rows = h_ref.shape[0]
    nq = N_HEADS * HEAD_DIM
    nkv = N_KV * HEAD_DIM
    tq = WINDOW
    n_blocks = rows // tq

    @pl.when(i == 0)
    def _():
        kprev[...] = jnp.zeros_like(kprev)
        vprev_t[...] = jnp.zeros_like(vprev_t)

    qkv = _swa_project(i, h_ref, wq_ref, wkv_ref, wbf)
    k_new = qkv[:, nq:nq + nkv]
    v_new = qkv[:, nq + nkv:]
    k_ref[...] = k_new
    v_ref[...] = v_new
    v_new_t = v_new.T
    key = lax.broadcasted_iota(I32, (2, 2 * WINDOW, PAIRS_PER_KV * tq), 1)

    for blk in range(n_blocks):
        rs = slice(blk * tq, (blk + 1) * tq)
        first = (i * n_blocks + blk) % blocks_per_seq == 0
        k_cur = k_new[rs]
        v_cur_t = v_new_t[:, rs]
        kcat = jnp.concatenate([kprev[...], k_cur], axis=0)
        vcat_t = jnp.concatenate([vprev_t[...], v_cur_t], axis=1)
        kprev[...] = k_cur
        vprev_t[...] = v_cur_t
        kswap = pltpu.roll(kcat, HEAD_DIM, 1)
        vswap_t = pltpu.roll(vcat_t, HEAD_DIM, 0)
        outs = []
        for hk in range(N_KV):
            kbd = _pair_block_diag(kcat, kswap, hk, 0)
            vbd_t = _pair_block_diag(vcat_t, vswap_t, hk, 1)
            st = lax.dot_general(kbd, _stack_pairs(qkv, rs, hk), NT_DIMS,
                                 preferred_element_type=F32) * (HEAD_DIM ** -0.5)
            s3 = st.reshape(2, 2 * WINDOW, PAIRS_PER_KV * tq) + bias_ref[hk]
            s3 = jnp.where(first & (key < WINDOW), NEG_INF, s3)
            sink = sink_ref[hk]
            m = jnp.maximum(jnp.max(s3, axis=1, keepdims=True), sink)
            pr = jnp.exp(s3 - m)
            pr = pr / (jnp.sum(pr, axis=1, keepdims=True) + jnp.exp(sink - m))
            o_t = jnp.dot(vbd_t, pr.reshape(4 * WINDOW, PAIRS_PER_KV * tq).astype(BF16),
                          preferred_element_type=F32)
            o4 = o_t.T
            outs.extend(o4[pp * tq:(pp + 1) * tq, :] for pp in range(PAIRS_PER_KV))
        yd_ref[rs, :] = jnp.concatenate(outs, axis=-1).astype(BF16)


def _swa_weight_specs(w_in, d):
    nq = N_HEADS * HEAD_DIM
    nkv = N_KV * HEAD_DIM
    nw = nq + 2 * nkv
    q_blk = (w_in.shape[1] - nw) // nq
    kv_blk = (w_in.shape[1] - 2 * nkv) // (2 * nkv)
    assert q_blk * nq + nw == w_in.shape[1] and kv_blk * 2 * nkv + 2 * nkv == w_in.shape[1]
    return [pl.BlockSpec((d, nq), lambda i: (0, q_blk)), pl.BlockSpec((d, 2 * nkv), lambda i: (0, kv_blk))]


def _swa_outputs(rows, rt):
    nq = N_HEADS * HEAD_DIM
    nkv = N_KV * HEAD_DIM
    specs = [pl.BlockSpec((rt, nq), lambda i: (i, 0)),
             pl.BlockSpec((rt, nkv), lambda i: (i, 0)),
             pl.BlockSpec((rt, nkv), lambda i: (i, 0))]
    shapes = [jax.ShapeDtypeStruct((rows, nq), BF16),
              jax.ShapeDtypeStruct((rows, nkv), F32),
              jax.ShapeDtypeStruct((rows, nkv), F32)]
    return specs, shapes


def _swa_cached_mixer(h2, w_in, k_cache, v_cache, bias, sinks, tq):
    rows, d = h2.shape
    nkv = N_KV * HEAD_DIM
    nw = N_HEADS * HEAD_DIM + 2 * nkv
    n_blocks = ROW_TILE // tq
    cache_spec = pl.BlockSpec((n_blocks, WINDOW, nkv), lambda i: (i, 0, 0))
    out_specs, out_shape = _swa_outputs(rows, ROW_TILE)
    return pl.pallas_call(
        functools.partial(_swa_cached_kernel, tq=tq),
        grid=(rows // ROW_TILE,),
        in_specs=[pl.BlockSpec((ROW_TILE, d), lambda i: (i, 0))] + _swa_weight_specs(w_in, d)
        + [cache_spec, cache_spec,
           pl.BlockSpec((N_HEADS * tq, 2 * WINDOW), lambda i: (0, 0)),
           pl.BlockSpec((N_HEADS * tq, 1), lambda i: (0, 0))],
        out_specs=out_specs,
        out_shape=out_shape,
        scratch_shapes=[pltpu.VMEM((d, nw), BF16)],
        compiler_params=_cparams(1),
        name="swa_cached",
    )(h2, w_in, w_in, k_cache, v_cache, bias, sinks)


def _swa_stream_mixer(h2, w_in, bias_t, sinks_t, blocks_per_seq):
    rows, d = h2.shape
    nkv = N_KV * HEAD_DIM
    nw = N_HEADS * HEAD_DIM + 2 * nkv
    lanes = PAIRS_PER_KV * WINDOW
    rt = min(rows, MATMUL_TILE)
    out_specs, out_shape = _swa_outputs(rows, rt)
    return pl.pallas_call(
        functools.partial(_swa_stream_kernel, blocks_per_seq=blocks_per_seq),
        grid=(rows // rt,),
        in_specs=[pl.BlockSpec((rt, d), lambda i: (i, 0))] + _swa_weight_specs(w_in, d)
        + [pl.BlockSpec((N_KV, 2, 2 * WINDOW, lanes), lambda i: (0, 0, 0, 0)),
           pl.BlockSpec((N_KV, 2, 1, lanes), lambda i: (0, 0, 0, 0))],
        out_specs=out_specs,
        out_shape=out_shape,
        scratch_shapes=[pltpu.VMEM((d, nw), BF16), pltpu.VMEM((WINDOW, nkv), F32),
                        pltpu.VMEM((nkv, WINDOW), F32)],
        compiler_params=_cparams(1),
        name="swa_stream",
    )(h2, w_in, w_in, bias_t, sinks_t)


def _t5_bucket(dist):
    max_exact = N_BUCKETS // 2
    dd = np.maximum(dist, 1)
    large = max_exact + (np.log(dd / max_exact) / np.log(WINDOW / max_exact)
                         * (N_BUCKETS - max_exact)).astype(np.int64)
    large = np.minimum(large, N_BUCKETS - 1)
    return np.where(dist < max_exact, dist, large).astype(np.int32)


def _attention_bias(rel_bias):
    by_dist = jnp.take(rel_bias.astype(F32), _t5_bucket(np.arange(WINDOW)), axis=0).T
    neg = jnp.full((N_HEADS, WINDOW), NEG_INF, F32)
    line = jnp.concatenate([neg, by_dist[:, ::-1], neg[:, :WINDOW - 1]], axis=1)
    return jnp.stack([line[:, WINDOW - 1 - q:3 * WINDOW - 1 - q] for q in range(WINDOW)], axis=1)


def kernel(x_prompt, x_sample, state_pool, state_conv, cache_swa_k, cache_swa_v, c_prompt, c_sample, w_ada, b_ada, norm_g, final_norm_g, w_in_even, w_out_even, w_pool, pool_scale, conv_w, w_in_odd, w_out_odd, gm_norm_g, gm_norm_b, gm_w_s, gm_b_s, attn_sinks, rel_bias, w_router, b_router, w_gate, w_up, w_down):
    d = D_MODEL
    bp, tp, _ = x_prompt.shape
    bs, ts, _ = x_sample.shape
    rows_s = bs * ts
    assert rows_s == ROW_TILE and tp % ROUTER_TILE == 0 and PAST_LEN % CHUNK == 0
    assert bp <= V7X_SUBLANES and CHUNK % ts == 0

    n_c = bp + bs
    c_pad = (-n_c) % V7X_SUBLANES
    c_all = jnp.concatenate([c_prompt, c_sample, jnp.zeros((c_pad, d), F32)], axis=0)
    mod_p = _adaln(c_all, w_ada, b_ada)
    mod_s = jnp.repeat(mod_p[:, bp:bp + bs], ts, axis=1)

    tps_p = tp // ROW_TILE
    xp = x_prompt.reshape(bp * tp, d)
    xs_ = x_sample.reshape(rows_s, d)
    w_in0, w_in1 = w_in_even[0], w_in_odd[0]

    zero_pool = jnp.zeros((bp, POOL_STATE, POOL_WIDTH), F32)
    zero_conv = jnp.zeros((bp, CONV_K - 1, CONV_WIDTH), F32)
    hp0, ya_p, pool_p = _pool_mixer(xp, norm_g[0, 0], mod_p, 0, w_in0, w_pool[0], pool_scale[0],
                                    zero_pool, 1, MATMUL_TILE, 0)
    hs0, ya_s, pool_s = _pool_mixer(xs_, norm_g[0, 0], mod_s, 0, w_in0, w_pool[0], pool_scale[0],
                                    state_pool[0], bs, ts, PAST_LEN)
    yb_p, conv_p = _conv_mixer(hp0, w_in0, conv_w[0], zero_conv, 1, MATMUL_TILE)
    yb_s, conv_s = _conv_mixer(hs0, w_in0, conv_w[0], state_conv[0], bs, ts)
    out_p = _outproj(ya_p, yb_p, xp, mod_p, 0, norm_g[0, 1], w_out_even[0], tp)
    out_s = _outproj(ya_s, yb_s, xs_, mod_s, 0, norm_g[0, 1], w_out_even[0], rows_s)
    (x2p, h1p), (x2s, h1s) = _moe(out_p, out_s, mod_p, mod_s, 0, norm_g[1, 0], tps_p,
                                  w_router, b_router, w_gate, w_up, w_down, final=False)

    bs_t = gm_b_s[0].T
    (yc_p,) = _gmlp_mixer(h1p, w_in1, gm_norm_g[0], gm_norm_b[0], gm_w_s[0], bs_t, CHUNK, CHUNK, False)
    yc_s, gv_s = _gmlp_mixer(h1s, w_in1, gm_norm_g[0], gm_norm_b[0], gm_w_s[0], bs_t, rows_s, ts, True)
    bias = _attention_bias(rel_bias)
    nkv = N_KV * HEAD_DIM
    bias_t = jnp.transpose(bias.reshape(N_KV, PAIRS_PER_KV, 2, WINDOW, 2 * WINDOW), (0, 2, 4, 1, 3))
    bias_t = bias_t.reshape(N_KV, 2, 2 * WINDOW, PAIRS_PER_KV * WINDOW)
    sinks_t = jnp.transpose(attn_sinks[0].reshape(N_KV, PAIRS_PER_KV, 2), (0, 2, 1))
    sinks_t = jnp.repeat(sinks_t, WINDOW, axis=-1).reshape(N_KV, 2, 1, PAIRS_PER_KV * WINDOW)
    yd_p, k_p, v_p = _swa_stream_mixer(h1p, w_in1, bias_t, sinks_t, tp // WINDOW)
    yd_s, k_s, v_s = _swa_cached_mixer(h1s, w_in1, cache_swa_k[0].reshape(bs, WINDOW, nkv),
                                       cache_swa_v[0].reshape(bs, WINDOW, nkv),
                                       bias[:, :ts, :].reshape(N_HEADS * ts, 2 * WINDOW),
                                       jnp.repeat(attn_sinks[0], ts).reshape(-1, 1), ts)
    out_p = _outproj(yc_p, yd_p, x2p, mod_p, 1, norm_g[1, 1], w_out_odd[0], tp)
    out_s = _outproj(yc_s, yd_s, x2s, mod_s, 1, norm_g[1, 1], w_out_odd[0], rows_s)
    (yp,), (ys_out,) = _moe(out_p, out_s, mod_p, mod_s, 1, final_norm_g, tps_p,
                            w_router, b_router, w_gate, w_up, w_down, final=True)

    k_p4 = k_p.reshape(bp, tp, N_KV, HEAD_DIM)[:, -WINDOW:]
    v_p4 = v_p.reshape(bp, tp, N_KV, HEAD_DIM)[:, -WINDOW:]
    k_s4 = jnp.concatenate([cache_swa_k[0], k_s.reshape(bs, ts, N_KV, HEAD_DIM)], axis=1)[:, -WINDOW:]
    v_s4 = jnp.concatenate([cache_swa_v[0], v_s.reshape(bs, ts, N_KV, HEAD_DIM)], axis=1)[:, -WINDOW:]
    return (yp.reshape(bp, tp, d), ys_out.reshape(bs, ts, d),
            pool_p[None], pool_s[None], conv_p[None], conv_s[None],
            k_p4[None], k_s4[None], v_p4[None], v_s4[None],
            gv_s.reshape(bs, ts, GM_WIDTH)[None])
```

```python
import functools

import numpy as np
import jax
import jax.numpy as jnp
from jax import lax
from jax.experimental import pallas as pl
from jax.experimental.pallas import tpu as pltpu

F32 = jnp.float32
BF16 = jnp.bfloat16
I32 = jnp.int32
U32 = jnp.uint32

D_MODEL = 2048
POOL_WINDOWS = (2, 4, 8, 16)
POOL_WIDTH = 1024
POOL_GROUP = 256
POOL_STATE = 15
CONV_WIDTH = 1024
CONV_K = 3
GM_WIDTH = 1024
GM_GROUPS = 8
GM_GROUP = 128
CHUNK = 128
HEAD_DIM = 64
N_HEADS = 16
N_KV = 2
WINDOW = 128
N_BUCKETS = 32
N_EXPERTS = 16
N_EXPERT_GROUPS = 4
EXP_PER_GROUP = 4
TOP_K = 2
EPS = 1e-6
NEG_INF = -1e30
PAST_LEN = 16384

V7X_SUBLANES = 8
V7X_LANES = 128
VMEM_LIMIT = 56 * 1024 * 1024

ROW_TILE = 256
MATMUL_TILE = 512
SUB_TILE = 256
ROUTER_TILE = 1024
POOL_HALO = 16
CONV_HALO = 8
MOE_TILE = 256
TAB_EXPERT, TAB_VALID, TAB_LAST_TILE, TAB_NUSED, TAB_NEXT = 0, 1, 2, 3, 4


def _cparams(n_axes):
    return pltpu.CompilerParams(dimension_semantics=("arbitrary",) * n_axes,
                                vmem_limit_bytes=VMEM_LIMIT)


def _rms(x, g):
    return x * lax.rsqrt(jnp.mean(x * x, axis=-1, keepdims=True) + EPS) * g


def _mod_spec(mod, layer, part):
    nrow = ROW_TILE if mod.shape[1] == ROW_TILE else V7X_SUBLANES
    return pl.BlockSpec((1, nrow, D_MODEL), lambda *_: (layer, 0, part))


def _mod_rows(m_ref, seq, rs=None):
    if m_ref.shape[1] == V7X_SUBLANES:
        return m_ref[0, pl.ds(seq, 1), :]
    return m_ref[0] if rs is None else m_ref[0, rs, :]


def _adaln_kernel(c_ref, w_ref, b_ref, o_ref):
    c = c_ref[...]
    a = (c * jax.nn.sigmoid(c)).astype(BF16)
    o_ref[0] = jnp.dot(a, w_ref[0].astype(BF16), preferred_element_type=F32) + b_ref[0]


def _adaln(c_all, w_ada, b_ada):
    depth, d, n6 = w_ada.shape
    m = c_all.shape[0]
    tn = 1024
    return pl.pallas_call(
        _adaln_kernel,
        grid=(depth, n6 // tn),
        in_specs=[pl.BlockSpec((m, d), lambda l, j: (0, 0)),
                  pl.BlockSpec((1, d, tn), lambda l, j: (l, 0, j)),
                  pl.BlockSpec((1, 1, tn), lambda l, j: (l, 0, j))],
        out_specs=pl.BlockSpec((1, m, tn), lambda l, j: (l, 0, j)),
        out_shape=jax.ShapeDtypeStruct((depth, m, n6), F32),
        compiler_params=_cparams(2),
        name="adaln",
    )(c_all, w_ada, b_ada.reshape(depth, 1, n6))


def _pool_kernel(x_ref, g_ref, sc_ref, sh_ref, w_ref, wp_ref, ps_ref, st_ref, h_ref, ya_ref, ns_ref,
                 wbf, wpbf, carry, *, nb, tm, tiles_per_seq, start):
    i = pl.program_id(0)
    t = i % tiles_per_seq
    seq = i // tiles_per_seq
    c = POOL_WIDTH
    halo = POOL_HALO

    @pl.when(i == 0)
    def _():
        wbf[...] = w_ref[...].astype(BF16)
        wpbf[...] = wp_ref[...].astype(BF16)

    @pl.when(t == 0)
    def _():
        carry[...] = st_ref[...]

    h = (_rms(x_ref[...], g_ref[...]) * (1.0 + _mod_rows(sc_ref, seq)) + _mod_rows(sh_ref, seq)).astype(BF16)
    h_ref[...] = h
    p = jnp.dot(h, wbf[...], preferred_element_type=F32)
    p3 = p.reshape(nb, tm, c)
    ext3 = jnp.concatenate([carry[...], p3], axis=1)
    tail = ext3[:, tm:tm + halo, :]
    ns_ref[...] = tail
    carry[...] = tail
    ext = ext3.reshape(nb * (halo + tm), c)
    pos = start + t * tm + lax.broadcasted_iota(I32, (1, tm, 1), 1)
    outs = []
    for gi, w in enumerate(POOL_WINDOWS):
        sl = slice(gi * POOL_GROUP, (gi + 1) * POOL_GROUP)
        acc = ext[:, sl]
        shift = 1
        while shift < w:
            acc = acc + pltpu.roll(acc, shift, 0)
            shift *= 2
        win = acc.reshape(nb, halo + tm, POOL_GROUP)[:, halo:, :]
        cnt = jnp.minimum(pos + 1, w).astype(F32)
        dgrp = win / cnt - p3[:, :, sl]
        outs.append(jnp.dot(dgrp.reshape(nb * tm, POOL_GROUP).astype(BF16), wpbf[gi],
                            preferred_element_type=F32))
    y = jnp.concatenate(outs, axis=-1) * ps_ref[...]
    ya_ref[...] = y.astype(BF16)


def _pool_mixer(x2, g, mod, layer, w_in, w_pool, pool_scale, state, nb, tm, start):
    rows, d = x2.shape
    nseq = state.shape[0]
    tiles_per_seq = (rows // nseq) // tm
    seq_blocks = nseq // nb
    c = POOL_WIDTH
    st = jnp.pad(state, ((0, 0), (POOL_HALO - POOL_STATE, 0), (0, 0)))
    kern = functools.partial(_pool_kernel, nb=nb, tm=tm, tiles_per_seq=tiles_per_seq, start=start)
    h2, ya, ns = pl.pallas_call(
        kern,
        grid=(seq_blocks * tiles_per_seq,),
        in_specs=[pl.BlockSpec((nb * tm, d), lambda i: (i, 0)),
                  pl.BlockSpec((1, d), lambda i: (0, 0)),
                  _mod_spec(mod, layer, 1), _mod_spec(mod, layer, 0),
                  pl.BlockSpec((d, c), lambda i: (0, 0)),
                  pl.BlockSpec((len(POOL_WINDOWS), POOL_GROUP, POOL_GROUP), lambda i: (0, 0, 0)),
                  pl.BlockSpec((1, c), lambda i: (0, 0)),
                  pl.BlockSpec((nb, POOL_HALO, c), lambda i: (i // tiles_per_seq, 0, 0))],
        out_specs=[pl.BlockSpec((nb * tm, d), lambda i: (i, 0)),
                   pl.BlockSpec((nb * tm, c), lambda i: (i, 0)),
                   pl.BlockSpec((nb, POOL_HALO, c), lambda i: (i // tiles_per_seq, 0, 0))],
        out_shape=[jax.ShapeDtypeStruct((rows, d), BF16),
                   jax.ShapeDtypeStruct((rows, c), BF16),
                   jax.ShapeDtypeStruct((nseq, POOL_HALO, c), F32)],
        scratch_shapes=[pltpu.VMEM((d, c), BF16),
                        pltpu.VMEM((len(POOL_WINDOWS), POOL_GROUP, POOL_GROUP), BF16),
                        pltpu.VMEM((nb, POOL_HALO, c), F32)],
        compiler_params=_cparams(1),
        name="pool_mixer",
    )(x2, g.reshape(1, d), mod, mod, w_in, w_pool, pool_scale.reshape(1, c), st)
    return h2, ya, ns[:, POOL_HALO - POOL_STATE:, :]


def _conv_kernel(h_ref, wx_ref, wb_ref, wc_ref, cw_ref, st_ref, yb_ref, ns_ref,
                 wxbf, wbbf, wcbf, carry, *, nb, tm, tiles_per_seq):
    i = pl.program_id(1)
    t = i % tiles_per_seq
    tc = wxbf.shape[1]
    halo = CONV_HALO

    @pl.when(i == 0)
    def _():
        wxbf[...] = wx_ref[...].astype(BF16)
        wbbf[...] = wb_ref[...].astype(BF16)
        wcbf[...] = wc_ref[...].astype(BF16)

    @pl.when(t == 0)
    def _():
        carry[...] = st_ref[...]

    h = h_ref[...]
    xin = jnp.dot(h, wxbf[...], preferred_element_type=F32)
    gb = jnp.dot(h, wbbf[...], preferred_element_type=F32)
    gc = jnp.dot(h, wcbf[...], preferred_element_type=F32)
    z3 = (gc * xin).reshape(nb, tm, tc)
    ext3 = jnp.concatenate([carry[...], z3], axis=1)
    tail = ext3[:, tm:tm + halo, :]
    ns_ref[...] = tail
    carry[...] = tail
    ext = ext3.reshape(nb * (halo + tm), tc)
    cw = cw_ref[...]
    conv = cw[0:1, :] * pltpu.roll(ext, 2, 0) + cw[1:2, :] * pltpu.roll(ext, 1, 0) + cw[2:3, :] * ext
    conv = conv.reshape(nb, halo + tm, tc)[:, halo:, :].reshape(nb * tm, tc)
    yb_ref[...] = (gb * conv).astype(BF16)


def _conv_mixer(h2, w_in, conv_w, state, nb, tm):
    rows, d = h2.shape
    nseq = state.shape[0]
    tiles_per_seq = (rows // nseq) // tm
    seq_blocks = nseq // nb
    c = CONV_WIDTH
    tc = 512
    cb = c // tc
    base = POOL_WIDTH // tc
    st = jnp.pad(state, ((0, 0), (CONV_HALO - (CONV_K - 1), 0), (0, 0)))
    kern = functools.partial(_conv_kernel, nb=nb, tm=tm, tiles_per_seq=tiles_per_seq)
    yb, ns = pl.pallas_call(
        kern,
        grid=(cb, seq_blocks * tiles_per_seq),
        in_specs=[pl.BlockSpec((nb * tm, d), lambda j, i: (i, 0)),
                  pl.BlockSpec((d, tc), lambda j, i: (0, base + j)),
                  pl.BlockSpec((d, tc), lambda j, i: (0, base + cb + j)),
                  pl.BlockSpec((d, tc), lambda j, i: (0, base + 2 * cb + j)),
                  pl.BlockSpec((CONV_K, tc), lambda j, i: (0, j)),
                  pl.BlockSpec((nb, CONV_HALO, tc), lambda j, i: (i // tiles_per_seq, 0, j))],
        out_specs=[pl.BlockSpec((nb * tm, tc), lambda j, i: (i, j)),
                   pl.BlockSpec((nb, CONV_HALO, tc), lambda j, i: (i // tiles_per_seq, 0, j))],
        out_shape=[jax.ShapeDtypeStruct((rows, c), BF16),
                   jax.ShapeDtypeStruct((nseq, CONV_HALO, c), F32)],
        scratch_shapes=[pltpu.VMEM((d, tc), BF16)] * 3 + [pltpu.VMEM((nb, CONV_HALO, tc), F32)],
        compiler_params=_cparams(2),
        name="conv_mixer",
    )(h2, w_in, w_in, w_in, conv_w, st)
    return yb, ns[:, CONV_HALO - (CONV_K - 1):, :]


def _pack_bf16_pairs(v):
    c = v.shape[1] // 2
    return pltpu.bitcast(pltpu.pack_elementwise([v[:, :c], v[:, c:]], packed_dtype=BF16), U32)


def _store_token_tiles(ref, v):
    rows = v.shape[0]
    for j in range(V7X_SUBLANES):
        ref[pl.ds(j, rows, stride=V7X_SUBLANES), :] = v[:, j * V7X_LANES:(j + 1) * V7X_LANES]


def _load_token_tiles(ref):
    rows = ref.shape[0] // V7X_SUBLANES
    return jnp.concatenate([ref[pl.ds(j, rows, stride=V7X_SUBLANES), :] for j in range(V7X_SUBLANES)],
                           axis=-1)


def _unpack_pairs_f32(w):
    return tuple(pltpu.unpack_elementwise(w, index=k, packed_dtype=BF16, unpacked_dtype=F32) for k in range(2))


def _unpack_bf16_pairs(w):
    lo, hi = _unpack_pairs_f32(w)
    return lo.astype(BF16), hi.astype(BF16)


def _outproj_kernel(ya_ref, yb_ref, x_ref, g1_ref, sc_ref, sh_ref, ng_ref, wo_ref,
                    x1_ref, hp_ref, hpt_ref, wobf, *, tiles_per_seq):
    i = pl.program_id(0)
    seq = i // tiles_per_seq
    half = ya_ref.shape[1]

    @pl.when(i == 0)
    def _():
        wobf[...] = wo_ref[...].astype(BF16)

    rows = x_ref.shape[0]
    sr = min(rows, SUB_TILE)
    subs = [pl.ds(s * sr, sr) for s in range(rows // sr)]
    ys = [jnp.dot(ya_ref[rs, :], wobf[:half, :], preferred_element_type=F32)
          + jnp.dot(yb_ref[rs, :], wobf[half:, :], preferred_element_type=F32) for rs in subs]
    for s, (rs, y) in enumerate(zip(subs, ys)):
        x1 = x_ref[rs, :] + _mod_rows(g1_ref, seq, rs) * y
        x1_ref[rs, :] = x1
        h2 = _rms(x1, ng_ref[...]) * (1.0 + _mod_rows(sc_ref, seq, rs)) + _mod_rows(sh_ref, seq, rs)
        packed = _pack_bf16_pairs(h2)
        hp_ref[rs, :] = packed
        _store_token_tiles(hpt_ref.at[pl.ds(s * sr * V7X_SUBLANES, sr * V7X_SUBLANES), :], packed)


def _outproj(ya, yb, x2, mod, layer, ng, w_out, seq_rows):
    rows_all, d = x2.shape
    half = ya.shape[1]
    rt = ROW_TILE
    row_spec = lambda w: pl.BlockSpec((rt, w), lambda i: (i, 0))
    return pl.pallas_call(
        functools.partial(_outproj_kernel, tiles_per_seq=seq_rows // rt),
        grid=(rows_all // rt,),
        in_specs=[row_spec(half), row_spec(half), row_spec(d),
                  _mod_spec(mod, layer, 2), _mod_spec(mod, layer, 4), _mod_spec(mod, layer, 3),
                  pl.BlockSpec((1, d), lambda i: (0, 0)),
                  pl.BlockSpec((d, d), lambda i: (0, 0), pipeline_mode=pl.Buffered(1))],
        out_specs=[row_spec(d), row_spec(d // 2),
                   pl.BlockSpec((rt * V7X_SUBLANES, V7X_LANES), lambda i: (i, 0))],
        out_shape=[jax.ShapeDtypeStruct((rows_all, d), F32),
                   jax.ShapeDtypeStruct((rows_all, d // 2), U32),
                   jax.ShapeDtypeStruct((rows_all * V7X_SUBLANES, V7X_LANES), U32)],
        scratch_shapes=[pltpu.VMEM((d, d), BF16)],
        compiler_params=_cparams(1),
        name="outproj",
    )(ya, yb, x2, mod, mod, mod, ng.reshape(1, d), w_out)


def _router_kernel(hpp_ref, hps_ref, wr_ref, br_ref, pos_ref, rw_ref, tab_ref,
                   cnt_acc, totals, starts, padded, *, nt_p, rows_s):
    ph = pl.program_id(0)
    t = pl.program_id(1)
    last = nt_p
    r = hpp_ref.shape[0]
    half = hpp_ref.shape[1]
    ne = N_EXPERTS
    sub = lax.broadcasted_iota(I32, (ne, V7X_LANES), 0)

    @pl.when(t == 0)
    def _():
        cnt_acc[...] = jnp.zeros_like(cnt_acc)

    @pl.when((ph == 0) & (t == 0))
    def _():
        starts[...] = jnp.zeros_like(starts)
        padded[...] = jnp.zeros_like(padded)

    @pl.when((ph == 1) & (t == 0))
    def _():
        pad = jnp.floor((totals[...] + (MOE_TILE - 1.0)) * (1.0 / MOE_TILE)) * MOE_TILE
        run = pad
        k = 1
        while k < ne:
            run = run + jnp.where(sub >= k, pltpu.roll(run, k, 0), 0.0)
            k *= 2
        padded[...] = pad
        starts[...] = run - pad

    is_s = t == last
    w_s = jnp.concatenate([hps_ref[...], jnp.zeros((r - rows_s, half), U32)], axis=0)
    w = jnp.where(is_s, w_s, hpp_ref[...])
    lo, hi = _unpack_bf16_pairs(w)
    wr = wr_ref[...].astype(BF16)
    nt_dims = (((1,), (1,)), ((), ()))
    log_t = (lax.dot_general(wr[:, :half], lo, nt_dims, preferred_element_type=F32)
             + lax.dot_general(wr[:, half:], hi, nt_dims, preferred_element_type=F32))

    s = jax.nn.sigmoid(log_t)
    sg = s + br_ref[...]
    eid = lax.broadcasted_iota(I32, (ne, r), 0)
    within = eid % EXP_PER_GROUP
    grp = eid // EXP_PER_GROUP

    def group_rot(x, k):
        return jnp.where(within + k < EXP_PER_GROUP,
                         pltpu.roll(x, ne - k, 0), pltpu.roll(x, EXP_PER_GROUP - k, 0))

    rank = jnp.zeros((ne, r), I32)
    for k in range(1, EXP_PER_GROUP):
        mate = group_rot(sg, k)
        wrapped = within + k >= EXP_PER_GROUP
        ahead = (mate > sg) | (wrapped & (mate == sg))
        rank = rank + ahead.astype(I32)
    top2 = rank < TOP_K
    kept = jnp.where(top2, sg, 0.0)
    gscore = kept
    for k in range(1, EXP_PER_GROUP):
        gscore = gscore + group_rot(kept, k)
    win = None
    for k in range(1, N_EXPERT_GROUPS):
        other = pltpu.roll(gscore, EXP_PER_GROUP * k, 0)
        beats = (gscore > other) | ((grp < k) & (gscore == other))
        win = beats if win is None else (win & beats)
    n_valid = jnp.where(is_s, rows_s, r)
    tok = lax.broadcasted_iota(I32, (ne, r), 1)
    sel = top2 & win & (tok < n_valid)
    picked = jnp.where(sel, s, 0.0)
    wsum = jnp.sum(picked, axis=0, keepdims=True)
    gate = picked / jnp.where(tok[0:1, :] < n_valid, wsum, 1.0)

    src = lax.broadcasted_iota(I32, (r, r), 0)
    dst = lax.broadcasted_iota(I32, (r, r), 1)
    before = (src < dst).astype(BF16)
    selb = sel.astype(F32)
    ranks = jnp.dot(selb.astype(BF16), before, preferred_element_type=F32)
    slot = (starts[...][:, 0:1] + cnt_acc[...][:, 0:1] + ranks).astype(I32)
    cnt_new = cnt_acc[...] + jnp.sum(selb, axis=1, keepdims=True)
    cnt_acc[...] = cnt_new

    @pl.when((ph == 0) & (t == last))
    def _():
        totals[...] = cnt_new

    e_a = jnp.min(jnp.where(sel, eid, ne), axis=0, keepdims=True)
    e_b = jnp.max(jnp.where(sel, eid, -1), axis=0, keepdims=True)
    is_a = sel & (eid == e_a)
    is_b = sel & (eid == e_b)
    pos_a = jnp.sum(jnp.where(is_a, slot, 0), axis=0, keepdims=True)
    pos_b = jnp.sum(jnp.where(is_b, slot, 0), axis=0, keepdims=True)
    w_a = jnp.sum(jnp.where(is_a, gate, 0.0), axis=0, keepdims=True)
    w_b = jnp.sum(jnp.where(is_b, gate, 0.0), axis=0, keepdims=True)
    pos_ref[0] = jnp.concatenate([pos_a, pos_b], axis=0)
    wmat = jnp.concatenate([w_a, w_b, jnp.zeros((V7X_LANES - 2, r), F32)], axis=0)
    rw_ref[...] = wmat.T

    @pl.when((ph == 1) & (t == last))
    def _():
        ends = starts[...] + padded[...]
        lane = lax.broadcasted_iota(I32, (ne, V7X_LANES), 1)
        tile_start = (lane * MOE_TILE).astype(F32)
        te = jnp.sum((tile_start >= ends).astype(I32), axis=0, keepdims=True)
        valid = te < ne
        last_e = jnp.max(jnp.where(padded[...] > 0.0, sub, 0), axis=0, keepdims=True)
        te = jnp.where(valid, te, last_e)
        n_used = jnp.sum(valid.astype(I32), axis=1, keepdims=True) + jnp.zeros((1, V7X_LANES), I32)
        last_tile = jnp.where(padded[...] > 0.0, ends - MOE_TILE, -1.0).astype(I32)
        last_tile_row = jnp.sum(jnp.where(sub == lane, last_tile, 0), axis=0, keepdims=True)
        later = jnp.min(jnp.where((sub > te) & (padded[...] > 0.0), sub, ne), axis=0, keepdims=True)
        next_e = jnp.where(later < ne, later, -1)
        zero = jnp.zeros((1, V7X_LANES), I32)
        tab_ref[...] = jnp.concatenate([te, valid.astype(I32), last_tile_row, n_used, next_e,
                                        zero, zero, zero], axis=0)


def _router(hp_p, hp_s, w_router, b_router):
    n_p, half = hp_p.shape
    rows_s = hp_s.shape[0]
    r = ROUTER_TILE
    nt_p = n_p // r
    nt = nt_p + 1
    kern = functools.partial(_router_kernel, nt_p=nt_p, rows_s=rows_s)
    pos, rw, tab = pl.pallas_call(
        kern,
        grid=(2, nt),
        in_specs=[pl.BlockSpec((r, half), lambda p, t: (jnp.minimum(t, nt_p - 1), 0)),
                  pl.BlockSpec((rows_s, half), lambda p, t: (0, 0)),
                  pl.BlockSpec((N_EXPERTS, 2 * half), lambda p, t: (0, 0)),
                  pl.BlockSpec((N_EXPERTS, 1), lambda p, t: (0, 0))],
        out_specs=[pl.BlockSpec((1, TOP_K, r), lambda p, t: (p * t, 0, 0)),
                   pl.BlockSpec((r, V7X_LANES), lambda p, t: (p * t, 0)),
                   pl.BlockSpec((V7X_SUBLANES, V7X_LANES), lambda p, t: (0, 0))],
        out_shape=[jax.ShapeDtypeStruct((nt, TOP_K, r), I32),
                   jax.ShapeDtypeStruct((nt * r, V7X_LANES), F32),
                   jax.ShapeDtypeStruct((V7X_SUBLANES, V7X_LANES), I32)],
        scratch_shapes=[pltpu.VMEM((N_EXPERTS, V7X_LANES), F32)] * 4,
        compiler_params=_cparams(2),
        name="router",
    )(hp_p, hp_s, w_router.T, b_router.reshape(N_EXPERTS, 1))
    return pos.reshape(-1), rw, tab.reshape(-1)


def _pos_index(tok0):
    return (tok0 // ROUTER_TILE) * (TOP_K * ROUTER_TILE) + tok0 % ROUTER_TILE


def _tokens(ref, first, n=1):
    start = pl.multiple_of(first * V7X_SUBLANES, V7X_SUBLANES)
    return ref.at[pl.ds(start, n * V7X_SUBLANES), :]


def _dispatch_kernel(pos_ref, tab_ref, hpp_ref, hps_ref, xs_ref, zbuf, sem, *, n_p_steps):
    i = pl.program_id(0)
    rows = hpp_ref.shape[0] // V7X_SUBLANES

    @pl.when(i == 0)
    def _():
        zbuf[...] = jnp.zeros_like(zbuf)

        def fill(e):
            first = pl.multiple_of(tab_ref[TAB_LAST_TILE * V7X_LANES + e], MOE_TILE)
            return pltpu.make_async_copy(zbuf, _tokens(xs_ref, first, MOE_TILE), sem)

        for e in range(N_EXPERTS):
            @pl.when(tab_ref[TAB_LAST_TILE * V7X_LANES + e] >= 0)
            def _():
                fill(e).start()
        for e in range(N_EXPERTS):
            @pl.when(tab_ref[TAB_LAST_TILE * V7X_LANES + e] >= 0)
            def _():
                fill(e).wait()

        def tail(j):
            first = pl.multiple_of(j * MOE_TILE, MOE_TILE)
            return pltpu.make_async_copy(zbuf, _tokens(xs_ref, first, MOE_TILE), sem)

        def tail_start(j, carry):
            tail(j).start()
            return carry

        def tail_wait(j, carry):
            tail(j).wait()
            return carry

        n_used = tab_ref[TAB_NUSED * V7X_LANES]
        n_tiles = xs_ref.shape[0] // (MOE_TILE * V7X_SUBLANES)
        lax.fori_loop(n_used, n_tiles, tail_start, 0)
        lax.fori_loop(n_used, n_tiles, tail_wait, 0)

    base = _pos_index(i * rows)

    def scatter(src_ref):
        def row_copy(r, dst):
            return pltpu.make_async_copy(_tokens(src_ref, r), _tokens(xs_ref, dst), sem)

        def issue(r, carry):
            row_copy(r, pos_ref[base + r]).start()
            row_copy(r, pos_ref[base + ROUTER_TILE + r]).start(priority=1)
            return carry

        lax.fori_loop(0, rows, issue, 0, unroll=8)
        block = pltpu.make_async_copy(src_ref, _tokens(xs_ref, 0, rows), sem)
        for _ in range(TOP_K):
            block.wait()

    @pl.when(i < n_p_steps)
    def _():
        scatter(hpp_ref)

    @pl.when(i == n_p_steps)
    def _():
        scatter(hps_ref)


def _dispatch(pos, tab, hpt_p, hpt_s, n_rows_sorted):
    sub = V7X_SUBLANES
    n_p_steps = hpt_p.shape[0] // (ROW_TILE * sub)
    assert hpt_s.shape[0] == ROW_TILE * sub
    kern = functools.partial(_dispatch_kernel, n_p_steps=n_p_steps)
    blk = (ROW_TILE * sub, V7X_LANES)
    return pl.pallas_call(
        kern,
        grid_spec=pltpu.PrefetchScalarGridSpec(
            num_scalar_prefetch=2,
            grid=(n_p_steps + 1,),
            in_specs=[pl.BlockSpec(blk, lambda i, p, t: (jnp.minimum(i, n_p_steps - 1), 0)),
                      pl.BlockSpec(blk, lambda i, p, t: (0, 0))],
            out_specs=pl.BlockSpec(memory_space=pl.ANY),
            scratch_shapes=[pltpu.VMEM((MOE_TILE * sub, V7X_LANES), U32), pltpu.SemaphoreType.DMA(())]),
        out_shape=jax.ShapeDtypeStruct((n_rows_sorted * sub, V7X_LANES), U32),
        compiler_params=_cparams(1),
        name="moe_dispatch",
    )(pos, tab, hpt_p, hpt_s)


def _experts_kernel(tab_ref, xs_ref, wg_hbm, wu_hbm, wd_hbm, ys_ref,
                    wg32, wu32, wd32, wgbf, wubf, wdbf, slot_ref, sems, *, layer):
    i = pl.program_id(0)
    expert = tab_ref[TAB_EXPERT * V7X_LANES + i]
    prev = tab_ref[TAB_EXPERT * V7X_LANES + jnp.maximum(i - 1, 0)]
    upcoming = tab_ref[TAB_NEXT * V7X_LANES + i]
    changed = (i == 0) | (expert != prev)
    half = V7X_SUBLANES * V7X_LANES

    def weight_copies(e, slot):
        return (pltpu.make_async_copy(wg_hbm.at[layer, e], wg32.at[slot], sems.at[0, slot]),
                pltpu.make_async_copy(wu_hbm.at[layer, e], wu32.at[slot], sems.at[1, slot]),
                pltpu.make_async_copy(wd_hbm.at[layer, e], wd32.at[slot], sems.at[2, slot]))

    @pl.when(i == 0)
    def _():
        slot_ref[0] = 0
        for cp in weight_copies(expert, 0):
            cp.start()

    @pl.when(changed & (i > 0))
    def _():
        slot_ref[0] = 1 - slot_ref[0]

    for slot in range(2):
        @pl.when(changed & (slot_ref[0] == slot))
        def _():
            for cp in weight_copies(expert, slot):
                cp.wait()
            wgbf[...] = wg32[slot].astype(BF16)
            wubf[...] = wu32[slot].astype(BF16)
            wdbf[...] = wd32[slot].astype(BF16)

            @pl.when(upcoming >= 0)
            def _():
                for cp in weight_copies(upcoming, 1 - slot):
                    cp.start(priority=1)

    @pl.when(tab_ref[TAB_VALID * V7X_LANES + i] > 0)
    def _():
        lo, hi = _unpack_bf16_pairs(_load_token_tiles(xs_ref))
        a = (jnp.dot(lo, wgbf[:half, :], preferred_element_type=F32)
             + jnp.dot(hi, wgbf[half:, :], preferred_element_type=F32))
        b = (jnp.dot(lo, wubf[:half, :], preferred_element_type=F32)
             + jnp.dot(hi, wubf[half:, :], preferred_element_type=F32))
        hid = (a * jax.nn.sigmoid(a)) * b
        y = jnp.dot(hid.astype(BF16), wdbf[...], preferred_element_type=F32)
        _store_token_tiles(ys_ref, _pack_bf16_pairs(y))

    @pl.when(tab_ref[TAB_VALID * V7X_LANES + i] == 0)
    def _():
        ys_ref[...] = jnp.zeros_like(ys_ref)


def _experts(tab, xs, w_gate, w_up, w_down, layer):
    sub = V7X_SUBLANES
    _, _, d, f = w_gate.shape
    nt = xs.shape[0] // (MOE_TILE * sub)
    assert nt <= V7X_LANES and d == 2 * sub * V7X_LANES
    blk = (MOE_TILE * sub, V7X_LANES)

    def tile(i, tab_ref):
        return jnp.minimum(i, tab_ref[TAB_NUSED * V7X_LANES] - 1)

    hbm = pl.BlockSpec(memory_space=pl.ANY)
    return pl.pallas_call(
        functools.partial(_experts_kernel, layer=layer),
        grid_spec=pltpu.PrefetchScalarGridSpec(
            num_scalar_prefetch=1,
            grid=(nt,),
            in_specs=[pl.BlockSpec(blk, lambda i, t: (tile(i, t), 0)), hbm, hbm, hbm],
            out_specs=pl.BlockSpec(blk, lambda i, t: (i, 0)),
            scratch_shapes=[pltpu.VMEM((2, d, f), F32), pltpu.VMEM((2, d, f), F32), pltpu.VMEM((2, f, d), F32),
                            pltpu.VMEM((d, f), BF16), pltpu.VMEM((d, f), BF16), pltpu.VMEM((f, d), BF16),
                            pltpu.SMEM((1,), I32), pltpu.SemaphoreType.DMA((3, 2))]),
        out_shape=jax.ShapeDtypeStruct(xs.shape, U32),
        compiler_params=_cparams(1),
        name="moe_experts",
    )(tab, xs, w_gate, w_up, w_down)


def _combine_kernel(pos_ref, ys_ref, x1_ref, rw_ref, g2_ref, ng_ref, sc_ref, sh_ref, *rest,
                    tok0, tiles_per_seq, n_steps, final):
    if final:
        x2_ref, buf, sems = rest
        hn_ref = None
    else:
        x2_ref, hn_ref, buf, sems = rest
    i = pl.program_id(0)
    seq = i // tiles_per_seq
    rows = x1_ref.shape[0]

    def gather(step, slot):
        base = _pos_index(tok0 + step * rows)

        def issue(r, carry):
            for k in range(TOP_K):
                src = pos_ref[base + k * ROUTER_TILE + r]
                pltpu.make_async_copy(_tokens(ys_ref, src), _tokens(buf.at[slot, k], r),
                                      sems.at[slot]).start(priority=k)
            return carry

        lax.fori_loop(0, rows, issue, 0, unroll=8)

    @pl.when(i == 0)
    def _():
        gather(0, 0)

    @pl.when(i + 1 < n_steps)
    def _():
        gather(i + 1, (i + 1) % 2)

    slot = i % 2
    for k in range(TOP_K):
        pltpu.make_async_copy(_tokens(ys_ref, 0, rows), buf.at[slot, k], sems.at[slot]).wait()

    rw = rw_ref[...]
    lo_a, hi_a = _unpack_pairs_f32(_load_token_tiles(buf.at[slot, 0]))
    lo_b, hi_b = _unpack_pairs_f32(_load_token_tiles(buf.at[slot, 1]))
    w_a = rw[:, 0:1]
    w_b = rw[:, 1:2]
    moe = jnp.concatenate([w_a * lo_a + w_b * lo_b, w_a * hi_a + w_b * hi_b], axis=-1)
    x2 = x1_ref[...] + _mod_rows(g2_ref, seq) * moe
    if final:
        x2_ref[...] = _rms(x2, ng_ref[...])
    else:
        x2_ref[...] = x2
        hn_ref[...] = (_rms(x2, ng_ref[...]) * (1.0 + _mod_rows(sc_ref, seq))
                       + _mod_rows(sh_ref, seq)).astype(BF16)


def _combine(pos, ys, x1, rw, tok0, mod, layer, ng, tiles_per_seq, final):
    n_tok, d = x1.shape
    kern = functools.partial(_combine_kernel, tok0=tok0, tiles_per_seq=tiles_per_seq,
                             n_steps=n_tok // ROW_TILE, final=final)
    rw_off = tok0 // ROW_TILE
    row_spec = lambda w: pl.BlockSpec((ROW_TILE, w), lambda i, p: (i, 0))
    out_shape = [jax.ShapeDtypeStruct((n_tok, d), F32)]
    out_specs = [row_spec(d)]
    if not final:
        out_shape.append(jax.ShapeDtypeStruct((n_tok, d), BF16))
        out_specs.append(row_spec(d))
    nxt = min(layer + 1, mod.shape[0] - 1)
    return pl.pallas_call(
        kern,
        grid_spec=pltpu.PrefetchScalarGridSpec(
            num_scalar_prefetch=1,
            grid=(n_tok // ROW_TILE,),
            in_specs=[pl.BlockSpec(memory_space=pl.ANY), row_spec(d),
                      pl.BlockSpec((ROW_TILE, V7X_LANES), lambda i, p: (rw_off + i, 0)),
                      _mod_spec(mod, layer, 5), pl.BlockSpec((1, d), lambda i, p: (0, 0)),
                      _mod_spec(mod, nxt, 1), _mod_spec(mod, nxt, 0)],
            out_specs=out_specs,
            scratch_shapes=[pltpu.VMEM((2, TOP_K, ROW_TILE * V7X_SUBLANES, V7X_LANES), U32),
                            pltpu.SemaphoreType.DMA((2,))]),
        out_shape=out_shape,
        compiler_params=_cparams(1),
        name="moe_combine_final" if final else "moe_combine",
    )(pos, ys, x1, rw, mod, ng.reshape(1, d), mod, mod)


def _moe(out_p, out_s, mod_p, mod_s, layer, ng, tps_p, w_router, b_router, w_gate, w_up, w_down, final):
    x1_p, hp_p, hpt_p = out_p
    x1_s, hp_s, hpt_s = out_s
    n_p = hp_p.shape[0]
    n_tok = n_p + hp_s.shape[0]
    max_rows = TOP_K * n_tok + N_EXPERTS * (MOE_TILE - 1)
    n_rows_sorted = -(-max_rows // MOE_TILE) * MOE_TILE
    pos, rw, tab = _router(hp_p, hp_s, w_router, b_router)
    xs = _dispatch(pos, tab, hpt_p, hpt_s, n_rows_sorted)
    ys = _experts(tab, xs, w_gate, w_up, w_down, layer)
    out_p = _combine(pos, ys, x1_p, rw, 0, mod_p, layer, ng, tps_p, final)
    out_s = _combine(pos, ys, x1_s, rw, n_p, mod_s, layer, ng, 1, final)
    return out_p, out_s


def _gmlp_kernel(h_ref, w_ref, lg_ref, lb_ref, ws_ref, bs_ref, yc_ref, *rest, ell, blk, emit_v):
    if emit_v:
        gv_ref, wbf, wsbf = rest
    else:
        wbf, wsbf = rest
    i = pl.program_id(0)
    rows = h_ref.shape[0]
    c = GM_WIDTH

    @pl.when(i == 0)
    def _():
        wbf[...] = w_ref[...].astype(BF16)
        r = lax.broadcasted_iota(I32, (ell, ell), 0)
        s = lax.broadcasted_iota(I32, (ell, ell), 1)
        keep = (r >= s) & ((r // blk) == (s // blk))
        rsel = (lax.broadcasted_iota(I32, (ell, CHUNK), 0) % blk
                == lax.broadcasted_iota(I32, (ell, CHUNK), 1)).astype(BF16)
        csel = (lax.broadcasted_iota(I32, (CHUNK, ell), 1) % blk
                == lax.broadcasted_iota(I32, (CHUNK, ell), 0)).astype(BF16)
        for g in range(GM_GROUPS):
            wchunk = ws_ref[g].astype(BF16)
            if blk == ell:
                full = wchunk
            else:
                rowsp = jnp.dot(rsel, wchunk, preferred_element_type=F32).astype(BF16)
                full = jnp.dot(rowsp, csel, preferred_element_type=F32).astype(BF16)
            wsbf[g] = jnp.where(keep, full, jnp.zeros_like(full))

    uv = jnp.dot(h_ref[...], wbf[...], preferred_element_type=F32)
    u = uv[:, :c]
    v = uv[:, c:]
    vc = v - jnp.mean(v, axis=-1, keepdims=True)
    vn = vc * lax.rsqrt(jnp.mean(vc * vc, axis=-1, keepdims=True) + EPS) * lg_ref[...] + lb_ref[...]
    if emit_v:
        gv_ref[...] = vn
    vb = vn.astype(BF16)
    bs = bs_ref[...]
    for ch in range(rows // ell):
        rs = slice(ch * ell, (ch + 1) * ell)
        outs = []
        for g in range(GM_GROUPS):
            cs = slice(g * GM_GROUP, (g + 1) * GM_GROUP)
            mixed = jnp.dot(wsbf[g], vb[rs, cs], preferred_element_type=F32)
            mixed = (mixed.reshape(ell // blk, blk, GM_GROUP) + bs[:blk, g:g + 1][None]).reshape(ell, GM_GROUP)
            outs.append(u[rs, cs] * mixed)
        yc_ref[rs, :] = jnp.concatenate(outs, axis=-1).astype(BF16)


def _gmlp_mixer(h2, w_in, ln_g, ln_b, ws, bs_t, ell, blk, emit_v):
    rows, d = h2.shape
    c = GM_WIDTH
    kern = functools.partial(_gmlp_kernel, ell=ell, blk=blk, emit_v=emit_v)
    rt = min(rows, MATMUL_TILE)
    out_specs = [pl.BlockSpec((rt, c), lambda i: (i, 0))]
    out_shape = [jax.ShapeDtypeStruct((rows, c), BF16)]
    if emit_v:
        out_specs.append(pl.BlockSpec((rt, c), lambda i: (i, 0)))
        out_shape.append(jax.ShapeDtypeStruct((rows, c), F32))
    return pl.pallas_call(
        kern,
        grid=(rows // rt,),
        in_specs=[pl.BlockSpec((rt, d), lambda i: (i, 0)),
                  pl.BlockSpec((d, 2 * c), lambda i: (0, 0), pipeline_mode=pl.Buffered(1)),
                  pl.BlockSpec((1, c), lambda i: (0, 0)),
                  pl.BlockSpec((1, c), lambda i: (0, 0)),
                  pl.BlockSpec((GM_GROUPS, CHUNK, CHUNK), lambda i: (0, 0, 0)),
                  pl.BlockSpec((CHUNK, GM_GROUPS), lambda i: (0, 0))],
        out_specs=out_specs,
        out_shape=out_shape,
        scratch_shapes=[pltpu.VMEM((d, 2 * c), BF16), pltpu.VMEM((GM_GROUPS, ell, ell), BF16)],
        compiler_params=_cparams(1),
        name="gmlp_mixer",
    )(h2, w_in, ln_g.reshape(1, c), ln_b.reshape(1, c), ws, bs_t)


PAIR_W = 2 * HEAD_DIM
PAIRS_PER_KV = N_HEADS // N_KV // 2
NT_DIMS = (((1,), (1,)), ((), ()))


def _swa_project(i, h_ref, wq_ref, wkv_ref, wbf):
    nq = N_HEADS * HEAD_DIM

    @pl.when(i == 0)
    def _():
        wbf[:, :nq] = wq_ref[...].astype(BF16)
        wbf[:, nq:] = wkv_ref[...].astype(BF16)

    return jnp.dot(h_ref[...], wbf[...], preferred_element_type=F32)


def _pair_block_diag(a, a_swapped, hk, axis):
    dim_axis = 1 - axis
    low = lax.broadcasted_iota(I32, a.shape, dim_axis) < HEAD_DIM
    lo, hi = (a, a_swapped) if hk == 0 else (a_swapped, a)
    return jnp.concatenate([jnp.where(low, lo, 0.0), jnp.where(low, 0.0, hi)], axis=axis).astype(BF16)


def _stack_pairs(qkv, rs, hk):
    p0 = hk * PAIRS_PER_KV
    return jnp.concatenate([qkv[rs, (p0 + pp) * PAIR_W:(p0 + pp + 1) * PAIR_W]
                            for pp in range(PAIRS_PER_KV)], axis=0).astype(BF16)


def _swa_cached_kernel(h_ref, wq_ref, wkv_ref, kp_ref, vp_ref, bias_ref, sink_ref, yd_ref, k_ref, v_ref,
                       wbf, *, tq):
    i = pl.program_id(0)
    rows = h_ref.shape[0]
    nq = N_HEADS * HEAD_DIM
    nkv = N_KV * HEAD_DIM
    n_blocks = rows // tq
    qkv = _swa_project(i, h_ref, wq_ref, wkv_ref, wbf)
    k_new = qkv[:, nq:nq + nkv]
    v_new = qkv[:, nq + nkv:]
    k_ref[...] = k_new
    v_ref[...] = v_new
    pad = jnp.zeros((WINDOW - tq, nkv), F32)

    scores, vbds = [], []
    for blk in range(n_blocks):
        rs = slice(blk * tq, (blk + 1) * tq)
        kcat = jnp.concatenate([kp_ref[blk], k_new[rs], pad], axis=0)
        vcat = jnp.concatenate([vp_ref[blk], v_new[rs], pad], axis=0)
        kswap = pltpu.roll(kcat, HEAD_DIM, 1)
        vswap = pltpu.roll(vcat, HEAD_DIM, 1)
        per_head = []
        for hk in range(N_KV):
            kbd = _pair_block_diag(kcat, kswap, hk, 0)
            vbds.append(_pair_block_diag(vcat, vswap, hk, 0))
            s4 = lax.dot_general(_stack_pairs(qkv, rs, hk), kbd, NT_DIMS,
                                 preferred_element_type=F32) * (HEAD_DIM ** -0.5)
            for pp in range(PAIRS_PER_KV):
                for sub in range(2):
                    per_head.append(s4[pp * tq:(pp + 1) * tq, sub * 2 * WINDOW:(sub + 1) * 2 * WINDOW])
        scores.append(jnp.concatenate(per_head, axis=0))

    s_all = jnp.stack(scores, axis=0) + bias_ref[...][None]
    sink = sink_ref[...][None]
    m = jnp.maximum(jnp.max(s_all, axis=-1, keepdims=True), sink)
    pr = jnp.exp(s_all - m)
    pr = pr / (jnp.sum(pr, axis=-1, keepdims=True) + jnp.exp(sink - m))

    for blk in range(n_blocks):
        outs = []
        for hk in range(N_KV):
            p4 = []
            for pp in range(PAIRS_PER_KV):
                h0 = 2 * (hk * PAIRS_PER_KV + pp)
                p4.append(jnp.concatenate([pr[blk, h0 * tq:(h0 + 1) * tq, :],
                                           pr[blk, (h0 + 1) * tq:(h0 + 2) * tq, :]], axis=-1))
            o4 = jnp.dot(jnp.concatenate(p4, axis=0).astype(BF16), vbds[blk * N_KV + hk],
                         preferred_element_type=F32)
            outs.extend(o4[pp * tq:(pp + 1) * tq, :] for pp in range(PAIRS_PER_KV))
        yd_ref[blk * tq:(blk + 1) * tq, :] = jnp.concatenate(outs, axis=-1).astype(BF16)


def _swa_stream_kernel(h_ref, wq_ref, wkv_ref, bias_ref, sink_ref, yd_ref, k_ref, v_ref,
                       wbf, kprev, vprev_t, *, blocks_per_seq):
    i = pl.program_id(0)
    rows = h_ref.shape[0]
    nq = N_HEADS * HEAD_DIM
    nkv = N_KV * HEAD_DIM
    tq = WINDOW
    n_blocks = rows // tq

    @pl.when(i == 0)
    def _():
        kprev[...] = jnp.zeros_like(kprev)
        vprev_t[...] = jnp.zeros_like(vprev_t)

    qkv = _swa_project(i, h_ref, wq_ref, wkv_ref, wbf)
    k_new = qkv[:, nq:nq + nkv]
    v_new = qkv[:, nq + nkv:]
    k_ref[...] = k_new
    v_ref[...] = v_new
    v_new_t = v_new.T
    key = lax.broadcasted_iota(I32, (2, 2 * WINDOW, PAIRS_PER_KV * tq), 1)

    for blk in range(n_blocks):
        rs = slice(blk * tq, (blk + 1) * tq)
        first = (i * n_blocks + blk) % blocks_per_seq == 0
        k_cur = k_new[rs]
        v_cur_t = v_new_t[:, rs]
        kcat = jnp.concatenate([kprev[...], k_cur], axis=0)
        vcat_t = jnp.concatenate([vprev_t[...], v_cur_t], axis=1)
        kprev[...] = k_cur
        vprev_t[...] = v_cur_t
        kswap = pltpu.roll(kcat, HEAD_DIM, 1)
        vswap_t = pltpu.roll(vcat_t, HEAD_DIM, 0)
        outs = []
        for hk in range(N_KV):
            kbd = _pair_block_diag(kcat, kswap, hk, 0)
            vbd_t = _pair_block_diag(vcat_t, vswap_t, hk, 1)
            st = lax.dot_general(kbd, _stack_pairs(qkv, rs, hk), NT_DIMS,
                                 preferred_element_type=F32) * (HEAD_DIM ** -0.5)
            s3 = st.reshape(2, 2 * WINDOW, PAIRS_PER_KV * tq) + bias_ref[hk]
            s3 = jnp.where(first & (key < WINDOW), NEG_INF, s3)
            sink = sink_ref[hk]
            m = jnp.maximum(jnp.max(s3, axis=1, keepdims=True), sink)
            pr = jnp.exp(s3 - m)
            pr = pr / (jnp.sum(pr, axis=1, keepdims=True) + jnp.exp(sink - m))
            o_t = jnp.dot(vbd_t, pr.reshape(4 * WINDOW, PAIRS_PER_KV * tq).astype(BF16),
                          preferred_element_type=F32)
            o4 = o_t.T
            outs.extend(o4[pp * tq:(pp + 1) * tq, :] for pp in range(PAIRS_PER_KV))
        yd_ref[rs, :] = jnp.concatenate(outs, axis=-1).astype(BF16)


def _swa_weight_specs(w_in, d):
    nq = N_HEADS * HEAD_DIM
    nkv = N_KV * HEAD_DIM
    nw = nq + 2 * nkv
    q_blk = (w_in.shape[1] - nw) // nq
    kv_blk = (w_in.shape[1] - 2 * nkv) // (2 * nkv)
    assert q_blk * nq + nw == w_in.shape[1] and kv_blk * 2 * nkv + 2 * nkv == w_in.shape[1]
    return [pl.BlockSpec((d, nq), lambda i: (0, q_blk)), pl.BlockSpec((d, 2 * nkv), lambda i: (0, kv_blk))]


def _swa_outputs(rows, rt):
    nq = N_HEADS * HEAD_DIM
    nkv = N_KV * HEAD_DIM
    specs = [pl.BlockSpec((rt, nq), lambda i: (i, 0)),
             pl.BlockSpec((rt, nkv), lambda i: (i, 0)),
             pl.BlockSpec((rt, nkv), lambda i: (i, 0))]
    shapes = [jax.ShapeDtypeStruct((rows, nq), BF16),
              jax.ShapeDtypeStruct((rows, nkv), F32),
              jax.ShapeDtypeStruct((rows, nkv), F32)]
    return specs, shapes


def _swa_cached_mixer(h2, w_in, k_cache, v_cache, bias, sinks, tq):
    rows, d = h2.shape
    nkv = N_KV * HEAD_DIM
    nw = N_HEADS * HEAD_DIM + 2 * nkv
    n_blocks = ROW_TILE // tq
    cache_spec = pl.BlockSpec((n_blocks, WINDOW, nkv), lambda i: (i, 0, 0))
    out_specs, out_shape = _swa_outputs(rows, ROW_TILE)
    return pl.pallas_call(
        functools.partial(_swa_cached_kernel, tq=tq),
        grid=(rows // ROW_TILE,),
        in_specs=[pl.BlockSpec((ROW_TILE, d), lambda i: (i, 0))] + _swa_weight_specs(w_in, d)
        + [cache_spec, cache_spec,
           pl.BlockSpec((N_HEADS * tq, 2 * WINDOW), lambda i: (0, 0)),
           pl.BlockSpec((N_HEADS * tq, 1), lambda i: (0, 0))],
        out_specs=out_specs,
        out_shape=out_shape,
        scratch_shapes=[pltpu.VMEM((d, nw), BF16)],
        compiler_params=_cparams(1),
        name="swa_cached",
    )(h2, w_in, w_in, k_cache, v_cache, bias, sinks)


def _swa_stream_mixer(h2, w_in, bias_t, sinks_t, blocks_per_seq):
    rows, d = h2.shape
    nkv = N_KV * HEAD_DIM
    nw = N_HEADS * HEAD_DIM + 2 * nkv
    lanes = PAIRS_PER_KV * WINDOW
    rt = min(rows, MATMUL_TILE)
    out_specs, out_shape = _swa_outputs(rows, rt)
    return pl.pallas_call(
        functools.partial(_swa_stream_kernel, blocks_per_seq=blocks_per_seq),
        grid=(rows // rt,),
        in_specs=[pl.BlockSpec((rt, d), lambda i: (i, 0))] + _swa_weight_specs(w_in, d)
        + [pl.BlockSpec((N_KV, 2, 2 * WINDOW, lanes), lambda i: (0, 0, 0, 0)),
           pl.BlockSpec((N_KV, 2, 1, lanes), lambda i: (0, 0, 0, 0))],
        out_specs=out_specs,
        out_shape=out_shape,
        scratch_shapes=[pltpu.VMEM((d, nw), BF16), pltpu.VMEM((WINDOW, nkv), F32),
                        pltpu.VMEM((nkv, WINDOW), F32)],
        compiler_params=_cparams(1),
        name="swa_stream",
    )(h2, w_in, w_in, bias_t, sinks_t)


def _t5_bucket(dist):
    max_exact = N_BUCKETS // 2
    dd = np.maximum(dist, 1)
    large = max_exact + (np.log(dd / max_exact) / np.log(WINDOW / max_exact)
                         * (N_BUCKETS - max_exact)).astype(np.int64)
    large = np.minimum(large, N_BUCKETS - 1)
    return np.where(dist < max_exact, dist, large).astype(np.int32)


def _attention_bias(rel_bias):
    by_dist = jnp.take(rel_bias.astype(F32), _t5_bucket(np.arange(WINDOW)), axis=0).T
    neg = jnp.full((N_HEADS, WINDOW), NEG_INF, F32)
    line = jnp.concatenate([neg, by_dist[:, ::-1], neg[:, :WINDOW - 1]], axis=1)
    return jnp.stack([line[:, WINDOW - 1 - q:3 * WINDOW - 1 - q] for q in range(WINDOW)], axis=1)


def kernel(x_prompt, x_sample, state_pool, state_conv, cache_swa_k, cache_swa_v, c_prompt, c_sample, w_ada, b_ada, norm_g, final_norm_g, w_in_even, w_out_even, w_pool, pool_scale, conv_w, w_in_odd, w_out_odd, gm_norm_g, gm_norm_b, gm_w_s, gm_b_s, attn_sinks, rel_bias, w_router, b_router, w_gate, w_up, w_down):
    d = D_MODEL
    bp, tp, _ = x_prompt.shape
    bs, ts, _ = x_sample.shape
    rows_s = bs * ts
    assert rows_s == ROW_TILE and tp % ROUTER_TILE == 0 and PAST_LEN % CHUNK == 0
    assert bp <= V7X_SUBLANES and CHUNK % ts == 0

    n_c = bp + bs
    c_pad = (-n_c) % V7X_SUBLANES
    c_all = jnp.concatenate([c_prompt, c_sample, jnp.zeros((c_pad, d), F32)], axis=0)
    mod_p = _adaln(c_all, w_ada, b_ada)
    mod_s = jnp.repeat(mod_p[:, bp:bp + bs], ts, axis=1)

    tps_p = tp // ROW_TILE
    xp = x_prompt.reshape(bp * tp, d)
    xs_ = x_sample.reshape(rows_s, d)
    w_in0, w_in1 = w_in_even[0], w_in_odd[0]

    zero_pool = jnp.zeros((bp, POOL_STATE, POOL_WIDTH), F32)
    zero_conv = jnp.zeros((bp, CONV_K - 1, CONV_WIDTH), F32)
    hp0, ya_p, pool_p = _pool_mixer(xp, norm_g[0, 0], mod_p, 0, w_in0, w_pool[0], pool_scale[0],
                                    zero_pool, 1, MATMUL_TILE, 0)
    hs0, ya_s, pool_s = _pool_mixer(xs_, norm_g[0, 0], mod_s, 0, w_in0, w_pool[0], pool_scale[0],
                                    state_pool[0], bs, ts, PAST_LEN)
    yb_p, conv_p = _conv_mixer(hp0, w_in0, conv_w[0], zero_conv, 1, MATMUL_TILE)
    yb_s, conv_s = _conv_mixer(hs0, w_in0, conv_w[0], state_conv[0], bs, ts)
    out_p = _outproj(ya_p, yb_p, xp, mod_p, 0, norm_g[0, 1], w_out_even[0], tp)
    out_s = _outproj(ya_s, yb_s, xs_, mod_s, 0, norm_g[0, 1], w_out_even[0], rows_s)
    (x2p, h1p), (x2s, h1s) = _moe(out_p, out_s, mod_p, mod_s, 0, norm_g[1, 0], tps_p,
                                  w_router, b_router, w_gate, w_up, w_down, final=False)

    bs_t = gm_b_s[0].T
    (yc_p,) = _gmlp_mixer(h1p, w_in1, gm_norm_g[0], gm_norm_b[0], gm_w_s[0], bs_t, CHUNK, CHUNK, False)
    yc_s, gv_s = _gmlp_mixer(h1s, w_in1, gm_norm_g[0], gm_norm_b[0], gm_w_s[0], bs_t, rows_s, ts, True)
    bias = _attention_bias(rel_bias)
    nkv = N_KV * HEAD_DIM
    bias_t = jnp.transpose(bias.reshape(N_KV, PAIRS_PER_KV, 2, WINDOW, 2 * WINDOW), (0, 2, 4, 1, 3))
    bias_t = bias_t.reshape(N_KV, 2, 2 * WINDOW, PAIRS_PER_KV * WINDOW)
    sinks_t = jnp.transpose(attn_sinks[0].reshape(N_KV, PAIRS_PER_KV, 2), (0, 2, 1))
    sinks_t = jnp.repeat(sinks_t, WINDOW, axis=-1).reshape(N_KV, 2, 1, PAIRS_PER_KV * WINDOW)
    yd_p, k_p, v_p = _swa_stream_mixer(h1p, w_in1, bias_t, sinks_t, tp // WINDOW)
    yd_s, k_s, v_s = _swa_cached_mixer(h1s, w_in1, cache_swa_k[0].reshape(bs, WINDOW, nkv),
                                       cache_swa_v[0].reshape(bs, WINDOW, nkv),
                                       bias[:, :ts, :].reshape(N_HEADS * ts, 2 * WINDOW),
                                       jnp.repeat(attn_sinks[0], ts).reshape(-1, 1), ts)
    out_p = _outproj(yc_p, yd_p, x2p, mod_p, 1, norm_g[1, 1], w_out_odd[0], tp)
    out_s = _outproj(yc_s, yd_s, x2s, mod_s, 1, norm_g[1, 1], w_out_odd[0], rows_s)
    (yp,), (ys_out,) = _moe(out_p, out_s, mod_p, mod_s, 1, final_norm_g, tps_p,
                            w_router, b_router, w_gate, w_up, w_down, final=True)

    k_p4 = k_p.reshape(bp, tp, N_KV, HEAD_DIM)[:, -WINDOW:]
    v_p4 = v_p.reshape(bp, tp, N_KV, HEAD_DIM)[:, -WINDOW:]
    k_s4 = jnp.concatenate([cache_swa_k[0], k_s.reshape(bs, ts, N_KV, HEAD_DIM)], axis=1)[:, -WINDOW:]
    v_s4 = jnp.concatenate([cache_swa_v[0], v_s.reshape(bs, ts, N_KV, HEAD_DIM)], axis=1)[:, -WINDOW:]
    return (yp.reshape(bp, tp, d), ys_out.reshape(bs, ts, d),
            pool_p[None], pool_s[None], conv_p[None], conv_s[None],
            k_p4[None], k_s4[None], v_p4[None], v_s4[None],
            gv_s.reshape(bs, ts, GM_WIDTH)[None])
```

```python
import functools

import numpy as np
import jax
import jax.numpy as jnp
from jax import lax
from jax.experimental import pallas as pl
from jax.experimental.pallas import tpu as pltpu

F32 = jnp.float32
BF16 = jnp.bfloat16
I32 = jnp.int32
U32 = jnp.uint32

D_MODEL = 2048
POOL_WINDOWS = (2, 4, 8, 16)
POOL_WIDTH = 1024
POOL_GROUP = 256
POOL_STATE = 15
CONV_WIDTH = 1024
CONV_K = 3
GM_WIDTH = 1024
GM_GROUPS = 8
GM_GROUP = 128
CHUNK = 128
HEAD_DIM = 64
N_HEADS = 16
N_KV = 2
WINDOW = 128
N_BUCKETS = 32
N_EXPERTS = 16
N_EXPERT_GROUPS = 4
EXP_PER_GROUP = 4
TOP_K = 2
EPS = 1e-6
NEG_INF = -1e30
PAST_LEN = 16384

V7X_SUBLANES = 8
V7X_LANES = 128
VMEM_LIMIT = 56 * 1024 * 1024

ROW_TILE = 256
MATMUL_TILE = 512
SUB_TILE = 256
ROUTER_TILE = 1024
POOL_HALO = 16
CONV_HALO = 8
MOE_TILE = 256
TAB_EXPERT, TAB_VALID, TAB_LAST_TILE, TAB_NUSED, TAB_NEXT = 0, 1, 2, 3, 4


def _cparams(n_axes):
    return pltpu.CompilerParams(dimension_semantics=("arbitrary",) * n_axes,
                                vmem_limit_bytes=VMEM_LIMIT)


def _rms(x, g):
    return x * lax.rsqrt(jnp.mean(x * x, axis=-1, keepdims=True) + EPS) * g


def _mod_spec(mod, layer, part):
    nrow = ROW_TILE if mod.shape[1] == ROW_TILE else V7X_SUBLANES
    return pl.BlockSpec((1, nrow, D_MODEL), lambda *_: (layer, 0, part))


def _mod_rows(m_ref, seq, rs=None):
    if m_ref.shape[1] == V7X_SUBLANES:
        return m_ref[0, pl.ds(seq, 1), :]
    return m_ref[0] if rs is None else m_ref[0, rs, :]


def _adaln_kernel(c_ref, w_ref, b_ref, o_ref):
    c = c_ref[...]
    a = (c * jax.nn.sigmoid(c)).astype(BF16)
    o_ref[0] = jnp.dot(a, w_ref[0].astype(BF16), preferred_element_type=F32) + b_ref[0]


def _adaln(c_all, w_ada, b_ada):
    depth, d, n6 = w_ada.shape
    m = c_all.shape[0]
    tn = 2048
    return pl.pallas_call(
        _adaln_kernel,
        grid=(depth, n6 // tn),
        in_specs=[pl.BlockSpec((m, d), lambda l, j: (0, 0)),
                  pl.BlockSpec((1, d, tn), lambda l, j: (l, 0, j)),
                  pl.BlockSpec((1, 1, tn), lambda l, j: (l, 0, j))],
        out_specs=pl.BlockSpec((1, m, tn), lambda l, j: (l, 0, j)),
        out_shape=jax.ShapeDtypeStruct((depth, m, n6), F32),
        compiler_params=_cparams(2),
        name="adaln",
    )(c_all, w_ada, b_ada.reshape(depth, 1, n6))


def _pool_kernel(x_ref, g_ref, sc_ref, sh_ref, w_ref, wp_ref, ps_ref, st_ref, h_ref, ya_ref, ns_ref,
                 wbf, wpbf, carry, *, nb, tm, tiles_per_seq, start):
    i = pl.program_id(0)
    t = i % tiles_per_seq
    seq = i // tiles_per_seq
    c = POOL_WIDTH
    halo = POOL_HALO

    @pl.when(i == 0)
    def _():
        wbf[...] = w_ref[...].astype(BF16)
        wpbf[...] = wp_ref[...].astype(BF16)

    @pl.when(t == 0)
    def _():
        carry[...] = st_ref[...]

    h = (_rms(x_ref[...], g_ref[...]) * (1.0 + _mod_rows(sc_ref, seq)) + _mod_rows(sh_ref, seq)).astype(BF16)
    h_ref[...] = h
    p = jnp.dot(h, wbf[...], preferred_element_type=F32)
    p3 = p.reshape(nb, tm, c)
    ext3 = jnp.concatenate([carry[...], p3], axis=1)
    tail = ext3[:, tm:tm + halo, :]
    ns_ref[...] = tail
    carry[...] = tail
    ext = ext3.reshape(nb * (halo + tm), c)
    pos = start + t * tm + lax.broadcasted_iota(I32, (1, tm, 1), 1)
    outs = []
    for gi, w in enumerate(POOL_WINDOWS):
        sl = slice(gi * POOL_GROUP, (gi + 1) * POOL_GROUP)
        acc = ext[:, sl]
        shift = 1
        while shift < w:
            acc = acc + pltpu.roll(acc, shift, 0)
            shift *= 2
        win = acc.reshape(nb, halo + tm, POOL_GROUP)[:, halo:, :]
        cnt = jnp.minimum(pos + 1, w).astype(F32)
        dgrp = win / cnt - p3[:, :, sl]
        outs.append(jnp.dot(dgrp.reshape(nb * tm, POOL_GROUP).astype(BF16), wpbf[gi],
                            preferred_element_type=F32))
    y = jnp.concatenate(outs, axis=-1) * ps_ref[...]
    ya_ref[...] = y.astype(BF16)


def _pool_mixer(x2, g, mod, layer, w_in, w_pool, pool_scale, state, nb, tm, start):
    rows, d = x2.shape
    nseq = state.shape[0]
    tiles_per_seq = (rows // nseq) // tm
    seq_blocks = nseq // nb
    c = POOL_WIDTH
    st = jnp.pad(state, ((0, 0), (POOL_HALO - POOL_STATE, 0), (0, 0)))
    kern = functools.partial(_pool_kernel, nb=nb, tm=tm, tiles_per_seq=tiles_per_seq, start=start)
    h2, ya, ns = pl.pallas_call(
        kern,
        grid=(seq_blocks * tiles_per_seq,),
        in_specs=[pl.BlockSpec((nb * tm, d), lambda i: (i, 0)),
                  pl.BlockSpec((1, d), lambda i: (0, 0)),
                  _mod_spec(mod, layer, 1), _mod_spec(mod, layer, 0),
                  pl.BlockSpec((d, c), lambda i: (0, 0)),
                  pl.BlockSpec((len(POOL_WINDOWS), POOL_GROUP, POOL_GROUP), lambda i: (0, 0, 0)),
                  pl.BlockSpec((1, c), lambda i: (0, 0)),
                  pl.BlockSpec((nb, POOL_HALO, c), lambda i: (i // tiles_per_seq, 0, 0))],
        out_specs=[pl.BlockSpec((nb * tm, d), lambda i: (i, 0)),
                   pl.BlockSpec((nb * tm, c), lambda i: (i, 0)),
                   pl.BlockSpec((nb, POOL_HALO, c), lambda i: (i // tiles_per_seq, 0, 0))],
        out_shape=[jax.ShapeDtypeStruct((rows, d), BF16),
                   jax.ShapeDtypeStruct((rows, c), BF16),
                   jax.ShapeDtypeStruct((nseq, POOL_HALO, c), F32)],
        scratch_shapes=[pltpu.VMEM((d, c), BF16),
                        pltpu.VMEM((len(POOL_WINDOWS), POOL_GROUP, POOL_GROUP), BF16),
                        pltpu.VMEM((nb, POOL_HALO, c), F32)],
        compiler_params=_cparams(1),
        name="pool_mixer",
    )(x2, g.reshape(1, d), mod, mod, w_in, w_pool, pool_scale.reshape(1, c), st)
    return h2, ya, ns[:, POOL_HALO - POOL_STATE:, :]


def _conv_kernel(h_ref, wx_ref, wb_ref, wc_ref, cw_ref, st_ref, yb_ref, ns_ref,
                 wxbf, wbbf, wcbf, carry, *, nb, tm, tiles_per_seq):
    i = pl.program_id(1)
    t = i % tiles_per_seq
    tc = wxbf.shape[1]
    halo = CONV_HALO

    @pl.when(i == 0)
    def _():
        wxbf[...] = wx_ref[...].astype(BF16)
        wbbf[...] = wb_ref[...].astype(BF16)
        wcbf[...] = wc_ref[...].astype(BF16)

    @pl.when(t == 0)
    def _():
        carry[...] = st_ref[...]

    h = h_ref[...]
    xin = jnp.dot(h, wxbf[...], preferred_element_type=F32)
    gb = jnp.dot(h, wbbf[...], preferred_element_type=F32)
    gc = jnp.dot(h, wcbf[...], preferred_element_type=F32)
    z3 = (gc * xin).reshape(nb, tm, tc)
    ext3 = jnp.concatenate([carry[...], z3], axis=1)
    tail = ext3[:, tm:tm + halo, :]
    ns_ref[...] = tail
    carry[...] = tail
    ext = ext3.reshape(nb * (halo + tm), tc)
    cw = cw_ref[...]
    conv = cw[0:1, :] * pltpu.roll(ext, 2, 0) + cw[1:2, :] * pltpu.roll(ext, 1, 0) + cw[2:3, :] * ext
    conv = conv.reshape(nb, halo + tm, tc)[:, halo:, :].reshape(nb * tm, tc)
    yb_ref[...] = (gb * conv).astype(BF16)


def _conv_mixer(h2, w_in, conv_w, state, nb, tm):
    rows, d = h2.shape
    nseq = state.shape[0]
    tiles_per_seq = (rows // nseq) // tm
    seq_blocks = nseq // nb
    c = CONV_WIDTH
    tc = 512
    cb = c // tc
    base = POOL_WIDTH // tc
    st = jnp.pad(state, ((0, 0), (CONV_HALO - (CONV_K - 1), 0), (0, 0)))
    kern = functools.partial(_conv_kernel, nb=nb, tm=tm, tiles_per_seq=tiles_per_seq)
    yb, ns = pl.pallas_call(
        kern,
        grid=(cb, seq_blocks * tiles_per_seq),
        in_specs=[pl.BlockSpec((nb * tm, d), lambda j, i: (i, 0)),
                  pl.BlockSpec((d, tc), lambda j, i: (0, base + j)),
                  pl.BlockSpec((d, tc), lambda j, i: (0, base + cb + j)),
                  pl.BlockSpec((d, tc), lambda j, i: (0, base + 2 * cb + j)),
                  pl.BlockSpec((CONV_K, tc), lambda j, i: (0, j)),
                  pl.BlockSpec((nb, CONV_HALO, tc), lambda j, i: (i // tiles_per_seq, 0, j))],
        out_specs=[pl.BlockSpec((nb * tm, tc), lambda j, i: (i, j)),
                   pl.BlockSpec((nb, CONV_HALO, tc), lambda j, i: (i // tiles_per_seq, 0, j))],
        out_shape=[jax.ShapeDtypeStruct((rows, c), BF16),
                   jax.ShapeDtypeStruct((nseq, CONV_HALO, c), F32)],
        scratch_shapes=[pltpu.VMEM((d, tc), BF16)] * 3 + [pltpu.VMEM((nb, CONV_HALO, tc), F32)],
        compiler_params=_cparams(2),
        name="conv_mixer",
    )(h2, w_in, w_in, w_in, conv_w, st)
    return yb, ns[:, CONV_HALO - (CONV_K - 1):, :]


def _pack_bf16_pairs(v):
    c = v.shape[1] // 2
    return pltpu.bitcast(pltpu.pack_elementwise([v[:, :c], v[:, c:]], packed_dtype=BF16), U32)


def _store_token_tiles(ref, v):
    rows = v.shape[0]
    for j in range(V7X_SUBLANES):
        ref[pl.ds(j, rows, stride=V7X_SUBLANES), :] = v[:, j * V7X_LANES:(j + 1) * V7X_LANES]


def _load_token_tiles(ref):
    rows = ref.shape[0] // V7X_SUBLANES
    return jnp.concatenate([ref[pl.ds(j, rows, stride=V7X_SUBLANES), :] for j in range(V7X_SUBLANES)],
                           axis=-1)


def _unpack_pairs_f32(w):
    return tuple(pltpu.unpack_elementwise(w, index=k, packed_dtype=BF16, unpacked_dtype=F32) for k in range(2))


def _unpack_bf16_pairs(w):
    lo, hi = _unpack_pairs_f32(w)
    return lo.astype(BF16), hi.astype(BF16)


def _outproj_kernel(ya_ref, yb_ref, x_ref, g1_ref, sc_ref, sh_ref, ng_ref, wo_ref,
                    x1_ref, hp_ref, hpt_ref, wobf, *, tiles_per_seq):
    i = pl.program_id(0)
    seq = i // tiles_per_seq
    half = ya_ref.shape[1]

    @pl.when(i == 0)
    def _():
        wobf[...] = wo_ref[...].astype(BF16)

    rows = x_ref.shape[0]
    sr = min(rows, SUB_TILE)
    subs = [pl.ds(s * sr, sr) for s in range(rows // sr)]
    ys = [jnp.dot(ya_ref[rs, :], wobf[:half, :], preferred_element_type=F32)
          + jnp.dot(yb_ref[rs, :], wobf[half:, :], preferred_element_type=F32) for rs in subs]
    for s, (rs, y) in enumerate(zip(subs, ys)):
        x1 = x_ref[rs, :] + _mod_rows(g1_ref, seq, rs) * y
        x1_ref[rs, :] = x1
        h2 = _rms(x1, ng_ref[...]) * (1.0 + _mod_rows(sc_ref, seq, rs)) + _mod_rows(sh_ref, seq, rs)
        packed = _pack_bf16_pairs(h2)
        hp_ref[rs, :] = packed
        _store_token_tiles(hpt_ref.at[pl.ds(s * sr * V7X_SUBLANES, sr * V7X_SUBLANES), :], packed)


def _outproj(ya, yb, x2, mod, layer, ng, w_out, seq_rows):
    rows_all, d = x2.shape
    half = ya.shape[1]
    rt = ROW_TILE
    row_spec = lambda w: pl.BlockSpec((rt, w), lambda i: (i, 0))
    return pl.pallas_call(
        functools.partial(_outproj_kernel, tiles_per_seq=seq_rows // rt),
        grid=(rows_all // rt,),
        in_specs=[row_spec(half), row_spec(half), row_spec(d),
                  _mod_spec(mod, layer, 2), _mod_spec(mod, layer, 4), _mod_spec(mod, layer, 3),
                  pl.BlockSpec((1, d), lambda i: (0, 0)),
                  pl.BlockSpec((d, d), lambda i: (0, 0), pipeline_mode=pl.Buffered(1))],
        out_specs=[row_spec(d), row_spec(d // 2),
                   pl.BlockSpec((rt * V7X_SUBLANES, V7X_LANES), lambda i: (i, 0))],
        out_shape=[jax.ShapeDtypeStruct((rows_all, d), F32),
                   jax.ShapeDtypeStruct((rows_all, d // 2), U32),
                   jax.ShapeDtypeStruct((rows_all * V7X_SUBLANES, V7X_LANES), U32)],
        scratch_shapes=[pltpu.VMEM((d, d), BF16)],
        compiler_params=_cparams(1),
        name="outproj",
    )(ya, yb, x2, mod, mod, mod, ng.reshape(1, d), w_out)


def _router_kernel(hpp_ref, hps_ref, wr_ref, br_ref, pos_ref, rw_ref, tab_ref,
                   cnt_acc, totals, starts, padded, before_ref, *, nt_p, rows_s):
    ph = pl.program_id(0)
    t = pl.program_id(1)
    last = nt_p
    r = hpp_ref.shape[0]
    half = hpp_ref.shape[1]
    ne = N_EXPERTS
    sub = lax.broadcasted_iota(I32, (ne, V7X_LANES), 0)

    @pl.when(t == 0)
    def _():
        cnt_acc[...] = jnp.zeros_like(cnt_acc)

    @pl.when((ph == 0) & (t == 0))
    def _():
        starts[...] = jnp.zeros_like(starts)
        padded[...] = jnp.zeros_like(padded)

    @pl.when((ph == 1) & (t == 0))
    def _():
        pad = jnp.floor((totals[...] + (MOE_TILE - 1.0)) * (1.0 / MOE_TILE)) * MOE_TILE
        run = pad
        k = 1
        while k < ne:
            run = run + jnp.where(sub >= k, pltpu.roll(run, k, 0), 0.0)
            k *= 2
        padded[...] = pad
        starts[...] = run - pad

    is_s = t == last
    w_s = jnp.concatenate([hps_ref[...], jnp.zeros((r - rows_s, half), U32)], axis=0)
    w = jnp.where(is_s, w_s, hpp_ref[...])
    lo, hi = _unpack_bf16_pairs(w)
    wr = wr_ref[...].astype(BF16)
    nt_dims = (((1,), (1,)), ((), ()))
    log_t = (lax.dot_general(wr[:, :half], lo, nt_dims, preferred_element_type=F32)
             + lax.dot_general(wr[:, half:], hi, nt_dims, preferred_element_type=F32))

    s = jax.nn.sigmoid(log_t)
    sg = s + br_ref[...]
    eid = lax.broadcasted_iota(I32, (ne, r), 0)
    within = eid % EXP_PER_GROUP
    grp = eid // EXP_PER_GROUP

    def group_rot(x, k):
        return jnp.where(within + k < EXP_PER_GROUP,
                         pltpu.roll(x, ne - k, 0), pltpu.roll(x, EXP_PER_GROUP - k, 0))

    rank = jnp.zeros((ne, r), I32)
    for k in range(1, EXP_PER_GROUP):
        mate = group_rot(sg, k)
        wrapped = within + k >= EXP_PER_GROUP
        ahead = (mate > sg) | (wrapped & (mate == sg))
        rank = rank + ahead.astype(I32)
    top2 = rank < TOP_K
    kept = jnp.where(top2, sg, 0.0)
    gscore = kept
    for k in range(1, EXP_PER_GROUP):
        gscore = gscore + group_rot(kept, k)
    win = None
    for k in range(1, N_EXPERT_GROUPS):
        other = pltpu.roll(gscore, EXP_PER_GROUP * k, 0)
        beats = (gscore > other) | ((grp < k) & (gscore == other))
        win = beats if win is None else (win & beats)
    n_valid = jnp.where(is_s, rows_s, r)
    tok = lax.broadcasted_iota(I32, (ne, r), 1)
    sel = top2 & win & (tok < n_valid)
    selb = sel.astype(F32)
    cnt_before = cnt_acc[...]
    cnt_new = cnt_before + jnp.sum(selb, axis=1, keepdims=True)
    cnt_acc[...] = cnt_new

    @pl.when((ph == 0) & (t == 0))
    def _():
        src = lax.broadcasted_iota(I32, (r, r), 0)
        dst = lax.broadcasted_iota(I32, (r, r), 1)
        before_ref[...] = (src < dst).astype(BF16)

    @pl.when((ph == 0) & (t == last))
    def _():
        totals[...] = cnt_new

    @pl.when(ph == 1)
    def _():
        picked = jnp.where(sel, s, 0.0)
        wsum = jnp.sum(picked, axis=0, keepdims=True)
        gate = picked / jnp.where(tok[0:1, :] < n_valid, wsum, 1.0)
        ranks = jnp.dot(selb.astype(BF16), before_ref[...], preferred_element_type=F32)
        slot = (starts[...][:, 0:1] + cnt_before[:, 0:1] + ranks).astype(I32)
        e_a = jnp.min(jnp.where(sel, eid, ne), axis=0, keepdims=True)
        e_b = jnp.max(jnp.where(sel, eid, -1), axis=0, keepdims=True)
        is_a = sel & (eid == e_a)
        is_b = sel & (eid == e_b)
        pos_a = jnp.sum(jnp.where(is_a, slot, 0), axis=0, keepdims=True)
        pos_b = jnp.sum(jnp.where(is_b, slot, 0), axis=0, keepdims=True)
        w_a = jnp.sum(jnp.where(is_a, gate, 0.0), axis=0, keepdims=True)
        w_b = jnp.sum(jnp.where(is_b, gate, 0.0), axis=0, keepdims=True)
        pos_ref[0] = jnp.concatenate([pos_a, pos_b], axis=0)
        wmat = jnp.concatenate([w_a, w_b, jnp.zeros((V7X_LANES - 2, r), F32)], axis=0)
        rw_ref[...] = wmat.T

    @pl.when((ph == 1) & (t == last))
    def _():
        ends = starts[...] + padded[...]
        lane = lax.broadcasted_iota(I32, (ne, V7X_LANES), 1)
        tile_start = (lane * MOE_TILE).astype(F32)
        te = jnp.sum((tile_start >= ends).astype(I32), axis=0, keepdims=True)
        valid = te < ne
        last_e = jnp.max(jnp.where(padded[...] > 0.0, sub, 0), axis=0, keepdims=True)
        te = jnp.where(valid, te, last_e)
        n_used = jnp.sum(valid.astype(I32), axis=1, keepdims=True) + jnp.zeros((1, V7X_LANES), I32)
        last_tile = jnp.where(padded[...] > 0.0, ends - MOE_TILE, -1.0).astype(I32)
        last_tile_row = jnp.sum(jnp.where(sub == lane, last_tile, 0), axis=0, keepdims=True)
        later = jnp.min(jnp.where((sub > te) & (padded[...] > 0.0), sub, ne), axis=0, keepdims=True)
        next_e = jnp.where(later < ne, later, -1)
        zero = jnp.zeros((1, V7X_LANES), I32)
        tab_ref[...] = jnp.concatenate([te, valid.astype(I32), last_tile_row, n_used, next_e,
                                        zero, zero, zero], axis=0)


def _router(hp_p, hp_s, w_router, b_router):
    n_p, half = hp_p.shape
    rows_s = hp_s.shape[0]
    r = ROUTER_TILE
    nt_p = n_p // r
    nt = nt_p + 1
    kern = functools.partial(_router_kernel, nt_p=nt_p, rows_s=rows_s)
    pos, rw, tab = pl.pallas_call(
        kern,
        grid=(2, nt),
        in_specs=[pl.BlockSpec((r, half), lambda p, t: (jnp.minimum(t, nt_p - 1), 0)),
                  pl.BlockSpec((rows_s, half), lambda p, t: (0, 0)),
                  pl.BlockSpec((N_EXPERTS, 2 * half), lambda p, t: (0, 0)),
                  pl.BlockSpec((N_EXPERTS, 1), lambda p, t: (0, 0))],
        out_specs=[pl.BlockSpec((1, TOP_K, r), lambda p, t: (p * t, 0, 0)),
                   pl.BlockSpec((r, V7X_LANES), lambda p, t: (p * t, 0)),
                   pl.BlockSpec((V7X_SUBLANES, V7X_LANES), lambda p, t: (0, 0))],
        out_shape=[jax.ShapeDtypeStruct((nt, TOP_K, r), I32),
                   jax.ShapeDtypeStruct((nt * r, V7X_LANES), F32),
                   jax.ShapeDtypeStruct((V7X_SUBLANES, V7X_LANES), I32)],
        scratch_shapes=[pltpu.VMEM((N_EXPERTS, V7X_LANES), F32)] * 4 + [pltpu.VMEM((r, r), BF16)],
        compiler_params=_cparams(2),
        name="router",
    )(hp_p, hp_s, w_router.T, b_router.reshape(N_EXPERTS, 1))
    return pos.reshape(-1), rw, tab.reshape(-1)


def _pos_index(tok0):
    return (tok0 // ROUTER_TILE) * (TOP_K * ROUTER_TILE) + tok0 % ROUTER_TILE


def _tokens(ref, first, n=1):
    start = pl.multiple_of(first * V7X_SUBLANES, V7X_SUBLANES)
    return ref.at[pl.ds(start, n * V7X_SUBLANES), :]


def _dispatch_kernel(pos_ref, tab_ref, hpp_ref, hps_ref, xs_ref, zbuf, sem, *, n_p_steps):
    i = pl.program_id(0)
    rows = hpp_ref.shape[0] // V7X_SUBLANES

    @pl.when(i == 0)
    def _():
        zbuf[...] = jnp.zeros_like(zbuf)

        def fill(e):
            first = pl.multiple_of(tab_ref[TAB_LAST_TILE * V7X_LANES + e], MOE_TILE)
            return pltpu.make_async_copy(zbuf, _tokens(xs_ref, first, MOE_TILE), sem)

        for e in range(N_EXPERTS):
            @pl.when(tab_ref[TAB_LAST_TILE * V7X_LANES + e] >= 0)
            def _():
                fill(e).start()
        for e in range(N_EXPERTS):
            @pl.when(tab_ref[TAB_LAST_TILE * V7X_LANES + e] >= 0)
            def _():
                fill(e).wait()

        def tail(j):
            first = pl.multiple_of(j * MOE_TILE, MOE_TILE)
            return pltpu.make_async_copy(zbuf, _tokens(xs_ref, first, MOE_TILE), sem)

        def tail_start(j, carry):
            tail(j).start()
            return carry

        def tail_wait(j, carry):
            tail(j).wait()
            return carry

        n_used = tab_ref[TAB_NUSED * V7X_LANES]
        n_tiles = xs_ref.shape[0] // (MOE_TILE * V7X_SUBLANES)
        lax.fori_loop(n_used, n_tiles, tail_start, 0)
        lax.fori_loop(n_used, n_tiles, tail_wait, 0)

    def scatter(src_ref, tok0):
        n = src_ref.shape[0] // V7X_SUBLANES
        base = _pos_index(tok0)

        def row_copy(r, dst):
            return pltpu.make_async_copy(_tokens(src_ref, r), _tokens(xs_ref, dst), sem)

        def issue(r, carry):
            row_copy(r, pos_ref[base + r]).start()
            row_copy(r, pos_ref[base + ROUTER_TILE + r]).start(priority=1)
            return carry

        lax.fori_loop(0, n, issue, 0, unroll=8)
        block = pltpu.make_async_copy(src_ref, _tokens(xs_ref, 0, n), sem)
        for _ in range(TOP_K):
            block.wait()

    @pl.when(i < n_p_steps)
    def _():
        scatter(hpp_ref, i * rows)

    @pl.when(i == n_p_steps)
    def _():
        scatter(hps_ref, n_p_steps * rows)


def _dispatch(pos, tab, hpt_p, hpt_s, n_rows_sorted):
    sub = V7X_SUBLANES
    n_p_steps = hpt_p.shape[0] // (MATMUL_TILE * sub)
    kern = functools.partial(_dispatch_kernel, n_p_steps=n_p_steps)
    blk = (MATMUL_TILE * sub, V7X_LANES)
    return pl.pallas_call(
        kern,
        grid_spec=pltpu.PrefetchScalarGridSpec(
            num_scalar_prefetch=2,
            grid=(n_p_steps + 1,),
            in_specs=[pl.BlockSpec(blk, lambda i, p, t: (jnp.minimum(i, n_p_steps - 1), 0)),
                      pl.BlockSpec(hpt_s.shape, lambda i, p, t: (0, 0))],
            out_specs=pl.BlockSpec(memory_space=pl.ANY),
            scratch_shapes=[pltpu.VMEM((MOE_TILE * sub, V7X_LANES), U32), pltpu.SemaphoreType.DMA(())]),
        out_shape=jax.ShapeDtypeStruct((n_rows_sorted * sub, V7X_LANES), U32),
        compiler_params=_cparams(1),
        name="moe_dispatch",
    )(pos, tab, hpt_p, hpt_s)


def _experts_kernel(tab_ref, xs_ref, wg_hbm, wu_hbm, wd_hbm, ys_ref,
                    wg32, wu32, wd32, wgbf, wubf, wdbf, slot_ref, sems, *, layer):
    i = pl.program_id(0)
    expert = tab_ref[TAB_EXPERT * V7X_LANES + i]
    prev = tab_ref[TAB_EXPERT * V7X_LANES + jnp.maximum(i - 1, 0)]
    upcoming = tab_ref[TAB_NEXT * V7X_LANES + i]
    changed = (i == 0) | (expert != prev)
    half = V7X_SUBLANES * V7X_LANES

    def weight_copies(e, slot):
        return (pltpu.make_async_copy(wg_hbm.at[layer, e], wg32.at[slot], sems.at[0, slot]),
                pltpu.make_async_copy(wu_hbm.at[layer, e], wu32.at[slot], sems.at[1, slot]),
                pltpu.make_async_copy(wd_hbm.at[layer, e], wd32.at[slot], sems.at[2, slot]))

    @pl.when(i == 0)
    def _():
        slot_ref[0] = 0
        for cp in weight_copies(expert, 0):
            cp.start()

    @pl.when(changed & (i > 0))
    def _():
        slot_ref[0] = 1 - slot_ref[0]

    def mlp(wg, wu, wd):
        lo, hi = _unpack_bf16_pairs(_load_token_tiles(xs_ref))
        a = (jnp.dot(lo, wg[:half, :], preferred_element_type=F32)
             + jnp.dot(hi, wg[half:, :], preferred_element_type=F32))
        b = (jnp.dot(lo, wu[:half, :], preferred_element_type=F32)
             + jnp.dot(hi, wu[half:, :], preferred_element_type=F32))
        hid = (a * jax.nn.sigmoid(a)) * b
        y = jnp.dot(hid.astype(BF16), wd, preferred_element_type=F32)
        _store_token_tiles(ys_ref, _pack_bf16_pairs(y))

    for slot in range(2):
        @pl.when(changed & (slot_ref[0] == slot))
        def _():
            for cp in weight_copies(expert, slot):
                cp.wait()

            @pl.when(upcoming >= 0)
            def _():
                for cp in weight_copies(upcoming, 1 - slot):
                    cp.start(priority=1)

            wg = wg32[slot].astype(BF16)
            wu = wu32[slot].astype(BF16)
            wd = wd32[slot].astype(BF16)
            wgbf[...] = wg
            wubf[...] = wu
            wdbf[...] = wd
            mlp(wg, wu, wd)

    valid = tab_ref[TAB_VALID * V7X_LANES + i] > 0

    @pl.when(valid & jnp.logical_not(changed))
    def _():
        mlp(wgbf[...], wubf[...], wdbf[...])

    @pl.when(jnp.logical_not(valid))
    def _():
        ys_ref[...] = jnp.zeros_like(ys_ref)


def _experts(tab, xs, w_gate, w_up, w_down, layer):
    sub = V7X_SUBLANES
    _, _, d, f = w_gate.shape
    nt = xs.shape[0] // (MOE_TILE * sub)
    assert nt <= V7X_LANES and d == 2 * sub * V7X_LANES
    blk = (MOE_TILE * sub, V7X_LANES)

    def tile(i, tab_ref):
        return jnp.minimum(i, tab_ref[TAB_NUSED * V7X_LANES] - 1)

    hbm = pl.BlockSpec(memory_space=pl.ANY)
    return pl.pallas_call(
        functools.partial(_experts_kernel, layer=layer),
        grid_spec=pltpu.PrefetchScalarGridSpec(
            num_scalar_prefetch=1,
            grid=(nt,),
            in_specs=[pl.BlockSpec(blk, lambda i, t: (tile(i, t), 0)), hbm, hbm, hbm],
            out_specs=pl.BlockSpec(blk, lambda i, t: (i, 0)),
            scratch_shapes=[pltpu.VMEM((2, d, f), F32), pltpu.VMEM((2, d, f), F32), pltpu.VMEM((2, f, d), F32),
                            pltpu.VMEM((d, f), BF16), pltpu.VMEM((d, f), BF16), pltpu.VMEM((f, d), BF16),
                            pltpu.SMEM((1,), I32), pltpu.SemaphoreType.DMA((3, 2))]),
        out_shape=jax.ShapeDtypeStruct(xs.shape, U32),
        compiler_params=_cparams(1),
        name="moe_experts",
    )(tab, xs, w_gate, w_up, w_down)


def _combine_kernel(pos_ref, ys_ref, x1_ref, rw_ref, g2_ref, ng_ref, sc_ref, sh_ref, *rest,
                    tok0, tiles_per_seq, n_steps, final):
    if final:
        x2_ref, buf, sems = rest
        hn_ref = None
    else:
        x2_ref, hn_ref, buf, sems = rest
    i = pl.program_id(0)
    seq = i // tiles_per_seq
    rows = x1_ref.shape[0]

    def gather(step, slot):
        base = _pos_index(tok0 + step * rows)

        def issue(r, carry):
            for k in range(TOP_K):
                src = pos_ref[base + k * ROUTER_TILE + r]
                pltpu.make_async_copy(_tokens(ys_ref, src), _tokens(buf.at[slot, k], r),
                                      sems.at[slot]).start(priority=k)
            return carry

        lax.fori_loop(0, rows, issue, 0, unroll=8)

    @pl.when(i == 0)
    def _():
        gather(0, 0)

    @pl.when(i + 1 < n_steps)
    def _():
        gather(i + 1, (i + 1) % 2)

    slot = i % 2
    for k in range(TOP_K):
        pltpu.make_async_copy(_tokens(ys_ref, 0, rows), buf.at[slot, k], sems.at[slot]).wait()

    rw = rw_ref[...]
    lo_a, hi_a = _unpack_pairs_f32(_load_token_tiles(buf.at[slot, 0]))
    lo_b, hi_b = _unpack_pairs_f32(_load_token_tiles(buf.at[slot, 1]))
    w_a = rw[:, 0:1]
    w_b = rw[:, 1:2]
    moe = jnp.concatenate([w_a * lo_a + w_b * lo_b, w_a * hi_a + w_b * hi_b], axis=-1)
    x2 = x1_ref[...] + _mod_rows(g2_ref, seq) * moe
    if final:
        x2_ref[...] = _rms(x2, ng_ref[...])
    else:
        x2_ref[...] = x2
        hn_ref[...] = (_rms(x2, ng_ref[...]) * (1.0 + _mod_rows(sc_ref, seq))
                       + _mod_rows(sh_ref, seq)).astype(BF16)


def _combine(pos, ys, x1, rw, tok0, mod, layer, ng, seq_rows, final):
    n_tok, d = x1.shape
    rt = min(n_tok, MATMUL_TILE)
    assert tok0 % rt == 0 and ROUTER_TILE % rt == 0
    kern = functools.partial(_combine_kernel, tok0=tok0, tiles_per_seq=seq_rows // rt,
                             n_steps=n_tok // rt, final=final)
    rw_off = tok0 // rt
    row_spec = lambda w: pl.BlockSpec((rt, w), lambda i, p: (i, 0))
    out_shape = [jax.ShapeDtypeStruct((n_tok, d), F32)]
    out_specs = [row_spec(d)]
    if not final:
        out_shape.append(jax.ShapeDtypeStruct((n_tok, d), BF16))
        out_specs.append(row_spec(d))
    nxt = min(layer + 1, mod.shape[0] - 1)
    return pl.pallas_call(
        kern,
        grid_spec=pltpu.PrefetchScalarGridSpec(
            num_scalar_prefetch=1,
            grid=(n_tok // rt,),
            in_specs=[pl.BlockSpec(memory_space=pl.ANY), row_spec(d),
                      pl.BlockSpec((rt, V7X_LANES), lambda i, p: (rw_off + i, 0)),
                      _mod_spec(mod, layer, 5), pl.BlockSpec((1, d), lambda i, p: (0, 0)),
                      _mod_spec(mod, nxt, 1), _mod_spec(mod, nxt, 0)],
            out_specs=out_specs,
            scratch_shapes=[pltpu.VMEM((2, TOP_K, rt * V7X_SUBLANES, V7X_LANES), U32),
                            pltpu.SemaphoreType.DMA((2,))]),
        out_shape=out_shape,
        compiler_params=_cparams(1),
        name="moe_combine_final" if final else "moe_combine",
    )(pos, ys, x1, rw, mod, ng.reshape(1, d), mod, mod)


def _moe(out_p, out_s, mod_p, mod_s, layer, ng, seq_rows_p, w_router, b_router, w_gate, w_up, w_down, final):
    x1_p, hp_p, hpt_p = out_p
    x1_s, hp_s, hpt_s = out_s
    n_p = x1_p.shape[0]
    n_tok = n_p + x1_s.shape[0]
    max_rows = TOP_K * n_tok + N_EXPERTS * (MOE_TILE - 1)
    n_rows_sorted = -(-max_rows // MOE_TILE) * MOE_TILE
    pos, rw, tab = _router(hp_p, hp_s, w_router, b_router)
    xs = _dispatch(pos, tab, hpt_p, hpt_s, n_rows_sorted)
    ys = _experts(tab, xs, w_gate, w_up, w_down, layer)
    out_p = _combine(pos, ys, x1_p, rw, 0, mod_p, layer, ng, seq_rows_p, final)
    out_s = _combine(pos, ys, x1_s, rw, n_p, mod_s, layer, ng, x1_s.shape[0], final)
    return out_p, out_s


def _gmlp_kernel(h_ref, w_ref, lg_ref, lb_ref, ws_ref, bs_ref, yc_ref, *rest, ell, blk, emit_v):
    if emit_v:
        gv_ref, wbf, wsbf = rest
    else:
        wbf, wsbf = rest
    i = pl.program_id(0)
    rows = h_ref.shape[0]
    c = GM_WIDTH

    @pl.when(i == 0)
    def _():
        wbf[...] = w_ref[...].astype(BF16)
        r = lax.broadcasted_iota(I32, (ell, ell), 0)
        s = lax.broadcasted_iota(I32, (ell, ell), 1)
        keep = (r >= s) & ((r // blk) == (s // blk))
        rsel = (lax.broadcasted_iota(I32, (ell, CHUNK), 0) % blk
                == lax.broadcasted_iota(I32, (ell, CHUNK), 1)).astype(BF16)
        csel = (lax.broadcasted_iota(I32, (CHUNK, ell), 1) % blk
                == lax.broadcasted_iota(I32, (CHUNK, ell), 0)).astype(BF16)
        for g in range(GM_GROUPS):
            wchunk = ws_ref[g].astype(BF16)
            if blk == ell:
                full = wchunk
            else:
                rowsp = jnp.dot(rsel, wchunk, preferred_element_type=F32).astype(BF16)
                full = jnp.dot(rowsp, csel, preferred_element_type=F32).astype(BF16)
            wsbf[g] = jnp.where(keep, full, jnp.zeros_like(full))

    uv = jnp.dot(h_ref[...], wbf[...], preferred_element_type=F32)
    u = uv[:, :c]
    v = uv[:, c:]
    vc = v - jnp.mean(v, axis=-1, keepdims=True)
    vn = vc * lax.rsqrt(jnp.mean(vc * vc, axis=-1, keepdims=True) + EPS) * lg_ref[...] + lb_ref[...]
    if emit_v:
        gv_ref[...] = vn
    vb = vn.astype(BF16)
    bs = bs_ref[...]
    for ch in range(rows // ell):
        rs = slice(ch * ell, (ch + 1) * ell)
        outs = []
        for g in range(GM_GROUPS):
            cs = slice(g * GM_GROUP, (g + 1) * GM_GROUP)
            mixed = jnp.dot(wsbf[g], vb[rs, cs], preferred_element_type=F32)
            mixed = (mixed.reshape(ell // blk, blk, GM_GROUP) + bs[:blk, g:g + 1][None]).reshape(ell, GM_GROUP)
            outs.append(u[rs, cs] * mixed)
        yc_ref[rs, :] = jnp.concatenate(outs, axis=-1).astype(BF16)


def _gmlp_mixer(h2, w_in, ln_g, ln_b, ws, bs_t, ell, blk, emit_v):
    rows, d = h2.shape
    c = GM_WIDTH
    kern = functools.partial(_gmlp_kernel, ell=ell, blk=blk, emit_v=emit_v)
    rt = min(rows, MATMUL_TILE)
    out_specs = [pl.BlockSpec((rt, c), lambda i: (i, 0))]
    out_shape = [jax.ShapeDtypeStruct((rows, c), BF16)]
    if emit_v:
        out_specs.append(pl.BlockSpec((rt, c), lambda i: (i, 0)))
        out_shape.append(jax.ShapeDtypeStruct((rows, c), F32))
    return pl.pallas_call(
        kern,
        grid=(rows // rt,),
        in_specs=[pl.BlockSpec((rt, d), lambda i: (i, 0)),
                  pl.BlockSpec((d, 2 * c), lambda i: (0, 0), pipeline_mode=pl.Buffered(1)),
                  pl.BlockSpec((1, c), lambda i: (0, 0)),
                  pl.BlockSpec((1, c), lambda i: (0, 0)),
                  pl.BlockSpec((GM_GROUPS, CHUNK, CHUNK), lambda i: (0, 0, 0)),
                  pl.BlockSpec((CHUNK, GM_GROUPS), lambda i: (0, 0))],
        out_specs=out_specs,
        out_shape=out_shape,
        scratch_shapes=[pltpu.VMEM((d, 2 * c), BF16), pltpu.VMEM((GM_GROUPS, ell, ell), BF16)],
        compiler_params=_cparams(1),
        name="gmlp_mixer",
    )(h2, w_in, ln_g.reshape(1, c), ln_b.reshape(1, c), ws, bs_t)


PAIR_W = 2 * HEAD_DIM
PAIRS_PER_KV = N_HEADS // N_KV // 2
NT_DIMS = (((1,), (1,)), ((), ()))
SCORE_SCALE = HEAD_DIM ** -0.5
assert float(np.log2(SCORE_SCALE)).is_integer()


def _swa_project(i, h_ref, wq_ref, wkv_ref, wbf):
    nq = N_HEADS * HEAD_DIM

    @pl.when(i == 0)
    def _():
        wbf[:, :nq] = wq_ref[...].astype(BF16)
        wbf[:, nq:] = wkv_ref[...].astype(BF16)

    return jnp.dot(h_ref[...], wbf[...], preferred_element_type=F32)


def _pair_block_diag(a, a_swapped, hk, axis):
    dim_axis = 1 - axis
    low = lax.broadcasted_iota(I32, a.shape, dim_axis) < HEAD_DIM
    lo, hi = (a, a_swapped) if hk == 0 else (a_swapped, a)
    return jnp.concatenate([jnp.where(low, lo, 0.0), jnp.where(low, 0.0, hi)], axis=axis).astype(BF16)


def _stack_pairs(qkv, rs, hk, scale=None):
    p0 = hk * PAIRS_PER_KV
    q = jnp.concatenate([qkv[rs, (p0 + pp) * PAIR_W:(p0 + pp + 1) * PAIR_W]
                         for pp in range(PAIRS_PER_KV)], axis=0)
    return (q if scale is None else q * scale).astype(BF16)


def _swa_cached_kernel(h_ref, wq_ref, wkv_ref, kp_ref, vp_ref, bias_ref, sink_ref, yd_ref, k_ref, v_ref,
                       wbf, *, tq):
    i = pl.program_id(0)
    rows = h_ref.shape[0]
    nq = N_HEADS * HEAD_DIM
    nkv = N_KV * HEAD_DIM
    n_blocks = rows // tq
    qkv = _swa_project(i, h_ref, wq_ref, wkv_ref, wbf)
    k_new = qkv[:, nq:nq + nkv]
    v_new = qkv[:, nq + nkv:]
    k_ref[...] = k_new
    v_ref[...] = v_new
    pad = jnp.zeros((WINDOW - tq, nkv), F32)

    scores, vbds = [], []
    for blk in range(n_blocks):
        rs = slice(blk * tq, (blk + 1) * tq)
        kcat = jnp.concatenate([kp_ref[blk], k_new[rs], pad], axis=0)
        vcat = jnp.concatenate([vp_ref[blk], v_new[rs], pad], axis=0)
        kswap = pltpu.roll(kcat, HEAD_DIM, 1)
        vswap = pltpu.roll(vcat, HEAD_DIM, 1)
        per_head = []
        for hk in range(N_KV):
            kbd = _pair_block_diag(kcat, kswap, hk, 0)
            vbds.append(_pair_block_diag(vcat, vswap, hk, 0))
            s4 = lax.dot_general(_stack_pairs(qkv, rs, hk), kbd, NT_DIMS,
                                 preferred_element_type=F32) * (HEAD_DIM ** -0.5)
            for pp in range(PAIRS_PER_KV):
                for sub in range(2):
                    per_head.append(s4[pp * tq:(pp + 1) * tq, sub * 2 * WINDOW:(sub + 1) * 2 * WINDOW])
        scores.append(jnp.concatenate(per_head, axis=0))

    s_all = jnp.stack(scores, axis=0) + bias_ref[...][None]
    sink = sink_ref[...][None]
    m = jnp.maximum(jnp.max(s_all, axis=-1, keepdims=True), sink)
    pr = jnp.exp(s_all - m)
    pr = pr / (jnp.sum(pr, axis=-1, keepdims=True) + jnp.exp(sink - m))

    for blk in range(n_blocks):
        outs = []
        for hk in range(N_KV):
            p4 = []
            for pp in range(PAIRS_PER_KV):
                h0 = 2 * (hk * PAIRS_PER_KV + pp)
                p4.append(jnp.concatenate([pr[blk, h0 * tq:(h0 + 1) * tq, :],
                                           pr[blk, (h0 + 1) * tq:(h0 + 2) * tq, :]], axis=-1))
            o4 = jnp.dot(jnp.concatenate(p4, axis=0).astype(BF16), vbds[blk * N_KV + hk],
                         preferred_element_type=F32)
            outs.extend(o4[pp * tq:(pp + 1) * tq, :] for pp in range(PAIRS_PER_KV))
        yd_ref[blk * tq:(blk + 1) * tq, :] = jnp.concatenate(outs, axis=-1).astype(BF16)


def _swa_stream_kernel(h_ref, wq_ref, wkv_ref, bias_ref, sink_ref, yd_ref, k_ref, v_ref,
                       wbf, kprev, vprev_t, *, blocks_per_seq):
    i = pl.program_id(0)
    rows = h_ref.shape[0]
    nq = N_HEADS * HEAD_DIM
    nkv = N_KV * HEAD_DIM
    tq = WINDOW
    n_blocks = rows // tq

    @pl.when(i == 0)
    def _():
        kprev[...] = jnp.zeros_like(kprev)
        vprev_t[...] = jnp.zeros_like(vprev_t)

    qkv = _swa_project(i, h_ref, wq_ref, wkv_ref, wbf)
    k_new = qkv[:, nq:nq + nkv]
    v_new = qkv[:, nq + nkv:]
    k_ref[...] = k_new
    v_ref[...] = v_new
    v_new_t = v_new.T
    lanes = PAIRS_PER_KV * tq

    for blk in range(n_blocks):
        rs = slice(blk * tq, (blk + 1) * tq)
        first = ((i * n_blocks + blk) % blocks_per_seq == 0).astype(I32)
        k_cur = k_new[rs]
        v_cur_t = v_new_t[:, rs]
        kcat = jnp.concatenate([kprev[...], k_cur], axis=0)
        vcat_t = jnp.concatenate([vprev_t[...], v_cur_t], axis=1)
        kprev[...] = k_cur
        vprev_t[...] = v_cur_t
        kswap = pltpu.roll(kcat, HEAD_DIM, 1)
        vswap_t = pltpu.roll(vcat_t, HEAD_DIM, 0)
        outs = []
        for hk in range(N_KV):
            kbd = _pair_block_diag(kcat, kswap, hk, 0)
            vbd_t = _pair_block_diag(vcat_t, vswap_t, hk, 1)
            st = lax.dot_general(kbd, _stack_pairs(qkv, rs, hk, SCORE_SCALE), NT_DIMS,
                                 preferred_element_type=F32)
            s3 = st.reshape(2, 2 * WINDOW, lanes) + bias_ref[first, hk]
            sink = sink_ref[hk]
            m = jnp.maximum(jnp.max(s3, axis=1, keepdims=True), sink)
            pr = jnp.exp(s3 - m)
            inv = 1.0 / (jnp.sum(pr, axis=1, keepdims=True) + jnp.exp(sink - m))
            o_t = jnp.dot(vbd_t, pr.reshape(4 * WINDOW, lanes).astype(BF16),
                          preferred_element_type=F32)
            norm = jnp.concatenate([jnp.broadcast_to(inv[sub], (HEAD_DIM, lanes)) for sub in range(2)], axis=0)
            o4 = (o_t * norm).T
            outs.extend(o4[pp * tq:(pp + 1) * tq, :] for pp in range(PAIRS_PER_KV))
        yd_ref[rs, :] = jnp.concatenate(outs, axis=-1).astype(BF16)


def _swa_weight_specs(w_in, d):
    nq = N_HEADS * HEAD_DIM
    nkv = N_KV * HEAD_DIM
    nw = nq + 2 * nkv
    q_blk = (w_in.shape[1] - nw) // nq
    kv_blk = (w_in.shape[1] - 2 * nkv) // (2 * nkv)
    assert q_blk * nq + nw == w_in.shape[1] and kv_blk * 2 * nkv + 2 * nkv == w_in.shape[1]
    return [pl.BlockSpec((d, nq), lambda i: (0, q_blk)), pl.BlockSpec((d, 2 * nkv), lambda i: (0, kv_blk))]


def _swa_outputs(rows, rt):
    nq = N_HEADS * HEAD_DIM
    nkv = N_KV * HEAD_DIM
    specs = [pl.BlockSpec((rt, nq), lambda i: (i, 0)),
             pl.BlockSpec((rt, nkv), lambda i: (i, 0)),
             pl.BlockSpec((rt, nkv), lambda i: (i, 0))]
    shapes = [jax.ShapeDtypeStruct((rows, nq), BF16),
              jax.ShapeDtypeStruct((rows, nkv), F32),
              jax.ShapeDtypeStruct((rows, nkv), F32)]
    return specs, shapes


def _swa_cached_mixer(h2, w_in, k_cache, v_cache, bias, sinks, tq):
    rows, d = h2.shape
    nkv = N_KV * HEAD_DIM
    nw = N_HEADS * HEAD_DIM + 2 * nkv
    n_blocks = ROW_TILE // tq
    cache_spec = pl.BlockSpec((n_blocks, WINDOW, nkv), lambda i: (i, 0, 0))
    out_specs, out_shape = _swa_outputs(rows, ROW_TILE)
    return pl.pallas_call(
        functools.partial(_swa_cached_kernel, tq=tq),
        grid=(rows // ROW_TILE,),
        in_specs=[pl.BlockSpec((ROW_TILE, d), lambda i: (i, 0))] + _swa_weight_specs(w_in, d)
        + [cache_spec, cache_spec,
           pl.BlockSpec((N_HEADS * tq, 2 * WINDOW), lambda i: (0, 0)),
           pl.BlockSpec((N_HEADS * tq, 1), lambda i: (0, 0))],
        out_specs=out_specs,
        out_shape=out_shape,
        scratch_shapes=[pltpu.VMEM((d, nw), BF16)],
        compiler_params=_cparams(1),
        name="swa_cached",
    )(h2, w_in, w_in, k_cache, v_cache, bias, sinks)


def _swa_stream_mixer(h2, w_in, bias_t, sinks_t, blocks_per_seq):
    rows, d = h2.shape
    nkv = N_KV * HEAD_DIM
    nw = N_HEADS * HEAD_DIM + 2 * nkv
    lanes = PAIRS_PER_KV * WINDOW
    rt = min(rows, MATMUL_TILE)
    out_specs, out_shape = _swa_outputs(rows, rt)
    return pl.pallas_call(
        functools.partial(_swa_stream_kernel, blocks_per_seq=blocks_per_seq),
        grid=(rows // rt,),
        in_specs=[pl.BlockSpec((rt, d), lambda i: (i, 0))] + _swa_weight_specs(w_in, d)
        + [pl.BlockSpec((2, N_KV, 2, 2 * WINDOW, lanes), lambda i: (0, 0, 0, 0, 0)),
           pl.BlockSpec((N_KV, 2, 1, lanes), lambda i: (0, 0, 0, 0))],
        out_specs=out_specs,
        out_shape=out_shape,
        scratch_shapes=[pltpu.VMEM((d, nw), BF16), pltpu.VMEM((WINDOW, nkv), F32),
                        pltpu.VMEM((nkv, WINDOW), F32)],
        compiler_params=_cparams(1),
        name="swa_stream",
    )(h2, w_in, w_in, bias_t, sinks_t)


def _t5_bucket(dist):
    max_exact = N_BUCKETS // 2
    dd = np.maximum(dist, 1)
    large = max_exact + (np.log(dd / max_exact) / np.log(WINDOW / max_exact)
                         * (N_BUCKETS - max_exact)).astype(np.int64)
    large = np.minimum(large, N_BUCKETS - 1)
    return np.where(dist < max_exact, dist, large).astype(np.int32)


def _attention_bias(rel_bias):
    by_dist = jnp.take(rel_bias.astype(F32), _t5_bucket(np.arange(WINDOW)), axis=0).T
    neg = jnp.full((N_HEADS, WINDOW), NEG_INF, F32)
    line = jnp.concatenate([neg, by_dist[:, ::-1], neg[:, :WINDOW - 1]], axis=1)
    return jnp.stack([line[:, WINDOW - 1 - q:3 * WINDOW - 1 - q] for q in range(WINDOW)], axis=1)


def kernel(x_prompt, x_sample, state_pool, state_conv, cache_swa_k, cache_swa_v, c_prompt, c_sample, w_ada, b_ada, norm_g, final_norm_g, w_in_even, w_out_even, w_pool, pool_scale, conv_w, w_in_odd, w_out_odd, gm_norm_g, gm_norm_b, gm_w_s, gm_b_s, attn_sinks, rel_bias, w_router, b_router, w_gate, w_up, w_down):
    d = D_MODEL
    bp, tp, _ = x_prompt.shape
    bs, ts, _ = x_sample.shape
    rows_s = bs * ts
    assert rows_s == ROW_TILE and tp % ROUTER_TILE == 0 and PAST_LEN % CHUNK == 0
    assert bp <= V7X_SUBLANES and CHUNK % ts == 0

    n_c = bp + bs
    c_pad = (-n_c) % V7X_SUBLANES
    c_all = jnp.concatenate([c_prompt, c_sample, jnp.zeros((c_pad, d), F32)], axis=0)
    mod_p = _adaln(c_all, w_ada, b_ada)
    mod_s = jnp.repeat(mod_p[:, bp:bp + bs], ts, axis=1)

    xp = x_prompt.reshape(bp * tp, d)
    xs_ = x_sample.reshape(rows_s, d)
    w_in0, w_in1 = w_in_even[0], w_in_odd[0]

    zero_pool = jnp.zeros((bp, POOL_STATE, POOL_WIDTH), F32)
    zero_conv = jnp.zeros((bp, CONV_K - 1, CONV_WIDTH), F32)
    hp0, ya_p, pool_p = _pool_mixer(xp, norm_g[0, 0], mod_p, 0, w_in0, w_pool[0], pool_scale[0],
                                    zero_pool, 1, MATMUL_TILE, 0)
    hs0, ya_s, pool_s = _pool_mixer(xs_, norm_g[0, 0], mod_s, 0, w_in0, w_pool[0], pool_scale[0],
                                    state_pool[0], bs, ts, PAST_LEN)
    yb_p, conv_p = _conv_mixer(hp0, w_in0, conv_w[0], zero_conv, 1, MATMUL_TILE)
    yb_s, conv_s = _conv_mixer(hs0, w_in0, conv_w[0], state_conv[0], bs, ts)
    out_p = _outproj(ya_p, yb_p, xp, mod_p, 0, norm_g[0, 1], w_out_even[0], tp)
    out_s = _outproj(ya_s, yb_s, xs_, mod_s, 0, norm_g[0, 1], w_out_even[0], rows_s)
    (x2p, h1p), (x2s, h1s) = _moe(out_p, out_s, mod_p, mod_s, 0, norm_g[1, 0], tp,
                                  w_router, b_router, w_gate, w_up, w_down, final=False)

    bs_t = gm_b_s[0].T
    (yc_p,) = _gmlp_mixer(h1p, w_in1, gm_norm_g[0], gm_norm_b[0], gm_w_s[0], bs_t, CHUNK, CHUNK, False)
    yc_s, gv_s = _gmlp_mixer(h1s, w_in1, gm_norm_g[0], gm_norm_b[0], gm_w_s[0], bs_t, rows_s, ts, True)
    bias = _attention_bias(rel_bias)
    nkv = N_KV * HEAD_DIM
    bias_t = jnp.transpose(bias.reshape(N_KV, PAIRS_PER_KV, 2, WINDOW, 2 * WINDOW), (0, 2, 4, 1, 3))
    bias_t = bias_t.reshape(N_KV, 2, 2 * WINDOW, PAIRS_PER_KV * WINDOW)
    before_start = (np.arange(2 * WINDOW) < WINDOW)[None, None, :, None]
    bias_t = jnp.stack([bias_t, jnp.where(before_start, NEG_INF, bias_t)], axis=0)
    sinks_t = jnp.transpose(attn_sinks[0].reshape(N_KV, PAIRS_PER_KV, 2), (0, 2, 1))
    sinks_t = jnp.repeat(sinks_t, WINDOW, axis=-1).reshape(N_KV, 2, 1, PAIRS_PER_KV * WINDOW)
    yd_p, k_p, v_p = _swa_stream_mixer(h1p, w_in1, bias_t, sinks_t, tp // WINDOW)
    yd_s, k_s, v_s = _swa_cached_mixer(h1s, w_in1, cache_swa_k[0].reshape(bs, WINDOW, nkv),
                                       cache_swa_v[0].reshape(bs, WINDOW, nkv),
                                       bias[:, :ts, :].reshape(N_HEADS * ts, 2 * WINDOW),
                                       jnp.repeat(attn_sinks[0], ts).reshape(-1, 1), ts)
    out_p = _outproj(yc_p, yd_p, x2p, mod_p, 1, norm_g[1, 1], w_out_odd[0], tp)
    out_s = _outproj(yc_s, yd_s, x2s, mod_s, 1, norm_g[1, 1], w_out_odd[0], rows_s)
    (yp,), (ys_out,) = _moe(out_p, out_s, mod_p, mod_s, 1, final_norm_g, tp,
                            w_router, b_router, w_gate, w_up, w_down, final=True)

    k_p4 = k_p.reshape(bp, tp, N_KV, HEAD_DIM)[:, -WINDOW:]
    v_p4 = v_p.reshape(bp, tp, N_KV, HEAD_DIM)[:, -WINDOW:]
    k_s4 = jnp.concatenate([cache_swa_k[0], k_s.reshape(bs, ts, N_KV, HEAD_DIM)], axis=1)[:, -WINDOW:]
    v_s4 = jnp.concatenate([cache_swa_v[0], v_s.reshape(bs, ts, N_KV, HEAD_DIM)], axis=1)[:, -WINDOW:]
    return (yp.reshape(bp, tp, d), ys_out.reshape(bs, ts, d),
            pool_p[None], pool_s[None], conv_p[None], conv_s[None],
            k_p4[None], k_s4[None], v_p4[None], v_s4[None],
            gv_s.reshape(bs, ts, GM_WIDTH)[None])
```

```python
import functools

import numpy as np
import jax
import jax.numpy as jnp
from jax import lax
from jax.experimental import pallas as pl
from jax.experimental.pallas import tpu as pltpu

F32 = jnp.float32
BF16 = jnp.bfloat16
I32 = jnp.int32
U32 = jnp.uint32

D_MODEL = 2048
POOL_WINDOWS = (2, 4, 8, 16)
POOL_WIDTH = 1024
POOL_GROUP = 256
POOL_STATE = 15
CONV_WIDTH = 1024
CONV_K = 3
GM_WIDTH = 1024
GM_GROUPS = 8
GM_GROUP = 128
CHUNK = 128
HEAD_DIM = 64
N_HEADS = 16
N_KV = 2
WINDOW = 128
N_BUCKETS = 32
N_EXPERTS = 16
N_EXPERT_GROUPS = 4
EXP_PER_GROUP = 4
TOP_K = 2
EPS = 1e-6
NEG_INF = -1e30
PAST_LEN = 16384

V7X_SUBLANES = 8
V7X_LANES = 128
VMEM_LIMIT = 56 * 1024 * 1024

ROW_TILE = 256
MATMUL_TILE = 512
SUB_TILE = 256
ROUTER_TILE = 1024
POOL_HALO = 16
CONV_HALO = 8
MOE_TILE = 256
TAB_EXPERT, TAB_VALID, TAB_LAST_TILE, TAB_NUSED, TAB_NEXT = 0, 1, 2, 3, 4


def _cparams(n_axes):
    return pltpu.CompilerParams(dimension_semantics=("arbitrary",) * n_axes,
                                vmem_limit_bytes=VMEM_LIMIT)


def _rms(x, g):
    return x * lax.rsqrt(jnp.mean(x * x, axis=-1, keepdims=True) + EPS) * g


def _mod_spec(mod, layer, part):
    nrow = ROW_TILE if mod.shape[1] == ROW_TILE else V7X_SUBLANES
    return pl.BlockSpec((1, nrow, D_MODEL), lambda *_: (layer, 0, part))


def _mod_rows(m_ref, seq, rs=None):
    if m_ref.shape[1] == V7X_SUBLANES:
        return m_ref[0, pl.ds(seq, 1), :]
    return m_ref[0] if rs is None else m_ref[0, rs, :]


def _adaln_kernel(c_ref, w_ref, b_ref, o_ref):
    c = c_ref[...]
    a = (c * jax.nn.sigmoid(c)).astype(BF16)
    o_ref[0] = jnp.dot(a, w_ref[0].astype(BF16), preferred_element_type=F32) + b_ref[0]


def _adaln(c_all, w_ada, b_ada):
    depth, d, n6 = w_ada.shape
    m = c_all.shape[0]
    tn = 1024
    return pl.pallas_call(
        _adaln_kernel,
        grid=(depth, n6 // tn),
        in_specs=[pl.BlockSpec((m, d), lambda l, j: (0, 0)),
                  pl.BlockSpec((1, d, tn), lambda l, j: (l, 0, j)),
                  pl.BlockSpec((1, 1, tn), lambda l, j: (l, 0, j))],
        out_specs=pl.BlockSpec((1, m, tn), lambda l, j: (l, 0, j)),
        out_shape=jax.ShapeDtypeStruct((depth, m, n6), F32),
        compiler_params=_cparams(2),
        name="adaln",
    )(c_all, w_ada, b_ada.reshape(depth, 1, n6))


def _pool_kernel(x_ref, g_ref, sc_ref, sh_ref, w_ref, wp_ref, ps_ref, st_ref, h_ref, ya_ref, ns_ref,
                 wbf, wpbf, carry, *, nb, tm, tiles_per_seq, start):
    i = pl.program_id(0)
    t = i % tiles_per_seq
    seq = i // tiles_per_seq
    c = POOL_WIDTH
    halo = POOL_HALO

    @pl.when(i == 0)
    def _():
        wbf[...] = w_ref[...].astype(BF16)
        wpbf[...] = wp_ref[...].astype(BF16)

    @pl.when(t == 0)
    def _():
        carry[...] = st_ref[...]

    h = (_rms(x_ref[...], g_ref[...]) * (1.0 + _mod_rows(sc_ref, seq)) + _mod_rows(sh_ref, seq)).astype(BF16)
    h_ref[...] = h
    p = jnp.dot(h, wbf[...], preferred_element_type=F32)
    p3 = p.reshape(nb, tm, c)
    ext3 = jnp.concatenate([carry[...], p3], axis=1)
    tail = ext3[:, tm:tm + halo, :]
    ns_ref[...] = tail
    carry[...] = tail
    ext = ext3.reshape(nb * (halo + tm), c)
    pos = start + t * tm + lax.broadcasted_iota(I32, (1, tm, 1), 1)
    outs = []
    for gi, w in enumerate(POOL_WINDOWS):
        sl = slice(gi * POOL_GROUP, (gi + 1) * POOL_GROUP)
        acc = ext[:, sl]
        shift = 1
        while shift < w:
            acc = acc + pltpu.roll(acc, shift, 0)
            shift *= 2
        win = acc.reshape(nb, halo + tm, POOL_GROUP)[:, halo:, :]
        cnt = jnp.minimum(pos + 1, w).astype(F32)
        dgrp = win / cnt - p3[:, :, sl]
        outs.append(jnp.dot(dgrp.reshape(nb * tm, POOL_GROUP).astype(BF16), wpbf[gi],
                            preferred_element_type=F32))
    y = jnp.concatenate(outs, axis=-1) * ps_ref[...]
    ya_ref[...] = y.astype(BF16)


def _pool_mixer(x2, g, mod, layer, w_in, w_pool, pool_scale, state, nb, tm, start):
    rows, d = x2.shape
    nseq = state.shape[0]
    tiles_per_seq = (rows // nseq) // tm
    seq_blocks = nseq // nb
    c = POOL_WIDTH
    st = jnp.pad(state, ((0, 0), (POOL_HALO - POOL_STATE, 0), (0, 0)))
    kern = functools.partial(_pool_kernel, nb=nb, tm=tm, tiles_per_seq=tiles_per_seq, start=start)
    h2, ya, ns = pl.pallas_call(
        kern,
        grid=(seq_blocks * tiles_per_seq,),
        in_specs=[pl.BlockSpec((nb * tm, d), lambda i: (i, 0)),
                  pl.BlockSpec((1, d), lambda i: (0, 0)),
                  _mod_spec(mod, layer, 1), _mod_spec(mod, layer, 0),
                  pl.BlockSpec((d, c), lambda i: (0, 0)),
                  pl.BlockSpec((len(POOL_WINDOWS), POOL_GROUP, POOL_GROUP), lambda i: (0, 0, 0)),
                  pl.BlockSpec((1, c), lambda i: (0, 0)),
                  pl.BlockSpec((nb, POOL_HALO, c), lambda i: (i // tiles_per_seq, 0, 0))],
        out_specs=[pl.BlockSpec((nb * tm, d), lambda i: (i, 0)),
                   pl.BlockSpec((nb * tm, c), lambda i: (i, 0)),
                   pl.BlockSpec((nb, POOL_HALO, c), lambda i: (i // tiles_per_seq, 0, 0))],
        out_shape=[jax.ShapeDtypeStruct((rows, d), BF16),
                   jax.ShapeDtypeStruct((rows, c), BF16),
                   jax.ShapeDtypeStruct((nseq, POOL_HALO, c), F32)],
        scratch_shapes=[pltpu.VMEM((d, c), BF16),
                        pltpu.VMEM((len(POOL_WINDOWS), POOL_GROUP, POOL_GROUP), BF16),
                        pltpu.VMEM((nb, POOL_HALO, c), F32)],
        compiler_params=_cparams(1),
        name="pool_mixer",
    )(x2, g.reshape(1, d), mod, mod, w_in, w_pool, pool_scale.reshape(1, c), st)
    return h2, ya, ns[:, POOL_HALO - POOL_STATE:, :]


def _conv_kernel(h_ref, wx_ref, wb_ref, wc_ref, cw_ref, st_ref, yb_ref, ns_ref,
                 wxbf, wbbf, wcbf, carry, *, nb, tm, tiles_per_seq):
    i = pl.program_id(1)
    t = i % tiles_per_seq
    tc = wxbf.shape[1]
    halo = CONV_HALO

    @pl.when(i == 0)
    def _():
        wxbf[...] = wx_ref[...].astype(BF16)
        wbbf[...] = wb_ref[...].astype(BF16)
        wcbf[...] = wc_ref[...].astype(BF16)

    @pl.when(t == 0)
    def _():
        carry[...] = st_ref[...]

    h = h_ref[...]
    xin = jnp.dot(h, wxbf[...], preferred_element_type=F32)
    gb = jnp.dot(h, wbbf[...], preferred_element_type=F32)
    gc = jnp.dot(h, wcbf[...], preferred_element_type=F32)
    z3 = (gc * xin).reshape(nb, tm, tc)
    ext3 = jnp.concatenate([carry[...], z3], axis=1)
    tail = ext3[:, tm:tm + halo, :]
    ns_ref[...] = tail
    carry[...] = tail
    ext = ext3.reshape(nb * (halo + tm), tc)
    cw = cw_ref[...]
    conv = cw[0:1, :] * pltpu.roll(ext, 2, 0) + cw[1:2, :] * pltpu.roll(ext, 1, 0) + cw[2:3, :] * ext
    conv = conv.reshape(nb, halo + tm, tc)[:, halo:, :].reshape(nb * tm, tc)
    yb_ref[...] = (gb * conv).astype(BF16)


def _conv_mixer(h2, w_in, conv_w, state, nb, tm):
    rows, d = h2.shape
    nseq = state.shape[0]
    tiles_per_seq = (rows // nseq) // tm
    seq_blocks = nseq // nb
    c = CONV_WIDTH
    tc = 512
    cb = c // tc
    base = POOL_WIDTH // tc
    st = jnp.pad(state, ((0, 0), (CONV_HALO - (CONV_K - 1), 0), (0, 0)))
    kern = functools.partial(_conv_kernel, nb=nb, tm=tm, tiles_per_seq=tiles_per_seq)
    yb, ns = pl.pallas_call(
        kern,
        grid=(cb, seq_blocks * tiles_per_seq),
        in_specs=[pl.BlockSpec((nb * tm, d), lambda j, i: (i, 0)),
                  pl.BlockSpec((d, tc), lambda j, i: (0, base + j)),
                  pl.BlockSpec((d, tc), lambda j, i: (0, base + cb + j)),
                  pl.BlockSpec((d, tc), lambda j, i: (0, base + 2 * cb + j)),
                  pl.BlockSpec((CONV_K, tc), lambda j, i: (0, j)),
                  pl.BlockSpec((nb, CONV_HALO, tc), lambda j, i: (i // tiles_per_seq, 0, j))],
        out_specs=[pl.BlockSpec((nb * tm, tc), lambda j, i: (i, j)),
                   pl.BlockSpec((nb, CONV_HALO, tc), lambda j, i: (i // tiles_per_seq, 0, j))],
        out_shape=[jax.ShapeDtypeStruct((rows, c), BF16),
                   jax.ShapeDtypeStruct((nseq, CONV_HALO, c), F32)],
        scratch_shapes=[pltpu.VMEM((d, tc), BF16)] * 3 + [pltpu.VMEM((nb, CONV_HALO, tc), F32)],
        compiler_params=_cparams(2),
        name="conv_mixer",
    )(h2, w_in, w_in, w_in, conv_w, st)
    return yb, ns[:, CONV_HALO - (CONV_K - 1):, :]


def _pack_bf16_pairs(v):
    c = v.shape[1] // 2
    return pltpu.bitcast(pltpu.pack_elementwise([v[:, :c], v[:, c:]], packed_dtype=BF16), U32)


def _store_token_tiles(ref, v):
    rows = v.shape[0]
    for j in range(V7X_SUBLANES):
        ref[pl.ds(j, rows, stride=V7X_SUBLANES), :] = v[:, j * V7X_LANES:(j + 1) * V7X_LANES]


def _load_token_tiles(ref):
    rows = ref.shape[0] // V7X_SUBLANES
    return jnp.concatenate([ref[pl.ds(j, rows, stride=V7X_SUBLANES), :] for j in range(V7X_SUBLANES)],
                           axis=-1)


def _unpack_pairs_f32(w):
    return tuple(pltpu.unpack_elementwise(w, index=k, packed_dtype=BF16, unpacked_dtype=F32) for k in range(2))


def _unpack_bf16_pairs(w):
    lo, hi = _unpack_pairs_f32(w)
    return lo.astype(BF16), hi.astype(BF16)


def _outproj_kernel(ya_ref, yb_ref, x_ref, g1_ref, sc_ref, sh_ref, ng_ref, wo_ref,
                    x1_ref, hp_ref, hpt_ref, wobf, *, tiles_per_seq):
    i = pl.program_id(0)
    seq = i // tiles_per_seq
    half = ya_ref.shape[1]

    @pl.when(i == 0)
    def _():
        wobf[...] = wo_ref[...].astype(BF16)

    rows = x_ref.shape[0]
    sr = min(rows, SUB_TILE)
    subs = [pl.ds(s * sr, sr) for s in range(rows // sr)]
    ys = [jnp.dot(ya_ref[rs, :], wobf[:half, :], preferred_element_type=F32)
          + jnp.dot(yb_ref[rs, :], wobf[half:, :], preferred_element_type=F32) for rs in subs]
    for s, (rs, y) in enumerate(zip(subs, ys)):
        x1 = x_ref[rs, :] + _mod_rows(g1_ref, seq, rs) * y
        x1_ref[rs, :] = x1
        h2 = _rms(x1, ng_ref[...]) * (1.0 + _mod_rows(sc_ref, seq, rs)) + _mod_rows(sh_ref, seq, rs)
        packed = _pack_bf16_pairs(h2)
        hp_ref[rs, :] = packed
        _store_token_tiles(hpt_ref.at[pl.ds(s * sr * V7X_SUBLANES, sr * V7X_SUBLANES), :], packed)


def _outproj(ya, yb, x2, mod, layer, ng, w_out, seq_rows):
    rows_all, d = x2.shape
    half = ya.shape[1]
    rt = ROW_TILE
    row_spec = lambda w: pl.BlockSpec((rt, w), lambda i: (i, 0))
    return pl.pallas_call(
        functools.partial(_outproj_kernel, tiles_per_seq=seq_rows // rt),
        grid=(rows_all // rt,),
        in_specs=[row_spec(half), row_spec(half), row_spec(d),
                  _mod_spec(mod, layer, 2), _mod_spec(mod, layer, 4), _mod_spec(mod, layer, 3),
                  pl.BlockSpec((1, d), lambda i: (0, 0)),
                  pl.BlockSpec((d, d), lambda i: (0, 0), pipeline_mode=pl.Buffered(1))],
        out_specs=[row_spec(d), row_spec(d // 2),
                   pl.BlockSpec((rt * V7X_SUBLANES, V7X_LANES), lambda i: (i, 0))],
        out_shape=[jax.ShapeDtypeStruct((rows_all, d), F32),
                   jax.ShapeDtypeStruct((rows_all, d // 2), U32),
                   jax.ShapeDtypeStruct((rows_all * V7X_SUBLANES, V7X_LANES), U32)],
        scratch_shapes=[pltpu.VMEM((d, d), BF16)],
        compiler_params=_cparams(1),
        name="outproj",
    )(ya, yb, x2, mod, mod, mod, ng.reshape(1, d), w_out)


def _router_kernel(hpp_ref, hps_ref, wr_ref, br_ref, pos_ref, rw_ref, tab_ref,
                   cnt_acc, totals, starts, padded, before_ref, s_all, sel_all, *, nt_p, rows_s):
    ph = pl.program_id(0)
    t = pl.program_id(1)
    last = nt_p
    r = hpp_ref.shape[0]
    half = hpp_ref.shape[1]
    ne = N_EXPERTS
    sub = lax.broadcasted_iota(I32, (ne, V7X_LANES), 0)

    @pl.when(t == 0)
    def _():
        cnt_acc[...] = jnp.zeros_like(cnt_acc)

    @pl.when((ph == 0) & (t == 0))
    def _():
        starts[...] = jnp.zeros_like(starts)
        padded[...] = jnp.zeros_like(padded)

    @pl.when((ph == 1) & (t == 0))
    def _():
        pad = jnp.floor((totals[...] + (MOE_TILE - 1.0)) * (1.0 / MOE_TILE)) * MOE_TILE
        run = pad
        k = 1
        while k < ne:
            run = run + jnp.where(sub >= k, pltpu.roll(run, k, 0), 0.0)
            k *= 2
        padded[...] = pad
        starts[...] = run - pad

    is_s = t == last
    eid = lax.broadcasted_iota(I32, (ne, r), 0)
    n_valid = jnp.where(is_s, rows_s, r)
    tok = lax.broadcasted_iota(I32, (ne, r), 1)

    @pl.when(ph == 0)
    def _():
        w_s = jnp.concatenate([hps_ref[...], jnp.zeros((r - rows_s, half), U32)], axis=0)
        w = jnp.where(is_s, w_s, hpp_ref[...])
        lo, hi = _unpack_bf16_pairs(w)
        wr = wr_ref[...].astype(BF16)
        log_t = (lax.dot_general(wr[:, :half], lo, NT_DIMS, preferred_element_type=F32)
                 + lax.dot_general(wr[:, half:], hi, NT_DIMS, preferred_element_type=F32))

        s = jax.nn.sigmoid(log_t)
        sg = s + br_ref[...]
        within = eid % EXP_PER_GROUP
        grp = eid // EXP_PER_GROUP

        def group_rot(x, k):
            return jnp.where(within + k < EXP_PER_GROUP,
                             pltpu.roll(x, ne - k, 0), pltpu.roll(x, EXP_PER_GROUP - k, 0))

        rank = jnp.zeros((ne, r), I32)
        for k in range(1, EXP_PER_GROUP):
            mate = group_rot(sg, k)
            wrapped = within + k >= EXP_PER_GROUP
            ahead = (mate > sg) | (wrapped & (mate == sg))
            rank = rank + ahead.astype(I32)
        top2 = rank < TOP_K
        kept = jnp.where(top2, sg, 0.0)
        gscore = kept
        for k in range(1, EXP_PER_GROUP):
            gscore = gscore + group_rot(kept, k)
        win = None
        for k in range(1, N_EXPERT_GROUPS):
            other = pltpu.roll(gscore, EXP_PER_GROUP * k, 0)
            beats = (gscore > other) | ((grp < k) & (gscore == other))
            win = beats if win is None else (win & beats)
        picked_now = top2 & win & (tok < n_valid)
        s_all[t] = s
        sel_all[t] = picked_now.astype(F32)

    s = s_all[t]
    selb = sel_all[t]
    sel = selb > 0.5
    cnt_before = cnt_acc[...]
    cnt_new = cnt_before + jnp.sum(selb, axis=1, keepdims=True)
    cnt_acc[...] = cnt_new

    @pl.when((ph == 0) & (t == 0))
    def _():
        src = lax.broadcasted_iota(I32, (r, r), 0)
        dst = lax.broadcasted_iota(I32, (r, r), 1)
        before_ref[...] = (src < dst).astype(BF16)

    @pl.when((ph == 0) & (t == last))
    def _():
        totals[...] = cnt_new

    @pl.when(ph == 1)
    def _():
        picked = jnp.where(sel, s, 0.0)
        wsum = jnp.sum(picked, axis=0, keepdims=True)
        gate = picked / jnp.where(tok[0:1, :] < n_valid, wsum, 1.0)
        ranks = jnp.dot(selb.astype(BF16), before_ref[...], preferred_element_type=F32)
        slot = (starts[...][:, 0:1] + cnt_before[:, 0:1] + ranks).astype(I32)
        e_a = jnp.min(jnp.where(sel, eid, ne), axis=0, keepdims=True)
        e_b = jnp.max(jnp.where(sel, eid, -1), axis=0, keepdims=True)
        is_a = sel & (eid == e_a)
        is_b = sel & (eid == e_b)
        pos_a = jnp.sum(jnp.where(is_a, slot, 0), axis=0, keepdims=True)
        pos_b = jnp.sum(jnp.where(is_b, slot, 0), axis=0, keepdims=True)
        w_a = jnp.sum(jnp.where(is_a, gate, 0.0), axis=0, keepdims=True)
        w_b = jnp.sum(jnp.where(is_b, gate, 0.0), axis=0, keepdims=True)
        pos_ref[0] = jnp.concatenate([pos_a, pos_b], axis=0)
        wmat = jnp.concatenate([w_a, w_b, jnp.zeros((V7X_LANES - 2, r), F32)], axis=0)
        rw_ref[...] = wmat.T

    @pl.when((ph == 1) & (t == last))
    def _():
        ends = starts[...] + padded[...]
        lane = lax.broadcasted_iota(I32, (ne, V7X_LANES), 1)
        tile_start = (lane * MOE_TILE).astype(F32)
        te = jnp.sum((tile_start >= ends).astype(I32), axis=0, keepdims=True)
        valid = te < ne
        last_e = jnp.max(jnp.where(padded[...] > 0.0, sub, 0), axis=0, keepdims=True)
        te = jnp.where(valid, te, last_e)
        n_used = jnp.sum(valid.astype(I32), axis=1, keepdims=True) + jnp.zeros((1, V7X_LANES), I32)
        last_tile = jnp.where(padded[...] > 0.0, ends - MOE_TILE, -1.0).astype(I32)
        last_tile_row = jnp.sum(jnp.where(sub == lane, last_tile, 0), axis=0, keepdims=True)
        later = jnp.min(jnp.where((sub > te) & (padded[...] > 0.0), sub, ne), axis=0, keepdims=True)
        next_e = jnp.where(later < ne, later, -1)
        zero = jnp.zeros((1, V7X_LANES), I32)
        tab_ref[...] = jnp.concatenate([te, valid.astype(I32), last_tile_row, n_used, next_e,
                                        zero, zero, zero], axis=0)


def _router(hp_p, hp_s, w_router, b_router):
    n_p, half = hp_p.shape
    rows_s = hp_s.shape[0]
    r = ROUTER_TILE
    nt_p = n_p // r
    nt = nt_p + 1
    kern = functools.partial(_router_kernel, nt_p=nt_p, rows_s=rows_s)
    pos, rw, tab = pl.pallas_call(
        kern,
        grid=(2, nt),
        in_specs=[pl.BlockSpec((r, half), lambda p, t: (jnp.minimum(t, nt_p - 1) * (1 - p), 0)),
                  pl.BlockSpec((rows_s, half), lambda p, t: (0, 0)),
                  pl.BlockSpec((N_EXPERTS, 2 * half), lambda p, t: (0, 0)),
                  pl.BlockSpec((N_EXPERTS, 1), lambda p, t: (0, 0))],
        out_specs=[pl.BlockSpec((1, TOP_K, r), lambda p, t: (p * t, 0, 0)),
                   pl.BlockSpec((r, V7X_LANES), lambda p, t: (p * t, 0)),
                   pl.BlockSpec((V7X_SUBLANES, V7X_LANES), lambda p, t: (0, 0))],
        out_shape=[jax.ShapeDtypeStruct((nt, TOP_K, r), I32),
                   jax.ShapeDtypeStruct((nt * r, V7X_LANES), F32),
                   jax.ShapeDtypeStruct((V7X_SUBLANES, V7X_LANES), I32)],
        scratch_shapes=[pltpu.VMEM((N_EXPERTS, V7X_LANES), F32)] * 4 + [pltpu.VMEM((r, r), BF16)]
        + [pltpu.VMEM((nt, N_EXPERTS, r), F32)] * 2,
        compiler_params=_cparams(2),
        name="router",
    )(hp_p, hp_s, w_router.T, b_router.reshape(N_EXPERTS, 1))
    return pos.reshape(-1), rw, tab.reshape(-1)


def _pos_index(tok0):
    return (tok0 // ROUTER_TILE) * (TOP_K * ROUTER_TILE) + tok0 % ROUTER_TILE


def _tokens(ref, first, n=1):
    start = pl.multiple_of(first * V7X_SUBLANES, V7X_SUBLANES)
    return ref.at[pl.ds(start, n * V7X_SUBLANES), :]


def _dispatch_kernel(pos_ref, tab_ref, hpp_ref, hps_ref, xs_ref, zbuf, sem, *, n_p_steps):
    i = pl.program_id(0)
    rows = hpp_ref.shape[0] // V7X_SUBLANES

    @pl.when(i == 0)
    def _():
        zbuf[...] = jnp.zeros_like(zbuf)

        def fill(e):
            first = pl.multiple_of(tab_ref[TAB_LAST_TILE * V7X_LANES + e], MOE_TILE)
            return pltpu.make_async_copy(zbuf, _tokens(xs_ref, first, MOE_TILE), sem)

        for e in range(N_EXPERTS):
            @pl.when(tab_ref[TAB_LAST_TILE * V7X_LANES + e] >= 0)
            def _():
                fill(e).start()
        for e in range(N_EXPERTS):
            @pl.when(tab_ref[TAB_LAST_TILE * V7X_LANES + e] >= 0)
            def _():
                fill(e).wait()

        def tail(j):
            first = pl.multiple_of(j * MOE_TILE, MOE_TILE)
            return pltpu.make_async_copy(zbuf, _tokens(xs_ref, first, MOE_TILE), sem)

        def tail_start(j, carry):
            tail(j).start()
            return carry

        def tail_wait(j, carry):
            tail(j).wait()
            return carry

        n_used = tab_ref[TAB_NUSED * V7X_LANES]
        n_tiles = xs_ref.shape[0] // (MOE_TILE * V7X_SUBLANES)
        lax.fori_loop(n_used, n_tiles, tail_start, 0)
        lax.fori_loop(n_used, n_tiles, tail_wait, 0)

    def scatter(src_ref, tok0):
        n = src_ref.shape[0] // V7X_SUBLANES
        base = _pos_index(tok0)

        def row_copy(r, dst):
            return pltpu.make_async_copy(_tokens(src_ref, r), _tokens(xs_ref, dst), sem)

        def issue(r, carry):
            row_copy(r, pos_ref[base + r]).start()
            row_copy(r, pos_ref[base + ROUTER_TILE + r]).start(priority=1)
            return carry

        lax.fori_loop(0, n, issue, 0, unroll=8)
        block = pltpu.make_async_copy(src_ref, _tokens(xs_ref, 0, n), sem)
        for _ in range(TOP_K):
            block.wait()

    @pl.when(i < n_p_steps)
    def _():
        scatter(hpp_ref, i * rows)

    @pl.when(i == n_p_steps)
    def _():
        scatter(hps_ref, n_p_steps * rows)


def _dispatch(pos, tab, hpt_p, hpt_s, n_rows_sorted):
    sub = V7X_SUBLANES
    n_p_steps = hpt_p.shape[0] // (MATMUL_TILE * sub)
    kern = functools.partial(_dispatch_kernel, n_p_steps=n_p_steps)
    blk = (MATMUL_TILE * sub, V7X_LANES)
    return pl.pallas_call(
        kern,
        grid_spec=pltpu.PrefetchScalarGridSpec(
            num_scalar_prefetch=2,
            grid=(n_p_steps + 1,),
            in_specs=[pl.BlockSpec(blk, lambda i, p, t: (jnp.minimum(i, n_p_steps - 1), 0)),
                      pl.BlockSpec(hpt_s.shape, lambda i, p, t: (0, 0))],
            out_specs=pl.BlockSpec(memory_space=pl.ANY),
            scratch_shapes=[pltpu.VMEM((MOE_TILE * sub, V7X_LANES), U32), pltpu.SemaphoreType.DMA(())]),
        out_shape=jax.ShapeDtypeStruct((n_rows_sorted * sub, V7X_LANES), U32),
        compiler_params=_cparams(1),
        name="moe_dispatch",
    )(pos, tab, hpt_p, hpt_s)


def _experts_kernel(tab_ref, xs_ref, wg_hbm, wu_hbm, wd_hbm, ys_ref,
                    wg32, wu32, wd32, wgbf, wubf, wdbf, slot_ref, sems, *, layer):
    i = pl.program_id(0)
    expert = tab_ref[TAB_EXPERT * V7X_LANES + i]
    prev = tab_ref[TAB_EXPERT * V7X_LANES + jnp.maximum(i - 1, 0)]
    upcoming = tab_ref[TAB_NEXT * V7X_LANES + i]
    changed = (i == 0) | (expert != prev)
    half = V7X_SUBLANES * V7X_LANES

    def weight_copies(e, slot):
        return (pltpu.make_async_copy(wg_hbm.at[layer, e], wg32.at[slot], sems.at[0, slot]),
                pltpu.make_async_copy(wu_hbm.at[layer, e], wu32.at[slot], sems.at[1, slot]),
                pltpu.make_async_copy(wd_hbm.at[layer, e], wd32.at[slot], sems.at[2, slot]))

    @pl.when(i == 0)
    def _():
        slot_ref[0] = 0
        for cp in weight_copies(expert, 0):
            cp.start()

    @pl.when(changed & (i > 0))
    def _():
        slot_ref[0] = 1 - slot_ref[0]

    def mlp(wg, wu, wd):
        lo, hi = _unpack_bf16_pairs(_load_token_tiles(xs_ref))
        a = (jnp.dot(lo, wg[:half, :], preferred_element_type=F32)
             + jnp.dot(hi, wg[half:, :], preferred_element_type=F32))
        b = (jnp.dot(lo, wu[:half, :], preferred_element_type=F32)
             + jnp.dot(hi, wu[half:, :], preferred_element_type=F32))
        hid = (a * jax.nn.sigmoid(a)) * b
        y = jnp.dot(hid.astype(BF16), wd, preferred_element_type=F32)
        _store_token_tiles(ys_ref, _pack_bf16_pairs(y))

    for slot in range(2):
        @pl.when(changed & (slot_ref[0] == slot))
        def _():
            for cp in weight_copies(expert, slot):
                cp.wait()

            @pl.when(upcoming >= 0)
            def _():
                for cp in weight_copies(upcoming, 1 - slot):
                    cp.start(priority=1)

            wg = wg32[slot].astype(BF16)
            wu = wu32[slot].astype(BF16)
            wd = wd32[slot].astype(BF16)
            wgbf[...] = wg
            wubf[...] = wu
            wdbf[...] = wd
            mlp(wg, wu, wd)

    valid = tab_ref[TAB_VALID * V7X_LANES + i] > 0

    @pl.when(valid & jnp.logical_not(changed))
    def _():
        mlp(wgbf[...], wubf[...], wdbf[...])

    @pl.when(jnp.logical_not(valid))
    def _():
        ys_ref[...] = jnp.zeros_like(ys_ref)


def _experts(tab, xs, w_gate, w_up, w_down, layer):
    sub = V7X_SUBLANES
    _, _, d, f = w_gate.shape
    nt = xs.shape[0] // (MOE_TILE * sub)
    assert nt <= V7X_LANES and d == 2 * sub * V7X_LANES
    blk = (MOE_TILE * sub, V7X_LANES)

    def tile(i, tab_ref):
        return jnp.minimum(i, tab_ref[TAB_NUSED * V7X_LANES] - 1)

    hbm = pl.BlockSpec(memory_space=pl.ANY)
    return pl.pallas_call(
        functools.partial(_experts_kernel, layer=layer),
        grid_spec=pltpu.PrefetchScalarGridSpec(
            num_scalar_prefetch=1,
            grid=(nt,),
            in_specs=[pl.BlockSpec(blk, lambda i, t: (tile(i, t), 0)), hbm, hbm, hbm],
            out_specs=pl.BlockSpec(blk, lambda i, t: (i, 0)),
            scratch_shapes=[pltpu.VMEM((2, d, f), F32), pltpu.VMEM((2, d, f), F32), pltpu.VMEM((2, f, d), F32),
                            pltpu.VMEM((d, f), BF16), pltpu.VMEM((d, f), BF16), pltpu.VMEM((f, d), BF16),
                            pltpu.SMEM((1,), I32), pltpu.SemaphoreType.DMA((3, 2))]),
        out_shape=jax.ShapeDtypeStruct(xs.shape, U32),
        compiler_params=_cparams(1),
        name="moe_experts",
    )(tab, xs, w_gate, w_up, w_down)


def _combine_kernel(pos_ref, ys_ref, x1_ref, rw_ref, g2_ref, ng_ref, sc_ref, sh_ref, *rest,
                    tok0, tiles_per_seq, n_steps, final):
    if final:
        x2_ref, buf, sems = rest
        hn_ref = None
    else:
        x2_ref, hn_ref, buf, sems = rest
    i = pl.program_id(0)
    seq = i // tiles_per_seq
    rows = x1_ref.shape[0]

    def gather(step, slot):
        base = _pos_index(tok0 + step * rows)

        def issue(r, carry):
            for k in range(TOP_K):
                src = pos_ref[base + k * ROUTER_TILE + r]
                pltpu.make_async_copy(_tokens(ys_ref, src), _tokens(buf.at[slot, k], r),
                                      sems.at[slot]).start(priority=k)
            return carry

        lax.fori_loop(0, rows, issue, 0, unroll=8)

    @pl.when(i == 0)
    def _():
        gather(0, 0)

    @pl.when(i + 1 < n_steps)
    def _():
        gather(i + 1, (i + 1) % 2)

    slot = i % 2
    for k in range(TOP_K):
        pltpu.make_async_copy(_tokens(ys_ref, 0, rows), buf.at[slot, k], sems.at[slot]).wait()

    rw = rw_ref[...]
    lo_a, hi_a = _unpack_pairs_f32(_load_token_tiles(buf.at[slot, 0]))
    lo_b, hi_b = _unpack_pairs_f32(_load_token_tiles(buf.at[slot, 1]))
    w_a = rw[:, 0:1]
    w_b = rw[:, 1:2]
    moe = jnp.concatenate([w_a * lo_a + w_b * lo_b, w_a * hi_a + w_b * hi_b], axis=-1)
    x2 = x1_ref[...] + _mod_rows(g2_ref, seq) * moe
    if final:
        x2_ref[...] = _rms(x2, ng_ref[...])
    else:
        x2_ref[...] = x2
        hn_ref[...] = (_rms(x2, ng_ref[...]) * (1.0 + _mod_rows(sc_ref, seq))
                       + _mod_rows(sh_ref, seq)).astype(BF16)


def _combine(pos, ys, x1, rw, tok0, mod, layer, ng, seq_rows, final):
    n_tok, d = x1.shape
    rt = ROW_TILE
    assert tok0 % rt == 0 and ROUTER_TILE % rt == 0
    kern = functools.partial(_combine_kernel, tok0=tok0, tiles_per_seq=seq_rows // rt,
                             n_steps=n_tok // rt, final=final)
    rw_off = tok0 // rt
    row_spec = lambda w: pl.BlockSpec((rt, w), lambda i, p: (i, 0))
    out_shape = [jax.ShapeDtypeStruct((n_tok, d), F32)]
    out_specs = [row_spec(d)]
    if not final:
        out_shape.append(jax.ShapeDtypeStruct((n_tok, d), BF16))
        out_specs.append(row_spec(d))
    nxt = min(layer + 1, mod.shape[0] - 1)
    return pl.pallas_call(
        kern,
        grid_spec=pltpu.PrefetchScalarGridSpec(
            num_scalar_prefetch=1,
            grid=(n_tok // rt,),
            in_specs=[pl.BlockSpec(memory_space=pl.ANY), row_spec(d),
                      pl.BlockSpec((rt, V7X_LANES), lambda i, p: (rw_off + i, 0)),
                      _mod_spec(mod, layer, 5), pl.BlockSpec((1, d), lambda i, p: (0, 0)),
                      _mod_spec(mod, nxt, 1), _mod_spec(mod, nxt, 0)],
            out_specs=out_specs,
            scratch_shapes=[pltpu.VMEM((2, TOP_K, rt * V7X_SUBLANES, V7X_LANES), U32),
                            pltpu.SemaphoreType.DMA((2,))]),
        out_shape=out_shape,
        compiler_params=_cparams(1),
        name="moe_combine_final" if final else "moe_combine",
    )(pos, ys, x1, rw, mod, ng.reshape(1, d), mod, mod)


def _moe(out_p, out_s, mod_p, mod_s, layer, ng, seq_rows_p, w_router, b_router, w_gate, w_up, w_down, final):
    x1_p, hp_p, hpt_p = out_p
    x1_s, hp_s, hpt_s = out_s
    n_p = x1_p.shape[0]
    n_tok = n_p + x1_s.shape[0]
    max_rows = TOP_K * n_tok + N_EXPERTS * (MOE_TILE - 1)
    n_rows_sorted = -(-max_rows // MOE_TILE) * MOE_TILE
    pos, rw, tab = _router(hp_p, hp_s, w_router, b_router)
    xs = _dispatch(pos, tab, hpt_p, hpt_s, n_rows_sorted)
    ys = _experts(tab, xs, w_gate, w_up, w_down, layer)
    out_p = _combine(pos, ys, x1_p, rw, 0, mod_p, layer, ng, seq_rows_p, final)
    out_s = _combine(pos, ys, x1_s, rw, n_p, mod_s, layer, ng, x1_s.shape[0], final)
    return out_p, out_s


def _gmlp_kernel(h_ref, w_ref, lg_ref, lb_ref, ws_ref, bs_ref, yc_ref, *rest, ell, blk, emit_v):
    if emit_v:
        gv_ref, wbf, wsbf = rest
    else:
        wbf, wsbf = rest
    i = pl.program_id(0)
    rows = h_ref.shape[0]
    c = GM_WIDTH

    @pl.when(i == 0)
    def _():
        wbf[...] = w_ref[...].astype(BF16)
        r = lax.broadcasted_iota(I32, (ell, ell), 0)
        s = lax.broadcasted_iota(I32, (ell, ell), 1)
        keep = (r >= s) & ((r // blk) == (s // blk))
        rsel = (lax.broadcasted_iota(I32, (ell, CHUNK), 0) % blk
                == lax.broadcasted_iota(I32, (ell, CHUNK), 1)).astype(BF16)
        csel = (lax.broadcasted_iota(I32, (CHUNK, ell), 1) % blk
                == lax.broadcasted_iota(I32, (CHUNK, ell), 0)).astype(BF16)
        for g in range(GM_GROUPS):
            wchunk = ws_ref[g].astype(BF16)
            if blk == ell:
                full = wchunk
            else:
                rowsp = jnp.dot(rsel, wchunk, preferred_element_type=F32).astype(BF16)
                full = jnp.dot(rowsp, csel, preferred_element_type=F32).astype(BF16)
            wsbf[g] = jnp.where(keep, full, jnp.zeros_like(full))

    uv = jnp.dot(h_ref[...], wbf[...], preferred_element_type=F32)
    u = uv[:, :c]
    v = uv[:, c:]
    vc = v - jnp.mean(v, axis=-1, keepdims=True)
    vn = vc * lax.rsqrt(jnp.mean(vc * vc, axis=-1, keepdims=True) + EPS) * lg_ref[...] + lb_ref[...]
    if emit_v:
        gv_ref[...] = vn
    vb = vn.astype(BF16)
    bs = bs_ref[...]
    for ch in range(rows // ell):
        rs = slice(ch * ell, (ch + 1) * ell)
        outs = []
        for g in range(GM_GROUPS):
            cs = slice(g * GM_GROUP, (g + 1) * GM_GROUP)
            mixed = jnp.dot(wsbf[g], vb[rs, cs], preferred_element_type=F32)
            mixed = (mixed.reshape(ell // blk, blk, GM_GROUP) + bs[:blk, g:g + 1][None]).reshape(ell, GM_GROUP)
            outs.append(u[rs, cs] * mixed)
        yc_ref[rs, :] = jnp.concatenate(outs, axis=-1).astype(BF16)


def _gmlp_mixer(h2, w_in, ln_g, ln_b, ws, bs_t, ell, blk, emit_v):
    rows, d = h2.shape
    c = GM_WIDTH
    kern = functools.partial(_gmlp_kernel, ell=ell, blk=blk, emit_v=emit_v)
    rt = min(rows, MATMUL_TILE)
    out_specs = [pl.BlockSpec((rt, c), lambda i: (i, 0))]
    out_shape = [jax.ShapeDtypeStruct((rows, c), BF16)]
    if emit_v:
        out_specs.append(pl.BlockSpec((rt, c), lambda i: (i, 0)))
        out_shape.append(jax.ShapeDtypeStruct((rows, c), F32))
    return pl.pallas_call(
        kern,
        grid=(rows // rt,),
        in_specs=[pl.BlockSpec((rt, d), lambda i: (i, 0)),
                  pl.BlockSpec((d, 2 * c), lambda i: (0, 0), pipeline_mode=pl.Buffered(1)),
                  pl.BlockSpec((1, c), lambda i: (0, 0)),
                  pl.BlockSpec((1, c), lambda i: (0, 0)),
                  pl.BlockSpec((GM_GROUPS, CHUNK, CHUNK), lambda i: (0, 0, 0)),
                  pl.BlockSpec((CHUNK, GM_GROUPS), lambda i: (0, 0))],
        out_specs=out_specs,
        out_shape=out_shape,
        scratch_shapes=[pltpu.VMEM((d, 2 * c), BF16), pltpu.VMEM((GM_GROUPS, ell, ell), BF16)],
        compiler_params=_cparams(1),
        name="gmlp_mixer",
    )(h2, w_in, ln_g.reshape(1, c), ln_b.reshape(1, c), ws, bs_t)


PAIR_W = 2 * HEAD_DIM
PAIRS_PER_KV = N_HEADS // N_KV // 2
NT_DIMS = (((1,), (1,)), ((), ()))
SCORE_SCALE = HEAD_DIM ** -0.5
assert float(np.log2(SCORE_SCALE)).is_integer()


def _swa_project(i, h_ref, wq_ref, wkv_ref, wbf):
    nq = N_HEADS * HEAD_DIM

    @pl.when(i == 0)
    def _():
        wbf[:, :nq] = wq_ref[...].astype(BF16)
        wbf[:, nq:] = wkv_ref[...].astype(BF16)

    return jnp.dot(h_ref[...], wbf[...], preferred_element_type=F32)


def _pair_block_diag(a, a_swapped, hk, axis):
    dim_axis = 1 - axis
    low = lax.broadcasted_iota(I32, a.shape, dim_axis) < HEAD_DIM
    lo, hi = (a, a_swapped) if hk == 0 else (a_swapped, a)
    return jnp.concatenate([jnp.where(low, lo, 0.0), jnp.where(low, 0.0, hi)], axis=axis).astype(BF16)


def _stack_pairs(qkv, rs, hk, scale=None):
    p0 = hk * PAIRS_PER_KV
    q = jnp.concatenate([qkv[rs, (p0 + pp) * PAIR_W:(p0 + pp + 1) * PAIR_W]
                         for pp in range(PAIRS_PER_KV)], axis=0)
    return (q if scale is None else q * scale).astype(BF16)


def _swa_cached_kernel(h_ref, wq_ref, wkv_ref, kp_ref, vp_ref, bias_ref, sink_ref, yd_ref, k_ref, v_ref,
                       wbf, *, tq):
    i = pl.program_id(0)
    rows = h_ref.shape[0]
    nq = N_HEADS * HEAD_DIM
    nkv = N_KV * HEAD_DIM
    n_blocks = rows // tq
    qkv = _swa_project(i, h_ref, wq_ref, wkv_ref, wbf)
    k_new = qkv[:, nq:nq + nkv]
    v_new = qkv[:, nq + nkv:]
    k_ref[...] = k_new
    v_ref[...] = v_new
    pad = jnp.zeros((WINDOW - tq, nkv), F32)

    scores, vbds = [], []
    for blk in range(n_blocks):
        rs = slice(blk * tq, (blk + 1) * tq)
        kcat = jnp.concatenate([kp_ref[blk], k_new[rs], pad], axis=0)
        vcat = jnp.concatenate([vp_ref[blk], v_new[rs], pad], axis=0)
        kswap = pltpu.roll(kcat, HEAD_DIM, 1)
        vswap = pltpu.roll(vcat, HEAD_DIM, 1)
        per_head = []
        for hk in range(N_KV):
            kbd = _pair_block_diag(kcat, kswap, hk, 0)
            vbds.append(_pair_block_diag(vcat, vswap, hk, 0))
            s4 = lax.dot_general(_stack_pairs(qkv, rs, hk), kbd, NT_DIMS,
                                 preferred_element_type=F32) * (HEAD_DIM ** -0.5)
            for pp in range(PAIRS_PER_KV):
                for sub in range(2):
                    per_head.append(s4[pp * tq:(pp + 1) * tq, sub * 2 * WINDOW:(sub + 1) * 2 * WINDOW])
        scores.append(jnp.concatenate(per_head, axis=0))

    s_all = jnp.stack(scores, axis=0) + bias_ref[...][None]
    sink = sink_ref[...][None]
    m = jnp.maximum(jnp.max(s_all, axis=-1, keepdims=True), sink)
    pr = jnp.exp(s_all - m)
    pr = pr / (jnp.sum(pr, axis=-1, keepdims=True) + jnp.exp(sink - m))

    for blk in range(n_blocks):
        outs = []
        for hk in range(N_KV):
            p4 = []
            for pp in range(PAIRS_PER_KV):
                h0 = 2 * (hk * PAIRS_PER_KV + pp)
                p4.append(jnp.concatenate([pr[blk, h0 * tq:(h0 + 1) * tq, :],
                                           pr[blk, (h0 + 1) * tq:(h0 + 2) * tq, :]], axis=-1))
            o4 = jnp.dot(jnp.concatenate(p4, axis=0).astype(BF16), vbds[blk * N_KV + hk],
                         preferred_element_type=F32)
            outs.extend(o4[pp * tq:(pp + 1) * tq, :] for pp in range(PAIRS_PER_KV))
        yd_ref[blk * tq:(blk + 1) * tq, :] = jnp.concatenate(outs, axis=-1).astype(BF16)


def _swa_stream_kernel(h_ref, wq_ref, wkv_ref, bias_ref, sink_ref, yd_ref, k_ref, v_ref,
                       wbf, kprev, vprev_t, *, blocks_per_seq):
    i = pl.program_id(0)
    rows = h_ref.shape[0]
    nq = N_HEADS * HEAD_DIM
    nkv = N_KV * HEAD_DIM
    tq = WINDOW
    n_blocks = rows // tq

    @pl.when(i == 0)
    def _():
        kprev[...] = jnp.zeros_like(kprev)
        vprev_t[...] = jnp.zeros_like(vprev_t)

    qkv = _swa_project(i, h_ref, wq_ref, wkv_ref, wbf)
    k_new = qkv[:, nq:nq + nkv]
    v_new = qkv[:, nq + nkv:]
    k_ref[...] = k_new
    v_ref[...] = v_new
    v_new_t = v_new.T
    lanes = PAIRS_PER_KV * tq

    for blk in range(n_blocks):
        rs = slice(blk * tq, (blk + 1) * tq)
        first = ((i * n_blocks + blk) % blocks_per_seq == 0).astype(I32)
        k_cur = k_new[rs]
        v_cur_t = v_new_t[:, rs]
        kcat = jnp.concatenate([kprev[...], k_cur], axis=0)
        vcat_t = jnp.concatenate([vprev_t[...], v_cur_t], axis=1)
        kprev[...] = k_cur
        vprev_t[...] = v_cur_t
        kswap = pltpu.roll(kcat, HEAD_DIM, 1)
        vswap_t = pltpu.roll(vcat_t, HEAD_DIM, 0)
        outs = []
        for hk in range(N_KV):
            kbd = _pair_block_diag(kcat, kswap, hk, 0)
            vbd_t = _pair_block_diag(vcat_t, vswap_t, hk, 1)
            st = lax.dot_general(kbd, _stack_pairs(qkv, rs, hk, SCORE_SCALE), NT_DIMS,
                                 preferred_element_type=F32)
            s3 = st.reshape(2, 2 * WINDOW, lanes) + bias_ref[first, hk]
            sink = sink_ref[hk]
            m = jnp.maximum(jnp.max(s3, axis=1, keepdims=True), sink)
            pr = jnp.exp(s3 - m)
            inv = 1.0 / (jnp.sum(pr, axis=1, keepdims=True) + jnp.exp(sink - m))
            o_t = jnp.dot(vbd_t, pr.reshape(4 * WINDOW, lanes).astype(BF16),
                          preferred_element_type=F32)
            norm = jnp.concatenate([jnp.broadcast_to(inv[sub], (HEAD_DIM, lanes)) for sub in range(2)], axis=0)
            o4 = (o_t * norm).T
            outs.extend(o4[pp * tq:(pp + 1) * tq, :] for pp in range(PAIRS_PER_KV))
        yd_ref[rs, :] = jnp.concatenate(outs, axis=-1).astype(BF16)


def _swa_weight_specs(w_in, d):
    nq = N_HEADS * HEAD_DIM
    nkv = N_KV * HEAD_DIM
    nw = nq + 2 * nkv
    q_blk = (w_in.shape[1] - nw) // nq
    kv_blk = (w_in.shape[1] - 2 * nkv) // (2 * nkv)
    assert q_blk * nq + nw == w_in.shape[1] and kv_blk * 2 * nkv + 2 * nkv == w_in.shape[1]
    return [pl.BlockSpec((d, nq), lambda i: (0, q_blk)), pl.BlockSpec((d, 2 * nkv), lambda i: (0, kv_blk))]


def _swa_outputs(rows, rt):
    nq = N_HEADS * HEAD_DIM
    nkv = N_KV * HEAD_DIM
    specs = [pl.BlockSpec((rt, nq), lambda i: (i, 0)),
             pl.BlockSpec((rt, nkv), lambda i: (i, 0)),
             pl.BlockSpec((rt, nkv), lambda i: (i, 0))]
    shapes = [jax.ShapeDtypeStruct((rows, nq), BF16),
              jax.ShapeDtypeStruct((rows, nkv), F32),
              jax.ShapeDtypeStruct((rows, nkv), F32)]
    return specs, shapes


def _swa_cached_mixer(h2, w_in, k_cache, v_cache, bias, sinks, tq):
    rows, d = h2.shape
    nkv = N_KV * HEAD_DIM
    nw = N_HEADS * HEAD_DIM + 2 * nkv
    n_blocks = ROW_TILE // tq
    cache_spec = pl.BlockSpec((n_blocks, WINDOW, nkv), lambda i: (i, 0, 0))
    out_specs, out_shape = _swa_outputs(rows, ROW_TILE)
    return pl.pallas_call(
        functools.partial(_swa_cached_kernel, tq=tq),
        grid=(rows // ROW_TILE,),
        in_specs=[pl.BlockSpec((ROW_TILE, d), lambda i: (i, 0))] + _swa_weight_specs(w_in, d)
        + [cache_spec, cache_spec,
           pl.BlockSpec((N_HEADS * tq, 2 * WINDOW), lambda i: (0, 0)),
           pl.BlockSpec((N_HEADS * tq, 1), lambda i: (0, 0))],
        out_specs=out_specs,
        out_shape=out_shape,
        scratch_shapes=[pltpu.VMEM((d, nw), BF16)],
        compiler_params=_cparams(1),
        name="swa_cached",
    )(h2, w_in, w_in, k_cache, v_cache, bias, sinks)


def _swa_stream_mixer(h2, w_in, bias_t, sinks_t, blocks_per_seq):
    rows, d = h2.shape
    nkv = N_KV * HEAD_DIM
    nw = N_HEADS * HEAD_DIM + 2 * nkv
    lanes = PAIRS_PER_KV * WINDOW
    rt = min(rows, MATMUL_TILE)
    out_specs, out_shape = _swa_outputs(rows, rt)
    return pl.pallas_call(
        functools.partial(_swa_stream_kernel, blocks_per_seq=blocks_per_seq),
        grid=(rows // rt,),
        in_specs=[pl.BlockSpec((rt, d), lambda i: (i, 0))] + _swa_weight_specs(w_in, d)
        + [pl.BlockSpec((2, N_KV, 2, 2 * WINDOW, lanes), lambda i: (0, 0, 0, 0, 0)),
           pl.BlockSpec((N_KV, 2, 1, lanes), lambda i: (0, 0, 0, 0))],
        out_specs=out_specs,
        out_shape=out_shape,
        scratch_shapes=[pltpu.VMEM((d, nw), BF16), pltpu.VMEM((WINDOW, nkv), F32),
                        pltpu.VMEM((nkv, WINDOW), F32)],
        compiler_params=_cparams(1),
        name="swa_stream",
    )(h2, w_in, w_in, bias_t, sinks_t)


def _t5_bucket(dist):
    max_exact = N_BUCKETS // 2
    dd = np.maximum(dist, 1)
    large = max_exact + (np.log(dd / max_exact) / np.log(WINDOW / max_exact)
                         * (N_BUCKETS - max_exact)).astype(np.int64)
    large = np.minimum(large, N_BUCKETS - 1)
    return np.where(dist < max_exact, dist, large).astype(np.int32)


def _attention_bias(rel_bias):
    by_dist = jnp.take(rel_bias.astype(F32), _t5_bucket(np.arange(WINDOW)), axis=0).T
    neg = jnp.full((N_HEADS, WINDOW), NEG_INF, F32)
    line = jnp.concatenate([neg, by_dist[:, ::-1], neg[:, :WINDOW - 1]], axis=1)
    return jnp.stack([line[:, WINDOW - 1 - q:3 * WINDOW - 1 - q] for q in range(WINDOW)], axis=1)


def kernel(x_prompt, x_sample, state_pool, state_conv, cache_swa_k, cache_swa_v, c_prompt, c_sample, w_ada, b_ada, norm_g, final_norm_g, w_in_even, w_out_even, w_pool, pool_scale, conv_w, w_in_odd, w_out_odd, gm_norm_g, gm_norm_b, gm_w_s, gm_b_s, attn_sinks, rel_bias, w_router, b_router, w_gate, w_up, w_down):
    d = D_MODEL
    bp, tp, _ = x_prompt.shape
    bs, ts, _ = x_sample.shape
    rows_s = bs * ts
    assert rows_s == ROW_TILE and tp % ROUTER_TILE == 0 and PAST_LEN % CHUNK == 0
    assert bp <= V7X_SUBLANES and CHUNK % ts == 0

    n_c = bp + bs
    c_pad = (-n_c) % V7X_SUBLANES
    c_all = jnp.concatenate([c_prompt, c_sample, jnp.zeros((c_pad, d), F32)], axis=0)
    mod_p = _adaln(c_all, w_ada, b_ada)
    mod_s = jnp.repeat(mod_p[:, bp:bp + bs], ts, axis=1)

    xp = x_prompt.reshape(bp * tp, d)
    xs_ = x_sample.reshape(rows_s, d)
    w_in0, w_in1 = w_in_even[0], w_in_odd[0]

    zero_pool = jnp.zeros((bp, POOL_STATE, POOL_WIDTH), F32)
    zero_conv = jnp.zeros((bp, CONV_K - 1, CONV_WIDTH), F32)
    hp0, ya_p, pool_p = _pool_mixer(xp, norm_g[0, 0], mod_p, 0, w_in0, w_pool[0], pool_scale[0],
                                    zero_pool, 1, MATMUL_TILE, 0)
    hs0, ya_s, pool_s = _pool_mixer(xs_, norm_g[0, 0], mod_s, 0, w_in0, w_pool[0], pool_scale[0],
                                    state_pool[0], bs, ts, PAST_LEN)
    yb_p, conv_p = _conv_mixer(hp0, w_in0, conv_w[0], zero_conv, 1, MATMUL_TILE)
    yb_s, conv_s = _conv_mixer(hs0, w_in0, conv_w[0], state_conv[0], bs, ts)
    out_p = _outproj(ya_p, yb_p, xp, mod_p, 0, norm_g[0, 1], w_out_even[0], tp)
    out_s = _outproj(ya_s, yb_s, xs_, mod_s, 0, norm_g[0, 1], w_out_even[0], rows_s)
    (x2p, h1p), (x2s, h1s) = _moe(out_p, out_s, mod_p, mod_s, 0, norm_g[1, 0], tp,
                                  w_router, b_router, w_gate, w_up, w_down, final=False)

    bs_t = gm_b_s[0].T
    (yc_p,) = _gmlp_mixer(h1p, w_in1, gm_norm_g[0], gm_norm_b[0], gm_w_s[0], bs_t, CHUNK, CHUNK, False)
    yc_s, gv_s = _gmlp_mixer(h1s, w_in1, gm_norm_g[0], gm_norm_b[0], gm_w_s[0], bs_t, rows_s, ts, True)
    bias = _attention_bias(rel_bias)
    nkv = N_KV * HEAD_DIM
    bias_t = jnp.transpose(bias.reshape(N_KV, PAIRS_PER_KV, 2, WINDOW, 2 * WINDOW), (0, 2, 4, 1, 3))
    bias_t = bias_t.reshape(N_KV, 2, 2 * WINDOW, PAIRS_PER_KV * WINDOW)
    before_start = (np.arange(2 * WINDOW) < WINDOW)[None, None, :, None]
    bias_t = jnp.stack([bias_t, jnp.where(before_start, NEG_INF, bias_t)], axis=0)
    sinks_t = jnp.transpose(attn_sinks[0].reshape(N_KV, PAIRS_PER_KV, 2), (0, 2, 1))
    sinks_t = jnp.repeat(sinks_t, WINDOW, axis=-1).reshape(N_KV, 2, 1, PAIRS_PER_KV * WINDOW)
    yd_p, k_p, v_p = _swa_stream_mixer(h1p, w_in1, bias_t, sinks_t, tp // WINDOW)
    yd_s, k_s, v_s = _swa_cached_mixer(h1s, w_in1, cache_swa_k[0].reshape(bs, WINDOW, nkv),
                                       cache_swa_v[0].reshape(bs, WINDOW, nkv),
                                       bias[:, :ts, :].reshape(N_HEADS * ts, 2 * WINDOW),
                                       jnp.repeat(attn_sinks[0], ts).reshape(-1, 1), ts)
    out_p = _outproj(yc_p, yd_p, x2p, mod_p, 1, norm_g[1, 1], w_out_odd[0], tp)
    out_s = _outproj(yc_s, yd_s, x2s, mod_s, 1, norm_g[1, 1], w_out_odd[0], rows_s)
    (yp,), (ys_out,) = _moe(out_p, out_s, mod_p, mod_s, 1, final_norm_g, tp,
                            w_router, b_router, w_gate, w_up, w_down, final=True)

    k_p4 = k_p.reshape(bp, tp, N_KV, HEAD_DIM)[:, -WINDOW:]
    v_p4 = v_p.reshape(bp, tp, N_KV, HEAD_DIM)[:, -WINDOW:]
    k_s4 = jnp.concatenate([cache_swa_k[0], k_s.reshape(bs, ts, N_KV, HEAD_DIM)], axis=1)[:, -WINDOW:]
    v_s4 = jnp.concatenate([cache_swa_v[0], v_s.reshape(bs, ts, N_KV, HEAD_DIM)], axis=1)[:, -WINDOW:]
    return (yp.reshape(bp, tp, d), ys_out.reshape(bs, ts, d),
            pool_p[None], pool_s[None], conv_p[None], conv_s[None],
            k_p4[None], k_s4[None], v_p4[None], v_s4[None],
            gv_s.reshape(bs, ts, GM_WIDTH)[None])
```

```python
import functools

import numpy as np
import jax
import jax.numpy as jnp
from jax import lax
from jax.experimental import pallas as pl
from jax.experimental.pallas import tpu as pltpu

F32 = jnp.float32
BF16 = jnp.bfloat16
I32 = jnp.int32
U32 = jnp.uint32

D_MODEL = 2048
POOL_WINDOWS = (2, 4, 8, 16)
POOL_WIDTH = 1024
POOL_GROUP = 256
POOL_STATE = 15
CONV_WIDTH = 1024
CONV_K = 3
GM_WIDTH = 1024
GM_GROUPS = 8
GM_GROUP = 128
CHUNK = 128
HEAD_DIM = 64
N_HEADS = 16
N_KV = 2
WINDOW = 128
N_BUCKETS = 32
N_EXPERTS = 16
N_EXPERT_GROUPS = 4
EXP_PER_GROUP = 4
TOP_K = 2
EPS = 1e-6
NEG_INF = -1e30
PAST_LEN = 16384

V7X_SUBLANES = 8
V7X_LANES = 128
VMEM_LIMIT = 56 * 1024 * 1024

ROW_TILE = 256
MATMUL_TILE = 512
ROUTER_TILE = 1024
POOL_HALO = 16
CONV_HALO = 8
MOE_TILE = 256
TAB_EXPERT, TAB_VALID, TAB_LAST_TILE, TAB_NUSED, TAB_NEXT = 0, 1, 2, 3, 4


def _cparams(n_axes):
    return pltpu.CompilerParams(dimension_semantics=("arbitrary",) * n_axes,
                                vmem_limit_bytes=VMEM_LIMIT)


def _rms(x, g):
    return x * lax.rsqrt(jnp.mean(x * x, axis=-1, keepdims=True) + EPS) * g


def _mod_spec(mod, layer, part):
    nrow = ROW_TILE if mod.shape[1] == ROW_TILE else V7X_SUBLANES
    return pl.BlockSpec((1, nrow, D_MODEL), lambda *_: (layer, 0, part))


def _mod_rows(m_ref, seq):
    if m_ref.shape[1] == V7X_SUBLANES:
        return m_ref[0, pl.ds(seq, 1), :]
    return m_ref[0]


def _adaln_kernel(c_ref, w_ref, b_ref, o_ref):
    c = c_ref[...]
    a = (c * jax.nn.sigmoid(c)).astype(BF16)
    o_ref[0] = jnp.dot(a, w_ref[0].astype(BF16), preferred_element_type=F32) + b_ref[0]


def _adaln(c_all, w_ada, b_ada):
    depth, d, n6 = w_ada.shape
    m = c_all.shape[0]
    tn = 1024
    return pl.pallas_call(
        _adaln_kernel,
        grid=(depth, n6 // tn),
        in_specs=[pl.BlockSpec((m, d), lambda l, j: (0, 0)),
                  pl.BlockSpec((1, d, tn), lambda l, j: (l, 0, j)),
                  pl.BlockSpec((1, 1, tn), lambda l, j: (l, 0, j))],
        out_specs=pl.BlockSpec((1, m, tn), lambda l, j: (l, 0, j)),
        out_shape=jax.ShapeDtypeStruct((depth, m, n6), F32),
        compiler_params=_cparams(2),
        name="adaln",
    )(c_all, w_ada, b_ada.reshape(depth, 1, n6))


def _pool_kernel(x_ref, g_ref, sc_ref, sh_ref, w_ref, wp_ref, ps_ref, st_ref, h_ref, ya_ref, ns_ref,
                 wbf, wpbf, carry, *, nb, tm, tiles_per_seq, start):
    i = pl.program_id(0)
    t = i % tiles_per_seq
    seq = i // tiles_per_seq
    c = POOL_WIDTH
    halo = POOL_HALO

    @pl.when(i == 0)
    def _():
        wbf[...] = w_ref[...].astype(BF16)
        wpbf[...] = wp_ref[...].astype(BF16)

    @pl.when(t == 0)
    def _():
        carry[...] = st_ref[...]

    h = (_rms(x_ref[...], g_ref[...]) * (1.0 + _mod_rows(sc_ref, seq)) + _mod_rows(sh_ref, seq)).astype(BF16)
    h_ref[...] = h
    p = jnp.dot(h, wbf[...], preferred_element_type=F32)
    p3 = p.reshape(nb, tm, c)
    ext3 = jnp.concatenate([carry[...], p3], axis=1)
    tail = ext3[:, tm:tm + halo, :]
    ns_ref[...] = tail
    carry[...] = tail
    ext = ext3.reshape(nb * (halo + tm), c)
    pos = start + t * tm + lax.broadcasted_iota(I32, (1, tm, 1), 1)
    outs = []
    for gi, w in enumerate(POOL_WINDOWS):
        sl = slice(gi * POOL_GROUP, (gi + 1) * POOL_GROUP)
        acc = ext[:, sl]
        shift = 1
        while shift < w:
            acc = acc + pltpu.roll(acc, shift, 0)
            shift *= 2
        win = acc.reshape(nb, halo + tm, POOL_GROUP)[:, halo:, :]
        cnt = jnp.minimum(pos + 1, w).astype(F32)
        dgrp = win / cnt - p3[:, :, sl]
        outs.append(jnp.dot(dgrp.reshape(nb * tm, POOL_GROUP).astype(BF16), wpbf[gi],
                            preferred_element_type=F32))
    y = jnp.concatenate(outs, axis=-1) * ps_ref[...]
    ya_ref[...] = y.astype(BF16)


def _pool_mixer(x2, g, mod, layer, w_in, w_pool, pool_scale, state, nb, tm, start):
    rows, d = x2.shape
    nseq = state.shape[0]
    tiles_per_seq = (rows // nseq) // tm
    seq_blocks = nseq // nb
    c = POOL_WIDTH
    st = jnp.pad(state, ((0, 0), (POOL_HALO - POOL_STATE, 0), (0, 0)))
    kern = functools.partial(_pool_kernel, nb=nb, tm=tm, tiles_per_seq=tiles_per_seq, start=start)
    h2, ya, ns = pl.pallas_call(
        kern,
        grid=(seq_blocks * tiles_per_seq,),
        in_specs=[pl.BlockSpec((nb * tm, d), lambda i: (i, 0)),
                  pl.BlockSpec((1, d), lambda i: (0, 0)),
                  _mod_spec(mod, layer, 1), _mod_spec(mod, layer, 0),
                  pl.BlockSpec((d, c), lambda i: (0, 0)),
                  pl.BlockSpec((len(POOL_WINDOWS), POOL_GROUP, POOL_GROUP), lambda i: (0, 0, 0)),
                  pl.BlockSpec((1, c), lambda i: (0, 0)),
                  pl.BlockSpec((nb, POOL_HALO, c), lambda i: (i // tiles_per_seq, 0, 0))],
        out_specs=[pl.BlockSpec((nb * tm, d), lambda i: (i, 0)),
                   pl.BlockSpec((nb * tm, c), lambda i: (i, 0)),
                   pl.BlockSpec((nb, POOL_HALO, c), lambda i: (i // tiles_per_seq, 0, 0))],
        out_shape=[jax.ShapeDtypeStruct((rows, d), BF16),
                   jax.ShapeDtypeStruct((rows, c), BF16),
                   jax.ShapeDtypeStruct((nseq, POOL_HALO, c), F32)],
        scratch_shapes=[pltpu.VMEM((d, c), BF16),
                        pltpu.VMEM((len(POOL_WINDOWS), POOL_GROUP, POOL_GROUP), BF16),
                        pltpu.VMEM((nb, POOL_HALO, c), F32)],
        compiler_params=_cparams(1),
        name="pool_mixer",
    )(x2, g.reshape(1, d), mod, mod, w_in, w_pool, pool_scale.reshape(1, c), st)
    return h2, ya, ns[:, POOL_HALO - POOL_STATE:, :]


def _conv_kernel(h_ref, wx_ref, wb_ref, wc_ref, cw_ref, st_ref, yb_ref, ns_ref,
                 wxbf, wbbf, wcbf, carry, *, nb, tm, tiles_per_seq):
    i = pl.program_id(1)
    t = i % tiles_per_seq
    tc = wxbf.shape[1]
    halo = CONV_HALO

    @pl.when(i == 0)
    def _():
        wxbf[...] = wx_ref[...].astype(BF16)
        wbbf[...] = wb_ref[...].astype(BF16)
        wcbf[...] = wc_ref[...].astype(BF16)

    @pl.when(t == 0)
    def _():
        carry[...] = st_ref[...]

    h = h_ref[...]
    xin = jnp.dot(h, wxbf[...], preferred_element_type=F32)
    gb = jnp.dot(h, wbbf[...], preferred_element_type=F32)
    gc = jnp.dot(h, wcbf[...], preferred_element_type=F32)
    z3 = (gc * xin).reshape(nb, tm, tc)
    ext3 = jnp.concatenate([carry[...], z3], axis=1)
    tail = ext3[:, tm:tm + halo, :]
    ns_ref[...] = tail
    carry[...] = tail
    ext = ext3.reshape(nb * (halo + tm), tc)
    cw = cw_ref[...]
    conv = cw[0:1, :] * pltpu.roll(ext, 2, 0) + cw[1:2, :] * pltpu.roll(ext, 1, 0) + cw[2:3, :] * ext
    conv = conv.reshape(nb, halo + tm, tc)[:, halo:, :].reshape(nb * tm, tc)
    yb_ref[...] = (gb * conv).astype(BF16)


def _conv_mixer(h2, w_in, conv_w, state, nb, tm):
    rows, d = h2.shape
    nseq = state.shape[0]
    tiles_per_seq = (rows // nseq) // tm
    seq_blocks = nseq // nb
    c = CONV_WIDTH
    tc = 512
    cb = c // tc
    base = POOL_WIDTH // tc
    st = jnp.pad(state, ((0, 0), (CONV_HALO - (CONV_K - 1), 0), (0, 0)))
    kern = functools.partial(_conv_kernel, nb=nb, tm=tm, tiles_per_seq=tiles_per_seq)
    yb, ns = pl.pallas_call(
        kern,
        grid=(cb, seq_blocks * tiles_per_seq),
        in_specs=[pl.BlockSpec((nb * tm, d), lambda j, i: (i, 0)),
                  pl.BlockSpec((d, tc), lambda j, i: (0, base + j)),
                  pl.BlockSpec((d, tc), lambda j, i: (0, base + cb + j)),
                  pl.BlockSpec((d, tc), lambda j, i: (0, base + 2 * cb + j)),
                  pl.BlockSpec((CONV_K, tc), lambda j, i: (0, j)),
                  pl.BlockSpec((nb, CONV_HALO, tc), lambda j, i: (i // tiles_per_seq, 0, j))],
        out_specs=[pl.BlockSpec((nb * tm, tc), lambda j, i: (i, j)),
                   pl.BlockSpec((nb, CONV_HALO, tc), lambda j, i: (i // tiles_per_seq, 0, j))],
        out_shape=[jax.ShapeDtypeStruct((rows, c), BF16),
                   jax.ShapeDtypeStruct((nseq, CONV_HALO, c), F32)],
        scratch_shapes=[pltpu.VMEM((d, tc), BF16)] * 3 + [pltpu.VMEM((nb, CONV_HALO, tc), F32)],
        compiler_params=_cparams(2),
        name="conv_mixer",
    )(h2, w_in, w_in, w_in, conv_w, st)
    return yb, ns[:, CONV_HALO - (CONV_K - 1):, :]


def _pack_bf16_pairs(v):
    c = v.shape[1] // 2
    return pltpu.bitcast(pltpu.pack_elementwise([v[:, :c], v[:, c:]], packed_dtype=BF16), U32)


def _store_token_tiles(ref, v):
    rows = v.shape[0]
    for j in range(V7X_SUBLANES):
        ref[pl.ds(j, rows, stride=V7X_SUBLANES), :] = v[:, j * V7X_LANES:(j + 1) * V7X_LANES]


def _load_token_tiles(ref):
    rows = ref.shape[0] // V7X_SUBLANES
    return jnp.concatenate([ref[pl.ds(j, rows, stride=V7X_SUBLANES), :] for j in range(V7X_SUBLANES)],
                           axis=-1)


def _unpack_pairs_f32(w):
    return tuple(pltpu.unpack_elementwise(w, index=k, packed_dtype=BF16, unpacked_dtype=F32) for k in range(2))


def _unpack_bf16_pairs(w):
    lo, hi = _unpack_pairs_f32(w)
    return lo.astype(BF16), hi.astype(BF16)


def _outproj_kernel(ya_ref, yb_ref, x_ref, g1_ref, sc_ref, sh_ref, ng_ref, wo_ref,
                    x1_ref, hp_ref, hpt_ref, wobf, *, tiles_per_seq):
    i = pl.program_id(0)
    seq = i // tiles_per_seq

    @pl.when(i == 0)
    def _():
        wobf[...] = wo_ref[...].astype(BF16)

    ycat = jnp.concatenate([ya_ref[...], yb_ref[...]], axis=-1)
    y = jnp.dot(ycat, wobf[...], preferred_element_type=F32)
    x1 = x_ref[...] + _mod_rows(g1_ref, seq) * y
    x1_ref[...] = x1
    h2 = _rms(x1, ng_ref[...]) * (1.0 + _mod_rows(sc_ref, seq)) + _mod_rows(sh_ref, seq)
    packed = _pack_bf16_pairs(h2)
    hp_ref[...] = packed
    _store_token_tiles(hpt_ref, packed)


def _outproj(ya, yb, x2, mod, layer, ng, w_out, seq_rows):
    rows_all, d = x2.shape
    half = ya.shape[1]
    rt = ROW_TILE
    row_spec = lambda w: pl.BlockSpec((rt, w), lambda i: (i, 0))
    return pl.pallas_call(
        functools.partial(_outproj_kernel, tiles_per_seq=seq_rows // rt),
        grid=(rows_all // rt,),
        in_specs=[row_spec(half), row_spec(half), row_spec(d),
                  _mod_spec(mod, layer, 2), _mod_spec(mod, layer, 4), _mod_spec(mod, layer, 3),
                  pl.BlockSpec((1, d), lambda i: (0, 0)),
                  pl.BlockSpec((d, d), lambda i: (0, 0), pipeline_mode=pl.Buffered(1))],
        out_specs=[row_spec(d), row_spec(d // 2),
                   pl.BlockSpec((rt * V7X_SUBLANES, V7X_LANES), lambda i: (i, 0))],
        out_shape=[jax.ShapeDtypeStruct((rows_all, d), F32),
                   jax.ShapeDtypeStruct((rows_all, d // 2), U32),
                   jax.ShapeDtypeStruct((rows_all * V7X_SUBLANES, V7X_LANES), U32)],
        scratch_shapes=[pltpu.VMEM((d, d), BF16)],
        compiler_params=_cparams(1),
        name="outproj",
    )(ya, yb, x2, mod, mod, mod, ng.reshape(1, d), w_out)


def _router_kernel(hpp_ref, hps_ref, wr_ref, br_ref, pos_ref, rw_ref, tab_ref,
                   cnt_acc, totals, starts, padded, before_ref, s_all, sel_all, *, nt_p, rows_s):
    ph = pl.program_id(0)
    t = pl.program_id(1)
    last = nt_p
    r = hpp_ref.shape[0]
    half = hpp_ref.shape[1]
    ne = N_EXPERTS
    sub = lax.broadcasted_iota(I32, (ne, V7X_LANES), 0)

    @pl.when(t == 0)
    def _():
        cnt_acc[...] = jnp.zeros_like(cnt_acc)

    @pl.when((ph == 0) & (t == 0))
    def _():
        starts[...] = jnp.zeros_like(starts)
        padded[...] = jnp.zeros_like(padded)

    @pl.when((ph == 1) & (t == 0))
    def _():
        pad = jnp.floor((totals[...] + (MOE_TILE - 1.0)) * (1.0 / MOE_TILE)) * MOE_TILE
        run = pad
        k = 1
        while k < ne:
            run = run + jnp.where(sub >= k, pltpu.roll(run, k, 0), 0.0)
            k *= 2
        padded[...] = pad
        starts[...] = run - pad

    is_s = t == last
    eid = lax.broadcasted_iota(I32, (ne, r), 0)
    n_valid = jnp.where(is_s, rows_s, r)
    tok = lax.broadcasted_iota(I32, (ne, r), 1)

    @pl.when(ph == 0)
    def _():
        w_s = jnp.concatenate([hps_ref[...], jnp.zeros((r - rows_s, half), U32)], axis=0)
        w = jnp.where(is_s, w_s, hpp_ref[...])
        lo, hi = _unpack_bf16_pairs(w)
        wr = wr_ref[...].astype(BF16)
        log_t = (lax.dot_general(wr[:, :half], lo, NT_DIMS, preferred_element_type=F32)
                 + lax.dot_general(wr[:, half:], hi, NT_DIMS, preferred_element_type=F32))

        s = jax.nn.sigmoid(log_t)
        sg = s + br_ref[...]
        within = eid % EXP_PER_GROUP
        grp = eid // EXP_PER_GROUP

        def group_rot(x, k):
            return jnp.where(within + k < EXP_PER_GROUP,
                             pltpu.roll(x, ne - k, 0), pltpu.roll(x, EXP_PER_GROUP - k, 0))

        rank = jnp.zeros((ne, r), I32)
        for k in range(1, EXP_PER_GROUP):
            mate = group_rot(sg, k)
            wrapped = within + k >= EXP_PER_GROUP
            ahead = (mate > sg) | (wrapped & (mate == sg))
            rank = rank + ahead.astype(I32)
        top2 = rank < TOP_K
        kept = jnp.where(top2, sg, 0.0)
        gscore = kept
        for k in range(1, EXP_PER_GROUP):
            gscore = gscore + group_rot(kept, k)
        win = None
        for k in range(1, N_EXPERT_GROUPS):
            other = pltpu.roll(gscore, EXP_PER_GROUP * k, 0)
            beats = (gscore > other) | ((grp < k) & (gscore == other))
            win = beats if win is None else (win & beats)
        picked_now = top2 & win & (tok < n_valid)
        s_all[t] = s
        sel_all[t] = picked_now.astype(F32)

    s = s_all[t]
    selb = sel_all[t]
    sel = selb > 0.5
    cnt_before = cnt_acc[...]
    cnt_new = cnt_before + jnp.sum(selb, axis=1, keepdims=True)
    cnt_acc[...] = cnt_new

    @pl.when((ph == 0) & (t == 0))
    def _():
        src = lax.broadcasted_iota(I32, (r, r), 0)
        dst = lax.broadcasted_iota(I32, (r, r), 1)
        before_ref[...] = (src < dst).astype(BF16)

    @pl.when((ph == 0) & (t == last))
    def _():
        totals[...] = cnt_new

    @pl.when(ph == 1)
    def _():
        picked = jnp.where(sel, s, 0.0)
        wsum = jnp.sum(picked, axis=0, keepdims=True)
        gate = picked / jnp.where(tok[0:1, :] < n_valid, wsum, 1.0)
        ranks = jnp.dot(selb.astype(BF16), before_ref[...], preferred_element_type=F32)
        slot = (starts[...][:, 0:1] + cnt_before[:, 0:1] + ranks).astype(I32)
        e_a = jnp.min(jnp.where(sel, eid, ne), axis=0, keepdims=True)
        e_b = jnp.max(jnp.where(sel, eid, -1), axis=0, keepdims=True)
        is_a = sel & (eid == e_a)
        is_b = sel & (eid == e_b)
        pos_a = jnp.sum(jnp.where(is_a, slot, 0), axis=0, keepdims=True)
        pos_b = jnp.sum(jnp.where(is_b, slot, 0), axis=0, keepdims=True)
        w_a = jnp.sum(jnp.where(is_a, gate, 0.0), axis=0, keepdims=True)
        w_b = jnp.sum(jnp.where(is_b, gate, 0.0), axis=0, keepdims=True)
        pos_ref[0] = jnp.concatenate([pos_a, pos_b], axis=0)
        wmat = jnp.concatenate([w_a, w_b, jnp.zeros((V7X_LANES - 2, r), F32)], axis=0)
        rw_ref[...] = wmat.T

    @pl.when((ph == 1) & (t == last))
    def _():
        ends = starts[...] + padded[...]
        lane = lax.broadcasted_iota(I32, (ne, V7X_LANES), 1)
        tile_start = (lane * MOE_TILE).astype(F32)
        te = jnp.sum((tile_start >= ends).astype(I32), axis=0, keepdims=True)
        valid = te < ne
        last_e = jnp.max(jnp.where(padded[...] > 0.0, sub, 0), axis=0, keepdims=True)
        te = jnp.where(valid, te, last_e)
        n_used = jnp.sum(valid.astype(I32), axis=1, keepdims=True) + jnp.zeros((1, V7X_LANES), I32)
        last_tile = jnp.where(padded[...] > 0.0, ends - MOE_TILE, -1.0).astype(I32)
        last_tile_row = jnp.sum(jnp.where(sub == lane, last_tile, 0), axis=0, keepdims=True)
        later = jnp.min(jnp.where((sub > te) & (padded[...] > 0.0), sub, ne), axis=0, keepdims=True)
        next_e = jnp.where(later < ne, later, -1)
        zero = jnp.zeros((1, V7X_LANES), I32)
        tab_ref[...] = jnp.concatenate([te, valid.astype(I32), last_tile_row, n_used, next_e,
                                        zero, zero, zero], axis=0)


def _router(hp_p, hp_s, w_router, b_router):
    n_p, half = hp_p.shape
    rows_s = hp_s.shape[0]
    r = ROUTER_TILE
    nt_p = n_p // r
    nt = nt_p + 1
    kern = functools.partial(_router_kernel, nt_p=nt_p, rows_s=rows_s)
    pos, rw, tab = pl.pallas_call(
        kern,
        grid=(2, nt),
        in_specs=[pl.BlockSpec((r, half), lambda p, t: (jnp.minimum(t, nt_p - 1) * (1 - p), 0)),
                  pl.BlockSpec((rows_s, half), lambda p, t: (0, 0)),
                  pl.BlockSpec((N_EXPERTS, 2 * half), lambda p, t: (0, 0)),
                  pl.BlockSpec((N_EXPERTS, 1), lambda p, t: (0, 0))],
        out_specs=[pl.BlockSpec((1, TOP_K, r), lambda p, t: (p * t, 0, 0)),
                   pl.BlockSpec((r, V7X_LANES), lambda p, t: (p * t, 0)),
                   pl.BlockSpec((V7X_SUBLANES, V7X_LANES), lambda p, t: (0, 0))],
        out_shape=[jax.ShapeDtypeStruct((nt, TOP_K, r), I32),
                   jax.ShapeDtypeStruct((nt * r, V7X_LANES), F32),
                   jax.ShapeDtypeStruct((V7X_SUBLANES, V7X_LANES), I32)],
        scratch_shapes=[pltpu.VMEM((N_EXPERTS, V7X_LANES), F32)] * 4 + [pltpu.VMEM((r, r), BF16)]
        + [pltpu.VMEM((nt, N_EXPERTS, r), F32)] * 2,
        compiler_params=_cparams(2),
        name="router",
    )(hp_p, hp_s, w_router.T, b_router.reshape(N_EXPERTS, 1))
    return pos.reshape(-1), rw, tab.reshape(-1)


def _pos_index(tok0):
    return (tok0 // ROUTER_TILE) * (TOP_K * ROUTER_TILE) + tok0 % ROUTER_TILE


def _tokens(ref, first, n=1):
    start = pl.multiple_of(first * V7X_SUBLANES, V7X_SUBLANES)
    return ref.at[pl.ds(start, n * V7X_SUBLANES), :]


def _dispatch_kernel(pos_ref, tab_ref, hpp_ref, hps_ref, xs_ref, zbuf, sem, *, n_p_steps):
    i = pl.program_id(0)
    rows = hpp_ref.shape[0] // V7X_SUBLANES

    @pl.when(i == 0)
    def _():
        zbuf[...] = jnp.zeros_like(zbuf)

        def fill(e):
            first = pl.multiple_of(tab_ref[TAB_LAST_TILE * V7X_LANES + e], MOE_TILE)
            return pltpu.make_async_copy(zbuf, _tokens(xs_ref, first, MOE_TILE), sem)

        for e in range(N_EXPERTS):
            @pl.when(tab_ref[TAB_LAST_TILE * V7X_LANES + e] >= 0)
            def _():
                fill(e).start()
        for e in range(N_EXPERTS):
            @pl.when(tab_ref[TAB_LAST_TILE * V7X_LANES + e] >= 0)
            def _():
                fill(e).wait()

        def tail(j):
            first = pl.multiple_of(j * MOE_TILE, MOE_TILE)
            return pltpu.make_async_copy(zbuf, _tokens(xs_ref, first, MOE_TILE), sem)

        def tail_start(j, carry):
            tail(j).start()
            return carry

        def tail_wait(j, carry):
            tail(j).wait()
            return carry

        n_used = tab_ref[TAB_NUSED * V7X_LANES]
        n_tiles = xs_ref.shape[0] // (MOE_TILE * V7X_SUBLANES)
        lax.fori_loop(n_used, n_tiles, tail_start, 0)
        lax.fori_loop(n_used, n_tiles, tail_wait, 0)

    def scatter(src_ref, tok0):
        n = src_ref.shape[0] // V7X_SUBLANES
        base = _pos_index(tok0)

        def row_copy(r, dst):
            return pltpu.make_async_copy(_tokens(src_ref, r), _tokens(xs_ref, dst), sem)

        def issue(r, carry):
            row_copy(r, pos_ref[base + r]).start()
            row_copy(r, pos_ref[base + ROUTER_TILE + r]).start(priority=1)
            return carry

        lax.fori_loop(0, n, issue, 0, unroll=8)
        block = pltpu.make_async_copy(src_ref, _tokens(xs_ref, 0, n), sem)
        for _ in range(TOP_K):
            block.wait()

    @pl.when(i < n_p_steps)
    def _():
        scatter(hpp_ref, i * rows)

    @pl.when(i == n_p_steps)
    def _():
        scatter(hps_ref, n_p_steps * rows)


def _dispatch(pos, tab, hpt_p, hpt_s, n_rows_sorted):
    sub = V7X_SUBLANES
    n_p_steps = hpt_p.shape[0] // (MATMUL_TILE * sub)
    kern = functools.partial(_dispatch_kernel, n_p_steps=n_p_steps)
    blk = (MATMUL_TILE * sub, V7X_LANES)
    return pl.pallas_call(
        kern,
        grid_spec=pltpu.PrefetchScalarGridSpec(
            num_scalar_prefetch=2,
            grid=(n_p_steps + 1,),
            in_specs=[pl.BlockSpec(blk, lambda i, p, t: (jnp.minimum(i, n_p_steps - 1), 0)),
                      pl.BlockSpec(hpt_s.shape, lambda i, p, t: (0, 0))],
            out_specs=pl.BlockSpec(memory_space=pl.ANY),
            scratch_shapes=[pltpu.VMEM((MOE_TILE * sub, V7X_LANES), U32), pltpu.SemaphoreType.DMA(())]),
        out_shape=jax.ShapeDtypeStruct((n_rows_sorted * sub, V7X_LANES), U32),
        compiler_params=_cparams(1),
        name="moe_dispatch",
    )(pos, tab, hpt_p, hpt_s)


def _experts_kernel(tab_ref, xs_ref, wg_hbm, wu_hbm, wd_hbm, ys_ref,
                    wg32, wu32, wd32, wgbf, wubf, wdbf, slot_ref, sems, *, layer):
    i = pl.program_id(0)
    expert = tab_ref[TAB_EXPERT * V7X_LANES + i]
    prev = tab_ref[TAB_EXPERT * V7X_LANES + jnp.maximum(i - 1, 0)]
    upcoming = tab_ref[TAB_NEXT * V7X_LANES + i]
    changed = (i == 0) | (expert != prev)

    def weight_copies(e, slot):
        return (pltpu.make_async_copy(wg_hbm.at[layer, e], wg32.at[slot], sems.at[0, slot]),
                pltpu.make_async_copy(wu_hbm.at[layer, e], wu32.at[slot], sems.at[1, slot]),
                pltpu.make_async_copy(wd_hbm.at[layer, e], wd32.at[slot], sems.at[2, slot]))

    @pl.when(i == 0)
    def _():
        slot_ref[0] = 0
        for cp in weight_copies(expert, 0):
            cp.start()

    @pl.when(changed & (i > 0))
    def _():
        slot_ref[0] = 1 - slot_ref[0]

    def mlp(wg, wu, wd):
        x = jnp.concatenate(_unpack_bf16_pairs(_load_token_tiles(xs_ref)), axis=-1)
        a = jnp.dot(x, wg, preferred_element_type=F32)
        b = jnp.dot(x, wu, preferred_element_type=F32)
        hid = (a * jax.nn.sigmoid(a)) * b
        y = jnp.dot(hid.astype(BF16), wd, preferred_element_type=F32)
        _store_token_tiles(ys_ref, _pack_bf16_pairs(y))

    for slot in range(2):
        @pl.when(changed & (slot_ref[0] == slot))
        def _():
            for cp in weight_copies(expert, slot):
                cp.wait()

            @pl.when(upcoming >= 0)
            def _():
                for cp in weight_copies(upcoming, 1 - slot):
                    cp.start(priority=1)

            wg = wg32[slot].astype(BF16)
            wu = wu32[slot].astype(BF16)
            wd = wd32[slot].astype(BF16)
            wgbf[...] = wg
            wubf[...] = wu
            wdbf[...] = wd
            mlp(wg, wu, wd)

    valid = tab_ref[TAB_VALID * V7X_LANES + i] > 0

    @pl.when(valid & jnp.logical_not(changed))
    def _():
        mlp(wgbf[...], wubf[...], wdbf[...])

    @pl.when(jnp.logical_not(valid))
    def _():
        ys_ref[...] = jnp.zeros_like(ys_ref)


def _experts(tab, xs, w_gate, w_up, w_down, layer):
    sub = V7X_SUBLANES
    _, _, d, f = w_gate.shape
    nt = xs.shape[0] // (MOE_TILE * sub)
    assert nt <= V7X_LANES and d == 2 * sub * V7X_LANES
    blk = (MOE_TILE * sub, V7X_LANES)

    def tile(i, tab_ref):
        return jnp.minimum(i, tab_ref[TAB_NUSED * V7X_LANES] - 1)

    hbm = pl.BlockSpec(memory_space=pl.ANY)
    return pl.pallas_call(
        functools.partial(_experts_kernel, layer=layer),
        grid_spec=pltpu.PrefetchScalarGridSpec(
            num_scalar_prefetch=1,
            grid=(nt,),
            in_specs=[pl.BlockSpec(blk, lambda i, t: (tile(i, t), 0)), hbm, hbm, hbm],
            out_specs=pl.BlockSpec(blk, lambda i, t: (i, 0)),
            scratch_shapes=[pltpu.VMEM((2, d, f), F32), pltpu.VMEM((2, d, f), F32), pltpu.VMEM((2, f, d), F32),
                            pltpu.VMEM((d, f), BF16), pltpu.VMEM((d, f), BF16), pltpu.VMEM((f, d), BF16),
                            pltpu.SMEM((1,), I32), pltpu.SemaphoreType.DMA((3, 2))]),
        out_shape=jax.ShapeDtypeStruct(xs.shape, U32),
        compiler_params=_cparams(1),
        name="moe_experts",
    )(tab, xs, w_gate, w_up, w_down)


def _combine_kernel(pos_ref, ys_ref, x1_ref, rw_ref, g2_ref, ng_ref, sc_ref, sh_ref, *rest,
                    tok0, tiles_per_seq, n_steps, final):
    if final:
        x2_ref, buf0, buf1, sems = rest
        hn_ref = None
    else:
        x2_ref, hn_ref, buf0, buf1, sems = rest
    bufs = (buf0, buf1)
    i = pl.program_id(0)
    seq = i // tiles_per_seq
    rows = x1_ref.shape[0]

    def gather(step, slot):
        base = _pos_index(tok0 + step * rows)
        for r in range(rows):
            for k in range(TOP_K):
                src = pos_ref[base + k * ROUTER_TILE + r]
                pltpu.make_async_copy(_tokens(ys_ref, src), _tokens(bufs[slot].at[k], r),
                                      sems.at[slot]).start(priority=k)

    def drain(slot):
        for k in range(TOP_K):
            pltpu.make_async_copy(_tokens(ys_ref, 0, rows), bufs[slot].at[k], sems.at[slot]).wait()

    def finish(slot):
        rw = rw_ref[...]
        lo_a, hi_a = _unpack_pairs_f32(_load_token_tiles(bufs[slot].at[0]))
        lo_b, hi_b = _unpack_pairs_f32(_load_token_tiles(bufs[slot].at[1]))
        w_a = rw[:, 0:1]
        w_b = rw[:, 1:2]
        moe = jnp.concatenate([w_a * lo_a + w_b * lo_b, w_a * hi_a + w_b * hi_b], axis=-1)
        x2 = x1_ref[...] + _mod_rows(g2_ref, seq) * moe
        if final:
            x2_ref[...] = _rms(x2, ng_ref[...])
        else:
            x2_ref[...] = x2
            hn_ref[...] = (_rms(x2, ng_ref[...]) * (1.0 + _mod_rows(sc_ref, seq))
                           + _mod_rows(sh_ref, seq)).astype(BF16)

    @pl.when(i == 0)
    def _():
        gather(0, 0)

    ahead = n_steps > 1
    for slot in range(2):
        @pl.when(i % 2 == slot)
        def _():
            drain(slot)
            if ahead:
                gather(jnp.minimum(i + 1, n_steps - 1), 1 - slot)
            finish(slot)
            if ahead:
                @pl.when(i == n_steps - 1)
                def _():
                    drain(1 - slot)


def _combine(pos, ys, x1, rw, tok0, mod, layer, ng, seq_rows, final):
    n_tok, d = x1.shape
    rt = ROW_TILE
    assert tok0 % rt == 0 and ROUTER_TILE % rt == 0
    kern = functools.partial(_combine_kernel, tok0=tok0, tiles_per_seq=seq_rows // rt,
                             n_steps=n_tok // rt, final=final)
    rw_off = tok0 // rt
    row_spec = lambda w: pl.BlockSpec((rt, w), lambda i, p: (i, 0))
    out_shape = [jax.ShapeDtypeStruct((n_tok, d), F32)]
    out_specs = [row_spec(d)]
    if not final:
        out_shape.append(jax.ShapeDtypeStruct((n_tok, d), BF16))
        out_specs.append(row_spec(d))
    nxt = min(layer + 1, mod.shape[0] - 1)
    return pl.pallas_call(
        kern,
        grid_spec=pltpu.PrefetchScalarGridSpec(
            num_scalar_prefetch=1,
            grid=(n_tok // rt,),
            in_specs=[pl.BlockSpec(memory_space=pl.ANY), row_spec(d),
                      pl.BlockSpec((rt, V7X_LANES), lambda i, p: (rw_off + i, 0)),
                      _mod_spec(mod, layer, 5), pl.BlockSpec((1, d), lambda i, p: (0, 0)),
                      _mod_spec(mod, nxt, 1), _mod_spec(mod, nxt, 0)],
            out_specs=out_specs,
            scratch_shapes=[pltpu.VMEM((TOP_K, rt * V7X_SUBLANES, V7X_LANES), U32)] * 2
            + [pltpu.SemaphoreType.DMA((2,))]),
        out_shape=out_shape,
        compiler_params=_cparams(1),
        name="moe_combine_final" if final else "moe_combine",
    )(pos, ys, x1, rw, mod, ng.reshape(1, d), mod, mod)


def _moe(out_p, out_s, mod_p, mod_s, layer, ng, seq_rows_p, w_router, b_router, w_gate, w_up, w_down, final):
    x1_p, hp_p, hpt_p = out_p
    x1_s, hp_s, hpt_s = out_s
    n_p = x1_p.shape[0]
    n_tok = n_p + x1_s.shape[0]
    max_rows = TOP_K * n_tok + N_EXPERTS * (MOE_TILE - 1)
    n_rows_sorted = -(-max_rows // MOE_TILE) * MOE_TILE
    pos, rw, tab = _router(hp_p, hp_s, w_router, b_router)
    xs = _dispatch(pos, tab, hpt_p, hpt_s, n_rows_sorted)
    ys = _experts(tab, xs, w_gate, w_up, w_down, layer)
    out_p = _combine(pos, ys, x1_p, rw, 0, mod_p, layer, ng, seq_rows_p, final)
    out_s = _combine(pos, ys, x1_s, rw, n_p, mod_s, layer, ng, x1_s.shape[0], final)
    return out_p, out_s


def _gmlp_kernel(h_ref, w_ref, lg_ref, lb_ref, ws_ref, bs_ref, yc_ref, *rest, ell, blk, emit_v):
    if emit_v:
        gv_ref, wbf, wsbf = rest
    else:
        wbf, wsbf = rest
    i = pl.program_id(0)
    rows = h_ref.shape[0]
    c = GM_WIDTH

    @pl.when(i == 0)
    def _():
        wbf[...] = w_ref[...].astype(BF16)
        r = lax.broadcasted_iota(I32, (ell, ell), 0)
        s = lax.broadcasted_iota(I32, (ell, ell), 1)
        keep = (r >= s) & ((r // blk) == (s // blk))
        rsel = (lax.broadcasted_iota(I32, (ell, CHUNK), 0) % blk
                == lax.broadcasted_iota(I32, (ell, CHUNK), 1)).astype(BF16)
        csel = (lax.broadcasted_iota(I32, (CHUNK, ell), 1) % blk
                == lax.broadcasted_iota(I32, (CHUNK, ell), 0)).astype(BF16)
        for g in range(GM_GROUPS):
            wchunk = ws_ref[g].astype(BF16)
            if blk == ell:
                full = wchunk
            else:
                rowsp = jnp.dot(rsel, wchunk, preferred_element_type=F32).astype(BF16)
                full = jnp.dot(rowsp, csel, preferred_element_type=F32).astype(BF16)
            wsbf[g] = jnp.where(keep, full, jnp.zeros_like(full))

    uv = jnp.dot(h_ref[...], wbf[...], preferred_element_type=F32)
    u = uv[:, :c]
    v = uv[:, c:]
    vc = v - jnp.mean(v, axis=-1, keepdims=True)
    vn = vc * lax.rsqrt(jnp.mean(vc * vc, axis=-1, keepdims=True) + EPS) * lg_ref[...] + lb_ref[...]
    if emit_v:
        gv_ref[...] = vn
    vb = vn.astype(BF16)
    bs = bs_ref[...]
    for ch in range(rows // ell):
        rs = slice(ch * ell, (ch + 1) * ell)
        outs = []
        for g in range(GM_GROUPS):
            cs = slice(g * GM_GROUP, (g + 1) * GM_GROUP)
            mixed = jnp.dot(wsbf[g], vb[rs, cs], preferred_element_type=F32)
            mixed = (mixed.reshape(ell // blk, blk, GM_GROUP) + bs[:blk, g:g + 1][None]).reshape(ell, GM_GROUP)
            outs.append(u[rs, cs] * mixed)
        yc_ref[rs, :] = jnp.concatenate(outs, axis=-1).astype(BF16)


def _gmlp_mixer(h2, w_in, ln_g, ln_b, ws, bs_t, ell, blk, emit_v):
    rows, d = h2.shape
    c = GM_WIDTH
    kern = functools.partial(_gmlp_kernel, ell=ell, blk=blk, emit_v=emit_v)
    rt = min(rows, MATMUL_TILE)
    out_specs = [pl.BlockSpec((rt, c), lambda i: (i, 0))]
    out_shape = [jax.ShapeDtypeStruct((rows, c), BF16)]
    if emit_v:
        out_specs.append(pl.BlockSpec((rt, c), lambda i: (i, 0)))
        out_shape.append(jax.ShapeDtypeStruct((rows, c), F32))
    return pl.pallas_call(
        kern,
        grid=(rows // rt,),
        in_specs=[pl.BlockSpec((rt, d), lambda i: (i, 0)),
                  pl.BlockSpec((d, 2 * c), lambda i: (0, 0), pipeline_mode=pl.Buffered(1)),
                  pl.BlockSpec((1, c), lambda i: (0, 0)),
                  pl.BlockSpec((1, c), lambda i: (0, 0)),
                  pl.BlockSpec((GM_GROUPS, CHUNK, CHUNK), lambda i: (0, 0, 0)),
                  pl.BlockSpec((CHUNK, GM_GROUPS), lambda i: (0, 0))],
        out_specs=out_specs,
        out_shape=out_shape,
        scratch_shapes=[pltpu.VMEM((d, 2 * c), BF16), pltpu.VMEM((GM_GROUPS, ell, ell), BF16)],
        compiler_params=_cparams(1),
        name="gmlp_mixer",
    )(h2, w_in, ln_g.reshape(1, c), ln_b.reshape(1, c), ws, bs_t)


PAIR_W = 2 * HEAD_DIM
PAIRS_PER_KV = N_HEADS // N_KV // 2
NT_DIMS = (((1,), (1,)), ((), ()))
SCORE_SCALE = HEAD_DIM ** -0.5
assert float(np.log2(SCORE_SCALE)).is_integer()


def _swa_project(i, h_ref, wq_ref, wkv_ref, wbf):
    nq = N_HEADS * HEAD_DIM

    @pl.when(i == 0)
    def _():
        wbf[:, :nq] = wq_ref[...].astype(BF16)
        wbf[:, nq:] = wkv_ref[...].astype(BF16)

    return jnp.dot(h_ref[...], wbf[...], preferred_element_type=F32)


def _pair_block_diag(a, a_swapped, hk, axis):
    dim_axis = 1 - axis
    low = lax.broadcasted_iota(I32, a.shape, dim_axis) < HEAD_DIM
    lo, hi = (a, a_swapped) if hk == 0 else (a_swapped, a)
    return jnp.concatenate([jnp.where(low, lo, 0.0), jnp.where(low, 0.0, hi)], axis=axis).astype(BF16)


def _stack_pairs(qkv, rs, hk, scale=None):
    p0 = hk * PAIRS_PER_KV
    q = jnp.concatenate([qkv[rs, (p0 + pp) * PAIR_W:(p0 + pp + 1) * PAIR_W]
                         for pp in range(PAIRS_PER_KV)], axis=0)
    return (q if scale is None else q * scale).astype(BF16)


def _swa_cached_kernel(h_ref, wq_ref, wkv_ref, kp_ref, vp_ref, bias_ref, sink_ref, yd_ref, k_ref, v_ref,
                       wbf, *, tq):
    i = pl.program_id(0)
    rows = h_ref.shape[0]
    nq = N_HEADS * HEAD_DIM
    nkv = N_KV * HEAD_DIM
    n_blocks = rows // tq
    qkv = _swa_project(i, h_ref, wq_ref, wkv_ref, wbf)
    k_new = qkv[:, nq:nq + nkv]
    v_new = qkv[:, nq + nkv:]
    k_ref[...] = k_new
    v_ref[...] = v_new
    pad = jnp.zeros((WINDOW - tq, nkv), F32)

    scores, vbds = [], []
    for blk in range(n_blocks):
        rs = slice(blk * tq, (blk + 1) * tq)
        kcat = jnp.concatenate([kp_ref[blk], k_new[rs], pad], axis=0)
        vcat = jnp.concatenate([vp_ref[blk], v_new[rs], pad], axis=0)
        kswap = pltpu.roll(kcat, HEAD_DIM, 1)
        vswap = pltpu.roll(vcat, HEAD_DIM, 1)
        per_head = []
        for hk in range(N_KV):
            kbd = _pair_block_diag(kcat, kswap, hk, 0)
            vbds.append(_pair_block_diag(vcat, vswap, hk, 0))
            s4 = lax.dot_general(_stack_pairs(qkv, rs, hk), kbd, NT_DIMS,
                                 preferred_element_type=F32) * (HEAD_DIM ** -0.5)
            for pp in range(PAIRS_PER_KV):
                for sub in range(2):
                    per_head.append(s4[pp * tq:(pp + 1) * tq, sub * 2 * WINDOW:(sub + 1) * 2 * WINDOW])
        scores.append(jnp.concatenate(per_head, axis=0))

    s_all = jnp.stack(scores, axis=0) + bias_ref[...][None]
    sink = sink_ref[...][None]
    m = jnp.maximum(jnp.max(s_all, axis=-1, keepdims=True), sink)
    pr = jnp.exp(s_all - m)
    pr = pr / (jnp.sum(pr, axis=-1, keepdims=True) + jnp.exp(sink - m))

    for blk in range(n_blocks):
        outs = []
        for hk in range(N_KV):
            p4 = []
            for pp in range(PAIRS_PER_KV):
                h0 = 2 * (hk * PAIRS_PER_KV + pp)
                p4.append(jnp.concatenate([pr[blk, h0 * tq:(h0 + 1) * tq, :],
                                           pr[blk, (h0 + 1) * tq:(h0 + 2) * tq, :]], axis=-1))
            o4 = jnp.dot(jnp.concatenate(p4, axis=0).astype(BF16), vbds[blk * N_KV + hk],
                         preferred_element_type=F32)
            outs.extend(o4[pp * tq:(pp + 1) * tq, :] for pp in range(PAIRS_PER_KV))
        yd_ref[blk * tq:(blk + 1) * tq, :] = jnp.concatenate(outs, axis=-1).astype(BF16)


def _swa_stream_kernel(h_ref, wq_ref, wkv_ref, bias_ref, sink_ref, yd_ref, k_ref, v_ref,
                       wbf, kprev, vprev_t, *, blocks_per_seq):
    i = pl.program_id(0)
    rows = h_ref.shape[0]
    nq = N_HEADS * HEAD_DIM
    nkv = N_KV * HEAD_DIM
    tq = WINDOW
    n_blocks = rows // tq

    @pl.when(i == 0)
    def _():
        kprev[...] = jnp.zeros_like(kprev)
        vprev_t[...] = jnp.zeros_like(vprev_t)

    qkv = _swa_project(i, h_ref, wq_ref, wkv_ref, wbf)
    k_new = qkv[:, nq:nq + nkv]
    v_new = qkv[:, nq + nkv:]
    k_ref[...] = k_new
    v_ref[...] = v_new
    v_new_t = v_new.T
    lanes = PAIRS_PER_KV * tq

    for blk in range(n_blocks):
        rs = slice(blk * tq, (blk + 1) * tq)
        first = ((i * n_blocks + blk) % blocks_per_seq == 0).astype(I32)
        k_cur = k_new[rs]
        v_cur_t = v_new_t[:, rs]
        kcat = jnp.concatenate([kprev[...], k_cur], axis=0)
        vcat_t = jnp.concatenate([vprev_t[...], v_cur_t], axis=1)
        kprev[...] = k_cur
        vprev_t[...] = v_cur_t
        kswap = pltpu.roll(kcat, HEAD_DIM, 1)
        vswap_t = pltpu.roll(vcat_t, HEAD_DIM, 0)
        outs = []
        for hk in range(N_KV):
            kbd = _pair_block_diag(kcat, kswap, hk, 0)
            vbd_t = _pair_block_diag(vcat_t, vswap_t, hk, 1)
            st = lax.dot_general(kbd, _stack_pairs(qkv, rs, hk, SCORE_SCALE), NT_DIMS,
                                 preferred_element_type=F32)
            s3 = st.reshape(2, 2 * WINDOW, lanes) + bias_ref[first, hk]
            sink = sink_ref[hk]
            m = jnp.maximum(jnp.max(s3, axis=1, keepdims=True), sink)
            pr = jnp.exp(s3 - m)
            inv = 1.0 / (jnp.sum(pr, axis=1, keepdims=True) + jnp.exp(sink - m))
            o_t = jnp.dot(vbd_t, pr.reshape(4 * WINDOW, lanes).astype(BF16),
                          preferred_element_type=F32)
            norm = jnp.concatenate([jnp.broadcast_to(inv[sub], (HEAD_DIM, lanes)) for sub in range(2)], axis=0)
            o4 = (o_t * norm).T
            outs.extend(o4[pp * tq:(pp + 1) * tq, :] for pp in range(PAIRS_PER_KV))
        yd_ref[rs, :] = jnp.concatenate(outs, axis=-1).astype(BF16)


def _swa_weight_specs(w_in, d):
    nq = N_HEADS * HEAD_DIM
    nkv = N_KV * HEAD_DIM
    nw = nq + 2 * nkv
    q_blk = (w_in.shape[1] - nw) // nq
    kv_blk = (w_in.shape[1] - 2 * nkv) // (2 * nkv)
    assert q_blk * nq + nw == w_in.shape[1] and kv_blk * 2 * nkv + 2 * nkv == w_in.shape[1]
    return [pl.BlockSpec((d, nq), lambda i: (0, q_blk)), pl.BlockSpec((d, 2 * nkv), lambda i: (0, kv_blk))]


def _swa_outputs(rows, rt):
    nq = N_HEADS * HEAD_DIM
    nkv = N_KV * HEAD_DIM
    specs = [pl.BlockSpec((rt, nq), lambda i: (i, 0)),
             pl.BlockSpec((rt, nkv), lambda i: (i, 0)),
             pl.BlockSpec((rt, nkv), lambda i: (i, 0))]
    shapes = [jax.ShapeDtypeStruct((rows, nq), BF16),
              jax.ShapeDtypeStruct((rows, nkv), F32),
              jax.ShapeDtypeStruct((rows, nkv), F32)]
    return specs, shapes


def _swa_cached_mixer(h2, w_in, k_cache, v_cache, bias, sinks, tq):
    rows, d = h2.shape
    nkv = N_KV * HEAD_DIM
    nw = N_HEADS * HEAD_DIM + 2 * nkv
    n_blocks = ROW_TILE // tq
    cache_spec = pl.BlockSpec((n_blocks, WINDOW, nkv), lambda i: (i, 0, 0))
    out_specs, out_shape = _swa_outputs(rows, ROW_TILE)
    return pl.pallas_call(
        functools.partial(_swa_cached_kernel, tq=tq),
        grid=(rows // ROW_TILE,),
        in_specs=[pl.BlockSpec((ROW_TILE, d), lambda i: (i, 0))] + _swa_weight_specs(w_in, d)
        + [cache_spec, cache_spec,
           pl.BlockSpec((N_HEADS * tq, 2 * WINDOW), lambda i: (0, 0)),
           pl.BlockSpec((N_HEADS * tq, 1), lambda i: (0, 0))],
        out_specs=out_specs,
        out_shape=out_shape,
        scratch_shapes=[pltpu.VMEM((d, nw), BF16)],
        compiler_params=_cparams(1),
        name="swa_cached",
    )(h2, w_in, w_in, k_cache, v_cache, bias, sinks)


def _swa_stream_mixer(h2, w_in, bias_t, sinks_t, blocks_per_seq):
    rows, d = h2.shape
    nkv = N_KV * HEAD_DIM
    nw = N_HEADS * HEAD_DIM + 2 * nkv
    lanes = PAIRS_PER_KV * WINDOW
    rt = min(rows, MATMUL_TILE)
    out_specs, out_shape = _swa_outputs(rows, rt)
    return pl.pallas_call(
        functools.partial(_swa_stream_kernel, blocks_per_seq=blocks_per_seq),
        grid=(rows // rt,),
        in_specs=[pl.BlockSpec((rt, d), lambda i: (i, 0))] + _swa_weight_specs(w_in, d)
        + [pl.BlockSpec((2, N_KV, 2, 2 * WINDOW, lanes), lambda i: (0, 0, 0, 0, 0)),
           pl.BlockSpec((N_KV, 2, 1, lanes), lambda i: (0, 0, 0, 0))],
        out_specs=out_specs,
        out_shape=out_shape,
        scratch_shapes=[pltpu.VMEM((d, nw), BF16), pltpu.VMEM((WINDOW, nkv), F32),
                        pltpu.VMEM((nkv, WINDOW), F32)],
        compiler_params=_cparams(1),
        name="swa_stream",
    )(h2, w_in, w_in, bias_t, sinks_t)


def _t5_bucket(dist):
    max_exact = N_BUCKETS // 2
    dd = np.maximum(dist, 1)
    large = max_exact + (np.log(dd / max_exact) / np.log(WINDOW / max_exact)
                         * (N_BUCKETS - max_exact)).astype(np.int64)
    large = np.minimum(large, N_BUCKETS - 1)
    return np.where(dist < max_exact, dist, large).astype(np.int32)


def _attention_bias(rel_bias):
    by_dist = jnp.take(rel_bias.astype(F32), _t5_bucket(np.arange(WINDOW)), axis=0).T
    neg = jnp.full((N_HEADS, WINDOW), NEG_INF, F32)
    line = jnp.concatenate([neg, by_dist[:, ::-1], neg[:, :WINDOW - 1]], axis=1)
    return jnp.stack([line[:, WINDOW - 1 - q:3 * WINDOW - 1 - q] for q in range(WINDOW)], axis=1)


def kernel(x_prompt, x_sample, state_pool, state_conv, cache_swa_k, cache_swa_v, c_prompt, c_sample, w_ada, b_ada, norm_g, final_norm_g, w_in_even, w_out_even, w_pool, pool_scale, conv_w, w_in_odd, w_out_odd, gm_norm_g, gm_norm_b, gm_w_s, gm_b_s, attn_sinks, rel_bias, w_router, b_router, w_gate, w_up, w_down):
    d = D_MODEL
    bp, tp, _ = x_prompt.shape
    bs, ts, _ = x_sample.shape
    rows_s = bs * ts
    assert rows_s == ROW_TILE and tp % ROUTER_TILE == 0 and PAST_LEN % CHUNK == 0
    assert bp <= V7X_SUBLANES and CHUNK % ts == 0

    n_c = bp + bs
    c_pad = (-n_c) % V7X_SUBLANES
    c_all = jnp.concatenate([c_prompt, c_sample, jnp.zeros((c_pad, d), F32)], axis=0)
    mod_p = _adaln(c_all, w_ada, b_ada)
    mod_s = jnp.repeat(mod_p[:, bp:bp + bs], ts, axis=1)

    xp = x_prompt.reshape(bp * tp, d)
    xs_ = x_sample.reshape(rows_s, d)
    w_in0, w_in1 = w_in_even[0], w_in_odd[0]

    zero_pool = jnp.zeros((bp, POOL_STATE, POOL_WIDTH), F32)
    zero_conv = jnp.zeros((bp, CONV_K - 1, CONV_WIDTH), F32)
    hp0, ya_p, pool_p = _pool_mixer(xp, norm_g[0, 0], mod_p, 0, w_in0, w_pool[0], pool_scale[0],
                                    zero_pool, 1, MATMUL_TILE, 0)
    hs0, ya_s, pool_s = _pool_mixer(xs_, norm_g[0, 0], mod_s, 0, w_in0, w_pool[0], pool_scale[0],
                                    state_pool[0], bs, ts, PAST_LEN)
    yb_p, conv_p = _conv_mixer(hp0, w_in0, conv_w[0], zero_conv, 1, MATMUL_TILE)
    yb_s, conv_s = _conv_mixer(hs0, w_in0, conv_w[0], state_conv[0], bs, ts)
    out_p = _outproj(ya_p, yb_p, xp, mod_p, 0, norm_g[0, 1], w_out_even[0], tp)
    out_s = _outproj(ya_s, yb_s, xs_, mod_s, 0, norm_g[0, 1], w_out_even[0], rows_s)
    (x2p, h1p), (x2s, h1s) = _moe(out_p, out_s, mod_p, mod_s, 0, norm_g[1, 0], tp,
                                  w_router, b_router, w_gate, w_up, w_down, final=False)

    bs_t = gm_b_s[0].T
    (yc_p,) = _gmlp_mixer(h1p, w_in1, gm_norm_g[0], gm_norm_b[0], gm_w_s[0], bs_t, CHUNK, CHUNK, False)
    yc_s, gv_s = _gmlp_mixer(h1s, w_in1, gm_norm_g[0], gm_norm_b[0], gm_w_s[0], bs_t, rows_s, ts, True)
    bias = _attention_bias(rel_bias)
    nkv = N_KV * HEAD_DIM
    bias_t = jnp.transpose(bias.reshape(N_KV, PAIRS_PER_KV, 2, WINDOW, 2 * WINDOW), (0, 2, 4, 1, 3))
    bias_t = bias_t.reshape(N_KV, 2, 2 * WINDOW, PAIRS_PER_KV * WINDOW)
    before_start = (np.arange(2 * WINDOW) < WINDOW)[None, None, :, None]
    bias_t = jnp.stack([bias_t, jnp.where(before_start, NEG_INF, bias_t)], axis=0)
    sinks_t = jnp.transpose(attn_sinks[0].reshape(N_KV, PAIRS_PER_KV, 2), (0, 2, 1))
    sinks_t = jnp.repeat(sinks_t, WINDOW, axis=-1).reshape(N_KV, 2, 1, PAIRS_PER_KV * WINDOW)
    yd_p, k_p, v_p = _swa_stream_mixer(h1p, w_in1, bias_t, sinks_t, tp // WINDOW)
    yd_s, k_s, v_s = _swa_cached_mixer(h1s, w_in1, cache_swa_k[0].reshape(bs, WINDOW, nkv),
                                       cache_swa_v[0].reshape(bs, WINDOW, nkv),
                                       bias[:, :ts, :].reshape(N_HEADS * ts, 2 * WINDOW),
                                       jnp.repeat(attn_sinks[0], ts).reshape(-1, 1), ts)
    out_p = _outproj(yc_p, yd_p, x2p, mod_p, 1, norm_g[1, 1], w_out_odd[0], tp)
    out_s = _outproj(yc_s, yd_s, x2s, mod_s, 1, norm_g[1, 1], w_out_odd[0], rows_s)
    (yp,), (ys_out,) = _moe(out_p, out_s, mod_p, mod_s, 1, final_norm_g, tp,
                            w_router, b_router, w_gate, w_up, w_down, final=True)

    k_p4 = k_p.reshape(bp, tp, nkv)[:, -WINDOW:].reshape(bp, WINDOW, N_KV, HEAD_DIM)
    v_p4 = v_p.reshape(bp, tp, nkv)[:, -WINDOW:].reshape(bp, WINDOW, N_KV, HEAD_DIM)
    k_s4 = jnp.concatenate([cache_swa_k[0], k_s.reshape(bs, ts, N_KV, HEAD_DIM)], axis=1)[:, -WINDOW:]
    v_s4 = jnp.concatenate([cache_swa_v[0], v_s.reshape(bs, ts, N_KV, HEAD_DIM)], axis=1)[:, -WINDOW:]
    return (yp.reshape(bp, tp, d), ys_out.reshape(bs, ts, d),
            pool_p[None], pool_s[None], conv_p[None], conv_s[None],
            k_p4[None], k_s4[None], v_p4[None], v_s4[None],
            gv_s.reshape(bs, ts, GM_WIDTH)[None])
```

```python
import functools

import numpy as np
import jax
import jax.numpy as jnp
from jax import lax
from jax.experimental import pallas as pl
from jax.experimental.pallas import tpu as pltpu

F32 = jnp.float32
BF16 = jnp.bfloat16
I32 = jnp.int32
U32 = jnp.uint32

D_MODEL = 2048
POOL_WINDOWS = (2, 4, 8, 16)
POOL_WIDTH = 1024
POOL_GROUP = 256
POOL_STATE = 15
CONV_WIDTH = 1024
CONV_K = 3
GM_WIDTH = 1024
GM_GROUPS = 8
GM_GROUP = 128
CHUNK = 128
HEAD_DIM = 64
N_HEADS = 16
N_KV = 2
WINDOW = 128
N_BUCKETS = 32
N_EXPERTS = 16
N_EXPERT_GROUPS = 4
EXP_PER_GROUP = 4
TOP_K = 2
EPS = 1e-6
NEG_INF = -1e30
PAST_LEN = 16384

V7X_SUBLANES = 8
V7X_LANES = 128
VMEM_LIMIT = 56 * 1024 * 1024

ROW_TILE = 256
MATMUL_TILE = 512
ROUTER_TILE = 1024
POOL_HALO = 16
CONV_HALO = 8
MOE_TILE = 256
TAB_EXPERT, TAB_VALID, TAB_LAST_TILE, TAB_NUSED, TAB_NEXT = 0, 1, 2, 3, 4


def _cparams(n_axes):
    return pltpu.CompilerParams(dimension_semantics=("arbitrary",) * n_axes,
                                vmem_limit_bytes=VMEM_LIMIT)


def _rms(x, g):
    return x * lax.rsqrt(jnp.mean(x * x, axis=-1, keepdims=True) + EPS) * g


def _mod_spec(mod, layer, part):
    nrow = ROW_TILE if mod.shape[1] == ROW_TILE else V7X_SUBLANES
    return pl.BlockSpec((1, nrow, D_MODEL), lambda *_: (layer, 0, part))


def _mod_rows(m_ref, seq):
    if m_ref.shape[1] == V7X_SUBLANES:
        return m_ref[0, pl.ds(seq, 1), :]
    return m_ref[0]


def _adaln_kernel(c_ref, w_ref, b_ref, o_ref):
    c = c_ref[...]
    a = (c * jax.nn.sigmoid(c)).astype(BF16)
    o_ref[0] = jnp.dot(a, w_ref[0].astype(BF16), preferred_element_type=F32) + b_ref[0]


def _adaln(c_all, w_ada, b_ada):
    depth, d, n6 = w_ada.shape
    m = c_all.shape[0]
    tn = 1024
    return pl.pallas_call(
        _adaln_kernel,
        grid=(depth, n6 // tn),
        in_specs=[pl.BlockSpec((m, d), lambda l, j: (0, 0)),
                  pl.BlockSpec((1, d, tn), lambda l, j: (l, 0, j)),
                  pl.BlockSpec((1, 1, tn), lambda l, j: (l, 0, j))],
        out_specs=pl.BlockSpec((1, m, tn), lambda l, j: (l, 0, j)),
        out_shape=jax.ShapeDtypeStruct((depth, m, n6), F32),
        compiler_params=_cparams(2),
        name="adaln",
    )(c_all, w_ada, b_ada.reshape(depth, 1, n6))


def _pool_kernel(x_ref, g_ref, sc_ref, sh_ref, w_ref, wp_ref, ps_ref, st_ref, h_ref, ya_ref, ns_ref,
                 wbf, wpbf, carry, *, nb, tm, tiles_per_seq, start):
    i = pl.program_id(0)
    t = i % tiles_per_seq
    seq = i // tiles_per_seq
    c = POOL_WIDTH
    halo = POOL_HALO

    @pl.when(i == 0)
    def _():
        wbf[...] = w_ref[...].astype(BF16)
        wpbf[...] = wp_ref[...].astype(BF16)

    @pl.when(t == 0)
    def _():
        carry[...] = st_ref[...]

    h = (_rms(x_ref[...], g_ref[...]) * (1.0 + _mod_rows(sc_ref, seq)) + _mod_rows(sh_ref, seq)).astype(BF16)
    h_ref[...] = h
    p = jnp.dot(h, wbf[...], preferred_element_type=F32)
    p3 = p.reshape(nb, tm, c)
    ext3 = jnp.concatenate([carry[...], p3], axis=1)
    tail = ext3[:, tm:tm + halo, :]
    ns_ref[...] = tail
    carry[...] = tail
    ext = ext3.reshape(nb * (halo + tm), c)
    pos = start + t * tm + lax.broadcasted_iota(I32, (1, tm, 1), 1)
    outs = []
    for gi, w in enumerate(POOL_WINDOWS):
        sl = slice(gi * POOL_GROUP, (gi + 1) * POOL_GROUP)
        acc = ext[:, sl]
        shift = 1
        while shift < w:
            acc = acc + pltpu.roll(acc, shift, 0)
            shift *= 2
        win = acc.reshape(nb, halo + tm, POOL_GROUP)[:, halo:, :]
        cnt = jnp.minimum(pos + 1, w).astype(F32)
        dgrp = win / cnt - p3[:, :, sl]
        outs.append(jnp.dot(dgrp.reshape(nb * tm, POOL_GROUP).astype(BF16), wpbf[gi],
                            preferred_element_type=F32))
    y = jnp.concatenate(outs, axis=-1) * ps_ref[...]
    ya_ref[...] = y.astype(BF16)


def _pool_mixer(x2, g, mod, layer, w_in, w_pool, pool_scale, state, nb, tm, start):
    rows, d = x2.shape
    nseq = state.shape[0]
    tiles_per_seq = (rows // nseq) // tm
    seq_blocks = nseq // nb
    c = POOL_WIDTH
    st = jnp.pad(state, ((0, 0), (POOL_HALO - POOL_STATE, 0), (0, 0)))
    kern = functools.partial(_pool_kernel, nb=nb, tm=tm, tiles_per_seq=tiles_per_seq, start=start)
    h2, ya, ns = pl.pallas_call(
        kern,
        grid=(seq_blocks * tiles_per_seq,),
        in_specs=[pl.BlockSpec((nb * tm, d), lambda i: (i, 0)),
                  pl.BlockSpec((1, d), lambda i: (0, 0)),
                  _mod_spec(mod, layer, 1), _mod_spec(mod, layer, 0),
                  pl.BlockSpec((d, c), lambda i: (0, 0)),
                  pl.BlockSpec((len(POOL_WINDOWS), POOL_GROUP, POOL_GROUP), lambda i: (0, 0, 0)),
                  pl.BlockSpec((1, c), lambda i: (0, 0)),
                  pl.BlockSpec((nb, POOL_HALO, c), lambda i: (i // tiles_per_seq, 0, 0))],
        out_specs=[pl.BlockSpec((nb * tm, d), lambda i: (i, 0)),
                   pl.BlockSpec((nb * tm, c), lambda i: (i, 0)),
                   pl.BlockSpec((nb, POOL_HALO, c), lambda i: (i // tiles_per_seq, 0, 0))],
        out_shape=[jax.ShapeDtypeStruct((rows, d), BF16),
                   jax.ShapeDtypeStruct((rows, c), BF16),
                   jax.ShapeDtypeStruct((nseq, POOL_HALO, c), F32)],
        scratch_shapes=[pltpu.VMEM((d, c), BF16),
                        pltpu.VMEM((len(POOL_WINDOWS), POOL_GROUP, POOL_GROUP), BF16),
                        pltpu.VMEM((nb, POOL_HALO, c), F32)],
        compiler_params=_cparams(1),
        name="pool_mixer",
    )(x2, g.reshape(1, d), mod, mod, w_in, w_pool, pool_scale.reshape(1, c), st)
    return h2, ya, ns[:, POOL_HALO - POOL_STATE:, :]


def _conv_kernel(h_ref, wx_ref, wb_ref, wc_ref, cw_ref, st_ref, yb_ref, ns_ref,
                 wxbf, wbbf, wcbf, carry, *, nb, tm, tiles_per_seq):
    i = pl.program_id(1)
    t = i % tiles_per_seq
    tc = wxbf.shape[1]
    halo = CONV_HALO

    @pl.when(i == 0)
    def _():
        wxbf[...] = wx_ref[...].astype(BF16)
        wbbf[...] = wb_ref[...].astype(BF16)
        wcbf[...] = wc_ref[...].astype(BF16)

    @pl.when(t == 0)
    def _():
        carry[...] = st_ref[...]

    h = h_ref[...]
    xin = jnp.dot(h, wxbf[...], preferred_element_type=F32)
    gb = jnp.dot(h, wbbf[...], preferred_element_type=F32)
    gc = jnp.dot(h, wcbf[...], preferred_element_type=F32)
    z3 = (gc * xin).reshape(nb, tm, tc)
    ext3 = jnp.concatenate([carry[...], z3], axis=1)
    tail = ext3[:, tm:tm + halo, :]
    ns_ref[...] = tail
    carry[...] = tail
    ext = ext3.reshape(nb * (halo + tm), tc)
    cw = cw_ref[...]
    conv = cw[0:1, :] * pltpu.roll(ext, 2, 0) + cw[1:2, :] * pltpu.roll(ext, 1, 0) + cw[2:3, :] * ext
    conv = conv.reshape(nb, halo + tm, tc)[:, halo:, :].reshape(nb * tm, tc)
    yb_ref[...] = (gb * conv).astype(BF16)


def _conv_mixer(h2, w_in, conv_w, state, nb, tm):
    rows, d = h2.shape
    nseq = state.shape[0]
    tiles_per_seq = (rows // nseq) // tm
    seq_blocks = nseq // nb
    c = CONV_WIDTH
    tc = 512
    cb = c // tc
    base = POOL_WIDTH // tc
    st = jnp.pad(state, ((0, 0), (CONV_HALO - (CONV_K - 1), 0), (0, 0)))
    kern = functools.partial(_conv_kernel, nb=nb, tm=tm, tiles_per_seq=tiles_per_seq)
    yb, ns = pl.pallas_call(
        kern,
        grid=(cb, seq_blocks * tiles_per_seq),
        in_specs=[pl.BlockSpec((nb * tm, d), lambda j, i: (i, 0)),
                  pl.BlockSpec((d, tc), lambda j, i: (0, base + j)),
                  pl.BlockSpec((d, tc), lambda j, i: (0, base + cb + j)),
                  pl.BlockSpec((d, tc), lambda j, i: (0, base + 2 * cb + j)),
                  pl.BlockSpec((CONV_K, tc), lambda j, i: (0, j)),
                  pl.BlockSpec((nb, CONV_HALO, tc), lambda j, i: (i // tiles_per_seq, 0, j))],
        out_specs=[pl.BlockSpec((nb * tm, tc), lambda j, i: (i, j)),
                   pl.BlockSpec((nb, CONV_HALO, tc), lambda j, i: (i // tiles_per_seq, 0, j))],
        out_shape=[jax.ShapeDtypeStruct((rows, c), BF16),
                   jax.ShapeDtypeStruct((nseq, CONV_HALO, c), F32)],
        scratch_shapes=[pltpu.VMEM((d, tc), BF16)] * 3 + [pltpu.VMEM((nb, CONV_HALO, tc), F32)],
        compiler_params=_cparams(2),
        name="conv_mixer",
    )(h2, w_in, w_in, w_in, conv_w, st)
    return yb, ns[:, CONV_HALO - (CONV_K - 1):, :]


def _pack_bf16_pairs(v):
    c = v.shape[1] // 2
    return pltpu.bitcast(pltpu.pack_elementwise([v[:, :c], v[:, c:]], packed_dtype=BF16), U32)


def _store_token_tiles(ref, v):
    rows = v.shape[0]
    for j in range(V7X_SUBLANES):
        ref[pl.ds(j, rows, stride=V7X_SUBLANES), :] = v[:, j * V7X_LANES:(j + 1) * V7X_LANES]


def _load_token_tiles(ref):
    rows = ref.shape[0] // V7X_SUBLANES
    return jnp.concatenate([ref[pl.ds(j, rows, stride=V7X_SUBLANES), :] for j in range(V7X_SUBLANES)],
                           axis=-1)


def _unpack_pairs_f32(w):
    return tuple(pltpu.unpack_elementwise(w, index=k, packed_dtype=BF16, unpacked_dtype=F32) for k in range(2))


def _unpack_bf16_pairs(w):
    lo, hi = _unpack_pairs_f32(w)
    return lo.astype(BF16), hi.astype(BF16)


def _outproj_kernel(ya_ref, yb_ref, x_ref, g1_ref, sc_ref, sh_ref, ng_ref, wo_ref,
                    x1_ref, hp_ref, hpt_ref, wobf, *, tiles_per_seq):
    i = pl.program_id(0)
    seq = i // tiles_per_seq

    @pl.when(i == 0)
    def _():
        wobf[...] = wo_ref[...].astype(BF16)

    ycat = jnp.concatenate([ya_ref[...], yb_ref[...]], axis=-1)
    y = jnp.dot(ycat, wobf[...], preferred_element_type=F32)
    x1 = x_ref[...] + _mod_rows(g1_ref, seq) * y
    x1_ref[...] = x1
    h2 = _rms(x1, ng_ref[...]) * (1.0 + _mod_rows(sc_ref, seq)) + _mod_rows(sh_ref, seq)
    packed = _pack_bf16_pairs(h2)
    hp_ref[...] = packed
    _store_token_tiles(hpt_ref, packed)


def _outproj(ya, yb, x2, mod, layer, ng, w_out, seq_rows):
    rows_all, d = x2.shape
    half = ya.shape[1]
    rt = ROW_TILE
    row_spec = lambda w: pl.BlockSpec((rt, w), lambda i: (i, 0))
    return pl.pallas_call(
        functools.partial(_outproj_kernel, tiles_per_seq=seq_rows // rt),
        grid=(rows_all // rt,),
        in_specs=[row_spec(half), row_spec(half), row_spec(d),
                  _mod_spec(mod, layer, 2), _mod_spec(mod, layer, 4), _mod_spec(mod, layer, 3),
                  pl.BlockSpec((1, d), lambda i: (0, 0)),
                  pl.BlockSpec((d, d), lambda i: (0, 0), pipeline_mode=pl.Buffered(1))],
        out_specs=[row_spec(d), row_spec(d // 2),
                   pl.BlockSpec((rt * V7X_SUBLANES, V7X_LANES), lambda i: (i, 0))],
        out_shape=[jax.ShapeDtypeStruct((rows_all, d), F32),
                   jax.ShapeDtypeStruct((rows_all, d // 2), U32),
                   jax.ShapeDtypeStruct((rows_all * V7X_SUBLANES, V7X_LANES), U32)],
        scratch_shapes=[pltpu.VMEM((d, d), BF16)],
        compiler_params=_cparams(1),
        name="outproj",
    )(ya, yb, x2, mod, mod, mod, ng.reshape(1, d), w_out)


def _router_kernel(hpp_ref, hps_ref, wr_ref, br_ref, pos_ref, rw_ref, tab_ref,
                   cnt_acc, totals, starts, padded, before_ref, s_all, sel_all, *, nt_p, rows_s):
    ph = pl.program_id(0)
    t = pl.program_id(1)
    last = nt_p
    r = hpp_ref.shape[0]
    half = hpp_ref.shape[1]
    ne = N_EXPERTS
    sub = lax.broadcasted_iota(I32, (ne, V7X_LANES), 0)

    @pl.when(t == 0)
    def _():
        cnt_acc[...] = jnp.zeros_like(cnt_acc)

    @pl.when((ph == 0) & (t == 0))
    def _():
        starts[...] = jnp.zeros_like(starts)
        padded[...] = jnp.zeros_like(padded)

    @pl.when((ph == 1) & (t == 0))
    def _():
        pad = jnp.floor((totals[...] + (MOE_TILE - 1.0)) * (1.0 / MOE_TILE)) * MOE_TILE
        run = pad
        k = 1
        while k < ne:
            run = run + jnp.where(sub >= k, pltpu.roll(run, k, 0), 0.0)
            k *= 2
        padded[...] = pad
        starts[...] = run - pad

    is_s = t == last
    eid = lax.broadcasted_iota(I32, (ne, r), 0)
    n_valid = jnp.where(is_s, rows_s, r)
    tok = lax.broadcasted_iota(I32, (ne, r), 1)

    @pl.when(ph == 0)
    def _():
        w_s = jnp.concatenate([hps_ref[...], jnp.zeros((r - rows_s, half), U32)], axis=0)
        w = jnp.where(is_s, w_s, hpp_ref[...])
        lo, hi = _unpack_bf16_pairs(w)
        wr = wr_ref[...].astype(BF16)
        log_t = (lax.dot_general(wr[:, :half], lo, NT_DIMS, preferred_element_type=F32)
                 + lax.dot_general(wr[:, half:], hi, NT_DIMS, preferred_element_type=F32))

        s = jax.nn.sigmoid(log_t)
        sg = s + br_ref[...]
        within = eid % EXP_PER_GROUP
        grp = eid // EXP_PER_GROUP

        def group_rot(x, k):
            return jnp.where(within + k < EXP_PER_GROUP,
                             pltpu.roll(x, ne - k, 0), pltpu.roll(x, EXP_PER_GROUP - k, 0))

        rank = jnp.zeros((ne, r), I32)
        for k in range(1, EXP_PER_GROUP):
            mate = group_rot(sg, k)
            wrapped = within + k >= EXP_PER_GROUP
            ahead = (mate > sg) | (wrapped & (mate == sg))
            rank = rank + ahead.astype(I32)
        top2 = rank < TOP_K
        kept = jnp.where(top2, sg, 0.0)
        gscore = kept
        for k in range(1, EXP_PER_GROUP):
            gscore = gscore + group_rot(kept, k)
        win = None
        for k in range(1, N_EXPERT_GROUPS):
            other = pltpu.roll(gscore, EXP_PER_GROUP * k, 0)
            beats = (gscore > other) | ((grp < k) & (gscore == other))
            win = beats if win is None else (win & beats)
        picked_now = top2 & win & (tok < n_valid)
        s_all[t] = s
        sel_all[t] = picked_now.astype(F32)

    s = s_all[t]
    selb = sel_all[t]
    sel = selb > 0.5
    cnt_before = cnt_acc[...]
    cnt_new = cnt_before + jnp.sum(selb, axis=1, keepdims=True)
    cnt_acc[...] = cnt_new

    @pl.when((ph == 0) & (t == 0))
    def _():
        src = lax.broadcasted_iota(I32, (r, r), 0)
        dst = lax.broadcasted_iota(I32, (r, r), 1)
        before_ref[...] = (src < dst).astype(BF16)

    @pl.when((ph == 0) & (t == last))
    def _():
        totals[...] = cnt_new

    @pl.when(ph == 1)
    def _():
        picked = jnp.where(sel, s, 0.0)
        wsum = jnp.sum(picked, axis=0, keepdims=True)
        gate = picked / jnp.where(tok[0:1, :] < n_valid, wsum, 1.0)
        ranks = jnp.dot(selb.astype(BF16), before_ref[...], preferred_element_type=F32)
        slot = (starts[...][:, 0:1] + cnt_before[:, 0:1] + ranks).astype(I32)
        e_a = jnp.min(jnp.where(sel, eid, ne), axis=0, keepdims=True)
        e_b = jnp.max(jnp.where(sel, eid, -1), axis=0, keepdims=True)
        is_a = sel & (eid == e_a)
        is_b = sel & (eid == e_b)
        pos_a = jnp.sum(jnp.where(is_a, slot, 0), axis=0, keepdims=True)
        pos_b = jnp.sum(jnp.where(is_b, slot, 0), axis=0, keepdims=True)
        w_a = jnp.sum(jnp.where(is_a, gate, 0.0), axis=0, keepdims=True)
        w_b = jnp.sum(jnp.where(is_b, gate, 0.0), axis=0, keepdims=True)
        pos_ref[0] = jnp.concatenate([pos_a, pos_b], axis=0)
        wmat = jnp.concatenate([w_a, w_b, jnp.zeros((V7X_LANES - 2, r), F32)], axis=0)
        rw_ref[...] = wmat.T

    @pl.when((ph == 1) & (t == last))
    def _():
        ends = starts[...] + padded[...]
        lane = lax.broadcasted_iota(I32, (ne, V7X_LANES), 1)
        tile_start = (lane * MOE_TILE).astype(F32)
        te = jnp.sum((tile_start >= ends).astype(I32), axis=0, keepdims=True)
        valid = te < ne
        last_e = jnp.max(jnp.where(padded[...] > 0.0, sub, 0), axis=0, keepdims=True)
        te = jnp.where(valid, te, last_e)
        n_used = jnp.sum(valid.astype(I32), axis=1, keepdims=True) + jnp.zeros((1, V7X_LANES), I32)
        last_tile = jnp.where(padded[...] > 0.0, ends - MOE_TILE, -1.0).astype(I32)
        last_tile_row = jnp.sum(jnp.where(sub == lane, last_tile, 0), axis=0, keepdims=True)
        later = jnp.min(jnp.where((sub > te) & (padded[...] > 0.0), sub, ne), axis=0, keepdims=True)
        next_e = jnp.where(later < ne, later, -1)
        zero = jnp.zeros((1, V7X_LANES), I32)
        tab_ref[...] = jnp.concatenate([te, valid.astype(I32), last_tile_row, n_used, next_e,
                                        zero, zero, zero], axis=0)


def _router(hp_p, hp_s, w_router, b_router):
    n_p, half = hp_p.shape
    rows_s = hp_s.shape[0]
    r = ROUTER_TILE
    nt_p = n_p // r
    nt = nt_p + 1
    kern = functools.partial(_router_kernel, nt_p=nt_p, rows_s=rows_s)
    pos, rw, tab = pl.pallas_call(
        kern,
        grid=(2, nt),
        in_specs=[pl.BlockSpec((r, half), lambda p, t: (jnp.minimum(t, nt_p - 1) * (1 - p), 0)),
                  pl.BlockSpec((rows_s, half), lambda p, t: (0, 0)),
                  pl.BlockSpec((N_EXPERTS, 2 * half), lambda p, t: (0, 0)),
                  pl.BlockSpec((N_EXPERTS, 1), lambda p, t: (0, 0))],
        out_specs=[pl.BlockSpec((1, TOP_K, r), lambda p, t: (p * t, 0, 0)),
                   pl.BlockSpec((r, V7X_LANES), lambda p, t: (p * t, 0)),
                   pl.BlockSpec((V7X_SUBLANES, V7X_LANES), lambda p, t: (0, 0))],
        out_shape=[jax.ShapeDtypeStruct((nt, TOP_K, r), I32),
                   jax.ShapeDtypeStruct((nt * r, V7X_LANES), F32),
                   jax.ShapeDtypeStruct((V7X_SUBLANES, V7X_LANES), I32)],
        scratch_shapes=[pltpu.VMEM((N_EXPERTS, V7X_LANES), F32)] * 4 + [pltpu.VMEM((r, r), BF16)]
        + [pltpu.VMEM((nt, N_EXPERTS, r), F32)] * 2,
        compiler_params=_cparams(2),
        name="router",
    )(hp_p, hp_s, w_router.T, b_router.reshape(N_EXPERTS, 1))
    return pos.reshape(-1), rw, tab.reshape(-1)


def _pos_index(tok0):
    return (tok0 // ROUTER_TILE) * (TOP_K * ROUTER_TILE) + tok0 % ROUTER_TILE


def _tokens(ref, first, n=1):
    start = pl.multiple_of(first * V7X_SUBLANES, V7X_SUBLANES)
    return ref.at[pl.ds(start, n * V7X_SUBLANES), :]


def _dispatch_kernel(pos_ref, tab_ref, hpp_ref, hps_ref, xs_ref, zbuf, sem, *, n_p_steps):
    i = pl.program_id(0)
    rows = hpp_ref.shape[0] // V7X_SUBLANES

    @pl.when(i == 0)
    def _():
        zbuf[...] = jnp.zeros_like(zbuf)

        def fill(e):
            first = pl.multiple_of(tab_ref[TAB_LAST_TILE * V7X_LANES + e], MOE_TILE)
            return pltpu.make_async_copy(zbuf, _tokens(xs_ref, first, MOE_TILE), sem)

        for e in range(N_EXPERTS):
            @pl.when(tab_ref[TAB_LAST_TILE * V7X_LANES + e] >= 0)
            def _():
                fill(e).start()
        for e in range(N_EXPERTS):
            @pl.when(tab_ref[TAB_LAST_TILE * V7X_LANES + e] >= 0)
            def _():
                fill(e).wait()

        def tail(j):
            first = pl.multiple_of(j * MOE_TILE, MOE_TILE)
            return pltpu.make_async_copy(zbuf, _tokens(xs_ref, first, MOE_TILE), sem)

        def tail_start(j, carry):
            tail(j).start()
            return carry

        def tail_wait(j, carry):
            tail(j).wait()
            return carry

        n_used = tab_ref[TAB_NUSED * V7X_LANES]
        n_tiles = xs_ref.shape[0] // (MOE_TILE * V7X_SUBLANES)
        lax.fori_loop(n_used, n_tiles, tail_start, 0)
        lax.fori_loop(n_used, n_tiles, tail_wait, 0)

    def scatter(src_ref, tok0):
        n = src_ref.shape[0] // V7X_SUBLANES
        base = _pos_index(tok0)

        def row_copy(r, dst):
            return pltpu.make_async_copy(_tokens(src_ref, r), _tokens(xs_ref, dst), sem)

        def issue(r, carry):
            row_copy(r, pos_ref[base + r]).start()
            row_copy(r, pos_ref[base + ROUTER_TILE + r]).start(priority=1)
            return carry

        lax.fori_loop(0, n, issue, 0, unroll=8)
        block = pltpu.make_async_copy(src_ref, _tokens(xs_ref, 0, n), sem)
        for _ in range(TOP_K):
            block.wait()

    @pl.when(i < n_p_steps)
    def _():
        scatter(hpp_ref, i * rows)

    @pl.when(i == n_p_steps)
    def _():
        scatter(hps_ref, n_p_steps * rows)


def _dispatch(pos, tab, hpt_p, hpt_s, n_rows_sorted):
    sub = V7X_SUBLANES
    n_p_steps = hpt_p.shape[0] // (MATMUL_TILE * sub)
    kern = functools.partial(_dispatch_kernel, n_p_steps=n_p_steps)
    blk = (MATMUL_TILE * sub, V7X_LANES)
    return pl.pallas_call(
        kern,
        grid_spec=pltpu.PrefetchScalarGridSpec(
            num_scalar_prefetch=2,
            grid=(n_p_steps + 1,),
            in_specs=[pl.BlockSpec(blk, lambda i, p, t: (jnp.minimum(i, n_p_steps - 1), 0)),
                      pl.BlockSpec(hpt_s.shape, lambda i, p, t: (0, 0))],
            out_specs=pl.BlockSpec(memory_space=pl.ANY),
            scratch_shapes=[pltpu.VMEM((MOE_TILE * sub, V7X_LANES), U32), pltpu.SemaphoreType.DMA(())]),
        out_shape=jax.ShapeDtypeStruct((n_rows_sorted * sub, V7X_LANES), U32),
        compiler_params=_cparams(1),
        name="moe_dispatch",
    )(pos, tab, hpt_p, hpt_s)


def _experts_kernel(tab_ref, xs_ref, wg_hbm, wu_hbm, wd_hbm, ys_ref,
                    wg32, wu32, wd32, wgbf, wubf, wdbf, slot_ref, sems, *, layer):
    i = pl.program_id(0)
    expert = tab_ref[TAB_EXPERT * V7X_LANES + i]
    prev = tab_ref[TAB_EXPERT * V7X_LANES + jnp.maximum(i - 1, 0)]
    upcoming = tab_ref[TAB_NEXT * V7X_LANES + i]
    changed = (i == 0) | (expert != prev)

    def weight_copies(e, slot):
        return (pltpu.make_async_copy(wg_hbm.at[layer, e], wg32.at[slot], sems.at[0, slot]),
                pltpu.make_async_copy(wu_hbm.at[layer, e], wu32.at[slot], sems.at[1, slot]),
                pltpu.make_async_copy(wd_hbm.at[layer, e], wd32.at[slot], sems.at[2, slot]))

    @pl.when(i == 0)
    def _():
        slot_ref[0] = 0
        for cp in weight_copies(expert, 0):
            cp.start()

    @pl.when(changed & (i > 0))
    def _():
        slot_ref[0] = 1 - slot_ref[0]

    def mlp(wg, wu, wd):
        x = jnp.concatenate(_unpack_bf16_pairs(_load_token_tiles(xs_ref)), axis=-1)
        a = jnp.dot(x, wg, preferred_element_type=F32)
        b = jnp.dot(x, wu, preferred_element_type=F32)
        hid = (a * jax.nn.sigmoid(a)) * b
        y = jnp.dot(hid.astype(BF16), wd, preferred_element_type=F32)
        _store_token_tiles(ys_ref, _pack_bf16_pairs(y))

    for slot in range(2):
        @pl.when(changed & (slot_ref[0] == slot))
        def _():
            for cp in weight_copies(expert, slot):
                cp.wait()

            @pl.when(upcoming >= 0)
            def _():
                for cp in weight_copies(upcoming, 1 - slot):
                    cp.start(priority=1)

            wg = wg32[slot].astype(BF16)
            wu = wu32[slot].astype(BF16)
            wd = wd32[slot].astype(BF16)
            wgbf[...] = wg
            wubf[...] = wu
            wdbf[...] = wd
            mlp(wg, wu, wd)

    valid = tab_ref[TAB_VALID * V7X_LANES + i] > 0

    @pl.when(valid & jnp.logical_not(changed))
    def _():
        mlp(wgbf[...], wubf[...], wdbf[...])

    @pl.when(jnp.logical_not(valid))
    def _():
        ys_ref[...] = jnp.zeros_like(ys_ref)


def _experts(tab, xs, w_gate, w_up, w_down, layer):
    sub = V7X_SUBLANES
    _, _, d, f = w_gate.shape
    nt = xs.shape[0] // (MOE_TILE * sub)
    assert nt <= V7X_LANES and d == 2 * sub * V7X_LANES
    blk = (MOE_TILE * sub, V7X_LANES)

    def tile(i, tab_ref):
        return jnp.minimum(i, tab_ref[TAB_NUSED * V7X_LANES] - 1)

    hbm = pl.BlockSpec(memory_space=pl.ANY)
    return pl.pallas_call(
        functools.partial(_experts_kernel, layer=layer),
        grid_spec=pltpu.PrefetchScalarGridSpec(
            num_scalar_prefetch=1,
            grid=(nt,),
            in_specs=[pl.BlockSpec(blk, lambda i, t: (tile(i, t), 0)), hbm, hbm, hbm],
            out_specs=pl.BlockSpec(blk, lambda i, t: (i, 0)),
            scratch_shapes=[pltpu.VMEM((2, d, f), F32), pltpu.VMEM((2, d, f), F32), pltpu.VMEM((2, f, d), F32),
                            pltpu.VMEM((d, f), BF16), pltpu.VMEM((d, f), BF16), pltpu.VMEM((f, d), BF16),
                            pltpu.SMEM((1,), I32), pltpu.SemaphoreType.DMA((3, 2))]),
        out_shape=jax.ShapeDtypeStruct(xs.shape, U32),
        compiler_params=_cparams(1),
        name="moe_experts",
    )(tab, xs, w_gate, w_up, w_down)


def _combine_kernel(pos_ref, ys_ref, x1_ref, rw_ref, g2_ref, ng_ref, sc_ref, sh_ref, *rest,
                    tok0, tiles_per_seq, n_steps, final):
    if final:
        x2_ref, buf0, buf1, sems = rest
        hn_ref = None
    else:
        x2_ref, hn_ref, buf0, buf1, sems = rest
    bufs = (buf0, buf1)
    i = pl.program_id(0)
    seq = i // tiles_per_seq
    rows = x1_ref.shape[0]

    def gather(step, slot):
        base = _pos_index(tok0 + step * rows)

        def issue(r, carry):
            for k in range(TOP_K):
                src = pos_ref[base + k * ROUTER_TILE + r]
                pltpu.make_async_copy(_tokens(ys_ref, src), _tokens(bufs[slot].at[k], r),
                                      sems.at[slot]).start(priority=k)
            return carry

        lax.fori_loop(0, rows, issue, 0, unroll=8)

    def drain(slot):
        for k in range(TOP_K):
            pltpu.make_async_copy(_tokens(ys_ref, 0, rows), bufs[slot].at[k], sems.at[slot]).wait()

    def finish(slot):
        rw = rw_ref[...]
        lo_a, hi_a = _unpack_pairs_f32(_load_token_tiles(bufs[slot].at[0]))
        lo_b, hi_b = _unpack_pairs_f32(_load_token_tiles(bufs[slot].at[1]))
        w_a = rw[:, 0:1]
        w_b = rw[:, 1:2]
        moe = jnp.concatenate([w_a * lo_a + w_b * lo_b, w_a * hi_a + w_b * hi_b], axis=-1)
        x2 = x1_ref[...] + _mod_rows(g2_ref, seq) * moe
        if final:
            x2_ref[...] = _rms(x2, ng_ref[...])
        else:
            x2_ref[...] = x2
            hn_ref[...] = (_rms(x2, ng_ref[...]) * (1.0 + _mod_rows(sc_ref, seq))
                           + _mod_rows(sh_ref, seq)).astype(BF16)

    @pl.when(i == 0)
    def _():
        gather(0, 0)

    for slot in range(2):
        @pl.when(i % 2 == slot)
        def _():
            @pl.when(i + 1 < n_steps)
            def _():
                gather(i + 1, 1 - slot)

            drain(slot)
            finish(slot)


def _combine(pos, ys, x1, rw, tok0, mod, layer, ng, seq_rows, final):
    n_tok, d = x1.shape
    rt = ROW_TILE
    assert tok0 % rt == 0 and ROUTER_TILE % rt == 0
    kern = functools.partial(_combine_kernel, tok0=tok0, tiles_per_seq=seq_rows // rt,
                             n_steps=n_tok // rt, final=final)
    rw_off = tok0 // rt
    row_spec = lambda w: pl.BlockSpec((rt, w), lambda i, p: (i, 0))
    out_shape = [jax.ShapeDtypeStruct((n_tok, d), F32)]
    out_specs = [row_spec(d)]
    if not final:
        out_shape.append(jax.ShapeDtypeStruct((n_tok, d), BF16))
        out_specs.append(row_spec(d))
    nxt = min(layer + 1, mod.shape[0] - 1)
    return pl.pallas_call(
        kern,
        grid_spec=pltpu.PrefetchScalarGridSpec(
            num_scalar_prefetch=1,
            grid=(n_tok // rt,),
            in_specs=[pl.BlockSpec(memory_space=pl.ANY), row_spec(d),
                      pl.BlockSpec((rt, V7X_LANES), lambda i, p: (rw_off + i, 0)),
                      _mod_spec(mod, layer, 5), pl.BlockSpec((1, d), lambda i, p: (0, 0)),
                      _mod_spec(mod, nxt, 1), _mod_spec(mod, nxt, 0)],
            out_specs=out_specs,
            scratch_shapes=[pltpu.VMEM((TOP_K, rt * V7X_SUBLANES, V7X_LANES), U32)] * 2
            + [pltpu.SemaphoreType.DMA((2,))]),
        out_shape=out_shape,
        compiler_params=_cparams(1),
        name="moe_combine_final" if final else "moe_combine",
    )(pos, ys, x1, rw, mod, ng.reshape(1, d), mod, mod)


def _moe(out_p, out_s, mod_p, mod_s, layer, ng, seq_rows_p, w_router, b_router, w_gate, w_up, w_down, final):
    x1_p, hp_p, hpt_p = out_p
    x1_s, hp_s, hpt_s = out_s
    n_p = x1_p.shape[0]
    n_tok = n_p + x1_s.shape[0]
    max_rows = TOP_K * n_tok + N_EXPERTS * (MOE_TILE - 1)
    n_rows_sorted = -(-max_rows // MOE_TILE) * MOE_TILE
    pos, rw, tab = _router(hp_p, hp_s, w_router, b_router)
    xs = _dispatch(pos, tab, hpt_p, hpt_s, n_rows_sorted)
    ys = _experts(tab, xs, w_gate, w_up, w_down, layer)
    out_p = _combine(pos, ys, x1_p, rw, 0, mod_p, layer, ng, seq_rows_p, final)
    out_s = _combine(pos, ys, x1_s, rw, n_p, mod_s, layer, ng, x1_s.shape[0], final)
    return out_p, out_s


def _gmlp_kernel(h_ref, w_ref, lg_ref, lb_ref, ws_ref, bs_ref, yc_ref, *rest, ell, blk, emit_v):
    if emit_v:
        gv_ref, wbf, wsbf = rest
    else:
        wbf, wsbf = rest
    i = pl.program_id(0)
    rows = h_ref.shape[0]
    c = GM_WIDTH

    @pl.when(i == 0)
    def _():
        wbf[...] = w_ref[...].astype(BF16)
        r = lax.broadcasted_iota(I32, (ell, ell), 0)
        s = lax.broadcasted_iota(I32, (ell, ell), 1)
        keep = (r >= s) & ((r // blk) == (s // blk))
        rsel = (lax.broadcasted_iota(I32, (ell, CHUNK), 0) % blk
                == lax.broadcasted_iota(I32, (ell, CHUNK), 1)).astype(BF16)
        csel = (lax.broadcasted_iota(I32, (CHUNK, ell), 1) % blk
                == lax.broadcasted_iota(I32, (CHUNK, ell), 0)).astype(BF16)
        for g in range(GM_GROUPS):
            wchunk = ws_ref[g].astype(BF16)
            if blk == ell:
                full = wchunk
            else:
                rowsp = jnp.dot(rsel, wchunk, preferred_element_type=F32).astype(BF16)
                full = jnp.dot(rowsp, csel, preferred_element_type=F32).astype(BF16)
            wsbf[g] = jnp.where(keep, full, jnp.zeros_like(full))

    uv = jnp.dot(h_ref[...], wbf[...], preferred_element_type=F32)
    u = uv[:, :c]
    v = uv[:, c:]
    vc = v - jnp.mean(v, axis=-1, keepdims=True)
    vn = vc * lax.rsqrt(jnp.mean(vc * vc, axis=-1, keepdims=True) + EPS) * lg_ref[...] + lb_ref[...]
    if emit_v:
        gv_ref[...] = vn
    vb = vn.astype(BF16)
    bs = bs_ref[...]
    for ch in range(rows // ell):
        rs = slice(ch * ell, (ch + 1) * ell)
        outs = []
        for g in range(GM_GROUPS):
            cs = slice(g * GM_GROUP, (g + 1) * GM_GROUP)
            mixed = jnp.dot(wsbf[g], vb[rs, cs], preferred_element_type=F32)
            mixed = (mixed.reshape(ell // blk, blk, GM_GROUP) + bs[:blk, g:g + 1][None]).reshape(ell, GM_GROUP)
            outs.append(u[rs, cs] * mixed)
        yc_ref[rs, :] = jnp.concatenate(outs, axis=-1).astype(BF16)


def _gmlp_mixer(h2, w_in, ln_g, ln_b, ws, bs_t, ell, blk, emit_v):
    rows, d = h2.shape
    c = GM_WIDTH
    kern = functools.partial(_gmlp_kernel, ell=ell, blk=blk, emit_v=emit_v)
    rt = min(rows, MATMUL_TILE)
    out_specs = [pl.BlockSpec((rt, c), lambda i: (i, 0))]
    out_shape = [jax.ShapeDtypeStruct((rows, c), BF16)]
    if emit_v:
        out_specs.append(pl.BlockSpec((rt, c), lambda i: (i, 0)))
        out_shape.append(jax.ShapeDtypeStruct((rows, c), F32))
    return pl.pallas_call(
        kern,
        grid=(rows // rt,),
        in_specs=[pl.BlockSpec((rt, d), lambda i: (i, 0)),
                  pl.BlockSpec((d, 2 * c), lambda i: (0, 0), pipeline_mode=pl.Buffered(1)),
                  pl.BlockSpec((1, c), lambda i: (0, 0)),
                  pl.BlockSpec((1, c), lambda i: (0, 0)),
                  pl.BlockSpec((GM_GROUPS, CHUNK, CHUNK), lambda i: (0, 0, 0)),
                  pl.BlockSpec((CHUNK, GM_GROUPS), lambda i: (0, 0))],
        out_specs=out_specs,
        out_shape=out_shape,
        scratch_shapes=[pltpu.VMEM((d, 2 * c), BF16), pltpu.VMEM((GM_GROUPS, ell, ell), BF16)],
        compiler_params=_cparams(1),
        name="gmlp_mixer",
    )(h2, w_in, ln_g.reshape(1, c), ln_b.reshape(1, c), ws, bs_t)


PAIR_W = 2 * HEAD_DIM
PAIRS_PER_KV = N_HEADS // N_KV // 2
NT_DIMS = (((1,), (1,)), ((), ()))
SCORE_SCALE = HEAD_DIM ** -0.5
assert float(np.log2(SCORE_SCALE)).is_integer()


def _swa_project(i, h_ref, wq_ref, wkv_ref, wbf):
    nq = N_HEADS * HEAD_DIM

    @pl.when(i == 0)
    def _():
        wbf[:, :nq] = wq_ref[...].astype(BF16)
        wbf[:, nq:] = wkv_ref[...].astype(BF16)

    return jnp.dot(h_ref[...], wbf[...], preferred_element_type=F32)


def _pair_block_diag(a, a_swapped, hk, axis):
    dim_axis = 1 - axis
    low = lax.broadcasted_iota(I32, a.shape, dim_axis) < HEAD_DIM
    lo, hi = (a, a_swapped) if hk == 0 else (a_swapped, a)
    return jnp.concatenate([jnp.where(low, lo, 0.0), jnp.where(low, 0.0, hi)], axis=axis).astype(BF16)


def _stack_pairs(qkv, rs, hk, scale=None):
    p0 = hk * PAIRS_PER_KV
    q = jnp.concatenate([qkv[rs, (p0 + pp) * PAIR_W:(p0 + pp + 1) * PAIR_W]
                         for pp in range(PAIRS_PER_KV)], axis=0)
    return (q if scale is None else q * scale).astype(BF16)


def _swa_cached_kernel(h_ref, wq_ref, wkv_ref, kp_ref, vp_ref, bias_ref, sink_ref, yd_ref, k_ref, v_ref,
                       wbf, *, tq):
    i = pl.program_id(0)
    rows = h_ref.shape[0]
    nq = N_HEADS * HEAD_DIM
    nkv = N_KV * HEAD_DIM
    n_blocks = rows // tq
    qkv = _swa_project(i, h_ref, wq_ref, wkv_ref, wbf)
    k_new = qkv[:, nq:nq + nkv]
    v_new = qkv[:, nq + nkv:]
    k_ref[...] = k_new
    v_ref[...] = v_new
    pad = jnp.zeros((WINDOW - tq, nkv), F32)

    scores, vbds = [], []
    for blk in range(n_blocks):
        rs = slice(blk * tq, (blk + 1) * tq)
        kcat = jnp.concatenate([kp_ref[blk], k_new[rs], pad], axis=0)
        vcat = jnp.concatenate([vp_ref[blk], v_new[rs], pad], axis=0)
        kswap = pltpu.roll(kcat, HEAD_DIM, 1)
        vswap = pltpu.roll(vcat, HEAD_DIM, 1)
        per_head = []
        for hk in range(N_KV):
            kbd = _pair_block_diag(kcat, kswap, hk, 0)
            vbds.append(_pair_block_diag(vcat, vswap, hk, 0))
            s4 = lax.dot_general(_stack_pairs(qkv, rs, hk), kbd, NT_DIMS,
                                 preferred_element_type=F32) * (HEAD_DIM ** -0.5)
            for pp in range(PAIRS_PER_KV):
                for sub in range(2):
                    per_head.append(s4[pp * tq:(pp + 1) * tq, sub * 2 * WINDOW:(sub + 1) * 2 * WINDOW])
        scores.append(jnp.concatenate(per_head, axis=0))

    s_all = jnp.stack(scores, axis=0) + bias_ref[...][None]
    sink = sink_ref[...][None]
    m = jnp.maximum(jnp.max(s_all, axis=-1, keepdims=True), sink)
    pr = jnp.exp(s_all - m)
    pr = pr / (jnp.sum(pr, axis=-1, keepdims=True) + jnp.exp(sink - m))

    for blk in range(n_blocks):
        outs = []
        for hk in range(N_KV):
            p4 = []
            for pp in range(PAIRS_PER_KV):
                h0 = 2 * (hk * PAIRS_PER_KV + pp)
                p4.append(jnp.concatenate([pr[blk, h0 * tq:(h0 + 1) * tq, :],
                                           pr[blk, (h0 + 1) * tq:(h0 + 2) * tq, :]], axis=-1))
            o4 = jnp.dot(jnp.concatenate(p4, axis=0).astype(BF16), vbds[blk * N_KV + hk],
                         preferred_element_type=F32)
            outs.extend(o4[pp * tq:(pp + 1) * tq, :] for pp in range(PAIRS_PER_KV))
        yd_ref[blk * tq:(blk + 1) * tq, :] = jnp.concatenate(outs, axis=-1).astype(BF16)


def _swa_stream_kernel(h_ref, wq_ref, wkv_ref, bias_ref, sink_ref, yd_ref, k_ref, v_ref,
                       wbf, kprev, vprev_t, *, blocks_per_seq):
    i = pl.program_id(0)
    rows = h_ref.shape[0]
    nq = N_HEADS * HEAD_DIM
    nkv = N_KV * HEAD_DIM
    tq = WINDOW
    n_blocks = rows // tq

    @pl.when(i == 0)
    def _():
        kprev[...] = jnp.zeros_like(kprev)
        vprev_t[...] = jnp.zeros_like(vprev_t)

    qkv = _swa_project(i, h_ref, wq_ref, wkv_ref, wbf)
    k_new = qkv[:, nq:nq + nkv]
    v_new = qkv[:, nq + nkv:]
    k_ref[...] = k_new
    v_ref[...] = v_new
    v_new_t = v_new.T
    lanes = PAIRS_PER_KV * tq

    for blk in range(n_blocks):
        rs = slice(blk * tq, (blk + 1) * tq)
        first = ((i * n_blocks + blk) % blocks_per_seq == 0).astype(I32)
        k_cur = k_new[rs]
        v_cur_t = v_new_t[:, rs]
        kcat = jnp.concatenate([kprev[...], k_cur], axis=0)
        vcat_t = jnp.concatenate([vprev_t[...], v_cur_t], axis=1)
        kprev[...] = k_cur
        vprev_t[...] = v_cur_t
        kswap = pltpu.roll(kcat, HEAD_DIM, 1)
        vswap_t = pltpu.roll(vcat_t, HEAD_DIM, 0)
        outs = []
        for hk in range(N_KV):
            kbd = _pair_block_diag(kcat, kswap, hk, 0)
            vbd_t = _pair_block_diag(vcat_t, vswap_t, hk, 1)
            st = lax.dot_general(kbd, _stack_pairs(qkv, rs, hk, SCORE_SCALE), NT_DIMS,
                                 preferred_element_type=F32)
            s3 = st.reshape(2, 2 * WINDOW, lanes) + bias_ref[first, hk]
            sink = sink_ref[hk]
            m = jnp.maximum(jnp.max(s3, axis=1, keepdims=True), sink)
            pr = jnp.exp(s3 - m)
            inv = 1.0 / (jnp.sum(pr, axis=1, keepdims=True) + jnp.exp(sink - m))
            o_t = jnp.dot(vbd_t, pr.reshape(4 * WINDOW, lanes).astype(BF16),
                          preferred_element_type=F32)
            norm = jnp.concatenate([jnp.broadcast_to(inv[sub], (HEAD_DIM, lanes)) for sub in range(2)], axis=0)
            o4 = (o_t * norm).T
            outs.extend(o4[pp * tq:(pp + 1) * tq, :] for pp in range(PAIRS_PER_KV))
        yd_ref[rs, :] = jnp.concatenate(outs, axis=-1).astype(BF16)


def _swa_weight_specs(w_in, d):
    nq = N_HEADS * HEAD_DIM
    nkv = N_KV * HEAD_DIM
    nw = nq + 2 * nkv
    q_blk = (w_in.shape[1] - nw) // nq
    kv_blk = (w_in.shape[1] - 2 * nkv) // (2 * nkv)
    assert q_blk * nq + nw == w_in.shape[1] and kv_blk * 2 * nkv + 2 * nkv == w_in.shape[1]
    return [pl.BlockSpec((d, nq), lambda i: (0, q_blk)), pl.BlockSpec((d, 2 * nkv), lambda i: (0, kv_blk))]


def _swa_outputs(rows, rt):
    nq = N_HEADS * HEAD_DIM
    nkv = N_KV * HEAD_DIM
    specs = [pl.BlockSpec((rt, nq), lambda i: (i, 0)),
             pl.BlockSpec((rt, nkv), lambda i: (i, 0)),
             pl.BlockSpec((rt, nkv), lambda i: (i, 0))]
    shapes = [jax.ShapeDtypeStruct((rows, nq), BF16),
              jax.ShapeDtypeStruct((rows, nkv), F32),
              jax.ShapeDtypeStruct((rows, nkv), F32)]
    return specs, shapes


def _swa_cached_mixer(h2, w_in, k_cache, v_cache, bias, sinks, tq):
    rows, d = h2.shape
    nkv = N_KV * HEAD_DIM
    nw = N_HEADS * HEAD_DIM + 2 * nkv
    n_blocks = ROW_TILE // tq
    cache_spec = pl.BlockSpec((n_blocks, WINDOW, nkv), lambda i: (i, 0, 0))
    out_specs, out_shape = _swa_outputs(rows, ROW_TILE)
    return pl.pallas_call(
        functools.partial(_swa_cached_kernel, tq=tq),
        grid=(rows // ROW_TILE,),
        in_specs=[pl.BlockSpec((ROW_TILE, d), lambda i: (i, 0))] + _swa_weight_specs(w_in, d)
        + [cache_spec, cache_spec,
           pl.BlockSpec((N_HEADS * tq, 2 * WINDOW), lambda i: (0, 0)),
           pl.BlockSpec((N_HEADS * tq, 1), lambda i: (0, 0))],
        out_specs=out_specs,
        out_shape=out_shape,
        scratch_shapes=[pltpu.VMEM((d, nw), BF16)],
        compiler_params=_cparams(1),
        name="swa_cached",
    )(h2, w_in, w_in, k_cache, v_cache, bias, sinks)


def _swa_stream_mixer(h2, w_in, bias_t, sinks_t, blocks_per_seq):
    rows, d = h2.shape
    nkv = N_KV * HEAD_DIM
    nw = N_HEADS * HEAD_DIM + 2 * nkv
    lanes = PAIRS_PER_KV * WINDOW
    rt = min(rows, MATMUL_TILE)
    out_specs, out_shape = _swa_outputs(rows, rt)
    return pl.pallas_call(
        functools.partial(_swa_stream_kernel, blocks_per_seq=blocks_per_seq),
        grid=(rows // rt,),
        in_specs=[pl.BlockSpec((rt, d), lambda i: (i, 0))] + _swa_weight_specs(w_in, d)
        + [pl.BlockSpec((2, N_KV, 2, 2 * WINDOW, lanes), lambda i: (0, 0, 0, 0, 0)),
           pl.BlockSpec((N_KV, 2, 1, lanes), lambda i: (0, 0, 0, 0))],
        out_specs=out_specs,
        out_shape=out_shape,
        scratch_shapes=[pltpu.VMEM((d, nw), BF16), pltpu.VMEM((WINDOW, nkv), F32),
                        pltpu.VMEM((nkv, WINDOW), F32)],
        compiler_params=_cparams(1),
        name="swa_stream",
    )(h2, w_in, w_in, bias_t, sinks_t)


def _t5_bucket(dist):
    max_exact = N_BUCKETS // 2
    dd = np.maximum(dist, 1)
    large = max_exact + (np.log(dd / max_exact) / np.log(WINDOW / max_exact)
                         * (N_BUCKETS - max_exact)).astype(np.int64)
    large = np.minimum(large, N_BUCKETS - 1)
    return np.where(dist < max_exact, dist, large).astype(np.int32)


def _attention_bias(rel_bias):
    by_dist = jnp.take(rel_bias.astype(F32), _t5_bucket(np.arange(WINDOW)), axis=0).T
    neg = jnp.full((N_HEADS, WINDOW), NEG_INF, F32)
    line = jnp.concatenate([neg, by_dist[:, ::-1], neg[:, :WINDOW - 1]], axis=1)
    rows = line[:, None, :]
    span = 1
    while span < WINDOW:
        rows = jnp.concatenate([rows[:, :, span:], rows[:, :, :rows.shape[2] - span]], axis=1)
        span *= 2
    return rows


def kernel(x_prompt, x_sample, state_pool, state_conv, cache_swa_k, cache_swa_v, c_prompt, c_sample, w_ada, b_ada, norm_g, final_norm_g, w_in_even, w_out_even, w_pool, pool_scale, conv_w, w_in_odd, w_out_odd, gm_norm_g, gm_norm_b, gm_w_s, gm_b_s, attn_sinks, rel_bias, w_router, b_router, w_gate, w_up, w_down):
    d = D_MODEL
    bp, tp, _ = x_prompt.shape
    bs, ts, _ = x_sample.shape
    rows_s = bs * ts
    assert rows_s == ROW_TILE and tp % ROUTER_TILE == 0 and PAST_LEN % CHUNK == 0
    assert bp <= V7X_SUBLANES and CHUNK % ts == 0

    n_c = bp + bs
    c_pad = (-n_c) % V7X_SUBLANES
    c_all = jnp.concatenate([c_prompt, c_sample, jnp.zeros((c_pad, d), F32)], axis=0)
    mod_p = _adaln(c_all, w_ada, b_ada)
    mod_s = jnp.repeat(mod_p[:, bp:bp + bs], ts, axis=1)

    xp = x_prompt.reshape(bp * tp, d)
    xs_ = x_sample.reshape(rows_s, d)
    w_in0, w_in1 = w_in_even[0], w_in_odd[0]

    zero_pool = jnp.zeros((bp, POOL_STATE, POOL_WIDTH), F32)
    zero_conv = jnp.zeros((bp, CONV_K - 1, CONV_WIDTH), F32)
    hp0, ya_p, pool_p = _pool_mixer(xp, norm_g[0, 0], mod_p, 0, w_in0, w_pool[0], pool_scale[0],
                                    zero_pool, 1, MATMUL_TILE, 0)
    hs0, ya_s, pool_s = _pool_mixer(xs_, norm_g[0, 0], mod_s, 0, w_in0, w_pool[0], pool_scale[0],
                                    state_pool[0], bs, ts, PAST_LEN)
    yb_p, conv_p = _conv_mixer(hp0, w_in0, conv_w[0], zero_conv, 1, MATMUL_TILE)
    yb_s, conv_s = _conv_mixer(hs0, w_in0, conv_w[0], state_conv[0], bs, ts)
    out_p = _outproj(ya_p, yb_p, xp, mod_p, 0, norm_g[0, 1], w_out_even[0], tp)
    out_s = _outproj(ya_s, yb_s, xs_, mod_s, 0, norm_g[0, 1], w_out_even[0], rows_s)
    (x2p, h1p), (x2s, h1s) = _moe(out_p, out_s, mod_p, mod_s, 0, norm_g[1, 0], tp,
                                  w_router, b_router, w_gate, w_up, w_down, final=False)

    bs_t = gm_b_s[0].T
    (yc_p,) = _gmlp_mixer(h1p, w_in1, gm_norm_g[0], gm_norm_b[0], gm_w_s[0], bs_t, CHUNK, CHUNK, False)
    yc_s, gv_s = _gmlp_mixer(h1s, w_in1, gm_norm_g[0], gm_norm_b[0], gm_w_s[0], bs_t, rows_s, ts, True)
    bias = _attention_bias(rel_bias)
    nkv = N_KV * HEAD_DIM
    bias_t = jnp.transpose(bias.reshape(N_KV, PAIRS_PER_KV, 2, WINDOW, 2 * WINDOW), (0, 2, 4, 1, 3))
    bias_t = bias_t.reshape(N_KV, 2, 2 * WINDOW, PAIRS_PER_KV * WINDOW)
    before_start = (np.arange(2 * WINDOW) < WINDOW)[None, None, :, None]
    bias_t = jnp.stack([bias_t, jnp.where(before_start, NEG_INF, bias_t)], axis=0)
    sinks_t = jnp.transpose(attn_sinks[0].reshape(N_KV, PAIRS_PER_KV, 2), (0, 2, 1))
    sinks_t = jnp.repeat(sinks_t, WINDOW, axis=-1).reshape(N_KV, 2, 1, PAIRS_PER_KV * WINDOW)
    yd_p, k_p, v_p = _swa_stream_mixer(h1p, w_in1, bias_t, sinks_t, tp // WINDOW)
    yd_s, k_s, v_s = _swa_cached_mixer(h1s, w_in1, cache_swa_k[0].reshape(bs, WINDOW, nkv),
                                       cache_swa_v[0].reshape(bs, WINDOW, nkv),
                                       bias[:, :ts, :].reshape(N_HEADS * ts, 2 * WINDOW),
                                       jnp.repeat(attn_sinks[0], ts).reshape(-1, 1), ts)
    out_p = _outproj(yc_p, yd_p, x2p, mod_p, 1, norm_g[1, 1], w_out_odd[0], tp)
    out_s = _outproj(yc_s, yd_s, x2s, mod_s, 1, norm_g[1, 1], w_out_odd[0], rows_s)
    (yp,), (ys_out,) = _moe(out_p, out_s, mod_p, mod_s, 1, final_norm_g, tp,
                            w_router, b_router, w_gate, w_up, w_down, final=True)

    k_p4 = k_p.reshape(bp, tp, nkv)[:, -WINDOW:].reshape(bp, WINDOW, N_KV, HEAD_DIM)
    v_p4 = v_p.reshape(bp, tp, nkv)[:, -WINDOW:].reshape(bp, WINDOW, N_KV, HEAD_DIM)
    k_s4 = jnp.concatenate([cache_swa_k[0], k_s.reshape(bs, ts, N_KV, HEAD_DIM)], axis=1)[:, -WINDOW:]
    v_s4 = jnp.concatenate([cache_swa_v[0], v_s.reshape(bs, ts, N_KV, HEAD_DIM)], axis=1)[:, -WINDOW:]
    return (yp.reshape(bp, tp, d), ys_out.reshape(bs, ts, d),
            pool_p[None], pool_s[None], conv_p[None], conv_s[None],
            k_p4[None], k_s4[None], v_p4[None], v_s4[None],
            gv_s.reshape(bs, ts, GM_WIDTH)[None])
```

```python
import functools

import numpy as np
import jax
import jax.numpy as jnp
from jax import lax
from jax.experimental import pallas as pl
from jax.experimental.pallas import tpu as pltpu

F32 = jnp.float32
BF16 = jnp.bfloat16
I32 = jnp.int32
U32 = jnp.uint32

D_MODEL = 2048
POOL_WINDOWS = (2, 4, 8, 16)
POOL_WIDTH = 1024
POOL_GROUP = 256
POOL_STATE = 15
CONV_WIDTH = 1024
CONV_K = 3
GM_WIDTH = 1024
GM_GROUPS = 8
GM_GROUP = 128
CHUNK = 128
HEAD_DIM = 64
N_HEADS = 16
N_KV = 2
WINDOW = 128
N_BUCKETS = 32
N_EXPERTS = 16
N_EXPERT_GROUPS = 4
EXP_PER_GROUP = 4
TOP_K = 2
EPS = 1e-6
NEG_INF = -1e30
PAST_LEN = 16384

V7X_SUBLANES = 8
V7X_LANES = 128
VMEM_LIMIT = 56 * 1024 * 1024

ROW_TILE = 256
MATMUL_TILE = 512
ROUTER_TILE = 1024
POOL_HALO = 16
CONV_HALO = 8
MOE_TILE = 256
ADALN_COL_TILE = 1024
CONV_COL_TILE = 512
TAB_EXPERT, TAB_VALID, TAB_LAST_TILE, TAB_NUSED, TAB_NEXT = 0, 1, 2, 3, 4


def _cparams(n_axes):
    return pltpu.CompilerParams(dimension_semantics=("arbitrary",) * n_axes,
                                vmem_limit_bytes=VMEM_LIMIT)


def _rms(x, g):
    return x * lax.rsqrt(jnp.mean(x * x, axis=-1, keepdims=True) + EPS) * g


def _mod_spec(mod, layer, part):
    nrow = ROW_TILE if mod.shape[1] == ROW_TILE else V7X_SUBLANES
    return pl.BlockSpec((1, nrow, D_MODEL), lambda *_: (layer, 0, part))


def _mod_rows(m_ref, seq):
    if m_ref.shape[1] == V7X_SUBLANES:
        return m_ref[0, pl.ds(seq, 1), :]
    return m_ref[0]


def _adaln_kernel(c_ref, w_ref, b_ref, o_ref):
    c = c_ref[...]
    a = (c * jax.nn.sigmoid(c)).astype(BF16)
    o_ref[0] = jnp.dot(a, w_ref[0].astype(BF16), preferred_element_type=F32) + b_ref[0]


def _adaln(c_all, w_ada, b_ada):
    depth, d, n6 = w_ada.shape
    m = c_all.shape[0]
    tn = ADALN_COL_TILE
    return pl.pallas_call(
        _adaln_kernel,
        grid=(depth, n6 // tn),
        in_specs=[pl.BlockSpec((m, d), lambda l, j: (0, 0)),
                  pl.BlockSpec((1, d, tn), lambda l, j: (l, 0, j)),
                  pl.BlockSpec((1, 1, tn), lambda l, j: (l, 0, j))],
        out_specs=pl.BlockSpec((1, m, tn), lambda l, j: (l, 0, j)),
        out_shape=jax.ShapeDtypeStruct((depth, m, n6), F32),
        compiler_params=_cparams(2),
        name="adaln",
    )(c_all, w_ada, b_ada.reshape(depth, 1, n6))


def _pool_kernel(x_ref, g_ref, sc_ref, sh_ref, w_ref, wp_ref, ps_ref, st_ref, h_ref, ya_ref, ns_ref,
                 wbf, wpbf, carry, *, nb, tm, tiles_per_seq, start):
    i = pl.program_id(0)
    t = i % tiles_per_seq
    seq = i // tiles_per_seq
    c = POOL_WIDTH
    halo = POOL_HALO

    @pl.when(i == 0)
    def _():
        wbf[...] = w_ref[...].astype(BF16)
        wpbf[...] = wp_ref[...].astype(BF16)

    @pl.when(t == 0)
    def _():
        carry[...] = st_ref[...]

    h = (_rms(x_ref[...], g_ref[...]) * (1.0 + _mod_rows(sc_ref, seq)) + _mod_rows(sh_ref, seq)).astype(BF16)
    h_ref[...] = h
    p = jnp.dot(h, wbf[...], preferred_element_type=F32)
    p3 = p.reshape(nb, tm, c)
    ext3 = jnp.concatenate([carry[...], p3], axis=1)
    tail = ext3[:, tm:tm + halo, :]
    ns_ref[...] = tail
    carry[...] = tail
    ext = ext3.reshape(nb * (halo + tm), c)
    pos = start + t * tm + lax.broadcasted_iota(I32, (1, tm, 1), 1)
    outs = []
    for gi, w in enumerate(POOL_WINDOWS):
        sl = slice(gi * POOL_GROUP, (gi + 1) * POOL_GROUP)
        acc = ext[:, sl]
        shift = 1
        while shift < w:
            acc = acc + pltpu.roll(acc, shift, 0)
            shift *= 2
        win = acc.reshape(nb, halo + tm, POOL_GROUP)[:, halo:, :]
        cnt = jnp.minimum(pos + 1, w).astype(F32)
        dgrp = win / cnt - p3[:, :, sl]
        outs.append(jnp.dot(dgrp.reshape(nb * tm, POOL_GROUP).astype(BF16), wpbf[gi],
                            preferred_element_type=F32))
    y = jnp.concatenate(outs, axis=-1) * ps_ref[...]
    ya_ref[...] = y.astype(BF16)


def _pool_mixer(x2, g, mod, layer, w_in, w_pool, pool_scale, state, nb, tm, start):
    rows, d = x2.shape
    nseq = state.shape[0]
    tiles_per_seq = (rows // nseq) // tm
    seq_blocks = nseq // nb
    c = POOL_WIDTH
    st = jnp.pad(state, ((0, 0), (POOL_HALO - POOL_STATE, 0), (0, 0)))
    kern = functools.partial(_pool_kernel, nb=nb, tm=tm, tiles_per_seq=tiles_per_seq, start=start)
    h2, ya, ns = pl.pallas_call(
        kern,
        grid=(seq_blocks * tiles_per_seq,),
        in_specs=[pl.BlockSpec((nb * tm, d), lambda i: (i, 0)),
                  pl.BlockSpec((1, d), lambda i: (0, 0)),
                  _mod_spec(mod, layer, 1), _mod_spec(mod, layer, 0),
                  pl.BlockSpec((d, c), lambda i: (0, 0)),
                  pl.BlockSpec((len(POOL_WINDOWS), POOL_GROUP, POOL_GROUP), lambda i: (0, 0, 0)),
                  pl.BlockSpec((1, c), lambda i: (0, 0)),
                  pl.BlockSpec((nb, POOL_HALO, c), lambda i: (i // tiles_per_seq, 0, 0))],
        out_specs=[pl.BlockSpec((nb * tm, d), lambda i: (i, 0)),
                   pl.BlockSpec((nb * tm, c), lambda i: (i, 0)),
                   pl.BlockSpec((nb, POOL_HALO, c), lambda i: (i // tiles_per_seq, 0, 0))],
        out_shape=[jax.ShapeDtypeStruct((rows, d), BF16),
                   jax.ShapeDtypeStruct((rows, c), BF16),
                   jax.ShapeDtypeStruct((nseq, POOL_HALO, c), F32)],
        scratch_shapes=[pltpu.VMEM((d, c), BF16),
                        pltpu.VMEM((len(POOL_WINDOWS), POOL_GROUP, POOL_GROUP), BF16),
                        pltpu.VMEM((nb, POOL_HALO, c), F32)],
        compiler_params=_cparams(1),
        name="pool_mixer",
    )(x2, g.reshape(1, d), mod, mod, w_in, w_pool, pool_scale.reshape(1, c), st)
    return h2, ya, ns[:, POOL_HALO - POOL_STATE:, :]


def _conv_kernel(h_ref, wx_ref, wb_ref, wc_ref, cw_ref, st_ref, yb_ref, ns_ref,
                 wxbf, wbbf, wcbf, carry, *, nb, tm, tiles_per_seq):
    i = pl.program_id(1)
    t = i % tiles_per_seq
    tc = wxbf.shape[1]
    halo = CONV_HALO

    @pl.when(i == 0)
    def _():
        wxbf[...] = wx_ref[...].astype(BF16)
        wbbf[...] = wb_ref[...].astype(BF16)
        wcbf[...] = wc_ref[...].astype(BF16)

    @pl.when(t == 0)
    def _():
        carry[...] = st_ref[...]

    h = h_ref[...]
    xin = jnp.dot(h, wxbf[...], preferred_element_type=F32)
    gb = jnp.dot(h, wbbf[...], preferred_element_type=F32)
    gc = jnp.dot(h, wcbf[...], preferred_element_type=F32)
    z3 = (gc * xin).reshape(nb, tm, tc)
    ext3 = jnp.concatenate([carry[...], z3], axis=1)
    tail = ext3[:, tm:tm + halo, :]
    ns_ref[...] = tail
    carry[...] = tail
    ext = ext3.reshape(nb * (halo + tm), tc)
    cw = cw_ref[...]
    conv = cw[0:1, :] * pltpu.roll(ext, 2, 0) + cw[1:2, :] * pltpu.roll(ext, 1, 0) + cw[2:3, :] * ext
    conv = conv.reshape(nb, halo + tm, tc)[:, halo:, :].reshape(nb * tm, tc)
    yb_ref[...] = (gb * conv).astype(BF16)


def _conv_mixer(h2, w_in, conv_w, state, nb, tm):
    rows, d = h2.shape
    nseq = state.shape[0]
    tiles_per_seq = (rows // nseq) // tm
    seq_blocks = nseq // nb
    c = CONV_WIDTH
    tc = CONV_COL_TILE
    cb = c // tc
    base = POOL_WIDTH // tc
    st = jnp.pad(state, ((0, 0), (CONV_HALO - (CONV_K - 1), 0), (0, 0)))
    kern = functools.partial(_conv_kernel, nb=nb, tm=tm, tiles_per_seq=tiles_per_seq)
    yb, ns = pl.pallas_call(
        kern,
        grid=(cb, seq_blocks * tiles_per_seq),
        in_specs=[pl.BlockSpec((nb * tm, d), lambda j, i: (i, 0)),
                  pl.BlockSpec((d, tc), lambda j, i: (0, base + j)),
                  pl.BlockSpec((d, tc), lambda j, i: (0, base + cb + j)),
                  pl.BlockSpec((d, tc), lambda j, i: (0, base + 2 * cb + j)),
                  pl.BlockSpec((CONV_K, tc), lambda j, i: (0, j)),
                  pl.BlockSpec((nb, CONV_HALO, tc), lambda j, i: (i // tiles_per_seq, 0, j))],
        out_specs=[pl.BlockSpec((nb * tm, tc), lambda j, i: (i, j)),
                   pl.BlockSpec((nb, CONV_HALO, tc), lambda j, i: (i // tiles_per_seq, 0, j))],
        out_shape=[jax.ShapeDtypeStruct((rows, c), BF16),
                   jax.ShapeDtypeStruct((nseq, CONV_HALO, c), F32)],
        scratch_shapes=[pltpu.VMEM((d, tc), BF16)] * 3 + [pltpu.VMEM((nb, CONV_HALO, tc), F32)],
        compiler_params=_cparams(2),
        name="conv_mixer",
    )(h2, w_in, w_in, w_in, conv_w, st)
    return yb, ns[:, CONV_HALO - (CONV_K - 1):, :]


def _pack_bf16_pairs(v):
    c = v.shape[1] // 2
    return pltpu.bitcast(pltpu.pack_elementwise([v[:, :c], v[:, c:]], packed_dtype=BF16), U32)


def _store_token_tiles(ref, v):
    rows = v.shape[0]
    for j in range(V7X_SUBLANES):
        ref[pl.ds(j, rows, stride=V7X_SUBLANES), :] = v[:, j * V7X_LANES:(j + 1) * V7X_LANES]


def _load_token_tiles(ref):
    rows = ref.shape[0] // V7X_SUBLANES
    return jnp.concatenate([ref[pl.ds(j, rows, stride=V7X_SUBLANES), :] for j in range(V7X_SUBLANES)],
                           axis=-1)


def _unpack_pairs_f32(w):
    return tuple(pltpu.unpack_elementwise(w, index=k, packed_dtype=BF16, unpacked_dtype=F32) for k in range(2))


def _unpack_bf16_pairs(w):
    lo, hi = _unpack_pairs_f32(w)
    return lo.astype(BF16), hi.astype(BF16)


def _outproj_kernel(ya_ref, yb_ref, x_ref, g1_ref, sc_ref, sh_ref, ng_ref, wo_ref,
                    x1_ref, hp_ref, hpt_ref, wobf, *, tiles_per_seq):
    i = pl.program_id(0)
    seq = i // tiles_per_seq

    @pl.when(i == 0)
    def _():
        wobf[...] = wo_ref[...].astype(BF16)

    ycat = jnp.concatenate([ya_ref[...], yb_ref[...]], axis=-1)
    y = jnp.dot(ycat, wobf[...], preferred_element_type=F32)
    x1 = x_ref[...] + _mod_rows(g1_ref, seq) * y
    x1_ref[...] = x1
    h2 = _rms(x1, ng_ref[...]) * (1.0 + _mod_rows(sc_ref, seq)) + _mod_rows(sh_ref, seq)
    packed = _pack_bf16_pairs(h2)
    hp_ref[...] = packed
    _store_token_tiles(hpt_ref, packed)


def _outproj(ya, yb, x2, mod, layer, ng, w_out, seq_rows):
    rows_all, d = x2.shape
    half = ya.shape[1]
    rt = ROW_TILE
    row_spec = lambda w: pl.BlockSpec((rt, w), lambda i: (i, 0))
    return pl.pallas_call(
        functools.partial(_outproj_kernel, tiles_per_seq=seq_rows // rt),
        grid=(rows_all // rt,),
        in_specs=[row_spec(half), row_spec(half), row_spec(d),
                  _mod_spec(mod, layer, 2), _mod_spec(mod, layer, 4), _mod_spec(mod, layer, 3),
                  pl.BlockSpec((1, d), lambda i: (0, 0)),
                  pl.BlockSpec((d, d), lambda i: (0, 0), pipeline_mode=pl.Buffered(1))],
        out_specs=[row_spec(d), row_spec(d // 2),
                   pl.BlockSpec((rt * V7X_SUBLANES, V7X_LANES), lambda i: (i, 0))],
        out_shape=[jax.ShapeDtypeStruct((rows_all, d), F32),
                   jax.ShapeDtypeStruct((rows_all, d // 2), U32),
                   jax.ShapeDtypeStruct((rows_all * V7X_SUBLANES, V7X_LANES), U32)],
        scratch_shapes=[pltpu.VMEM((d, d), BF16)],
        compiler_params=_cparams(1),
        name="outproj",
    )(ya, yb, x2, mod, mod, mod, ng.reshape(1, d), w_out)


def _router_kernel(hpp_ref, hps_ref, wr_ref, br_ref, pos_ref, rw_ref, tab_ref,
                   cnt_acc, totals, starts, padded, before_ref, s_all, sel_all, *, nt_p, rows_s):
    ph = pl.program_id(0)
    t = pl.program_id(1)
    last = nt_p
    r = hpp_ref.shape[0]
    half = hpp_ref.shape[1]
    ne = N_EXPERTS
    sub = lax.broadcasted_iota(I32, (ne, V7X_LANES), 0)

    @pl.when(t == 0)
    def _():
        cnt_acc[...] = jnp.zeros_like(cnt_acc)

    @pl.when((ph == 0) & (t == 0))
    def _():
        starts[...] = jnp.zeros_like(starts)
        padded[...] = jnp.zeros_like(padded)

    @pl.when((ph == 1) & (t == 0))
    def _():
        pad = jnp.floor((totals[...] + (MOE_TILE - 1.0)) * (1.0 / MOE_TILE)) * MOE_TILE
        run = pad
        k = 1
        while k < ne:
            run = run + jnp.where(sub >= k, pltpu.roll(run, k, 0), 0.0)
            k *= 2
        padded[...] = pad
        starts[...] = run - pad

    is_s = t == last
    eid = lax.broadcasted_iota(I32, (ne, r), 0)
    n_valid = jnp.where(is_s, rows_s, r)
    tok = lax.broadcasted_iota(I32, (ne, r), 1)

    @pl.when(ph == 0)
    def _():
        w_s = jnp.concatenate([hps_ref[...], jnp.zeros((r - rows_s, half), U32)], axis=0)
        w = jnp.where(is_s, w_s, hpp_ref[...])
        lo, hi = _unpack_bf16_pairs(w)
        wr = wr_ref[...].astype(BF16)
        log_t = (lax.dot_general(wr[:, :half], lo, NT_DIMS, preferred_element_type=F32)
                 + lax.dot_general(wr[:, half:], hi, NT_DIMS, preferred_element_type=F32))

        s = jax.nn.sigmoid(log_t)
        sg = s + br_ref[...]
        within = eid % EXP_PER_GROUP
        grp = eid // EXP_PER_GROUP

        def group_rot(x, k):
            return jnp.where(within + k < EXP_PER_GROUP,
                             pltpu.roll(x, ne - k, 0), pltpu.roll(x, EXP_PER_GROUP - k, 0))

        rank = jnp.zeros((ne, r), I32)
        for k in range(1, EXP_PER_GROUP):
            mate = group_rot(sg, k)
            wrapped = within + k >= EXP_PER_GROUP
            ahead = (mate > sg) | (wrapped & (mate == sg))
            rank = rank + ahead.astype(I32)
        top2 = rank < TOP_K
        kept = jnp.where(top2, sg, 0.0)
        gscore = kept
        for k in range(1, EXP_PER_GROUP):
            gscore = gscore + group_rot(kept, k)
        win = None
        for k in range(1, N_EXPERT_GROUPS):
            other = pltpu.roll(gscore, EXP_PER_GROUP * k, 0)
            beats = (gscore > other) | ((grp < k) & (gscore == other))
            win = beats if win is None else (win & beats)
        picked_now = top2 & win & (tok < n_valid)
        s_all[t] = s
        sel_all[t] = picked_now.astype(F32)

    s = s_all[t]
    selb = sel_all[t]
    sel = selb > 0.5
    cnt_before = cnt_acc[...]
    cnt_new = cnt_before + jnp.sum(selb, axis=1, keepdims=True)
    cnt_acc[...] = cnt_new

    @pl.when((ph == 0) & (t == 0))
    def _():
        src = lax.broadcasted_iota(I32, (r, r), 0)
        dst = lax.broadcasted_iota(I32, (r, r), 1)
        before_ref[...] = (src < dst).astype(BF16)

    @pl.when((ph == 0) & (t == last))
    def _():
        totals[...] = cnt_new

    @pl.when(ph == 1)
    def _():
        picked = jnp.where(sel, s, 0.0)
        wsum = jnp.sum(picked, axis=0, keepdims=True)
        gate = picked / jnp.where(tok[0:1, :] < n_valid, wsum, 1.0)
        ranks = jnp.dot(selb.astype(BF16), before_ref[...], preferred_element_type=F32)
        slot = (starts[...][:, 0:1] + cnt_before[:, 0:1] + ranks).astype(I32)
        e_a = jnp.min(jnp.where(sel, eid, ne), axis=0, keepdims=True)
        e_b = jnp.max(jnp.where(sel, eid, -1), axis=0, keepdims=True)
        is_a = sel & (eid == e_a)
        is_b = sel & (eid == e_b)
        pos_a = jnp.sum(jnp.where(is_a, slot, 0), axis=0, keepdims=True)
        pos_b = jnp.sum(jnp.where(is_b, slot, 0), axis=0, keepdims=True)
        w_a = jnp.sum(jnp.where(is_a, gate, 0.0), axis=0, keepdims=True)
        w_b = jnp.sum(jnp.where(is_b, gate, 0.0), axis=0, keepdims=True)
        pos_ref[0] = jnp.concatenate([pos_a, pos_b], axis=0)
        wmat = jnp.concatenate([w_a, w_b, jnp.zeros((V7X_LANES - 2, r), F32)], axis=0)
        rw_ref[...] = wmat.T

    @pl.when((ph == 1) & (t == last))
    def _():
        ends = starts[...] + padded[...]
        lane = lax.broadcasted_iota(I32, (ne, V7X_LANES), 1)
        tile_start = (lane * MOE_TILE).astype(F32)
        te = jnp.sum((tile_start >= ends).astype(I32), axis=0, keepdims=True)
        valid = te < ne
        last_e = jnp.max(jnp.where(padded[...] > 0.0, sub, 0), axis=0, keepdims=True)
        te = jnp.where(valid, te, last_e)
        n_used = jnp.sum(valid.astype(I32), axis=1, keepdims=True) + jnp.zeros((1, V7X_LANES), I32)
        last_tile = jnp.where(padded[...] > 0.0, ends - MOE_TILE, -1.0).astype(I32)
        last_tile_row = jnp.sum(jnp.where(sub == lane, last_tile, 0), axis=0, keepdims=True)
        later = jnp.min(jnp.where((sub > te) & (padded[...] > 0.0), sub, ne), axis=0, keepdims=True)
        next_e = jnp.where(later < ne, later, -1)
        zero = jnp.zeros((1, V7X_LANES), I32)
        tab_ref[...] = jnp.concatenate([te, valid.astype(I32), last_tile_row, n_used, next_e,
                                        zero, zero, zero], axis=0)


def _router(hp_p, hp_s, w_router, b_router):
    n_p, half = hp_p.shape
    rows_s = hp_s.shape[0]
    r = ROUTER_TILE
    nt_p = n_p // r
    nt = nt_p + 1
    kern = functools.partial(_router_kernel, nt_p=nt_p, rows_s=rows_s)
    pos, rw, tab = pl.pallas_call(
        kern,
        grid=(2, nt),
        in_specs=[pl.BlockSpec((r, half), lambda p, t: (jnp.minimum(t, nt_p - 1) * (1 - p), 0)),
                  pl.BlockSpec((rows_s, half), lambda p, t: (0, 0)),
                  pl.BlockSpec((N_EXPERTS, 2 * half), lambda p, t: (0, 0)),
                  pl.BlockSpec((N_EXPERTS, 1), lambda p, t: (0, 0))],
        out_specs=[pl.BlockSpec((1, TOP_K, r), lambda p, t: (p * t, 0, 0)),
                   pl.BlockSpec((r, V7X_LANES), lambda p, t: (p * t, 0)),
                   pl.BlockSpec((V7X_SUBLANES, V7X_LANES), lambda p, t: (0, 0))],
        out_shape=[jax.ShapeDtypeStruct((nt, TOP_K, r), I32),
                   jax.ShapeDtypeStruct((nt * r, V7X_LANES), F32),
                   jax.ShapeDtypeStruct((V7X_SUBLANES, V7X_LANES), I32)],
        scratch_shapes=[pltpu.VMEM((N_EXPERTS, V7X_LANES), F32)] * 4 + [pltpu.VMEM((r, r), BF16)]
        + [pltpu.VMEM((nt, N_EXPERTS, r), F32)] * 2,
        compiler_params=_cparams(2),
        name="router",
    )(hp_p, hp_s, w_router.T, b_router.reshape(N_EXPERTS, 1))
    return pos.reshape(-1), rw, tab.reshape(-1)


def _pos_index(tok0):
    return (tok0 // ROUTER_TILE) * (TOP_K * ROUTER_TILE) + tok0 % ROUTER_TILE


def _tokens(ref, first, n=1):
    start = pl.multiple_of(first * V7X_SUBLANES, V7X_SUBLANES)
    return ref.at[pl.ds(start, n * V7X_SUBLANES), :]


def _dispatch_kernel(pos_ref, tab_ref, hpp_ref, hps_ref, xs_ref, zbuf, sem, *, n_p_steps):
    i = pl.program_id(0)
    rows = hpp_ref.shape[0] // V7X_SUBLANES

    @pl.when(i == 0)
    def _():
        zbuf[...] = jnp.zeros_like(zbuf)

        def fill(e):
            first = pl.multiple_of(tab_ref[TAB_LAST_TILE * V7X_LANES + e], MOE_TILE)
            return pltpu.make_async_copy(zbuf, _tokens(xs_ref, first, MOE_TILE), sem)

        for e in range(N_EXPERTS):
            @pl.when(tab_ref[TAB_LAST_TILE * V7X_LANES + e] >= 0)
            def _():
                fill(e).start()
        for e in range(N_EXPERTS):
            @pl.when(tab_ref[TAB_LAST_TILE * V7X_LANES + e] >= 0)
            def _():
                fill(e).wait()

        def tail(j):
            first = pl.multiple_of(j * MOE_TILE, MOE_TILE)
            return pltpu.make_async_copy(zbuf, _tokens(xs_ref, first, MOE_TILE), sem)

        def tail_start(j, carry):
            tail(j).start()
            return carry

        def tail_wait(j, carry):
            tail(j).wait()
            return carry

        n_used = tab_ref[TAB_NUSED * V7X_LANES]
        n_tiles = xs_ref.shape[0] // (MOE_TILE * V7X_SUBLANES)
        lax.fori_loop(n_used, n_tiles, tail_start, 0)
        lax.fori_loop(n_used, n_tiles, tail_wait, 0)

    def scatter(src_ref, tok0):
        n = src_ref.shape[0] // V7X_SUBLANES
        base = _pos_index(tok0)

        def row_copy(r, dst):
            return pltpu.make_async_copy(_tokens(src_ref, r), _tokens(xs_ref, dst), sem)

        def issue(r, carry):
            row_copy(r, pos_ref[base + r]).start()
            row_copy(r, pos_ref[base + ROUTER_TILE + r]).start(priority=1)
            return carry

        lax.fori_loop(0, n, issue, 0, unroll=8)
        block = pltpu.make_async_copy(src_ref, _tokens(xs_ref, 0, n), sem)
        for _ in range(TOP_K):
            block.wait()

    @pl.when(i < n_p_steps)
    def _():
        scatter(hpp_ref, i * rows)

    @pl.when(i == n_p_steps)
    def _():
        scatter(hps_ref, n_p_steps * rows)


def _dispatch(pos, tab, hpt_p, hpt_s, n_rows_sorted):
    sub = V7X_SUBLANES
    n_p_steps = hpt_p.shape[0] // (MATMUL_TILE * sub)
    kern = functools.partial(_dispatch_kernel, n_p_steps=n_p_steps)
    blk = (MATMUL_TILE * sub, V7X_LANES)
    return pl.pallas_call(
        kern,
        grid_spec=pltpu.PrefetchScalarGridSpec(
            num_scalar_prefetch=2,
            grid=(n_p_steps + 1,),
            in_specs=[pl.BlockSpec(blk, lambda i, p, t: (jnp.minimum(i, n_p_steps - 1), 0)),
                      pl.BlockSpec(hpt_s.shape, lambda i, p, t: (0, 0))],
            out_specs=pl.BlockSpec(memory_space=pl.ANY),
            scratch_shapes=[pltpu.VMEM((MOE_TILE * sub, V7X_LANES), U32), pltpu.SemaphoreType.DMA(())]),
        out_shape=jax.ShapeDtypeStruct((n_rows_sorted * sub, V7X_LANES), U32),
        compiler_params=_cparams(1),
        name="moe_dispatch",
    )(pos, tab, hpt_p, hpt_s)


def _experts_kernel(tab_ref, xs_ref, wg_hbm, wu_hbm, wd_hbm, ys_ref,
                    wg32, wu32, wd32, wgbf, wubf, wdbf, slot_ref, sems, *, layer):
    i = pl.program_id(0)
    expert = tab_ref[TAB_EXPERT * V7X_LANES + i]
    prev = tab_ref[TAB_EXPERT * V7X_LANES + jnp.maximum(i - 1, 0)]
    upcoming = tab_ref[TAB_NEXT * V7X_LANES + i]
    changed = (i == 0) | (expert != prev)

    def weight_copies(e, slot):
        return (pltpu.make_async_copy(wg_hbm.at[layer, e], wg32.at[slot], sems.at[0, slot]),
                pltpu.make_async_copy(wu_hbm.at[layer, e], wu32.at[slot], sems.at[1, slot]),
                pltpu.make_async_copy(wd_hbm.at[layer, e], wd32.at[slot], sems.at[2, slot]))

    @pl.when(i == 0)
    def _():
        slot_ref[0] = 0
        for cp in weight_copies(expert, 0):
            cp.start()

    @pl.when(changed & (i > 0))
    def _():
        slot_ref[0] = 1 - slot_ref[0]

    def mlp(wg, wu, wd):
        x = jnp.concatenate(_unpack_bf16_pairs(_load_token_tiles(xs_ref)), axis=-1)
        a = jnp.dot(x, wg, preferred_element_type=F32)
        b = jnp.dot(x, wu, preferred_element_type=F32)
        hid = (a * jax.nn.sigmoid(a)) * b
        y = jnp.dot(hid.astype(BF16), wd, preferred_element_type=F32)
        _store_token_tiles(ys_ref, _pack_bf16_pairs(y))

    for slot in range(2):
        @pl.when(changed & (slot_ref[0] == slot))
        def _():
            for cp in weight_copies(expert, slot):
                cp.wait()

            @pl.when(upcoming >= 0)
            def _():
                for cp in weight_copies(upcoming, 1 - slot):
                    cp.start(priority=1)

            wg = wg32[slot].astype(BF16)
            wu = wu32[slot].astype(BF16)
            wd = wd32[slot].astype(BF16)
            wgbf[...] = wg
            wubf[...] = wu
            wdbf[...] = wd
            mlp(wg, wu, wd)

    valid = tab_ref[TAB_VALID * V7X_LANES + i] > 0

    @pl.when(valid & jnp.logical_not(changed))
    def _():
        mlp(wgbf[...], wubf[...], wdbf[...])

    @pl.when(jnp.logical_not(valid))
    def _():
        ys_ref[...] = jnp.zeros_like(ys_ref)


def _experts(tab, xs, w_gate, w_up, w_down, layer):
    sub = V7X_SUBLANES
    _, _, d, f = w_gate.shape
    nt = xs.shape[0] // (MOE_TILE * sub)
    assert nt <= V7X_LANES and d == 2 * sub * V7X_LANES
    blk = (MOE_TILE * sub, V7X_LANES)

    def tile(i, tab_ref):
        return jnp.minimum(i, tab_ref[TAB_NUSED * V7X_LANES] - 1)

    hbm = pl.BlockSpec(memory_space=pl.ANY)
    return pl.pallas_call(
        functools.partial(_experts_kernel, layer=layer),
        grid_spec=pltpu.PrefetchScalarGridSpec(
            num_scalar_prefetch=1,
            grid=(nt,),
            in_specs=[pl.BlockSpec(blk, lambda i, t: (tile(i, t), 0)), hbm, hbm, hbm],
            out_specs=pl.BlockSpec(blk, lambda i, t: (i, 0)),
            scratch_shapes=[pltpu.VMEM((2, d, f), F32), pltpu.VMEM((2, d, f), F32), pltpu.VMEM((2, f, d), F32),
                            pltpu.VMEM((d, f), BF16), pltpu.VMEM((d, f), BF16), pltpu.VMEM((f, d), BF16),
                            pltpu.SMEM((1,), I32), pltpu.SemaphoreType.DMA((3, 2))]),
        out_shape=jax.ShapeDtypeStruct(xs.shape, U32),
        compiler_params=_cparams(1),
        name="moe_experts",
    )(tab, xs, w_gate, w_up, w_down)


def _combine_kernel(pos_ref, ys_ref, rw_ref, ng_ref, *rest, tiles_per_seq, n_p_steps, final):
    n_in = 4
    n_out = 1 if final else 2
    trunk_in = (rest[:n_in], rest[n_in:2 * n_in])
    outs = rest[2 * n_in:2 * n_in + 2 * n_out]
    trunk_out = (outs[:n_out], outs[n_out:])
    buf0, buf1, sems = rest[2 * n_in + 2 * n_out:]
    bufs = (buf0, buf1)
    i = pl.program_id(0)
    n_steps = n_p_steps + 1
    rows = rw_ref.shape[0]

    def gather(step, slot):
        base = _pos_index(step * rows)

        def issue(r, carry):
            for k in range(TOP_K):
                src = pos_ref[base + k * ROUTER_TILE + r]
                pltpu.make_async_copy(_tokens(ys_ref, src), _tokens(bufs[slot].at[k], r),
                                      sems.at[slot]).start(priority=k)
            return carry

        lax.fori_loop(0, rows, issue, 0, unroll=8)

    def drain(slot):
        for k in range(TOP_K):
            pltpu.make_async_copy(_tokens(ys_ref, 0, rows), bufs[slot].at[k], sems.at[slot]).wait()

    def finish(slot, trunk, seq):
        x1_ref, g2_ref, sc_ref, sh_ref = trunk_in[trunk]
        rw = rw_ref[...]
        lo_a, hi_a = _unpack_pairs_f32(_load_token_tiles(bufs[slot].at[0]))
        lo_b, hi_b = _unpack_pairs_f32(_load_token_tiles(bufs[slot].at[1]))
        w_a = rw[:, 0:1]
        w_b = rw[:, 1:2]
        moe = jnp.concatenate([w_a * lo_a + w_b * lo_b, w_a * hi_a + w_b * hi_b], axis=-1)
        x2 = x1_ref[...] + _mod_rows(g2_ref, seq) * moe
        if final:
            trunk_out[trunk][0][...] = _rms(x2, ng_ref[...])
        else:
            trunk_out[trunk][0][...] = x2
            trunk_out[trunk][1][...] = (_rms(x2, ng_ref[...]) * (1.0 + _mod_rows(sc_ref, seq))
                                        + _mod_rows(sh_ref, seq)).astype(BF16)

    @pl.when(i == 0)
    def _():
        gather(0, 0)

    for slot in range(2):
        @pl.when(i % 2 == slot)
        def _():
            @pl.when(i + 1 < n_steps)
            def _():
                gather(i + 1, 1 - slot)

            drain(slot)

            @pl.when(i < n_p_steps)
            def _():
                finish(slot, 0, i // tiles_per_seq)

            @pl.when(i == n_p_steps)
            def _():
                finish(slot, 1, 0)


def _combine(pos, ys, x1_p, x1_s, rw, mod_p, mod_s, layer, ng, seq_rows_p, final):
    n_p, d = x1_p.shape
    rt = ROW_TILE
    assert x1_s.shape[0] == rt and n_p % rt == 0 and ROUTER_TILE % rt == 0
    n_p_steps = n_p // rt
    kern = functools.partial(_combine_kernel, tiles_per_seq=seq_rows_p // rt, n_p_steps=n_p_steps, final=final)
    p_spec = lambda w: pl.BlockSpec((rt, w), lambda i, p: (jnp.minimum(i, n_p_steps - 1), 0))
    s_spec = lambda w: pl.BlockSpec((rt, w), lambda i, p: (0, 0))
    nxt = min(layer + 1, mod_p.shape[0] - 1)

    def trunk_specs(spec, mod):
        return [spec(d), _mod_spec(mod, layer, 5), _mod_spec(mod, nxt, 1), _mod_spec(mod, nxt, 0)]

    out_dtypes = [F32] if final else [F32, BF16]
    out_shape = ([jax.ShapeDtypeStruct((n_p, d), t) for t in out_dtypes]
                 + [jax.ShapeDtypeStruct((rt, d), t) for t in out_dtypes])
    out_specs = [p_spec(d)] * len(out_dtypes) + [s_spec(d)] * len(out_dtypes)
    res = pl.pallas_call(
        kern,
        grid_spec=pltpu.PrefetchScalarGridSpec(
            num_scalar_prefetch=1,
            grid=(n_p_steps + 1,),
            in_specs=[pl.BlockSpec(memory_space=pl.ANY),
                      pl.BlockSpec((rt, V7X_LANES), lambda i, p: (i, 0)),
                      pl.BlockSpec((1, d), lambda i, p: (0, 0))]
            + trunk_specs(p_spec, mod_p) + trunk_specs(s_spec, mod_s),
            out_specs=out_specs,
            scratch_shapes=[pltpu.VMEM((TOP_K, rt * V7X_SUBLANES, V7X_LANES), U32)] * 2
            + [pltpu.SemaphoreType.DMA((2,))]),
        out_shape=out_shape,
        compiler_params=_cparams(1),
        name="moe_combine_final" if final else "moe_combine",
    )(pos, ys, rw, ng.reshape(1, d), x1_p, mod_p, mod_p, mod_p, x1_s, mod_s, mod_s, mod_s)
    return res[:len(out_dtypes)], res[len(out_dtypes):]


def _moe(out_p, out_s, mod_p, mod_s, layer, ng, seq_rows_p, w_router, b_router, w_gate, w_up, w_down, final):
    x1_p, hp_p, hpt_p = out_p
    x1_s, hp_s, hpt_s = out_s
    n_p = x1_p.shape[0]
    n_tok = n_p + x1_s.shape[0]
    max_rows = TOP_K * n_tok + N_EXPERTS * (MOE_TILE - 1)
    n_rows_sorted = -(-max_rows // MOE_TILE) * MOE_TILE
    pos, rw, tab = _router(hp_p, hp_s, w_router, b_router)
    xs = _dispatch(pos, tab, hpt_p, hpt_s, n_rows_sorted)
    ys = _experts(tab, xs, w_gate, w_up, w_down, layer)
    return _combine(pos, ys, x1_p, x1_s, rw, mod_p, mod_s, layer, ng, seq_rows_p, final)


def _gmlp_kernel(h_ref, w_ref, lg_ref, lb_ref, ws_ref, bs_ref, yc_ref, *rest, ell, blk, emit_v):
    if emit_v:
        gv_ref, wbf, wsbf = rest
    else:
        wbf, wsbf = rest
    i = pl.program_id(0)
    rows = h_ref.shape[0]
    c = GM_WIDTH

    @pl.when(i == 0)
    def _():
        wbf[...] = w_ref[...].astype(BF16)
        r = lax.broadcasted_iota(I32, (ell, ell), 0)
        s = lax.broadcasted_iota(I32, (ell, ell), 1)
        keep = (r >= s) & ((r // blk) == (s // blk))
        rsel = (lax.broadcasted_iota(I32, (ell, CHUNK), 0) % blk
                == lax.broadcasted_iota(I32, (ell, CHUNK), 1)).astype(BF16)
        csel = (lax.broadcasted_iota(I32, (CHUNK, ell), 1) % blk
                == lax.broadcasted_iota(I32, (CHUNK, ell), 0)).astype(BF16)
        for g in range(GM_GROUPS):
            wchunk = ws_ref[g].astype(BF16)
            if blk == ell:
                full = wchunk
            else:
                rowsp = jnp.dot(rsel, wchunk, preferred_element_type=F32).astype(BF16)
                full = jnp.dot(rowsp, csel, preferred_element_type=F32).astype(BF16)
            wsbf[g] = jnp.where(keep, full, jnp.zeros_like(full))

    uv = jnp.dot(h_ref[...], wbf[...], preferred_element_type=F32)
    u = uv[:, :c]
    v = uv[:, c:]
    vc = v - jnp.mean(v, axis=-1, keepdims=True)
    vn = vc * lax.rsqrt(jnp.mean(vc * vc, axis=-1, keepdims=True) + EPS) * lg_ref[...] + lb_ref[...]
    if emit_v:
        gv_ref[...] = vn
    vb = vn.astype(BF16)
    bs = bs_ref[...]
    for ch in range(rows // ell):
        rs = slice(ch * ell, (ch + 1) * ell)
        outs = []
        for g in range(GM_GROUPS):
            cs = slice(g * GM_GROUP, (g + 1) * GM_GROUP)
            mixed = jnp.dot(wsbf[g], vb[rs, cs], preferred_element_type=F32)
            mixed = (mixed.reshape(ell // blk, blk, GM_GROUP) + bs[:blk, g:g + 1][None]).reshape(ell, GM_GROUP)
            outs.append(u[rs, cs] * mixed)
        yc_ref[rs, :] = jnp.concatenate(outs, axis=-1).astype(BF16)


def _gmlp_mixer(h2, w_in, ln_g, ln_b, ws, bs_t, ell, blk, emit_v):
    rows, d = h2.shape
    c = GM_WIDTH
    kern = functools.partial(_gmlp_kernel, ell=ell, blk=blk, emit_v=emit_v)
    rt = min(rows, MATMUL_TILE)
    out_specs = [pl.BlockSpec((rt, c), lambda i: (i, 0))]
    out_shape = [jax.ShapeDtypeStruct((rows, c), BF16)]
    if emit_v:
        out_specs.append(pl.BlockSpec((rt, c), lambda i: (i, 0)))
        out_shape.append(jax.ShapeDtypeStruct((rows, c), F32))
    return pl.pallas_call(
        kern,
        grid=(rows // rt,),
        in_specs=[pl.BlockSpec((rt, d), lambda i: (i, 0)),
                  pl.BlockSpec((d, 2 * c), lambda i: (0, 0), pipeline_mode=pl.Buffered(1)),
                  pl.BlockSpec((1, c), lambda i: (0, 0)),
                  pl.BlockSpec((1, c), lambda i: (0, 0)),
                  pl.BlockSpec((GM_GROUPS, CHUNK, CHUNK), lambda i: (0, 0, 0)),
                  pl.BlockSpec((CHUNK, GM_GROUPS), lambda i: (0, 0))],
        out_specs=out_specs,
        out_shape=out_shape,
        scratch_shapes=[pltpu.VMEM((d, 2 * c), BF16), pltpu.VMEM((GM_GROUPS, ell, ell), BF16)],
        compiler_params=_cparams(1),
        name="gmlp_mixer",
    )(h2, w_in, ln_g.reshape(1, c), ln_b.reshape(1, c), ws, bs_t)


PAIR_W = 2 * HEAD_DIM
PAIRS_PER_KV = N_HEADS // N_KV // 2
NT_DIMS = (((1,), (1,)), ((), ()))
SCORE_SCALE = HEAD_DIM ** -0.5
assert float(np.log2(SCORE_SCALE)).is_integer()


def _swa_project(i, h_ref, wq_ref, wkv_ref, wbf):
    nq = N_HEADS * HEAD_DIM

    @pl.when(i == 0)
    def _():
        wbf[:, :nq] = wq_ref[...].astype(BF16)
        wbf[:, nq:] = wkv_ref[...].astype(BF16)

    return jnp.dot(h_ref[...], wbf[...], preferred_element_type=F32)


def _pair_block_diag(a, a_swapped, hk, axis):
    dim_axis = 1 - axis
    low = lax.broadcasted_iota(I32, a.shape, dim_axis) < HEAD_DIM
    lo, hi = (a, a_swapped) if hk == 0 else (a_swapped, a)
    return jnp.concatenate([jnp.where(low, lo, 0.0), jnp.where(low, 0.0, hi)], axis=axis).astype(BF16)


def _stack_pairs(qkv, rs, hk, scale=None):
    p0 = hk * PAIRS_PER_KV
    q = jnp.concatenate([qkv[rs, (p0 + pp) * PAIR_W:(p0 + pp + 1) * PAIR_W]
                         for pp in range(PAIRS_PER_KV)], axis=0)
    return (q if scale is None else q * scale).astype(BF16)


def _swa_cached_kernel(h_ref, wq_ref, wkv_ref, kp_ref, vp_ref, bias_ref, sink_ref, yd_ref, k_ref, v_ref,
                       wbf, *, tq):
    i = pl.program_id(0)
    rows = h_ref.shape[0]
    nq = N_HEADS * HEAD_DIM
    nkv = N_KV * HEAD_DIM
    n_blocks = rows // tq
    qkv = _swa_project(i, h_ref, wq_ref, wkv_ref, wbf)
    k_new = qkv[:, nq:nq + nkv]
    v_new = qkv[:, nq + nkv:]
    k_ref[...] = k_new
    v_ref[...] = v_new
    pad = jnp.zeros((WINDOW - tq, nkv), F32)

    scores, vbds = [], []
    for blk in range(n_blocks):
        rs = slice(blk * tq, (blk + 1) * tq)
        kcat = jnp.concatenate([kp_ref[blk], k_new[rs], pad], axis=0)
        vcat = jnp.concatenate([vp_ref[blk], v_new[rs], pad], axis=0)
        kswap = pltpu.roll(kcat, HEAD_DIM, 1)
        vswap = pltpu.roll(vcat, HEAD_DIM, 1)
        per_head = []
        for hk in range(N_KV):
            kbd = _pair_block_diag(kcat, kswap, hk, 0)
            vbds.append(_pair_block_diag(vcat, vswap, hk, 0))
            s4 = lax.dot_general(_stack_pairs(qkv, rs, hk), kbd, NT_DIMS,
                                 preferred_element_type=F32) * (HEAD_DIM ** -0.5)
            for pp in range(PAIRS_PER_KV):
                for sub in range(2):
                    per_head.append(s4[pp * tq:(pp + 1) * tq, sub * 2 * WINDOW:(sub + 1) * 2 * WINDOW])
        scores.append(jnp.concatenate(per_head, axis=0))

    s_all = jnp.stack(scores, axis=0) + bias_ref[...][None]
    sink = sink_ref[...][None]
    m = jnp.maximum(jnp.max(s_all, axis=-1, keepdims=True), sink)
    pr = jnp.exp(s_all - m)
    pr = pr / (jnp.sum(pr, axis=-1, keepdims=True) + jnp.exp(sink - m))

    for blk in range(n_blocks):
        outs = []
        for hk in range(N_KV):
            p4 = []
            for pp in range(PAIRS_PER_KV):
                h0 = 2 * (hk * PAIRS_PER_KV + pp)
                p4.append(jnp.concatenate([pr[blk, h0 * tq:(h0 + 1) * tq, :],
                                           pr[blk, (h0 + 1) * tq:(h0 + 2) * tq, :]], axis=-1))
            o4 = jnp.dot(jnp.concatenate(p4, axis=0).astype(BF16), vbds[blk * N_KV + hk],
                         preferred_element_type=F32)
            outs.extend(o4[pp * tq:(pp + 1) * tq, :] for pp in range(PAIRS_PER_KV))
        yd_ref[blk * tq:(blk + 1) * tq, :] = jnp.concatenate(outs, axis=-1).astype(BF16)


def _swa_stream_kernel(h_ref, wq_ref, wkv_ref, bias_ref, sink_ref, yd_ref, k_ref, v_ref,
                       wbf, kprev, vprev_t, *, blocks_per_seq):
    i = pl.program_id(0)
    rows = h_ref.shape[0]
    nq = N_HEADS * HEAD_DIM
    nkv = N_KV * HEAD_DIM
    tq = WINDOW
    n_blocks = rows // tq

    @pl.when(i == 0)
    def _():
        kprev[...] = jnp.zeros_like(kprev)
        vprev_t[...] = jnp.zeros_like(vprev_t)

    qkv = _swa_project(i, h_ref, wq_ref, wkv_ref, wbf)
    k_new = qkv[:, nq:nq + nkv]
    v_new = qkv[:, nq + nkv:]
    k_ref[...] = k_new
    v_ref[...] = v_new
    v_new_t = v_new.T
    lanes = PAIRS_PER_KV * tq

    for blk in range(n_blocks):
        rs = slice(blk * tq, (blk + 1) * tq)
        first = ((i * n_blocks + blk) % blocks_per_seq == 0).astype(I32)
        k_cur = k_new[rs]
        v_cur_t = v_new_t[:, rs]
        kcat = jnp.concatenate([kprev[...], k_cur], axis=0)
        vcat_t = jnp.concatenate([vprev_t[...], v_cur_t], axis=1)
        kprev[...] = k_cur
        vprev_t[...] = v_cur_t
        kswap = pltpu.roll(kcat, HEAD_DIM, 1)
        vswap_t = pltpu.roll(vcat_t, HEAD_DIM, 0)
        outs = []
        for hk in range(N_KV):
            kbd = _pair_block_diag(kcat, kswap, hk, 0)
            vbd_t = _pair_block_diag(vcat_t, vswap_t, hk, 1)
            st = lax.dot_general(kbd, _stack_pairs(qkv, rs, hk, SCORE_SCALE), NT_DIMS,
                                 preferred_element_type=F32)
            s3 = st.reshape(2, 2 * WINDOW, lanes) + bias_ref[first, hk]
            sink = sink_ref[hk]
            m = jnp.maximum(jnp.max(s3, axis=1, keepdims=True), sink)
            pr = jnp.exp(s3 - m)
            inv = 1.0 / (jnp.sum(pr, axis=1, keepdims=True) + jnp.exp(sink - m))
            o_t = jnp.dot(vbd_t, pr.reshape(4 * WINDOW, lanes).astype(BF16),
                          preferred_element_type=F32)
            norm = jnp.concatenate([jnp.broadcast_to(inv[sub], (HEAD_DIM, lanes)) for sub in range(2)], axis=0)
            o4 = (o_t * norm).T
            outs.extend(o4[pp * tq:(pp + 1) * tq, :] for pp in range(PAIRS_PER_KV))
        yd_ref[rs, :] = jnp.concatenate(outs, axis=-1).astype(BF16)


def _swa_weight_specs(w_in, d):
    nq = N_HEADS * HEAD_DIM
    nkv = N_KV * HEAD_DIM
    nw = nq + 2 * nkv
    q_blk = (w_in.shape[1] - nw) // nq
    kv_blk = (w_in.shape[1] - 2 * nkv) // (2 * nkv)
    assert q_blk * nq + nw == w_in.shape[1] and kv_blk * 2 * nkv + 2 * nkv == w_in.shape[1]
    return [pl.BlockSpec((d, nq), lambda i: (0, q_blk)), pl.BlockSpec((d, 2 * nkv), lambda i: (0, kv_blk))]


def _swa_outputs(rows, rt):
    nq = N_HEADS * HEAD_DIM
    nkv = N_KV * HEAD_DIM
    specs = [pl.BlockSpec((rt, nq), lambda i: (i, 0)),
             pl.BlockSpec((rt, nkv), lambda i: (i, 0)),
             pl.BlockSpec((rt, nkv), lambda i: (i, 0))]
    shapes = [jax.ShapeDtypeStruct((rows, nq), BF16),
              jax.ShapeDtypeStruct((rows, nkv), F32),
              jax.ShapeDtypeStruct((rows, nkv), F32)]
    return specs, shapes


def _swa_cached_mixer(h2, w_in, k_cache, v_cache, bias, sinks, tq):
    rows, d = h2.shape
    nkv = N_KV * HEAD_DIM
    nw = N_HEADS * HEAD_DIM + 2 * nkv
    n_blocks = ROW_TILE // tq
    cache_spec = pl.BlockSpec((n_blocks, WINDOW, nkv), lambda i: (i, 0, 0))
    out_specs, out_shape = _swa_outputs(rows, ROW_TILE)
    return pl.pallas_call(
        functools.partial(_swa_cached_kernel, tq=tq),
        grid=(rows // ROW_TILE,),
        in_specs=[pl.BlockSpec((ROW_TILE, d), lambda i: (i, 0))] + _swa_weight_specs(w_in, d)
        + [cache_spec, cache_spec,
           pl.BlockSpec((N_HEADS * tq, 2 * WINDOW), lambda i: (0, 0)),
           pl.BlockSpec((N_HEADS * tq, 1), lambda i: (0, 0))],
        out_specs=out_specs,
        out_shape=out_shape,
        scratch_shapes=[pltpu.VMEM((d, nw), BF16)],
        compiler_params=_cparams(1),
        name="swa_cached",
    )(h2, w_in, w_in, k_cache, v_cache, bias, sinks)


def _swa_stream_mixer(h2, w_in, bias_t, sinks_t, blocks_per_seq):
    rows, d = h2.shape
    nkv = N_KV * HEAD_DIM
    nw = N_HEADS * HEAD_DIM + 2 * nkv
    lanes = PAIRS_PER_KV * WINDOW
    rt = min(rows, MATMUL_TILE)
    out_specs, out_shape = _swa_outputs(rows, rt)
    return pl.pallas_call(
        functools.partial(_swa_stream_kernel, blocks_per_seq=blocks_per_seq),
        grid=(rows // rt,),
        in_specs=[pl.BlockSpec((rt, d), lambda i: (i, 0))] + _swa_weight_specs(w_in, d)
        + [pl.BlockSpec((2, N_KV, 2, 2 * WINDOW, lanes), lambda i: (0, 0, 0, 0, 0)),
           pl.BlockSpec((N_KV, 2, 1, lanes), lambda i: (0, 0, 0, 0))],
        out_specs=out_specs,
        out_shape=out_shape,
        scratch_shapes=[pltpu.VMEM((d, nw), BF16), pltpu.VMEM((WINDOW, nkv), F32),
                        pltpu.VMEM((nkv, WINDOW), F32)],
        compiler_params=_cparams(1),
        name="swa_stream",
    )(h2, w_in, w_in, bias_t, sinks_t)


def _t5_bucket(dist):
    max_exact = N_BUCKETS // 2
    dd = np.maximum(dist, 1)
    large = max_exact + (np.log(dd / max_exact) / np.log(WINDOW / max_exact)
                         * (N_BUCKETS - max_exact)).astype(np.int64)
    large = np.minimum(large, N_BUCKETS - 1)
    return np.where(dist < max_exact, dist, large).astype(np.int32)


def _attention_bias(rel_bias):
    by_dist = jnp.take(rel_bias.astype(F32), _t5_bucket(np.arange(WINDOW)), axis=0).T
    neg = jnp.full((N_HEADS, WINDOW), NEG_INF, F32)
    line = jnp.concatenate([neg, by_dist[:, ::-1], neg[:, :WINDOW - 1]], axis=1)
    rows = line[:, None, :]
    span = 1
    while span < WINDOW:
        rows = jnp.concatenate([rows[:, :, span:], rows[:, :, :rows.shape[2] - span]], axis=1)
        span *= 2
    return rows


def kernel(x_prompt, x_sample, state_pool, state_conv, cache_swa_k, cache_swa_v, c_prompt, c_sample, w_ada, b_ada, norm_g, final_norm_g, w_in_even, w_out_even, w_pool, pool_scale, conv_w, w_in_odd, w_out_odd, gm_norm_g, gm_norm_b, gm_w_s, gm_b_s, attn_sinks, rel_bias, w_router, b_router, w_gate, w_up, w_down):
    d = D_MODEL
    bp, tp, _ = x_prompt.shape
    bs, ts, _ = x_sample.shape
    rows_s = bs * ts
    assert rows_s == ROW_TILE and tp % ROUTER_TILE == 0 and PAST_LEN % CHUNK == 0
    assert bp <= V7X_SUBLANES and CHUNK % ts == 0

    n_c = bp + bs
    c_pad = (-n_c) % V7X_SUBLANES
    c_all = jnp.concatenate([c_prompt, c_sample, jnp.zeros((c_pad, d), F32)], axis=0)
    mod_p = _adaln(c_all, w_ada, b_ada)
    mod_s = jnp.repeat(mod_p[:, bp:bp + bs], ts, axis=1)

    xp = x_prompt.reshape(bp * tp, d)
    xs_ = x_sample.reshape(rows_s, d)
    w_in0, w_in1 = w_in_even[0], w_in_odd[0]

    zero_pool = jnp.zeros((bp, POOL_STATE, POOL_WIDTH), F32)
    zero_conv = jnp.zeros((bp, CONV_K - 1, CONV_WIDTH), F32)
    hp0, ya_p, pool_p = _pool_mixer(xp, norm_g[0, 0], mod_p, 0, w_in0, w_pool[0], pool_scale[0],
                                    zero_pool, 1, MATMUL_TILE, 0)
    hs0, ya_s, pool_s = _pool_mixer(xs_, norm_g[0, 0], mod_s, 0, w_in0, w_pool[0], pool_scale[0],
                                    state_pool[0], bs, ts, PAST_LEN)
    yb_p, conv_p = _conv_mixer(hp0, w_in0, conv_w[0], zero_conv, 1, MATMUL_TILE)
    yb_s, conv_s = _conv_mixer(hs0, w_in0, conv_w[0], state_conv[0], bs, ts)
    out_p = _outproj(ya_p, yb_p, xp, mod_p, 0, norm_g[0, 1], w_out_even[0], tp)
    out_s = _outproj(ya_s, yb_s, xs_, mod_s, 0, norm_g[0, 1], w_out_even[0], rows_s)
    (x2p, h1p), (x2s, h1s) = _moe(out_p, out_s, mod_p, mod_s, 0, norm_g[1, 0], tp,
                                  w_router, b_router, w_gate, w_up, w_down, final=False)

    bs_t = gm_b_s[0].T
    (yc_p,) = _gmlp_mixer(h1p, w_in1, gm_norm_g[0], gm_norm_b[0], gm_w_s[0], bs_t, CHUNK, CHUNK, False)
    yc_s, gv_s = _gmlp_mixer(h1s, w_in1, gm_norm_g[0], gm_norm_b[0], gm_w_s[0], bs_t, rows_s, ts, True)
    bias = _attention_bias(rel_bias)
    nkv = N_KV * HEAD_DIM
    bias_t = jnp.transpose(bias.reshape(N_KV, PAIRS_PER_KV, 2, WINDOW, 2 * WINDOW), (0, 2, 4, 1, 3))
    bias_t = bias_t.reshape(N_KV, 2, 2 * WINDOW, PAIRS_PER_KV * WINDOW)
    before_start = (np.arange(2 * WINDOW) < WINDOW)[None, None, :, None]
    bias_t = jnp.stack([bias_t, jnp.where(before_start, NEG_INF, bias_t)], axis=0)
    sinks_t = jnp.transpose(attn_sinks[0].reshape(N_KV, PAIRS_PER_KV, 2), (0, 2, 1))
    sinks_t = jnp.repeat(sinks_t, WINDOW, axis=-1).reshape(N_KV, 2, 1, PAIRS_PER_KV * WINDOW)
    yd_p, k_p, v_p = _swa_stream_mixer(h1p, w_in1, bias_t, sinks_t, tp // WINDOW)
    yd_s, k_s, v_s = _swa_cached_mixer(h1s, w_in1, cache_swa_k[0].reshape(bs, WINDOW, nkv),
                                       cache_swa_v[0].reshape(bs, WINDOW, nkv),
                                       bias[:, :ts, :].reshape(N_HEADS * ts, 2 * WINDOW),
                                       jnp.repeat(attn_sinks[0], ts).reshape(-1, 1), ts)
    out_p = _outproj(yc_p, yd_p, x2p, mod_p, 1, norm_g[1, 1], w_out_odd[0], tp)
    out_s = _outproj(yc_s, yd_s, x2s, mod_s, 1, norm_g[1, 1], w_out_odd[0], rows_s)
    (yp,), (ys_out,) = _moe(out_p, out_s, mod_p, mod_s, 1, final_norm_g, tp,
                            w_router, b_router, w_gate, w_up, w_down, final=True)

    k_p4 = k_p.reshape(bp, tp, nkv)[:, -WINDOW:].reshape(bp, WINDOW, N_KV, HEAD_DIM)
    v_p4 = v_p.reshape(bp, tp, nkv)[:, -WINDOW:].reshape(bp, WINDOW, N_KV, HEAD_DIM)
    k_s4 = jnp.concatenate([cache_swa_k[0], k_s.reshape(bs, ts, N_KV, HEAD_DIM)], axis=1)[:, -WINDOW:]
    v_s4 = jnp.concatenate([cache_swa_v[0], v_s.reshape(bs, ts, N_KV, HEAD_DIM)], axis=1)[:, -WINDOW:]
    return (yp.reshape(bp, tp, d), ys_out.reshape(bs, ts, d),
            pool_p[None], pool_s[None], conv_p[None], conv_s[None],
            k_p4[None], k_s4[None], v_p4[None], v_s4[None],
            gv_s.reshape(bs, ts, GM_WIDTH)[None])
```

```python
import functools

import numpy as np
import jax
import jax.numpy as jnp
from jax import lax
from jax.experimental import pallas as pl
from jax.experimental.pallas import tpu as pltpu

F32 = jnp.float32
BF16 = jnp.bfloat16
I32 = jnp.int32
U32 = jnp.uint32

D_MODEL = 2048
POOL_WINDOWS = (2, 4, 8, 16)
POOL_WIDTH = 1024
POOL_GROUP = 256
POOL_STATE = 15
CONV_WIDTH = 1024
CONV_K = 3
GM_WIDTH = 1024
GM_GROUPS = 8
GM_GROUP = 128
CHUNK = 128
HEAD_DIM = 64
N_HEADS = 16
N_KV = 2
WINDOW = 128
N_BUCKETS = 32
N_EXPERTS = 16
N_EXPERT_GROUPS = 4
EXP_PER_GROUP = 4
TOP_K = 2
EPS = 1e-6
NEG_INF = -1e30
PAST_LEN = 16384

V7X_SUBLANES = 8
V7X_LANES = 128
VMEM_LIMIT = 56 * 1024 * 1024

ROW_TILE = 256
MATMUL_TILE = 512
ROUTER_TILE = 1024
POOL_HALO = 16
CONV_HALO = 8
MOE_TILE = 256
ADALN_COL_TILE = 1024
CONV_COL_TILE = 512
TAB_EXPERT, TAB_VALID, TAB_LAST_TILE, TAB_NUSED, TAB_NEXT = 0, 1, 2, 3, 4


def _cparams(n_axes):
    return pltpu.CompilerParams(dimension_semantics=("arbitrary",) * n_axes,
                                vmem_limit_bytes=VMEM_LIMIT)


def _rms(x, g):
    return x * lax.rsqrt(jnp.mean(x * x, axis=-1, keepdims=True) + EPS) * g


def _mod_spec(mod, layer, part):
    nrow = ROW_TILE if mod.shape[1] == ROW_TILE else V7X_SUBLANES
    return pl.BlockSpec((1, nrow, D_MODEL), lambda *_: (layer, 0, part))


def _mod_rows(m_ref, seq):
    if m_ref.shape[1] == V7X_SUBLANES:
        return m_ref[0, pl.ds(seq, 1), :]
    return m_ref[0]


def _adaln_kernel(c_ref, w_ref, b_ref, o_ref):
    c = c_ref[...]
    a = (c * jax.nn.sigmoid(c)).astype(BF16)
    o_ref[0] = jnp.dot(a, w_ref[0].astype(BF16), preferred_element_type=F32) + b_ref[0]


def _adaln(c_all, w_ada, b_ada):
    depth, d, n6 = w_ada.shape
    m = c_all.shape[0]
    tn = ADALN_COL_TILE
    return pl.pallas_call(
        _adaln_kernel,
        grid=(depth, n6 // tn),
        in_specs=[pl.BlockSpec((m, d), lambda l, j: (0, 0)),
                  pl.BlockSpec((1, d, tn), lambda l, j: (l, 0, j)),
                  pl.BlockSpec((1, 1, tn), lambda l, j: (l, 0, j))],
        out_specs=pl.BlockSpec((1, m, tn), lambda l, j: (l, 0, j)),
        out_shape=jax.ShapeDtypeStruct((depth, m, n6), F32),
        compiler_params=_cparams(2),
        name="adaln",
    )(c_all, w_ada, b_ada.reshape(depth, 1, n6))


def _pool_kernel(x_ref, g_ref, sc_ref, sh_ref, w_ref, wp_ref, ps_ref, st_ref, h_ref, ya_ref, ns_ref,
                 wbf, wpbf, carry, *, nb, tm, tiles_per_seq, start):
    i = pl.program_id(0)
    t = i % tiles_per_seq
    seq = i // tiles_per_seq
    c = POOL_WIDTH
    halo = POOL_HALO

    @pl.when(i == 0)
    def _():
        wbf[...] = w_ref[...].astype(BF16)
        wpbf[...] = wp_ref[...].astype(BF16)

    @pl.when(t == 0)
    def _():
        carry[...] = st_ref[...]

    h = (_rms(x_ref[...], g_ref[...]) * (1.0 + _mod_rows(sc_ref, seq)) + _mod_rows(sh_ref, seq)).astype(BF16)
    h_ref[...] = h
    p = jnp.dot(h, wbf[...], preferred_element_type=F32)
    p3 = p.reshape(nb, tm, c)
    ext3 = jnp.concatenate([carry[...], p3], axis=1)
    tail = ext3[:, tm:tm + halo, :]
    ns_ref[...] = tail[:, halo - POOL_STATE:, :]
    carry[...] = tail
    ext = ext3.reshape(nb * (halo + tm), c)
    pos = start + t * tm + lax.broadcasted_iota(I32, (1, tm, 1), 1)
    outs = []
    for gi, w in enumerate(POOL_WINDOWS):
        sl = slice(gi * POOL_GROUP, (gi + 1) * POOL_GROUP)
        acc = ext[:, sl]
        shift = 1
        while shift < w:
            acc = acc + pltpu.roll(acc, shift, 0)
            shift *= 2
        win = acc.reshape(nb, halo + tm, POOL_GROUP)[:, halo:, :]
        cnt = jnp.minimum(pos + 1, w).astype(F32)
        dgrp = win / cnt - p3[:, :, sl]
        outs.append(jnp.dot(dgrp.reshape(nb * tm, POOL_GROUP).astype(BF16), wpbf[gi],
                            preferred_element_type=F32))
    y = jnp.concatenate(outs, axis=-1) * ps_ref[...]
    ya_ref[...] = y.astype(BF16)


def _pool_mixer(x2, g, mod, layer, w_in, w_pool, pool_scale, state, nseq, nb, tm, start):
    rows, d = x2.shape
    tiles_per_seq = (rows // nseq) // tm
    seq_blocks = nseq // nb
    c = POOL_WIDTH
    if state is None:
        st = jnp.zeros((nseq, POOL_HALO, c), F32)
    else:
        st = jnp.pad(state, ((0, 0), (POOL_HALO - POOL_STATE, 0), (0, 0)))
    kern = functools.partial(_pool_kernel, nb=nb, tm=tm, tiles_per_seq=tiles_per_seq, start=start)
    h2, ya, ns = pl.pallas_call(
        kern,
        grid=(seq_blocks * tiles_per_seq,),
        in_specs=[pl.BlockSpec((nb * tm, d), lambda i: (i, 0)),
                  pl.BlockSpec((1, d), lambda i: (0, 0)),
                  _mod_spec(mod, layer, 1), _mod_spec(mod, layer, 0),
                  pl.BlockSpec((d, c), lambda i: (0, 0)),
                  pl.BlockSpec((len(POOL_WINDOWS), POOL_GROUP, POOL_GROUP), lambda i: (0, 0, 0)),
                  pl.BlockSpec((1, c), lambda i: (0, 0)),
                  pl.BlockSpec((nb, POOL_HALO, c), lambda i: (i // tiles_per_seq, 0, 0))],
        out_specs=[pl.BlockSpec((nb * tm, d), lambda i: (i, 0)),
                   pl.BlockSpec((nb * tm, c), lambda i: (i, 0)),
                   pl.BlockSpec((nb, POOL_STATE, c), lambda i: (i // tiles_per_seq, 0, 0))],
        out_shape=[jax.ShapeDtypeStruct((rows, d), BF16),
                   jax.ShapeDtypeStruct((rows, c), BF16),
                   jax.ShapeDtypeStruct((nseq, POOL_STATE, c), F32)],
        scratch_shapes=[pltpu.VMEM((d, c), BF16),
                        pltpu.VMEM((len(POOL_WINDOWS), POOL_GROUP, POOL_GROUP), BF16),
                        pltpu.VMEM((nb, POOL_HALO, c), F32)],
        compiler_params=_cparams(1),
        name="pool_mixer",
    )(x2, g.reshape(1, d), mod, mod, w_in, w_pool, pool_scale.reshape(1, c), st)
    return h2, ya, ns


def _conv_kernel(h_ref, wx_ref, wb_ref, wc_ref, cw_ref, st_ref, yb_ref, ns_ref,
                 wxbf, wbbf, wcbf, carry, *, nb, tm, tiles_per_seq):
    i = pl.program_id(1)
    t = i % tiles_per_seq
    tc = wxbf.shape[1]
    halo = CONV_HALO

    @pl.when(i == 0)
    def _():
        wxbf[...] = wx_ref[...].astype(BF16)
        wbbf[...] = wb_ref[...].astype(BF16)
        wcbf[...] = wc_ref[...].astype(BF16)

    @pl.when(t == 0)
    def _():
        carry[...] = st_ref[...]

    h = h_ref[...]
    xin = jnp.dot(h, wxbf[...], preferred_element_type=F32)
    gb = jnp.dot(h, wbbf[...], preferred_element_type=F32)
    gc = jnp.dot(h, wcbf[...], preferred_element_type=F32)
    z3 = (gc * xin).reshape(nb, tm, tc)
    ext3 = jnp.concatenate([carry[...], z3], axis=1)
    tail = ext3[:, tm:tm + halo, :]
    ns_ref[...] = tail[:, halo - (CONV_K - 1):, :]
    carry[...] = tail
    ext = ext3.reshape(nb * (halo + tm), tc)
    cw = cw_ref[...]
    conv = cw[0:1, :] * pltpu.roll(ext, 2, 0) + cw[1:2, :] * pltpu.roll(ext, 1, 0) + cw[2:3, :] * ext
    conv = conv.reshape(nb, halo + tm, tc)[:, halo:, :].reshape(nb * tm, tc)
    yb_ref[...] = (gb * conv).astype(BF16)


def _conv_mixer(h2, w_in, conv_w, state, nseq, nb, tm):
    rows, d = h2.shape
    tiles_per_seq = (rows // nseq) // tm
    seq_blocks = nseq // nb
    c = CONV_WIDTH
    tc = CONV_COL_TILE
    cb = c // tc
    base = POOL_WIDTH // tc
    if state is None:
        st = jnp.zeros((nseq, CONV_HALO, c), F32)
    else:
        st = jnp.pad(state, ((0, 0), (CONV_HALO - (CONV_K - 1), 0), (0, 0)))
    kern = functools.partial(_conv_kernel, nb=nb, tm=tm, tiles_per_seq=tiles_per_seq)
    yb, ns = pl.pallas_call(
        kern,
        grid=(cb, seq_blocks * tiles_per_seq),
        in_specs=[pl.BlockSpec((nb * tm, d), lambda j, i: (i, 0)),
                  pl.BlockSpec((d, tc), lambda j, i: (0, base + j)),
                  pl.BlockSpec((d, tc), lambda j, i: (0, base + cb + j)),
                  pl.BlockSpec((d, tc), lambda j, i: (0, base + 2 * cb + j)),
                  pl.BlockSpec((CONV_K, tc), lambda j, i: (0, j)),
                  pl.BlockSpec((nb, CONV_HALO, tc), lambda j, i: (i // tiles_per_seq, 0, j))],
        out_specs=[pl.BlockSpec((nb * tm, tc), lambda j, i: (i, j)),
                   pl.BlockSpec((nb, CONV_K - 1, tc), lambda j, i: (i // tiles_per_seq, 0, j))],
        out_shape=[jax.ShapeDtypeStruct((rows, c), BF16),
                   jax.ShapeDtypeStruct((nseq, CONV_K - 1, c), F32)],
        scratch_shapes=[pltpu.VMEM((d, tc), BF16)] * 3 + [pltpu.VMEM((nb, CONV_HALO, tc), F32)],
        compiler_params=_cparams(2),
        name="conv_mixer",
    )(h2, w_in, w_in, w_in, conv_w, st)
    return yb, ns


def _pack_bf16_pairs(v):
    c = v.shape[1] // 2
    return pltpu.bitcast(pltpu.pack_elementwise([v[:, :c], v[:, c:]], packed_dtype=BF16), U32)


def _store_token_tiles(ref, v):
    rows = v.shape[0]
    for j in range(V7X_SUBLANES):
        ref[pl.ds(j, rows, stride=V7X_SUBLANES), :] = v[:, j * V7X_LANES:(j + 1) * V7X_LANES]


def _load_token_tiles(ref):
    rows = ref.shape[0] // V7X_SUBLANES
    return jnp.concatenate([ref[pl.ds(j, rows, stride=V7X_SUBLANES), :] for j in range(V7X_SUBLANES)],
                           axis=-1)


def _unpack_pairs_f32(w):
    return tuple(pltpu.unpack_elementwise(w, index=k, packed_dtype=BF16, unpacked_dtype=F32) for k in range(2))


def _unpack_bf16_pairs(w):
    lo, hi = _unpack_pairs_f32(w)
    return lo.astype(BF16), hi.astype(BF16)


def _outproj_kernel(ya_ref, yb_ref, x_ref, g1_ref, sc_ref, sh_ref, ng_ref, wo_ref,
                    x1_ref, hp_ref, hpt_ref, wobf, *, tiles_per_seq):
    i = pl.program_id(0)
    seq = i // tiles_per_seq

    @pl.when(i == 0)
    def _():
        wobf[...] = wo_ref[...].astype(BF16)

    ycat = jnp.concatenate([ya_ref[...], yb_ref[...]], axis=-1)
    y = jnp.dot(ycat, wobf[...], preferred_element_type=F32)
    x1 = x_ref[...] + _mod_rows(g1_ref, seq) * y
    x1_ref[...] = x1
    h2 = _rms(x1, ng_ref[...]) * (1.0 + _mod_rows(sc_ref, seq)) + _mod_rows(sh_ref, seq)
    packed = _pack_bf16_pairs(h2)
    hp_ref[...] = packed
    _store_token_tiles(hpt_ref, packed)


def _outproj(ya, yb, x2, mod, layer, ng, w_out, seq_rows):
    rows_all, d = x2.shape
    half = ya.shape[1]
    rt = ROW_TILE
    row_spec = lambda w: pl.BlockSpec((rt, w), lambda i: (i, 0))
    return pl.pallas_call(
        functools.partial(_outproj_kernel, tiles_per_seq=seq_rows // rt),
        grid=(rows_all // rt,),
        in_specs=[row_spec(half), row_spec(half), row_spec(d),
                  _mod_spec(mod, layer, 2), _mod_spec(mod, layer, 4), _mod_spec(mod, layer, 3),
                  pl.BlockSpec((1, d), lambda i: (0, 0)),
                  pl.BlockSpec((d, d), lambda i: (0, 0), pipeline_mode=pl.Buffered(1))],
        out_specs=[row_spec(d), row_spec(d // 2),
                   pl.BlockSpec((rt * V7X_SUBLANES, V7X_LANES), lambda i: (i, 0))],
        out_shape=[jax.ShapeDtypeStruct((rows_all, d), F32),
                   jax.ShapeDtypeStruct((rows_all, d // 2), U32),
                   jax.ShapeDtypeStruct((rows_all * V7X_SUBLANES, V7X_LANES), U32)],
        scratch_shapes=[pltpu.VMEM((d, d), BF16)],
        compiler_params=_cparams(1),
        name="outproj",
    )(ya, yb, x2, mod, mod, mod, ng.reshape(1, d), w_out)


def _router_kernel(hpp_ref, hps_ref, wr_ref, br_ref, pos_ref, rw_ref, tab_ref,
                   cnt_acc, totals, starts, padded, before_ref, s_all, sel_all, *, nt_p, rows_s):
    ph = pl.program_id(0)
    t = pl.program_id(1)
    last = nt_p
    r = hpp_ref.shape[0]
    half = hpp_ref.shape[1]
    ne = N_EXPERTS
    sub = lax.broadcasted_iota(I32, (ne, V7X_LANES), 0)

    @pl.when(t == 0)
    def _():
        cnt_acc[...] = jnp.zeros_like(cnt_acc)

    @pl.when((ph == 0) & (t == 0))
    def _():
        starts[...] = jnp.zeros_like(starts)
        padded[...] = jnp.zeros_like(padded)

    @pl.when((ph == 1) & (t == 0))
    def _():
        pad = jnp.floor((totals[...] + (MOE_TILE - 1.0)) * (1.0 / MOE_TILE)) * MOE_TILE
        run = pad
        k = 1
        while k < ne:
            run = run + jnp.where(sub >= k, pltpu.roll(run, k, 0), 0.0)
            k *= 2
        padded[...] = pad
        starts[...] = run - pad

    is_s = t == last
    eid = lax.broadcasted_iota(I32, (ne, r), 0)
    n_valid = jnp.where(is_s, rows_s, r)
    tok = lax.broadcasted_iota(I32, (ne, r), 1)

    @pl.when(ph == 0)
    def _():
        w_s = jnp.concatenate([hps_ref[...], jnp.zeros((r - rows_s, half), U32)], axis=0)
        w = jnp.where(is_s, w_s, hpp_ref[...])
        lo, hi = _unpack_bf16_pairs(w)
        wr = wr_ref[...].astype(BF16)
        log_t = (lax.dot_general(wr[:, :half], lo, NT_DIMS, preferred_element_type=F32)
                 + lax.dot_general(wr[:, half:], hi, NT_DIMS, preferred_element_type=F32))

        s = jax.nn.sigmoid(log_t)
        sg = s + br_ref[...]
        within = eid % EXP_PER_GROUP
        grp = eid // EXP_PER_GROUP

        def group_rot(x, k):
            return jnp.where(within + k < EXP_PER_GROUP,
                             pltpu.roll(x, ne - k, 0), pltpu.roll(x, EXP_PER_GROUP - k, 0))

        rank = jnp.zeros((ne, r), I32)
        for k in range(1, EXP_PER_GROUP):
            mate = group_rot(sg, k)
            wrapped = within + k >= EXP_PER_GROUP
            ahead = (mate > sg) | (wrapped & (mate == sg))
            rank = rank + ahead.astype(I32)
        top2 = rank < TOP_K
        kept = jnp.where(top2, sg, 0.0)
        gscore = kept
        for k in range(1, EXP_PER_GROUP):
            gscore = gscore + group_rot(kept, k)
        win = None
        for k in range(1, N_EXPERT_GROUPS):
            other = pltpu.roll(gscore, EXP_PER_GROUP * k, 0)
            beats = (gscore > other) | ((grp < k) & (gscore == other))
            win = beats if win is None else (win & beats)
        picked_now = top2 & win & (tok < n_valid)
        s_all[t] = s
        sel_all[t] = picked_now.astype(F32)

    s = s_all[t]
    selb = sel_all[t]
    sel = selb > 0.5
    cnt_before = cnt_acc[...]
    cnt_new = cnt_before + jnp.sum(selb, axis=1, keepdims=True)
    cnt_acc[...] = cnt_new

    @pl.when((ph == 0) & (t == 0))
    def _():
        src = lax.broadcasted_iota(I32, (r, r), 0)
        dst = lax.broadcasted_iota(I32, (r, r), 1)
        before_ref[...] = (src < dst).astype(BF16)

    @pl.when((ph == 0) & (t == last))
    def _():
        totals[...] = cnt_new

    @pl.when(ph == 1)
    def _():
        picked = jnp.where(sel, s, 0.0)
        wsum = jnp.sum(picked, axis=0, keepdims=True)
        gate = picked / jnp.where(tok[0:1, :] < n_valid, wsum, 1.0)
        ranks = jnp.dot(selb.astype(BF16), before_ref[...], preferred_element_type=F32)
        slot = (starts[...][:, 0:1] + cnt_before[:, 0:1] + ranks).astype(I32)
        e_a = jnp.min(jnp.where(sel, eid, ne), axis=0, keepdims=True)
        e_b = jnp.max(jnp.where(sel, eid, -1), axis=0, keepdims=True)
        is_a = sel & (eid == e_a)
        is_b = sel & (eid == e_b)
        pos_a = jnp.sum(jnp.where(is_a, slot, 0), axis=0, keepdims=True)
        pos_b = jnp.sum(jnp.where(is_b, slot, 0), axis=0, keepdims=True)
        w_a = jnp.sum(jnp.where(is_a, gate, 0.0), axis=0, keepdims=True)
        w_b = jnp.sum(jnp.where(is_b, gate, 0.0), axis=0, keepdims=True)
        pos_ref[0] = jnp.concatenate([pos_a, pos_b], axis=0)
        wmat = jnp.concatenate([w_a, w_b, jnp.zeros((V7X_LANES - 2, r), F32)], axis=0)
        rw_ref[...] = wmat.T

    @pl.when((ph == 1) & (t == last))
    def _():
        ends = starts[...] + padded[...]
        lane = lax.broadcasted_iota(I32, (ne, V7X_LANES), 1)
        tile_start = (lane * MOE_TILE).astype(F32)
        te = jnp.sum((tile_start >= ends).astype(I32), axis=0, keepdims=True)
        valid = te < ne
        last_e = jnp.max(jnp.where(padded[...] > 0.0, sub, 0), axis=0, keepdims=True)
        te = jnp.where(valid, te, last_e)
        n_used = jnp.sum(valid.astype(I32), axis=1, keepdims=True) + jnp.zeros((1, V7X_LANES), I32)
        last_tile = jnp.where(padded[...] > 0.0, ends - MOE_TILE, -1.0).astype(I32)
        last_tile_row = jnp.sum(jnp.where(sub == lane, last_tile, 0), axis=0, keepdims=True)
        later = jnp.min(jnp.where((sub > te) & (padded[...] > 0.0), sub, ne), axis=0, keepdims=True)
        next_e = jnp.where(later < ne, later, -1)
        zero = jnp.zeros((1, V7X_LANES), I32)
        tab_ref[...] = jnp.concatenate([te, valid.astype(I32), last_tile_row, n_used, next_e,
                                        zero, zero, zero], axis=0)


def _router(hp_p, hp_s, w_router, b_router):
    n_p, half = hp_p.shape
    rows_s = hp_s.shape[0]
    r = ROUTER_TILE
    nt_p = n_p // r
    nt = nt_p + 1
    kern = functools.partial(_router_kernel, nt_p=nt_p, rows_s=rows_s)
    pos, rw, tab = pl.pallas_call(
        kern,
        grid=(2, nt),
        in_specs=[pl.BlockSpec((r, half), lambda p, t: (jnp.minimum(t, nt_p - 1) * (1 - p), 0)),
                  pl.BlockSpec((rows_s, half), lambda p, t: (0, 0)),
                  pl.BlockSpec((N_EXPERTS, 2 * half), lambda p, t: (0, 0)),
                  pl.BlockSpec((N_EXPERTS, 1), lambda p, t: (0, 0))],
        out_specs=[pl.BlockSpec((1, TOP_K, r), lambda p, t: (p * t, 0, 0)),
                   pl.BlockSpec((r, V7X_LANES), lambda p, t: (p * t, 0)),
                   pl.BlockSpec((V7X_SUBLANES, V7X_LANES), lambda p, t: (0, 0))],
        out_shape=[jax.ShapeDtypeStruct((nt, TOP_K, r), I32),
                   jax.ShapeDtypeStruct((nt * r, V7X_LANES), F32),
                   jax.ShapeDtypeStruct((V7X_SUBLANES, V7X_LANES), I32)],
        scratch_shapes=[pltpu.VMEM((N_EXPERTS, V7X_LANES), F32)] * 4 + [pltpu.VMEM((r, r), BF16)]
        + [pltpu.VMEM((nt, N_EXPERTS, r), F32)] * 2,
        compiler_params=_cparams(2),
        name="router",
    )(hp_p, hp_s, w_router.T, b_router.reshape(N_EXPERTS, 1))
    return pos.reshape(-1), rw, tab.reshape(-1)


def _pos_index(tok0):
    return (tok0 // ROUTER_TILE) * (TOP_K * ROUTER_TILE) + tok0 % ROUTER_TILE


def _tokens(ref, first, n=1):
    start = pl.multiple_of(first * V7X_SUBLANES, V7X_SUBLANES)
    return ref.at[pl.ds(start, n * V7X_SUBLANES), :]


def _dispatch_kernel(pos_ref, tab_ref, hpp_ref, hps_ref, xs_ref, zbuf, sem, *, n_p_steps):
    i = pl.program_id(0)
    rows = hpp_ref.shape[0] // V7X_SUBLANES

    @pl.when(i == 0)
    def _():
        zbuf[...] = jnp.zeros_like(zbuf)

        def fill(e):
            first = pl.multiple_of(tab_ref[TAB_LAST_TILE * V7X_LANES + e], MOE_TILE)
            return pltpu.make_async_copy(zbuf, _tokens(xs_ref, first, MOE_TILE), sem)

        for e in range(N_EXPERTS):
            @pl.when(tab_ref[TAB_LAST_TILE * V7X_LANES + e] >= 0)
            def _():
                fill(e).start()
        for e in range(N_EXPERTS):
            @pl.when(tab_ref[TAB_LAST_TILE * V7X_LANES + e] >= 0)
            def _():
                fill(e).wait()

        def tail(j):
            first = pl.multiple_of(j * MOE_TILE, MOE_TILE)
            return pltpu.make_async_copy(zbuf, _tokens(xs_ref, first, MOE_TILE), sem)

        def tail_start(j, carry):
            tail(j).start()
            return carry

        def tail_wait(j, carry):
            tail(j).wait()
            return carry

        n_used = tab_ref[TAB_NUSED * V7X_LANES]
        n_tiles = xs_ref.shape[0] // (MOE_TILE * V7X_SUBLANES)
        lax.fori_loop(n_used, n_tiles, tail_start, 0)
        lax.fori_loop(n_used, n_tiles, tail_wait, 0)

    def scatter(src_ref, tok0):
        n = src_ref.shape[0] // V7X_SUBLANES
        base = _pos_index(tok0)

        def row_copy(r, dst):
            return pltpu.make_async_copy(_tokens(src_ref, r), _tokens(xs_ref, dst), sem)

        def issue(r, carry):
            row_copy(r, pos_ref[base + r]).start()
            row_copy(r, pos_ref[base + ROUTER_TILE + r]).start(priority=1)
            return carry

        lax.fori_loop(0, n, issue, 0, unroll=8)
        block = pltpu.make_async_copy(src_ref, _tokens(xs_ref, 0, n), sem)
        for _ in range(TOP_K):
            block.wait()

    @pl.when(i < n_p_steps)
    def _():
        scatter(hpp_ref, i * rows)

    @pl.when(i == n_p_steps)
    def _():
        scatter(hps_ref, n_p_steps * rows)


def _dispatch(pos, tab, hpt_p, hpt_s, n_rows_sorted):
    sub = V7X_SUBLANES
    n_p_steps = hpt_p.shape[0] // (MATMUL_TILE * sub)
    kern = functools.partial(_dispatch_kernel, n_p_steps=n_p_steps)
    blk = (MATMUL_TILE * sub, V7X_LANES)
    return pl.pallas_call(
        kern,
        grid_spec=pltpu.PrefetchScalarGridSpec(
            num_scalar_prefetch=2,
            grid=(n_p_steps + 1,),
            in_specs=[pl.BlockSpec(blk, lambda i, p, t: (jnp.minimum(i, n_p_steps - 1), 0)),
                      pl.BlockSpec(hpt_s.shape, lambda i, p, t: (0, 0))],
            out_specs=pl.BlockSpec(memory_space=pl.ANY),
            scratch_shapes=[pltpu.VMEM((MOE_TILE * sub, V7X_LANES), U32), pltpu.SemaphoreType.DMA(())]),
        out_shape=jax.ShapeDtypeStruct((n_rows_sorted * sub, V7X_LANES), U32),
        compiler_params=_cparams(1),
        name="moe_dispatch",
    )(pos, tab, hpt_p, hpt_s)


def _experts_kernel(tab_ref, xs_ref, wg_hbm, wu_hbm, wd_hbm, ys_ref,
                    wg32, wu32, wd32, wgbf, wubf, wdbf, slot_ref, sems, *, layer):
    i = pl.program_id(0)
    expert = tab_ref[TAB_EXPERT * V7X_LANES + i]
    prev = tab_ref[TAB_EXPERT * V7X_LANES + jnp.maximum(i - 1, 0)]
    upcoming = tab_ref[TAB_NEXT * V7X_LANES + i]
    changed = (i == 0) | (expert != prev)

    def weight_copies(e, slot):
        return (pltpu.make_async_copy(wg_hbm.at[layer, e], wg32.at[slot], sems.at[0, slot]),
                pltpu.make_async_copy(wu_hbm.at[layer, e], wu32.at[slot], sems.at[1, slot]),
                pltpu.make_async_copy(wd_hbm.at[layer, e], wd32.at[slot], sems.at[2, slot]))

    @pl.when(i == 0)
    def _():
        slot_ref[0] = 0
        for cp in weight_copies(expert, 0):
            cp.start()

    @pl.when(changed & (i > 0))
    def _():
        slot_ref[0] = 1 - slot_ref[0]

    def mlp(wg, wu, wd):
        x = jnp.concatenate(_unpack_bf16_pairs(_load_token_tiles(xs_ref)), axis=-1)
        a = jnp.dot(x, wg, preferred_element_type=F32)
        b = jnp.dot(x, wu, preferred_element_type=F32)
        hid = (a * jax.nn.sigmoid(a)) * b
        y = jnp.dot(hid.astype(BF16), wd, preferred_element_type=F32)
        _store_token_tiles(ys_ref, _pack_bf16_pairs(y))

    for slot in range(2):
        @pl.when(changed & (slot_ref[0] == slot))
        def _():
            for cp in weight_copies(expert, slot):
                cp.wait()

            @pl.when(upcoming >= 0)
            def _():
                for cp in weight_copies(upcoming, 1 - slot):
                    cp.start(priority=1)

            wg = wg32[slot].astype(BF16)
            wu = wu32[slot].astype(BF16)
            wd = wd32[slot].astype(BF16)
            wgbf[...] = wg
            wubf[...] = wu
            wdbf[...] = wd
            mlp(wg, wu, wd)

    valid = tab_ref[TAB_VALID * V7X_LANES + i] > 0

    @pl.when(valid & jnp.logical_not(changed))
    def _():
        mlp(wgbf[...], wubf[...], wdbf[...])

    @pl.when(jnp.logical_not(valid))
    def _():
        ys_ref[...] = jnp.zeros_like(ys_ref)


def _experts(tab, xs, w_gate, w_up, w_down, layer):
    sub = V7X_SUBLANES
    _, _, d, f = w_gate.shape
    nt = xs.shape[0] // (MOE_TILE * sub)
    assert nt <= V7X_LANES and d == 2 * sub * V7X_LANES
    blk = (MOE_TILE * sub, V7X_LANES)

    def tile(i, tab_ref):
        return jnp.minimum(i, tab_ref[TAB_NUSED * V7X_LANES] - 1)

    hbm = pl.BlockSpec(memory_space=pl.ANY)
    return pl.pallas_call(
        functools.partial(_experts_kernel, layer=layer),
        grid_spec=pltpu.PrefetchScalarGridSpec(
            num_scalar_prefetch=1,
            grid=(nt,),
            in_specs=[pl.BlockSpec(blk, lambda i, t: (tile(i, t), 0)), hbm, hbm, hbm],
            out_specs=pl.BlockSpec(blk, lambda i, t: (i, 0)),
            scratch_shapes=[pltpu.VMEM((2, d, f), F32), pltpu.VMEM((2, d, f), F32), pltpu.VMEM((2, f, d), F32),
                            pltpu.VMEM((d, f), BF16), pltpu.VMEM((d, f), BF16), pltpu.VMEM((f, d), BF16),
                            pltpu.SMEM((1,), I32), pltpu.SemaphoreType.DMA((3, 2))]),
        out_shape=jax.ShapeDtypeStruct(xs.shape, U32),
        compiler_params=_cparams(1),
        name="moe_experts",
    )(tab, xs, w_gate, w_up, w_down)


def _combine_kernel(pos_ref, ys_ref, rw_ref, ng_ref, *rest, tiles_per_seq, n_p_steps, final):
    n_in = 4
    n_out = 1 if final else 2
    trunk_in = (rest[:n_in], rest[n_in:2 * n_in])
    outs = rest[2 * n_in:2 * n_in + 2 * n_out]
    trunk_out = (outs[:n_out], outs[n_out:])
    buf0, buf1, sems = rest[2 * n_in + 2 * n_out:]
    bufs = (buf0, buf1)
    i = pl.program_id(0)
    n_steps = n_p_steps + 1
    rows = rw_ref.shape[0]

    def gather(step, slot):
        base = _pos_index(step * rows)

        def issue(r, carry):
            for k in range(TOP_K):
                src = pos_ref[base + k * ROUTER_TILE + r]
                pltpu.make_async_copy(_tokens(ys_ref, src), _tokens(bufs[slot].at[k], r),
                                      sems.at[slot]).start(priority=k)
            return carry

        lax.fori_loop(0, rows, issue, 0, unroll=8)

    def drain(slot):
        for k in range(TOP_K):
            pltpu.make_async_copy(_tokens(ys_ref, 0, rows), bufs[slot].at[k], sems.at[slot]).wait()

    def finish(slot, trunk, seq):
        x1_ref, g2_ref, sc_ref, sh_ref = trunk_in[trunk]
        rw = rw_ref[...]
        lo_a, hi_a = _unpack_pairs_f32(_load_token_tiles(bufs[slot].at[0]))
        lo_b, hi_b = _unpack_pairs_f32(_load_token_tiles(bufs[slot].at[1]))
        w_a = rw[:, 0:1]
        w_b = rw[:, 1:2]
        moe = jnp.concatenate([w_a * lo_a + w_b * lo_b, w_a * hi_a + w_b * hi_b], axis=-1)
        x2 = x1_ref[...] + _mod_rows(g2_ref, seq) * moe
        if final:
            trunk_out[trunk][0][...] = _rms(x2, ng_ref[...])
        else:
            trunk_out[trunk][0][...] = x2
            trunk_out[trunk][1][...] = (_rms(x2, ng_ref[...]) * (1.0 + _mod_rows(sc_ref, seq))
                                        + _mod_rows(sh_ref, seq)).astype(BF16)

    @pl.when(i == 0)
    def _():
        gather(0, 0)

    for slot in range(2):
        @pl.when(i % 2 == slot)
        def _():
            @pl.when(i + 1 < n_steps)
            def _():
                gather(i + 1, 1 - slot)

            drain(slot)

            @pl.when(i < n_p_steps)
            def _():
                finish(slot, 0, i // tiles_per_seq)

            @pl.when(i == n_p_steps)
            def _():
                finish(slot, 1, 0)


def _combine(pos, ys, x1_p, x1_s, rw, mod_p, mod_s, layer, ng, seq_rows_p, final):
    n_p, d = x1_p.shape
    rt = ROW_TILE
    assert x1_s.shape[0] == rt and n_p % rt == 0 and ROUTER_TILE % rt == 0
    n_p_steps = n_p // rt
    kern = functools.partial(_combine_kernel, tiles_per_seq=seq_rows_p // rt, n_p_steps=n_p_steps, final=final)
    p_spec = lambda w: pl.BlockSpec((rt, w), lambda i, p: (jnp.minimum(i, n_p_steps - 1), 0))
    s_spec = lambda w: pl.BlockSpec((rt, w), lambda i, p: (0, 0))
    nxt = min(layer + 1, mod_p.shape[0] - 1)

    def trunk_specs(spec, mod):
        return [spec(d), _mod_spec(mod, layer, 5), _mod_spec(mod, nxt, 1), _mod_spec(mod, nxt, 0)]

    out_dtypes = [F32] if final else [F32, BF16]
    out_shape = ([jax.ShapeDtypeStruct((n_p, d), t) for t in out_dtypes]
                 + [jax.ShapeDtypeStruct((rt, d), t) for t in out_dtypes])
    out_specs = [p_spec(d)] * len(out_dtypes) + [s_spec(d)] * len(out_dtypes)
    res = pl.pallas_call(
        kern,
        grid_spec=pltpu.PrefetchScalarGridSpec(
            num_scalar_prefetch=1,
            grid=(n_p_steps + 1,),
            in_specs=[pl.BlockSpec(memory_space=pl.ANY),
                      pl.BlockSpec((rt, V7X_LANES), lambda i, p: (i, 0)),
                      pl.BlockSpec((1, d), lambda i, p: (0, 0))]
            + trunk_specs(p_spec, mod_p) + trunk_specs(s_spec, mod_s),
            out_specs=out_specs,
            scratch_shapes=[pltpu.VMEM((TOP_K, rt * V7X_SUBLANES, V7X_LANES), U32)] * 2
            + [pltpu.SemaphoreType.DMA((2,))]),
        out_shape=out_shape,
        compiler_params=_cparams(1),
        name="moe_combine_final" if final else "moe_combine",
    )(pos, ys, rw, ng.reshape(1, d), x1_p, mod_p, mod_p, mod_p, x1_s, mod_s, mod_s, mod_s)
    return res[:len(out_dtypes)], res[len(out_dtypes):]


def _moe(out_p, out_s, mod_p, mod_s, layer, ng, seq_rows_p, w_router, b_router, w_gate, w_up, w_down, final):
    x1_p, hp_p, hpt_p = out_p
    x1_s, hp_s, hpt_s = out_s
    n_p = x1_p.shape[0]
    n_tok = n_p + x1_s.shape[0]
    max_rows = TOP_K * n_tok + N_EXPERTS * (MOE_TILE - 1)
    n_rows_sorted = -(-max_rows // MOE_TILE) * MOE_TILE
    pos, rw, tab = _router(hp_p, hp_s, w_router, b_router)
    xs = _dispatch(pos, tab, hpt_p, hpt_s, n_rows_sorted)
    ys = _experts(tab, xs, w_gate, w_up, w_down, layer)
    return _combine(pos, ys, x1_p, x1_s, rw, mod_p, mod_s, layer, ng, seq_rows_p, final)


def _gmlp_kernel(h_ref, w_ref, lg_ref, lb_ref, ws_ref, bs_ref, yc_ref, *rest, ell, blk, emit_v):
    if emit_v:
        gv_ref, wbf, wsbf = rest
    else:
        wbf, wsbf = rest
    i = pl.program_id(0)
    rows = h_ref.shape[0]
    c = GM_WIDTH

    @pl.when(i == 0)
    def _():
        wbf[...] = w_ref[...].astype(BF16)
        r = lax.broadcasted_iota(I32, (ell, ell), 0)
        s = lax.broadcasted_iota(I32, (ell, ell), 1)
        keep = (r >= s) & ((r // blk) == (s // blk))
        rsel = (lax.broadcasted_iota(I32, (ell, CHUNK), 0) % blk
                == lax.broadcasted_iota(I32, (ell, CHUNK), 1)).astype(BF16)
        csel = (lax.broadcasted_iota(I32, (CHUNK, ell), 1) % blk
                == lax.broadcasted_iota(I32, (CHUNK, ell), 0)).astype(BF16)
        for g in range(GM_GROUPS):
            wchunk = ws_ref[g].astype(BF16)
            if blk == ell:
                full = wchunk
            else:
                rowsp = jnp.dot(rsel, wchunk, preferred_element_type=F32).astype(BF16)
                full = jnp.dot(rowsp, csel, preferred_element_type=F32).astype(BF16)
            wsbf[g] = jnp.where(keep, full, jnp.zeros_like(full))

    uv = jnp.dot(h_ref[...], wbf[...], preferred_element_type=F32)
    u = uv[:, :c]
    v = uv[:, c:]
    vc = v - jnp.mean(v, axis=-1, keepdims=True)
    vn = vc * lax.rsqrt(jnp.mean(vc * vc, axis=-1, keepdims=True) + EPS) * lg_ref[...] + lb_ref[...]
    if emit_v:
        gv_ref[...] = vn
    vb = vn.astype(BF16)
    bs = bs_ref[...]
    for ch in range(rows // ell):
        rs = slice(ch * ell, (ch + 1) * ell)
        outs = []
        for g in range(GM_GROUPS):
            cs = slice(g * GM_GROUP, (g + 1) * GM_GROUP)
            mixed = jnp.dot(wsbf[g], vb[rs, cs], preferred_element_type=F32)
            mixed = (mixed.reshape(ell // blk, blk, GM_GROUP) + bs[:blk, g:g + 1][None]).reshape(ell, GM_GROUP)
            outs.append(u[rs, cs] * mixed)
        yc_ref[rs, :] = jnp.concatenate(outs, axis=-1).astype(BF16)


def _gmlp_mixer(h2, w_in, ln_g, ln_b, ws, bs_t, ell, blk, emit_v):
    rows, d = h2.shape
    c = GM_WIDTH
    kern = functools.partial(_gmlp_kernel, ell=ell, blk=blk, emit_v=emit_v)
    rt = min(rows, MATMUL_TILE)
    out_specs = [pl.BlockSpec((rt, c), lambda i: (i, 0))]
    out_shape = [jax.ShapeDtypeStruct((rows, c), BF16)]
    if emit_v:
        out_specs.append(pl.BlockSpec((rt, c), lambda i: (i, 0)))
        out_shape.append(jax.ShapeDtypeStruct((rows, c), F32))
    return pl.pallas_call(
        kern,
        grid=(rows // rt,),
        in_specs=[pl.BlockSpec((rt, d), lambda i: (i, 0)),
                  pl.BlockSpec((d, 2 * c), lambda i: (0, 0), pipeline_mode=pl.Buffered(1)),
                  pl.BlockSpec((1, c), lambda i: (0, 0)),
                  pl.BlockSpec((1, c), lambda i: (0, 0)),
                  pl.BlockSpec((GM_GROUPS, CHUNK, CHUNK), lambda i: (0, 0, 0)),
                  pl.BlockSpec((CHUNK, GM_GROUPS), lambda i: (0, 0))],
        out_specs=out_specs,
        out_shape=out_shape,
        scratch_shapes=[pltpu.VMEM((d, 2 * c), BF16), pltpu.VMEM((GM_GROUPS, ell, ell), BF16)],
        compiler_params=_cparams(1),
        name="gmlp_mixer",
    )(h2, w_in, ln_g.reshape(1, c), ln_b.reshape(1, c), ws, bs_t)


PAIR_W = 2 * HEAD_DIM
PAIRS_PER_KV = N_HEADS // N_KV // 2
NT_DIMS = (((1,), (1,)), ((), ()))
SCORE_SCALE = HEAD_DIM ** -0.5
assert float(np.log2(SCORE_SCALE)).is_integer()


def _swa_project(i, h_ref, wq_ref, wkv_ref, wbf):
    nq = N_HEADS * HEAD_DIM

    @pl.when(i == 0)
    def _():
        wbf[:, :nq] = wq_ref[...].astype(BF16)
        wbf[:, nq:] = wkv_ref[...].astype(BF16)

    return jnp.dot(h_ref[...], wbf[...], preferred_element_type=F32)


def _pair_block_diag(a, a_swapped, hk, axis):
    dim_axis = 1 - axis
    low = lax.broadcasted_iota(I32, a.shape, dim_axis) < HEAD_DIM
    lo, hi = (a, a_swapped) if hk == 0 else (a_swapped, a)
    return jnp.concatenate([jnp.where(low, lo, 0.0), jnp.where(low, 0.0, hi)], axis=axis).astype(BF16)


def _stack_pairs(qkv, rs, hk, scale=None):
    p0 = hk * PAIRS_PER_KV
    q = jnp.concatenate([qkv[rs, (p0 + pp) * PAIR_W:(p0 + pp + 1) * PAIR_W]
                         for pp in range(PAIRS_PER_KV)], axis=0)
    return (q if scale is None else q * scale).astype(BF16)


def _swa_cached_kernel(h_ref, wq_ref, wkv_ref, kp_ref, vp_ref, bias_ref, sink_ref, yd_ref, k_ref, v_ref,
                       wbf, *, tq):
    i = pl.program_id(0)
    rows = h_ref.shape[0]
    nq = N_HEADS * HEAD_DIM
    nkv = N_KV * HEAD_DIM
    n_blocks = rows // tq
    qkv = _swa_project(i, h_ref, wq_ref, wkv_ref, wbf)
    k_new = qkv[:, nq:nq + nkv]
    v_new = qkv[:, nq + nkv:]
    k_ref[...] = k_new
    v_ref[...] = v_new
    pad = jnp.zeros((WINDOW - tq, nkv), F32)

    scores, vbds = [], []
    for blk in range(n_blocks):
        rs = slice(blk * tq, (blk + 1) * tq)
        kcat = jnp.concatenate([kp_ref[blk], k_new[rs], pad], axis=0)
        vcat = jnp.concatenate([vp_ref[blk], v_new[rs], pad], axis=0)
        kswap = pltpu.roll(kcat, HEAD_DIM, 1)
        vswap = pltpu.roll(vcat, HEAD_DIM, 1)
        per_head = []
        for hk in range(N_KV):
            kbd = _pair_block_diag(kcat, kswap, hk, 0)
            vbds.append(_pair_block_diag(vcat, vswap, hk, 0))
            s4 = lax.dot_general(_stack_pairs(qkv, rs, hk), kbd, NT_DIMS,
                                 preferred_element_type=F32) * (HEAD_DIM ** -0.5)
            for pp in range(PAIRS_PER_KV):
                for sub in range(2):
                    per_head.append(s4[pp * tq:(pp + 1) * tq, sub * 2 * WINDOW:(sub + 1) * 2 * WINDOW])
        scores.append(jnp.concatenate(per_head, axis=0))

    s_all = jnp.stack(scores, axis=0) + bias_ref[...][None]
    sink = sink_ref[...][None]
    m = jnp.maximum(jnp.max(s_all, axis=-1, keepdims=True), sink)
    pr = jnp.exp(s_all - m)
    pr = pr / (jnp.sum(pr, axis=-1, keepdims=True) + jnp.exp(sink - m))

    for blk in range(n_blocks):
        outs = []
        for hk in range(N_KV):
            p4 = []
            for pp in range(PAIRS_PER_KV):
                h0 = 2 * (hk * PAIRS_PER_KV + pp)
                p4.append(jnp.concatenate([pr[blk, h0 * tq:(h0 + 1) * tq, :],
                                           pr[blk, (h0 + 1) * tq:(h0 + 2) * tq, :]], axis=-1))
            o4 = jnp.dot(jnp.concatenate(p4, axis=0).astype(BF16), vbds[blk * N_KV + hk],
                         preferred_element_type=F32)
            outs.extend(o4[pp * tq:(pp + 1) * tq, :] for pp in range(PAIRS_PER_KV))
        yd_ref[blk * tq:(blk + 1) * tq, :] = jnp.concatenate(outs, axis=-1).astype(BF16)


def _swa_stream_kernel(h_ref, wq_ref, wkv_ref, bias_ref, sink_ref, yd_ref, k_ref, v_ref,
                       wbf, kprev, vprev_t, *, blocks_per_seq):
    i = pl.program_id(0)
    rows = h_ref.shape[0]
    nq = N_HEADS * HEAD_DIM
    nkv = N_KV * HEAD_DIM
    tq = WINDOW
    n_blocks = rows // tq

    @pl.when(i == 0)
    def _():
        kprev[...] = jnp.zeros_like(kprev)
        vprev_t[...] = jnp.zeros_like(vprev_t)

    qkv = _swa_project(i, h_ref, wq_ref, wkv_ref, wbf)
    k_new = qkv[:, nq:nq + nkv]
    v_new = qkv[:, nq + nkv:]
    k_ref[...] = k_new
    v_ref[...] = v_new
    v_new_t = v_new.T
    lanes = PAIRS_PER_KV * tq

    for blk in range(n_blocks):
        rs = slice(blk * tq, (blk + 1) * tq)
        first = ((i * n_blocks + blk) % blocks_per_seq == 0).astype(I32)
        k_cur = k_new[rs]
        v_cur_t = v_new_t[:, rs]
        kcat = jnp.concatenate([kprev[...], k_cur], axis=0)
        vcat_t = jnp.concatenate([vprev_t[...], v_cur_t], axis=1)
        kprev[...] = k_cur
        vprev_t[...] = v_cur_t
        kswap = pltpu.roll(kcat, HEAD_DIM, 1)
        vswap_t = pltpu.roll(vcat_t, HEAD_DIM, 0)
        outs = []
        for hk in range(N_KV):
            kbd = _pair_block_diag(kcat, kswap, hk, 0)
            vbd_t = _pair_block_diag(vcat_t, vswap_t, hk, 1)
            st = lax.dot_general(kbd, _stack_pairs(qkv, rs, hk, SCORE_SCALE), NT_DIMS,
                                 preferred_element_type=F32)
            s3 = st.reshape(2, 2 * WINDOW, lanes) + bias_ref[first, hk]
            sink = sink_ref[hk]
            m = jnp.maximum(jnp.max(s3, axis=1, keepdims=True), sink)
            pr = jnp.exp(s3 - m)
            inv = 1.0 / (jnp.sum(pr, axis=1, keepdims=True) + jnp.exp(sink - m))
            o_t = jnp.dot(vbd_t, pr.reshape(4 * WINDOW, lanes).astype(BF16),
                          preferred_element_type=F32)
            norm = jnp.concatenate([jnp.broadcast_to(inv[sub], (HEAD_DIM, lanes)) for sub in range(2)], axis=0)
            o4 = (o_t * norm).T
            outs.extend(o4[pp * tq:(pp + 1) * tq, :] for pp in range(PAIRS_PER_KV))
        yd_ref[rs, :] = jnp.concatenate(outs, axis=-1).astype(BF16)


def _swa_weight_specs(w_in, d):
    nq = N_HEADS * HEAD_DIM
    nkv = N_KV * HEAD_DIM
    nw = nq + 2 * nkv
    q_blk = (w_in.shape[1] - nw) // nq
    kv_blk = (w_in.shape[1] - 2 * nkv) // (2 * nkv)
    assert q_blk * nq + nw == w_in.shape[1] and kv_blk * 2 * nkv + 2 * nkv == w_in.shape[1]
    return [pl.BlockSpec((d, nq), lambda i: (0, q_blk)), pl.BlockSpec((d, 2 * nkv), lambda i: (0, kv_blk))]


def _swa_outputs(rows, rt):
    nq = N_HEADS * HEAD_DIM
    nkv = N_KV * HEAD_DIM
    specs = [pl.BlockSpec((rt, nq), lambda i: (i, 0)),
             pl.BlockSpec((rt, nkv), lambda i: (i, 0)),
             pl.BlockSpec((rt, nkv), lambda i: (i, 0))]
    shapes = [jax.ShapeDtypeStruct((rows, nq), BF16),
              jax.ShapeDtypeStruct((rows, nkv), F32),
              jax.ShapeDtypeStruct((rows, nkv), F32)]
    return specs, shapes


def _swa_cached_mixer(h2, w_in, k_cache, v_cache, bias, sinks, tq):
    rows, d = h2.shape
    nkv = N_KV * HEAD_DIM
    nw = N_HEADS * HEAD_DIM + 2 * nkv
    n_blocks = ROW_TILE // tq
    cache_spec = pl.BlockSpec((n_blocks, WINDOW, nkv), lambda i: (i, 0, 0))
    out_specs, out_shape = _swa_outputs(rows, ROW_TILE)
    return pl.pallas_call(
        functools.partial(_swa_cached_kernel, tq=tq),
        grid=(rows // ROW_TILE,),
        in_specs=[pl.BlockSpec((ROW_TILE, d), lambda i: (i, 0))] + _swa_weight_specs(w_in, d)
        + [cache_spec, cache_spec,
           pl.BlockSpec((N_HEADS * tq, 2 * WINDOW), lambda i: (0, 0)),
           pl.BlockSpec((N_HEADS * tq, 1), lambda i: (0, 0))],
        out_specs=out_specs,
        out_shape=out_shape,
        scratch_shapes=[pltpu.VMEM((d, nw), BF16)],
        compiler_params=_cparams(1),
        name="swa_cached",
    )(h2, w_in, w_in, k_cache, v_cache, bias, sinks)


def _swa_stream_mixer(h2, w_in, bias_t, sinks_t, blocks_per_seq):
    rows, d = h2.shape
    nkv = N_KV * HEAD_DIM
    nw = N_HEADS * HEAD_DIM + 2 * nkv
    lanes = PAIRS_PER_KV * WINDOW
    rt = min(rows, MATMUL_TILE)
    out_specs, out_shape = _swa_outputs(rows, rt)
    return pl.pallas_call(
        functools.partial(_swa_stream_kernel, blocks_per_seq=blocks_per_seq),
        grid=(rows // rt,),
        in_specs=[pl.BlockSpec((rt, d), lambda i: (i, 0))] + _swa_weight_specs(w_in, d)
        + [pl.BlockSpec((2, N_KV, 2, 2 * WINDOW, lanes), lambda i: (0, 0, 0, 0, 0)),
           pl.BlockSpec((N_KV, 2, 1, lanes), lambda i: (0, 0, 0, 0))],
        out_specs=out_specs,
        out_shape=out_shape,
        scratch_shapes=[pltpu.VMEM((d, nw), BF16), pltpu.VMEM((WINDOW, nkv), F32),
                        pltpu.VMEM((nkv, WINDOW), F32)],
        compiler_params=_cparams(1),
        name="swa_stream",
    )(h2, w_in, w_in, bias_t, sinks_t)


def _t5_bucket(dist):
    max_exact = N_BUCKETS // 2
    dd = np.maximum(dist, 1)
    large = max_exact + (np.log(dd / max_exact) / np.log(WINDOW / max_exact)
                         * (N_BUCKETS - max_exact)).astype(np.int64)
    large = np.minimum(large, N_BUCKETS - 1)
    return np.where(dist < max_exact, dist, large).astype(np.int32)


def _attention_bias(rel_bias):
    by_dist = jnp.take(rel_bias.astype(F32), _t5_bucket(np.arange(WINDOW)), axis=0).T
    neg = jnp.full((N_HEADS, WINDOW), NEG_INF, F32)
    line = jnp.concatenate([neg, by_dist[:, ::-1], neg[:, :WINDOW - 1]], axis=1)
    rows = line[:, None, :]
    span = 1
    while span < WINDOW:
        rows = jnp.concatenate([rows[:, :, span:], rows[:, :, :rows.shape[2] - span]], axis=1)
        span *= 2
    return rows


def kernel(x_prompt, x_sample, state_pool, state_conv, cache_swa_k, cache_swa_v, c_prompt, c_sample, w_ada, b_ada, norm_g, final_norm_g, w_in_even, w_out_even, w_pool, pool_scale, conv_w, w_in_odd, w_out_odd, gm_norm_g, gm_norm_b, gm_w_s, gm_b_s, attn_sinks, rel_bias, w_router, b_router, w_gate, w_up, w_down):
    d = D_MODEL
    bp, tp, _ = x_prompt.shape
    bs, ts, _ = x_sample.shape
    rows_s = bs * ts
    assert rows_s == ROW_TILE and tp % ROUTER_TILE == 0 and PAST_LEN % CHUNK == 0
    assert bp <= V7X_SUBLANES and CHUNK % ts == 0

    n_c = bp + bs
    c_pad = (-n_c) % V7X_SUBLANES
    c_all = jnp.concatenate([c_prompt, c_sample, jnp.zeros((c_pad, d), F32)], axis=0)
    mod_p = _adaln(c_all, w_ada, b_ada)
    mod_s = jnp.repeat(mod_p[:, bp:bp + bs], ts, axis=1)

    xp = x_prompt.reshape(bp * tp, d)
    xs_ = x_sample.reshape(rows_s, d)
    w_in0, w_in1 = w_in_even[0], w_in_odd[0]

    hp0, ya_p, pool_p = _pool_mixer(xp, norm_g[0, 0], mod_p, 0, w_in0, w_pool[0], pool_scale[0],
                                    None, bp, 1, MATMUL_TILE, 0)
    hs0, ya_s, pool_s = _pool_mixer(xs_, norm_g[0, 0], mod_s, 0, w_in0, w_pool[0], pool_scale[0],
                                    state_pool[0], bs, bs, ts, PAST_LEN)
    yb_p, conv_p = _conv_mixer(hp0, w_in0, conv_w[0], None, bp, 1, MATMUL_TILE)
    yb_s, conv_s = _conv_mixer(hs0, w_in0, conv_w[0], state_conv[0], bs, bs, ts)
    out_p = _outproj(ya_p, yb_p, xp, mod_p, 0, norm_g[0, 1], w_out_even[0], tp)
    out_s = _outproj(ya_s, yb_s, xs_, mod_s, 0, norm_g[0, 1], w_out_even[0], rows_s)
    (x2p, h1p), (x2s, h1s) = _moe(out_p, out_s, mod_p, mod_s, 0, norm_g[1, 0], tp,
                                  w_router, b_router, w_gate, w_up, w_down, final=False)

    bs_t = gm_b_s[0].T
    (yc_p,) = _gmlp_mixer(h1p, w_in1, gm_norm_g[0], gm_norm_b[0], gm_w_s[0], bs_t, CHUNK, CHUNK, False)
    yc_s, gv_s = _gmlp_mixer(h1s, w_in1, gm_norm_g[0], gm_norm_b[0], gm_w_s[0], bs_t, rows_s, ts, True)
    bias = _attention_bias(rel_bias)
    nkv = N_KV * HEAD_DIM
    bias_t = jnp.transpose(bias.reshape(N_KV, PAIRS_PER_KV, 2, WINDOW, 2 * WINDOW), (0, 2, 4, 1, 3))
    bias_t = bias_t.reshape(N_KV, 2, 2 * WINDOW, PAIRS_PER_KV * WINDOW)
    before_start = (np.arange(2 * WINDOW) < WINDOW)[None, None, :, None]
    bias_t = jnp.stack([bias_t, jnp.where(before_start, NEG_INF, bias_t)], axis=0)
    sinks_t = jnp.transpose(attn_sinks[0].reshape(N_KV, PAIRS_PER_KV, 2), (0, 2, 1))
    sinks_t = jnp.repeat(sinks_t, WINDOW, axis=-1).reshape(N_KV, 2, 1, PAIRS_PER_KV * WINDOW)
    yd_p, k_p, v_p = _swa_stream_mixer(h1p, w_in1, bias_t, sinks_t, tp // WINDOW)
    yd_s, k_s, v_s = _swa_cached_mixer(h1s, w_in1, cache_swa_k[0].reshape(bs, WINDOW, nkv),
                                       cache_swa_v[0].reshape(bs, WINDOW, nkv),
                                       bias[:, :ts, :].reshape(N_HEADS * ts, 2 * WINDOW),
                                       jnp.repeat(attn_sinks[0], ts).reshape(-1, 1), ts)
    out_p = _outproj(yc_p, yd_p, x2p, mod_p, 1, norm_g[1, 1], w_out_odd[0], tp)
    out_s = _outproj(yc_s, yd_s, x2s, mod_s, 1, norm_g[1, 1], w_out_odd[0], rows_s)
    (yp,), (ys_out,) = _moe(out_p, out_s, mod_p, mod_s, 1, final_norm_g, tp,
                            w_router, b_router, w_gate, w_up, w_down, final=True)

    k_p4 = k_p.reshape(bp, tp, nkv)[:, -WINDOW:].reshape(bp, WINDOW, N_KV, HEAD_DIM)
    v_p4 = v_p.reshape(bp, tp, nkv)[:, -WINDOW:].reshape(bp, WINDOW, N_KV, HEAD_DIM)
    k_s4 = jnp.concatenate([cache_swa_k[0], k_s.reshape(bs, ts, N_KV, HEAD_DIM)], axis=1)[:, -WINDOW:]
    v_s4 = jnp.concatenate([cache_swa_v[0], v_s.reshape(bs, ts, N_KV, HEAD_DIM)], axis=1)[:, -WINDOW:]
    return (yp.reshape(bp, tp, d), ys_out.reshape(bs, ts, d),
            pool_p[None], pool_s[None], conv_p[None], conv_s[None],
            k_p4[None], k_s4[None], v_p4[None], v_s4[None],
            gv_s.reshape(bs, ts, GM_WIDTH)[None])
```

```python
import functools

import numpy as np
import jax
import jax.numpy as jnp
from jax import lax
from jax.experimental import pallas as pl
from jax.experimental.pallas import tpu as pltpu

F32 = jnp.float32
BF16 = jnp.bfloat16
I32 = jnp.int32
U32 = jnp.uint32

D_MODEL = 2048
POOL_WINDOWS = (2, 4, 8, 16)
POOL_WIDTH = 1024
POOL_GROUP = 256
POOL_STATE = 15
CONV_WIDTH = 1024
CONV_K = 3
GM_WIDTH = 1024
GM_GROUPS = 8
GM_GROUP = 128
CHUNK = 128
HEAD_DIM = 64
N_HEADS = 16
N_KV = 2
WINDOW = 128
N_BUCKETS = 32
N_EXPERTS = 16
N_EXPERT_GROUPS = 4
EXP_PER_GROUP = 4
TOP_K = 2
EPS = 1e-6
NEG_INF = -1e30
PAST_LEN = 16384

V7X_SUBLANES = 8
V7X_LANES = 128
VMEM_LIMIT = 56 * 1024 * 1024

ROW_TILE = 256
MATMUL_TILE = 512
ROUTER_TILE = 1024
POOL_HALO = 16
CONV_HALO = 8
MOE_TILE = 256
ADALN_COL_TILE = 1024
CONV_COL_TILE = 512
TAB_EXPERT, TAB_VALID, TAB_LAST_TILE, TAB_NUSED, TAB_NEXT, TAB_NEXT2 = 0, 1, 2, 3, 4, 5
WEIGHT_SLOTS = 3


def _cparams(n_axes):
    return pltpu.CompilerParams(dimension_semantics=("arbitrary",) * n_axes,
                                vmem_limit_bytes=VMEM_LIMIT)


def _rms(x, g):
    return x * lax.rsqrt(jnp.mean(x * x, axis=-1, keepdims=True) + EPS) * g


def _mod_spec(mod, layer, part):
    nrow = ROW_TILE if mod.shape[1] == ROW_TILE else V7X_SUBLANES
    return pl.BlockSpec((1, nrow, D_MODEL), lambda *_: (layer, 0, part))


def _mod_rows(m_ref, seq):
    if m_ref.shape[1] == V7X_SUBLANES:
        return m_ref[0, pl.ds(seq, 1), :]
    return m_ref[0]


def _adaln_kernel(c_ref, w_ref, b_ref, o_ref):
    c = c_ref[...]
    a = (c * jax.nn.sigmoid(c)).astype(BF16)
    o_ref[0] = jnp.dot(a, w_ref[0].astype(BF16), preferred_element_type=F32) + b_ref[0]


def _adaln(c_all, w_ada, b_ada):
    depth, d, n6 = w_ada.shape
    m = c_all.shape[0]
    tn = ADALN_COL_TILE
    return pl.pallas_call(
        _adaln_kernel,
        grid=(depth, n6 // tn),
        in_specs=[pl.BlockSpec((m, d), lambda l, j: (0, 0)),
                  pl.BlockSpec((1, d, tn), lambda l, j: (l, 0, j)),
                  pl.BlockSpec((1, 1, tn), lambda l, j: (l, 0, j))],
        out_specs=pl.BlockSpec((1, m, tn), lambda l, j: (l, 0, j)),
        out_shape=jax.ShapeDtypeStruct((depth, m, n6), F32),
        compiler_params=_cparams(2),
        name="adaln",
    )(c_all, w_ada, b_ada.reshape(depth, 1, n6))


def _pool_kernel(x_ref, g_ref, sc_ref, sh_ref, w_ref, wp_ref, ps_ref, st_ref, h_ref, ya_ref, ns_ref,
                 wbf, wpbf, carry, *, nb, tm, tiles_per_seq, start):
    i = pl.program_id(0)
    t = i % tiles_per_seq
    seq = i // tiles_per_seq
    c = POOL_WIDTH
    halo = POOL_HALO

    @pl.when(i == 0)
    def _():
        wbf[...] = w_ref[...].astype(BF16)
        wpbf[...] = wp_ref[...].astype(BF16)

    @pl.when(t == 0)
    def _():
        carry[...] = st_ref[...]

    h = (_rms(x_ref[...], g_ref[...]) * (1.0 + _mod_rows(sc_ref, seq)) + _mod_rows(sh_ref, seq)).astype(BF16)
    h_ref[...] = h
    p = jnp.dot(h, wbf[...], preferred_element_type=F32)
    p3 = p.reshape(nb, tm, c)
    ext3 = jnp.concatenate([carry[...], p3], axis=1)
    tail = ext3[:, tm:tm + halo, :]
    ns_ref[...] = tail[:, halo - POOL_STATE:, :]
    carry[...] = tail
    ext = ext3.reshape(nb * (halo + tm), c)
    pos = start + t * tm + lax.broadcasted_iota(I32, (1, tm, 1), 1)
    outs = []
    for gi, w in enumerate(POOL_WINDOWS):
        sl = slice(gi * POOL_GROUP, (gi + 1) * POOL_GROUP)
        acc = ext[:, sl]
        shift = 1
        while shift < w:
            acc = acc + pltpu.roll(acc, shift, 0)
            shift *= 2
        win = acc.reshape(nb, halo + tm, POOL_GROUP)[:, halo:, :]
        cnt = jnp.minimum(pos + 1, w).astype(F32)
        dgrp = win / cnt - p3[:, :, sl]
        outs.append(jnp.dot(dgrp.reshape(nb * tm, POOL_GROUP).astype(BF16), wpbf[gi],
                            preferred_element_type=F32))
    y = jnp.concatenate(outs, axis=-1) * ps_ref[...]
    ya_ref[...] = y.astype(BF16)


def _pool_mixer(x2, g, mod, layer, w_in, w_pool, pool_scale, state, nseq, nb, tm, start):
    rows, d = x2.shape
    tiles_per_seq = (rows // nseq) // tm
    seq_blocks = nseq // nb
    c = POOL_WIDTH
    if state is None:
        st = jnp.zeros((nseq, POOL_HALO, c), F32)
    else:
        st = jnp.pad(state, ((0, 0), (POOL_HALO - POOL_STATE, 0), (0, 0)))
    kern = functools.partial(_pool_kernel, nb=nb, tm=tm, tiles_per_seq=tiles_per_seq, start=start)
    h2, ya, ns = pl.pallas_call(
        kern,
        grid=(seq_blocks * tiles_per_seq,),
        in_specs=[pl.BlockSpec((nb * tm, d), lambda i: (i, 0)),
                  pl.BlockSpec((1, d), lambda i: (0, 0)),
                  _mod_spec(mod, layer, 1), _mod_spec(mod, layer, 0),
                  pl.BlockSpec((d, c), lambda i: (0, 0)),
                  pl.BlockSpec((len(POOL_WINDOWS), POOL_GROUP, POOL_GROUP), lambda i: (0, 0, 0)),
                  pl.BlockSpec((1, c), lambda i: (0, 0)),
                  pl.BlockSpec((nb, POOL_HALO, c), lambda i: (i // tiles_per_seq, 0, 0))],
        out_specs=[pl.BlockSpec((nb * tm, d), lambda i: (i, 0)),
                   pl.BlockSpec((nb * tm, c), lambda i: (i, 0)),
                   pl.BlockSpec((nb, POOL_STATE, c), lambda i: (i // tiles_per_seq, 0, 0))],
        out_shape=[jax.ShapeDtypeStruct((rows, d), BF16),
                   jax.ShapeDtypeStruct((rows, c), BF16),
                   jax.ShapeDtypeStruct((nseq, POOL_STATE, c), F32)],
        scratch_shapes=[pltpu.VMEM((d, c), BF16),
                        pltpu.VMEM((len(POOL_WINDOWS), POOL_GROUP, POOL_GROUP), BF16),
                        pltpu.VMEM((nb, POOL_HALO, c), F32)],
        compiler_params=_cparams(1),
        name="pool_mixer",
    )(x2, g.reshape(1, d), mod, mod, w_in, w_pool, pool_scale.reshape(1, c), st)
    return h2, ya, ns


def _conv_kernel(h_ref, wx_ref, wb_ref, wc_ref, cw_ref, st_ref, yb_ref, ns_ref,
                 wxbf, wbbf, wcbf, carry, *, nb, tm, tiles_per_seq):
    i = pl.program_id(1)
    t = i % tiles_per_seq
    tc = wxbf.shape[1]
    halo = CONV_HALO

    @pl.when(i == 0)
    def _():
        wxbf[...] = wx_ref[...].astype(BF16)
        wbbf[...] = wb_ref[...].astype(BF16)
        wcbf[...] = wc_ref[...].astype(BF16)

    @pl.when(t == 0)
    def _():
        carry[...] = st_ref[...]

    h = h_ref[...]
    xin = jnp.dot(h, wxbf[...], preferred_element_type=F32)
    gb = jnp.dot(h, wbbf[...], preferred_element_type=F32)
    gc = jnp.dot(h, wcbf[...], preferred_element_type=F32)
    z3 = (gc * xin).reshape(nb, tm, tc)
    ext3 = jnp.concatenate([carry[...], z3], axis=1)
    tail = ext3[:, tm:tm + halo, :]
    ns_ref[...] = tail[:, halo - (CONV_K - 1):, :]
    carry[...] = tail
    ext = ext3.reshape(nb * (halo + tm), tc)
    cw = cw_ref[...]
    conv = cw[0:1, :] * pltpu.roll(ext, 2, 0) + cw[1:2, :] * pltpu.roll(ext, 1, 0) + cw[2:3, :] * ext
    conv = conv.reshape(nb, halo + tm, tc)[:, halo:, :].reshape(nb * tm, tc)
    yb_ref[...] = (gb * conv).astype(BF16)


def _conv_mixer(h2, w_in, conv_w, state, nseq, nb, tm):
    rows, d = h2.shape
    tiles_per_seq = (rows // nseq) // tm
    seq_blocks = nseq // nb
    c = CONV_WIDTH
    tc = CONV_COL_TILE
    cb = c // tc
    base = POOL_WIDTH // tc
    if state is None:
        st = jnp.zeros((nseq, CONV_HALO, c), F32)
    else:
        st = jnp.pad(state, ((0, 0), (CONV_HALO - (CONV_K - 1), 0), (0, 0)))
    kern = functools.partial(_conv_kernel, nb=nb, tm=tm, tiles_per_seq=tiles_per_seq)
    yb, ns = pl.pallas_call(
        kern,
        grid=(cb, seq_blocks * tiles_per_seq),
        in_specs=[pl.BlockSpec((nb * tm, d), lambda j, i: (i, 0)),
                  pl.BlockSpec((d, tc), lambda j, i: (0, base + j)),
                  pl.BlockSpec((d, tc), lambda j, i: (0, base + cb + j)),
                  pl.BlockSpec((d, tc), lambda j, i: (0, base + 2 * cb + j)),
                  pl.BlockSpec((CONV_K, tc), lambda j, i: (0, j)),
                  pl.BlockSpec((nb, CONV_HALO, tc), lambda j, i: (i // tiles_per_seq, 0, j))],
        out_specs=[pl.BlockSpec((nb * tm, tc), lambda j, i: (i, j)),
                   pl.BlockSpec((nb, CONV_K - 1, tc), lambda j, i: (i // tiles_per_seq, 0, j))],
        out_shape=[jax.ShapeDtypeStruct((rows, c), BF16),
                   jax.ShapeDtypeStruct((nseq, CONV_K - 1, c), F32)],
        scratch_shapes=[pltpu.VMEM((d, tc), BF16)] * 3 + [pltpu.VMEM((nb, CONV_HALO, tc), F32)],
        compiler_params=_cparams(2),
        name="conv_mixer",
    )(h2, w_in, w_in, w_in, conv_w, st)
    return yb, ns


def _pack_bf16_pairs(v):
    c = v.shape[1] // 2
    return pltpu.bitcast(pltpu.pack_elementwise([v[:, :c], v[:, c:]], packed_dtype=BF16), U32)


def _store_token_tiles(ref, v):
    rows = v.shape[0]
    for j in range(V7X_SUBLANES):
        ref[pl.ds(j, rows, stride=V7X_SUBLANES), :] = v[:, j * V7X_LANES:(j + 1) * V7X_LANES]


def _load_token_tiles(ref):
    rows = ref.shape[0] // V7X_SUBLANES
    return jnp.concatenate([ref[pl.ds(j, rows, stride=V7X_SUBLANES), :] for j in range(V7X_SUBLANES)],
                           axis=-1)


def _unpack_pairs_f32(w):
    return tuple(pltpu.unpack_elementwise(w, index=k, packed_dtype=BF16, unpacked_dtype=F32) for k in range(2))


def _unpack_bf16_pairs(w):
    lo, hi = _unpack_pairs_f32(w)
    return lo.astype(BF16), hi.astype(BF16)


def _outproj_kernel(ya_ref, yb_ref, x_ref, g1_ref, sc_ref, sh_ref, ng_ref, wo_ref,
                    x1_ref, hp_ref, hpt_ref, wobf, *, tiles_per_seq):
    i = pl.program_id(0)
    seq = i // tiles_per_seq

    @pl.when(i == 0)
    def _():
        wobf[...] = wo_ref[...].astype(BF16)

    ycat = jnp.concatenate([ya_ref[...], yb_ref[...]], axis=-1)
    y = jnp.dot(ycat, wobf[...], preferred_element_type=F32)
    x1 = x_ref[...] + _mod_rows(g1_ref, seq) * y
    x1_ref[...] = x1
    h2 = _rms(x1, ng_ref[...]) * (1.0 + _mod_rows(sc_ref, seq)) + _mod_rows(sh_ref, seq)
    packed = _pack_bf16_pairs(h2)
    hp_ref[...] = packed
    _store_token_tiles(hpt_ref, packed)


def _outproj(ya, yb, x2, mod, layer, ng, w_out, seq_rows):
    rows_all, d = x2.shape
    half = ya.shape[1]
    rt = ROW_TILE
    row_spec = lambda w: pl.BlockSpec((rt, w), lambda i: (i, 0))
    return pl.pallas_call(
        functools.partial(_outproj_kernel, tiles_per_seq=seq_rows // rt),
        grid=(rows_all // rt,),
        in_specs=[row_spec(half), row_spec(half), row_spec(d),
                  _mod_spec(mod, layer, 2), _mod_spec(mod, layer, 4), _mod_spec(mod, layer, 3),
                  pl.BlockSpec((1, d), lambda i: (0, 0)),
                  pl.BlockSpec((d, d), lambda i: (0, 0), pipeline_mode=pl.Buffered(1))],
        out_specs=[row_spec(d), row_spec(d // 2),
                   pl.BlockSpec((rt * V7X_SUBLANES, V7X_LANES), lambda i: (i, 0))],
        out_shape=[jax.ShapeDtypeStruct((rows_all, d), F32),
                   jax.ShapeDtypeStruct((rows_all, d // 2), U32),
                   jax.ShapeDtypeStruct((rows_all * V7X_SUBLANES, V7X_LANES), U32)],
        scratch_shapes=[pltpu.VMEM((d, d), BF16)],
        compiler_params=_cparams(1),
        name="outproj",
    )(ya, yb, x2, mod, mod, mod, ng.reshape(1, d), w_out)


def _router_kernel(hpp_ref, hps_ref, wr_ref, br_ref, pos_ref, rw_ref, tab_ref,
                   cnt_acc, totals, starts, padded, before_ref, s_all, sel_all, *, nt_p, rows_s):
    ph = pl.program_id(0)
    t = pl.program_id(1)
    last = nt_p
    r = hpp_ref.shape[0]
    half = hpp_ref.shape[1]
    ne = N_EXPERTS
    sub = lax.broadcasted_iota(I32, (ne, V7X_LANES), 0)

    @pl.when(t == 0)
    def _():
        cnt_acc[...] = jnp.zeros_like(cnt_acc)

    @pl.when((ph == 0) & (t == 0))
    def _():
        starts[...] = jnp.zeros_like(starts)
        padded[...] = jnp.zeros_like(padded)

    @pl.when((ph == 1) & (t == 0))
    def _():
        pad = jnp.floor((totals[...] + (MOE_TILE - 1.0)) * (1.0 / MOE_TILE)) * MOE_TILE
        run = pad
        k = 1
        while k < ne:
            run = run + jnp.where(sub >= k, pltpu.roll(run, k, 0), 0.0)
            k *= 2
        padded[...] = pad
        starts[...] = run - pad

    is_s = t == last
    eid = lax.broadcasted_iota(I32, (ne, r), 0)
    n_valid = jnp.where(is_s, rows_s, r)
    tok = lax.broadcasted_iota(I32, (ne, r), 1)

    @pl.when(ph == 0)
    def _():
        w_s = jnp.concatenate([hps_ref[...], jnp.zeros((r - rows_s, half), U32)], axis=0)
        w = jnp.where(is_s, w_s, hpp_ref[...])
        lo, hi = _unpack_bf16_pairs(w)
        wr = wr_ref[...].astype(BF16)
        log_t = (lax.dot_general(wr[:, :half], lo, NT_DIMS, preferred_element_type=F32)
                 + lax.dot_general(wr[:, half:], hi, NT_DIMS, preferred_element_type=F32))

        s = jax.nn.sigmoid(log_t)
        sg = s + br_ref[...]
        within = eid % EXP_PER_GROUP
        grp = eid // EXP_PER_GROUP

        def group_rot(x, k):
            return jnp.where(within + k < EXP_PER_GROUP,
                             pltpu.roll(x, ne - k, 0), pltpu.roll(x, EXP_PER_GROUP - k, 0))

        rank = jnp.zeros((ne, r), I32)
        for k in range(1, EXP_PER_GROUP):
            mate = group_rot(sg, k)
            wrapped = within + k >= EXP_PER_GROUP
            ahead = (mate > sg) | (wrapped & (mate == sg))
            rank = rank + ahead.astype(I32)
        top2 = rank < TOP_K
        kept = jnp.where(top2, sg, 0.0)
        gscore = kept
        for k in range(1, EXP_PER_GROUP):
            gscore = gscore + group_rot(kept, k)
        win = None
        for k in range(1, N_EXPERT_GROUPS):
            other = pltpu.roll(gscore, EXP_PER_GROUP * k, 0)
            beats = (gscore > other) | ((grp < k) & (gscore == other))
            win = beats if win is None else (win & beats)
        picked_now = top2 & win & (tok < n_valid)
        s_all[t] = s
        sel_all[t] = picked_now.astype(F32)

    s = s_all[t]
    selb = sel_all[t]
    sel = selb > 0.5
    cnt_before = cnt_acc[...]
    cnt_new = cnt_before + jnp.sum(selb, axis=1, keepdims=True)
    cnt_acc[...] = cnt_new

    @pl.when((ph == 0) & (t == 0))
    def _():
        src = lax.broadcasted_iota(I32, (r, r), 0)
        dst = lax.broadcasted_iota(I32, (r, r), 1)
        before_ref[...] = (src < dst).astype(BF16)

    @pl.when((ph == 0) & (t == last))
    def _():
        totals[...] = cnt_new

    @pl.when(ph == 1)
    def _():
        picked = jnp.where(sel, s, 0.0)
        wsum = jnp.sum(picked, axis=0, keepdims=True)
        gate = picked / jnp.where(tok[0:1, :] < n_valid, wsum, 1.0)
        ranks = jnp.dot(selb.astype(BF16), before_ref[...], preferred_element_type=F32)
        slot = (starts[...][:, 0:1] + cnt_before[:, 0:1] + ranks).astype(I32)
        e_a = jnp.min(jnp.where(sel, eid, ne), axis=0, keepdims=True)
        e_b = jnp.max(jnp.where(sel, eid, -1), axis=0, keepdims=True)
        is_a = sel & (eid == e_a)
        is_b = sel & (eid == e_b)
        pos_a = jnp.sum(jnp.where(is_a, slot, 0), axis=0, keepdims=True)
        pos_b = jnp.sum(jnp.where(is_b, slot, 0), axis=0, keepdims=True)
        w_a = jnp.sum(jnp.where(is_a, gate, 0.0), axis=0, keepdims=True)
        w_b = jnp.sum(jnp.where(is_b, gate, 0.0), axis=0, keepdims=True)
        pos_ref[0] = jnp.concatenate([pos_a, pos_b], axis=0)
        wmat = jnp.concatenate([w_a, w_b, jnp.zeros((V7X_LANES - 2, r), F32)], axis=0)
        rw_ref[...] = wmat.T

    @pl.when((ph == 1) & (t == last))
    def _():
        ends = starts[...] + padded[...]
        lane = lax.broadcasted_iota(I32, (ne, V7X_LANES), 1)
        tile_start = (lane * MOE_TILE).astype(F32)
        te = jnp.sum((tile_start >= ends).astype(I32), axis=0, keepdims=True)
        valid = te < ne
        last_e = jnp.max(jnp.where(padded[...] > 0.0, sub, 0), axis=0, keepdims=True)
        te = jnp.where(valid, te, last_e)
        n_used = jnp.sum(valid.astype(I32), axis=1, keepdims=True) + jnp.zeros((1, V7X_LANES), I32)
        last_tile = jnp.where(padded[...] > 0.0, ends - MOE_TILE, -1.0).astype(I32)
        last_tile_row = jnp.sum(jnp.where(sub == lane, last_tile, 0), axis=0, keepdims=True)
        later = jnp.min(jnp.where((sub > te) & (padded[...] > 0.0), sub, ne), axis=0, keepdims=True)
        next_e = jnp.where(later < ne, later, -1)
        later2 = jnp.min(jnp.where((sub > later) & (padded[...] > 0.0), sub, ne), axis=0, keepdims=True)
        next2_e = jnp.where(later2 < ne, later2, -1)
        zero = jnp.zeros((1, V7X_LANES), I32)
        tab_ref[...] = jnp.concatenate([te, valid.astype(I32), last_tile_row, n_used, next_e, next2_e,
                                        zero, zero], axis=0)


def _router(hp_p, hp_s, w_router, b_router):
    n_p, half = hp_p.shape
    rows_s = hp_s.shape[0]
    r = ROUTER_TILE
    nt_p = n_p // r
    nt = nt_p + 1
    kern = functools.partial(_router_kernel, nt_p=nt_p, rows_s=rows_s)
    pos, rw, tab = pl.pallas_call(
        kern,
        grid=(2, nt),
        in_specs=[pl.BlockSpec((r, half), lambda p, t: (jnp.minimum(t, nt_p - 1) * (1 - p), 0)),
                  pl.BlockSpec((rows_s, half), lambda p, t: (0, 0)),
                  pl.BlockSpec((N_EXPERTS, 2 * half), lambda p, t: (0, 0)),
                  pl.BlockSpec((N_EXPERTS, 1), lambda p, t: (0, 0))],
        out_specs=[pl.BlockSpec((1, TOP_K, r), lambda p, t: (p * t, 0, 0)),
                   pl.BlockSpec((r, V7X_LANES), lambda p, t: (p * t, 0)),
                   pl.BlockSpec((V7X_SUBLANES, V7X_LANES), lambda p, t: (0, 0))],
        out_shape=[jax.ShapeDtypeStruct((nt, TOP_K, r), I32),
                   jax.ShapeDtypeStruct((nt * r, V7X_LANES), F32),
                   jax.ShapeDtypeStruct((V7X_SUBLANES, V7X_LANES), I32)],
        scratch_shapes=[pltpu.VMEM((N_EXPERTS, V7X_LANES), F32)] * 4 + [pltpu.VMEM((r, r), BF16)]
        + [pltpu.VMEM((nt, N_EXPERTS, r), F32)] * 2,
        compiler_params=_cparams(2),
        name="router",
    )(hp_p, hp_s, w_router.T, b_router.reshape(N_EXPERTS, 1))
    return pos.reshape(-1), rw, tab.reshape(-1)


def _pos_index(tok0):
    return (tok0 // ROUTER_TILE) * (TOP_K * ROUTER_TILE) + tok0 % ROUTER_TILE


def _tokens(ref, first, n=1):
    start = pl.multiple_of(first * V7X_SUBLANES, V7X_SUBLANES)
    return ref.at[pl.ds(start, n * V7X_SUBLANES), :]


def _dispatch_kernel(pos_ref, tab_ref, hpp_ref, hps_ref, xs_ref, zbuf, sem, *, n_p_steps):
    i = pl.program_id(0)
    rows = hpp_ref.shape[0] // V7X_SUBLANES

    @pl.when(i == 0)
    def _():
        zbuf[...] = jnp.zeros_like(zbuf)

        def fill(e):
            first = pl.multiple_of(tab_ref[TAB_LAST_TILE * V7X_LANES + e], MOE_TILE)
            return pltpu.make_async_copy(zbuf, _tokens(xs_ref, first, MOE_TILE), sem)

        for e in range(N_EXPERTS):
            @pl.when(tab_ref[TAB_LAST_TILE * V7X_LANES + e] >= 0)
            def _():
                fill(e).start()
        for e in range(N_EXPERTS):
            @pl.when(tab_ref[TAB_LAST_TILE * V7X_LANES + e] >= 0)
            def _():
                fill(e).wait()

        def tail(j):
            first = pl.multiple_of(j * MOE_TILE, MOE_TILE)
            return pltpu.make_async_copy(zbuf, _tokens(xs_ref, first, MOE_TILE), sem)

        def tail_start(j, carry):
            tail(j).start()
            return carry

        def tail_wait(j, carry):
            tail(j).wait()
            return carry

        n_used = tab_ref[TAB_NUSED * V7X_LANES]
        n_tiles = xs_ref.shape[0] // (MOE_TILE * V7X_SUBLANES)
        lax.fori_loop(n_used, n_tiles, tail_start, 0)
        lax.fori_loop(n_used, n_tiles, tail_wait, 0)

    def scatter(src_ref, tok0):
        n = src_ref.shape[0] // V7X_SUBLANES
        base = _pos_index(tok0)

        def row_copy(r, dst):
            return pltpu.make_async_copy(_tokens(src_ref, r), _tokens(xs_ref, dst), sem)

        def issue(r, carry):
            row_copy(r, pos_ref[base + r]).start()
            row_copy(r, pos_ref[base + ROUTER_TILE + r]).start(priority=1)
            return carry

        lax.fori_loop(0, n, issue, 0, unroll=8)
        block = pltpu.make_async_copy(src_ref, _tokens(xs_ref, 0, n), sem)
        for _ in range(TOP_K):
            block.wait()

    @pl.when(i < n_p_steps)
    def _():
        scatter(hpp_ref, i * rows)

    @pl.when(i == n_p_steps)
    def _():
        scatter(hps_ref, n_p_steps * rows)


def _dispatch(pos, tab, hpt_p, hpt_s, n_rows_sorted):
    sub = V7X_SUBLANES
    n_p_steps = hpt_p.shape[0] // (MATMUL_TILE * sub)
    kern = functools.partial(_dispatch_kernel, n_p_steps=n_p_steps)
    blk = (MATMUL_TILE * sub, V7X_LANES)
    return pl.pallas_call(
        kern,
        grid_spec=pltpu.PrefetchScalarGridSpec(
            num_scalar_prefetch=2,
            grid=(n_p_steps + 1,),
            in_specs=[pl.BlockSpec(blk, lambda i, p, t: (jnp.minimum(i, n_p_steps - 1), 0)),
                      pl.BlockSpec(hpt_s.shape, lambda i, p, t: (0, 0))],
            out_specs=pl.BlockSpec(memory_space=pl.ANY),
            scratch_shapes=[pltpu.VMEM((MOE_TILE * sub, V7X_LANES), U32), pltpu.SemaphoreType.DMA(())]),
        out_shape=jax.ShapeDtypeStruct((n_rows_sorted * sub, V7X_LANES), U32),
        compiler_params=_cparams(1),
        name="moe_dispatch",
    )(pos, tab, hpt_p, hpt_s)


def _experts_kernel(tab_ref, xs_ref, wg_hbm, wu_hbm, wd_hbm, ys_ref,
                    wg32, wu32, wd32, wgbf, wubf, wdbf, slot_ref, sems, *, layer):
    i = pl.program_id(0)
    expert = tab_ref[TAB_EXPERT * V7X_LANES + i]
    prev = tab_ref[TAB_EXPERT * V7X_LANES + jnp.maximum(i - 1, 0)]
    upcoming = tab_ref[TAB_NEXT * V7X_LANES + i]
    upcoming2 = tab_ref[TAB_NEXT2 * V7X_LANES + i]
    changed = (i == 0) | (expert != prev)

    def weight_copies(e, slot):
        return (pltpu.make_async_copy(wg_hbm.at[layer, e], wg32.at[slot], sems.at[0, slot]),
                pltpu.make_async_copy(wu_hbm.at[layer, e], wu32.at[slot], sems.at[1, slot]),
                pltpu.make_async_copy(wd_hbm.at[layer, e], wd32.at[slot], sems.at[2, slot]))

    def fetch(e, slot):
        for n, cp in enumerate(weight_copies(e, slot)):
            cp.start(priority=n % 2)

    @pl.when(i == 0)
    def _():
        slot_ref[0] = 0
        fetch(expert, 0)

        @pl.when(upcoming >= 0)
        def _():
            fetch(upcoming, 1)

    @pl.when(changed & (i > 0))
    def _():
        slot_ref[0] = jnp.where(slot_ref[0] == WEIGHT_SLOTS - 1, 0, slot_ref[0] + 1)

    def mlp(wg, wu, wd):
        x = jnp.concatenate(_unpack_bf16_pairs(_load_token_tiles(xs_ref)), axis=-1)
        a = jnp.dot(x, wg, preferred_element_type=F32)
        b = jnp.dot(x, wu, preferred_element_type=F32)
        hid = (a * jax.nn.sigmoid(a)) * b
        y = jnp.dot(hid.astype(BF16), wd, preferred_element_type=F32)
        _store_token_tiles(ys_ref, _pack_bf16_pairs(y))

    for slot in range(WEIGHT_SLOTS):
        @pl.when(changed & (slot_ref[0] == slot))
        def _():
            for cp in weight_copies(expert, slot):
                cp.wait()

            @pl.when(upcoming2 >= 0)
            def _():
                fetch(upcoming2, (slot + 2) % WEIGHT_SLOTS)

            wg = wg32[slot].astype(BF16)
            wu = wu32[slot].astype(BF16)
            wd = wd32[slot].astype(BF16)
            wgbf[...] = wg
            wubf[...] = wu
            wdbf[...] = wd
            mlp(wg, wu, wd)

    valid = tab_ref[TAB_VALID * V7X_LANES + i] > 0

    @pl.when(valid & jnp.logical_not(changed))
    def _():
        mlp(wgbf[...], wubf[...], wdbf[...])

    @pl.when(jnp.logical_not(valid))
    def _():
        ys_ref[...] = jnp.zeros_like(ys_ref)


def _experts(tab, xs, w_gate, w_up, w_down, layer):
    sub = V7X_SUBLANES
    _, _, d, f = w_gate.shape
    nt = xs.shape[0] // (MOE_TILE * sub)
    assert nt <= V7X_LANES and d == 2 * sub * V7X_LANES
    blk = (MOE_TILE * sub, V7X_LANES)

    def tile(i, tab_ref):
        return jnp.minimum(i, tab_ref[TAB_NUSED * V7X_LANES] - 1)

    hbm = pl.BlockSpec(memory_space=pl.ANY)
    return pl.pallas_call(
        functools.partial(_experts_kernel, layer=layer),
        grid_spec=pltpu.PrefetchScalarGridSpec(
            num_scalar_prefetch=1,
            grid=(nt,),
            in_specs=[pl.BlockSpec(blk, lambda i, t: (tile(i, t), 0)), hbm, hbm, hbm],
            out_specs=pl.BlockSpec(blk, lambda i, t: (i, 0)),
            scratch_shapes=[pltpu.VMEM((WEIGHT_SLOTS, d, f), F32), pltpu.VMEM((WEIGHT_SLOTS, d, f), F32),
                            pltpu.VMEM((WEIGHT_SLOTS, f, d), F32),
                            pltpu.VMEM((d, f), BF16), pltpu.VMEM((d, f), BF16), pltpu.VMEM((f, d), BF16),
                            pltpu.SMEM((1,), I32), pltpu.SemaphoreType.DMA((3, WEIGHT_SLOTS))]),
        out_shape=jax.ShapeDtypeStruct(xs.shape, U32),
        compiler_params=_cparams(1),
        name="moe_experts",
    )(tab, xs, w_gate, w_up, w_down)


def _combine_kernel(pos_ref, ys_ref, rw_ref, ng_ref, *rest, tiles_per_seq, n_p_steps, final):
    n_in = 4
    n_out = 1 if final else 2
    trunk_in = (rest[:n_in], rest[n_in:2 * n_in])
    outs = rest[2 * n_in:2 * n_in + 2 * n_out]
    trunk_out = (outs[:n_out], outs[n_out:])
    buf0, buf1, sems = rest[2 * n_in + 2 * n_out:]
    bufs = (buf0, buf1)
    i = pl.program_id(0)
    n_steps = n_p_steps + 1
    rows = rw_ref.shape[0]

    def gather(step, slot):
        base = _pos_index(step * rows)

        def issue(r, carry):
            for k in range(TOP_K):
                src = pos_ref[base + k * ROUTER_TILE + r]
                pltpu.make_async_copy(_tokens(ys_ref, src), _tokens(bufs[slot].at[k], r),
                                      sems.at[slot]).start(priority=k)
            return carry

        lax.fori_loop(0, rows, issue, 0, unroll=8)

    def drain(slot):
        for k in range(TOP_K):
            pltpu.make_async_copy(_tokens(ys_ref, 0, rows), bufs[slot].at[k], sems.at[slot]).wait()

    def finish(slot, trunk, seq):
        x1_ref, g2_ref, sc_ref, sh_ref = trunk_in[trunk]
        rw = rw_ref[...]
        lo_a, hi_a = _unpack_pairs_f32(_load_token_tiles(bufs[slot].at[0]))
        lo_b, hi_b = _unpack_pairs_f32(_load_token_tiles(bufs[slot].at[1]))
        w_a = rw[:, 0:1]
        w_b = rw[:, 1:2]
        moe = jnp.concatenate([w_a * lo_a + w_b * lo_b, w_a * hi_a + w_b * hi_b], axis=-1)
        x2 = x1_ref[...] + _mod_rows(g2_ref, seq) * moe
        if final:
            trunk_out[trunk][0][...] = _rms(x2, ng_ref[...])
        else:
            trunk_out[trunk][0][...] = x2
            trunk_out[trunk][1][...] = (_rms(x2, ng_ref[...]) * (1.0 + _mod_rows(sc_ref, seq))
                                        + _mod_rows(sh_ref, seq)).astype(BF16)

    @pl.when(i == 0)
    def _():
        gather(0, 0)

    for slot in range(2):
        @pl.when(i % 2 == slot)
        def _():
            @pl.when(i + 1 < n_steps)
            def _():
                gather(i + 1, 1 - slot)

            drain(slot)

            @pl.when(i < n_p_steps)
            def _():
                finish(slot, 0, i // tiles_per_seq)

            @pl.when(i == n_p_steps)
            def _():
                finish(slot, 1, 0)


def _combine(pos, ys, x1_p, x1_s, rw, mod_p, mod_s, layer, ng, seq_rows_p, final):
    n_p, d = x1_p.shape
    rt = ROW_TILE
    assert x1_s.shape[0] == rt and n_p % rt == 0 and ROUTER_TILE % rt == 0
    n_p_steps = n_p // rt
    kern = functools.partial(_combine_kernel, tiles_per_seq=seq_rows_p // rt, n_p_steps=n_p_steps, final=final)
    p_spec = lambda w: pl.BlockSpec((rt, w), lambda i, p: (jnp.minimum(i, n_p_steps - 1), 0))
    s_spec = lambda w: pl.BlockSpec((rt, w), lambda i, p: (0, 0))
    nxt = min(layer + 1, mod_p.shape[0] - 1)

    def trunk_specs(spec, mod):
        return [spec(d), _mod_spec(mod, layer, 5), _mod_spec(mod, nxt, 1), _mod_spec(mod, nxt, 0)]

    out_dtypes = [F32] if final else [F32, BF16]
    out_shape = ([jax.ShapeDtypeStruct((n_p, d), t) for t in out_dtypes]
                 + [jax.ShapeDtypeStruct((rt, d), t) for t in out_dtypes])
    out_specs = [p_spec(d)] * len(out_dtypes) + [s_spec(d)] * len(out_dtypes)
    res = pl.pallas_call(
        kern,
        grid_spec=pltpu.PrefetchScalarGridSpec(
            num_scalar_prefetch=1,
            grid=(n_p_steps + 1,),
            in_specs=[pl.BlockSpec(memory_space=pl.ANY),
                      pl.BlockSpec((rt, V7X_LANES), lambda i, p: (i, 0)),
                      pl.BlockSpec((1, d), lambda i, p: (0, 0))]
            + trunk_specs(p_spec, mod_p) + trunk_specs(s_spec, mod_s),
            out_specs=out_specs,
            scratch_shapes=[pltpu.VMEM((TOP_K, rt * V7X_SUBLANES, V7X_LANES), U32)] * 2
            + [pltpu.SemaphoreType.DMA((2,))]),
        out_shape=out_shape,
        compiler_params=_cparams(1),
        name="moe_combine_final" if final else "moe_combine",
    )(pos, ys, rw, ng.reshape(1, d), x1_p, mod_p, mod_p, mod_p, x1_s, mod_s, mod_s, mod_s)
    return res[:len(out_dtypes)], res[len(out_dtypes):]


def _moe(out_p, out_s, mod_p, mod_s, layer, ng, seq_rows_p, w_router, b_router, w_gate, w_up, w_down, final):
    x1_p, hp_p, hpt_p = out_p
    x1_s, hp_s, hpt_s = out_s
    n_p = x1_p.shape[0]
    n_tok = n_p + x1_s.shape[0]
    max_rows = TOP_K * n_tok + N_EXPERTS * (MOE_TILE - 1)
    n_rows_sorted = -(-max_rows // MOE_TILE) * MOE_TILE
    pos, rw, tab = _router(hp_p, hp_s, w_router, b_router)
    xs = _dispatch(pos, tab, hpt_p, hpt_s, n_rows_sorted)
    ys = _experts(tab, xs, w_gate, w_up, w_down, layer)
    return _combine(pos, ys, x1_p, x1_s, rw, mod_p, mod_s, layer, ng, seq_rows_p, final)


def _gmlp_kernel(h_ref, w_ref, lg_ref, lb_ref, ws_ref, bs_ref, yc_ref, *rest, ell, blk, emit_v):
    if emit_v:
        gv_ref, wbf, wsbf = rest
    else:
        wbf, wsbf = rest
    i = pl.program_id(0)
    rows = h_ref.shape[0]
    c = GM_WIDTH

    @pl.when(i == 0)
    def _():
        wbf[...] = w_ref[...].astype(BF16)
        r = lax.broadcasted_iota(I32, (ell, ell), 0)
        s = lax.broadcasted_iota(I32, (ell, ell), 1)
        keep = (r >= s) & ((r // blk) == (s // blk))
        rsel = (lax.broadcasted_iota(I32, (ell, CHUNK), 0) % blk
                == lax.broadcasted_iota(I32, (ell, CHUNK), 1)).astype(BF16)
        csel = (lax.broadcasted_iota(I32, (CHUNK, ell), 1) % blk
                == lax.broadcasted_iota(I32, (CHUNK, ell), 0)).astype(BF16)
        for g in range(GM_GROUPS):
            wchunk = ws_ref[g].astype(BF16)
            if blk == ell:
                full = wchunk
            else:
                rowsp = jnp.dot(rsel, wchunk, preferred_element_type=F32).astype(BF16)
                full = jnp.dot(rowsp, csel, preferred_element_type=F32).astype(BF16)
            wsbf[g] = jnp.where(keep, full, jnp.zeros_like(full))

    uv = jnp.dot(h_ref[...], wbf[...], preferred_element_type=F32)
    u = uv[:, :c]
    v = uv[:, c:]
    vc = v - jnp.mean(v, axis=-1, keepdims=True)
    vn = vc * lax.rsqrt(jnp.mean(vc * vc, axis=-1, keepdims=True) + EPS) * lg_ref[...] + lb_ref[...]
    if emit_v:
        gv_ref[...] = vn
    vb = vn.astype(BF16)
    bs = bs_ref[...]
    for ch in range(rows // ell):
        rs = slice(ch * ell, (ch + 1) * ell)
        outs = []
        for g in range(GM_GROUPS):
            cs = slice(g * GM_GROUP, (g + 1) * GM_GROUP)
            mixed = jnp.dot(wsbf[g], vb[rs, cs], preferred_element_type=F32)
            mixed = (mixed.reshape(ell // blk, blk, GM_GROUP) + bs[:blk, g:g + 1][None]).reshape(ell, GM_GROUP)
            outs.append(u[rs, cs] * mixed)
        yc_ref[rs, :] = jnp.concatenate(outs, axis=-1).astype(BF16)


def _gmlp_mixer(h2, w_in, ln_g, ln_b, ws, bs_t, ell, blk, emit_v):
    rows, d = h2.shape
    c = GM_WIDTH
    kern = functools.partial(_gmlp_kernel, ell=ell, blk=blk, emit_v=emit_v)
    rt = min(rows, MATMUL_TILE)
    out_specs = [pl.BlockSpec((rt, c), lambda i: (i, 0))]
    out_shape = [jax.ShapeDtypeStruct((rows, c), BF16)]
    if emit_v:
        out_specs.append(pl.BlockSpec((rt, c), lambda i: (i, 0)))
        out_shape.append(jax.ShapeDtypeStruct((rows, c), F32))
    return pl.pallas_call(
        kern,
        grid=(rows // rt,),
        in_specs=[pl.BlockSpec((rt, d), lambda i: (i, 0)),
                  pl.BlockSpec((d, 2 * c), lambda i: (0, 0), pipeline_mode=pl.Buffered(1)),
                  pl.BlockSpec((1, c), lambda i: (0, 0)),
                  pl.BlockSpec((1, c), lambda i: (0, 0)),
                  pl.BlockSpec((GM_GROUPS, CHUNK, CHUNK), lambda i: (0, 0, 0)),
                  pl.BlockSpec((CHUNK, GM_GROUPS), lambda i: (0, 0))],
        out_specs=out_specs,
        out_shape=out_shape,
        scratch_shapes=[pltpu.VMEM((d, 2 * c), BF16), pltpu.VMEM((GM_GROUPS, ell, ell), BF16)],
        compiler_params=_cparams(1),
        name="gmlp_mixer",
    )(h2, w_in, ln_g.reshape(1, c), ln_b.reshape(1, c), ws, bs_t)


PAIR_W = 2 * HEAD_DIM
PAIRS_PER_KV = N_HEADS // N_KV // 2
NT_DIMS = (((1,), (1,)), ((), ()))
SCORE_SCALE = HEAD_DIM ** -0.5
assert float(np.log2(SCORE_SCALE)).is_integer()


def _swa_project(i, h_ref, wq_ref, wkv_ref, wbf):
    nq = N_HEADS * HEAD_DIM

    @pl.when(i == 0)
    def _():
        wbf[:, :nq] = wq_ref[...].astype(BF16)
        wbf[:, nq:] = wkv_ref[...].astype(BF16)

    return jnp.dot(h_ref[...], wbf[...], preferred_element_type=F32)


def _pair_block_diag(a, a_swapped, hk, axis):
    dim_axis = 1 - axis
    low = lax.broadcasted_iota(I32, a.shape, dim_axis) < HEAD_DIM
    lo, hi = (a, a_swapped) if hk == 0 else (a_swapped, a)
    return jnp.concatenate([jnp.where(low, lo, 0.0), jnp.where(low, 0.0, hi)], axis=axis).astype(BF16)


def _stack_pairs(qkv, rs, hk, scale=None):
    p0 = hk * PAIRS_PER_KV
    q = jnp.concatenate([qkv[rs, (p0 + pp) * PAIR_W:(p0 + pp + 1) * PAIR_W]
                         for pp in range(PAIRS_PER_KV)], axis=0)
    return (q if scale is None else q * scale).astype(BF16)


def _swa_cached_kernel(h_ref, wq_ref, wkv_ref, kp_ref, vp_ref, bias_ref, sink_ref, yd_ref, k_ref, v_ref,
                       wbf, *, tq):
    i = pl.program_id(0)
    rows = h_ref.shape[0]
    nq = N_HEADS * HEAD_DIM
    nkv = N_KV * HEAD_DIM
    n_blocks = rows // tq
    qkv = _swa_project(i, h_ref, wq_ref, wkv_ref, wbf)
    k_new = qkv[:, nq:nq + nkv]
    v_new = qkv[:, nq + nkv:]
    k_ref[...] = k_new
    v_ref[...] = v_new
    pad = jnp.zeros((WINDOW - tq, nkv), F32)

    scores, vbds = [], []
    for blk in range(n_blocks):
        rs = slice(blk * tq, (blk + 1) * tq)
        kcat = jnp.concatenate([kp_ref[blk], k_new[rs], pad], axis=0)
        vcat = jnp.concatenate([vp_ref[blk], v_new[rs], pad], axis=0)
        kswap = pltpu.roll(kcat, HEAD_DIM, 1)
        vswap = pltpu.roll(vcat, HEAD_DIM, 1)
        per_head = []
        for hk in range(N_KV):
            kbd = _pair_block_diag(kcat, kswap, hk, 0)
            vbds.append(_pair_block_diag(vcat, vswap, hk, 0))
            s4 = lax.dot_general(_stack_pairs(qkv, rs, hk), kbd, NT_DIMS,
                                 preferred_element_type=F32) * (HEAD_DIM ** -0.5)
            for pp in range(PAIRS_PER_KV):
                for sub in range(2):
                    per_head.append(s4[pp * tq:(pp + 1) * tq, sub * 2 * WINDOW:(sub + 1) * 2 * WINDOW])
        scores.append(jnp.concatenate(per_head, axis=0))

    s_all = jnp.stack(scores, axis=0) + bias_ref[...][None]
    sink = sink_ref[...][None]
    m = jnp.maximum(jnp.max(s_all, axis=-1, keepdims=True), sink)
    pr = jnp.exp(s_all - m)
    pr = pr / (jnp.sum(pr, axis=-1, keepdims=True) + jnp.exp(sink - m))

    for blk in range(n_blocks):
        outs = []
        for hk in range(N_KV):
            p4 = []
            for pp in range(PAIRS_PER_KV):
                h0 = 2 * (hk * PAIRS_PER_KV + pp)
                p4.append(jnp.concatenate([pr[blk, h0 * tq:(h0 + 1) * tq, :],
                                           pr[blk, (h0 + 1) * tq:(h0 + 2) * tq, :]], axis=-1))
            o4 = jnp.dot(jnp.concatenate(p4, axis=0).astype(BF16), vbds[blk * N_KV + hk],
                         preferred_element_type=F32)
            outs.extend(o4[pp * tq:(pp + 1) * tq, :] for pp in range(PAIRS_PER_KV))
        yd_ref[blk * tq:(blk + 1) * tq, :] = jnp.concatenate(outs, axis=-1).astype(BF16)


def _swa_stream_kernel(h_ref, wq_ref, wkv_ref, bias_ref, sink_ref, yd_ref, k_ref, v_ref,
                       wbf, kprev, vprev_t, *, blocks_per_seq):
    i = pl.program_id(0)
    rows = h_ref.shape[0]
    nq = N_HEADS * HEAD_DIM
    nkv = N_KV * HEAD_DIM
    tq = WINDOW
    n_blocks = rows // tq

    @pl.when(i == 0)
    def _():
        kprev[...] = jnp.zeros_like(kprev)
        vprev_t[...] = jnp.zeros_like(vprev_t)

    qkv = _swa_project(i, h_ref, wq_ref, wkv_ref, wbf)
    k_new = qkv[:, nq:nq + nkv]
    v_new = qkv[:, nq + nkv:]
    k_ref[...] = k_new
    v_ref[...] = v_new
    v_new_t = v_new.T
    lanes = PAIRS_PER_KV * tq

    for blk in range(n_blocks):
        rs = slice(blk * tq, (blk + 1) * tq)
        first = ((i * n_blocks + blk) % blocks_per_seq == 0).astype(I32)
        k_cur = k_new[rs]
        v_cur_t = v_new_t[:, rs]
        kcat = jnp.concatenate([kprev[...], k_cur], axis=0)
        vcat_t = jnp.concatenate([vprev_t[...], v_cur_t], axis=1)
        kprev[...] = k_cur
        vprev_t[...] = v_cur_t
        kswap = pltpu.roll(kcat, HEAD_DIM, 1)
        vswap_t = pltpu.roll(vcat_t, HEAD_DIM, 0)
        outs = []
        for hk in range(N_KV):
            kbd = _pair_block_diag(kcat, kswap, hk, 0)
            vbd_t = _pair_block_diag(vcat_t, vswap_t, hk, 1)
            st = lax.dot_general(kbd, _stack_pairs(qkv, rs, hk, SCORE_SCALE), NT_DIMS,
                                 preferred_element_type=F32)
            s3 = st.reshape(2, 2 * WINDOW, lanes) + bias_ref[first, hk]
            sink = sink_ref[hk]
            m = jnp.maximum(jnp.max(s3, axis=1, keepdims=True), sink)
            pr = jnp.exp(s3 - m)
            inv = 1.0 / (jnp.sum(pr, axis=1, keepdims=True) + jnp.exp(sink - m))
            o_t = jnp.dot(vbd_t, pr.reshape(4 * WINDOW, lanes).astype(BF16),
                          preferred_element_type=F32)
            norm = jnp.concatenate([jnp.broadcast_to(inv[sub], (HEAD_DIM, lanes)) for sub in range(2)], axis=0)
            o4 = (o_t * norm).T
            outs.extend(o4[pp * tq:(pp + 1) * tq, :] for pp in range(PAIRS_PER_KV))
        yd_ref[rs, :] = jnp.concatenate(outs, axis=-1).astype(BF16)


def _swa_weight_specs(w_in, d):
    nq = N_HEADS * HEAD_DIM
    nkv = N_KV * HEAD_DIM
    nw = nq + 2 * nkv
    q_blk = (w_in.shape[1] - nw) // nq
    kv_blk = (w_in.shape[1] - 2 * nkv) // (2 * nkv)
    assert q_blk * nq + nw == w_in.shape[1] and kv_blk * 2 * nkv + 2 * nkv == w_in.shape[1]
    return [pl.BlockSpec((d, nq), lambda i: (0, q_blk)), pl.BlockSpec((d, 2 * nkv), lambda i: (0, kv_blk))]


def _swa_outputs(rows, rt):
    nq = N_HEADS * HEAD_DIM
    nkv = N_KV * HEAD_DIM
    specs = [pl.BlockSpec((rt, nq), lambda i: (i, 0)),
             pl.BlockSpec((rt, nkv), lambda i: (i, 0)),
             pl.BlockSpec((rt, nkv), lambda i: (i, 0))]
    shapes = [jax.ShapeDtypeStruct((rows, nq), BF16),
              jax.ShapeDtypeStruct((rows, nkv), F32),
              jax.ShapeDtypeStruct((rows, nkv), F32)]
    return specs, shapes


def _swa_cached_mixer(h2, w_in, k_cache, v_cache, bias, sinks, tq):
    rows, d = h2.shape
    nkv = N_KV * HEAD_DIM
    nw = N_HEADS * HEAD_DIM + 2 * nkv
    n_blocks = ROW_TILE // tq
    cache_spec = pl.BlockSpec((n_blocks, WINDOW, nkv), lambda i: (i, 0, 0))
    out_specs, out_shape = _swa_outputs(rows, ROW_TILE)
    return pl.pallas_call(
        functools.partial(_swa_cached_kernel, tq=tq),
        grid=(rows // ROW_TILE,),
        in_specs=[pl.BlockSpec((ROW_TILE, d), lambda i: (i, 0))] + _swa_weight_specs(w_in, d)
        + [cache_spec, cache_spec,
           pl.BlockSpec((N_HEADS * tq, 2 * WINDOW), lambda i: (0, 0)),
           pl.BlockSpec((N_HEADS * tq, 1), lambda i: (0, 0))],
        out_specs=out_specs,
        out_shape=out_shape,
        scratch_shapes=[pltpu.VMEM((d, nw), BF16)],
        compiler_params=_cparams(1),
        name="swa_cached",
    )(h2, w_in, w_in, k_cache, v_cache, bias, sinks)


def _swa_stream_mixer(h2, w_in, bias_t, sinks_t, blocks_per_seq):
    rows, d = h2.shape
    nkv = N_KV * HEAD_DIM
    nw = N_HEADS * HEAD_DIM + 2 * nkv
    lanes = PAIRS_PER_KV * WINDOW
    rt = min(rows, MATMUL_TILE)
    out_specs, out_shape = _swa_outputs(rows, rt)
    return pl.pallas_call(
        functools.partial(_swa_stream_kernel, blocks_per_seq=blocks_per_seq),
        grid=(rows // rt,),
        in_specs=[pl.BlockSpec((rt, d), lambda i: (i, 0))] + _swa_weight_specs(w_in, d)
        + [pl.BlockSpec((2, N_KV, 2, 2 * WINDOW, lanes), lambda i: (0, 0, 0, 0, 0)),
           pl.BlockSpec((N_KV, 2, 1, lanes), lambda i: (0, 0, 0, 0))],
        out_specs=out_specs,
        out_shape=out_shape,
        scratch_shapes=[pltpu.VMEM((d, nw), BF16), pltpu.VMEM((WINDOW, nkv), F32),
                        pltpu.VMEM((nkv, WINDOW), F32)],
        compiler_params=_cparams(1),
        name="swa_stream",
    )(h2, w_in, w_in, bias_t, sinks_t)


def _t5_bucket(dist):
    max_exact = N_BUCKETS // 2
    dd = np.maximum(dist, 1)
    large = max_exact + (np.log(dd / max_exact) / np.log(WINDOW / max_exact)
                         * (N_BUCKETS - max_exact)).astype(np.int64)
    large = np.minimum(large, N_BUCKETS - 1)
    return np.where(dist < max_exact, dist, large).astype(np.int32)


def _attention_bias(rel_bias):
    by_dist = jnp.take(rel_bias.astype(F32), _t5_bucket(np.arange(WINDOW)), axis=0).T
    neg = jnp.full((N_HEADS, WINDOW), NEG_INF, F32)
    line = jnp.concatenate([neg, by_dist[:, ::-1], neg[:, :WINDOW - 1]], axis=1)
    rows = line[:, None, :]
    span = 1
    while span < WINDOW:
        rows = jnp.concatenate([rows[:, :, span:], rows[:, :, :rows.shape[2] - span]], axis=1)
        span *= 2
    return rows


def kernel(x_prompt, x_sample, state_pool, state_conv, cache_swa_k, cache_swa_v, c_prompt, c_sample, w_ada, b_ada, norm_g, final_norm_g, w_in_even, w_out_even, w_pool, pool_scale, conv_w, w_in_odd, w_out_odd, gm_norm_g, gm_norm_b, gm_w_s, gm_b_s, attn_sinks, rel_bias, w_router, b_router, w_gate, w_up, w_down):
    d = D_MODEL
    bp, tp, _ = x_prompt.shape
    bs, ts, _ = x_sample.shape
    rows_s = bs * ts
    assert rows_s == ROW_TILE and tp % ROUTER_TILE == 0 and PAST_LEN % CHUNK == 0
    assert bp <= V7X_SUBLANES and CHUNK % ts == 0

    n_c = bp + bs
    c_pad = (-n_c) % V7X_SUBLANES
    c_all = jnp.concatenate([c_prompt, c_sample, jnp.zeros((c_pad, d), F32)], axis=0)
    mod_p = _adaln(c_all, w_ada, b_ada)
    mod_s = jnp.repeat(mod_p[:, bp:bp + bs], ts, axis=1)

    xp = x_prompt.reshape(bp * tp, d)
    xs_ = x_sample.reshape(rows_s, d)
    w_in0, w_in1 = w_in_even[0], w_in_odd[0]

    hp0, ya_p, pool_p = _pool_mixer(xp, norm_g[0, 0], mod_p, 0, w_in0, w_pool[0], pool_scale[0],
                                    None, bp, 1, MATMUL_TILE, 0)
    hs0, ya_s, pool_s = _pool_mixer(xs_, norm_g[0, 0], mod_s, 0, w_in0, w_pool[0], pool_scale[0],
                                    state_pool[0], bs, bs, ts, PAST_LEN)
    yb_p, conv_p = _conv_mixer(hp0, w_in0, conv_w[0], None, bp, 1, MATMUL_TILE)
    yb_s, conv_s = _conv_mixer(hs0, w_in0, conv_w[0], state_conv[0], bs, bs, ts)
    out_p = _outproj(ya_p, yb_p, xp, mod_p, 0, norm_g[0, 1], w_out_even[0], tp)
    out_s = _outproj(ya_s, yb_s, xs_, mod_s, 0, norm_g[0, 1], w_out_even[0], rows_s)
    (x2p, h1p), (x2s, h1s) = _moe(out_p, out_s, mod_p, mod_s, 0, norm_g[1, 0], tp,
                                  w_router, b_router, w_gate, w_up, w_down, final=False)

    bs_t = gm_b_s[0].T
    (yc_p,) = _gmlp_mixer(h1p, w_in1, gm_norm_g[0], gm_norm_b[0], gm_w_s[0], bs_t, CHUNK, CHUNK, False)
    yc_s, gv_s = _gmlp_mixer(h1s, w_in1, gm_norm_g[0], gm_norm_b[0], gm_w_s[0], bs_t, rows_s, ts, True)
    bias = _attention_bias(rel_bias)
    nkv = N_KV * HEAD_DIM
    bias_t = jnp.transpose(bias.reshape(N_KV, PAIRS_PER_KV, 2, WINDOW, 2 * WINDOW), (0, 2, 4, 1, 3))
    bias_t = bias_t.reshape(N_KV, 2, 2 * WINDOW, PAIRS_PER_KV * WINDOW)
    before_start = (np.arange(2 * WINDOW) < WINDOW)[None, None, :, None]
    bias_t = jnp.stack([bias_t, jnp.where(before_start, NEG_INF, bias_t)], axis=0)
    sinks_t = jnp.transpose(attn_sinks[0].reshape(N_KV, PAIRS_PER_KV, 2), (0, 2, 1))
    sinks_t = jnp.repeat(sinks_t, WINDOW, axis=-1).reshape(N_KV, 2, 1, PAIRS_PER_KV * WINDOW)
    yd_p, k_p, v_p = _swa_stream_mixer(h1p, w_in1, bias_t, sinks_t, tp // WINDOW)
    yd_s, k_s, v_s = _swa_cached_mixer(h1s, w_in1, cache_swa_k[0].reshape(bs, WINDOW, nkv),
                                       cache_swa_v[0].reshape(bs, WINDOW, nkv),
                                       bias[:, :ts, :].reshape(N_HEADS * ts, 2 * WINDOW),
                                       jnp.repeat(attn_sinks[0], ts).reshape(-1, 1), ts)
    out_p = _outproj(yc_p, yd_p, x2p, mod_p, 1, norm_g[1, 1], w_out_odd[0], tp)
    out_s = _outproj(yc_s, yd_s, x2s, mod_s, 1, norm_g[1, 1], w_out_odd[0], rows_s)
    (yp,), (ys_out,) = _moe(out_p, out_s, mod_p, mod_s, 1, final_norm_g, tp,
                            w_router, b_router, w_gate, w_up, w_down, final=True)

    k_p4 = k_p.reshape(bp, tp, nkv)[:, -WINDOW:].reshape(bp, WINDOW, N_KV, HEAD_DIM)
    v_p4 = v_p.reshape(bp, tp, nkv)[:, -WINDOW:].reshape(bp, WINDOW, N_KV, HEAD_DIM)
    k_s4 = jnp.concatenate([cache_swa_k[0], k_s.reshape(bs, ts, N_KV, HEAD_DIM)], axis=1)[:, -WINDOW:]
    v_s4 = jnp.concatenate([cache_swa_v[0], v_s.reshape(bs, ts, N_KV, HEAD_DIM)], axis=1)[:, -WINDOW:]
    return (yp.reshape(bp, tp, d), ys_out.reshape(bs, ts, d),
            pool_p[None], pool_s[None], conv_p[None], conv_s[None],
            k_p4[None], k_s4[None], v_p4[None], v_s4[None],
            gv_s.reshape(bs, ts, GM_WIDTH)[None])
```

```python
import functools

import numpy as np
import jax
import jax.numpy as jnp
from jax import lax
from jax.experimental import pallas as pl
from jax.experimental.pallas import tpu as pltpu

F32 = jnp.float32
BF16 = jnp.bfloat16
I32 = jnp.int32
U32 = jnp.uint32

D_MODEL = 2048
POOL_WINDOWS = (2, 4, 8, 16)
POOL_WIDTH = 1024
POOL_GROUP = 256
POOL_STATE = 15
CONV_WIDTH = 1024
CONV_K = 3
GM_WIDTH = 1024
GM_GROUPS = 8
GM_GROUP = 128
CHUNK = 128
HEAD_DIM = 64
N_HEADS = 16
N_KV = 2
WINDOW = 128
N_BUCKETS = 32
N_EXPERTS = 16
N_EXPERT_GROUPS = 4
EXP_PER_GROUP = 4
TOP_K = 2
EPS = 1e-6
NEG_INF = -1e30
PAST_LEN = 16384

V7X_SUBLANES = 8
V7X_LANES = 128
VMEM_LIMIT = 56 * 1024 * 1024

ROW_TILE = 256
MATMUL_TILE = 512
ROUTER_TILE = 1024
POOL_HALO = 16
CONV_HALO = 8
MOE_TILE = 256
HIDDEN_CHUNK = 256
ADALN_COL_TILE = 1024
CONV_COL_TILE = 512
TAB_EXPERT, TAB_VALID, TAB_LAST_TILE, TAB_NUSED, TAB_NEXT = 0, 1, 2, 3, 4


def _cparams(n_axes):
    return pltpu.CompilerParams(dimension_semantics=("arbitrary",) * n_axes,
                                vmem_limit_bytes=VMEM_LIMIT)


def _rms(x, g):
    return x * lax.rsqrt(jnp.mean(x * x, axis=-1, keepdims=True) + EPS) * g


def _mod_spec(mod, layer, part):
    nrow = ROW_TILE if mod.shape[1] == ROW_TILE else V7X_SUBLANES
    return pl.BlockSpec((1, nrow, D_MODEL), lambda *_: (layer, 0, part))


def _mod_rows(m_ref, seq):
    if m_ref.shape[1] == V7X_SUBLANES:
        return m_ref[0, pl.ds(seq, 1), :]
    return m_ref[0]


def _adaln_kernel(c_ref, w_ref, b_ref, o_ref):
    c = c_ref[...]
    a = (c * jax.nn.sigmoid(c)).astype(BF16)
    o_ref[0] = jnp.dot(a, w_ref[0].astype(BF16), preferred_element_type=F32) + b_ref[0]


def _adaln(c_all, w_ada, b_ada):
    depth, d, n6 = w_ada.shape
    m = c_all.shape[0]
    tn = ADALN_COL_TILE
    return pl.pallas_call(
        _adaln_kernel,
        grid=(depth, n6 // tn),
        in_specs=[pl.BlockSpec((m, d), lambda l, j: (0, 0)),
                  pl.BlockSpec((1, d, tn), lambda l, j: (l, 0, j)),
                  pl.BlockSpec((1, 1, tn), lambda l, j: (l, 0, j))],
        out_specs=pl.BlockSpec((1, m, tn), lambda l, j: (l, 0, j)),
        out_shape=jax.ShapeDtypeStruct((depth, m, n6), F32),
        compiler_params=_cparams(2),
        name="adaln",
    )(c_all, w_ada, b_ada.reshape(depth, 1, n6))


def _pool_kernel(x_ref, g_ref, sc_ref, sh_ref, w_ref, wp_ref, ps_ref, st_ref, h_ref, ya_ref, ns_ref,
                 wbf, wpbf, carry, *, nb, tm, tiles_per_seq, start):
    i = pl.program_id(0)
    t = i % tiles_per_seq
    seq = i // tiles_per_seq
    c = POOL_WIDTH
    halo = POOL_HALO

    @pl.when(i == 0)
    def _():
        wbf[...] = w_ref[...].astype(BF16)
        wpbf[...] = wp_ref[...].astype(BF16)

    @pl.when(t == 0)
    def _():
        carry[...] = st_ref[...]

    h = (_rms(x_ref[...], g_ref[...]) * (1.0 + _mod_rows(sc_ref, seq)) + _mod_rows(sh_ref, seq)).astype(BF16)
    h_ref[...] = h
    p = jnp.dot(h, wbf[...], preferred_element_type=F32)
    p3 = p.reshape(nb, tm, c)
    ext3 = jnp.concatenate([carry[...], p3], axis=1)
    tail = ext3[:, tm:tm + halo, :]
    ns_ref[...] = tail[:, halo - POOL_STATE:, :]
    carry[...] = tail
    ext = ext3.reshape(nb * (halo + tm), c)
    pos = start + t * tm + lax.broadcasted_iota(I32, (1, tm, 1), 1)
    outs = []
    for gi, w in enumerate(POOL_WINDOWS):
        sl = slice(gi * POOL_GROUP, (gi + 1) * POOL_GROUP)
        acc = ext[:, sl]
        shift = 1
        while shift < w:
            acc = acc + pltpu.roll(acc, shift, 0)
            shift *= 2
        win = acc.reshape(nb, halo + tm, POOL_GROUP)[:, halo:, :]
        cnt = jnp.minimum(pos + 1, w).astype(F32)
        dgrp = win / cnt - p3[:, :, sl]
        outs.append(jnp.dot(dgrp.reshape(nb * tm, POOL_GROUP).astype(BF16), wpbf[gi],
                            preferred_element_type=F32))
    y = jnp.concatenate(outs, axis=-1) * ps_ref[...]
    ya_ref[...] = y.astype(BF16)


def _pool_mixer(x2, g, mod, layer, w_in, w_pool, pool_scale, state, nseq, nb, tm, start):
    rows, d = x2.shape
    tiles_per_seq = (rows // nseq) // tm
    seq_blocks = nseq // nb
    c = POOL_WIDTH
    if state is None:
        st = jnp.zeros((nseq, POOL_HALO, c), F32)
    else:
        st = jnp.pad(state, ((0, 0), (POOL_HALO - POOL_STATE, 0), (0, 0)))
    kern = functools.partial(_pool_kernel, nb=nb, tm=tm, tiles_per_seq=tiles_per_seq, start=start)
    h2, ya, ns = pl.pallas_call(
        kern,
        grid=(seq_blocks * tiles_per_seq,),
        in_specs=[pl.BlockSpec((nb * tm, d), lambda i: (i, 0)),
                  pl.BlockSpec((1, d), lambda i: (0, 0)),
                  _mod_spec(mod, layer, 1), _mod_spec(mod, layer, 0),
                  pl.BlockSpec((d, c), lambda i: (0, 0)),
                  pl.BlockSpec((len(POOL_WINDOWS), POOL_GROUP, POOL_GROUP), lambda i: (0, 0, 0)),
                  pl.BlockSpec((1, c), lambda i: (0, 0)),
                  pl.BlockSpec((nb, POOL_HALO, c), lambda i: (i // tiles_per_seq, 0, 0))],
        out_specs=[pl.BlockSpec((nb * tm, d), lambda i: (i, 0)),
                   pl.BlockSpec((nb * tm, c), lambda i: (i, 0)),
                   pl.BlockSpec((nb, POOL_STATE, c), lambda i: (i // tiles_per_seq, 0, 0))],
        out_shape=[jax.ShapeDtypeStruct((rows, d), BF16),
                   jax.ShapeDtypeStruct((rows, c), BF16),
                   jax.ShapeDtypeStruct((nseq, POOL_STATE, c), F32)],
        scratch_shapes=[pltpu.VMEM((d, c), BF16),
                        pltpu.VMEM((len(POOL_WINDOWS), POOL_GROUP, POOL_GROUP), BF16),
                        pltpu.VMEM((nb, POOL_HALO, c), F32)],
        compiler_params=_cparams(1),
        name="pool_mixer",
    )(x2, g.reshape(1, d), mod, mod, w_in, w_pool, pool_scale.reshape(1, c), st)
    return h2, ya, ns


def _conv_kernel(h_ref, wx_ref, wb_ref, wc_ref, cw_ref, st_ref, yb_ref, ns_ref,
                 wxbf, wbbf, wcbf, carry, *, nb, tm, tiles_per_seq):
    i = pl.program_id(1)
    t = i % tiles_per_seq
    tc = wxbf.shape[1]
    halo = CONV_HALO

    @pl.when(i == 0)
    def _():
        wxbf[...] = wx_ref[...].astype(BF16)
        wbbf[...] = wb_ref[...].astype(BF16)
        wcbf[...] = wc_ref[...].astype(BF16)

    @pl.when(t == 0)
    def _():
        carry[...] = st_ref[...]

    h = h_ref[...]
    xin = jnp.dot(h, wxbf[...], preferred_element_type=F32)
    gb = jnp.dot(h, wbbf[...], preferred_element_type=F32)
    gc = jnp.dot(h, wcbf[...], preferred_element_type=F32)
    z3 = (gc * xin).reshape(nb, tm, tc)
    ext3 = jnp.concatenate([carry[...], z3], axis=1)
    tail = ext3[:, tm:tm + halo, :]
    ns_ref[...] = tail[:, halo - (CONV_K - 1):, :]
    carry[...] = tail
    ext = ext3.reshape(nb * (halo + tm), tc)
    cw = cw_ref[...]
    conv = cw[0:1, :] * pltpu.roll(ext, 2, 0) + cw[1:2, :] * pltpu.roll(ext, 1, 0) + cw[2:3, :] * ext
    conv = conv.reshape(nb, halo + tm, tc)[:, halo:, :].reshape(nb * tm, tc)
    yb_ref[...] = (gb * conv).astype(BF16)


def _conv_mixer(h2, w_in, conv_w, state, nseq, nb, tm):
    rows, d = h2.shape
    tiles_per_seq = (rows // nseq) // tm
    seq_blocks = nseq // nb
    c = CONV_WIDTH
    tc = CONV_COL_TILE
    cb = c // tc
    base = POOL_WIDTH // tc
    if state is None:
        st = jnp.zeros((nseq, CONV_HALO, c), F32)
    else:
        st = jnp.pad(state, ((0, 0), (CONV_HALO - (CONV_K - 1), 0), (0, 0)))
    kern = functools.partial(_conv_kernel, nb=nb, tm=tm, tiles_per_seq=tiles_per_seq)
    yb, ns = pl.pallas_call(
        kern,
        grid=(cb, seq_blocks * tiles_per_seq),
        in_specs=[pl.BlockSpec((nb * tm, d), lambda j, i: (i, 0)),
                  pl.BlockSpec((d, tc), lambda j, i: (0, base + j)),
                  pl.BlockSpec((d, tc), lambda j, i: (0, base + cb + j)),
                  pl.BlockSpec((d, tc), lambda j, i: (0, base + 2 * cb + j)),
                  pl.BlockSpec((CONV_K, tc), lambda j, i: (0, j)),
                  pl.BlockSpec((nb, CONV_HALO, tc), lambda j, i: (i // tiles_per_seq, 0, j))],
        out_specs=[pl.BlockSpec((nb * tm, tc), lambda j, i: (i, j)),
                   pl.BlockSpec((nb, CONV_K - 1, tc), lambda j, i: (i // tiles_per_seq, 0, j))],
        out_shape=[jax.ShapeDtypeStruct((rows, c), BF16),
                   jax.ShapeDtypeStruct((nseq, CONV_K - 1, c), F32)],
        scratch_shapes=[pltpu.VMEM((d, tc), BF16)] * 3 + [pltpu.VMEM((nb, CONV_HALO, tc), F32)],
        compiler_params=_cparams(2),
        name="conv_mixer",
    )(h2, w_in, w_in, w_in, conv_w, st)
    return yb, ns


def _pack_bf16_pairs(v):
    c = v.shape[1] // 2
    return pltpu.bitcast(pltpu.pack_elementwise([v[:, :c], v[:, c:]], packed_dtype=BF16), U32)


def _store_token_tiles(ref, v):
    rows = v.shape[0]
    for j in range(V7X_SUBLANES):
        ref[pl.ds(j, rows, stride=V7X_SUBLANES), :] = v[:, j * V7X_LANES:(j + 1) * V7X_LANES]


def _load_token_tiles(ref):
    rows = ref.shape[0] // V7X_SUBLANES
    return jnp.concatenate([ref[pl.ds(j, rows, stride=V7X_SUBLANES), :] for j in range(V7X_SUBLANES)],
                           axis=-1)


def _unpack_pairs_f32(w):
    return tuple(pltpu.unpack_elementwise(w, index=k, packed_dtype=BF16, unpacked_dtype=F32) for k in range(2))


def _unpack_bf16_pairs(w):
    lo, hi = _unpack_pairs_f32(w)
    return lo.astype(BF16), hi.astype(BF16)


def _outproj_kernel(ya_ref, yb_ref, x_ref, g1_ref, sc_ref, sh_ref, ng_ref, wo_ref,
                    x1_ref, hp_ref, hpt_ref, wobf, *, tiles_per_seq):
    i = pl.program_id(0)
    seq = i // tiles_per_seq

    @pl.when(i == 0)
    def _():
        wobf[...] = wo_ref[...].astype(BF16)

    ycat = jnp.concatenate([ya_ref[...], yb_ref[...]], axis=-1)
    y = jnp.dot(ycat, wobf[...], preferred_element_type=F32)
    x1 = x_ref[...] + _mod_rows(g1_ref, seq) * y
    x1_ref[...] = x1
    h2 = _rms(x1, ng_ref[...]) * (1.0 + _mod_rows(sc_ref, seq)) + _mod_rows(sh_ref, seq)
    packed = _pack_bf16_pairs(h2)
    hp_ref[...] = packed
    _store_token_tiles(hpt_ref, packed)


def _outproj(ya, yb, x2, mod, layer, ng, w_out, seq_rows):
    rows_all, d = x2.shape
    half = ya.shape[1]
    rt = ROW_TILE
    row_spec = lambda w: pl.BlockSpec((rt, w), lambda i: (i, 0))
    return pl.pallas_call(
        functools.partial(_outproj_kernel, tiles_per_seq=seq_rows // rt),
        grid=(rows_all // rt,),
        in_specs=[row_spec(half), row_spec(half), row_spec(d),
                  _mod_spec(mod, layer, 2), _mod_spec(mod, layer, 4), _mod_spec(mod, layer, 3),
                  pl.BlockSpec((1, d), lambda i: (0, 0)),
                  pl.BlockSpec((d, d), lambda i: (0, 0), pipeline_mode=pl.Buffered(1))],
        out_specs=[row_spec(d), row_spec(d // 2),
                   pl.BlockSpec((rt * V7X_SUBLANES, V7X_LANES), lambda i: (i, 0))],
        out_shape=[jax.ShapeDtypeStruct((rows_all, d), F32),
                   jax.ShapeDtypeStruct((rows_all, d // 2), U32),
                   jax.ShapeDtypeStruct((rows_all * V7X_SUBLANES, V7X_LANES), U32)],
        scratch_shapes=[pltpu.VMEM((d, d), BF16)],
        compiler_params=_cparams(1),
        name="outproj",
    )(ya, yb, x2, mod, mod, mod, ng.reshape(1, d), w_out)


def _router_kernel(hpp_ref, hps_ref, wr_ref, br_ref, pos_ref, rw_ref, tab_ref,
                   cnt_acc, totals, starts, padded, before_ref, s_all, sel_all, *, nt_p, rows_s):
    ph = pl.program_id(0)
    t = pl.program_id(1)
    last = nt_p
    r = hpp_ref.shape[0]
    half = hpp_ref.shape[1]
    ne = N_EXPERTS
    sub = lax.broadcasted_iota(I32, (ne, V7X_LANES), 0)

    @pl.when(t == 0)
    def _():
        cnt_acc[...] = jnp.zeros_like(cnt_acc)

    @pl.when((ph == 0) & (t == 0))
    def _():
        starts[...] = jnp.zeros_like(starts)
        padded[...] = jnp.zeros_like(padded)

    @pl.when((ph == 1) & (t == 0))
    def _():
        pad = jnp.floor((totals[...] + (MOE_TILE - 1.0)) * (1.0 / MOE_TILE)) * MOE_TILE
        run = pad
        k = 1
        while k < ne:
            run = run + jnp.where(sub >= k, pltpu.roll(run, k, 0), 0.0)
            k *= 2
        padded[...] = pad
        starts[...] = run - pad

    is_s = t == last
    eid = lax.broadcasted_iota(I32, (ne, r), 0)
    n_valid = jnp.where(is_s, rows_s, r)
    tok = lax.broadcasted_iota(I32, (ne, r), 1)

    @pl.when(ph == 0)
    def _():
        w_s = jnp.concatenate([hps_ref[...], jnp.zeros((r - rows_s, half), U32)], axis=0)
        w = jnp.where(is_s, w_s, hpp_ref[...])
        lo, hi = _unpack_bf16_pairs(w)
        wr = wr_ref[...].astype(BF16)
        log_t = (lax.dot_general(wr[:, :half], lo, NT_DIMS, preferred_element_type=F32)
                 + lax.dot_general(wr[:, half:], hi, NT_DIMS, preferred_element_type=F32))

        s = jax.nn.sigmoid(log_t)
        sg = s + br_ref[...]
        within = eid % EXP_PER_GROUP
        grp = eid // EXP_PER_GROUP

        def group_rot(x, k):
            return jnp.where(within + k < EXP_PER_GROUP,
                             pltpu.roll(x, ne - k, 0), pltpu.roll(x, EXP_PER_GROUP - k, 0))

        rank = jnp.zeros((ne, r), I32)
        for k in range(1, EXP_PER_GROUP):
            mate = group_rot(sg, k)
            wrapped = within + k >= EXP_PER_GROUP
            ahead = (mate > sg) | (wrapped & (mate == sg))
            rank = rank + ahead.astype(I32)
        top2 = rank < TOP_K
        kept = jnp.where(top2, sg, 0.0)
        gscore = kept
        for k in range(1, EXP_PER_GROUP):
            gscore = gscore + group_rot(kept, k)
        win = None
        for k in range(1, N_EXPERT_GROUPS):
            other = pltpu.roll(gscore, EXP_PER_GROUP * k, 0)
            beats = (gscore > other) | ((grp < k) & (gscore == other))
            win = beats if win is None else (win & beats)
        picked_now = top2 & win & (tok < n_valid)
        s_all[t] = s
        sel_all[t] = picked_now.astype(F32)

    s = s_all[t]
    selb = sel_all[t]
    sel = selb > 0.5
    cnt_before = cnt_acc[...]
    cnt_new = cnt_before + jnp.sum(selb, axis=1, keepdims=True)
    cnt_acc[...] = cnt_new

    @pl.when((ph == 0) & (t == 0))
    def _():
        src = lax.broadcasted_iota(I32, (r, r), 0)
        dst = lax.broadcasted_iota(I32, (r, r), 1)
        before_ref[...] = (src < dst).astype(BF16)

    @pl.when((ph == 0) & (t == last))
    def _():
        totals[...] = cnt_new

    @pl.when(ph == 1)
    def _():
        picked = jnp.where(sel, s, 0.0)
        wsum = jnp.sum(picked, axis=0, keepdims=True)
        gate = picked / jnp.where(tok[0:1, :] < n_valid, wsum, 1.0)
        ranks = jnp.dot(selb.astype(BF16), before_ref[...], preferred_element_type=F32)
        slot = (starts[...][:, 0:1] + cnt_before[:, 0:1] + ranks).astype(I32)
        e_a = jnp.min(jnp.where(sel, eid, ne), axis=0, keepdims=True)
        e_b = jnp.max(jnp.where(sel, eid, -1), axis=0, keepdims=True)
        is_a = sel & (eid == e_a)
        is_b = sel & (eid == e_b)
        pos_a = jnp.sum(jnp.where(is_a, slot, 0), axis=0, keepdims=True)
        pos_b = jnp.sum(jnp.where(is_b, slot, 0), axis=0, keepdims=True)
        w_a = jnp.sum(jnp.where(is_a, gate, 0.0), axis=0, keepdims=True)
        w_b = jnp.sum(jnp.where(is_b, gate, 0.0), axis=0, keepdims=True)
        pos_ref[0] = jnp.concatenate([pos_a, pos_b], axis=0)
        wmat = jnp.concatenate([w_a, w_b, jnp.zeros((V7X_LANES - 2, r), F32)], axis=0)
        rw_ref[...] = wmat.T

    @pl.when((ph == 1) & (t == last))
    def _():
        ends = starts[...] + padded[...]
        lane = lax.broadcasted_iota(I32, (ne, V7X_LANES), 1)
        tile_start = (lane * MOE_TILE).astype(F32)
        te = jnp.sum((tile_start >= ends).astype(I32), axis=0, keepdims=True)
        valid = te < ne
        last_e = jnp.max(jnp.where(padded[...] > 0.0, sub, 0), axis=0, keepdims=True)
        te = jnp.where(valid, te, last_e)
        n_used = jnp.sum(valid.astype(I32), axis=1, keepdims=True) + jnp.zeros((1, V7X_LANES), I32)
        last_tile = jnp.where(padded[...] > 0.0, ends - MOE_TILE, -1.0).astype(I32)
        last_tile_row = jnp.sum(jnp.where(sub == lane, last_tile, 0), axis=0, keepdims=True)
        later = jnp.min(jnp.where((sub > te) & (padded[...] > 0.0), sub, ne), axis=0, keepdims=True)
        next_e = jnp.where(later < ne, later, -1)
        zero = jnp.zeros((1, V7X_LANES), I32)
        tab_ref[...] = jnp.concatenate([te, valid.astype(I32), last_tile_row, n_used, next_e,
                                        zero, zero, zero], axis=0)


def _router(hp_p, hp_s, w_router, b_router):
    n_p, half = hp_p.shape
    rows_s = hp_s.shape[0]
    r = ROUTER_TILE
    nt_p = n_p // r
    nt = nt_p + 1
    kern = functools.partial(_router_kernel, nt_p=nt_p, rows_s=rows_s)
    pos, rw, tab = pl.pallas_call(
        kern,
        grid=(2, nt),
        in_specs=[pl.BlockSpec((r, half), lambda p, t: (jnp.minimum(t, nt_p - 1) * (1 - p), 0)),
                  pl.BlockSpec((rows_s, half), lambda p, t: (0, 0)),
                  pl.BlockSpec((N_EXPERTS, 2 * half), lambda p, t: (0, 0)),
                  pl.BlockSpec((N_EXPERTS, 1), lambda p, t: (0, 0))],
        out_specs=[pl.BlockSpec((1, TOP_K, r), lambda p, t: (p * t, 0, 0)),
                   pl.BlockSpec((r, V7X_LANES), lambda p, t: (p * t, 0)),
                   pl.BlockSpec((V7X_SUBLANES, V7X_LANES), lambda p, t: (0, 0))],
        out_shape=[jax.ShapeDtypeStruct((nt, TOP_K, r), I32),
                   jax.ShapeDtypeStruct((nt * r, V7X_LANES), F32),
                   jax.ShapeDtypeStruct((V7X_SUBLANES, V7X_LANES), I32)],
        scratch_shapes=[pltpu.VMEM((N_EXPERTS, V7X_LANES), F32)] * 4 + [pltpu.VMEM((r, r), BF16)]
        + [pltpu.VMEM((nt, N_EXPERTS, r), F32)] * 2,
        compiler_params=_cparams(2),
        name="router",
    )(hp_p, hp_s, w_router.T, b_router.reshape(N_EXPERTS, 1))
    return pos.reshape(-1), rw, tab.reshape(-1)


def _pos_index(tok0):
    return (tok0 // ROUTER_TILE) * (TOP_K * ROUTER_TILE) + tok0 % ROUTER_TILE


def _tokens(ref, first, n=1):
    start = pl.multiple_of(first * V7X_SUBLANES, V7X_SUBLANES)
    return ref.at[pl.ds(start, n * V7X_SUBLANES), :]


def _dispatch_kernel(pos_ref, tab_ref, hpp_ref, hps_ref, xs_ref, zbuf, sem, *, n_p_steps):
    i = pl.program_id(0)
    rows = hpp_ref.shape[0] // V7X_SUBLANES

    @pl.when(i == 0)
    def _():
        zbuf[...] = jnp.zeros_like(zbuf)

        def fill(e):
            first = pl.multiple_of(tab_ref[TAB_LAST_TILE * V7X_LANES + e], MOE_TILE)
            return pltpu.make_async_copy(zbuf, _tokens(xs_ref, first, MOE_TILE), sem)

        for e in range(N_EXPERTS):
            @pl.when(tab_ref[TAB_LAST_TILE * V7X_LANES + e] >= 0)
            def _():
                fill(e).start()
        for e in range(N_EXPERTS):
            @pl.when(tab_ref[TAB_LAST_TILE * V7X_LANES + e] >= 0)
            def _():
                fill(e).wait()

        def tail(j):
            first = pl.multiple_of(j * MOE_TILE, MOE_TILE)
            return pltpu.make_async_copy(zbuf, _tokens(xs_ref, first, MOE_TILE), sem)

        def tail_start(j, carry):
            tail(j).start()
            return carry

        def tail_wait(j, carry):
            tail(j).wait()
            return carry

        n_used = tab_ref[TAB_NUSED * V7X_LANES]
        n_tiles = xs_ref.shape[0] // (MOE_TILE * V7X_SUBLANES)
        lax.fori_loop(n_used, n_tiles, tail_start, 0)
        lax.fori_loop(n_used, n_tiles, tail_wait, 0)

    def scatter(src_ref, tok0):
        n = src_ref.shape[0] // V7X_SUBLANES
        base = _pos_index(tok0)

        def row_copy(r, dst):
            return pltpu.make_async_copy(_tokens(src_ref, r), _tokens(xs_ref, dst), sem)

        def issue(r, carry):
            row_copy(r, pos_ref[base + r]).start()
            row_copy(r, pos_ref[base + ROUTER_TILE + r]).start(priority=1)
            return carry

        lax.fori_loop(0, n, issue, 0, unroll=8)
        block = pltpu.make_async_copy(src_ref, _tokens(xs_ref, 0, n), sem)
        for _ in range(TOP_K):
            block.wait()

    @pl.when(i < n_p_steps)
    def _():
        scatter(hpp_ref, i * rows)

    @pl.when(i == n_p_steps)
    def _():
        scatter(hps_ref, n_p_steps * rows)


def _dispatch(pos, tab, hpt_p, hpt_s, n_rows_sorted):
    sub = V7X_SUBLANES
    n_p_steps = hpt_p.shape[0] // (MATMUL_TILE * sub)
    kern = functools.partial(_dispatch_kernel, n_p_steps=n_p_steps)
    blk = (MATMUL_TILE * sub, V7X_LANES)
    return pl.pallas_call(
        kern,
        grid_spec=pltpu.PrefetchScalarGridSpec(
            num_scalar_prefetch=2,
            grid=(n_p_steps + 1,),
            in_specs=[pl.BlockSpec(blk, lambda i, p, t: (jnp.minimum(i, n_p_steps - 1), 0)),
                      pl.BlockSpec(hpt_s.shape, lambda i, p, t: (0, 0))],
            out_specs=pl.BlockSpec(memory_space=pl.ANY),
            scratch_shapes=[pltpu.VMEM((MOE_TILE * sub, V7X_LANES), U32), pltpu.SemaphoreType.DMA(())]),
        out_shape=jax.ShapeDtypeStruct((n_rows_sorted * sub, V7X_LANES), U32),
        compiler_params=_cparams(1),
        name="moe_dispatch",
    )(pos, tab, hpt_p, hpt_s)


def _experts_kernel(tab_ref, xs_ref, wg_hbm, wu_hbm, wd_hbm, ys_ref,
                    wg32, wu32, wd32, wgbf, wubf, wdbf, slot_ref, sems, *, layer):
    i = pl.program_id(0)
    expert = tab_ref[TAB_EXPERT * V7X_LANES + i]
    prev = tab_ref[TAB_EXPERT * V7X_LANES + jnp.maximum(i - 1, 0)]
    upcoming = tab_ref[TAB_NEXT * V7X_LANES + i]
    changed = (i == 0) | (expert != prev)

    def weight_copies(e, slot):
        return (pltpu.make_async_copy(wg_hbm.at[layer, e], wg32.at[slot], sems.at[0, slot]),
                pltpu.make_async_copy(wu_hbm.at[layer, e], wu32.at[slot], sems.at[1, slot]),
                pltpu.make_async_copy(wd_hbm.at[layer, e], wd32.at[slot], sems.at[2, slot]))

    @pl.when(i == 0)
    def _():
        slot_ref[0] = 0
        for cp in weight_copies(expert, 0):
            cp.start()

    @pl.when(changed & (i > 0))
    def _():
        slot_ref[0] = 1 - slot_ref[0]

    def mlp(wg, wu, wd):
        x = jnp.concatenate(_unpack_bf16_pairs(_load_token_tiles(xs_ref)), axis=-1)
        y = None
        f = wg.shape[1]
        for c0 in range(0, f, HIDDEN_CHUNK):
            cs = slice(c0, c0 + HIDDEN_CHUNK)
            a = jnp.dot(x, wg[:, cs], preferred_element_type=F32)
            b = jnp.dot(x, wu[:, cs], preferred_element_type=F32)
            hid = ((a * jax.nn.sigmoid(a)) * b).astype(BF16)
            part = jnp.dot(hid, wd[cs, :], preferred_element_type=F32)
            y = part if y is None else y + part
        _store_token_tiles(ys_ref, _pack_bf16_pairs(y))

    for slot in range(2):
        @pl.when(changed & (slot_ref[0] == slot))
        def _():
            for cp in weight_copies(expert, slot):
                cp.wait()

            @pl.when(upcoming >= 0)
            def _():
                for cp in weight_copies(upcoming, 1 - slot):
                    cp.start(priority=1)

            wg = wg32[slot].astype(BF16)
            wu = wu32[slot].astype(BF16)
            wd = wd32[slot].astype(BF16)
            wgbf[...] = wg
            wubf[...] = wu
            wdbf[...] = wd
            mlp(wg, wu, wd)

    valid = tab_ref[TAB_VALID * V7X_LANES + i] > 0

    @pl.when(valid & jnp.logical_not(changed))
    def _():
        mlp(wgbf[...], wubf[...], wdbf[...])

    @pl.when(jnp.logical_not(valid))
    def _():
        ys_ref[...] = jnp.zeros_like(ys_ref)


def _experts(tab, xs, w_gate, w_up, w_down, layer):
    sub = V7X_SUBLANES
    _, _, d, f = w_gate.shape
    nt = xs.shape[0] // (MOE_TILE * sub)
    assert nt <= V7X_LANES and d == 2 * sub * V7X_LANES
    blk = (MOE_TILE * sub, V7X_LANES)

    def tile(i, tab_ref):
        return jnp.minimum(i, tab_ref[TAB_NUSED * V7X_LANES] - 1)

    hbm = pl.BlockSpec(memory_space=pl.ANY)
    return pl.pallas_call(
        functools.partial(_experts_kernel, layer=layer),
        grid_spec=pltpu.PrefetchScalarGridSpec(
            num_scalar_prefetch=1,
            grid=(nt,),
            in_specs=[pl.BlockSpec(blk, lambda i, t: (tile(i, t), 0)), hbm, hbm, hbm],
            out_specs=pl.BlockSpec(blk, lambda i, t: (i, 0)),
            scratch_shapes=[pltpu.VMEM((2, d, f), F32), pltpu.VMEM((2, d, f), F32), pltpu.VMEM((2, f, d), F32),
                            pltpu.VMEM((d, f), BF16), pltpu.VMEM((d, f), BF16), pltpu.VMEM((f, d), BF16),
                            pltpu.SMEM((1,), I32), pltpu.SemaphoreType.DMA((3, 2))]),
        out_shape=jax.ShapeDtypeStruct(xs.shape, U32),
        compiler_params=_cparams(1),
        name="moe_experts",
    )(tab, xs, w_gate, w_up, w_down)


def _combine_kernel(pos_ref, ys_ref, rw_ref, ng_ref, *rest, tiles_per_seq, n_p_steps, final):
    n_in = 4
    n_out = 1 if final else 2
    trunk_in = (rest[:n_in], rest[n_in:2 * n_in])
    outs = rest[2 * n_in:2 * n_in + 2 * n_out]
    trunk_out = (outs[:n_out], outs[n_out:])
    buf0, buf1, sems = rest[2 * n_in + 2 * n_out:]
    bufs = (buf0, buf1)
    i = pl.program_id(0)
    n_steps = n_p_steps + 1
    rows = rw_ref.shape[0]

    def gather(step, slot):
        base = _pos_index(step * rows)

        def issue(r, carry):
            for k in range(TOP_K):
                src = pos_ref[base + k * ROUTER_TILE + r]
                pltpu.make_async_copy(_tokens(ys_ref, src), _tokens(bufs[slot].at[k], r),
                                      sems.at[slot]).start(priority=k)
            return carry

        lax.fori_loop(0, rows, issue, 0, unroll=8)

    def drain(slot):
        for k in range(TOP_K):
            pltpu.make_async_copy(_tokens(ys_ref, 0, rows), bufs[slot].at[k], sems.at[slot]).wait()

    def finish(slot, trunk, seq):
        x1_ref, g2_ref, sc_ref, sh_ref = trunk_in[trunk]
        rw = rw_ref[...]
        lo_a, hi_a = _unpack_pairs_f32(_load_token_tiles(bufs[slot].at[0]))
        lo_b, hi_b = _unpack_pairs_f32(_load_token_tiles(bufs[slot].at[1]))
        w_a = rw[:, 0:1]
        w_b = rw[:, 1:2]
        moe = jnp.concatenate([w_a * lo_a + w_b * lo_b, w_a * hi_a + w_b * hi_b], axis=-1)
        x2 = x1_ref[...] + _mod_rows(g2_ref, seq) * moe
        if final:
            trunk_out[trunk][0][...] = _rms(x2, ng_ref[...])
        else:
            trunk_out[trunk][0][...] = x2
            trunk_out[trunk][1][...] = (_rms(x2, ng_ref[...]) * (1.0 + _mod_rows(sc_ref, seq))
                                        + _mod_rows(sh_ref, seq)).astype(BF16)

    @pl.when(i == 0)
    def _():
        gather(0, 0)

    for slot in range(2):
        @pl.when(i % 2 == slot)
        def _():
            @pl.when(i + 1 < n_steps)
            def _():
                gather(i + 1, 1 - slot)

            drain(slot)

            @pl.when(i < n_p_steps)
            def _():
                finish(slot, 0, i // tiles_per_seq)

            @pl.when(i == n_p_steps)
            def _():
                finish(slot, 1, 0)


def _combine(pos, ys, x1_p, x1_s, rw, mod_p, mod_s, layer, ng, seq_rows_p, final):
    n_p, d = x1_p.shape
    rt = ROW_TILE
    assert x1_s.shape[0] == rt and n_p % rt == 0 and ROUTER_TILE % rt == 0
    n_p_steps = n_p // rt
    kern = functools.partial(_combine_kernel, tiles_per_seq=seq_rows_p // rt, n_p_steps=n_p_steps, final=final)
    p_spec = lambda w: pl.BlockSpec((rt, w), lambda i, p: (jnp.minimum(i, n_p_steps - 1), 0))
    s_spec = lambda w: pl.BlockSpec((rt, w), lambda i, p: (0, 0))
    nxt = min(layer + 1, mod_p.shape[0] - 1)

    def trunk_specs(spec, mod):
        return [spec(d), _mod_spec(mod, layer, 5), _mod_spec(mod, nxt, 1), _mod_spec(mod, nxt, 0)]

    out_dtypes = [F32] if final else [F32, BF16]
    out_shape = ([jax.ShapeDtypeStruct((n_p, d), t) for t in out_dtypes]
                 + [jax.ShapeDtypeStruct((rt, d), t) for t in out_dtypes])
    out_specs = [p_spec(d)] * len(out_dtypes) + [s_spec(d)] * len(out_dtypes)
    res = pl.pallas_call(
        kern,
        grid_spec=pltpu.PrefetchScalarGridSpec(
            num_scalar_prefetch=1,
            grid=(n_p_steps + 1,),
            in_specs=[pl.BlockSpec(memory_space=pl.ANY),
                      pl.BlockSpec((rt, V7X_LANES), lambda i, p: (i, 0)),
                      pl.BlockSpec((1, d), lambda i, p: (0, 0))]
            + trunk_specs(p_spec, mod_p) + trunk_specs(s_spec, mod_s),
            out_specs=out_specs,
            scratch_shapes=[pltpu.VMEM((TOP_K, rt * V7X_SUBLANES, V7X_LANES), U32)] * 2
            + [pltpu.SemaphoreType.DMA((2,))]),
        out_shape=out_shape,
        compiler_params=_cparams(1),
        name="moe_combine_final" if final else "moe_combine",
    )(pos, ys, rw, ng.reshape(1, d), x1_p, mod_p, mod_p, mod_p, x1_s, mod_s, mod_s, mod_s)
    return res[:len(out_dtypes)], res[len(out_dtypes):]


def _moe(out_p, out_s, mod_p, mod_s, layer, ng, seq_rows_p, w_router, b_router, w_gate, w_up, w_down, final):
    x1_p, hp_p, hpt_p = out_p
    x1_s, hp_s, hpt_s = out_s
    n_p = x1_p.shape[0]
    n_tok = n_p + x1_s.shape[0]
    max_rows = TOP_K * n_tok + N_EXPERTS * (MOE_TILE - 1)
    n_rows_sorted = -(-max_rows // MOE_TILE) * MOE_TILE
    pos, rw, tab = _router(hp_p, hp_s, w_router, b_router)
    xs = _dispatch(pos, tab, hpt_p, hpt_s, n_rows_sorted)
    ys = _experts(tab, xs, w_gate, w_up, w_down, layer)
    return _combine(pos, ys, x1_p, x1_s, rw, mod_p, mod_s, layer, ng, seq_rows_p, final)


def _gmlp_kernel(h_ref, w_ref, lg_ref, lb_ref, ws_ref, bs_ref, yc_ref, *rest, ell, blk, emit_v):
    if emit_v:
        gv_ref, wbf, wsbf = rest
    else:
        wbf, wsbf = rest
    i = pl.program_id(0)
    rows = h_ref.shape[0]
    c = GM_WIDTH

    @pl.when(i == 0)
    def _():
        wbf[...] = w_ref[...].astype(BF16)
        r = lax.broadcasted_iota(I32, (ell, ell), 0)
        s = lax.broadcasted_iota(I32, (ell, ell), 1)
        keep = (r >= s) & ((r // blk) == (s // blk))
        rsel = (lax.broadcasted_iota(I32, (ell, CHUNK), 0) % blk
                == lax.broadcasted_iota(I32, (ell, CHUNK), 1)).astype(BF16)
        csel = (lax.broadcasted_iota(I32, (CHUNK, ell), 1) % blk
                == lax.broadcasted_iota(I32, (CHUNK, ell), 0)).astype(BF16)
        for g in range(GM_GROUPS):
            wchunk = ws_ref[g].astype(BF16)
            if blk == ell:
                full = wchunk
            else:
                rowsp = jnp.dot(rsel, wchunk, preferred_element_type=F32).astype(BF16)
                full = jnp.dot(rowsp, csel, preferred_element_type=F32).astype(BF16)
            wsbf[g] = jnp.where(keep, full, jnp.zeros_like(full))

    uv = jnp.dot(h_ref[...], wbf[...], preferred_element_type=F32)
    u = uv[:, :c]
    v = uv[:, c:]
    vc = v - jnp.mean(v, axis=-1, keepdims=True)
    vn = vc * lax.rsqrt(jnp.mean(vc * vc, axis=-1, keepdims=True) + EPS) * lg_ref[...] + lb_ref[...]
    if emit_v:
        gv_ref[...] = vn
    vb = vn.astype(BF16)
    bs = bs_ref[...]
    for ch in range(rows // ell):
        rs = slice(ch * ell, (ch + 1) * ell)
        outs = []
        for g in range(GM_GROUPS):
            cs = slice(g * GM_GROUP, (g + 1) * GM_GROUP)
            mixed = jnp.dot(wsbf[g], vb[rs, cs], preferred_element_type=F32)
            mixed = (mixed.reshape(ell // blk, blk, GM_GROUP) + bs[:blk, g:g + 1][None]).reshape(ell, GM_GROUP)
            outs.append(u[rs, cs] * mixed)
        yc_ref[rs, :] = jnp.concatenate(outs, axis=-1).astype(BF16)


def _gmlp_mixer(h2, w_in, ln_g, ln_b, ws, bs_t, ell, blk, emit_v):
    rows, d = h2.shape
    c = GM_WIDTH
    kern = functools.partial(_gmlp_kernel, ell=ell, blk=blk, emit_v=emit_v)
    rt = min(rows, MATMUL_TILE)
    out_specs = [pl.BlockSpec((rt, c), lambda i: (i, 0))]
    out_shape = [jax.ShapeDtypeStruct((rows, c), BF16)]
    if emit_v:
        out_specs.append(pl.BlockSpec((rt, c), lambda i: (i, 0)))
        out_shape.append(jax.ShapeDtypeStruct((rows, c), F32))
    return pl.pallas_call(
        kern,
        grid=(rows // rt,),
        in_specs=[pl.BlockSpec((rt, d), lambda i: (i, 0)),
                  pl.BlockSpec((d, 2 * c), lambda i: (0, 0), pipeline_mode=pl.Buffered(1)),
                  pl.BlockSpec((1, c), lambda i: (0, 0)),
                  pl.BlockSpec((1, c), lambda i: (0, 0)),
                  pl.BlockSpec((GM_GROUPS, CHUNK, CHUNK), lambda i: (0, 0, 0)),
                  pl.BlockSpec((CHUNK, GM_GROUPS), lambda i: (0, 0))],
        out_specs=out_specs,
        out_shape=out_shape,
        scratch_shapes=[pltpu.VMEM((d, 2 * c), BF16), pltpu.VMEM((GM_GROUPS, ell, ell), BF16)],
        compiler_params=_cparams(1),
        name="gmlp_mixer",
    )(h2, w_in, ln_g.reshape(1, c), ln_b.reshape(1, c), ws, bs_t)


PAIR_W = 2 * HEAD_DIM
PAIRS_PER_KV = N_HEADS // N_KV // 2
NT_DIMS = (((1,), (1,)), ((), ()))
SCORE_SCALE = HEAD_DIM ** -0.5
assert float(np.log2(SCORE_SCALE)).is_integer()


def _swa_project(i, h_ref, wq_ref, wkv_ref, wbf):
    nq = N_HEADS * HEAD_DIM

    @pl.when(i == 0)
    def _():
        wbf[:, :nq] = wq_ref[...].astype(BF16)
        wbf[:, nq:] = wkv_ref[...].astype(BF16)

    return jnp.dot(h_ref[...], wbf[...], preferred_element_type=F32)


def _pair_block_diag(a, a_swapped, hk, axis):
    dim_axis = 1 - axis
    low = lax.broadcasted_iota(I32, a.shape, dim_axis) < HEAD_DIM
    lo, hi = (a, a_swapped) if hk == 0 else (a_swapped, a)
    return jnp.concatenate([jnp.where(low, lo, 0.0), jnp.where(low, 0.0, hi)], axis=axis).astype(BF16)


def _stack_pairs(qkv, rs, hk, scale=None):
    p0 = hk * PAIRS_PER_KV
    q = jnp.concatenate([qkv[rs, (p0 + pp) * PAIR_W:(p0 + pp + 1) * PAIR_W]
                         for pp in range(PAIRS_PER_KV)], axis=0)
    return (q if scale is None else q * scale).astype(BF16)


def _swa_cached_kernel(h_ref, wq_ref, wkv_ref, kp_ref, vp_ref, bias_ref, sink_ref, yd_ref, k_ref, v_ref,
                       wbf, *, tq):
    i = pl.program_id(0)
    rows = h_ref.shape[0]
    nq = N_HEADS * HEAD_DIM
    nkv = N_KV * HEAD_DIM
    n_blocks = rows // tq
    qkv = _swa_project(i, h_ref, wq_ref, wkv_ref, wbf)
    k_new = qkv[:, nq:nq + nkv]
    v_new = qkv[:, nq + nkv:]
    k_ref[...] = k_new
    v_ref[...] = v_new
    pad = jnp.zeros((WINDOW - tq, nkv), F32)

    scores, vbds = [], []
    for blk in range(n_blocks):
        rs = slice(blk * tq, (blk + 1) * tq)
        kcat = jnp.concatenate([kp_ref[blk], k_new[rs], pad], axis=0)
        vcat = jnp.concatenate([vp_ref[blk], v_new[rs], pad], axis=0)
        kswap = pltpu.roll(kcat, HEAD_DIM, 1)
        vswap = pltpu.roll(vcat, HEAD_DIM, 1)
        per_head = []
        for hk in range(N_KV):
            kbd = _pair_block_diag(kcat, kswap, hk, 0)
            vbds.append(_pair_block_diag(vcat, vswap, hk, 0))
            s4 = lax.dot_general(_stack_pairs(qkv, rs, hk), kbd, NT_DIMS,
                                 preferred_element_type=F32) * (HEAD_DIM ** -0.5)
            for pp in range(PAIRS_PER_KV):
                for sub in range(2):
                    per_head.append(s4[pp * tq:(pp + 1) * tq, sub * 2 * WINDOW:(sub + 1) * 2 * WINDOW])
        scores.append(jnp.concatenate(per_head, axis=0))

    s_all = jnp.stack(scores, axis=0) + bias_ref[...][None]
    sink = sink_ref[...][None]
    m = jnp.maximum(jnp.max(s_all, axis=-1, keepdims=True), sink)
    pr = jnp.exp(s_all - m)
    pr = pr / (jnp.sum(pr, axis=-1, keepdims=True) + jnp.exp(sink - m))

    for blk in range(n_blocks):
        outs = []
        for hk in range(N_KV):
            p4 = []
            for pp in range(PAIRS_PER_KV):
                h0 = 2 * (hk * PAIRS_PER_KV + pp)
                p4.append(jnp.concatenate([pr[blk, h0 * tq:(h0 + 1) * tq, :],
                                           pr[blk, (h0 + 1) * tq:(h0 + 2) * tq, :]], axis=-1))
            o4 = jnp.dot(jnp.concatenate(p4, axis=0).astype(BF16), vbds[blk * N_KV + hk],
                         preferred_element_type=F32)
            outs.extend(o4[pp * tq:(pp + 1) * tq, :] for pp in range(PAIRS_PER_KV))
        yd_ref[blk * tq:(blk + 1) * tq, :] = jnp.concatenate(outs, axis=-1).astype(BF16)


def _swa_stream_kernel(h_ref, wq_ref, wkv_ref, bias_ref, sink_ref, yd_ref, k_ref, v_ref,
                       wbf, kprev, vprev_t, *, blocks_per_seq):
    i = pl.program_id(0)
    rows = h_ref.shape[0]
    nq = N_HEADS * HEAD_DIM
    nkv = N_KV * HEAD_DIM
    tq = WINDOW
    n_blocks = rows // tq

    @pl.when(i == 0)
    def _():
        kprev[...] = jnp.zeros_like(kprev)
        vprev_t[...] = jnp.zeros_like(vprev_t)

    qkv = _swa_project(i, h_ref, wq_ref, wkv_ref, wbf)
    k_new = qkv[:, nq:nq + nkv]
    v_new = qkv[:, nq + nkv:]
    k_ref[...] = k_new
    v_ref[...] = v_new
    v_new_t = v_new.T
    lanes = PAIRS_PER_KV * tq

    for blk in range(n_blocks):
        rs = slice(blk * tq, (blk + 1) * tq)
        first = ((i * n_blocks + blk) % blocks_per_seq == 0).astype(I32)
        k_cur = k_new[rs]
        v_cur_t = v_new_t[:, rs]
        kcat = jnp.concatenate([kprev[...], k_cur], axis=0)
        vcat_t = jnp.concatenate([vprev_t[...], v_cur_t], axis=1)
        kprev[...] = k_cur
        vprev_t[...] = v_cur_t
        kswap = pltpu.roll(kcat, HEAD_DIM, 1)
        vswap_t = pltpu.roll(vcat_t, HEAD_DIM, 0)
        outs = []
        for hk in range(N_KV):
            kbd = _pair_block_diag(kcat, kswap, hk, 0)
            vbd_t = _pair_block_diag(vcat_t, vswap_t, hk, 1)
            st = lax.dot_general(kbd, _stack_pairs(qkv, rs, hk, SCORE_SCALE), NT_DIMS,
                                 preferred_element_type=F32)
            s3 = st.reshape(2, 2 * WINDOW, lanes) + bias_ref[first, hk]
            sink = sink_ref[hk]
            m = jnp.maximum(jnp.max(s3, axis=1, keepdims=True), sink)
            pr = jnp.exp(s3 - m)
            inv = 1.0 / (jnp.sum(pr, axis=1, keepdims=True) + jnp.exp(sink - m))
            o_t = jnp.dot(vbd_t, pr.reshape(4 * WINDOW, lanes).astype(BF16),
                          preferred_element_type=F32)
            norm = jnp.concatenate([jnp.broadcast_to(inv[sub], (HEAD_DIM, lanes)) for sub in range(2)], axis=0)
            o4 = (o_t * norm).T
            outs.extend(o4[pp * tq:(pp + 1) * tq, :] for pp in range(PAIRS_PER_KV))
        yd_ref[rs, :] = jnp.concatenate(outs, axis=-1).astype(BF16)


def _swa_weight_specs(w_in, d):
    nq = N_HEADS * HEAD_DIM
    nkv = N_KV * HEAD_DIM
    nw = nq + 2 * nkv
    q_blk = (w_in.shape[1] - nw) // nq
    kv_blk = (w_in.shape[1] - 2 * nkv) // (2 * nkv)
    assert q_blk * nq + nw == w_in.shape[1] and kv_blk * 2 * nkv + 2 * nkv == w_in.shape[1]
    return [pl.BlockSpec((d, nq), lambda i: (0, q_blk)), pl.BlockSpec((d, 2 * nkv), lambda i: (0, kv_blk))]


def _swa_outputs(rows, rt):
    nq = N_HEADS * HEAD_DIM
    nkv = N_KV * HEAD_DIM
    specs = [pl.BlockSpec((rt, nq), lambda i: (i, 0)),
             pl.BlockSpec((rt, nkv), lambda i: (i, 0)),
             pl.BlockSpec((rt, nkv), lambda i: (i, 0))]
    shapes = [jax.ShapeDtypeStruct((rows, nq), BF16),
              jax.ShapeDtypeStruct((rows, nkv), F32),
              jax.ShapeDtypeStruct((rows, nkv), F32)]
    return specs, shapes


def _swa_cached_mixer(h2, w_in, k_cache, v_cache, bias, sinks, tq):
    rows, d = h2.shape
    nkv = N_KV * HEAD_DIM
    nw = N_HEADS * HEAD_DIM + 2 * nkv
    n_blocks = ROW_TILE // tq
    cache_spec = pl.BlockSpec((n_blocks, WINDOW, nkv), lambda i: (i, 0, 0))
    out_specs, out_shape = _swa_outputs(rows, ROW_TILE)
    return pl.pallas_call(
        functools.partial(_swa_cached_kernel, tq=tq),
        grid=(rows // ROW_TILE,),
        in_specs=[pl.BlockSpec((ROW_TILE, d), lambda i: (i, 0))] + _swa_weight_specs(w_in, d)
        + [cache_spec, cache_spec,
           pl.BlockSpec((N_HEADS * tq, 2 * WINDOW), lambda i: (0, 0)),
           pl.BlockSpec((N_HEADS * tq, 1), lambda i: (0, 0))],
        out_specs=out_specs,
        out_shape=out_shape,
        scratch_shapes=[pltpu.VMEM((d, nw), BF16)],
        compiler_params=_cparams(1),
        name="swa_cached",
    )(h2, w_in, w_in, k_cache, v_cache, bias, sinks)


def _swa_stream_mixer(h2, w_in, bias_t, sinks_t, blocks_per_seq):
    rows, d = h2.shape
    nkv = N_KV * HEAD_DIM
    nw = N_HEADS * HEAD_DIM + 2 * nkv
    lanes = PAIRS_PER_KV * WINDOW
    rt = min(rows, MATMUL_TILE)
    out_specs, out_shape = _swa_outputs(rows, rt)
    return pl.pallas_call(
        functools.partial(_swa_stream_kernel, blocks_per_seq=blocks_per_seq),
        grid=(rows // rt,),
        in_specs=[pl.BlockSpec((rt, d), lambda i: (i, 0))] + _swa_weight_specs(w_in, d)
        + [pl.BlockSpec((2, N_KV, 2, 2 * WINDOW, lanes), lambda i: (0, 0, 0, 0, 0)),
           pl.BlockSpec((N_KV, 2, 1, lanes), lambda i: (0, 0, 0, 0))],
        out_specs=out_specs,
        out_shape=out_shape,
        scratch_shapes=[pltpu.VMEM((d, nw), BF16), pltpu.VMEM((WINDOW, nkv), F32),
                        pltpu.VMEM((nkv, WINDOW), F32)],
        compiler_params=_cparams(1),
        name="swa_stream",
    )(h2, w_in, w_in, bias_t, sinks_t)


def _t5_bucket(dist):
    max_exact = N_BUCKETS // 2
    dd = np.maximum(dist, 1)
    large = max_exact + (np.log(dd / max_exact) / np.log(WINDOW / max_exact)
                         * (N_BUCKETS - max_exact)).astype(np.int64)
    large = np.minimum(large, N_BUCKETS - 1)
    return np.where(dist < max_exact, dist, large).astype(np.int32)


def _attention_bias(rel_bias):
    by_dist = jnp.take(rel_bias.astype(F32), _t5_bucket(np.arange(WINDOW)), axis=0).T
    neg = jnp.full((N_HEADS, WINDOW), NEG_INF, F32)
    line = jnp.concatenate([neg, by_dist[:, ::-1], neg[:, :WINDOW - 1]], axis=1)
    rows = line[:, None, :]
    span = 1
    while span < WINDOW:
        rows = jnp.concatenate([rows[:, :, span:], rows[:, :, :rows.shape[2] - span]], axis=1)
        span *= 2
    return rows


def kernel(x_prompt, x_sample, state_pool, state_conv, cache_swa_k, cache_swa_v, c_prompt, c_sample, w_ada, b_ada, norm_g, final_norm_g, w_in_even, w_out_even, w_pool, pool_scale, conv_w, w_in_odd, w_out_odd, gm_norm_g, gm_norm_b, gm_w_s, gm_b_s, attn_sinks, rel_bias, w_router, b_router, w_gate, w_up, w_down):
    d = D_MODEL
    bp, tp, _ = x_prompt.shape
    bs, ts, _ = x_sample.shape
    rows_s = bs * ts
    assert rows_s == ROW_TILE and tp % ROUTER_TILE == 0 and PAST_LEN % CHUNK == 0
    assert bp <= V7X_SUBLANES and CHUNK % ts == 0

    n_c = bp + bs
    c_pad = (-n_c) % V7X_SUBLANES
    c_all = jnp.concatenate([c_prompt, c_sample, jnp.zeros((c_pad, d), F32)], axis=0)
    mod_p = _adaln(c_all, w_ada, b_ada)
    mod_s = jnp.repeat(mod_p[:, bp:bp + bs], ts, axis=1)

    xp = x_prompt.reshape(bp * tp, d)
    xs_ = x_sample.reshape(rows_s, d)
    w_in0, w_in1 = w_in_even[0], w_in_odd[0]

    hp0, ya_p, pool_p = _pool_mixer(xp, norm_g[0, 0], mod_p, 0, w_in0, w_pool[0], pool_scale[0],
                                    None, bp, 1, MATMUL_TILE, 0)
    hs0, ya_s, pool_s = _pool_mixer(xs_, norm_g[0, 0], mod_s, 0, w_in0, w_pool[0], pool_scale[0],
                                    state_pool[0], bs, bs, ts, PAST_LEN)
    yb_p, conv_p = _conv_mixer(hp0, w_in0, conv_w[0], None, bp, 1, MATMUL_TILE)
    yb_s, conv_s = _conv_mixer(hs0, w_in0, conv_w[0], state_conv[0], bs, bs, ts)
    out_p = _outproj(ya_p, yb_p, xp, mod_p, 0, norm_g[0, 1], w_out_even[0], tp)
    out_s = _outproj(ya_s, yb_s, xs_, mod_s, 0, norm_g[0, 1], w_out_even[0], rows_s)
    (x2p, h1p), (x2s, h1s) = _moe(out_p, out_s, mod_p, mod_s, 0, norm_g[1, 0], tp,
                                  w_router, b_router, w_gate, w_up, w_down, final=False)

    bs_t = gm_b_s[0].T
    (yc_p,) = _gmlp_mixer(h1p, w_in1, gm_norm_g[0], gm_norm_b[0], gm_w_s[0], bs_t, CHUNK, CHUNK, False)
    yc_s, gv_s = _gmlp_mixer(h1s, w_in1, gm_norm_g[0], gm_norm_b[0], gm_w_s[0], bs_t, rows_s, ts, True)
    bias = _attention_bias(rel_bias)
    nkv = N_KV * HEAD_DIM
    bias_t = jnp.transpose(bias.reshape(N_KV, PAIRS_PER_KV, 2, WINDOW, 2 * WINDOW), (0, 2, 4, 1, 3))
    bias_t = bias_t.reshape(N_KV, 2, 2 * WINDOW, PAIRS_PER_KV * WINDOW)
    before_start = (np.arange(2 * WINDOW) < WINDOW)[None, None, :, None]
    bias_t = jnp.stack([bias_t, jnp.where(before_start, NEG_INF, bias_t)], axis=0)
    sinks_t = jnp.transpose(attn_sinks[0].reshape(N_KV, PAIRS_PER_KV, 2), (0, 2, 1))
    sinks_t = jnp.repeat(sinks_t, WINDOW, axis=-1).reshape(N_KV, 2, 1, PAIRS_PER_KV * WINDOW)
    yd_p, k_p, v_p = _swa_stream_mixer(h1p, w_in1, bias_t, sinks_t, tp // WINDOW)
    yd_s, k_s, v_s = _swa_cached_mixer(h1s, w_in1, cache_swa_k[0].reshape(bs, WINDOW, nkv),
                                       cache_swa_v[0].reshape(bs, WINDOW, nkv),
                                       bias[:, :ts, :].reshape(N_HEADS * ts, 2 * WINDOW),
                                       jnp.repeat(attn_sinks[0], ts).reshape(-1, 1), ts)
    out_p = _outproj(yc_p, yd_p, x2p, mod_p, 1, norm_g[1, 1], w_out_odd[0], tp)
    out_s = _outproj(yc_s, yd_s, x2s, mod_s, 1, norm_g[1, 1], w_out_odd[0], rows_s)
    (yp,), (ys_out,) = _moe(out_p, out_s, mod_p, mod_s, 1, final_norm_g, tp,
                            w_router, b_router, w_gate, w_up, w_down, final=True)

    k_p4 = k_p.reshape(bp, tp, nkv)[:, -WINDOW:].reshape(bp, WINDOW, N_KV, HEAD_DIM)
    v_p4 = v_p.reshape(bp, tp, nkv)[:, -WINDOW:].reshape(bp, WINDOW, N_KV, HEAD_DIM)
    k_s4 = jnp.concatenate([cache_swa_k[0], k_s.reshape(bs, ts, N_KV, HEAD_DIM)], axis=1)[:, -WINDOW:]
    v_s4 = jnp.concatenate([cache_swa_v[0], v_s.reshape(bs, ts, N_KV, HEAD_DIM)], axis=1)[:, -WINDOW:]
    return (yp.reshape(bp, tp, d), ys_out.reshape(bs, ts, d),
            pool_p[None], pool_s[None], conv_p[None], conv_s[None],
            k_p4[None], k_s4[None], v_p4[None], v_s4[None],
            gv_s.reshape(bs, ts, GM_WIDTH)[None])
```

```python
import functools

import numpy as np
import jax
import jax.numpy as jnp
from jax import lax
from jax.experimental import pallas as pl
from jax.experimental.pallas import tpu as pltpu

F32 = jnp.float32
BF16 = jnp.bfloat16
I32 = jnp.int32
U32 = jnp.uint32

D_MODEL = 2048
POOL_WINDOWS = (2, 4, 8, 16)
POOL_WIDTH = 1024
POOL_GROUP = 256
POOL_STATE = 15
CONV_WIDTH = 1024
CONV_K = 3
GM_WIDTH = 1024
GM_GROUPS = 8
GM_GROUP = 128
CHUNK = 128
HEAD_DIM = 64
N_HEADS = 16
N_KV = 2
WINDOW = 128
N_BUCKETS = 32
N_EXPERTS = 16
N_EXPERT_GROUPS = 4
EXP_PER_GROUP = 4
TOP_K = 2
EPS = 1e-6
NEG_INF = -1e30
PAST_LEN = 16384

V7X_SUBLANES = 8
V7X_LANES = 128
VMEM_LIMIT = 56 * 1024 * 1024

ROW_TILE = 256
MATMUL_TILE = 512
ROUTER_TILE = 1024
POOL_HALO = 16
CONV_HALO = 8
MOE_TILE = 256
ADALN_COL_TILE = 1024
CONV_COL_TILE = 512
TAB_EXPERT, TAB_VALID, TAB_LAST_TILE, TAB_NUSED, TAB_NEXT = 0, 1, 2, 3, 4


def _cparams(n_axes):
    return pltpu.CompilerParams(dimension_semantics=("arbitrary",) * n_axes,
                                vmem_limit_bytes=VMEM_LIMIT)


def _rms(x, g):
    return x * lax.rsqrt(jnp.mean(x * x, axis=-1, keepdims=True) + EPS) * g


def _mod_spec(mod, layer, part):
    nrow = ROW_TILE if mod.shape[1] == ROW_TILE else V7X_SUBLANES
    return pl.BlockSpec((1, nrow, D_MODEL), lambda *_: (layer, 0, part))


def _mod_rows(m_ref, seq):
    if m_ref.shape[1] == V7X_SUBLANES:
        return m_ref[0, pl.ds(seq, 1), :]
    return m_ref[0]


def _adaln_kernel(c_ref, w_ref, b_ref, o_ref):
    c = c_ref[...]
    a = (c * jax.nn.sigmoid(c)).astype(BF16)
    o_ref[0] = jnp.dot(a, w_ref[0].astype(BF16), preferred_element_type=F32) + b_ref[0]


def _adaln(c_all, w_ada, b_ada):
    depth, d, n6 = w_ada.shape
    m = c_all.shape[0]
    tn = ADALN_COL_TILE
    return pl.pallas_call(
        _adaln_kernel,
        grid=(depth, n6 // tn),
        in_specs=[pl.BlockSpec((m, d), lambda l, j: (0, 0)),
                  pl.BlockSpec((1, d, tn), lambda l, j: (l, 0, j)),
                  pl.BlockSpec((1, 1, tn), lambda l, j: (l, 0, j))],
        out_specs=pl.BlockSpec((1, m, tn), lambda l, j: (l, 0, j)),
        out_shape=jax.ShapeDtypeStruct((depth, m, n6), F32),
        compiler_params=_cparams(2),
        name="adaln",
    )(c_all, w_ada, b_ada.reshape(depth, 1, n6))


def _pool_kernel(x_ref, g_ref, sc_ref, sh_ref, w_ref, wp_ref, ps_ref, st_ref, h_ref, ya_ref, ns_ref,
                 wbf, wpbf, carry, *, nb, tm, tiles_per_seq, start):
    i = pl.program_id(0)
    t = i % tiles_per_seq
    seq = i // tiles_per_seq
    c = POOL_WIDTH
    halo = POOL_HALO

    @pl.when(i == 0)
    def _():
        wbf[...] = w_ref[...].astype(BF16)
        wpbf[...] = wp_ref[...].astype(BF16)

    @pl.when(t == 0)
    def _():
        carry[...] = st_ref[...]

    h = (_rms(x_ref[...], g_ref[...]) * (1.0 + _mod_rows(sc_ref, seq)) + _mod_rows(sh_ref, seq)).astype(BF16)
    h_ref[...] = h
    p = jnp.dot(h, wbf[...], preferred_element_type=F32)
    p3 = p.reshape(nb, tm, c)
    ext3 = jnp.concatenate([carry[...], p3], axis=1)
    tail = ext3[:, tm:tm + halo, :]
    ns_ref[...] = tail[:, halo - POOL_STATE:, :]
    carry[...] = tail
    ext = ext3.reshape(nb * (halo + tm), c)
    pos = start + t * tm + lax.broadcasted_iota(I32, (1, tm, 1), 1)
    outs = []
    for gi, w in enumerate(POOL_WINDOWS):
        sl = slice(gi * POOL_GROUP, (gi + 1) * POOL_GROUP)
        acc = ext[:, sl]
        shift = 1
        while shift < w:
            acc = acc + pltpu.roll(acc, shift, 0)
            shift *= 2
        win = acc.reshape(nb, halo + tm, POOL_GROUP)[:, halo:, :]
        cnt = jnp.minimum(pos + 1, w).astype(F32)
        dgrp = win / cnt - p3[:, :, sl]
        outs.append(jnp.dot(dgrp.reshape(nb * tm, POOL_GROUP).astype(BF16), wpbf[gi],
                            preferred_element_type=F32))
    y = jnp.concatenate(outs, axis=-1) * ps_ref[...]
    ya_ref[...] = y.astype(BF16)


def _pool_mixer(x2, g, mod, layer, w_in, w_pool, pool_scale, state, nseq, nb, tm, start):
    rows, d = x2.shape
    tiles_per_seq = (rows // nseq) // tm
    seq_blocks = nseq // nb
    c = POOL_WIDTH
    if state is None:
        st = jnp.zeros((nseq, POOL_HALO, c), F32)
    else:
        st = jnp.pad(state, ((0, 0), (POOL_HALO - POOL_STATE, 0), (0, 0)))
    kern = functools.partial(_pool_kernel, nb=nb, tm=tm, tiles_per_seq=tiles_per_seq, start=start)
    h2, ya, ns = pl.pallas_call(
        kern,
        grid=(seq_blocks * tiles_per_seq,),
        in_specs=[pl.BlockSpec((nb * tm, d), lambda i: (i, 0)),
                  pl.BlockSpec((1, d), lambda i: (0, 0)),
                  _mod_spec(mod, layer, 1), _mod_spec(mod, layer, 0),
                  pl.BlockSpec((d, c), lambda i: (0, 0)),
                  pl.BlockSpec((len(POOL_WINDOWS), POOL_GROUP, POOL_GROUP), lambda i: (0, 0, 0)),
                  pl.BlockSpec((1, c), lambda i: (0, 0)),
                  pl.BlockSpec((nb, POOL_HALO, c), lambda i: (i // tiles_per_seq, 0, 0))],
        out_specs=[pl.BlockSpec((nb * tm, d), lambda i: (i, 0)),
                   pl.BlockSpec((nb * tm, c), lambda i: (i, 0)),
                   pl.BlockSpec((nb, POOL_STATE, c), lambda i: (i // tiles_per_seq, 0, 0))],
        out_shape=[jax.ShapeDtypeStruct((rows, d), BF16),
                   jax.ShapeDtypeStruct((rows, c), BF16),
                   jax.ShapeDtypeStruct((nseq, POOL_STATE, c), F32)],
        scratch_shapes=[pltpu.VMEM((d, c), BF16),
                        pltpu.VMEM((len(POOL_WINDOWS), POOL_GROUP, POOL_GROUP), BF16),
                        pltpu.VMEM((nb, POOL_HALO, c), F32)],
        compiler_params=_cparams(1),
        name="pool_mixer",
    )(x2, g.reshape(1, d), mod, mod, w_in, w_pool, pool_scale.reshape(1, c), st)
    return h2, ya, ns


def _conv_kernel(h_ref, wx_ref, wb_ref, wc_ref, cw_ref, st_ref, yb_ref, ns_ref,
                 wxbf, wbbf, wcbf, carry, *, nb, tm, tiles_per_seq):
    i = pl.program_id(1)
    t = i % tiles_per_seq
    tc = wxbf.shape[1]
    halo = CONV_HALO

    @pl.when(i == 0)
    def _():
        wxbf[...] = wx_ref[...].astype(BF16)
        wbbf[...] = wb_ref[...].astype(BF16)
        wcbf[...] = wc_ref[...].astype(BF16)

    @pl.when(t == 0)
    def _():
        carry[...] = st_ref[...]

    h = h_ref[...]
    xin = jnp.dot(h, wxbf[...], preferred_element_type=F32)
    gb = jnp.dot(h, wbbf[...], preferred_element_type=F32)
    gc = jnp.dot(h, wcbf[...], preferred_element_type=F32)
    z3 = (gc * xin).reshape(nb, tm, tc)
    ext3 = jnp.concatenate([carry[...], z3], axis=1)
    tail = ext3[:, tm:tm + halo, :]
    ns_ref[...] = tail[:, halo - (CONV_K - 1):, :]
    carry[...] = tail
    ext = ext3.reshape(nb * (halo + tm), tc)
    cw = cw_ref[...]
    conv = cw[0:1, :] * pltpu.roll(ext, 2, 0) + cw[1:2, :] * pltpu.roll(ext, 1, 0) + cw[2:3, :] * ext
    conv = conv.reshape(nb, halo + tm, tc)[:, halo:, :].reshape(nb * tm, tc)
    yb_ref[...] = (gb * conv).astype(BF16)


def _conv_mixer(h2, w_in, conv_w, state, nseq, nb, tm):
    rows, d = h2.shape
    tiles_per_seq = (rows // nseq) // tm
    seq_blocks = nseq // nb
    c = CONV_WIDTH
    tc = CONV_COL_TILE
    cb = c // tc
    base = POOL_WIDTH // tc
    if state is None:
        st = jnp.zeros((nseq, CONV_HALO, c), F32)
    else:
        st = jnp.pad(state, ((0, 0), (CONV_HALO - (CONV_K - 1), 0), (0, 0)))
    kern = functools.partial(_conv_kernel, nb=nb, tm=tm, tiles_per_seq=tiles_per_seq)
    yb, ns = pl.pallas_call(
        kern,
        grid=(cb, seq_blocks * tiles_per_seq),
        in_specs=[pl.BlockSpec((nb * tm, d), lambda j, i: (i, 0)),
                  pl.BlockSpec((d, tc), lambda j, i: (0, base + j)),
                  pl.BlockSpec((d, tc), lambda j, i: (0, base + cb + j)),
                  pl.BlockSpec((d, tc), lambda j, i: (0, base + 2 * cb + j)),
                  pl.BlockSpec((CONV_K, tc), lambda j, i: (0, j)),
                  pl.BlockSpec((nb, CONV_HALO, tc), lambda j, i: (i // tiles_per_seq, 0, j))],
        out_specs=[pl.BlockSpec((nb * tm, tc), lambda j, i: (i, j)),
                   pl.BlockSpec((nb, CONV_K - 1, tc), lambda j, i: (i // tiles_per_seq, 0, j))],
        out_shape=[jax.ShapeDtypeStruct((rows, c), BF16),
                   jax.ShapeDtypeStruct((nseq, CONV_K - 1, c), F32)],
        scratch_shapes=[pltpu.VMEM((d, tc), BF16)] * 3 + [pltpu.VMEM((nb, CONV_HALO, tc), F32)],
        compiler_params=_cparams(2),
        name="conv_mixer",
    )(h2, w_in, w_in, w_in, conv_w, st)
    return yb, ns


def _pack_bf16_pairs(v):
    c = v.shape[1] // 2
    return pltpu.bitcast(pltpu.pack_elementwise([v[:, :c], v[:, c:]], packed_dtype=BF16), U32)


def _store_token_tiles(ref, v):
    rows = v.shape[0]
    for j in range(V7X_SUBLANES):
        ref[pl.ds(j, rows, stride=V7X_SUBLANES), :] = v[:, j * V7X_LANES:(j + 1) * V7X_LANES]


def _load_token_tiles(ref):
    rows = ref.shape[0] // V7X_SUBLANES
    return jnp.concatenate([ref[pl.ds(j, rows, stride=V7X_SUBLANES), :] for j in range(V7X_SUBLANES)],
                           axis=-1)


def _unpack_pairs_f32(w):
    return tuple(pltpu.unpack_elementwise(w, index=k, packed_dtype=BF16, unpacked_dtype=F32) for k in range(2))


def _unpack_bf16_pairs(w):
    lo, hi = _unpack_pairs_f32(w)
    return lo.astype(BF16), hi.astype(BF16)


def _outproj_kernel(ya_ref, yb_ref, x_ref, g1_ref, sc_ref, sh_ref, ng_ref, wo_ref,
                    x1_ref, hp_ref, hpt_ref, wobf, *, tiles_per_seq):
    i = pl.program_id(0)
    seq = i // tiles_per_seq

    @pl.when(i == 0)
    def _():
        wobf[...] = wo_ref[...].astype(BF16)

    ycat = jnp.concatenate([ya_ref[...], yb_ref[...]], axis=-1)
    y = jnp.dot(ycat, wobf[...], preferred_element_type=F32)
    x1 = x_ref[...] + _mod_rows(g1_ref, seq) * y
    x1_ref[...] = x1
    h2 = _rms(x1, ng_ref[...]) * (1.0 + _mod_rows(sc_ref, seq)) + _mod_rows(sh_ref, seq)
    packed = _pack_bf16_pairs(h2)
    hp_ref[...] = packed
    _store_token_tiles(hpt_ref, packed)


def _outproj(ya, yb, x2, mod, layer, ng, w_out, seq_rows):
    rows_all, d = x2.shape
    half = ya.shape[1]
    rt = ROW_TILE
    row_spec = lambda w: pl.BlockSpec((rt, w), lambda i: (i, 0))
    return pl.pallas_call(
        functools.partial(_outproj_kernel, tiles_per_seq=seq_rows // rt),
        grid=(rows_all // rt,),
        in_specs=[row_spec(half), row_spec(half), row_spec(d),
                  _mod_spec(mod, layer, 2), _mod_spec(mod, layer, 4), _mod_spec(mod, layer, 3),
                  pl.BlockSpec((1, d), lambda i: (0, 0)),
                  pl.BlockSpec((d, d), lambda i: (0, 0), pipeline_mode=pl.Buffered(1))],
        out_specs=[row_spec(d), row_spec(d // 2),
                   pl.BlockSpec((rt * V7X_SUBLANES, V7X_LANES), lambda i: (i, 0))],
        out_shape=[jax.ShapeDtypeStruct((rows_all, d), F32),
                   jax.ShapeDtypeStruct((rows_all, d // 2), U32),
                   jax.ShapeDtypeStruct((rows_all * V7X_SUBLANES, V7X_LANES), U32)],
        scratch_shapes=[pltpu.VMEM((d, d), BF16)],
        compiler_params=_cparams(1),
        name="outproj",
    )(ya, yb, x2, mod, mod, mod, ng.reshape(1, d), w_out)


def _router_kernel(hpp_ref, hps_ref, wr_ref, br_ref, pos_ref, rw_ref, tab_ref,
                   cnt_acc, totals, starts, padded, before_ref, s_all, sel_all, *, nt_p, rows_s):
    ph = pl.program_id(0)
    t = pl.program_id(1)
    last = nt_p
    r = hpp_ref.shape[0]
    half = hpp_ref.shape[1]
    ne = N_EXPERTS
    sub = lax.broadcasted_iota(I32, (ne, V7X_LANES), 0)

    @pl.when(t == 0)
    def _():
        cnt_acc[...] = jnp.zeros_like(cnt_acc)

    @pl.when((ph == 0) & (t == 0))
    def _():
        starts[...] = jnp.zeros_like(starts)
        padded[...] = jnp.zeros_like(padded)

    @pl.when((ph == 1) & (t == 0))
    def _():
        pad = jnp.floor((totals[...] + (MOE_TILE - 1.0)) * (1.0 / MOE_TILE)) * MOE_TILE
        run = pad
        k = 1
        while k < ne:
            run = run + jnp.where(sub >= k, pltpu.roll(run, k, 0), 0.0)
            k *= 2
        padded[...] = pad
        starts[...] = run - pad

    is_s = t == last
    eid = lax.broadcasted_iota(I32, (ne, r), 0)
    n_valid = jnp.where(is_s, rows_s, r)
    tok = lax.broadcasted_iota(I32, (ne, r), 1)

    @pl.when(ph == 0)
    def _():
        w_s = jnp.concatenate([hps_ref[...], jnp.zeros((r - rows_s, half), U32)], axis=0)
        w = jnp.where(is_s, w_s, hpp_ref[...])
        lo, hi = _unpack_bf16_pairs(w)
        wr = wr_ref[...].astype(BF16)
        log_t = (lax.dot_general(wr[:, :half], lo, NT_DIMS, preferred_element_type=F32)
                 + lax.dot_general(wr[:, half:], hi, NT_DIMS, preferred_element_type=F32))

        s = jax.nn.sigmoid(log_t)
        sg = s + br_ref[...]
        within = eid % EXP_PER_GROUP
        grp = eid // EXP_PER_GROUP

        def group_rot(x, k):
            return jnp.where(within + k < EXP_PER_GROUP,
                             pltpu.roll(x, ne - k, 0), pltpu.roll(x, EXP_PER_GROUP - k, 0))

        rank = jnp.zeros((ne, r), I32)
        for k in range(1, EXP_PER_GROUP):
            mate = group_rot(sg, k)
            wrapped = within + k >= EXP_PER_GROUP
            ahead = (mate > sg) | (wrapped & (mate == sg))
            rank = rank + ahead.astype(I32)
        top2 = rank < TOP_K
        kept = jnp.where(top2, sg, 0.0)
        gscore = kept
        for k in range(1, EXP_PER_GROUP):
            gscore = gscore + group_rot(kept, k)
        win = None
        for k in range(1, N_EXPERT_GROUPS):
            other = pltpu.roll(gscore, EXP_PER_GROUP * k, 0)
            beats = (gscore > other) | ((grp < k) & (gscore == other))
            win = beats if win is None else (win & beats)
        picked_now = top2 & win & (tok < n_valid)
        s_all[t] = s
        sel_all[t] = picked_now.astype(F32)

    s = s_all[t]
    selb = sel_all[t]
    sel = selb > 0.5
    cnt_before = cnt_acc[...]
    cnt_new = cnt_before + jnp.sum(selb, axis=1, keepdims=True)
    cnt_acc[...] = cnt_new

    @pl.when((ph == 0) & (t == 0))
    def _():
        src = lax.broadcasted_iota(I32, (r, r), 0)
        dst = lax.broadcasted_iota(I32, (r, r), 1)
        before_ref[...] = (src < dst).astype(BF16)

    @pl.when((ph == 0) & (t == last))
    def _():
        totals[...] = cnt_new

    @pl.when(ph == 1)
    def _():
        picked = jnp.where(sel, s, 0.0)
        wsum = jnp.sum(picked, axis=0, keepdims=True)
        gate = picked / jnp.where(tok[0:1, :] < n_valid, wsum, 1.0)
        ranks = jnp.dot(selb.astype(BF16), before_ref[...], preferred_element_type=F32)
        slot = (starts[...][:, 0:1] + cnt_before[:, 0:1] + ranks).astype(I32)
        e_a = jnp.min(jnp.where(sel, eid, ne), axis=0, keepdims=True)
        e_b = jnp.max(jnp.where(sel, eid, -1), axis=0, keepdims=True)
        is_a = sel & (eid == e_a)
        is_b = sel & (eid == e_b)
        pos_a = jnp.sum(jnp.where(is_a, slot, 0), axis=0, keepdims=True)
        pos_b = jnp.sum(jnp.where(is_b, slot, 0), axis=0, keepdims=True)
        w_a = jnp.sum(jnp.where(is_a, gate, 0.0), axis=0, keepdims=True)
        w_b = jnp.sum(jnp.where(is_b, gate, 0.0), axis=0, keepdims=True)
        pos_ref[0] = jnp.concatenate([pos_a, pos_b], axis=0)
        wmat = jnp.concatenate([w_a, w_b, jnp.zeros((V7X_LANES - 2, r), F32)], axis=0)
        rw_ref[...] = wmat.T

    @pl.when((ph == 1) & (t == last))
    def _():
        ends = starts[...] + padded[...]
        lane = lax.broadcasted_iota(I32, (ne, V7X_LANES), 1)
        tile_start = (lane * MOE_TILE).astype(F32)
        te = jnp.sum((tile_start >= ends).astype(I32), axis=0, keepdims=True)
        valid = te < ne
        last_e = jnp.max(jnp.where(padded[...] > 0.0, sub, 0), axis=0, keepdims=True)
        te = jnp.where(valid, te, last_e)
        n_used = jnp.sum(valid.astype(I32), axis=1, keepdims=True) + jnp.zeros((1, V7X_LANES), I32)
        last_tile = jnp.where(padded[...] > 0.0, ends - MOE_TILE, -1.0).astype(I32)
        last_tile_row = jnp.sum(jnp.where(sub == lane, last_tile, 0), axis=0, keepdims=True)
        later = jnp.min(jnp.where((sub > te) & (padded[...] > 0.0), sub, ne), axis=0, keepdims=True)
        next_e = jnp.where(later < ne, later, -1)
        zero = jnp.zeros((1, V7X_LANES), I32)
        tab_ref[...] = jnp.concatenate([te, valid.astype(I32), last_tile_row, n_used, next_e,
                                        zero, zero, zero], axis=0)


def _router(hp_p, hp_s, w_router, b_router):
    n_p, half = hp_p.shape
    rows_s = hp_s.shape[0]
    r = ROUTER_TILE
    nt_p = n_p // r
    nt = nt_p + 1
    kern = functools.partial(_router_kernel, nt_p=nt_p, rows_s=rows_s)
    pos, rw, tab = pl.pallas_call(
        kern,
        grid=(2, nt),
        in_specs=[pl.BlockSpec((r, half), lambda p, t: (jnp.minimum(t, nt_p - 1) * (1 - p), 0)),
                  pl.BlockSpec((rows_s, half), lambda p, t: (0, 0)),
                  pl.BlockSpec((N_EXPERTS, 2 * half), lambda p, t: (0, 0)),
                  pl.BlockSpec((N_EXPERTS, 1), lambda p, t: (0, 0))],
        out_specs=[pl.BlockSpec((1, TOP_K, r), lambda p, t: (p * t, 0, 0)),
                   pl.BlockSpec((r, V7X_LANES), lambda p, t: (p * t, 0)),
                   pl.BlockSpec((V7X_SUBLANES, V7X_LANES), lambda p, t: (0, 0))],
        out_shape=[jax.ShapeDtypeStruct((nt, TOP_K, r), I32),
                   jax.ShapeDtypeStruct((nt * r, V7X_LANES), F32),
                   jax.ShapeDtypeStruct((V7X_SUBLANES, V7X_LANES), I32)],
        scratch_shapes=[pltpu.VMEM((N_EXPERTS, V7X_LANES), F32)] * 4 + [pltpu.VMEM((r, r), BF16)]
        + [pltpu.VMEM((nt, N_EXPERTS, r), F32)] * 2,
        compiler_params=_cparams(2),
        name="router",
    )(hp_p, hp_s, w_router.T, b_router.reshape(N_EXPERTS, 1))
    return pos.reshape(-1), rw, tab.reshape(-1)


def _pos_index(tok0):
    return (tok0 // ROUTER_TILE) * (TOP_K * ROUTER_TILE) + tok0 % ROUTER_TILE


def _tokens(ref, first, n=1):
    start = pl.multiple_of(first * V7X_SUBLANES, V7X_SUBLANES)
    return ref.at[pl.ds(start, n * V7X_SUBLANES), :]


def _dispatch_kernel(pos_ref, tab_ref, hpp_ref, hps_ref, xs_ref, zbuf, sem, *, n_p_steps):
    i = pl.program_id(0)
    rows = hpp_ref.shape[0] // V7X_SUBLANES

    @pl.when(i == 0)
    def _():
        zbuf[...] = jnp.zeros_like(zbuf)

        def fill(e):
            first = pl.multiple_of(tab_ref[TAB_LAST_TILE * V7X_LANES + e], MOE_TILE)
            return pltpu.make_async_copy(zbuf, _tokens(xs_ref, first, MOE_TILE), sem)

        for e in range(N_EXPERTS):
            @pl.when(tab_ref[TAB_LAST_TILE * V7X_LANES + e] >= 0)
            def _():
                fill(e).start()
        for e in range(N_EXPERTS):
            @pl.when(tab_ref[TAB_LAST_TILE * V7X_LANES + e] >= 0)
            def _():
                fill(e).wait()

        def tail(j):
            first = pl.multiple_of(j * MOE_TILE, MOE_TILE)
            return pltpu.make_async_copy(zbuf, _tokens(xs_ref, first, MOE_TILE), sem)

        def tail_start(j, carry):
            tail(j).start()
            return carry

        def tail_wait(j, carry):
            tail(j).wait()
            return carry

        n_used = tab_ref[TAB_NUSED * V7X_LANES]
        n_tiles = xs_ref.shape[0] // (MOE_TILE * V7X_SUBLANES)
        lax.fori_loop(n_used, n_tiles, tail_start, 0)
        lax.fori_loop(n_used, n_tiles, tail_wait, 0)

    def scatter(src_ref, tok0):
        n = src_ref.shape[0] // V7X_SUBLANES
        base = _pos_index(tok0)

        def row_copy(r, dst):
            return pltpu.make_async_copy(_tokens(src_ref, r), _tokens(xs_ref, dst), sem)

        def issue(r, carry):
            row_copy(r, pos_ref[base + r]).start()
            row_copy(r, pos_ref[base + ROUTER_TILE + r]).start(priority=1)
            return carry

        lax.fori_loop(0, n, issue, 0, unroll=8)
        block = pltpu.make_async_copy(src_ref, _tokens(xs_ref, 0, n), sem)
        for _ in range(TOP_K):
            block.wait()

    @pl.when(i < n_p_steps)
    def _():
        scatter(hpp_ref, i * rows)

    @pl.when(i == n_p_steps)
    def _():
        scatter(hps_ref, n_p_steps * rows)


def _dispatch(pos, tab, hpt_p, hpt_s, n_rows_sorted):
    sub = V7X_SUBLANES
    n_p_steps = hpt_p.shape[0] // (ROUTER_TILE * sub)
    kern = functools.partial(_dispatch_kernel, n_p_steps=n_p_steps)
    blk = (ROUTER_TILE * sub, V7X_LANES)
    return pl.pallas_call(
        kern,
        grid_spec=pltpu.PrefetchScalarGridSpec(
            num_scalar_prefetch=2,
            grid=(n_p_steps + 1,),
            in_specs=[pl.BlockSpec(blk, lambda i, p, t: (jnp.minimum(i, n_p_steps - 1), 0)),
                      pl.BlockSpec(hpt_s.shape, lambda i, p, t: (0, 0))],
            out_specs=pl.BlockSpec(memory_space=pl.ANY),
            scratch_shapes=[pltpu.VMEM((MOE_TILE * sub, V7X_LANES), U32), pltpu.SemaphoreType.DMA(())]),
        out_shape=jax.ShapeDtypeStruct((n_rows_sorted * sub, V7X_LANES), U32),
        compiler_params=_cparams(1),
        name="moe_dispatch",
    )(pos, tab, hpt_p, hpt_s)


def _experts_kernel(tab_ref, xs_ref, wg_hbm, wu_hbm, wd_hbm, ys_ref,
                    wg32, wu32, wd32, wgbf, wubf, wdbf, slot_ref, sems, *, layer):
    i = pl.program_id(0)
    expert = tab_ref[TAB_EXPERT * V7X_LANES + i]
    prev = tab_ref[TAB_EXPERT * V7X_LANES + jnp.maximum(i - 1, 0)]
    upcoming = tab_ref[TAB_NEXT * V7X_LANES + i]
    changed = (i == 0) | (expert != prev)

    def weight_copies(e, slot):
        return (pltpu.make_async_copy(wg_hbm.at[layer, e], wg32.at[slot], sems.at[0, slot]),
                pltpu.make_async_copy(wu_hbm.at[layer, e], wu32.at[slot], sems.at[1, slot]),
                pltpu.make_async_copy(wd_hbm.at[layer, e], wd32.at[slot], sems.at[2, slot]))

    @pl.when(i == 0)
    def _():
        slot_ref[0] = 0
        for cp in weight_copies(expert, 0):
            cp.start()

    @pl.when(changed & (i > 0))
    def _():
        slot_ref[0] = 1 - slot_ref[0]

    def mlp(wg, wu, wd):
        x = jnp.concatenate(_unpack_bf16_pairs(_load_token_tiles(xs_ref)), axis=-1)
        a = jnp.dot(x, wg, preferred_element_type=F32)
        b = jnp.dot(x, wu, preferred_element_type=F32)
        hid = (a * jax.nn.sigmoid(a)) * b
        y = jnp.dot(hid.astype(BF16), wd, preferred_element_type=F32)
        _store_token_tiles(ys_ref, _pack_bf16_pairs(y))

    for slot in range(2):
        @pl.when(changed & (slot_ref[0] == slot))
        def _():
            for cp in weight_copies(expert, slot):
                cp.wait()

            @pl.when(upcoming >= 0)
            def _():
                for cp in weight_copies(upcoming, 1 - slot):
                    cp.start(priority=1)

            wg = wg32[slot].astype(BF16)
            wu = wu32[slot].astype(BF16)
            wd = wd32[slot].astype(BF16)
            wgbf[...] = wg
            wubf[...] = wu
            wdbf[...] = wd
            mlp(wg, wu, wd)

    valid = tab_ref[TAB_VALID * V7X_LANES + i] > 0

    @pl.when(valid & jnp.logical_not(changed))
    def _():
        mlp(wgbf[...], wubf[...], wdbf[...])

    @pl.when(jnp.logical_not(valid))
    def _():
        ys_ref[...] = jnp.zeros_like(ys_ref)


def _experts(tab, xs, w_gate, w_up, w_down, layer):
    sub = V7X_SUBLANES
    _, _, d, f = w_gate.shape
    nt = xs.shape[0] // (MOE_TILE * sub)
    assert nt <= V7X_LANES and d == 2 * sub * V7X_LANES
    blk = (MOE_TILE * sub, V7X_LANES)

    def tile(i, tab_ref):
        return jnp.minimum(i, tab_ref[TAB_NUSED * V7X_LANES] - 1)

    hbm = pl.BlockSpec(memory_space=pl.ANY)
    return pl.pallas_call(
        functools.partial(_experts_kernel, layer=layer),
        grid_spec=pltpu.PrefetchScalarGridSpec(
            num_scalar_prefetch=1,
            grid=(nt,),
            in_specs=[pl.BlockSpec(blk, lambda i, t: (tile(i, t), 0)), hbm, hbm, hbm],
            out_specs=pl.BlockSpec(blk, lambda i, t: (i, 0)),
            scratch_shapes=[pltpu.VMEM((2, d, f), F32), pltpu.VMEM((2, d, f), F32), pltpu.VMEM((2, f, d), F32),
                            pltpu.VMEM((d, f), BF16), pltpu.VMEM((d, f), BF16), pltpu.VMEM((f, d), BF16),
                            pltpu.SMEM((1,), I32), pltpu.SemaphoreType.DMA((3, 2))]),
        out_shape=jax.ShapeDtypeStruct(xs.shape, U32),
        compiler_params=_cparams(1),
        name="moe_experts",
    )(tab, xs, w_gate, w_up, w_down)


def _combine_kernel(pos_ref, ys_ref, rw_ref, ng_ref, *rest, tiles_per_seq, n_p_steps, final):
    n_in = 4
    n_out = 1 if final else 2
    trunk_in = (rest[:n_in], rest[n_in:2 * n_in])
    outs = rest[2 * n_in:2 * n_in + 2 * n_out]
    trunk_out = (outs[:n_out], outs[n_out:])
    buf0, buf1, sems = rest[2 * n_in + 2 * n_out:]
    bufs = (buf0, buf1)
    i = pl.program_id(0)
    n_steps = n_p_steps + 1
    rows = rw_ref.shape[0]

    def gather(step, slot):
        base = _pos_index(step * rows)

        def issue(r, carry):
            for k in range(TOP_K):
                src = pos_ref[base + k * ROUTER_TILE + r]
                pltpu.make_async_copy(_tokens(ys_ref, src), _tokens(bufs[slot].at[k], r),
                                      sems.at[slot]).start(priority=k)
            return carry

        lax.fori_loop(0, rows, issue, 0, unroll=8)

    def drain(slot):
        for k in range(TOP_K):
            pltpu.make_async_copy(_tokens(ys_ref, 0, rows), bufs[slot].at[k], sems.at[slot]).wait()

    def finish(slot, trunk, seq):
        x1_ref, g2_ref, sc_ref, sh_ref = trunk_in[trunk]
        rw = rw_ref[...]
        lo_a, hi_a = _unpack_pairs_f32(_load_token_tiles(bufs[slot].at[0]))
        lo_b, hi_b = _unpack_pairs_f32(_load_token_tiles(bufs[slot].at[1]))
        w_a = rw[:, 0:1]
        w_b = rw[:, 1:2]
        moe = jnp.concatenate([w_a * lo_a + w_b * lo_b, w_a * hi_a + w_b * hi_b], axis=-1)
        x2 = x1_ref[...] + _mod_rows(g2_ref, seq) * moe
        if final:
            trunk_out[trunk][0][...] = _rms(x2, ng_ref[...])
        else:
            trunk_out[trunk][0][...] = x2
            trunk_out[trunk][1][...] = (_rms(x2, ng_ref[...]) * (1.0 + _mod_rows(sc_ref, seq))
                                        + _mod_rows(sh_ref, seq)).astype(BF16)

    @pl.when(i == 0)
    def _():
        gather(0, 0)

    for slot in range(2):
        @pl.when(i % 2 == slot)
        def _():
            @pl.when(i + 1 < n_steps)
            def _():
                gather(i + 1, 1 - slot)

            drain(slot)

            @pl.when(i < n_p_steps)
            def _():
                finish(slot, 0, i // tiles_per_seq)

            @pl.when(i == n_p_steps)
            def _():
                finish(slot, 1, 0)


def _combine(pos, ys, x1_p, x1_s, rw, mod_p, mod_s, layer, ng, seq_rows_p, final):
    n_p, d = x1_p.shape
    rt = ROW_TILE
    assert x1_s.shape[0] == rt and n_p % rt == 0 and ROUTER_TILE % rt == 0
    n_p_steps = n_p // rt
    kern = functools.partial(_combine_kernel, tiles_per_seq=seq_rows_p // rt, n_p_steps=n_p_steps, final=final)
    p_spec = lambda w: pl.BlockSpec((rt, w), lambda i, p: (jnp.minimum(i, n_p_steps - 1), 0))
    s_spec = lambda w: pl.BlockSpec((rt, w), lambda i, p: (0, 0))
    nxt = min(layer + 1, mod_p.shape[0] - 1)

    def trunk_specs(spec, mod):
        return [spec(d), _mod_spec(mod, layer, 5), _mod_spec(mod, nxt, 1), _mod_spec(mod, nxt, 0)]

    out_dtypes = [F32] if final else [F32, BF16]
    out_shape = ([jax.ShapeDtypeStruct((n_p, d), t) for t in out_dtypes]
                 + [jax.ShapeDtypeStruct((rt, d), t) for t in out_dtypes])
    out_specs = [p_spec(d)] * len(out_dtypes) + [s_spec(d)] * len(out_dtypes)
    res = pl.pallas_call(
        kern,
        grid_spec=pltpu.PrefetchScalarGridSpec(
            num_scalar_prefetch=1,
            grid=(n_p_steps + 1,),
            in_specs=[pl.BlockSpec(memory_space=pl.ANY),
                      pl.BlockSpec((rt, V7X_LANES), lambda i, p: (i, 0)),
                      pl.BlockSpec((1, d), lambda i, p: (0, 0))]
            + trunk_specs(p_spec, mod_p) + trunk_specs(s_spec, mod_s),
            out_specs=out_specs,
            scratch_shapes=[pltpu.VMEM((TOP_K, rt * V7X_SUBLANES, V7X_LANES), U32)] * 2
            + [pltpu.SemaphoreType.DMA((2,))]),
        out_shape=out_shape,
        compiler_params=_cparams(1),
        name="moe_combine_final" if final else "moe_combine",
    )(pos, ys, rw, ng.reshape(1, d), x1_p, mod_p, mod_p, mod_p, x1_s, mod_s, mod_s, mod_s)
    return res[:len(out_dtypes)], res[len(out_dtypes):]


def _moe(out_p, out_s, mod_p, mod_s, layer, ng, seq_rows_p, w_router, b_router, w_gate, w_up, w_down, final):
    x1_p, hp_p, hpt_p = out_p
    x1_s, hp_s, hpt_s = out_s
    n_p = x1_p.shape[0]
    n_tok = n_p + x1_s.shape[0]
    max_rows = TOP_K * n_tok + N_EXPERTS * (MOE_TILE - 1)
    n_rows_sorted = -(-max_rows // MOE_TILE) * MOE_TILE
    pos, rw, tab = _router(hp_p, hp_s, w_router, b_router)
    xs = _dispatch(pos, tab, hpt_p, hpt_s, n_rows_sorted)
    ys = _experts(tab, xs, w_gate, w_up, w_down, layer)
    return _combine(pos, ys, x1_p, x1_s, rw, mod_p, mod_s, layer, ng, seq_rows_p, final)


def _gmlp_kernel(h_ref, w_ref, lg_ref, lb_ref, ws_ref, bs_ref, yc_ref, *rest, ell, blk, emit_v):
    if emit_v:
        gv_ref, wbf, wsbf = rest
    else:
        wbf, wsbf = rest
    i = pl.program_id(0)
    rows = h_ref.shape[0]
    c = GM_WIDTH

    @pl.when(i == 0)
    def _():
        wbf[...] = w_ref[...].astype(BF16)
        r = lax.broadcasted_iota(I32, (ell, ell), 0)
        s = lax.broadcasted_iota(I32, (ell, ell), 1)
        keep = (r >= s) & ((r // blk) == (s // blk))
        rsel = (lax.broadcasted_iota(I32, (ell, CHUNK), 0) % blk
                == lax.broadcasted_iota(I32, (ell, CHUNK), 1)).astype(BF16)
        csel = (lax.broadcasted_iota(I32, (CHUNK, ell), 1) % blk
                == lax.broadcasted_iota(I32, (CHUNK, ell), 0)).astype(BF16)
        for g in range(GM_GROUPS):
            wchunk = ws_ref[g].astype(BF16)
            if blk == ell:
                full = wchunk
            else:
                rowsp = jnp.dot(rsel, wchunk, preferred_element_type=F32).astype(BF16)
                full = jnp.dot(rowsp, csel, preferred_element_type=F32).astype(BF16)
            wsbf[g] = jnp.where(keep, full, jnp.zeros_like(full))

    uv = jnp.dot(h_ref[...], wbf[...], preferred_element_type=F32)
    u = uv[:, :c]
    v = uv[:, c:]
    vc = v - jnp.mean(v, axis=-1, keepdims=True)
    vn = vc * lax.rsqrt(jnp.mean(vc * vc, axis=-1, keepdims=True) + EPS) * lg_ref[...] + lb_ref[...]
    if emit_v:
        gv_ref[...] = vn
    vb = vn.astype(BF16)
    bs = bs_ref[...]
    for ch in range(rows // ell):
        rs = slice(ch * ell, (ch + 1) * ell)
        outs = []
        for g in range(GM_GROUPS):
            cs = slice(g * GM_GROUP, (g + 1) * GM_GROUP)
            mixed = jnp.dot(wsbf[g], vb[rs, cs], preferred_element_type=F32)
            mixed = (mixed.reshape(ell // blk, blk, GM_GROUP) + bs[:blk, g:g + 1][None]).reshape(ell, GM_GROUP)
            outs.append(u[rs, cs] * mixed)
        yc_ref[rs, :] = jnp.concatenate(outs, axis=-1).astype(BF16)


def _gmlp_mixer(h2, w_in, ln_g, ln_b, ws, bs_t, ell, blk, emit_v):
    rows, d = h2.shape
    c = GM_WIDTH
    kern = functools.partial(_gmlp_kernel, ell=ell, blk=blk, emit_v=emit_v)
    rt = min(rows, MATMUL_TILE)
    out_specs = [pl.BlockSpec((rt, c), lambda i: (i, 0))]
    out_shape = [jax.ShapeDtypeStruct((rows, c), BF16)]
    if emit_v:
        out_specs.append(pl.BlockSpec((rt, c), lambda i: (i, 0)))
        out_shape.append(jax.ShapeDtypeStruct((rows, c), F32))
    return pl.pallas_call(
        kern,
        grid=(rows // rt,),
        in_specs=[pl.BlockSpec((rt, d), lambda i: (i, 0)),
                  pl.BlockSpec((d, 2 * c), lambda i: (0, 0), pipeline_mode=pl.Buffered(1)),
                  pl.BlockSpec((1, c), lambda i: (0, 0)),
                  pl.BlockSpec((1, c), lambda i: (0, 0)),
                  pl.BlockSpec((GM_GROUPS, CHUNK, CHUNK), lambda i: (0, 0, 0)),
                  pl.BlockSpec((CHUNK, GM_GROUPS), lambda i: (0, 0))],
        out_specs=out_specs,
        out_shape=out_shape,
        scratch_shapes=[pltpu.VMEM((d, 2 * c), BF16), pltpu.VMEM((GM_GROUPS, ell, ell), BF16)],
        compiler_params=_cparams(1),
        name="gmlp_mixer",
    )(h2, w_in, ln_g.reshape(1, c), ln_b.reshape(1, c), ws, bs_t)


PAIR_W = 2 * HEAD_DIM
PAIRS_PER_KV = N_HEADS // N_KV // 2
NT_DIMS = (((1,), (1,)), ((), ()))
SCORE_SCALE = HEAD_DIM ** -0.5
assert float(np.log2(SCORE_SCALE)).is_integer()


def _swa_project(i, h_ref, wq_ref, wkv_ref, wbf):
    nq = N_HEADS * HEAD_DIM

    @pl.when(i == 0)
    def _():
        wbf[:, :nq] = wq_ref[...].astype(BF16)
        wbf[:, nq:] = wkv_ref[...].astype(BF16)

    return jnp.dot(h_ref[...], wbf[...], preferred_element_type=F32)


def _pair_block_diag(a, a_swapped, hk, axis):
    dim_axis = 1 - axis
    low = lax.broadcasted_iota(I32, a.shape, dim_axis) < HEAD_DIM
    lo, hi = (a, a_swapped) if hk == 0 else (a_swapped, a)
    return jnp.concatenate([jnp.where(low, lo, 0.0), jnp.where(low, 0.0, hi)], axis=axis).astype(BF16)


def _stack_pairs(qkv, rs, hk, scale=None):
    p0 = hk * PAIRS_PER_KV
    q = jnp.concatenate([qkv[rs, (p0 + pp) * PAIR_W:(p0 + pp + 1) * PAIR_W]
                         for pp in range(PAIRS_PER_KV)], axis=0)
    return (q if scale is None else q * scale).astype(BF16)


def _swa_cached_kernel(h_ref, wq_ref, wkv_ref, kp_ref, vp_ref, bias_ref, sink_ref, yd_ref, k_ref, v_ref,
                       wbf, *, tq):
    i = pl.program_id(0)
    rows = h_ref.shape[0]
    nq = N_HEADS * HEAD_DIM
    nkv = N_KV * HEAD_DIM
    n_blocks = rows // tq
    qkv = _swa_project(i, h_ref, wq_ref, wkv_ref, wbf)
    k_new = qkv[:, nq:nq + nkv]
    v_new = qkv[:, nq + nkv:]
    k_ref[...] = k_new
    v_ref[...] = v_new
    pad = jnp.zeros((WINDOW - tq, nkv), F32)

    scores, vbds = [], []
    for blk in range(n_blocks):
        rs = slice(blk * tq, (blk + 1) * tq)
        kcat = jnp.concatenate([kp_ref[blk], k_new[rs], pad], axis=0)
        vcat = jnp.concatenate([vp_ref[blk], v_new[rs], pad], axis=0)
        kswap = pltpu.roll(kcat, HEAD_DIM, 1)
        vswap = pltpu.roll(vcat, HEAD_DIM, 1)
        per_head = []
        for hk in range(N_KV):
            kbd = _pair_block_diag(kcat, kswap, hk, 0)
            vbds.append(_pair_block_diag(vcat, vswap, hk, 0))
            s4 = lax.dot_general(_stack_pairs(qkv, rs, hk), kbd, NT_DIMS,
                                 preferred_element_type=F32) * (HEAD_DIM ** -0.5)
            for pp in range(PAIRS_PER_KV):
                for sub in range(2):
                    per_head.append(s4[pp * tq:(pp + 1) * tq, sub * 2 * WINDOW:(sub + 1) * 2 * WINDOW])
        scores.append(jnp.concatenate(per_head, axis=0))

    s_all = jnp.stack(scores, axis=0) + bias_ref[...][None]
    sink = sink_ref[...][None]
    m = jnp.maximum(jnp.max(s_all, axis=-1, keepdims=True), sink)
    pr = jnp.exp(s_all - m)
    pr = pr / (jnp.sum(pr, axis=-1, keepdims=True) + jnp.exp(sink - m))

    for blk in range(n_blocks):
        outs = []
        for hk in range(N_KV):
            p4 = []
            for pp in range(PAIRS_PER_KV):
                h0 = 2 * (hk * PAIRS_PER_KV + pp)
                p4.append(jnp.concatenate([pr[blk, h0 * tq:(h0 + 1) * tq, :],
                                           pr[blk, (h0 + 1) * tq:(h0 + 2) * tq, :]], axis=-1))
            o4 = jnp.dot(jnp.concatenate(p4, axis=0).astype(BF16), vbds[blk * N_KV + hk],
                         preferred_element_type=F32)
            outs.extend(o4[pp * tq:(pp + 1) * tq, :] for pp in range(PAIRS_PER_KV))
        yd_ref[blk * tq:(blk + 1) * tq, :] = jnp.concatenate(outs, axis=-1).astype(BF16)


def _swa_stream_kernel(h_ref, wq_ref, wkv_ref, bias_ref, sink_ref, yd_ref, k_ref, v_ref,
                       wbf, kprev, vprev_t, *, blocks_per_seq):
    i = pl.program_id(0)
    rows = h_ref.shape[0]
    nq = N_HEADS * HEAD_DIM
    nkv = N_KV * HEAD_DIM
    tq = WINDOW
    n_blocks = rows // tq

    @pl.when(i == 0)
    def _():
        kprev[...] = jnp.zeros_like(kprev)
        vprev_t[...] = jnp.zeros_like(vprev_t)

    qkv = _swa_project(i, h_ref, wq_ref, wkv_ref, wbf)
    k_new = qkv[:, nq:nq + nkv]
    v_new = qkv[:, nq + nkv:]
    k_ref[...] = k_new
    v_ref[...] = v_new
    v_new_t = v_new.T
    lanes = PAIRS_PER_KV * tq

    for blk in range(n_blocks):
        rs = slice(blk * tq, (blk + 1) * tq)
        first = ((i * n_blocks + blk) % blocks_per_seq == 0).astype(I32)
        k_cur = k_new[rs]
        v_cur_t = v_new_t[:, rs]
        kcat = jnp.concatenate([kprev[...], k_cur], axis=0)
        vcat_t = jnp.concatenate([vprev_t[...], v_cur_t], axis=1)
        kprev[...] = k_cur
        vprev_t[...] = v_cur_t
        kswap = pltpu.roll(kcat, HEAD_DIM, 1)
        vswap_t = pltpu.roll(vcat_t, HEAD_DIM, 0)
        outs = []
        for hk in range(N_KV):
            kbd = _pair_block_diag(kcat, kswap, hk, 0)
            vbd_t = _pair_block_diag(vcat_t, vswap_t, hk, 1)
            st = lax.dot_general(kbd, _stack_pairs(qkv, rs, hk, SCORE_SCALE), NT_DIMS,
                                 preferred_element_type=F32)
            s3 = st.reshape(2, 2 * WINDOW, lanes) + bias_ref[first, hk]
            sink = sink_ref[hk]
            m = jnp.maximum(jnp.max(s3, axis=1, keepdims=True), sink)
            pr = jnp.exp(s3 - m)
            inv = 1.0 / (jnp.sum(pr, axis=1, keepdims=True) + jnp.exp(sink - m))
            o_t = jnp.dot(vbd_t, pr.reshape(4 * WINDOW, lanes).astype(BF16),
                          preferred_element_type=F32)
            norm = jnp.concatenate([jnp.broadcast_to(inv[sub], (HEAD_DIM, lanes)) for sub in range(2)], axis=0)
            o4 = (o_t * norm).T
            outs.extend(o4[pp * tq:(pp + 1) * tq, :] for pp in range(PAIRS_PER_KV))
        yd_ref[rs, :] = jnp.concatenate(outs, axis=-1).astype(BF16)


def _swa_weight_specs(w_in, d):
    nq = N_HEADS * HEAD_DIM
    nkv = N_KV * HEAD_DIM
    nw = nq + 2 * nkv
    q_blk = (w_in.shape[1] - nw) // nq
    kv_blk = (w_in.shape[1] - 2 * nkv) // (2 * nkv)
    assert q_blk * nq + nw == w_in.shape[1] and kv_blk * 2 * nkv + 2 * nkv == w_in.shape[1]
    return [pl.BlockSpec((d, nq), lambda i: (0, q_blk)), pl.BlockSpec((d, 2 * nkv), lambda i: (0, kv_blk))]


def _swa_outputs(rows, rt):
    nq = N_HEADS * HEAD_DIM
    nkv = N_KV * HEAD_DIM
    specs = [pl.BlockSpec((rt, nq), lambda i: (i, 0)),
             pl.BlockSpec((rt, nkv), lambda i: (i, 0)),
             pl.BlockSpec((rt, nkv), lambda i: (i, 0))]
    shapes = [jax.ShapeDtypeStruct((rows, nq), BF16),
              jax.ShapeDtypeStruct((rows, nkv), F32),
              jax.ShapeDtypeStruct((rows, nkv), F32)]
    return specs, shapes


def _swa_cached_mixer(h2, w_in, k_cache, v_cache, bias, sinks, tq):
    rows, d = h2.shape
    nkv = N_KV * HEAD_DIM
    nw = N_HEADS * HEAD_DIM + 2 * nkv
    n_blocks = ROW_TILE // tq
    cache_spec = pl.BlockSpec((n_blocks, WINDOW, nkv), lambda i: (i, 0, 0))
    out_specs, out_shape = _swa_outputs(rows, ROW_TILE)
    return pl.pallas_call(
        functools.partial(_swa_cached_kernel, tq=tq),
        grid=(rows // ROW_TILE,),
        in_specs=[pl.BlockSpec((ROW_TILE, d), lambda i: (i, 0))] + _swa_weight_specs(w_in, d)
        + [cache_spec, cache_spec,
           pl.BlockSpec((N_HEADS * tq, 2 * WINDOW), lambda i: (0, 0)),
           pl.BlockSpec((N_HEADS * tq, 1), lambda i: (0, 0))],
        out_specs=out_specs,
        out_shape=out_shape,
        scratch_shapes=[pltpu.VMEM((d, nw), BF16)],
        compiler_params=_cparams(1),
        name="swa_cached",
    )(h2, w_in, w_in, k_cache, v_cache, bias, sinks)


def _swa_stream_mixer(h2, w_in, bias_t, sinks_t, blocks_per_seq):
    rows, d = h2.shape
    nkv = N_KV * HEAD_DIM
    nw = N_HEADS * HEAD_DIM + 2 * nkv
    lanes = PAIRS_PER_KV * WINDOW
    rt = min(rows, MATMUL_TILE)
    out_specs, out_shape = _swa_outputs(rows, rt)
    return pl.pallas_call(
        functools.partial(_swa_stream_kernel, blocks_per_seq=blocks_per_seq),
        grid=(rows // rt,),
        in_specs=[pl.BlockSpec((rt, d), lambda i: (i, 0))] + _swa_weight_specs(w_in, d)
        + [pl.BlockSpec((2, N_KV, 2, 2 * WINDOW, lanes), lambda i: (0, 0, 0, 0, 0)),
           pl.BlockSpec((N_KV, 2, 1, lanes), lambda i: (0, 0, 0, 0))],
        out_specs=out_specs,
        out_shape=out_shape,
        scratch_shapes=[pltpu.VMEM((d, nw), BF16), pltpu.VMEM((WINDOW, nkv), F32),
                        pltpu.VMEM((nkv, WINDOW), F32)],
        compiler_params=_cparams(1),
        name="swa_stream",
    )(h2, w_in, w_in, bias_t, sinks_t)


def _t5_bucket(dist):
    max_exact = N_BUCKETS // 2
    dd = np.maximum(dist, 1)
    large = max_exact + (np.log(dd / max_exact) / np.log(WINDOW / max_exact)
                         * (N_BUCKETS - max_exact)).astype(np.int64)
    large = np.minimum(large, N_BUCKETS - 1)
    return np.where(dist < max_exact, dist, large).astype(np.int32)


def _attention_bias(rel_bias):
    by_dist = jnp.take(rel_bias.astype(F32), _t5_bucket(np.arange(WINDOW)), axis=0).T
    neg = jnp.full((N_HEADS, WINDOW), NEG_INF, F32)
    line = jnp.concatenate([neg, by_dist[:, ::-1], neg[:, :WINDOW - 1]], axis=1)
    rows = line[:, None, :]
    span = 1
    while span < WINDOW:
        rows = jnp.concatenate([rows[:, :, span:], rows[:, :, :rows.shape[2] - span]], axis=1)
        span *= 2
    return rows


def kernel(x_prompt, x_sample, state_pool, state_conv, cache_swa_k, cache_swa_v, c_prompt, c_sample, w_ada, b_ada, norm_g, final_norm_g, w_in_even, w_out_even, w_pool, pool_scale, conv_w, w_in_odd, w_out_odd, gm_norm_g, gm_norm_b, gm_w_s, gm_b_s, attn_sinks, rel_bias, w_router, b_router, w_gate, w_up, w_down):
    d = D_MODEL
    bp, tp, _ = x_prompt.shape
    bs, ts, _ = x_sample.shape
    rows_s = bs * ts
    assert rows_s == ROW_TILE and tp % ROUTER_TILE == 0 and PAST_LEN % CHUNK == 0
    assert bp <= V7X_SUBLANES and CHUNK % ts == 0

    n_c = bp + bs
    c_pad = (-n_c) % V7X_SUBLANES
    c_all = jnp.concatenate([c_prompt, c_sample, jnp.zeros((c_pad, d), F32)], axis=0)
    mod_p = _adaln(c_all, w_ada, b_ada)
    mod_s = jnp.repeat(mod_p[:, bp:bp + bs], ts, axis=1)

    xp = x_prompt.reshape(bp * tp, d)
    xs_ = x_sample.reshape(rows_s, d)
    w_in0, w_in1 = w_in_even[0], w_in_odd[0]

    hp0, ya_p, pool_p = _pool_mixer(xp, norm_g[0, 0], mod_p, 0, w_in0, w_pool[0], pool_scale[0],
                                    None, bp, 1, MATMUL_TILE, 0)
    hs0, ya_s, pool_s = _pool_mixer(xs_, norm_g[0, 0], mod_s, 0, w_in0, w_pool[0], pool_scale[0],
                                    state_pool[0], bs, bs, ts, PAST_LEN)
    yb_p, conv_p = _conv_mixer(hp0, w_in0, conv_w[0], None, bp, 1, MATMUL_TILE)
    yb_s, conv_s = _conv_mixer(hs0, w_in0, conv_w[0], state_conv[0], bs, bs, ts)
    out_p = _outproj(ya_p, yb_p, xp, mod_p, 0, norm_g[0, 1], w_out_even[0], tp)
    out_s = _outproj(ya_s, yb_s, xs_, mod_s, 0, norm_g[0, 1], w_out_even[0], rows_s)
    (x2p, h1p), (x2s, h1s) = _moe(out_p, out_s, mod_p, mod_s, 0, norm_g[1, 0], tp,
                                  w_router, b_router, w_gate, w_up, w_down, final=False)

    bs_t = gm_b_s[0].T
    (yc_p,) = _gmlp_mixer(h1p, w_in1, gm_norm_g[0], gm_norm_b[0], gm_w_s[0], bs_t, CHUNK, CHUNK, False)
    yc_s, gv_s = _gmlp_mixer(h1s, w_in1, gm_norm_g[0], gm_norm_b[0], gm_w_s[0], bs_t, rows_s, ts, True)
    bias = _attention_bias(rel_bias)
    nkv = N_KV * HEAD_DIM
    bias_t = jnp.transpose(bias.reshape(N_KV, PAIRS_PER_KV, 2, WINDOW, 2 * WINDOW), (0, 2, 4, 1, 3))
    bias_t = bias_t.reshape(N_KV, 2, 2 * WINDOW, PAIRS_PER_KV * WINDOW)
    before_start = (np.arange(2 * WINDOW) < WINDOW)[None, None, :, None]
    bias_t = jnp.stack([bias_t, jnp.where(before_start, NEG_INF, bias_t)], axis=0)
    sinks_t = jnp.transpose(attn_sinks[0].reshape(N_KV, PAIRS_PER_KV, 2), (0, 2, 1))
    sinks_t = jnp.repeat(sinks_t, WINDOW, axis=-1).reshape(N_KV, 2, 1, PAIRS_PER_KV * WINDOW)
    yd_p, k_p, v_p = _swa_stream_mixer(h1p, w_in1, bias_t, sinks_t, tp // WINDOW)
    yd_s, k_s, v_s = _swa_cached_mixer(h1s, w_in1, cache_swa_k[0].reshape(bs, WINDOW, nkv),
                                       cache_swa_v[0].reshape(bs, WINDOW, nkv),
                                       bias[:, :ts, :].reshape(N_HEADS * ts, 2 * WINDOW),
                                       jnp.repeat(attn_sinks[0], ts).reshape(-1, 1), ts)
    out_p = _outproj(yc_p, yd_p, x2p, mod_p, 1, norm_g[1, 1], w_out_odd[0], tp)
    out_s = _outproj(yc_s, yd_s, x2s, mod_s, 1, norm_g[1, 1], w_out_odd[0], rows_s)
    (yp,), (ys_out,) = _moe(out_p, out_s, mod_p, mod_s, 1, final_norm_g, tp,
                            w_router, b_router, w_gate, w_up, w_down, final=True)

    k_p4 = k_p.reshape(bp, tp, nkv)[:, -WINDOW:].reshape(bp, WINDOW, N_KV, HEAD_DIM)
    v_p4 = v_p.reshape(bp, tp, nkv)[:, -WINDOW:].reshape(bp, WINDOW, N_KV, HEAD_DIM)
    k_s4 = jnp.concatenate([cache_swa_k[0], k_s.reshape(bs, ts, N_KV, HEAD_DIM)], axis=1)[:, -WINDOW:]
    v_s4 = jnp.concatenate([cache_swa_v[0], v_s.reshape(bs, ts, N_KV, HEAD_DIM)], axis=1)[:, -WINDOW:]
    return (yp.reshape(bp, tp, d), ys_out.reshape(bs, ts, d),
            pool_p[None], pool_s[None], conv_p[None], conv_s[None],
            k_p4[None], k_s4[None], v_p4[None], v_s4[None],
            gv_s.reshape(bs, ts, GM_WIDTH)[None])
```

```python
import functools

import numpy as np
import jax
import jax.numpy as jnp
from jax import lax
from jax.experimental import pallas as pl
from jax.experimental.pallas import tpu as pltpu

F32 = jnp.float32
BF16 = jnp.bfloat16
I32 = jnp.int32
U32 = jnp.uint32

D_MODEL = 2048
POOL_WINDOWS = (2, 4, 8, 16)
POOL_WIDTH = 1024
POOL_GROUP = 256
POOL_STATE = 15
CONV_WIDTH = 1024
CONV_K = 3
GM_WIDTH = 1024
GM_GROUPS = 8
GM_GROUP = 128
CHUNK = 128
HEAD_DIM = 64
N_HEADS = 16
N_KV = 2
WINDOW = 128
N_BUCKETS = 32
N_EXPERTS = 16
N_EXPERT_GROUPS = 4
EXP_PER_GROUP = 4
TOP_K = 2
EPS = 1e-6
NEG_INF = -1e30
PAST_LEN = 16384

V7X_SUBLANES = 8
V7X_LANES = 128
VMEM_LIMIT = 56 * 1024 * 1024

ROW_TILE = 256
MATMUL_TILE = 512
ROUTER_TILE = 1024
POOL_HALO = 16
CONV_HALO = 8
MOE_TILE = 256
ADALN_COL_TILE = 1024
CONV_COL_TILE = 512
TAB_EXPERT, TAB_VALID, TAB_LAST_TILE, TAB_NUSED, TAB_NEXT = 0, 1, 2, 3, 4


def _cparams(n_axes):
    return pltpu.CompilerParams(dimension_semantics=("arbitrary",) * n_axes,
                                vmem_limit_bytes=VMEM_LIMIT)


def _rms(x, g):
    return x * lax.rsqrt(jnp.mean(x * x, axis=-1, keepdims=True) + EPS) * g


def _mod_spec(mod, layer, part):
    nrow = ROW_TILE if mod.shape[1] == ROW_TILE else V7X_SUBLANES
    return pl.BlockSpec((1, nrow, D_MODEL), lambda *_: (layer, 0, part))


def _mod_rows(m_ref, seq):
    if m_ref.shape[1] == V7X_SUBLANES:
        return m_ref[0, pl.ds(seq, 1), :]
    return m_ref[0]


def _adaln_kernel(c_ref, w_ref, b_ref, o_ref):
    c = c_ref[...]
    a = (c * jax.nn.sigmoid(c)).astype(BF16)
    o_ref[0] = jnp.dot(a, w_ref[0].astype(BF16), preferred_element_type=F32) + b_ref[0]


def _adaln(c_all, w_ada, b_ada):
    depth, d, n6 = w_ada.shape
    m = c_all.shape[0]
    tn = ADALN_COL_TILE
    return pl.pallas_call(
        _adaln_kernel,
        grid=(depth, n6 // tn),
        in_specs=[pl.BlockSpec((m, d), lambda l, j: (0, 0)),
                  pl.BlockSpec((1, d, tn), lambda l, j: (l, 0, j)),
                  pl.BlockSpec((1, 1, tn), lambda l, j: (l, 0, j))],
        out_specs=pl.BlockSpec((1, m, tn), lambda l, j: (l, 0, j)),
        out_shape=jax.ShapeDtypeStruct((depth, m, n6), F32),
        compiler_params=_cparams(2),
        name="adaln",
    )(c_all, w_ada, b_ada.reshape(depth, 1, n6))


def _pool_kernel(x_ref, g_ref, sc_ref, sh_ref, w_ref, wp_ref, ps_ref, st_ref, h_ref, ya_ref, ns_ref,
                 wbf, wpbf, carry, *, nb, tm, tiles_per_seq, start):
    i = pl.program_id(0)
    t = i % tiles_per_seq
    seq = i // tiles_per_seq
    c = POOL_WIDTH
    halo = POOL_HALO

    @pl.when(i == 0)
    def _():
        wbf[...] = w_ref[...].astype(BF16)
        wpbf[...] = wp_ref[...].astype(BF16)

    @pl.when(t == 0)
    def _():
        carry[...] = st_ref[...]

    h = (_rms(x_ref[...], g_ref[...]) * (1.0 + _mod_rows(sc_ref, seq)) + _mod_rows(sh_ref, seq)).astype(BF16)
    h_ref[...] = h
    p = jnp.dot(h, wbf[...], preferred_element_type=F32)
    p3 = p.reshape(nb, tm, c)
    ext3 = jnp.concatenate([carry[...], p3], axis=1)
    tail = ext3[:, tm:tm + halo, :]
    ns_ref[...] = tail[:, halo - POOL_STATE:, :]
    carry[...] = tail
    ext = ext3.reshape(nb * (halo + tm), c)
    pos = start + t * tm + lax.broadcasted_iota(I32, (1, tm, 1), 1)
    outs = []
    for gi, w in enumerate(POOL_WINDOWS):
        sl = slice(gi * POOL_GROUP, (gi + 1) * POOL_GROUP)
        acc = ext[:, sl]
        shift = 1
        while shift < w:
            acc = acc + pltpu.roll(acc, shift, 0)
            shift *= 2
        win = acc.reshape(nb, halo + tm, POOL_GROUP)[:, halo:, :]
        cnt = jnp.minimum(pos + 1, w).astype(F32)
        dgrp = win / cnt - p3[:, :, sl]
        outs.append(jnp.dot(dgrp.reshape(nb * tm, POOL_GROUP).astype(BF16), wpbf[gi],
                            preferred_element_type=F32))
    y = jnp.concatenate(outs, axis=-1) * ps_ref[...]
    ya_ref[...] = y.astype(BF16)


def _pool_mixer(x2, g, mod, layer, w_in, w_pool, pool_scale, state, nseq, nb, tm, start):
    rows, d = x2.shape
    tiles_per_seq = (rows // nseq) // tm
    seq_blocks = nseq // nb
    c = POOL_WIDTH
    if state is None:
        st = jnp.zeros((nseq, POOL_HALO, c), F32)
    else:
        st = jnp.pad(state, ((0, 0), (POOL_HALO - POOL_STATE, 0), (0, 0)))
    kern = functools.partial(_pool_kernel, nb=nb, tm=tm, tiles_per_seq=tiles_per_seq, start=start)
    h2, ya, ns = pl.pallas_call(
        kern,
        grid=(seq_blocks * tiles_per_seq,),
        in_specs=[pl.BlockSpec((nb * tm, d), lambda i: (i, 0)),
                  pl.BlockSpec((1, d), lambda i: (0, 0)),
                  _mod_spec(mod, layer, 1), _mod_spec(mod, layer, 0),
                  pl.BlockSpec((d, c), lambda i: (0, 0)),
                  pl.BlockSpec((len(POOL_WINDOWS), POOL_GROUP, POOL_GROUP), lambda i: (0, 0, 0)),
                  pl.BlockSpec((1, c), lambda i: (0, 0)),
                  pl.BlockSpec((nb, POOL_HALO, c), lambda i: (i // tiles_per_seq, 0, 0))],
        out_specs=[pl.BlockSpec((nb * tm, d), lambda i: (i, 0)),
                   pl.BlockSpec((nb * tm, c), lambda i: (i, 0)),
                   pl.BlockSpec((nb, POOL_STATE, c), lambda i: (i // tiles_per_seq, 0, 0))],
        out_shape=[jax.ShapeDtypeStruct((rows, d), BF16),
                   jax.ShapeDtypeStruct((rows, c), BF16),
                   jax.ShapeDtypeStruct((nseq, POOL_STATE, c), F32)],
        scratch_shapes=[pltpu.VMEM((d, c), BF16),
                        pltpu.VMEM((len(POOL_WINDOWS), POOL_GROUP, POOL_GROUP), BF16),
                        pltpu.VMEM((nb, POOL_HALO, c), F32)],
        compiler_params=_cparams(1),
        name="pool_mixer",
    )(x2, g.reshape(1, d), mod, mod, w_in, w_pool, pool_scale.reshape(1, c), st)
    return h2, ya, ns


def _conv_kernel(h_ref, wx_ref, wb_ref, wc_ref, cw_ref, st_ref, yb_ref, ns_ref,
                 wxbf, wbbf, wcbf, carry, *, nb, tm, tiles_per_seq):
    i = pl.program_id(1)
    t = i % tiles_per_seq
    tc = wxbf.shape[1]
    halo = CONV_HALO

    @pl.when(i == 0)
    def _():
        wxbf[...] = wx_ref[...].astype(BF16)
        wbbf[...] = wb_ref[...].astype(BF16)
        wcbf[...] = wc_ref[...].astype(BF16)

    @pl.when(t == 0)
    def _():
        carry[...] = st_ref[...]

    h = h_ref[...]
    xin = jnp.dot(h, wxbf[...], preferred_element_type=F32)
    gb = jnp.dot(h, wbbf[...], preferred_element_type=F32)
    gc = jnp.dot(h, wcbf[...], preferred_element_type=F32)
    z3 = (gc * xin).reshape(nb, tm, tc)
    ext3 = jnp.concatenate([carry[...], z3], axis=1)
    tail = ext3[:, tm:tm + halo, :]
    ns_ref[...] = tail[:, halo - (CONV_K - 1):, :]
    carry[...] = tail
    ext = ext3.reshape(nb * (halo + tm), tc)
    cw = cw_ref[...]
    conv = cw[0:1, :] * pltpu.roll(ext, 2, 0) + cw[1:2, :] * pltpu.roll(ext, 1, 0) + cw[2:3, :] * ext
    conv = conv.reshape(nb, halo + tm, tc)[:, halo:, :].reshape(nb * tm, tc)
    yb_ref[...] = (gb * conv).astype(BF16)


def _conv_mixer(h2, w_in, conv_w, state, nseq, nb, tm):
    rows, d = h2.shape
    tiles_per_seq = (rows // nseq) // tm
    seq_blocks = nseq // nb
    c = CONV_WIDTH
    tc = CONV_COL_TILE
    cb = c // tc
    base = POOL_WIDTH // tc
    if state is None:
        st = jnp.zeros((nseq, CONV_HALO, c), F32)
    else:
        st = jnp.pad(state, ((0, 0), (CONV_HALO - (CONV_K - 1), 0), (0, 0)))
    kern = functools.partial(_conv_kernel, nb=nb, tm=tm, tiles_per_seq=tiles_per_seq)
    yb, ns = pl.pallas_call(
        kern,
        grid=(cb, seq_blocks * tiles_per_seq),
        in_specs=[pl.BlockSpec((nb * tm, d), lambda j, i: (i, 0)),
                  pl.BlockSpec((d, tc), lambda j, i: (0, base + j)),
                  pl.BlockSpec((d, tc), lambda j, i: (0, base + cb + j)),
                  pl.BlockSpec((d, tc), lambda j, i: (0, base + 2 * cb + j)),
                  pl.BlockSpec((CONV_K, tc), lambda j, i: (0, j)),
                  pl.BlockSpec((nb, CONV_HALO, tc), lambda j, i: (i // tiles_per_seq, 0, j))],
        out_specs=[pl.BlockSpec((nb * tm, tc), lambda j, i: (i, j)),
                   pl.BlockSpec((nb, CONV_K - 1, tc), lambda j, i: (i // tiles_per_seq, 0, j))],
        out_shape=[jax.ShapeDtypeStruct((rows, c), BF16),
                   jax.ShapeDtypeStruct((nseq, CONV_K - 1, c), F32)],
        scratch_shapes=[pltpu.VMEM((d, tc), BF16)] * 3 + [pltpu.VMEM((nb, CONV_HALO, tc), F32)],
        compiler_params=_cparams(2),
        name="conv_mixer",
    )(h2, w_in, w_in, w_in, conv_w, st)
    return yb, ns


def _pack_bf16_pairs(v):
    c = v.shape[1] // 2
    return pltpu.bitcast(pltpu.pack_elementwise([v[:, :c], v[:, c:]], packed_dtype=BF16), U32)


def _store_token_tiles(ref, v):
    rows = v.shape[0]
    for j in range(V7X_SUBLANES):
        ref[pl.ds(j, rows, stride=V7X_SUBLANES), :] = v[:, j * V7X_LANES:(j + 1) * V7X_LANES]


def _load_token_tiles(ref):
    rows = ref.shape[0] // V7X_SUBLANES
    return jnp.concatenate([ref[pl.ds(j, rows, stride=V7X_SUBLANES), :] for j in range(V7X_SUBLANES)],
                           axis=-1)


def _unpack_pairs_f32(w):
    return tuple(pltpu.unpack_elementwise(w, index=k, packed_dtype=BF16, unpacked_dtype=F32) for k in range(2))


def _unpack_bf16_pairs(w):
    lo, hi = _unpack_pairs_f32(w)
    return lo.astype(BF16), hi.astype(BF16)


def _outproj_kernel(ya_ref, yb_ref, x_ref, g1_ref, sc_ref, sh_ref, ng_ref, wo_ref,
                    x1_ref, hp_ref, hpt_ref, wobf, *, tiles_per_seq):
    i = pl.program_id(0)
    seq = i // tiles_per_seq

    @pl.when(i == 0)
    def _():
        wobf[...] = wo_ref[...].astype(BF16)

    ycat = jnp.concatenate([ya_ref[...], yb_ref[...]], axis=-1)
    y = jnp.dot(ycat, wobf[...], preferred_element_type=F32)
    x1 = x_ref[...] + _mod_rows(g1_ref, seq) * y
    x1_ref[...] = x1
    h2 = _rms(x1, ng_ref[...]) * (1.0 + _mod_rows(sc_ref, seq)) + _mod_rows(sh_ref, seq)
    packed = _pack_bf16_pairs(h2)
    hp_ref[...] = packed
    _store_token_tiles(hpt_ref, packed)


def _outproj(ya, yb, x2, mod, layer, ng, w_out, seq_rows):
    rows_all, d = x2.shape
    half = ya.shape[1]
    rt = ROW_TILE
    row_spec = lambda w: pl.BlockSpec((rt, w), lambda i: (i, 0))
    return pl.pallas_call(
        functools.partial(_outproj_kernel, tiles_per_seq=seq_rows // rt),
        grid=(rows_all // rt,),
        in_specs=[row_spec(half), row_spec(half), row_spec(d),
                  _mod_spec(mod, layer, 2), _mod_spec(mod, layer, 4), _mod_spec(mod, layer, 3),
                  pl.BlockSpec((1, d), lambda i: (0, 0)),
                  pl.BlockSpec((d, d), lambda i: (0, 0), pipeline_mode=pl.Buffered(1))],
        out_specs=[row_spec(d), row_spec(d // 2),
                   pl.BlockSpec((rt * V7X_SUBLANES, V7X_LANES), lambda i: (i, 0))],
        out_shape=[jax.ShapeDtypeStruct((rows_all, d), F32),
                   jax.ShapeDtypeStruct((rows_all, d // 2), U32),
                   jax.ShapeDtypeStruct((rows_all * V7X_SUBLANES, V7X_LANES), U32)],
        scratch_shapes=[pltpu.VMEM((d, d), BF16)],
        compiler_params=_cparams(1),
        name="outproj",
    )(ya, yb, x2, mod, mod, mod, ng.reshape(1, d), w_out)


def _router_kernel(hpp_ref, hps_ref, wr_ref, br_ref, pos_ref, rw_ref, tab_ref,
                   cnt_acc, totals, starts, padded, before_ref, s_all, sel_all, *, nt_p, rows_s):
    ph = pl.program_id(0)
    t = pl.program_id(1)
    last = nt_p
    r = hpp_ref.shape[0]
    half = hpp_ref.shape[1]
    ne = N_EXPERTS
    sub = lax.broadcasted_iota(I32, (ne, V7X_LANES), 0)

    @pl.when(t == 0)
    def _():
        cnt_acc[...] = jnp.zeros_like(cnt_acc)

    @pl.when((ph == 0) & (t == 0))
    def _():
        starts[...] = jnp.zeros_like(starts)
        padded[...] = jnp.zeros_like(padded)

    @pl.when((ph == 1) & (t == 0))
    def _():
        pad = jnp.floor((totals[...] + (MOE_TILE - 1.0)) * (1.0 / MOE_TILE)) * MOE_TILE
        run = pad
        k = 1
        while k < ne:
            run = run + jnp.where(sub >= k, pltpu.roll(run, k, 0), 0.0)
            k *= 2
        padded[...] = pad
        starts[...] = run - pad

    is_s = t == last
    eid = lax.broadcasted_iota(I32, (ne, r), 0)
    n_valid = jnp.where(is_s, rows_s, r)
    tok = lax.broadcasted_iota(I32, (ne, r), 1)

    @pl.when(ph == 0)
    def _():
        w_s = jnp.concatenate([hps_ref[...], jnp.zeros((r - rows_s, half), U32)], axis=0)
        w = jnp.where(is_s, w_s, hpp_ref[...])
        lo, hi = _unpack_bf16_pairs(w)
        wr = wr_ref[...].astype(BF16)
        log_t = (lax.dot_general(wr[:, :half], lo, NT_DIMS, preferred_element_type=F32)
                 + lax.dot_general(wr[:, half:], hi, NT_DIMS, preferred_element_type=F32))

        s = jax.nn.sigmoid(log_t)
        sg = s + br_ref[...]
        within = eid % EXP_PER_GROUP
        grp = eid // EXP_PER_GROUP

        def group_rot(x, k):
            return jnp.where(within + k < EXP_PER_GROUP,
                             pltpu.roll(x, ne - k, 0), pltpu.roll(x, EXP_PER_GROUP - k, 0))

        rank = jnp.zeros((ne, r), I32)
        for k in range(1, EXP_PER_GROUP):
            mate = group_rot(sg, k)
            wrapped = within + k >= EXP_PER_GROUP
            ahead = (mate > sg) | (wrapped & (mate == sg))
            rank = rank + ahead.astype(I32)
        top2 = rank < TOP_K
        kept = jnp.where(top2, sg, 0.0)
        gscore = kept
        for k in range(1, EXP_PER_GROUP):
            gscore = gscore + group_rot(kept, k)
        win = None
        for k in range(1, N_EXPERT_GROUPS):
            other = pltpu.roll(gscore, EXP_PER_GROUP * k, 0)
            beats = (gscore > other) | ((grp < k) & (gscore == other))
            win = beats if win is None else (win & beats)
        picked_now = top2 & win & (tok < n_valid)
        s_all[t] = s
        sel_all[t] = picked_now.astype(F32)

    s = s_all[t]
    selb = sel_all[t]
    sel = selb > 0.5
    cnt_before = cnt_acc[...]
    cnt_new = cnt_before + jnp.sum(selb, axis=1, keepdims=True)
    cnt_acc[...] = cnt_new

    @pl.when((ph == 0) & (t == 0))
    def _():
        src = lax.broadcasted_iota(I32, (r, r), 0)
        dst = lax.broadcasted_iota(I32, (r, r), 1)
        before_ref[...] = (src < dst).astype(BF16)

    @pl.when((ph == 0) & (t == last))
    def _():
        totals[...] = cnt_new

    @pl.when(ph == 1)
    def _():
        picked = jnp.where(sel, s, 0.0)
        wsum = jnp.sum(picked, axis=0, keepdims=True)
        gate = picked / jnp.where(tok[0:1, :] < n_valid, wsum, 1.0)
        ranks = jnp.dot(selb.astype(BF16), before_ref[...], preferred_element_type=F32)
        slot = (starts[...][:, 0:1] + cnt_before[:, 0:1] + ranks).astype(I32)
        e_a = jnp.min(jnp.where(sel, eid, ne), axis=0, keepdims=True)
        e_b = jnp.max(jnp.where(sel, eid, -1), axis=0, keepdims=True)
        is_a = sel & (eid == e_a)
        is_b = sel & (eid == e_b)
        pos_a = jnp.sum(jnp.where(is_a, slot, 0), axis=0, keepdims=True)
        pos_b = jnp.sum(jnp.where(is_b, slot, 0), axis=0, keepdims=True)
        w_a = jnp.sum(jnp.where(is_a, gate, 0.0), axis=0, keepdims=True)
        w_b = jnp.sum(jnp.where(is_b, gate, 0.0), axis=0, keepdims=True)
        pos_ref[0] = jnp.concatenate([pos_a, pos_b], axis=0)
        wmat = jnp.concatenate([w_a, w_b, jnp.zeros((V7X_LANES - 2, r), F32)], axis=0)
        rw_ref[...] = wmat.T

    @pl.when((ph == 1) & (t == last))
    def _():
        ends = starts[...] + padded[...]
        lane = lax.broadcasted_iota(I32, (ne, V7X_LANES), 1)
        tile_start = (lane * MOE_TILE).astype(F32)
        te = jnp.sum((tile_start >= ends).astype(I32), axis=0, keepdims=True)
        valid = te < ne
        last_e = jnp.max(jnp.where(padded[...] > 0.0, sub, 0), axis=0, keepdims=True)
        te = jnp.where(valid, te, last_e)
        n_used = jnp.sum(valid.astype(I32), axis=1, keepdims=True) + jnp.zeros((1, V7X_LANES), I32)
        last_tile = jnp.where(padded[...] > 0.0, ends - MOE_TILE, -1.0).astype(I32)
        last_tile_row = jnp.sum(jnp.where(sub == lane, last_tile, 0), axis=0, keepdims=True)
        later = jnp.min(jnp.where((sub > te) & (padded[...] > 0.0), sub, ne), axis=0, keepdims=True)
        next_e = jnp.where(later < ne, later, -1)
        zero = jnp.zeros((1, V7X_LANES), I32)
        tab_ref[...] = jnp.concatenate([te, valid.astype(I32), last_tile_row, n_used, next_e,
                                        zero, zero, zero], axis=0)


def _router(hp_p, hp_s, w_router, b_router):
    n_p, half = hp_p.shape
    rows_s = hp_s.shape[0]
    r = ROUTER_TILE
    nt_p = n_p // r
    nt = nt_p + 1
    kern = functools.partial(_router_kernel, nt_p=nt_p, rows_s=rows_s)
    pos, rw, tab = pl.pallas_call(
        kern,
        grid=(2, nt),
        in_specs=[pl.BlockSpec((r, half), lambda p, t: (jnp.minimum(t, nt_p - 1) * (1 - p), 0)),
                  pl.BlockSpec((rows_s, half), lambda p, t: (0, 0)),
                  pl.BlockSpec((N_EXPERTS, 2 * half), lambda p, t: (0, 0)),
                  pl.BlockSpec((N_EXPERTS, 1), lambda p, t: (0, 0))],
        out_specs=[pl.BlockSpec((1, TOP_K, r), lambda p, t: (p * t, 0, 0)),
                   pl.BlockSpec((r, V7X_LANES), lambda p, t: (p * t, 0)),
                   pl.BlockSpec((V7X_SUBLANES, V7X_LANES), lambda p, t: (0, 0))],
        out_shape=[jax.ShapeDtypeStruct((nt, TOP_K, r), I32),
                   jax.ShapeDtypeStruct((nt * r, V7X_LANES), F32),
                   jax.ShapeDtypeStruct((V7X_SUBLANES, V7X_LANES), I32)],
        scratch_shapes=[pltpu.VMEM((N_EXPERTS, V7X_LANES), F32)] * 4 + [pltpu.VMEM((r, r), BF16)]
        + [pltpu.VMEM((nt, N_EXPERTS, r), F32)] * 2,
        compiler_params=_cparams(2),
        name="router",
    )(hp_p, hp_s, w_router.T, b_router.reshape(N_EXPERTS, 1))
    return pos.reshape(-1), rw, tab.reshape(-1)


def _pos_index(tok0):
    return (tok0 // ROUTER_TILE) * (TOP_K * ROUTER_TILE) + tok0 % ROUTER_TILE


def _tokens(ref, first, n=1):
    start = pl.multiple_of(first * V7X_SUBLANES, V7X_SUBLANES)
    return ref.at[pl.ds(start, n * V7X_SUBLANES), :]


def _dispatch_kernel(pos_ref, tab_ref, hpp_ref, hps_ref, xs_ref, zbuf, sem, *, n_p_steps):
    i = pl.program_id(0)
    rows = hpp_ref.shape[0] // V7X_SUBLANES

    @pl.when(i == 0)
    def _():
        zbuf[...] = jnp.zeros_like(zbuf)

        def fill(e):
            first = pl.multiple_of(tab_ref[TAB_LAST_TILE * V7X_LANES + e], MOE_TILE)
            return pltpu.make_async_copy(zbuf, _tokens(xs_ref, first, MOE_TILE), sem)

        for e in range(N_EXPERTS):
            @pl.when(tab_ref[TAB_LAST_TILE * V7X_LANES + e] >= 0)
            def _():
                fill(e).start()
        for e in range(N_EXPERTS):
            @pl.when(tab_ref[TAB_LAST_TILE * V7X_LANES + e] >= 0)
            def _():
                fill(e).wait()

        def tail(j):
            first = pl.multiple_of(j * MOE_TILE, MOE_TILE)
            return pltpu.make_async_copy(zbuf, _tokens(xs_ref, first, MOE_TILE), sem)

        def tail_start(j, carry):
            tail(j).start()
            return carry

        def tail_wait(j, carry):
            tail(j).wait()
            return carry

        n_used = tab_ref[TAB_NUSED * V7X_LANES]
        n_tiles = xs_ref.shape[0] // (MOE_TILE * V7X_SUBLANES)
        lax.fori_loop(n_used, n_tiles, tail_start, 0)
        lax.fori_loop(n_used, n_tiles, tail_wait, 0)

    def scatter(src_ref, tok0):
        n = src_ref.shape[0] // V7X_SUBLANES
        base = _pos_index(tok0)

        def row_copy(r, dst):
            return pltpu.make_async_copy(_tokens(src_ref, r), _tokens(xs_ref, dst), sem)

        def issue(r, carry):
            row_copy(r, pos_ref[base + r]).start()
            row_copy(r, pos_ref[base + ROUTER_TILE + r]).start(priority=1)
            return carry

        lax.fori_loop(0, n, issue, 0, unroll=8)
        block = pltpu.make_async_copy(src_ref, _tokens(xs_ref, 0, n), sem)
        for _ in range(TOP_K):
            block.wait()

    @pl.when(i < n_p_steps)
    def _():
        scatter(hpp_ref, i * rows)

    @pl.when(i == n_p_steps)
    def _():
        scatter(hps_ref, n_p_steps * rows)


def _dispatch(pos, tab, hpt_p, hpt_s, n_rows_sorted):
    sub = V7X_SUBLANES
    n_p_steps = hpt_p.shape[0] // (ROUTER_TILE * sub)
    kern = functools.partial(_dispatch_kernel, n_p_steps=n_p_steps)
    blk = (ROUTER_TILE * sub, V7X_LANES)
    return pl.pallas_call(
        kern,
        grid_spec=pltpu.PrefetchScalarGridSpec(
            num_scalar_prefetch=2,
            grid=(n_p_steps + 1,),
            in_specs=[pl.BlockSpec(blk, lambda i, p, t: (jnp.minimum(i, n_p_steps - 1), 0)),
                      pl.BlockSpec(hpt_s.shape, lambda i, p, t: (0, 0))],
            out_specs=pl.BlockSpec(memory_space=pl.ANY),
            scratch_shapes=[pltpu.VMEM((MOE_TILE * sub, V7X_LANES), U32), pltpu.SemaphoreType.DMA(())]),
        out_shape=jax.ShapeDtypeStruct((n_rows_sorted * sub, V7X_LANES), U32),
        compiler_params=_cparams(1),
        name="moe_dispatch",
    )(pos, tab, hpt_p, hpt_s)


def _experts_kernel(tab_ref, xs_ref, wg_hbm, wu_hbm, wd_hbm, ys_ref,
                    wg32, wu32, wd32, wgbf, wubf, wdbf, slot_ref, sems, *, layer):
    i = pl.program_id(0)
    expert = tab_ref[TAB_EXPERT * V7X_LANES + i]
    prev = tab_ref[TAB_EXPERT * V7X_LANES + jnp.maximum(i - 1, 0)]
    upcoming = tab_ref[TAB_NEXT * V7X_LANES + i]
    changed = (i == 0) | (expert != prev)

    def weight_copies(e, slot):
        return (pltpu.make_async_copy(wg_hbm.at[layer, e], wg32.at[slot], sems.at[0, slot]),
                pltpu.make_async_copy(wu_hbm.at[layer, e], wu32.at[slot], sems.at[1, slot]),
                pltpu.make_async_copy(wd_hbm.at[layer, e], wd32.at[slot], sems.at[2, slot]))

    @pl.when(i == 0)
    def _():
        slot_ref[0] = 0
        for cp in weight_copies(expert, 0):
            cp.start()

    @pl.when(changed & (i > 0))
    def _():
        slot_ref[0] = 1 - slot_ref[0]

    def mlp(wg, wu, wd):
        x = jnp.concatenate(_unpack_bf16_pairs(_load_token_tiles(xs_ref)), axis=-1)
        a = jnp.dot(x, wg, preferred_element_type=F32)
        b = jnp.dot(x, wu, preferred_element_type=F32)
        hid = (a * jax.nn.sigmoid(a)) * b
        y = jnp.dot(hid.astype(BF16), wd, preferred_element_type=F32)
        _store_token_tiles(ys_ref, _pack_bf16_pairs(y))

    for slot in range(2):
        @pl.when(changed & (slot_ref[0] == slot))
        def _():
            for cp in weight_copies(expert, slot):
                cp.wait()

            @pl.when(upcoming >= 0)
            def _():
                for n, cp in enumerate(weight_copies(upcoming, 1 - slot)):
                    cp.start(priority=n % 2)

            wg = wg32[slot].astype(BF16)
            wu = wu32[slot].astype(BF16)
            wd = wd32[slot].astype(BF16)
            wgbf[...] = wg
            wubf[...] = wu
            wdbf[...] = wd
            mlp(wg, wu, wd)

    valid = tab_ref[TAB_VALID * V7X_LANES + i] > 0

    @pl.when(valid & jnp.logical_not(changed))
    def _():
        mlp(wgbf[...], wubf[...], wdbf[...])

    @pl.when(jnp.logical_not(valid))
    def _():
        ys_ref[...] = jnp.zeros_like(ys_ref)


def _experts(tab, xs, w_gate, w_up, w_down, layer):
    sub = V7X_SUBLANES
    _, _, d, f = w_gate.shape
    nt = xs.shape[0] // (MOE_TILE * sub)
    assert nt <= V7X_LANES and d == 2 * sub * V7X_LANES
    blk = (MOE_TILE * sub, V7X_LANES)

    def tile(i, tab_ref):
        return jnp.minimum(i, tab_ref[TAB_NUSED * V7X_LANES] - 1)

    hbm = pl.BlockSpec(memory_space=pl.ANY)
    return pl.pallas_call(
        functools.partial(_experts_kernel, layer=layer),
        grid_spec=pltpu.PrefetchScalarGridSpec(
            num_scalar_prefetch=1,
            grid=(nt,),
            in_specs=[pl.BlockSpec(blk, lambda i, t: (tile(i, t), 0)), hbm, hbm, hbm],
            out_specs=pl.BlockSpec(blk, lambda i, t: (i, 0)),
            scratch_shapes=[pltpu.VMEM((2, d, f), F32), pltpu.VMEM((2, d, f), F32), pltpu.VMEM((2, f, d), F32),
                            pltpu.VMEM((d, f), BF16), pltpu.VMEM((d, f), BF16), pltpu.VMEM((f, d), BF16),
                            pltpu.SMEM((1,), I32), pltpu.SemaphoreType.DMA((3, 2))]),
        out_shape=jax.ShapeDtypeStruct(xs.shape, U32),
        compiler_params=_cparams(1),
        name="moe_experts",
    )(tab, xs, w_gate, w_up, w_down)


def _combine_kernel(pos_ref, ys_ref, rw_ref, ng_ref, *rest, tiles_per_seq, n_p_steps, final):
    n_in = 4
    n_out = 1 if final else 2
    trunk_in = (rest[:n_in], rest[n_in:2 * n_in])
    outs = rest[2 * n_in:2 * n_in + 2 * n_out]
    trunk_out = (outs[:n_out], outs[n_out:])
    buf0, buf1, sems = rest[2 * n_in + 2 * n_out:]
    bufs = (buf0, buf1)
    i = pl.program_id(0)
    n_steps = n_p_steps + 1
    rows = rw_ref.shape[0]

    def gather(step, slot):
        base = _pos_index(step * rows)

        def issue(r, carry):
            for k in range(TOP_K):
                src = pos_ref[base + k * ROUTER_TILE + r]
                pltpu.make_async_copy(_tokens(ys_ref, src), _tokens(bufs[slot].at[k], r),
                                      sems.at[slot]).start(priority=k)
            return carry

        lax.fori_loop(0, rows, issue, 0, unroll=8)

    def drain(slot):
        for k in range(TOP_K):
            pltpu.make_async_copy(_tokens(ys_ref, 0, rows), bufs[slot].at[k], sems.at[slot]).wait()

    def finish(slot, trunk, seq):
        x1_ref, g2_ref, sc_ref, sh_ref = trunk_in[trunk]
        rw = rw_ref[...]
        lo_a, hi_a = _unpack_pairs_f32(_load_token_tiles(bufs[slot].at[0]))
        lo_b, hi_b = _unpack_pairs_f32(_load_token_tiles(bufs[slot].at[1]))
        w_a = rw[:, 0:1]
        w_b = rw[:, 1:2]
        moe = jnp.concatenate([w_a * lo_a + w_b * lo_b, w_a * hi_a + w_b * hi_b], axis=-1)
        x2 = x1_ref[...] + _mod_rows(g2_ref, seq) * moe
        if final:
            trunk_out[trunk][0][...] = _rms(x2, ng_ref[...])
        else:
            trunk_out[trunk][0][...] = x2
            trunk_out[trunk][1][...] = (_rms(x2, ng_ref[...]) * (1.0 + _mod_rows(sc_ref, seq))
                                        + _mod_rows(sh_ref, seq)).astype(BF16)

    @pl.when(i == 0)
    def _():
        gather(0, 0)

    for slot in range(2):
        @pl.when(i % 2 == slot)
        def _():
            @pl.when(i + 1 < n_steps)
            def _():
                gather(i + 1, 1 - slot)

            drain(slot)

            @pl.when(i < n_p_steps)
            def _():
                finish(slot, 0, i // tiles_per_seq)

            @pl.when(i == n_p_steps)
            def _():
                finish(slot, 1, 0)


def _combine(pos, ys, x1_p, x1_s, rw, mod_p, mod_s, layer, ng, seq_rows_p, final):
    n_p, d = x1_p.shape
    rt = ROW_TILE
    assert x1_s.shape[0] == rt and n_p % rt == 0 and ROUTER_TILE % rt == 0
    n_p_steps = n_p // rt
    kern = functools.partial(_combine_kernel, tiles_per_seq=seq_rows_p // rt, n_p_steps=n_p_steps, final=final)
    p_spec = lambda w: pl.BlockSpec((rt, w), lambda i, p: (jnp.minimum(i, n_p_steps - 1), 0))
    s_spec = lambda w: pl.BlockSpec((rt, w), lambda i, p: (0, 0))
    nxt = min(layer + 1, mod_p.shape[0] - 1)

    def trunk_specs(spec, mod):
        return [spec(d), _mod_spec(mod, layer, 5), _mod_spec(mod, nxt, 1), _mod_spec(mod, nxt, 0)]

    out_dtypes = [F32] if final else [F32, BF16]
    out_shape = ([jax.ShapeDtypeStruct((n_p, d), t) for t in out_dtypes]
                 + [jax.ShapeDtypeStruct((rt, d), t) for t in out_dtypes])
    out_specs = [p_spec(d)] * len(out_dtypes) + [s_spec(d)] * len(out_dtypes)
    res = pl.pallas_call(
        kern,
        grid_spec=pltpu.PrefetchScalarGridSpec(
            num_scalar_prefetch=1,
            grid=(n_p_steps + 1,),
            in_specs=[pl.BlockSpec(memory_space=pl.ANY),
                      pl.BlockSpec((rt, V7X_LANES), lambda i, p: (i, 0)),
                      pl.BlockSpec((1, d), lambda i, p: (0, 0))]
            + trunk_specs(p_spec, mod_p) + trunk_specs(s_spec, mod_s),
            out_specs=out_specs,
            scratch_shapes=[pltpu.VMEM((TOP_K, rt * V7X_SUBLANES, V7X_LANES), U32)] * 2
            + [pltpu.SemaphoreType.DMA((2,))]),
        out_shape=out_shape,
        compiler_params=_cparams(1),
        name="moe_combine_final" if final else "moe_combine",
    )(pos, ys, rw, ng.reshape(1, d), x1_p, mod_p, mod_p, mod_p, x1_s, mod_s, mod_s, mod_s)
    return res[:len(out_dtypes)], res[len(out_dtypes):]


def _moe(out_p, out_s, mod_p, mod_s, layer, ng, seq_rows_p, w_router, b_router, w_gate, w_up, w_down, final):
    x1_p, hp_p, hpt_p = out_p
    x1_s, hp_s, hpt_s = out_s
    n_p = x1_p.shape[0]
    n_tok = n_p + x1_s.shape[0]
    max_rows = TOP_K * n_tok + N_EXPERTS * (MOE_TILE - 1)
    n_rows_sorted = -(-max_rows // MOE_TILE) * MOE_TILE
    pos, rw, tab = _router(hp_p, hp_s, w_router, b_router)
    xs = _dispatch(pos, tab, hpt_p, hpt_s, n_rows_sorted)
    ys = _experts(tab, xs, w_gate, w_up, w_down, layer)
    return _combine(pos, ys, x1_p, x1_s, rw, mod_p, mod_s, layer, ng, seq_rows_p, final)


def _gmlp_kernel(h_ref, w_ref, lg_ref, lb_ref, ws_ref, bs_ref, yc_ref, *rest, ell, blk, emit_v):
    if emit_v:
        gv_ref, wbf, wsbf = rest
    else:
        wbf, wsbf = rest
    i = pl.program_id(0)
    rows = h_ref.shape[0]
    c = GM_WIDTH

    @pl.when(i == 0)
    def _():
        wbf[...] = w_ref[...].astype(BF16)
        r = lax.broadcasted_iota(I32, (ell, ell), 0)
        s = lax.broadcasted_iota(I32, (ell, ell), 1)
        keep = (r >= s) & ((r // blk) == (s // blk))
        rsel = (lax.broadcasted_iota(I32, (ell, CHUNK), 0) % blk
                == lax.broadcasted_iota(I32, (ell, CHUNK), 1)).astype(BF16)
        csel = (lax.broadcasted_iota(I32, (CHUNK, ell), 1) % blk
                == lax.broadcasted_iota(I32, (CHUNK, ell), 0)).astype(BF16)
        for g in range(GM_GROUPS):
            wchunk = ws_ref[g].astype(BF16)
            if blk == ell:
                full = wchunk
            else:
                rowsp = jnp.dot(rsel, wchunk, preferred_element_type=F32).astype(BF16)
                full = jnp.dot(rowsp, csel, preferred_element_type=F32).astype(BF16)
            wsbf[g] = jnp.where(keep, full, jnp.zeros_like(full))

    uv = jnp.dot(h_ref[...], wbf[...], preferred_element_type=F32)
    u = uv[:, :c]
    v = uv[:, c:]
    vc = v - jnp.mean(v, axis=-1, keepdims=True)
    vn = vc * lax.rsqrt(jnp.mean(vc * vc, axis=-1, keepdims=True) + EPS) * lg_ref[...] + lb_ref[...]
    if emit_v:
        gv_ref[...] = vn
    vb = vn.astype(BF16)
    bs = bs_ref[...]
    for ch in range(rows // ell):
        rs = slice(ch * ell, (ch + 1) * ell)
        outs = []
        for g in range(GM_GROUPS):
            cs = slice(g * GM_GROUP, (g + 1) * GM_GROUP)
            mixed = jnp.dot(wsbf[g], vb[rs, cs], preferred_element_type=F32)
            mixed = (mixed.reshape(ell // blk, blk, GM_GROUP) + bs[:blk, g:g + 1][None]).reshape(ell, GM_GROUP)
            outs.append(u[rs, cs] * mixed)
        yc_ref[rs, :] = jnp.concatenate(outs, axis=-1).astype(BF16)


def _gmlp_mixer(h2, w_in, ln_g, ln_b, ws, bs_t, ell, blk, emit_v):
    rows, d = h2.shape
    c = GM_WIDTH
    kern = functools.partial(_gmlp_kernel, ell=ell, blk=blk, emit_v=emit_v)
    rt = min(rows, MATMUL_TILE)
    out_specs = [pl.BlockSpec((rt, c), lambda i: (i, 0))]
    out_shape = [jax.ShapeDtypeStruct((rows, c), BF16)]
    if emit_v:
        out_specs.append(pl.BlockSpec((rt, c), lambda i: (i, 0)))
        out_shape.append(jax.ShapeDtypeStruct((rows, c), F32))
    return pl.pallas_call(
        kern,
        grid=(rows // rt,),
        in_specs=[pl.BlockSpec((rt, d), lambda i: (i, 0)),
                  pl.BlockSpec((d, 2 * c), lambda i: (0, 0), pipeline_mode=pl.Buffered(1)),
                  pl.BlockSpec((1, c), lambda i: (0, 0)),
                  pl.BlockSpec((1, c), lambda i: (0, 0)),
                  pl.BlockSpec((GM_GROUPS, CHUNK, CHUNK), lambda i: (0, 0, 0)),
                  pl.BlockSpec((CHUNK, GM_GROUPS), lambda i: (0, 0))],
        out_specs=out_specs,
        out_shape=out_shape,
        scratch_shapes=[pltpu.VMEM((d, 2 * c), BF16), pltpu.VMEM((GM_GROUPS, ell, ell), BF16)],
        compiler_params=_cparams(1),
        name="gmlp_mixer",
    )(h2, w_in, ln_g.reshape(1, c), ln_b.reshape(1, c), ws, bs_t)


PAIR_W = 2 * HEAD_DIM
PAIRS_PER_KV = N_HEADS // N_KV // 2
NT_DIMS = (((1,), (1,)), ((), ()))
SCORE_SCALE = HEAD_DIM ** -0.5
assert float(np.log2(SCORE_SCALE)).is_integer()


def _swa_project(i, h_ref, wq_ref, wkv_ref, wbf):
    nq = N_HEADS * HEAD_DIM

    @pl.when(i == 0)
    def _():
        wbf[:, :nq] = wq_ref[...].astype(BF16)
        wbf[:, nq:] = wkv_ref[...].astype(BF16)

    return jnp.dot(h_ref[...], wbf[...], preferred_element_type=F32)


def _pair_block_diag(a, a_swapped, hk, axis):
    dim_axis = 1 - axis
    low = lax.broadcasted_iota(I32, a.shape, dim_axis) < HEAD_DIM
    lo, hi = (a, a_swapped) if hk == 0 else (a_swapped, a)
    return jnp.concatenate([jnp.where(low, lo, 0.0), jnp.where(low, 0.0, hi)], axis=axis).astype(BF16)


def _stack_pairs(qkv, rs, hk, scale=None):
    p0 = hk * PAIRS_PER_KV
    q = jnp.concatenate([qkv[rs, (p0 + pp) * PAIR_W:(p0 + pp + 1) * PAIR_W]
                         for pp in range(PAIRS_PER_KV)], axis=0)
    return (q if scale is None else q * scale).astype(BF16)


def _swa_cached_kernel(h_ref, wq_ref, wkv_ref, kp_ref, vp_ref, bias_ref, sink_ref, yd_ref, k_ref, v_ref,
                       wbf, *, tq):
    i = pl.program_id(0)
    rows = h_ref.shape[0]
    nq = N_HEADS * HEAD_DIM
    nkv = N_KV * HEAD_DIM
    n_blocks = rows // tq
    qkv = _swa_project(i, h_ref, wq_ref, wkv_ref, wbf)
    k_new = qkv[:, nq:nq + nkv]
    v_new = qkv[:, nq + nkv:]
    k_ref[...] = k_new
    v_ref[...] = v_new
    pad = jnp.zeros((WINDOW - tq, nkv), F32)

    scores, vbds = [], []
    for blk in range(n_blocks):
        rs = slice(blk * tq, (blk + 1) * tq)
        kcat = jnp.concatenate([kp_ref[blk], k_new[rs], pad], axis=0)
        vcat = jnp.concatenate([vp_ref[blk], v_new[rs], pad], axis=0)
        kswap = pltpu.roll(kcat, HEAD_DIM, 1)
        vswap = pltpu.roll(vcat, HEAD_DIM, 1)
        per_head = []
        for hk in range(N_KV):
            kbd = _pair_block_diag(kcat, kswap, hk, 0)
            vbds.append(_pair_block_diag(vcat, vswap, hk, 0))
            s4 = lax.dot_general(_stack_pairs(qkv, rs, hk), kbd, NT_DIMS,
                                 preferred_element_type=F32) * (HEAD_DIM ** -0.5)
            for pp in range(PAIRS_PER_KV):
                for sub in range(2):
                    per_head.append(s4[pp * tq:(pp + 1) * tq, sub * 2 * WINDOW:(sub + 1) * 2 * WINDOW])
        scores.append(jnp.concatenate(per_head, axis=0))

    s_all = jnp.stack(scores, axis=0) + bias_ref[...][None]
    sink = sink_ref[...][None]
    m = jnp.maximum(jnp.max(s_all, axis=-1, keepdims=True), sink)
    pr = jnp.exp(s_all - m)
    pr = pr / (jnp.sum(pr, axis=-1, keepdims=True) + jnp.exp(sink - m))

    for blk in range(n_blocks):
        outs = []
        for hk in range(N_KV):
            p4 = []
            for pp in range(PAIRS_PER_KV):
                h0 = 2 * (hk * PAIRS_PER_KV + pp)
                p4.append(jnp.concatenate([pr[blk, h0 * tq:(h0 + 1) * tq, :],
                                           pr[blk, (h0 + 1) * tq:(h0 + 2) * tq, :]], axis=-1))
            o4 = jnp.dot(jnp.concatenate(p4, axis=0).astype(BF16), vbds[blk * N_KV + hk],
                         preferred_element_type=F32)
            outs.extend(o4[pp * tq:(pp + 1) * tq, :] for pp in range(PAIRS_PER_KV))
        yd_ref[blk * tq:(blk + 1) * tq, :] = jnp.concatenate(outs, axis=-1).astype(BF16)


def _swa_stream_kernel(h_ref, wq_ref, wkv_ref, bias_ref, sink_ref, yd_ref, k_ref, v_ref,
                       wbf, kprev, vprev_t, *, blocks_per_seq):
    i = pl.program_id(0)
    rows = h_ref.shape[0]
    nq = N_HEADS * HEAD_DIM
    nkv = N_KV * HEAD_DIM
    tq = WINDOW
    n_blocks = rows // tq

    @pl.when(i == 0)
    def _():
        kprev[...] = jnp.zeros_like(kprev)
        vprev_t[...] = jnp.zeros_like(vprev_t)

    qkv = _swa_project(i, h_ref, wq_ref, wkv_ref, wbf)
    k_new = qkv[:, nq:nq + nkv]
    v_new = qkv[:, nq + nkv:]
    k_ref[...] = k_new
    v_ref[...] = v_new
    v_new_t = v_new.T
    lanes = PAIRS_PER_KV * tq

    for blk in range(n_blocks):
        rs = slice(blk * tq, (blk + 1) * tq)
        first = ((i * n_blocks + blk) % blocks_per_seq == 0).astype(I32)
        k_cur = k_new[rs]
        v_cur_t = v_new_t[:, rs]
        kcat = jnp.concatenate([kprev[...], k_cur], axis=0)
        vcat_t = jnp.concatenate([vprev_t[...], v_cur_t], axis=1)
        kprev[...] = k_cur
        vprev_t[...] = v_cur_t
        kswap = pltpu.roll(kcat, HEAD_DIM, 1)
        vswap_t = pltpu.roll(vcat_t, HEAD_DIM, 0)
        outs = []
        for hk in range(N_KV):
            kbd = _pair_block_diag(kcat, kswap, hk, 0)
            vbd_t = _pair_block_diag(vcat_t, vswap_t, hk, 1)
            st = lax.dot_general(kbd, _stack_pairs(qkv, rs, hk, SCORE_SCALE), NT_DIMS,
                                 preferred_element_type=F32)
            s3 = st.reshape(2, 2 * WINDOW, lanes) + bias_ref[first, hk]
            sink = sink_ref[hk]
            m = jnp.maximum(jnp.max(s3, axis=1, keepdims=True), sink)
            pr = jnp.exp(s3 - m)
            inv = 1.0 / (jnp.sum(pr, axis=1, keepdims=True) + jnp.exp(sink - m))
            o_t = jnp.dot(vbd_t, pr.reshape(4 * WINDOW, lanes).astype(BF16),
                          preferred_element_type=F32)
            norm = jnp.concatenate([jnp.broadcast_to(inv[sub], (HEAD_DIM, lanes)) for sub in range(2)], axis=0)
            o4 = (o_t * norm).T
            outs.extend(o4[pp * tq:(pp + 1) * tq, :] for pp in range(PAIRS_PER_KV))
        yd_ref[rs, :] = jnp.concatenate(outs, axis=-1).astype(BF16)


def _swa_weight_specs(w_in, d):
    nq = N_HEADS * HEAD_DIM
    nkv = N_KV * HEAD_DIM
    nw = nq + 2 * nkv
    q_blk = (w_in.shape[1] - nw) // nq
    kv_blk = (w_in.shape[1] - 2 * nkv) // (2 * nkv)
    assert q_blk * nq + nw == w_in.shape[1] and kv_blk * 2 * nkv + 2 * nkv == w_in.shape[1]
    return [pl.BlockSpec((d, nq), lambda i: (0, q_blk)), pl.BlockSpec((d, 2 * nkv), lambda i: (0, kv_blk))]


def _swa_outputs(rows, rt):
    nq = N_HEADS * HEAD_DIM
    nkv = N_KV * HEAD_DIM
    specs = [pl.BlockSpec((rt, nq), lambda i: (i, 0)),
             pl.BlockSpec((rt, nkv), lambda i: (i, 0)),
             pl.BlockSpec((rt, nkv), lambda i: (i, 0))]
    shapes = [jax.ShapeDtypeStruct((rows, nq), BF16),
              jax.ShapeDtypeStruct((rows, nkv), F32),
              jax.ShapeDtypeStruct((rows, nkv), F32)]
    return specs, shapes


def _swa_cached_mixer(h2, w_in, k_cache, v_cache, bias, sinks, tq):
    rows, d = h2.shape
    nkv = N_KV * HEAD_DIM
    nw = N_HEADS * HEAD_DIM + 2 * nkv
    n_blocks = ROW_TILE // tq
    cache_spec = pl.BlockSpec((n_blocks, WINDOW, nkv), lambda i: (i, 0, 0))
    out_specs, out_shape = _swa_outputs(rows, ROW_TILE)
    return pl.pallas_call(
        functools.partial(_swa_cached_kernel, tq=tq),
        grid=(rows // ROW_TILE,),
        in_specs=[pl.BlockSpec((ROW_TILE, d), lambda i: (i, 0))] + _swa_weight_specs(w_in, d)
        + [cache_spec, cache_spec,
           pl.BlockSpec((N_HEADS * tq, 2 * WINDOW), lambda i: (0, 0)),
           pl.BlockSpec((N_HEADS * tq, 1), lambda i: (0, 0))],
        out_specs=out_specs,
        out_shape=out_shape,
        scratch_shapes=[pltpu.VMEM((d, nw), BF16)],
        compiler_params=_cparams(1),
        name="swa_cached",
    )(h2, w_in, w_in, k_cache, v_cache, bias, sinks)


def _swa_stream_mixer(h2, w_in, bias_t, sinks_t, blocks_per_seq):
    rows, d = h2.shape
    nkv = N_KV * HEAD_DIM
    nw = N_HEADS * HEAD_DIM + 2 * nkv
    lanes = PAIRS_PER_KV * WINDOW
    rt = min(rows, MATMUL_TILE)
    out_specs, out_shape = _swa_outputs(rows, rt)
    return pl.pallas_call(
        functools.partial(_swa_stream_kernel, blocks_per_seq=blocks_per_seq),
        grid=(rows // rt,),
        in_specs=[pl.BlockSpec((rt, d), lambda i: (i, 0))] + _swa_weight_specs(w_in, d)
        + [pl.BlockSpec((2, N_KV, 2, 2 * WINDOW, lanes), lambda i: (0, 0, 0, 0, 0)),
           pl.BlockSpec((N_KV, 2, 1, lanes), lambda i: (0, 0, 0, 0))],
        out_specs=out_specs,
        out_shape=out_shape,
        scratch_shapes=[pltpu.VMEM((d, nw), BF16), pltpu.VMEM((WINDOW, nkv), F32),
                        pltpu.VMEM((nkv, WINDOW), F32)],
        compiler_params=_cparams(1),
        name="swa_stream",
    )(h2, w_in, w_in, bias_t, sinks_t)


def _t5_bucket(dist):
    max_exact = N_BUCKETS // 2
    dd = np.maximum(dist, 1)
    large = max_exact + (np.log(dd / max_exact) / np.log(WINDOW / max_exact)
                         * (N_BUCKETS - max_exact)).astype(np.int64)
    large = np.minimum(large, N_BUCKETS - 1)
    return np.where(dist < max_exact, dist, large).astype(np.int32)


def _attention_bias(rel_bias):
    by_dist = jnp.take(rel_bias.astype(F32), _t5_bucket(np.arange(WINDOW)), axis=0).T
    neg = jnp.full((N_HEADS, WINDOW), NEG_INF, F32)
    line = jnp.concatenate([neg, by_dist[:, ::-1], neg[:, :WINDOW - 1]], axis=1)
    rows = line[:, None, :]
    span = 1
    while span < WINDOW:
        rows = jnp.concatenate([rows[:, :, span:], rows[:, :, :rows.shape[2] - span]], axis=1)
        span *= 2
    return rows


def kernel(x_prompt, x_sample, state_pool, state_conv, cache_swa_k, cache_swa_v, c_prompt, c_sample, w_ada, b_ada, norm_g, final_norm_g, w_in_even, w_out_even, w_pool, pool_scale, conv_w, w_in_odd, w_out_odd, gm_norm_g, gm_norm_b, gm_w_s, gm_b_s, attn_sinks, rel_bias, w_router, b_router, w_gate, w_up, w_down):
    d = D_MODEL
    bp, tp, _ = x_prompt.shape
    bs, ts, _ = x_sample.shape
    rows_s = bs * ts
    assert rows_s == ROW_TILE and tp % ROUTER_TILE == 0 and PAST_LEN % CHUNK == 0
    assert bp <= V7X_SUBLANES and CHUNK % ts == 0

    n_c = bp + bs
    c_pad = (-n_c) % V7X_SUBLANES
    c_all = jnp.concatenate([c_prompt, c_sample, jnp.zeros((c_pad, d), F32)], axis=0)
    mod_p = _adaln(c_all, w_ada, b_ada)
    mod_s = jnp.repeat(mod_p[:, bp:bp + bs], ts, axis=1)

    xp = x_prompt.reshape(bp * tp, d)
    xs_ = x_sample.reshape(rows_s, d)
    w_in0, w_in1 = w_in_even[0], w_in_odd[0]

    hp0, ya_p, pool_p = _pool_mixer(xp, norm_g[0, 0], mod_p, 0, w_in0, w_pool[0], pool_scale[0],
                                    None, bp, 1, MATMUL_TILE, 0)
    hs0, ya_s, pool_s = _pool_mixer(xs_, norm_g[0, 0], mod_s, 0, w_in0, w_pool[0], pool_scale[0],
                                    state_pool[0], bs, bs, ts, PAST_LEN)
    yb_p, conv_p = _conv_mixer(hp0, w_in0, conv_w[0], None, bp, 1, MATMUL_TILE)
    yb_s, conv_s = _conv_mixer(hs0, w_in0, conv_w[0], state_conv[0], bs, bs, ts)
    out_p = _outproj(ya_p, yb_p, xp, mod_p, 0, norm_g[0, 1], w_out_even[0], tp)
    out_s = _outproj(ya_s, yb_s, xs_, mod_s, 0, norm_g[0, 1], w_out_even[0], rows_s)
    (x2p, h1p), (x2s, h1s) = _moe(out_p, out_s, mod_p, mod_s, 0, norm_g[1, 0], tp,
                                  w_router, b_router, w_gate, w_up, w_down, final=False)

    bs_t = gm_b_s[0].T
    (yc_p,) = _gmlp_mixer(h1p, w_in1, gm_norm_g[0], gm_norm_b[0], gm_w_s[0], bs_t, CHUNK, CHUNK, False)
    yc_s, gv_s = _gmlp_mixer(h1s, w_in1, gm_norm_g[0], gm_norm_b[0], gm_w_s[0], bs_t, rows_s, ts, True)
    bias = _attention_bias(rel_bias)
    nkv = N_KV * HEAD_DIM
    bias_t = jnp.transpose(bias.reshape(N_KV, PAIRS_PER_KV, 2, WINDOW, 2 * WINDOW), (0, 2, 4, 1, 3))
    bias_t = bias_t.reshape(N_KV, 2, 2 * WINDOW, PAIRS_PER_KV * WINDOW)
    before_start = (np.arange(2 * WINDOW) < WINDOW)[None, None, :, None]
    bias_t = jnp.stack([bias_t, jnp.where(before_start, NEG_INF, bias_t)], axis=0)
    sinks_t = jnp.transpose(attn_sinks[0].reshape(N_KV, PAIRS_PER_KV, 2), (0, 2, 1))
    sinks_t = jnp.repeat(sinks_t, WINDOW, axis=-1).reshape(N_KV, 2, 1, PAIRS_PER_KV * WINDOW)
    yd_p, k_p, v_p = _swa_stream_mixer(h1p, w_in1, bias_t, sinks_t, tp // WINDOW)
    yd_s, k_s, v_s = _swa_cached_mixer(h1s, w_in1, cache_swa_k[0].reshape(bs, WINDOW, nkv),
                                       cache_swa_v[0].reshape(bs, WINDOW, nkv),
                                       bias[:, :ts, :].reshape(N_HEADS * ts, 2 * WINDOW),
                                       jnp.repeat(attn_sinks[0], ts).reshape(-1, 1), ts)
    out_p = _outproj(yc_p, yd_p, x2p, mod_p, 1, norm_g[1, 1], w_out_odd[0], tp)
    out_s = _outproj(yc_s, yd_s, x2s, mod_s, 1, norm_g[1, 1], w_out_odd[0], rows_s)
    (yp,), (ys_out,) = _moe(out_p, out_s, mod_p, mod_s, 1, final_norm_g, tp,
                            w_router, b_router, w_gate, w_up, w_down, final=True)

    k_p4 = k_p.reshape(bp, tp, nkv)[:, -WINDOW:].reshape(bp, WINDOW, N_KV, HEAD_DIM)
    v_p4 = v_p.reshape(bp, tp, nkv)[:, -WINDOW:].reshape(bp, WINDOW, N_KV, HEAD_DIM)
    k_s4 = jnp.concatenate([cache_swa_k[0], k_s.reshape(bs, ts, N_KV, HEAD_DIM)], axis=1)[:, -WINDOW:]
    v_s4 = jnp.concatenate([cache_swa_v[0], v_s.reshape(bs, ts, N_KV, HEAD_DIM)], axis=1)[:, -WINDOW:]
    return (yp.reshape(bp, tp, d), ys_out.reshape(bs, ts, d),
            pool_p[None], pool_s[None], conv_p[None], conv_s[None],
            k_p4[None], k_s4[None], v_p4[None], v_s4[None],
            gv_s.reshape(bs, ts, GM_WIDTH)[None])
```

```python
import functools

import numpy as np
import jax
import jax.numpy as jnp
from jax import lax
from jax.experimental import pallas as pl
from jax.experimental.pallas import tpu as pltpu

F32 = jnp.float32
BF16 = jnp.bfloat16
I32 = jnp.int32
U32 = jnp.uint32

D_MODEL = 2048
POOL_WINDOWS = (2, 4, 8, 16)
POOL_WIDTH = 1024
POOL_GROUP = 256
POOL_STATE = 15
CONV_WIDTH = 1024
CONV_K = 3
GM_WIDTH = 1024
GM_GROUPS = 8
GM_GROUP = 128
CHUNK = 128
HEAD_DIM = 64
N_HEADS = 16
N_KV = 2
WINDOW = 128
N_BUCKETS = 32
N_EXPERTS = 16
N_EXPERT_GROUPS = 4
EXP_PER_GROUP = 4
TOP_K = 2
EPS = 1e-6
NEG_INF = -1e30
PAST_LEN = 16384

V7X_SUBLANES = 8
V7X_LANES = 128
VMEM_LIMIT = 56 * 1024 * 1024

ROW_TILE = 256
MATMUL_TILE = 512
ROUTER_TILE = 1024
POOL_HALO = 16
CONV_HALO = 8
MOE_TILE = 256
XS_RING = 3
ADALN_COL_TILE = 1024
CONV_COL_TILE = 512
TAB_EXPERT, TAB_VALID, TAB_LAST_TILE, TAB_NUSED, TAB_NEXT = 0, 1, 2, 3, 4


def _cparams(n_axes):
    return pltpu.CompilerParams(dimension_semantics=("arbitrary",) * n_axes,
                                vmem_limit_bytes=VMEM_LIMIT)


def _rms(x, g):
    return x * lax.rsqrt(jnp.mean(x * x, axis=-1, keepdims=True) + EPS) * g


def _mod_spec(mod, layer, part):
    nrow = ROW_TILE if mod.shape[1] == ROW_TILE else V7X_SUBLANES
    return pl.BlockSpec((1, nrow, D_MODEL), lambda *_: (layer, 0, part))


def _mod_rows(m_ref, seq):
    if m_ref.shape[1] == V7X_SUBLANES:
        return m_ref[0, pl.ds(seq, 1), :]
    return m_ref[0]


def _adaln_kernel(c_ref, w_ref, b_ref, o_ref):
    c = c_ref[...]
    a = (c * jax.nn.sigmoid(c)).astype(BF16)
    o_ref[0] = jnp.dot(a, w_ref[0].astype(BF16), preferred_element_type=F32) + b_ref[0]


def _adaln(c_all, w_ada, b_ada):
    depth, d, n6 = w_ada.shape
    m = c_all.shape[0]
    tn = ADALN_COL_TILE
    return pl.pallas_call(
        _adaln_kernel,
        grid=(depth, n6 // tn),
        in_specs=[pl.BlockSpec((m, d), lambda l, j: (0, 0)),
                  pl.BlockSpec((1, d, tn), lambda l, j: (l, 0, j)),
                  pl.BlockSpec((1, 1, tn), lambda l, j: (l, 0, j))],
        out_specs=pl.BlockSpec((1, m, tn), lambda l, j: (l, 0, j)),
        out_shape=jax.ShapeDtypeStruct((depth, m, n6), F32),
        compiler_params=_cparams(2),
        name="adaln",
    )(c_all, w_ada, b_ada.reshape(depth, 1, n6))


def _pool_kernel(x_ref, g_ref, sc_ref, sh_ref, w_ref, wp_ref, ps_ref, st_ref, h_ref, ya_ref, ns_ref,
                 wbf, wpbf, carry, *, nb, tm, tiles_per_seq, start):
    i = pl.program_id(0)
    t = i % tiles_per_seq
    seq = i // tiles_per_seq
    c = POOL_WIDTH
    halo = POOL_HALO

    @pl.when(i == 0)
    def _():
        wbf[...] = w_ref[...].astype(BF16)
        wpbf[...] = wp_ref[...].astype(BF16)

    @pl.when(t == 0)
    def _():
        carry[...] = st_ref[...]

    h = (_rms(x_ref[...], g_ref[...]) * (1.0 + _mod_rows(sc_ref, seq)) + _mod_rows(sh_ref, seq)).astype(BF16)
    h_ref[...] = h
    p = jnp.dot(h, wbf[...], preferred_element_type=F32)
    p3 = p.reshape(nb, tm, c)
    ext3 = jnp.concatenate([carry[...], p3], axis=1)
    tail = ext3[:, tm:tm + halo, :]
    ns_ref[...] = tail[:, halo - POOL_STATE:, :]
    carry[...] = tail
    ext = ext3.reshape(nb * (halo + tm), c)
    pos = start + t * tm + lax.broadcasted_iota(I32, (1, tm, 1), 1)
    outs = []
    for gi, w in enumerate(POOL_WINDOWS):
        sl = slice(gi * POOL_GROUP, (gi + 1) * POOL_GROUP)
        acc = ext[:, sl]
        shift = 1
        while shift < w:
            acc = acc + pltpu.roll(acc, shift, 0)
            shift *= 2
        win = acc.reshape(nb, halo + tm, POOL_GROUP)[:, halo:, :]
        cnt = jnp.minimum(pos + 1, w).astype(F32)
        dgrp = win / cnt - p3[:, :, sl]
        outs.append(jnp.dot(dgrp.reshape(nb * tm, POOL_GROUP).astype(BF16), wpbf[gi],
                            preferred_element_type=F32))
    y = jnp.concatenate(outs, axis=-1) * ps_ref[...]
    ya_ref[...] = y.astype(BF16)


def _pool_mixer(x2, g, mod, layer, w_in, w_pool, pool_scale, state, nseq, nb, tm, start):
    rows, d = x2.shape
    tiles_per_seq = (rows // nseq) // tm
    seq_blocks = nseq // nb
    c = POOL_WIDTH
    if state is None:
        st = jnp.zeros((nseq, POOL_HALO, c), F32)
    else:
        st = jnp.pad(state, ((0, 0), (POOL_HALO - POOL_STATE, 0), (0, 0)))
    kern = functools.partial(_pool_kernel, nb=nb, tm=tm, tiles_per_seq=tiles_per_seq, start=start)
    h2, ya, ns = pl.pallas_call(
        kern,
        grid=(seq_blocks * tiles_per_seq,),
        in_specs=[pl.BlockSpec((nb * tm, d), lambda i: (i, 0)),
                  pl.BlockSpec((1, d), lambda i: (0, 0)),
                  _mod_spec(mod, layer, 1), _mod_spec(mod, layer, 0),
                  pl.BlockSpec((d, c), lambda i: (0, 0)),
                  pl.BlockSpec((len(POOL_WINDOWS), POOL_GROUP, POOL_GROUP), lambda i: (0, 0, 0)),
                  pl.BlockSpec((1, c), lambda i: (0, 0)),
                  pl.BlockSpec((nb, POOL_HALO, c), lambda i: (i // tiles_per_seq, 0, 0))],
        out_specs=[pl.BlockSpec((nb * tm, d), lambda i: (i, 0)),
                   pl.BlockSpec((nb * tm, c), lambda i: (i, 0)),
                   pl.BlockSpec((nb, POOL_STATE, c), lambda i: (i // tiles_per_seq, 0, 0))],
        out_shape=[jax.ShapeDtypeStruct((rows, d), BF16),
                   jax.ShapeDtypeStruct((rows, c), BF16),
                   jax.ShapeDtypeStruct((nseq, POOL_STATE, c), F32)],
        scratch_shapes=[pltpu.VMEM((d, c), BF16),
                        pltpu.VMEM((len(POOL_WINDOWS), POOL_GROUP, POOL_GROUP), BF16),
                        pltpu.VMEM((nb, POOL_HALO, c), F32)],
        compiler_params=_cparams(1),
        name="pool_mixer",
    )(x2, g.reshape(1, d), mod, mod, w_in, w_pool, pool_scale.reshape(1, c), st)
    return h2, ya, ns


def _conv_kernel(h_ref, wx_ref, wb_ref, wc_ref, cw_ref, st_ref, yb_ref, ns_ref,
                 wxbf, wbbf, wcbf, carry, *, nb, tm, tiles_per_seq):
    i = pl.program_id(1)
    t = i % tiles_per_seq
    tc = wxbf.shape[1]
    halo = CONV_HALO

    @pl.when(i == 0)
    def _():
        wxbf[...] = wx_ref[...].astype(BF16)
        wbbf[...] = wb_ref[...].astype(BF16)
        wcbf[...] = wc_ref[...].astype(BF16)

    @pl.when(t == 0)
    def _():
        carry[...] = st_ref[...]

    h = h_ref[...]
    xin = jnp.dot(h, wxbf[...], preferred_element_type=F32)
    gb = jnp.dot(h, wbbf[...], preferred_element_type=F32)
    gc = jnp.dot(h, wcbf[...], preferred_element_type=F32)
    z3 = (gc * xin).reshape(nb, tm, tc)
    ext3 = jnp.concatenate([carry[...], z3], axis=1)
    tail = ext3[:, tm:tm + halo, :]
    ns_ref[...] = tail[:, halo - (CONV_K - 1):, :]
    carry[...] = tail
    ext = ext3.reshape(nb * (halo + tm), tc)
    cw = cw_ref[...]
    conv = cw[0:1, :] * pltpu.roll(ext, 2, 0) + cw[1:2, :] * pltpu.roll(ext, 1, 0) + cw[2:3, :] * ext
    conv = conv.reshape(nb, halo + tm, tc)[:, halo:, :].reshape(nb * tm, tc)
    yb_ref[...] = (gb * conv).astype(BF16)


def _conv_mixer(h2, w_in, conv_w, state, nseq, nb, tm):
    rows, d = h2.shape
    tiles_per_seq = (rows // nseq) // tm
    seq_blocks = nseq // nb
    c = CONV_WIDTH
    tc = CONV_COL_TILE
    cb = c // tc
    base = POOL_WIDTH // tc
    if state is None:
        st = jnp.zeros((nseq, CONV_HALO, c), F32)
    else:
        st = jnp.pad(state, ((0, 0), (CONV_HALO - (CONV_K - 1), 0), (0, 0)))
    kern = functools.partial(_conv_kernel, nb=nb, tm=tm, tiles_per_seq=tiles_per_seq)
    yb, ns = pl.pallas_call(
        kern,
        grid=(cb, seq_blocks * tiles_per_seq),
        in_specs=[pl.BlockSpec((nb * tm, d), lambda j, i: (i, 0)),
                  pl.BlockSpec((d, tc), lambda j, i: (0, base + j)),
                  pl.BlockSpec((d, tc), lambda j, i: (0, base + cb + j)),
                  pl.BlockSpec((d, tc), lambda j, i: (0, base + 2 * cb + j)),
                  pl.BlockSpec((CONV_K, tc), lambda j, i: (0, j)),
                  pl.BlockSpec((nb, CONV_HALO, tc), lambda j, i: (i // tiles_per_seq, 0, j))],
        out_specs=[pl.BlockSpec((nb * tm, tc), lambda j, i: (i, j)),
                   pl.BlockSpec((nb, CONV_K - 1, tc), lambda j, i: (i // tiles_per_seq, 0, j))],
        out_shape=[jax.ShapeDtypeStruct((rows, c), BF16),
                   jax.ShapeDtypeStruct((nseq, CONV_K - 1, c), F32)],
        scratch_shapes=[pltpu.VMEM((d, tc), BF16)] * 3 + [pltpu.VMEM((nb, CONV_HALO, tc), F32)],
        compiler_params=_cparams(2),
        name="conv_mixer",
    )(h2, w_in, w_in, w_in, conv_w, st)
    return yb, ns


def _pack_bf16_pairs(v):
    c = v.shape[1] // 2
    return pltpu.bitcast(pltpu.pack_elementwise([v[:, :c], v[:, c:]], packed_dtype=BF16), U32)


def _store_token_tiles(ref, v):
    rows = v.shape[0]
    for j in range(V7X_SUBLANES):
        ref[pl.ds(j, rows, stride=V7X_SUBLANES), :] = v[:, j * V7X_LANES:(j + 1) * V7X_LANES]


def _load_token_tiles(ref):
    rows = ref.shape[0] // V7X_SUBLANES
    return jnp.concatenate([ref[pl.ds(j, rows, stride=V7X_SUBLANES), :] for j in range(V7X_SUBLANES)],
                           axis=-1)


def _unpack_pairs_f32(w):
    return tuple(pltpu.unpack_elementwise(w, index=k, packed_dtype=BF16, unpacked_dtype=F32) for k in range(2))


def _unpack_bf16_pairs(w):
    lo, hi = _unpack_pairs_f32(w)
    return lo.astype(BF16), hi.astype(BF16)


def _outproj_kernel(ya_ref, yb_ref, x_ref, g1_ref, sc_ref, sh_ref, ng_ref, wo_ref,
                    x1_ref, hp_ref, hpt_ref, wobf, *, tiles_per_seq):
    i = pl.program_id(0)
    seq = i // tiles_per_seq

    @pl.when(i == 0)
    def _():
        wobf[...] = wo_ref[...].astype(BF16)

    ycat = jnp.concatenate([ya_ref[...], yb_ref[...]], axis=-1)
    y = jnp.dot(ycat, wobf[...], preferred_element_type=F32)
    x1 = x_ref[...] + _mod_rows(g1_ref, seq) * y
    x1_ref[...] = x1
    h2 = _rms(x1, ng_ref[...]) * (1.0 + _mod_rows(sc_ref, seq)) + _mod_rows(sh_ref, seq)
    packed = _pack_bf16_pairs(h2)
    hp_ref[...] = packed
    _store_token_tiles(hpt_ref, packed)


def _outproj(ya, yb, x2, mod, layer, ng, w_out, seq_rows):
    rows_all, d = x2.shape
    half = ya.shape[1]
    rt = ROW_TILE
    row_spec = lambda w: pl.BlockSpec((rt, w), lambda i: (i, 0))
    return pl.pallas_call(
        functools.partial(_outproj_kernel, tiles_per_seq=seq_rows // rt),
        grid=(rows_all // rt,),
        in_specs=[row_spec(half), row_spec(half), row_spec(d),
                  _mod_spec(mod, layer, 2), _mod_spec(mod, layer, 4), _mod_spec(mod, layer, 3),
                  pl.BlockSpec((1, d), lambda i: (0, 0)),
                  pl.BlockSpec((d, d), lambda i: (0, 0), pipeline_mode=pl.Buffered(1))],
        out_specs=[row_spec(d), row_spec(d // 2),
                   pl.BlockSpec((rt * V7X_SUBLANES, V7X_LANES), lambda i: (i, 0))],
        out_shape=[jax.ShapeDtypeStruct((rows_all, d), F32),
                   jax.ShapeDtypeStruct((rows_all, d // 2), U32),
                   jax.ShapeDtypeStruct((rows_all * V7X_SUBLANES, V7X_LANES), U32)],
        scratch_shapes=[pltpu.VMEM((d, d), BF16)],
        compiler_params=_cparams(1),
        name="outproj",
    )(ya, yb, x2, mod, mod, mod, ng.reshape(1, d), w_out)


def _router_kernel(hpp_ref, hps_ref, wr_ref, br_ref, pos_ref, rw_ref, tab_ref,
                   cnt_acc, totals, starts, padded, before_ref, s_all, sel_all, *, nt_p, rows_s):
    ph = pl.program_id(0)
    t = pl.program_id(1)
    last = nt_p
    r = hpp_ref.shape[0]
    half = hpp_ref.shape[1]
    ne = N_EXPERTS
    sub = lax.broadcasted_iota(I32, (ne, V7X_LANES), 0)

    @pl.when(t == 0)
    def _():
        cnt_acc[...] = jnp.zeros_like(cnt_acc)

    @pl.when((ph == 0) & (t == 0))
    def _():
        starts[...] = jnp.zeros_like(starts)
        padded[...] = jnp.zeros_like(padded)

    @pl.when((ph == 1) & (t == 0))
    def _():
        pad = jnp.floor((totals[...] + (MOE_TILE - 1.0)) * (1.0 / MOE_TILE)) * MOE_TILE
        run = pad
        k = 1
        while k < ne:
            run = run + jnp.where(sub >= k, pltpu.roll(run, k, 0), 0.0)
            k *= 2
        padded[...] = pad
        starts[...] = run - pad

    is_s = t == last
    eid = lax.broadcasted_iota(I32, (ne, r), 0)
    n_valid = jnp.where(is_s, rows_s, r)
    tok = lax.broadcasted_iota(I32, (ne, r), 1)

    @pl.when(ph == 0)
    def _():
        w_s = jnp.concatenate([hps_ref[...], jnp.zeros((r - rows_s, half), U32)], axis=0)
        w = jnp.where(is_s, w_s, hpp_ref[...])
        lo, hi = _unpack_bf16_pairs(w)
        wr = wr_ref[...].astype(BF16)
        log_t = (lax.dot_general(wr[:, :half], lo, NT_DIMS, preferred_element_type=F32)
                 + lax.dot_general(wr[:, half:], hi, NT_DIMS, preferred_element_type=F32))

        s = jax.nn.sigmoid(log_t)
        sg = s + br_ref[...]
        within = eid % EXP_PER_GROUP
        grp = eid // EXP_PER_GROUP

        def group_rot(x, k):
            return jnp.where(within + k < EXP_PER_GROUP,
                             pltpu.roll(x, ne - k, 0), pltpu.roll(x, EXP_PER_GROUP - k, 0))

        rank = jnp.zeros((ne, r), I32)
        for k in range(1, EXP_PER_GROUP):
            mate = group_rot(sg, k)
            wrapped = within + k >= EXP_PER_GROUP
            ahead = (mate > sg) | (wrapped & (mate == sg))
            rank = rank + ahead.astype(I32)
        top2 = rank < TOP_K
        kept = jnp.where(top2, sg, 0.0)
        gscore = kept
        for k in range(1, EXP_PER_GROUP):
            gscore = gscore + group_rot(kept, k)
        win = None
        for k in range(1, N_EXPERT_GROUPS):
            other = pltpu.roll(gscore, EXP_PER_GROUP * k, 0)
            beats = (gscore > other) | ((grp < k) & (gscore == other))
            win = beats if win is None else (win & beats)
        picked_now = top2 & win & (tok < n_valid)
        s_all[t] = s
        sel_all[t] = picked_now.astype(F32)

    s = s_all[t]
    selb = sel_all[t]
    sel = selb > 0.5
    cnt_before = cnt_acc[...]
    cnt_new = cnt_before + jnp.sum(selb, axis=1, keepdims=True)
    cnt_acc[...] = cnt_new

    @pl.when((ph == 0) & (t == 0))
    def _():
        src = lax.broadcasted_iota(I32, (r, r), 0)
        dst = lax.broadcasted_iota(I32, (r, r), 1)
        before_ref[...] = (src < dst).astype(BF16)

    @pl.when((ph == 0) & (t == last))
    def _():
        totals[...] = cnt_new

    @pl.when(ph == 1)
    def _():
        picked = jnp.where(sel, s, 0.0)
        wsum = jnp.sum(picked, axis=0, keepdims=True)
        gate = picked / jnp.where(tok[0:1, :] < n_valid, wsum, 1.0)
        ranks = jnp.dot(selb.astype(BF16), before_ref[...], preferred_element_type=F32)
        slot = (starts[...][:, 0:1] + cnt_before[:, 0:1] + ranks).astype(I32)
        e_a = jnp.min(jnp.where(sel, eid, ne), axis=0, keepdims=True)
        e_b = jnp.max(jnp.where(sel, eid, -1), axis=0, keepdims=True)
        is_a = sel & (eid == e_a)
        is_b = sel & (eid == e_b)
        pos_a = jnp.sum(jnp.where(is_a, slot, 0), axis=0, keepdims=True)
        pos_b = jnp.sum(jnp.where(is_b, slot, 0), axis=0, keepdims=True)
        w_a = jnp.sum(jnp.where(is_a, gate, 0.0), axis=0, keepdims=True)
        w_b = jnp.sum(jnp.where(is_b, gate, 0.0), axis=0, keepdims=True)
        pos_ref[0] = jnp.concatenate([pos_a, pos_b], axis=0)
        wmat = jnp.concatenate([w_a, w_b, jnp.zeros((V7X_LANES - 2, r), F32)], axis=0)
        rw_ref[...] = wmat.T

    @pl.when((ph == 1) & (t == last))
    def _():
        ends = starts[...] + padded[...]
        lane = lax.broadcasted_iota(I32, (ne, V7X_LANES), 1)
        tile_start = (lane * MOE_TILE).astype(F32)
        te = jnp.sum((tile_start >= ends).astype(I32), axis=0, keepdims=True)
        valid = te < ne
        last_e = jnp.max(jnp.where(padded[...] > 0.0, sub, 0), axis=0, keepdims=True)
        te = jnp.where(valid, te, last_e)
        n_used = jnp.sum(valid.astype(I32), axis=1, keepdims=True) + jnp.zeros((1, V7X_LANES), I32)
        last_tile = jnp.where(padded[...] > 0.0, ends - MOE_TILE, -1.0).astype(I32)
        last_tile_row = jnp.sum(jnp.where(sub == lane, last_tile, 0), axis=0, keepdims=True)
        later = jnp.min(jnp.where((sub > te) & (padded[...] > 0.0), sub, ne), axis=0, keepdims=True)
        next_e = jnp.where(later < ne, later, -1)
        zero = jnp.zeros((1, V7X_LANES), I32)
        tab_ref[...] = jnp.concatenate([te, valid.astype(I32), last_tile_row, n_used, next_e,
                                        zero, zero, zero], axis=0)


def _router(hp_p, hp_s, w_router, b_router):
    n_p, half = hp_p.shape
    rows_s = hp_s.shape[0]
    r = ROUTER_TILE
    nt_p = n_p // r
    nt = nt_p + 1
    kern = functools.partial(_router_kernel, nt_p=nt_p, rows_s=rows_s)
    pos, rw, tab = pl.pallas_call(
        kern,
        grid=(2, nt),
        in_specs=[pl.BlockSpec((r, half), lambda p, t: (jnp.minimum(t, nt_p - 1) * (1 - p), 0)),
                  pl.BlockSpec((rows_s, half), lambda p, t: (0, 0)),
                  pl.BlockSpec((N_EXPERTS, 2 * half), lambda p, t: (0, 0)),
                  pl.BlockSpec((N_EXPERTS, 1), lambda p, t: (0, 0))],
        out_specs=[pl.BlockSpec((1, TOP_K, r), lambda p, t: (p * t, 0, 0)),
                   pl.BlockSpec((r, V7X_LANES), lambda p, t: (p * t, 0)),
                   pl.BlockSpec((V7X_SUBLANES, V7X_LANES), lambda p, t: (0, 0))],
        out_shape=[jax.ShapeDtypeStruct((nt, TOP_K, r), I32),
                   jax.ShapeDtypeStruct((nt * r, V7X_LANES), F32),
                   jax.ShapeDtypeStruct((V7X_SUBLANES, V7X_LANES), I32)],
        scratch_shapes=[pltpu.VMEM((N_EXPERTS, V7X_LANES), F32)] * 4 + [pltpu.VMEM((r, r), BF16)]
        + [pltpu.VMEM((nt, N_EXPERTS, r), F32)] * 2,
        compiler_params=_cparams(2),
        name="router",
    )(hp_p, hp_s, w_router.T, b_router.reshape(N_EXPERTS, 1))
    return pos.reshape(-1), rw, tab.reshape(-1)


def _pos_index(tok0):
    return (tok0 // ROUTER_TILE) * (TOP_K * ROUTER_TILE) + tok0 % ROUTER_TILE


def _tokens(ref, first, n=1):
    start = pl.multiple_of(first * V7X_SUBLANES, V7X_SUBLANES)
    return ref.at[pl.ds(start, n * V7X_SUBLANES), :]


def _dispatch_kernel(pos_ref, tab_ref, hpp_ref, hps_ref, xs_ref, zbuf, sem, *, n_p_steps):
    i = pl.program_id(0)
    rows = hpp_ref.shape[0] // V7X_SUBLANES

    @pl.when(i == 0)
    def _():
        zbuf[...] = jnp.zeros_like(zbuf)

        def fill(e):
            first = pl.multiple_of(tab_ref[TAB_LAST_TILE * V7X_LANES + e], MOE_TILE)
            return pltpu.make_async_copy(zbuf, _tokens(xs_ref, first, MOE_TILE), sem)

        for e in range(N_EXPERTS):
            @pl.when(tab_ref[TAB_LAST_TILE * V7X_LANES + e] >= 0)
            def _():
                fill(e).start()
        for e in range(N_EXPERTS):
            @pl.when(tab_ref[TAB_LAST_TILE * V7X_LANES + e] >= 0)
            def _():
                fill(e).wait()

        def tail(j):
            first = pl.multiple_of(j * MOE_TILE, MOE_TILE)
            return pltpu.make_async_copy(zbuf, _tokens(xs_ref, first, MOE_TILE), sem)

        def tail_start(j, carry):
            tail(j).start()
            return carry

        def tail_wait(j, carry):
            tail(j).wait()
            return carry

        n_used = tab_ref[TAB_NUSED * V7X_LANES]
        n_tiles = xs_ref.shape[0] // (MOE_TILE * V7X_SUBLANES)
        lax.fori_loop(n_used, n_tiles, tail_start, 0)
        lax.fori_loop(n_used, n_tiles, tail_wait, 0)

    def scatter(src_ref, tok0):
        n = src_ref.shape[0] // V7X_SUBLANES
        base = _pos_index(tok0)

        def row_copy(r, dst):
            return pltpu.make_async_copy(_tokens(src_ref, r), _tokens(xs_ref, dst), sem)

        def issue(r, carry):
            row_copy(r, pos_ref[base + r]).start()
            row_copy(r, pos_ref[base + ROUTER_TILE + r]).start(priority=1)
            return carry

        lax.fori_loop(0, n, issue, 0, unroll=8)
        block = pltpu.make_async_copy(src_ref, _tokens(xs_ref, 0, n), sem)
        for _ in range(TOP_K):
            block.wait()

    @pl.when(i < n_p_steps)
    def _():
        scatter(hpp_ref, i * rows)

    @pl.when(i == n_p_steps)
    def _():
        scatter(hps_ref, n_p_steps * rows)


def _dispatch(pos, tab, hpt_p, hpt_s, n_rows_sorted):
    sub = V7X_SUBLANES
    n_p_steps = hpt_p.shape[0] // (ROUTER_TILE * sub)
    kern = functools.partial(_dispatch_kernel, n_p_steps=n_p_steps)
    blk = (ROUTER_TILE * sub, V7X_LANES)
    return pl.pallas_call(
        kern,
        grid_spec=pltpu.PrefetchScalarGridSpec(
            num_scalar_prefetch=2,
            grid=(n_p_steps + 1,),
            in_specs=[pl.BlockSpec(blk, lambda i, p, t: (jnp.minimum(i, n_p_steps - 1), 0)),
                      pl.BlockSpec(hpt_s.shape, lambda i, p, t: (0, 0))],
            out_specs=pl.BlockSpec(memory_space=pl.ANY),
            scratch_shapes=[pltpu.VMEM((MOE_TILE * sub, V7X_LANES), U32), pltpu.SemaphoreType.DMA(())]),
        out_shape=jax.ShapeDtypeStruct((n_rows_sorted * sub, V7X_LANES), U32),
        compiler_params=_cparams(1),
        name="moe_dispatch",
    )(pos, tab, hpt_p, hpt_s)


def _experts_kernel(tab_ref, xs_hbm, wg_hbm, wu_hbm, wd_hbm, ys_ref,
                    wg32, wu32, wd32, wgbf, wubf, wdbf, slot_ref, sems, xbuf, xsems, *, layer):
    i = pl.program_id(0)
    expert = tab_ref[TAB_EXPERT * V7X_LANES + i]
    prev = tab_ref[TAB_EXPERT * V7X_LANES + jnp.maximum(i - 1, 0)]
    upcoming = tab_ref[TAB_NEXT * V7X_LANES + i]
    n_used = tab_ref[TAB_NUSED * V7X_LANES]
    changed = (i == 0) | (expert != prev)
    depth = xbuf.shape[0]

    def tile_copy(j):
        j = jnp.asarray(j, I32)
        return pltpu.make_async_copy(_tokens(xs_hbm, j * MOE_TILE, MOE_TILE), xbuf.at[j % depth],
                                     xsems.at[j % depth])

    @pl.when(i == 0)
    def _():
        for j in range(depth - 1):
            @pl.when(j < n_used)
            def _():
                tile_copy(j).start()

    @pl.when(i + depth - 1 < n_used)
    def _():
        tile_copy(i + depth - 1).start()

    @pl.when(i < n_used)
    def _():
        tile_copy(i).wait()

    xs_ref = xbuf.at[i % depth]

    def weight_copies(e, slot):
        return (pltpu.make_async_copy(wg_hbm.at[layer, e], wg32.at[slot], sems.at[0, slot]),
                pltpu.make_async_copy(wu_hbm.at[layer, e], wu32.at[slot], sems.at[1, slot]),
                pltpu.make_async_copy(wd_hbm.at[layer, e], wd32.at[slot], sems.at[2, slot]))

    @pl.when(i == 0)
    def _():
        slot_ref[0] = 0
        for cp in weight_copies(expert, 0):
            cp.start()

    @pl.when(changed & (i > 0))
    def _():
        slot_ref[0] = 1 - slot_ref[0]

    def mlp(wg, wu, wd):
        x = jnp.concatenate(_unpack_bf16_pairs(_load_token_tiles(xs_ref)), axis=-1)
        a = jnp.dot(x, wg, preferred_element_type=F32)
        b = jnp.dot(x, wu, preferred_element_type=F32)
        hid = (a * jax.nn.sigmoid(a)) * b
        y = jnp.dot(hid.astype(BF16), wd, preferred_element_type=F32)
        _store_token_tiles(ys_ref, _pack_bf16_pairs(y))

    for slot in range(2):
        @pl.when(changed & (slot_ref[0] == slot))
        def _():
            for cp in weight_copies(expert, slot):
                cp.wait()

            @pl.when(upcoming >= 0)
            def _():
                for cp in weight_copies(upcoming, 1 - slot):
                    cp.start(priority=1)

            wg = wg32[slot].astype(BF16)
            wu = wu32[slot].astype(BF16)
            wd = wd32[slot].astype(BF16)
            wgbf[...] = wg
            wubf[...] = wu
            wdbf[...] = wd
            mlp(wg, wu, wd)

    valid = tab_ref[TAB_VALID * V7X_LANES + i] > 0

    @pl.when(valid & jnp.logical_not(changed))
    def _():
        mlp(wgbf[...], wubf[...], wdbf[...])

    @pl.when(jnp.logical_not(valid))
    def _():
        ys_ref[...] = jnp.zeros_like(ys_ref)


def _experts(tab, xs, w_gate, w_up, w_down, layer):
    sub = V7X_SUBLANES
    _, _, d, f = w_gate.shape
    nt = xs.shape[0] // (MOE_TILE * sub)
    assert nt <= V7X_LANES and d == 2 * sub * V7X_LANES
    blk = (MOE_TILE * sub, V7X_LANES)

    hbm = pl.BlockSpec(memory_space=pl.ANY)
    return pl.pallas_call(
        functools.partial(_experts_kernel, layer=layer),
        grid_spec=pltpu.PrefetchScalarGridSpec(
            num_scalar_prefetch=1,
            grid=(nt,),
            in_specs=[hbm, hbm, hbm, hbm],
            out_specs=pl.BlockSpec(blk, lambda i, t: (i, 0)),
            scratch_shapes=[pltpu.VMEM((2, d, f), F32), pltpu.VMEM((2, d, f), F32), pltpu.VMEM((2, f, d), F32),
                            pltpu.VMEM((d, f), BF16), pltpu.VMEM((d, f), BF16), pltpu.VMEM((f, d), BF16),
                            pltpu.SMEM((1,), I32), pltpu.SemaphoreType.DMA((3, 2)),
                            pltpu.VMEM((XS_RING,) + blk, U32), pltpu.SemaphoreType.DMA((XS_RING,))]),
        out_shape=jax.ShapeDtypeStruct(xs.shape, U32),
        compiler_params=_cparams(1),
        name="moe_experts",
    )(tab, xs, w_gate, w_up, w_down)


def _combine_kernel(pos_ref, ys_ref, rw_ref, ng_ref, *rest, tiles_per_seq, n_p_steps, final):
    n_in = 4
    n_out = 1 if final else 2
    trunk_in = (rest[:n_in], rest[n_in:2 * n_in])
    outs = rest[2 * n_in:2 * n_in + 2 * n_out]
    trunk_out = (outs[:n_out], outs[n_out:])
    buf0, buf1, sems = rest[2 * n_in + 2 * n_out:]
    bufs = (buf0, buf1)
    i = pl.program_id(0)
    n_steps = n_p_steps + 1
    rows = rw_ref.shape[0]

    def gather(step, slot):
        base = _pos_index(step * rows)

        def issue(r, carry):
            for k in range(TOP_K):
                src = pos_ref[base + k * ROUTER_TILE + r]
                pltpu.make_async_copy(_tokens(ys_ref, src), _tokens(bufs[slot].at[k], r),
                                      sems.at[slot]).start(priority=k)
            return carry

        lax.fori_loop(0, rows, issue, 0, unroll=8)

    def drain(slot):
        for k in range(TOP_K):
            pltpu.make_async_copy(_tokens(ys_ref, 0, rows), bufs[slot].at[k], sems.at[slot]).wait()

    def finish(slot, trunk, seq):
        x1_ref, g2_ref, sc_ref, sh_ref = trunk_in[trunk]
        rw = rw_ref[...]
        lo_a, hi_a = _unpack_pairs_f32(_load_token_tiles(bufs[slot].at[0]))
        lo_b, hi_b = _unpack_pairs_f32(_load_token_tiles(bufs[slot].at[1]))
        w_a = rw[:, 0:1]
        w_b = rw[:, 1:2]
        moe = jnp.concatenate([w_a * lo_a + w_b * lo_b, w_a * hi_a + w_b * hi_b], axis=-1)
        x2 = x1_ref[...] + _mod_rows(g2_ref, seq) * moe
        if final:
            trunk_out[trunk][0][...] = _rms(x2, ng_ref[...])
        else:
            trunk_out[trunk][0][...] = x2
            trunk_out[trunk][1][...] = (_rms(x2, ng_ref[...]) * (1.0 + _mod_rows(sc_ref, seq))
                                        + _mod_rows(sh_ref, seq)).astype(BF16)

    @pl.when(i == 0)
    def _():
        gather(0, 0)

    for slot in range(2):
        @pl.when(i % 2 == slot)
        def _():
            @pl.when(i + 1 < n_steps)
            def _():
                gather(i + 1, 1 - slot)

            drain(slot)

            @pl.when(i < n_p_steps)
            def _():
                finish(slot, 0, i // tiles_per_seq)

            @pl.when(i == n_p_steps)
            def _():
                finish(slot, 1, 0)


def _combine(pos, ys, x1_p, x1_s, rw, mod_p, mod_s, layer, ng, seq_rows_p, final):
    n_p, d = x1_p.shape
    rt = ROW_TILE
    assert x1_s.shape[0] == rt and n_p % rt == 0 and ROUTER_TILE % rt == 0
    n_p_steps = n_p // rt
    kern = functools.partial(_combine_kernel, tiles_per_seq=seq_rows_p // rt, n_p_steps=n_p_steps, final=final)
    p_spec = lambda w: pl.BlockSpec((rt, w), lambda i, p: (jnp.minimum(i, n_p_steps - 1), 0))
    s_spec = lambda w: pl.BlockSpec((rt, w), lambda i, p: (0, 0))
    nxt = min(layer + 1, mod_p.shape[0] - 1)

    def trunk_specs(spec, mod):
        return [spec(d), _mod_spec(mod, layer, 5), _mod_spec(mod, nxt, 1), _mod_spec(mod, nxt, 0)]

    out_dtypes = [F32] if final else [F32, BF16]
    out_shape = ([jax.ShapeDtypeStruct((n_p, d), t) for t in out_dtypes]
                 + [jax.ShapeDtypeStruct((rt, d), t) for t in out_dtypes])
    out_specs = [p_spec(d)] * len(out_dtypes) + [s_spec(d)] * len(out_dtypes)
    res = pl.pallas_call(
        kern,
        grid_spec=pltpu.PrefetchScalarGridSpec(
            num_scalar_prefetch=1,
            grid=(n_p_steps + 1,),
            in_specs=[pl.BlockSpec(memory_space=pl.ANY),
                      pl.BlockSpec((rt, V7X_LANES), lambda i, p: (i, 0)),
                      pl.BlockSpec((1, d), lambda i, p: (0, 0))]
            + trunk_specs(p_spec, mod_p) + trunk_specs(s_spec, mod_s),
            out_specs=out_specs,
            scratch_shapes=[pltpu.VMEM((TOP_K, rt * V7X_SUBLANES, V7X_LANES), U32)] * 2
            + [pltpu.SemaphoreType.DMA((2,))]),
        out_shape=out_shape,
        compiler_params=_cparams(1),
        name="moe_combine_final" if final else "moe_combine",
    )(pos, ys, rw, ng.reshape(1, d), x1_p, mod_p, mod_p, mod_p, x1_s, mod_s, mod_s, mod_s)
    return res[:len(out_dtypes)], res[len(out_dtypes):]


def _moe(out_p, out_s, mod_p, mod_s, layer, ng, seq_rows_p, w_router, b_router, w_gate, w_up, w_down, final):
    x1_p, hp_p, hpt_p = out_p
    x1_s, hp_s, hpt_s = out_s
    n_p = x1_p.shape[0]
    n_tok = n_p + x1_s.shape[0]
    max_rows = TOP_K * n_tok + N_EXPERTS * (MOE_TILE - 1)
    n_rows_sorted = -(-max_rows // MOE_TILE) * MOE_TILE
    pos, rw, tab = _router(hp_p, hp_s, w_router, b_router)
    xs = _dispatch(pos, tab, hpt_p, hpt_s, n_rows_sorted)
    ys = _experts(tab, xs, w_gate, w_up, w_down, layer)
    return _combine(pos, ys, x1_p, x1_s, rw, mod_p, mod_s, layer, ng, seq_rows_p, final)


def _gmlp_kernel(h_ref, w_ref, lg_ref, lb_ref, ws_ref, bs_ref, yc_ref, *rest, ell, blk, emit_v):
    if emit_v:
        gv_ref, wbf, wsbf = rest
    else:
        wbf, wsbf = rest
    i = pl.program_id(0)
    rows = h_ref.shape[0]
    c = GM_WIDTH

    @pl.when(i == 0)
    def _():
        wbf[...] = w_ref[...].astype(BF16)
        r = lax.broadcasted_iota(I32, (ell, ell), 0)
        s = lax.broadcasted_iota(I32, (ell, ell), 1)
        keep = (r >= s) & ((r // blk) == (s // blk))
        rsel = (lax.broadcasted_iota(I32, (ell, CHUNK), 0) % blk
                == lax.broadcasted_iota(I32, (ell, CHUNK), 1)).astype(BF16)
        csel = (lax.broadcasted_iota(I32, (CHUNK, ell), 1) % blk
                == lax.broadcasted_iota(I32, (CHUNK, ell), 0)).astype(BF16)
        for g in range(GM_GROUPS):
            wchunk = ws_ref[g].astype(BF16)
            if blk == ell:
                full = wchunk
            else:
                rowsp = jnp.dot(rsel, wchunk, preferred_element_type=F32).astype(BF16)
                full = jnp.dot(rowsp, csel, preferred_element_type=F32).astype(BF16)
            wsbf[g] = jnp.where(keep, full, jnp.zeros_like(full))

    uv = jnp.dot(h_ref[...], wbf[...], preferred_element_type=F32)
    u = uv[:, :c]
    v = uv[:, c:]
    vc = v - jnp.mean(v, axis=-1, keepdims=True)
    vn = vc * lax.rsqrt(jnp.mean(vc * vc, axis=-1, keepdims=True) + EPS) * lg_ref[...] + lb_ref[...]
    if emit_v:
        gv_ref[...] = vn
    vb = vn.astype(BF16)
    bs = bs_ref[...]
    for ch in range(rows // ell):
        rs = slice(ch * ell, (ch + 1) * ell)
        outs = []
        for g in range(GM_GROUPS):
            cs = slice(g * GM_GROUP, (g + 1) * GM_GROUP)
            mixed = jnp.dot(wsbf[g], vb[rs, cs], preferred_element_type=F32)
            mixed = (mixed.reshape(ell // blk, blk, GM_GROUP) + bs[:blk, g:g + 1][None]).reshape(ell, GM_GROUP)
            outs.append(u[rs, cs] * mixed)
        yc_ref[rs, :] = jnp.concatenate(outs, axis=-1).astype(BF16)


def _gmlp_mixer(h2, w_in, ln_g, ln_b, ws, bs_t, ell, blk, emit_v):
    rows, d = h2.shape
    c = GM_WIDTH
    kern = functools.partial(_gmlp_kernel, ell=ell, blk=blk, emit_v=emit_v)
    rt = min(rows, MATMUL_TILE)
    out_specs = [pl.BlockSpec((rt, c), lambda i: (i, 0))]
    out_shape = [jax.ShapeDtypeStruct((rows, c), BF16)]
    if emit_v:
        out_specs.append(pl.BlockSpec((rt, c), lambda i: (i, 0)))
        out_shape.append(jax.ShapeDtypeStruct((rows, c), F32))
    return pl.pallas_call(
        kern,
        grid=(rows // rt,),
        in_specs=[pl.BlockSpec((rt, d), lambda i: (i, 0)),
                  pl.BlockSpec((d, 2 * c), lambda i: (0, 0), pipeline_mode=pl.Buffered(1)),
                  pl.BlockSpec((1, c), lambda i: (0, 0)),
                  pl.BlockSpec((1, c), lambda i: (0, 0)),
                  pl.BlockSpec((GM_GROUPS, CHUNK, CHUNK), lambda i: (0, 0, 0)),
                  pl.BlockSpec((CHUNK, GM_GROUPS), lambda i: (0, 0))],
        out_specs=out_specs,
        out_shape=out_shape,
        scratch_shapes=[pltpu.VMEM((d, 2 * c), BF16), pltpu.VMEM((GM_GROUPS, ell, ell), BF16)],
        compiler_params=_cparams(1),
        name="gmlp_mixer",
    )(h2, w_in, ln_g.reshape(1, c), ln_b.reshape(1, c), ws, bs_t)


PAIR_W = 2 * HEAD_DIM
PAIRS_PER_KV = N_HEADS // N_KV // 2
NT_DIMS = (((1,), (1,)), ((), ()))
SCORE_SCALE = HEAD_DIM ** -0.5
assert float(np.log2(SCORE_SCALE)).is_integer()


def _swa_project(i, h_ref, wq_ref, wkv_ref, wbf):
    nq = N_HEADS * HEAD_DIM

    @pl.when(i == 0)
    def _():
        wbf[:, :nq] = wq_ref[...].astype(BF16)
        wbf[:, nq:] = wkv_ref[...].astype(BF16)

    return jnp.dot(h_ref[...], wbf[...], preferred_element_type=F32)


def _pair_block_diag(a, a_swapped, hk, axis):
    dim_axis = 1 - axis
    low = lax.broadcasted_iota(I32, a.shape, dim_axis) < HEAD_DIM
    lo, hi = (a, a_swapped) if hk == 0 else (a_swapped, a)
    return jnp.concatenate([jnp.where(low, lo, 0.0), jnp.where(low, 0.0, hi)], axis=axis).astype(BF16)


def _stack_pairs(qkv, rs, hk, scale=None):
    p0 = hk * PAIRS_PER_KV
    q = jnp.concatenate([qkv[rs, (p0 + pp) * PAIR_W:(p0 + pp + 1) * PAIR_W]
                         for pp in range(PAIRS_PER_KV)], axis=0)
    return (q if scale is None else q * scale).astype(BF16)


def _swa_cached_kernel(h_ref, wq_ref, wkv_ref, kp_ref, vp_ref, bias_ref, sink_ref, yd_ref, k_ref, v_ref,
                       wbf, *, tq):
    i = pl.program_id(0)
    rows = h_ref.shape[0]
    nq = N_HEADS * HEAD_DIM
    nkv = N_KV * HEAD_DIM
    n_blocks = rows // tq
    qkv = _swa_project(i, h_ref, wq_ref, wkv_ref, wbf)
    k_new = qkv[:, nq:nq + nkv]
    v_new = qkv[:, nq + nkv:]
    k_ref[...] = k_new
    v_ref[...] = v_new
    pad = jnp.zeros((WINDOW - tq, nkv), F32)

    scores, vbds = [], []
    for blk in range(n_blocks):
        rs = slice(blk * tq, (blk + 1) * tq)
        kcat = jnp.concatenate([kp_ref[blk], k_new[rs], pad], axis=0)
        vcat = jnp.concatenate([vp_ref[blk], v_new[rs], pad], axis=0)
        kswap = pltpu.roll(kcat, HEAD_DIM, 1)
        vswap = pltpu.roll(vcat, HEAD_DIM, 1)
        per_head = []
        for hk in range(N_KV):
            kbd = _pair_block_diag(kcat, kswap, hk, 0)
            vbds.append(_pair_block_diag(vcat, vswap, hk, 0))
            s4 = lax.dot_general(_stack_pairs(qkv, rs, hk), kbd, NT_DIMS,
                                 preferred_element_type=F32) * (HEAD_DIM ** -0.5)
            for pp in range(PAIRS_PER_KV):
                for sub in range(2):
                    per_head.append(s4[pp * tq:(pp + 1) * tq, sub * 2 * WINDOW:(sub + 1) * 2 * WINDOW])
        scores.append(jnp.concatenate(per_head, axis=0))

    s_all = jnp.stack(scores, axis=0) + bias_ref[...][None]
    sink = sink_ref[...][None]
    m = jnp.maximum(jnp.max(s_all, axis=-1, keepdims=True), sink)
    pr = jnp.exp(s_all - m)
    pr = pr / (jnp.sum(pr, axis=-1, keepdims=True) + jnp.exp(sink - m))

    for blk in range(n_blocks):
        outs = []
        for hk in range(N_KV):
            p4 = []
            for pp in range(PAIRS_PER_KV):
                h0 = 2 * (hk * PAIRS_PER_KV + pp)
                p4.append(jnp.concatenate([pr[blk, h0 * tq:(h0 + 1) * tq, :],
                                           pr[blk, (h0 + 1) * tq:(h0 + 2) * tq, :]], axis=-1))
            o4 = jnp.dot(jnp.concatenate(p4, axis=0).astype(BF16), vbds[blk * N_KV + hk],
                         preferred_element_type=F32)
            outs.extend(o4[pp * tq:(pp + 1) * tq, :] for pp in range(PAIRS_PER_KV))
        yd_ref[blk * tq:(blk + 1) * tq, :] = jnp.concatenate(outs, axis=-1).astype(BF16)


def _swa_stream_kernel(h_ref, wq_ref, wkv_ref, bias_ref, sink_ref, yd_ref, k_ref, v_ref,
                       wbf, kprev, vprev_t, *, blocks_per_seq):
    i = pl.program_id(0)
    rows = h_ref.shape[0]
    nq = N_HEADS * HEAD_DIM
    nkv = N_KV * HEAD_DIM
    tq = WINDOW
    n_blocks = rows // tq

    @pl.when(i == 0)
    def _():
        kprev[...] = jnp.zeros_like(kprev)
        vprev_t[...] = jnp.zeros_like(vprev_t)

    qkv = _swa_project(i, h_ref, wq_ref, wkv_ref, wbf)
    k_new = qkv[:, nq:nq + nkv]
    v_new = qkv[:, nq + nkv:]
    k_ref[...] = k_new
    v_ref[...] = v_new
    v_new_t = v_new.T
    lanes = PAIRS_PER_KV * tq

    for blk in range(n_blocks):
        rs = slice(blk * tq, (blk + 1) * tq)
        first = ((i * n_blocks + blk) % blocks_per_seq == 0).astype(I32)
        k_cur = k_new[rs]
        v_cur_t = v_new_t[:, rs]
        kcat = jnp.concatenate([kprev[...], k_cur], axis=0)
        vcat_t = jnp.concatenate([vprev_t[...], v_cur_t], axis=1)
        kprev[...] = k_cur
        vprev_t[...] = v_cur_t
        kswap = pltpu.roll(kcat, HEAD_DIM, 1)
        vswap_t = pltpu.roll(vcat_t, HEAD_DIM, 0)
        outs = []
        for hk in range(N_KV):
            kbd = _pair_block_diag(kcat, kswap, hk, 0)
            vbd_t = _pair_block_diag(vcat_t, vswap_t, hk, 1)
            st = lax.dot_general(kbd, _stack_pairs(qkv, rs, hk, SCORE_SCALE), NT_DIMS,
                                 preferred_element_type=F32)
            s3 = st.reshape(2, 2 * WINDOW, lanes) + bias_ref[first, hk]
            sink = sink_ref[hk]
            m = jnp.maximum(jnp.max(s3, axis=1, keepdims=True), sink)
            pr = jnp.exp(s3 - m)
            inv = 1.0 / (jnp.sum(pr, axis=1, keepdims=True) + jnp.exp(sink - m))
            o_t = jnp.dot(vbd_t, pr.reshape(4 * WINDOW, lanes).astype(BF16),
                          preferred_element_type=F32)
            norm = jnp.concatenate([jnp.broadcast_to(inv[sub], (HEAD_DIM, lanes)) for sub in range(2)], axis=0)
            o4 = (o_t * norm).T
            outs.extend(o4[pp * tq:(pp + 1) * tq, :] for pp in range(PAIRS_PER_KV))
        yd_ref[rs, :] = jnp.concatenate(outs, axis=-1).astype(BF16)


def _swa_weight_specs(w_in, d):
    nq = N_HEADS * HEAD_DIM
    nkv = N_KV * HEAD_DIM
    nw = nq + 2 * nkv
    q_blk = (w_in.shape[1] - nw) // nq
    kv_blk = (w_in.shape[1] - 2 * nkv) // (2 * nkv)
    assert q_blk * nq + nw == w_in.shape[1] and kv_blk * 2 * nkv + 2 * nkv == w_in.shape[1]
    return [pl.BlockSpec((d, nq), lambda i: (0, q_blk)), pl.BlockSpec((d, 2 * nkv), lambda i: (0, kv_blk))]


def _swa_outputs(rows, rt):
    nq = N_HEADS * HEAD_DIM
    nkv = N_KV * HEAD_DIM
    specs = [pl.BlockSpec((rt, nq), lambda i: (i, 0)),
             pl.BlockSpec((rt, nkv), lambda i: (i, 0)),
             pl.BlockSpec((rt, nkv), lambda i: (i, 0))]
    shapes = [jax.ShapeDtypeStruct((rows, nq), BF16),
              jax.ShapeDtypeStruct((rows, nkv), F32),
              jax.ShapeDtypeStruct((rows, nkv), F32)]
    return specs, shapes


def _swa_cached_mixer(h2, w_in, k_cache, v_cache, bias, sinks, tq):
    rows, d = h2.shape
    nkv = N_KV * HEAD_DIM
    nw = N_HEADS * HEAD_DIM + 2 * nkv
    n_blocks = ROW_TILE // tq
    cache_spec = pl.BlockSpec((n_blocks, WINDOW, nkv), lambda i: (i, 0, 0))
    out_specs, out_shape = _swa_outputs(rows, ROW_TILE)
    return pl.pallas_call(
        functools.partial(_swa_cached_kernel, tq=tq),
        grid=(rows // ROW_TILE,),
        in_specs=[pl.BlockSpec((ROW_TILE, d), lambda i: (i, 0))] + _swa_weight_specs(w_in, d)
        + [cache_spec, cache_spec,
           pl.BlockSpec((N_HEADS * tq, 2 * WINDOW), lambda i: (0, 0)),
           pl.BlockSpec((N_HEADS * tq, 1), lambda i: (0, 0))],
        out_specs=out_specs,
        out_shape=out_shape,
        scratch_shapes=[pltpu.VMEM((d, nw), BF16)],
        compiler_params=_cparams(1),
        name="swa_cached",
    )(h2, w_in, w_in, k_cache, v_cache, bias, sinks)


def _swa_stream_mixer(h2, w_in, bias_t, sinks_t, blocks_per_seq):
    rows, d = h2.shape
    nkv = N_KV * HEAD_DIM
    nw = N_HEADS * HEAD_DIM + 2 * nkv
    lanes = PAIRS_PER_KV * WINDOW
    rt = min(rows, MATMUL_TILE)
    out_specs, out_shape = _swa_outputs(rows, rt)
    return pl.pallas_call(
        functools.partial(_swa_stream_kernel, blocks_per_seq=blocks_per_seq),
        grid=(rows // rt,),
        in_specs=[pl.BlockSpec((rt, d), lambda i: (i, 0))] + _swa_weight_specs(w_in, d)
        + [pl.BlockSpec((2, N_KV, 2, 2 * WINDOW, lanes), lambda i: (0, 0, 0, 0, 0)),
           pl.BlockSpec((N_KV, 2, 1, lanes), lambda i: (0, 0, 0, 0))],
        out_specs=out_specs,
        out_shape=out_shape,
        scratch_shapes=[pltpu.VMEM((d, nw), BF16), pltpu.VMEM((WINDOW, nkv), F32),
                        pltpu.VMEM((nkv, WINDOW), F32)],
        compiler_params=_cparams(1),
        name="swa_stream",
    )(h2, w_in, w_in, bias_t, sinks_t)


def _t5_bucket(dist):
    max_exact = N_BUCKETS // 2
    dd = np.maximum(dist, 1)
    large = max_exact + (np.log(dd / max_exact) / np.log(WINDOW / max_exact)
                         * (N_BUCKETS - max_exact)).astype(np.int64)
    large = np.minimum(large, N_BUCKETS - 1)
    return np.where(dist < max_exact, dist, large).astype(np.int32)


def _attention_bias(rel_bias):
    by_dist = jnp.take(rel_bias.astype(F32), _t5_bucket(np.arange(WINDOW)), axis=0).T
    neg = jnp.full((N_HEADS, WINDOW), NEG_INF, F32)
    line = jnp.concatenate([neg, by_dist[:, ::-1], neg[:, :WINDOW - 1]], axis=1)
    rows = line[:, None, :]
    span = 1
    while span < WINDOW:
        rows = jnp.concatenate([rows[:, :, span:], rows[:, :, :rows.shape[2] - span]], axis=1)
        span *= 2
    return rows


def kernel(x_prompt, x_sample, state_pool, state_conv, cache_swa_k, cache_swa_v, c_prompt, c_sample, w_ada, b_ada, norm_g, final_norm_g, w_in_even, w_out_even, w_pool, pool_scale, conv_w, w_in_odd, w_out_odd, gm_norm_g, gm_norm_b, gm_w_s, gm_b_s, attn_sinks, rel_bias, w_router, b_router, w_gate, w_up, w_down):
    d = D_MODEL
    bp, tp, _ = x_prompt.shape
    bs, ts, _ = x_sample.shape
    rows_s = bs * ts
    assert rows_s == ROW_TILE and tp % ROUTER_TILE == 0 and PAST_LEN % CHUNK == 0
    assert bp <= V7X_SUBLANES and CHUNK % ts == 0

    n_c = bp + bs
    c_pad = (-n_c) % V7X_SUBLANES
    c_all = jnp.concatenate([c_prompt, c_sample, jnp.zeros((c_pad, d), F32)], axis=0)
    mod_p = _adaln(c_all, w_ada, b_ada)
    mod_s = jnp.repeat(mod_p[:, bp:bp + bs], ts, axis=1)

    xp = x_prompt.reshape(bp * tp, d)
    xs_ = x_sample.reshape(rows_s, d)
    w_in0, w_in1 = w_in_even[0], w_in_odd[0]

    hp0, ya_p, pool_p = _pool_mixer(xp, norm_g[0, 0], mod_p, 0, w_in0, w_pool[0], pool_scale[0],
                                    None, bp, 1, MATMUL_TILE, 0)
    hs0, ya_s, pool_s = _pool_mixer(xs_, norm_g[0, 0], mod_s, 0, w_in0, w_pool[0], pool_scale[0],
                                    state_pool[0], bs, bs, ts, PAST_LEN)
    yb_p, conv_p = _conv_mixer(hp0, w_in0, conv_w[0], None, bp, 1, MATMUL_TILE)
    yb_s, conv_s = _conv_mixer(hs0, w_in0, conv_w[0], state_conv[0], bs, bs, ts)
    out_p = _outproj(ya_p, yb_p, xp, mod_p, 0, norm_g[0, 1], w_out_even[0], tp)
    out_s = _outproj(ya_s, yb_s, xs_, mod_s, 0, norm_g[0, 1], w_out_even[0], rows_s)
    (x2p, h1p), (x2s, h1s) = _moe(out_p, out_s, mod_p, mod_s, 0, norm_g[1, 0], tp,
                                  w_router, b_router, w_gate, w_up, w_down, final=False)

    bs_t = gm_b_s[0].T
    (yc_p,) = _gmlp_mixer(h1p, w_in1, gm_norm_g[0], gm_norm_b[0], gm_w_s[0], bs_t, CHUNK, CHUNK, False)
    yc_s, gv_s = _gmlp_mixer(h1s, w_in1, gm_norm_g[0], gm_norm_b[0], gm_w_s[0], bs_t, rows_s, ts, True)
    bias = _attention_bias(rel_bias)
    nkv = N_KV * HEAD_DIM
    bias_t = jnp.transpose(bias.reshape(N_KV, PAIRS_PER_KV, 2, WINDOW, 2 * WINDOW), (0, 2, 4, 1, 3))
    bias_t = bias_t.reshape(N_KV, 2, 2 * WINDOW, PAIRS_PER_KV * WINDOW)
    before_start = (np.arange(2 * WINDOW) < WINDOW)[None, None, :, None]
    bias_t = jnp.stack([bias_t, jnp.where(before_start, NEG_INF, bias_t)], axis=0)
    sinks_t = jnp.transpose(attn_sinks[0].reshape(N_KV, PAIRS_PER_KV, 2), (0, 2, 1))
    sinks_t = jnp.repeat(sinks_t, WINDOW, axis=-1).reshape(N_KV, 2, 1, PAIRS_PER_KV * WINDOW)
    yd_p, k_p, v_p = _swa_stream_mixer(h1p, w_in1, bias_t, sinks_t, tp // WINDOW)
    yd_s, k_s, v_s = _swa_cached_mixer(h1s, w_in1, cache_swa_k[0].reshape(bs, WINDOW, nkv),
                                       cache_swa_v[0].reshape(bs, WINDOW, nkv),
                                       bias[:, :ts, :].reshape(N_HEADS * ts, 2 * WINDOW),
                                       jnp.repeat(attn_sinks[0], ts).reshape(-1, 1), ts)
    out_p = _outproj(yc_p, yd_p, x2p, mod_p, 1, norm_g[1, 1], w_out_odd[0], tp)
    out_s = _outproj(yc_s, yd_s, x2s, mod_s, 1, norm_g[1, 1], w_out_odd[0], rows_s)
    (yp,), (ys_out,) = _moe(out_p, out_s, mod_p, mod_s, 1, final_norm_g, tp,
                            w_router, b_router, w_gate, w_up, w_down, final=True)

    k_p4 = k_p.reshape(bp, tp, nkv)[:, -WINDOW:].reshape(bp, WINDOW, N_KV, HEAD_DIM)
    v_p4 = v_p.reshape(bp, tp, nkv)[:, -WINDOW:].reshape(bp, WINDOW, N_KV, HEAD_DIM)
    k_s4 = jnp.concatenate([cache_swa_k[0], k_s.reshape(bs, ts, N_KV, HEAD_DIM)], axis=1)[:, -WINDOW:]
    v_s4 = jnp.concatenate([cache_swa_v[0], v_s.reshape(bs, ts, N_KV, HEAD_DIM)], axis=1)[:, -WINDOW:]
    return (yp.reshape(bp, tp, d), ys_out.reshape(bs, ts, d),
            pool_p[None], pool_s[None], conv_p[None], conv_s[None],
            k_p4[None], k_s4[None], v_p4[None], v_s4[None],
            gv_s.reshape(bs, ts, GM_WIDTH)[None])
```

```python
import functools

import numpy as np
import jax
import jax.numpy as jnp
from jax import lax
from jax.experimental import pallas as pl
from jax.experimental.pallas import tpu as pltpu

F32 = jnp.float32
BF16 = jnp.bfloat16
I32 = jnp.int32
U32 = jnp.uint32

D_MODEL = 2048
POOL_WINDOWS = (2, 4, 8, 16)
POOL_WIDTH = 1024
POOL_GROUP = 256
POOL_STATE = 15
CONV_WIDTH = 1024
CONV_K = 3
GM_WIDTH = 1024
GM_GROUPS = 8
GM_GROUP = 128
CHUNK = 128
HEAD_DIM = 64
N_HEADS = 16
N_KV = 2
WINDOW = 128
N_BUCKETS = 32
N_EXPERTS = 16
N_EXPERT_GROUPS = 4
EXP_PER_GROUP = 4
TOP_K = 2
EPS = 1e-6
NEG_INF = -1e30
PAST_LEN = 16384

V7X_SUBLANES = 8
V7X_LANES = 128
VMEM_LIMIT = 56 * 1024 * 1024

ROW_TILE = 256
MATMUL_TILE = 512
ROUTER_TILE = 1024
POOL_HALO = 16
CONV_HALO = 8
MOE_TILE = 256
XS_RING = 3
ADALN_COL_TILE = 1024
CONV_COL_TILE = 512
TAB_EXPERT, TAB_VALID, TAB_LAST_TILE, TAB_NUSED, TAB_NEXT = 0, 1, 2, 3, 4


def _cparams(n_axes):
    return pltpu.CompilerParams(dimension_semantics=("arbitrary",) * n_axes,
                                vmem_limit_bytes=VMEM_LIMIT)


def _rms(x, g):
    return x * lax.rsqrt(jnp.mean(x * x, axis=-1, keepdims=True) + EPS) * g


def _mod_spec(mod, layer, part):
    nrow = ROW_TILE if mod.shape[1] == ROW_TILE else V7X_SUBLANES
    return pl.BlockSpec((1, nrow, D_MODEL), lambda *_: (layer, 0, part))


def _mod_rows(m_ref, seq):
    if m_ref.shape[1] == V7X_SUBLANES:
        return m_ref[0, pl.ds(seq, 1), :]
    return m_ref[0]


def _adaln_kernel(c_ref, w_ref, b_ref, o_ref):
    c = c_ref[...]
    a = (c * jax.nn.sigmoid(c)).astype(BF16)
    o_ref[0] = jnp.dot(a, w_ref[0].astype(BF16), preferred_element_type=F32) + b_ref[0]


def _adaln(c_all, w_ada, b_ada):
    depth, d, n6 = w_ada.shape
    m = c_all.shape[0]
    tn = ADALN_COL_TILE
    return pl.pallas_call(
        _adaln_kernel,
        grid=(depth, n6 // tn),
        in_specs=[pl.BlockSpec((m, d), lambda l, j: (0, 0)),
                  pl.BlockSpec((1, d, tn), lambda l, j: (l, 0, j)),
                  pl.BlockSpec((1, 1, tn), lambda l, j: (l, 0, j))],
        out_specs=pl.BlockSpec((1, m, tn), lambda l, j: (l, 0, j)),
        out_shape=jax.ShapeDtypeStruct((depth, m, n6), F32),
        compiler_params=_cparams(2),
        name="adaln",
    )(c_all, w_ada, b_ada.reshape(depth, 1, n6))


def _pool_kernel(x_ref, g_ref, sc_ref, sh_ref, w_ref, wp_ref, ps_ref, st_ref, h_ref, ya_ref, ns_ref,
                 wbf, wpbf, carry, *, nb, tm, tiles_per_seq, start):
    i = pl.program_id(0)
    t = i % tiles_per_seq
    seq = i // tiles_per_seq
    c = POOL_WIDTH
    halo = POOL_HALO

    @pl.when(i == 0)
    def _():
        wbf[...] = w_ref[...].astype(BF16)
        wpbf[...] = wp_ref[...].astype(BF16)

    @pl.when(t == 0)
    def _():
        carry[...] = st_ref[...]

    h = (_rms(x_ref[...], g_ref[...]) * (1.0 + _mod_rows(sc_ref, seq)) + _mod_rows(sh_ref, seq)).astype(BF16)
    h_ref[...] = h
    p = jnp.dot(h, wbf[...], preferred_element_type=F32)
    p3 = p.reshape(nb, tm, c)
    ext3 = jnp.concatenate([carry[...], p3], axis=1)
    tail = ext3[:, tm:tm + halo, :]
    ns_ref[...] = tail[:, halo - POOL_STATE:, :]
    carry[...] = tail
    ext = ext3.reshape(nb * (halo + tm), c)
    pos = start + t * tm + lax.broadcasted_iota(I32, (1, tm, 1), 1)
    outs = []
    for gi, w in enumerate(POOL_WINDOWS):
        sl = slice(gi * POOL_GROUP, (gi + 1) * POOL_GROUP)
        acc = ext[:, sl]
        shift = 1
        while shift < w:
            acc = acc + pltpu.roll(acc, shift, 0)
            shift *= 2
        win = acc.reshape(nb, halo + tm, POOL_GROUP)[:, halo:, :]
        cnt = jnp.minimum(pos + 1, w).astype(F32)
        dgrp = win / cnt - p3[:, :, sl]
        outs.append(jnp.dot(dgrp.reshape(nb * tm, POOL_GROUP).astype(BF16), wpbf[gi],
                            preferred_element_type=F32))
    y = jnp.concatenate(outs, axis=-1) * ps_ref[...]
    ya_ref[...] = y.astype(BF16)


def _pool_mixer(x2, g, mod, layer, w_in, w_pool, pool_scale, state, nseq, nb, tm, start):
    rows, d = x2.shape
    tiles_per_seq = (rows // nseq) // tm
    seq_blocks = nseq // nb
    c = POOL_WIDTH
    if state is None:
        st = jnp.zeros((nseq, POOL_HALO, c), F32)
    else:
        st = jnp.pad(state, ((0, 0), (POOL_HALO - POOL_STATE, 0), (0, 0)))
    kern = functools.partial(_pool_kernel, nb=nb, tm=tm, tiles_per_seq=tiles_per_seq, start=start)
    h2, ya, ns = pl.pallas_call(
        kern,
        grid=(seq_blocks * tiles_per_seq,),
        in_specs=[pl.BlockSpec((nb * tm, d), lambda i: (i, 0)),
                  pl.BlockSpec((1, d), lambda i: (0, 0)),
                  _mod_spec(mod, layer, 1), _mod_spec(mod, layer, 0),
                  pl.BlockSpec((d, c), lambda i: (0, 0)),
                  pl.BlockSpec((len(POOL_WINDOWS), POOL_GROUP, POOL_GROUP), lambda i: (0, 0, 0)),
                  pl.BlockSpec((1, c), lambda i: (0, 0)),
                  pl.BlockSpec((nb, POOL_HALO, c), lambda i: (i // tiles_per_seq, 0, 0))],
        out_specs=[pl.BlockSpec((nb * tm, d), lambda i: (i, 0)),
                   pl.BlockSpec((nb * tm, c), lambda i: (i, 0)),
                   pl.BlockSpec((nb, POOL_STATE, c), lambda i: (i // tiles_per_seq, 0, 0))],
        out_shape=[jax.ShapeDtypeStruct((rows, d), BF16),
                   jax.ShapeDtypeStruct((rows, c), BF16),
                   jax.ShapeDtypeStruct((nseq, POOL_STATE, c), F32)],
        scratch_shapes=[pltpu.VMEM((d, c), BF16),
                        pltpu.VMEM((len(POOL_WINDOWS), POOL_GROUP, POOL_GROUP), BF16),
                        pltpu.VMEM((nb, POOL_HALO, c), F32)],
        compiler_params=_cparams(1),
        name="pool_mixer",
    )(x2, g.reshape(1, d), mod, mod, w_in, w_pool, pool_scale.reshape(1, c), st)
    return h2, ya, ns


def _conv_kernel(h_ref, wx_ref, wb_ref, wc_ref, cw_ref, st_ref, yb_ref, ns_ref,
                 wxbf, wbbf, wcbf, carry, *, nb, tm, tiles_per_seq):
    i = pl.program_id(1)
    t = i % tiles_per_seq
    tc = wxbf.shape[1]
    halo = CONV_HALO

    @pl.when(i == 0)
    def _():
        wxbf[...] = wx_ref[...].astype(BF16)
        wbbf[...] = wb_ref[...].astype(BF16)
        wcbf[...] = wc_ref[...].astype(BF16)

    @pl.when(t == 0)
    def _():
        carry[...] = st_ref[...]

    h = h_ref[...]
    xin = jnp.dot(h, wxbf[...], preferred_element_type=F32)
    gb = jnp.dot(h, wbbf[...], preferred_element_type=F32)
    gc = jnp.dot(h, wcbf[...], preferred_element_type=F32)
    z3 = (gc * xin).reshape(nb, tm, tc)
    ext3 = jnp.concatenate([carry[...], z3], axis=1)
    tail = ext3[:, tm:tm + halo, :]
    ns_ref[...] = tail[:, halo - (CONV_K - 1):, :]
    carry[...] = tail
    ext = ext3.reshape(nb * (halo + tm), tc)
    cw = cw_ref[...]
    conv = cw[0:1, :] * pltpu.roll(ext, 2, 0) + cw[1:2, :] * pltpu.roll(ext, 1, 0) + cw[2:3, :] * ext
    conv = conv.reshape(nb, halo + tm, tc)[:, halo:, :].reshape(nb * tm, tc)
    yb_ref[...] = (gb * conv).astype(BF16)


def _conv_mixer(h2, w_in, conv_w, state, nseq, nb, tm):
    rows, d = h2.shape
    tiles_per_seq = (rows // nseq) // tm
    seq_blocks = nseq // nb
    c = CONV_WIDTH
    tc = CONV_COL_TILE
    cb = c // tc
    base = POOL_WIDTH // tc
    if state is None:
        st = jnp.zeros((nseq, CONV_HALO, c), F32)
    else:
        st = jnp.pad(state, ((0, 0), (CONV_HALO - (CONV_K - 1), 0), (0, 0)))
    kern = functools.partial(_conv_kernel, nb=nb, tm=tm, tiles_per_seq=tiles_per_seq)
    yb, ns = pl.pallas_call(
        kern,
        grid=(cb, seq_blocks * tiles_per_seq),
        in_specs=[pl.BlockSpec((nb * tm, d), lambda j, i: (i, 0)),
                  pl.BlockSpec((d, tc), lambda j, i: (0, base + j)),
                  pl.BlockSpec((d, tc), lambda j, i: (0, base + cb + j)),
                  pl.BlockSpec((d, tc), lambda j, i: (0, base + 2 * cb + j)),
                  pl.BlockSpec((CONV_K, tc), lambda j, i: (0, j)),
                  pl.BlockSpec((nb, CONV_HALO, tc), lambda j, i: (i // tiles_per_seq, 0, j))],
        out_specs=[pl.BlockSpec((nb * tm, tc), lambda j, i: (i, j)),
                   pl.BlockSpec((nb, CONV_K - 1, tc), lambda j, i: (i // tiles_per_seq, 0, j))],
        out_shape=[jax.ShapeDtypeStruct((rows, c), BF16),
                   jax.ShapeDtypeStruct((nseq, CONV_K - 1, c), F32)],
        scratch_shapes=[pltpu.VMEM((d, tc), BF16)] * 3 + [pltpu.VMEM((nb, CONV_HALO, tc), F32)],
        compiler_params=_cparams(2),
        name="conv_mixer",
    )(h2, w_in, w_in, w_in, conv_w, st)
    return yb, ns


def _pack_bf16_pairs(v):
    c = v.shape[1] // 2
    return pltpu.bitcast(pltpu.pack_elementwise([v[:, :c], v[:, c:]], packed_dtype=BF16), U32)


def _store_token_tiles(ref, v):
    rows = v.shape[0]
    for j in range(V7X_SUBLANES):
        ref[pl.ds(j, rows, stride=V7X_SUBLANES), :] = v[:, j * V7X_LANES:(j + 1) * V7X_LANES]


def _load_token_tiles(ref):
    rows = ref.shape[0] // V7X_SUBLANES
    return jnp.concatenate([ref[pl.ds(j, rows, stride=V7X_SUBLANES), :] for j in range(V7X_SUBLANES)],
                           axis=-1)


def _unpack_pairs_f32(w):
    return tuple(pltpu.unpack_elementwise(w, index=k, packed_dtype=BF16, unpacked_dtype=F32) for k in range(2))


def _unpack_bf16_pairs(w):
    lo, hi = _unpack_pairs_f32(w)
    return lo.astype(BF16), hi.astype(BF16)


def _outproj_kernel(ya_ref, yb_ref, x_hbm, g1_ref, sc_ref, sh_ref, ng_ref, wo_ref,
                    x1_ref, hp_ref, hpt_ref, wobf, xring, xsems, *, tiles_per_seq, n_steps):
    i = pl.program_id(0)
    seq = i // tiles_per_seq
    depth, rt, _ = xring.shape

    def x_copy(j):
        j = jnp.asarray(j, I32)
        rows = pl.ds(pl.multiple_of(j * rt, rt), rt)
        return pltpu.make_async_copy(x_hbm.at[rows, :], xring.at[j % depth], xsems.at[j % depth])

    @pl.when(i == 0)
    def _():
        wobf[...] = wo_ref[...].astype(BF16)
        for j in range(min(depth - 1, n_steps)):
            x_copy(j).start()

    @pl.when(i + depth - 1 < n_steps)
    def _():
        x_copy(i + depth - 1).start()

    x_copy(i).wait()
    ycat = jnp.concatenate([ya_ref[...], yb_ref[...]], axis=-1)
    y = jnp.dot(ycat, wobf[...], preferred_element_type=F32)
    x1 = xring[i % depth] + _mod_rows(g1_ref, seq) * y
    x1_ref[...] = x1
    h2 = _rms(x1, ng_ref[...]) * (1.0 + _mod_rows(sc_ref, seq)) + _mod_rows(sh_ref, seq)
    packed = _pack_bf16_pairs(h2)
    hp_ref[...] = packed
    _store_token_tiles(hpt_ref, packed)


def _outproj(ya, yb, x2, mod, layer, ng, w_out, seq_rows):
    rows_all, d = x2.shape
    half = ya.shape[1]
    rt = ROW_TILE
    row_spec = lambda w: pl.BlockSpec((rt, w), lambda i: (i, 0))
    return pl.pallas_call(
        functools.partial(_outproj_kernel, tiles_per_seq=seq_rows // rt, n_steps=rows_all // rt),
        grid=(rows_all // rt,),
        in_specs=[row_spec(half), row_spec(half), pl.BlockSpec(memory_space=pl.ANY),
                  _mod_spec(mod, layer, 2), _mod_spec(mod, layer, 4), _mod_spec(mod, layer, 3),
                  pl.BlockSpec((1, d), lambda i: (0, 0)),
                  pl.BlockSpec((d, d), lambda i: (0, 0), pipeline_mode=pl.Buffered(1))],
        out_specs=[row_spec(d), row_spec(d // 2),
                   pl.BlockSpec((rt * V7X_SUBLANES, V7X_LANES), lambda i: (i, 0))],
        out_shape=[jax.ShapeDtypeStruct((rows_all, d), F32),
                   jax.ShapeDtypeStruct((rows_all, d // 2), U32),
                   jax.ShapeDtypeStruct((rows_all * V7X_SUBLANES, V7X_LANES), U32)],
        scratch_shapes=[pltpu.VMEM((d, d), BF16), pltpu.VMEM((XS_RING, rt, d), F32),
                        pltpu.SemaphoreType.DMA((XS_RING,))],
        compiler_params=_cparams(1),
        name="outproj",
    )(ya, yb, x2, mod, mod, mod, ng.reshape(1, d), w_out)


def _router_kernel(hpp_ref, hps_ref, wr_ref, br_ref, pos_ref, rw_ref, tab_ref,
                   cnt_acc, totals, starts, padded, before_ref, s_all, sel_all, *, nt_p, rows_s):
    ph = pl.program_id(0)
    t = pl.program_id(1)
    last = nt_p
    r = hpp_ref.shape[0]
    half = hpp_ref.shape[1]
    ne = N_EXPERTS
    sub = lax.broadcasted_iota(I32, (ne, V7X_LANES), 0)

    @pl.when(t == 0)
    def _():
        cnt_acc[...] = jnp.zeros_like(cnt_acc)

    @pl.when((ph == 0) & (t == 0))
    def _():
        starts[...] = jnp.zeros_like(starts)
        padded[...] = jnp.zeros_like(padded)

    @pl.when((ph == 1) & (t == 0))
    def _():
        pad = jnp.floor((totals[...] + (MOE_TILE - 1.0)) * (1.0 / MOE_TILE)) * MOE_TILE
        run = pad
        k = 1
        while k < ne:
            run = run + jnp.where(sub >= k, pltpu.roll(run, k, 0), 0.0)
            k *= 2
        padded[...] = pad
        starts[...] = run - pad

    is_s = t == last
    eid = lax.broadcasted_iota(I32, (ne, r), 0)
    n_valid = jnp.where(is_s, rows_s, r)
    tok = lax.broadcasted_iota(I32, (ne, r), 1)

    @pl.when(ph == 0)
    def _():
        w_s = jnp.concatenate([hps_ref[...], jnp.zeros((r - rows_s, half), U32)], axis=0)
        w = jnp.where(is_s, w_s, hpp_ref[...])
        lo, hi = _unpack_bf16_pairs(w)
        wr = wr_ref[...].astype(BF16)
        log_t = (lax.dot_general(wr[:, :half], lo, NT_DIMS, preferred_element_type=F32)
                 + lax.dot_general(wr[:, half:], hi, NT_DIMS, preferred_element_type=F32))

        s = jax.nn.sigmoid(log_t)
        sg = s + br_ref[...]
        within = eid % EXP_PER_GROUP
        grp = eid // EXP_PER_GROUP

        def group_rot(x, k):
            return jnp.where(within + k < EXP_PER_GROUP,
                             pltpu.roll(x, ne - k, 0), pltpu.roll(x, EXP_PER_GROUP - k, 0))

        rank = jnp.zeros((ne, r), I32)
        for k in range(1, EXP_PER_GROUP):
            mate = group_rot(sg, k)
            wrapped = within + k >= EXP_PER_GROUP
            ahead = (mate > sg) | (wrapped & (mate == sg))
            rank = rank + ahead.astype(I32)
        top2 = rank < TOP_K
        kept = jnp.where(top2, sg, 0.0)
        gscore = kept
        for k in range(1, EXP_PER_GROUP):
            gscore = gscore + group_rot(kept, k)
        win = None
        for k in range(1, N_EXPERT_GROUPS):
            other = pltpu.roll(gscore, EXP_PER_GROUP * k, 0)
            beats = (gscore > other) | ((grp < k) & (gscore == other))
            win = beats if win is None else (win & beats)
        picked_now = top2 & win & (tok < n_valid)
        s_all[t] = s
        sel_all[t] = picked_now.astype(F32)

    s = s_all[t]
    selb = sel_all[t]
    sel = selb > 0.5
    cnt_before = cnt_acc[...]
    cnt_new = cnt_before + jnp.sum(selb, axis=1, keepdims=True)
    cnt_acc[...] = cnt_new

    @pl.when((ph == 0) & (t == 0))
    def _():
        src = lax.broadcasted_iota(I32, (r, r), 0)
        dst = lax.broadcasted_iota(I32, (r, r), 1)
        before_ref[...] = (src < dst).astype(BF16)

    @pl.when((ph == 0) & (t == last))
    def _():
        totals[...] = cnt_new

    @pl.when(ph == 1)
    def _():
        picked = jnp.where(sel, s, 0.0)
        wsum = jnp.sum(picked, axis=0, keepdims=True)
        gate = picked / jnp.where(tok[0:1, :] < n_valid, wsum, 1.0)
        ranks = jnp.dot(selb.astype(BF16), before_ref[...], preferred_element_type=F32)
        slot = (starts[...][:, 0:1] + cnt_before[:, 0:1] + ranks).astype(I32)
        e_a = jnp.min(jnp.where(sel, eid, ne), axis=0, keepdims=True)
        e_b = jnp.max(jnp.where(sel, eid, -1), axis=0, keepdims=True)
        is_a = sel & (eid == e_a)
        is_b = sel & (eid == e_b)
        pos_a = jnp.sum(jnp.where(is_a, slot, 0), axis=0, keepdims=True)
        pos_b = jnp.sum(jnp.where(is_b, slot, 0), axis=0, keepdims=True)
        w_a = jnp.sum(jnp.where(is_a, gate, 0.0), axis=0, keepdims=True)
        w_b = jnp.sum(jnp.where(is_b, gate, 0.0), axis=0, keepdims=True)
        pos_ref[0] = jnp.concatenate([pos_a, pos_b], axis=0)
        wmat = jnp.concatenate([w_a, w_b, jnp.zeros((V7X_LANES - 2, r), F32)], axis=0)
        rw_ref[...] = wmat.T

    @pl.when((ph == 1) & (t == last))
    def _():
        ends = starts[...] + padded[...]
        lane = lax.broadcasted_iota(I32, (ne, V7X_LANES), 1)
        tile_start = (lane * MOE_TILE).astype(F32)
        te = jnp.sum((tile_start >= ends).astype(I32), axis=0, keepdims=True)
        valid = te < ne
        last_e = jnp.max(jnp.where(padded[...] > 0.0, sub, 0), axis=0, keepdims=True)
        te = jnp.where(valid, te, last_e)
        n_used = jnp.sum(valid.astype(I32), axis=1, keepdims=True) + jnp.zeros((1, V7X_LANES), I32)
        last_tile = jnp.where(padded[...] > 0.0, ends - MOE_TILE, -1.0).astype(I32)
        last_tile_row = jnp.sum(jnp.where(sub == lane, last_tile, 0), axis=0, keepdims=True)
        later = jnp.min(jnp.where((sub > te) & (padded[...] > 0.0), sub, ne), axis=0, keepdims=True)
        next_e = jnp.where(later < ne, later, -1)
        zero = jnp.zeros((1, V7X_LANES), I32)
        tab_ref[...] = jnp.concatenate([te, valid.astype(I32), last_tile_row, n_used, next_e,
                                        zero, zero, zero], axis=0)


def _router(hp_p, hp_s, w_router, b_router):
    n_p, half = hp_p.shape
    rows_s = hp_s.shape[0]
    r = ROUTER_TILE
    nt_p = n_p // r
    nt = nt_p + 1
    kern = functools.partial(_router_kernel, nt_p=nt_p, rows_s=rows_s)
    pos, rw, tab = pl.pallas_call(
        kern,
        grid=(2, nt),
        in_specs=[pl.BlockSpec((r, half), lambda p, t: (jnp.minimum(t, nt_p - 1) * (1 - p), 0)),
                  pl.BlockSpec((rows_s, half), lambda p, t: (0, 0)),
                  pl.BlockSpec((N_EXPERTS, 2 * half), lambda p, t: (0, 0)),
                  pl.BlockSpec((N_EXPERTS, 1), lambda p, t: (0, 0))],
        out_specs=[pl.BlockSpec((1, TOP_K, r), lambda p, t: (p * t, 0, 0)),
                   pl.BlockSpec((r, V7X_LANES), lambda p, t: (p * t, 0)),
                   pl.BlockSpec((V7X_SUBLANES, V7X_LANES), lambda p, t: (0, 0))],
        out_shape=[jax.ShapeDtypeStruct((nt, TOP_K, r), I32),
                   jax.ShapeDtypeStruct((nt * r, V7X_LANES), F32),
                   jax.ShapeDtypeStruct((V7X_SUBLANES, V7X_LANES), I32)],
        scratch_shapes=[pltpu.VMEM((N_EXPERTS, V7X_LANES), F32)] * 4 + [pltpu.VMEM((r, r), BF16)]
        + [pltpu.VMEM((nt, N_EXPERTS, r), F32)] * 2,
        compiler_params=_cparams(2),
        name="router",
    )(hp_p, hp_s, w_router.T, b_router.reshape(N_EXPERTS, 1))
    return pos.reshape(-1), rw, tab.reshape(-1)


def _pos_index(tok0):
    return (tok0 // ROUTER_TILE) * (TOP_K * ROUTER_TILE) + tok0 % ROUTER_TILE


def _tokens(ref, first, n=1):
    start = pl.multiple_of(first * V7X_SUBLANES, V7X_SUBLANES)
    return ref.at[pl.ds(start, n * V7X_SUBLANES), :]


def _dispatch_kernel(pos_ref, tab_ref, hpp_ref, hps_ref, xs_ref, zbuf, sem, *, n_p_steps):
    i = pl.program_id(0)
    rows = hpp_ref.shape[0] // V7X_SUBLANES

    @pl.when(i == 0)
    def _():
        zbuf[...] = jnp.zeros_like(zbuf)

        def fill(e):
            first = pl.multiple_of(tab_ref[TAB_LAST_TILE * V7X_LANES + e], MOE_TILE)
            return pltpu.make_async_copy(zbuf, _tokens(xs_ref, first, MOE_TILE), sem)

        for e in range(N_EXPERTS):
            @pl.when(tab_ref[TAB_LAST_TILE * V7X_LANES + e] >= 0)
            def _():
                fill(e).start()
        for e in range(N_EXPERTS):
            @pl.when(tab_ref[TAB_LAST_TILE * V7X_LANES + e] >= 0)
            def _():
                fill(e).wait()

        def tail(j):
            first = pl.multiple_of(j * MOE_TILE, MOE_TILE)
            return pltpu.make_async_copy(zbuf, _tokens(xs_ref, first, MOE_TILE), sem)

        def tail_start(j, carry):
            tail(j).start()
            return carry

        def tail_wait(j, carry):
            tail(j).wait()
            return carry

        n_used = tab_ref[TAB_NUSED * V7X_LANES]
        n_tiles = xs_ref.shape[0] // (MOE_TILE * V7X_SUBLANES)
        lax.fori_loop(n_used, n_tiles, tail_start, 0)
        lax.fori_loop(n_used, n_tiles, tail_wait, 0)

    def scatter(src_ref, tok0):
        n = src_ref.shape[0] // V7X_SUBLANES
        base = _pos_index(tok0)

        def row_copy(r, dst):
            return pltpu.make_async_copy(_tokens(src_ref, r), _tokens(xs_ref, dst), sem)

        def issue(r, carry):
            row_copy(r, pos_ref[base + r]).start()
            row_copy(r, pos_ref[base + ROUTER_TILE + r]).start(priority=1)
            return carry

        lax.fori_loop(0, n, issue, 0, unroll=8)
        block = pltpu.make_async_copy(src_ref, _tokens(xs_ref, 0, n), sem)
        for _ in range(TOP_K):
            block.wait()

    @pl.when(i < n_p_steps)
    def _():
        scatter(hpp_ref, i * rows)

    @pl.when(i == n_p_steps)
    def _():
        scatter(hps_ref, n_p_steps * rows)


def _dispatch(pos, tab, hpt_p, hpt_s, n_rows_sorted):
    sub = V7X_SUBLANES
    n_p_steps = hpt_p.shape[0] // (ROUTER_TILE * sub)
    kern = functools.partial(_dispatch_kernel, n_p_steps=n_p_steps)
    blk = (ROUTER_TILE * sub, V7X_LANES)
    return pl.pallas_call(
        kern,
        grid_spec=pltpu.PrefetchScalarGridSpec(
            num_scalar_prefetch=2,
            grid=(n_p_steps + 1,),
            in_specs=[pl.BlockSpec(blk, lambda i, p, t: (jnp.minimum(i, n_p_steps - 1), 0)),
                      pl.BlockSpec(hpt_s.shape, lambda i, p, t: (0, 0))],
            out_specs=pl.BlockSpec(memory_space=pl.ANY),
            scratch_shapes=[pltpu.VMEM((MOE_TILE * sub, V7X_LANES), U32), pltpu.SemaphoreType.DMA(())]),
        out_shape=jax.ShapeDtypeStruct((n_rows_sorted * sub, V7X_LANES), U32),
        compiler_params=_cparams(1),
        name="moe_dispatch",
    )(pos, tab, hpt_p, hpt_s)


def _experts_kernel(tab_ref, xs_hbm, wg_hbm, wu_hbm, wd_hbm, ys_ref,
                    wg32, wu32, wd32, wgbf, wubf, wdbf, slot_ref, sems, xbuf, xsems, *, layer):
    i = pl.program_id(0)
    expert = tab_ref[TAB_EXPERT * V7X_LANES + i]
    prev = tab_ref[TAB_EXPERT * V7X_LANES + jnp.maximum(i - 1, 0)]
    upcoming = tab_ref[TAB_NEXT * V7X_LANES + i]
    n_used = tab_ref[TAB_NUSED * V7X_LANES]
    changed = (i == 0) | (expert != prev)
    depth = xbuf.shape[0]

    def tile_copy(j):
        j = jnp.asarray(j, I32)
        return pltpu.make_async_copy(_tokens(xs_hbm, j * MOE_TILE, MOE_TILE), xbuf.at[j % depth],
                                     xsems.at[j % depth])

    @pl.when(i == 0)
    def _():
        for j in range(depth - 1):
            @pl.when(j < n_used)
            def _():
                tile_copy(j).start()

    @pl.when(i + depth - 1 < n_used)
    def _():
        tile_copy(i + depth - 1).start()

    @pl.when(i < n_used)
    def _():
        tile_copy(i).wait()

    xs_ref = xbuf.at[i % depth]

    def weight_copies(e, slot):
        return (pltpu.make_async_copy(wg_hbm.at[layer, e], wg32.at[slot], sems.at[0, slot]),
                pltpu.make_async_copy(wu_hbm.at[layer, e], wu32.at[slot], sems.at[1, slot]),
                pltpu.make_async_copy(wd_hbm.at[layer, e], wd32.at[slot], sems.at[2, slot]))

    @pl.when(i == 0)
    def _():
        slot_ref[0] = 0
        for cp in weight_copies(expert, 0):
            cp.start()

    @pl.when(changed & (i > 0))
    def _():
        slot_ref[0] = 1 - slot_ref[0]

    def mlp(wg, wu, wd):
        x = jnp.concatenate(_unpack_bf16_pairs(_load_token_tiles(xs_ref)), axis=-1)
        a = jnp.dot(x, wg, preferred_element_type=F32)
        b = jnp.dot(x, wu, preferred_element_type=F32)
        hid = (a * jax.nn.sigmoid(a)) * b
        y = jnp.dot(hid.astype(BF16), wd, preferred_element_type=F32)
        _store_token_tiles(ys_ref, _pack_bf16_pairs(y))

    for slot in range(2):
        @pl.when(changed & (slot_ref[0] == slot))
        def _():
            for cp in weight_copies(expert, slot):
                cp.wait()

            @pl.when(upcoming >= 0)
            def _():
                for cp in weight_copies(upcoming, 1 - slot):
                    cp.start(priority=1)

            wg = wg32[slot].astype(BF16)
            wu = wu32[slot].astype(BF16)
            wd = wd32[slot].astype(BF16)
            wgbf[...] = wg
            wubf[...] = wu
            wdbf[...] = wd
            mlp(wg, wu, wd)

    valid = tab_ref[TAB_VALID * V7X_LANES + i] > 0

    @pl.when(valid & jnp.logical_not(changed))
    def _():
        mlp(wgbf[...], wubf[...], wdbf[...])

    @pl.when(jnp.logical_not(valid))
    def _():
        ys_ref[...] = jnp.zeros_like(ys_ref)


def _experts(tab, xs, w_gate, w_up, w_down, layer):
    sub = V7X_SUBLANES
    _, _, d, f = w_gate.shape
    nt = xs.shape[0] // (MOE_TILE * sub)
    assert nt <= V7X_LANES and d == 2 * sub * V7X_LANES
    blk = (MOE_TILE * sub, V7X_LANES)

    hbm = pl.BlockSpec(memory_space=pl.ANY)
    return pl.pallas_call(
        functools.partial(_experts_kernel, layer=layer),
        grid_spec=pltpu.PrefetchScalarGridSpec(
            num_scalar_prefetch=1,
            grid=(nt,),
            in_specs=[hbm, hbm, hbm, hbm],
            out_specs=pl.BlockSpec(blk, lambda i, t: (i, 0)),
            scratch_shapes=[pltpu.VMEM((2, d, f), F32), pltpu.VMEM((2, d, f), F32), pltpu.VMEM((2, f, d), F32),
                            pltpu.VMEM((d, f), BF16), pltpu.VMEM((d, f), BF16), pltpu.VMEM((f, d), BF16),
                            pltpu.SMEM((1,), I32), pltpu.SemaphoreType.DMA((3, 2)),
                            pltpu.VMEM((XS_RING,) + blk, U32), pltpu.SemaphoreType.DMA((XS_RING,))]),
        out_shape=jax.ShapeDtypeStruct(xs.shape, U32),
        compiler_params=_cparams(1),
        name="moe_experts",
    )(tab, xs, w_gate, w_up, w_down)


def _combine_kernel(pos_ref, ys_ref, rw_ref, ng_ref, *rest, tiles_per_seq, n_p_steps, final):
    n_in = 4
    n_out = 1 if final else 2
    trunk_in = (rest[:n_in], rest[n_in:2 * n_in])
    outs = rest[2 * n_in:2 * n_in + 2 * n_out]
    trunk_out = (outs[:n_out], outs[n_out:])
    buf0, buf1, sems = rest[2 * n_in + 2 * n_out:]
    bufs = (buf0, buf1)
    i = pl.program_id(0)
    n_steps = n_p_steps + 1
    rows = rw_ref.shape[0]

    def gather(step, slot):
        base = _pos_index(step * rows)

        def issue(r, carry):
            for k in range(TOP_K):
                src = pos_ref[base + k * ROUTER_TILE + r]
                pltpu.make_async_copy(_tokens(ys_ref, src), _tokens(bufs[slot].at[k], r),
                                      sems.at[slot]).start(priority=k)
            return carry

        lax.fori_loop(0, rows, issue, 0, unroll=8)

    def drain(slot):
        for k in range(TOP_K):
            pltpu.make_async_copy(_tokens(ys_ref, 0, rows), bufs[slot].at[k], sems.at[slot]).wait()

    def finish(slot, trunk, seq):
        x1_ref, g2_ref, sc_ref, sh_ref = trunk_in[trunk]
        rw = rw_ref[...]
        lo_a, hi_a = _unpack_pairs_f32(_load_token_tiles(bufs[slot].at[0]))
        lo_b, hi_b = _unpack_pairs_f32(_load_token_tiles(bufs[slot].at[1]))
        w_a = rw[:, 0:1]
        w_b = rw[:, 1:2]
        moe = jnp.concatenate([w_a * lo_a + w_b * lo_b, w_a * hi_a + w_b * hi_b], axis=-1)
        x2 = x1_ref[...] + _mod_rows(g2_ref, seq) * moe
        if final:
            trunk_out[trunk][0][...] = _rms(x2, ng_ref[...])
        else:
            trunk_out[trunk][0][...] = x2
            trunk_out[trunk][1][...] = (_rms(x2, ng_ref[...]) * (1.0 + _mod_rows(sc_ref, seq))
                                        + _mod_rows(sh_ref, seq)).astype(BF16)

    @pl.when(i == 0)
    def _():
        gather(0, 0)

    for slot in range(2):
        @pl.when(i % 2 == slot)
        def _():
            @pl.when(i + 1 < n_steps)
            def _():
                gather(i + 1, 1 - slot)

            drain(slot)

            @pl.when(i < n_p_steps)
            def _():
                finish(slot, 0, i // tiles_per_seq)

            @pl.when(i == n_p_steps)
            def _():
                finish(slot, 1, 0)


def _combine(pos, ys, x1_p, x1_s, rw, mod_p, mod_s, layer, ng, seq_rows_p, final):
    n_p, d = x1_p.shape
    rt = ROW_TILE
    assert x1_s.shape[0] == rt and n_p % rt == 0 and ROUTER_TILE % rt == 0
    n_p_steps = n_p // rt
    kern = functools.partial(_combine_kernel, tiles_per_seq=seq_rows_p // rt, n_p_steps=n_p_steps, final=final)
    p_spec = lambda w: pl.BlockSpec((rt, w), lambda i, p: (jnp.minimum(i, n_p_steps - 1), 0))
    s_spec = lambda w: pl.BlockSpec((rt, w), lambda i, p: (0, 0))
    nxt = min(layer + 1, mod_p.shape[0] - 1)

    def trunk_specs(spec, mod):
        return [spec(d), _mod_spec(mod, layer, 5), _mod_spec(mod, nxt, 1), _mod_spec(mod, nxt, 0)]

    out_dtypes = [F32] if final else [F32, BF16]
    out_shape = ([jax.ShapeDtypeStruct((n_p, d), t) for t in out_dtypes]
                 + [jax.ShapeDtypeStruct((rt, d), t) for t in out_dtypes])
    out_specs = [p_spec(d)] * len(out_dtypes) + [s_spec(d)] * len(out_dtypes)
    res = pl.pallas_call(
        kern,
        grid_spec=pltpu.PrefetchScalarGridSpec(
            num_scalar_prefetch=1,
            grid=(n_p_steps + 1,),
            in_specs=[pl.BlockSpec(memory_space=pl.ANY),
                      pl.BlockSpec((rt, V7X_LANES), lambda i, p: (i, 0)),
                      pl.BlockSpec((1, d), lambda i, p: (0, 0))]
            + trunk_specs(p_spec, mod_p) + trunk_specs(s_spec, mod_s),
            out_specs=out_specs,
            scratch_shapes=[pltpu.VMEM((TOP_K, rt * V7X_SUBLANES, V7X_LANES), U32)] * 2
            + [pltpu.SemaphoreType.DMA((2,))]),
        out_shape=out_shape,
        compiler_params=_cparams(1),
        name="moe_combine_final" if final else "moe_combine",
    )(pos, ys, rw, ng.reshape(1, d), x1_p, mod_p, mod_p, mod_p, x1_s, mod_s, mod_s, mod_s)
    return res[:len(out_dtypes)], res[len(out_dtypes):]


def _moe(out_p, out_s, mod_p, mod_s, layer, ng, seq_rows_p, w_router, b_router, w_gate, w_up, w_down, final):
    x1_p, hp_p, hpt_p = out_p
    x1_s, hp_s, hpt_s = out_s
    n_p = x1_p.shape[0]
    n_tok = n_p + x1_s.shape[0]
    max_rows = TOP_K * n_tok + N_EXPERTS * (MOE_TILE - 1)
    n_rows_sorted = -(-max_rows // MOE_TILE) * MOE_TILE
    pos, rw, tab = _router(hp_p, hp_s, w_router, b_router)
    xs = _dispatch(pos, tab, hpt_p, hpt_s, n_rows_sorted)
    ys = _experts(tab, xs, w_gate, w_up, w_down, layer)
    return _combine(pos, ys, x1_p, x1_s, rw, mod_p, mod_s, layer, ng, seq_rows_p, final)


def _gmlp_kernel(h_ref, w_ref, lg_ref, lb_ref, ws_ref, bs_ref, yc_ref, *rest, ell, blk, emit_v):
    if emit_v:
        gv_ref, wbf, wsbf = rest
    else:
        wbf, wsbf = rest
    i = pl.program_id(0)
    rows = h_ref.shape[0]
    c = GM_WIDTH

    @pl.when(i == 0)
    def _():
        wbf[...] = w_ref[...].astype(BF16)
        r = lax.broadcasted_iota(I32, (ell, ell), 0)
        s = lax.broadcasted_iota(I32, (ell, ell), 1)
        keep = (r >= s) & ((r // blk) == (s // blk))
        rsel = (lax.broadcasted_iota(I32, (ell, CHUNK), 0) % blk
                == lax.broadcasted_iota(I32, (ell, CHUNK), 1)).astype(BF16)
        csel = (lax.broadcasted_iota(I32, (CHUNK, ell), 1) % blk
                == lax.broadcasted_iota(I32, (CHUNK, ell), 0)).astype(BF16)
        for g in range(GM_GROUPS):
            wchunk = ws_ref[g].astype(BF16)
            if blk == ell:
                full = wchunk
            else:
                rowsp = jnp.dot(rsel, wchunk, preferred_element_type=F32).astype(BF16)
                full = jnp.dot(rowsp, csel, preferred_element_type=F32).astype(BF16)
            wsbf[g] = jnp.where(keep, full, jnp.zeros_like(full))

    uv = jnp.dot(h_ref[...], wbf[...], preferred_element_type=F32)
    u = uv[:, :c]
    v = uv[:, c:]
    vc = v - jnp.mean(v, axis=-1, keepdims=True)
    vn = vc * lax.rsqrt(jnp.mean(vc * vc, axis=-1, keepdims=True) + EPS) * lg_ref[...] + lb_ref[...]
    if emit_v:
        gv_ref[...] = vn
    vb = vn.astype(BF16)
    bs = bs_ref[...]
    for ch in range(rows // ell):
        rs = slice(ch * ell, (ch + 1) * ell)
        outs = []
        for g in range(GM_GROUPS):
            cs = slice(g * GM_GROUP, (g + 1) * GM_GROUP)
            mixed = jnp.dot(wsbf[g], vb[rs, cs], preferred_element_type=F32)
            mixed = (mixed.reshape(ell // blk, blk, GM_GROUP) + bs[:blk, g:g + 1][None]).reshape(ell, GM_GROUP)
            outs.append(u[rs, cs] * mixed)
        yc_ref[rs, :] = jnp.concatenate(outs, axis=-1).astype(BF16)


def _gmlp_mixer(h2, w_in, ln_g, ln_b, ws, bs_t, ell, blk, emit_v):
    rows, d = h2.shape
    c = GM_WIDTH
    kern = functools.partial(_gmlp_kernel, ell=ell, blk=blk, emit_v=emit_v)
    rt = min(rows, MATMUL_TILE)
    out_specs = [pl.BlockSpec((rt, c), lambda i: (i, 0))]
    out_shape = [jax.ShapeDtypeStruct((rows, c), BF16)]
    if emit_v:
        out_specs.append(pl.BlockSpec((rt, c), lambda i: (i, 0)))
        out_shape.append(jax.ShapeDtypeStruct((rows, c), F32))
    return pl.pallas_call(
        kern,
        grid=(rows // rt,),
        in_specs=[pl.BlockSpec((rt, d), lambda i: (i, 0)),
                  pl.BlockSpec((d, 2 * c), lambda i: (0, 0), pipeline_mode=pl.Buffered(1)),
                  pl.BlockSpec((1, c), lambda i: (0, 0)),
                  pl.BlockSpec((1, c), lambda i: (0, 0)),
                  pl.BlockSpec((GM_GROUPS, CHUNK, CHUNK), lambda i: (0, 0, 0)),
                  pl.BlockSpec((CHUNK, GM_GROUPS), lambda i: (0, 0))],
        out_specs=out_specs,
        out_shape=out_shape,
        scratch_shapes=[pltpu.VMEM((d, 2 * c), BF16), pltpu.VMEM((GM_GROUPS, ell, ell), BF16)],
        compiler_params=_cparams(1),
        name="gmlp_mixer",
    )(h2, w_in, ln_g.reshape(1, c), ln_b.reshape(1, c), ws, bs_t)


PAIR_W = 2 * HEAD_DIM
PAIRS_PER_KV = N_HEADS // N_KV // 2
NT_DIMS = (((1,), (1,)), ((), ()))
SCORE_SCALE = HEAD_DIM ** -0.5
assert float(np.log2(SCORE_SCALE)).is_integer()


def _swa_project(i, h_ref, wq_ref, wkv_ref, wbf):
    nq = N_HEADS * HEAD_DIM

    @pl.when(i == 0)
    def _():
        wbf[:, :nq] = wq_ref[...].astype(BF16)
        wbf[:, nq:] = wkv_ref[...].astype(BF16)

    return jnp.dot(h_ref[...], wbf[...], preferred_element_type=F32)


def _pair_block_diag(a, a_swapped, hk, axis):
    dim_axis = 1 - axis
    low = lax.broadcasted_iota(I32, a.shape, dim_axis) < HEAD_DIM
    lo, hi = (a, a_swapped) if hk == 0 else (a_swapped, a)
    return jnp.concatenate([jnp.where(low, lo, 0.0), jnp.where(low, 0.0, hi)], axis=axis).astype(BF16)


def _stack_pairs(qkv, rs, hk, scale=None):
    p0 = hk * PAIRS_PER_KV
    q = jnp.concatenate([qkv[rs, (p0 + pp) * PAIR_W:(p0 + pp + 1) * PAIR_W]
                         for pp in range(PAIRS_PER_KV)], axis=0)
    return (q if scale is None else q * scale).astype(BF16)


def _swa_cached_kernel(h_ref, wq_ref, wkv_ref, kp_ref, vp_ref, bias_ref, sink_ref, yd_ref, k_ref, v_ref,
                       wbf, *, tq):
    i = pl.program_id(0)
    rows = h_ref.shape[0]
    nq = N_HEADS * HEAD_DIM
    nkv = N_KV * HEAD_DIM
    n_blocks = rows // tq
    qkv = _swa_project(i, h_ref, wq_ref, wkv_ref, wbf)
    k_new = qkv[:, nq:nq + nkv]
    v_new = qkv[:, nq + nkv:]
    k_ref[...] = k_new
    v_ref[...] = v_new
    pad = jnp.zeros((WINDOW - tq, nkv), F32)

    scores, vbds = [], []
    for blk in range(n_blocks):
        rs = slice(blk * tq, (blk + 1) * tq)
        kcat = jnp.concatenate([kp_ref[blk], k_new[rs], pad], axis=0)
        vcat = jnp.concatenate([vp_ref[blk], v_new[rs], pad], axis=0)
        kswap = pltpu.roll(kcat, HEAD_DIM, 1)
        vswap = pltpu.roll(vcat, HEAD_DIM, 1)
        per_head = []
        for hk in range(N_KV):
            kbd = _pair_block_diag(kcat, kswap, hk, 0)
            vbds.append(_pair_block_diag(vcat, vswap, hk, 0))
            s4 = lax.dot_general(_stack_pairs(qkv, rs, hk), kbd, NT_DIMS,
                                 preferred_element_type=F32) * (HEAD_DIM ** -0.5)
            for pp in range(PAIRS_PER_KV):
                for sub in range(2):
                    per_head.append(s4[pp * tq:(pp + 1) * tq, sub * 2 * WINDOW:(sub + 1) * 2 * WINDOW])
        scores.append(jnp.concatenate(per_head, axis=0))

    s_all = jnp.stack(scores, axis=0) + bias_ref[...][None]
    sink = sink_ref[...][None]
    m = jnp.maximum(jnp.max(s_all, axis=-1, keepdims=True), sink)
    pr = jnp.exp(s_all - m)
    pr = pr / (jnp.sum(pr, axis=-1, keepdims=True) + jnp.exp(sink - m))

    for blk in range(n_blocks):
        outs = []
        for hk in range(N_KV):
            p4 = []
            for pp in range(PAIRS_PER_KV):
                h0 = 2 * (hk * PAIRS_PER_KV + pp)
                p4.append(jnp.concatenate([pr[blk, h0 * tq:(h0 + 1) * tq, :],
                                           pr[blk, (h0 + 1) * tq:(h0 + 2) * tq, :]], axis=-1))
            o4 = jnp.dot(jnp.concatenate(p4, axis=0).astype(BF16), vbds[blk * N_KV + hk],
                         preferred_element_type=F32)
            outs.extend(o4[pp * tq:(pp + 1) * tq, :] for pp in range(PAIRS_PER_KV))
        yd_ref[blk * tq:(blk + 1) * tq, :] = jnp.concatenate(outs, axis=-1).astype(BF16)


def _swa_stream_kernel(h_ref, wq_ref, wkv_ref, bias_ref, sink_ref, yd_ref, k_ref, v_ref,
                       wbf, kprev, vprev_t, *, blocks_per_seq):
    i = pl.program_id(0)
    rows = h_ref.shape[0]
    nq = N_HEADS * HEAD_DIM
    nkv = N_KV * HEAD_DIM
    tq = WINDOW
    n_blocks = rows // tq

    @pl.when(i == 0)
    def _():
        kprev[...] = jnp.zeros_like(kprev)
        vprev_t[...] = jnp.zeros_like(vprev_t)

    qkv = _swa_project(i, h_ref, wq_ref, wkv_ref, wbf)
    k_new = qkv[:, nq:nq + nkv]
    v_new = qkv[:, nq + nkv:]
    k_ref[...] = k_new
    v_ref[...] = v_new
    v_new_t = v_new.T
    lanes = PAIRS_PER_KV * tq

    for blk in range(n_blocks):
        rs = slice(blk * tq, (blk + 1) * tq)
        first = ((i * n_blocks + blk) % blocks_per_seq == 0).astype(I32)
        k_cur = k_new[rs]
        v_cur_t = v_new_t[:, rs]
        kcat = jnp.concatenate([kprev[...], k_cur], axis=0)
        vcat_t = jnp.concatenate([vprev_t[...], v_cur_t], axis=1)
        kprev[...] = k_cur
        vprev_t[...] = v_cur_t
        kswap = pltpu.roll(kcat, HEAD_DIM, 1)
        vswap_t = pltpu.roll(vcat_t, HEAD_DIM, 0)
        outs = []
        for hk in range(N_KV):
            kbd = _pair_block_diag(kcat, kswap, hk, 0)
            vbd_t = _pair_block_diag(vcat_t, vswap_t, hk, 1)
            st = lax.dot_general(kbd, _stack_pairs(qkv, rs, hk, SCORE_SCALE), NT_DIMS,
                                 preferred_element_type=F32)
            s3 = st.reshape(2, 2 * WINDOW, lanes) + bias_ref[first, hk]
            sink = sink_ref[hk]
            m = jnp.maximum(jnp.max(s3, axis=1, keepdims=True), sink)
            pr = jnp.exp(s3 - m)
            inv = 1.0 / (jnp.sum(pr, axis=1, keepdims=True) + jnp.exp(sink - m))
            o_t = jnp.dot(vbd_t, pr.reshape(4 * WINDOW, lanes).astype(BF16),
                          preferred_element_type=F32)
            norm = jnp.concatenate([jnp.broadcast_to(inv[sub], (HEAD_DIM, lanes)) for sub in range(2)], axis=0)
            o4 = (o_t * norm).T
            outs.extend(o4[pp * tq:(pp + 1) * tq, :] for pp in range(PAIRS_PER_KV))
        yd_ref[rs, :] = jnp.concatenate(outs, axis=-1).astype(BF16)


def _swa_weight_specs(w_in, d):
    nq = N_HEADS * HEAD_DIM
    nkv = N_KV * HEAD_DIM
    nw = nq + 2 * nkv
    q_blk = (w_in.shape[1] - nw) // nq
    kv_blk = (w_in.shape[1] - 2 * nkv) // (2 * nkv)
    assert q_blk * nq + nw == w_in.shape[1] and kv_blk * 2 * nkv + 2 * nkv == w_in.shape[1]
    return [pl.BlockSpec((d, nq), lambda i: (0, q_blk)), pl.BlockSpec((d, 2 * nkv), lambda i: (0, kv_blk))]


def _swa_outputs(rows, rt):
    nq = N_HEADS * HEAD_DIM
    nkv = N_KV * HEAD_DIM
    specs = [pl.BlockSpec((rt, nq), lambda i: (i, 0)),
             pl.BlockSpec((rt, nkv), lambda i: (i, 0)),
             pl.BlockSpec((rt, nkv), lambda i: (i, 0))]
    shapes = [jax.ShapeDtypeStruct((rows, nq), BF16),
              jax.ShapeDtypeStruct((rows, nkv), F32),
              jax.ShapeDtypeStruct((rows, nkv), F32)]
    return specs, shapes


def _swa_cached_mixer(h2, w_in, k_cache, v_cache, bias, sinks, tq):
    rows, d = h2.shape
    nkv = N_KV * HEAD_DIM
    nw = N_HEADS * HEAD_DIM + 2 * nkv
    n_blocks = ROW_TILE // tq
    cache_spec = pl.BlockSpec((n_blocks, WINDOW, nkv), lambda i: (i, 0, 0))
    out_specs, out_shape = _swa_outputs(rows, ROW_TILE)
    return pl.pallas_call(
        functools.partial(_swa_cached_kernel, tq=tq),
        grid=(rows // ROW_TILE,),
        in_specs=[pl.BlockSpec((ROW_TILE, d), lambda i: (i, 0))] + _swa_weight_specs(w_in, d)
        + [cache_spec, cache_spec,
           pl.BlockSpec((N_HEADS * tq, 2 * WINDOW), lambda i: (0, 0)),
           pl.BlockSpec((N_HEADS * tq, 1), lambda i: (0, 0))],
        out_specs=out_specs,
        out_shape=out_shape,
        scratch_shapes=[pltpu.VMEM((d, nw), BF16)],
        compiler_params=_cparams(1),
        name="swa_cached",
    )(h2, w_in, w_in, k_cache, v_cache, bias, sinks)


def _swa_stream_mixer(h2, w_in, bias_t, sinks_t, blocks_per_seq):
    rows, d = h2.shape
    nkv = N_KV * HEAD_DIM
    nw = N_HEADS * HEAD_DIM + 2 * nkv
    lanes = PAIRS_PER_KV * WINDOW
    rt = min(rows, MATMUL_TILE)
    out_specs, out_shape = _swa_outputs(rows, rt)
    return pl.pallas_call(
        functools.partial(_swa_stream_kernel, blocks_per_seq=blocks_per_seq),
        grid=(rows // rt,),
        in_specs=[pl.BlockSpec((rt, d), lambda i: (i, 0))] + _swa_weight_specs(w_in, d)
        + [pl.BlockSpec((2, N_KV, 2, 2 * WINDOW, lanes), lambda i: (0, 0, 0, 0, 0)),
           pl.BlockSpec((N_KV, 2, 1, lanes), lambda i: (0, 0, 0, 0))],
        out_specs=out_specs,
        out_shape=out_shape,
        scratch_shapes=[pltpu.VMEM((d, nw), BF16), pltpu.VMEM((WINDOW, nkv), F32),
                        pltpu.VMEM((nkv, WINDOW), F32)],
        compiler_params=_cparams(1),
        name="swa_stream",
    )(h2, w_in, w_in, bias_t, sinks_t)


def _t5_bucket(dist):
    max_exact = N_BUCKETS // 2
    dd = np.maximum(dist, 1)
    large = max_exact + (np.log(dd / max_exact) / np.log(WINDOW / max_exact)
                         * (N_BUCKETS - max_exact)).astype(np.int64)
    large = np.minimum(large, N_BUCKETS - 1)
    return np.where(dist < max_exact, dist, large).astype(np.int32)


def _attention_bias(rel_bias):
    by_dist = jnp.take(rel_bias.astype(F32), _t5_bucket(np.arange(WINDOW)), axis=0).T
    neg = jnp.full((N_HEADS, WINDOW), NEG_INF, F32)
    line = jnp.concatenate([neg, by_dist[:, ::-1], neg[:, :WINDOW - 1]], axis=1)
    rows = line[:, None, :]
    span = 1
    while span < WINDOW:
        rows = jnp.concatenate([rows[:, :, span:], rows[:, :, :rows.shape[2] - span]], axis=1)
        span *= 2
    return rows


def kernel(x_prompt, x_sample, state_pool, state_conv, cache_swa_k, cache_swa_v, c_prompt, c_sample, w_ada, b_ada, norm_g, final_norm_g, w_in_even, w_out_even, w_pool, pool_scale, conv_w, w_in_odd, w_out_odd, gm_norm_g, gm_norm_b, gm_w_s, gm_b_s, attn_sinks, rel_bias, w_router, b_router, w_gate, w_up, w_down):
    d = D_MODEL
    bp, tp, _ = x_prompt.shape
    bs, ts, _ = x_sample.shape
    rows_s = bs * ts
    assert rows_s == ROW_TILE and tp % ROUTER_TILE == 0 and PAST_LEN % CHUNK == 0
    assert bp <= V7X_SUBLANES and CHUNK % ts == 0

    n_c = bp + bs
    c_pad = (-n_c) % V7X_SUBLANES
    c_all = jnp.concatenate([c_prompt, c_sample, jnp.zeros((c_pad, d), F32)], axis=0)
    mod_p = _adaln(c_all, w_ada, b_ada)
    mod_s = jnp.repeat(mod_p[:, bp:bp + bs], ts, axis=1)

    xp = x_prompt.reshape(bp * tp, d)
    xs_ = x_sample.reshape(rows_s, d)
    w_in0, w_in1 = w_in_even[0], w_in_odd[0]

    hp0, ya_p, pool_p = _pool_mixer(xp, norm_g[0, 0], mod_p, 0, w_in0, w_pool[0], pool_scale[0],
                                    None, bp, 1, MATMUL_TILE, 0)
    hs0, ya_s, pool_s = _pool_mixer(xs_, norm_g[0, 0], mod_s, 0, w_in0, w_pool[0], pool_scale[0],
                                    state_pool[0], bs, bs, ts, PAST_LEN)
    yb_p, conv_p = _conv_mixer(hp0, w_in0, conv_w[0], None, bp, 1, MATMUL_TILE)
    yb_s, conv_s = _conv_mixer(hs0, w_in0, conv_w[0], state_conv[0], bs, bs, ts)
    out_p = _outproj(ya_p, yb_p, xp, mod_p, 0, norm_g[0, 1], w_out_even[0], tp)
    out_s = _outproj(ya_s, yb_s, xs_, mod_s, 0, norm_g[0, 1], w_out_even[0], rows_s)
    (x2p, h1p), (x2s, h1s) = _moe(out_p, out_s, mod_p, mod_s, 0, norm_g[1, 0], tp,
                                  w_router, b_router, w_gate, w_up, w_down, final=False)

    bs_t = gm_b_s[0].T
    (yc_p,) = _gmlp_mixer(h1p, w_in1, gm_norm_g[0], gm_norm_b[0], gm_w_s[0], bs_t, CHUNK, CHUNK, False)
    yc_s, gv_s = _gmlp_mixer(h1s, w_in1, gm_norm_g[0], gm_norm_b[0], gm_w_s[0], bs_t, rows_s, ts, True)
    bias = _attention_bias(rel_bias)
    nkv = N_KV * HEAD_DIM
    bias_t = jnp.transpose(bias.reshape(N_KV, PAIRS_PER_KV, 2, WINDOW, 2 * WINDOW), (0, 2, 4, 1, 3))
    bias_t = bias_t.reshape(N_KV, 2, 2 * WINDOW, PAIRS_PER_KV * WINDOW)
    before_start = (np.arange(2 * WINDOW) < WINDOW)[None, None, :, None]
    bias_t = jnp.stack([bias_t, jnp.where(before_start, NEG_INF, bias_t)], axis=0)
    sinks_t = jnp.transpose(attn_sinks[0].reshape(N_KV, PAIRS_PER_KV, 2), (0, 2, 1))
    sinks_t = jnp.repeat(sinks_t, WINDOW, axis=-1).reshape(N_KV, 2, 1, PAIRS_PER_KV * WINDOW)
    yd_p, k_p, v_p = _swa_stream_mixer(h1p, w_in1, bias_t, sinks_t, tp // WINDOW)
    yd_s, k_s, v_s = _swa_cached_mixer(h1s, w_in1, cache_swa_k[0].reshape(bs, WINDOW, nkv),
                                       cache_swa_v[0].reshape(bs, WINDOW, nkv),
                                       bias[:, :ts, :].reshape(N_HEADS * ts, 2 * WINDOW),
                                       jnp.repeat(attn_sinks[0], ts).reshape(-1, 1), ts)
    out_p = _outproj(yc_p, yd_p, x2p, mod_p, 1, norm_g[1, 1], w_out_odd[0], tp)
    out_s = _outproj(yc_s, yd_s, x2s, mod_s, 1, norm_g[1, 1], w_out_odd[0], rows_s)
    (yp,), (ys_out,) = _moe(out_p, out_s, mod_p, mod_s, 1, final_norm_g, tp,
                            w_router, b_router, w_gate, w_up, w_down, final=True)

    k_p4 = k_p.reshape(bp, tp, nkv)[:, -WINDOW:].reshape(bp, WINDOW, N_KV, HEAD_DIM)
    v_p4 = v_p.reshape(bp, tp, nkv)[:, -WINDOW:].reshape(bp, WINDOW, N_KV, HEAD_DIM)
    k_s4 = jnp.concatenate([cache_swa_k[0], k_s.reshape(bs, ts, N_KV, HEAD_DIM)], axis=1)[:, -WINDOW:]
    v_s4 = jnp.concatenate([cache_swa_v[0], v_s.reshape(bs, ts, N_KV, HEAD_DIM)], axis=1)[:, -WINDOW:]
    return (yp.reshape(bp, tp, d), ys_out.reshape(bs, ts, d),
            pool_p[None], pool_s[None], conv_p[None], conv_s[None],
            k_p4[None], k_s4[None], v_p4[None], v_s4[None],
            gv_s.reshape(bs, ts, GM_WIDTH)[None])
```

```python
import functools

import numpy as np
import jax
import jax.numpy as jnp
from jax import lax
from jax.experimental import pallas as pl
from jax.experimental.pallas import tpu as pltpu

F32 = jnp.float32
BF16 = jnp.bfloat16
I32 = jnp.int32
U32 = jnp.uint32

D_MODEL = 2048
POOL_WINDOWS = (2, 4, 8, 16)
POOL_WIDTH = 1024
POOL_GROUP = 256
POOL_STATE = 15
CONV_WIDTH = 1024
CONV_K = 3
GM_WIDTH = 1024
GM_GROUPS = 8
GM_GROUP = 128
CHUNK = 128
HEAD_DIM = 64
N_HEADS = 16
N_KV = 2
WINDOW = 128
N_BUCKETS = 32
N_EXPERTS = 16
N_EXPERT_GROUPS = 4
EXP_PER_GROUP = 4
TOP_K = 2
EPS = 1e-6
NEG_INF = -1e30
PAST_LEN = 16384

V7X_SUBLANES = 8
V7X_LANES = 128
VMEM_LIMIT = 56 * 1024 * 1024

ROW_TILE = 256
MATMUL_TILE = 512
ROUTER_TILE = 1024
POOL_HALO = 16
CONV_HALO = 8
MOE_TILE = 256
XS_RING = 3
ADALN_COL_TILE = 1024
CONV_COL_TILE = 512
TAB_EXPERT, TAB_VALID, TAB_LAST_TILE, TAB_NUSED, TAB_NEXT = 0, 1, 2, 3, 4


def _cparams(n_axes):
    return pltpu.CompilerParams(dimension_semantics=("arbitrary",) * n_axes,
                                vmem_limit_bytes=VMEM_LIMIT)


def _rms(x, g):
    return x * lax.rsqrt(jnp.mean(x * x, axis=-1, keepdims=True) + EPS) * g


def _mod_spec(mod, layer, part):
    nrow = ROW_TILE if mod.shape[1] == ROW_TILE else V7X_SUBLANES
    return pl.BlockSpec((1, nrow, D_MODEL), lambda *_: (layer, 0, part))


def _mod_rows(m_ref, seq):
    if m_ref.shape[1] == V7X_SUBLANES:
        return m_ref[0, pl.ds(seq, 1), :]
    return m_ref[0]


def _adaln_kernel(c_ref, w_ref, b_ref, o_ref):
    c = c_ref[...]
    a = (c * jax.nn.sigmoid(c)).astype(BF16)
    o_ref[0] = jnp.dot(a, w_ref[0].astype(BF16), preferred_element_type=F32) + b_ref[0]


def _adaln(c_all, w_ada, b_ada):
    depth, d, n6 = w_ada.shape
    m = c_all.shape[0]
    tn = ADALN_COL_TILE

    def outer(c_ref, w_hbm, b_hbm, o_hbm):
        def body(w_ref, b_ref, o_ref):
            _adaln_kernel(c_ref, w_ref, b_ref, o_ref)

        pltpu.emit_pipeline(
            body,
            grid=(depth, n6 // tn),
            in_specs=[pl.BlockSpec((1, d, tn), lambda l, j: (l, 0, j), pipeline_mode=pl.Buffered(3)),
                      pl.BlockSpec((1, 1, tn), lambda l, j: (l, 0, j))],
            out_specs=[pl.BlockSpec((1, m, tn), lambda l, j: (l, 0, j))],
        )(w_hbm, b_hbm, o_hbm)

    return pl.pallas_call(
        outer,
        in_specs=[pl.BlockSpec(memory_space=pltpu.VMEM),
                  pl.BlockSpec(memory_space=pl.ANY),
                  pl.BlockSpec(memory_space=pl.ANY)],
        out_specs=pl.BlockSpec(memory_space=pl.ANY),
        out_shape=jax.ShapeDtypeStruct((depth, m, n6), F32),
        compiler_params=_cparams(0),
        name="adaln",
    )(c_all, w_ada, b_ada.reshape(depth, 1, n6))


def _pool_kernel(x_ref, g_ref, sc_ref, sh_ref, w_ref, wp_ref, ps_ref, st_ref, h_ref, ya_ref, ns_ref,
                 wbf, wpbf, carry, *, nb, tm, tiles_per_seq, start):
    i = pl.program_id(0)
    t = i % tiles_per_seq
    seq = i // tiles_per_seq
    c = POOL_WIDTH
    halo = POOL_HALO

    @pl.when(i == 0)
    def _():
        wbf[...] = w_ref[...].astype(BF16)
        wpbf[...] = wp_ref[...].astype(BF16)

    @pl.when(t == 0)
    def _():
        carry[...] = st_ref[...]

    h = (_rms(x_ref[...], g_ref[...]) * (1.0 + _mod_rows(sc_ref, seq)) + _mod_rows(sh_ref, seq)).astype(BF16)
    h_ref[...] = h
    p = jnp.dot(h, wbf[...], preferred_element_type=F32)
    p3 = p.reshape(nb, tm, c)
    ext3 = jnp.concatenate([carry[...], p3], axis=1)
    tail = ext3[:, tm:tm + halo, :]
    ns_ref[...] = tail[:, halo - POOL_STATE:, :]
    carry[...] = tail
    ext = ext3.reshape(nb * (halo + tm), c)
    pos = start + t * tm + lax.broadcasted_iota(I32, (1, tm, 1), 1)
    outs = []
    for gi, w in enumerate(POOL_WINDOWS):
        sl = slice(gi * POOL_GROUP, (gi + 1) * POOL_GROUP)
        acc = ext[:, sl]
        shift = 1
        while shift < w:
            acc = acc + pltpu.roll(acc, shift, 0)
            shift *= 2
        win = acc.reshape(nb, halo + tm, POOL_GROUP)[:, halo:, :]
        cnt = jnp.minimum(pos + 1, w).astype(F32)
        dgrp = win / cnt - p3[:, :, sl]
        outs.append(jnp.dot(dgrp.reshape(nb * tm, POOL_GROUP).astype(BF16), wpbf[gi],
                            preferred_element_type=F32))
    y = jnp.concatenate(outs, axis=-1) * ps_ref[...]
    ya_ref[...] = y.astype(BF16)


def _pool_mixer(x2, g, mod, layer, w_in, w_pool, pool_scale, state, nseq, nb, tm, start):
    rows, d = x2.shape
    tiles_per_seq = (rows // nseq) // tm
    seq_blocks = nseq // nb
    c = POOL_WIDTH
    if state is None:
        st = jnp.zeros((nseq, POOL_HALO, c), F32)
    else:
        st = jnp.pad(state, ((0, 0), (POOL_HALO - POOL_STATE, 0), (0, 0)))
    kern = functools.partial(_pool_kernel, nb=nb, tm=tm, tiles_per_seq=tiles_per_seq, start=start)
    h2, ya, ns = pl.pallas_call(
        kern,
        grid=(seq_blocks * tiles_per_seq,),
        in_specs=[pl.BlockSpec((nb * tm, d), lambda i: (i, 0)),
                  pl.BlockSpec((1, d), lambda i: (0, 0)),
                  _mod_spec(mod, layer, 1), _mod_spec(mod, layer, 0),
                  pl.BlockSpec((d, c), lambda i: (0, 0)),
                  pl.BlockSpec((len(POOL_WINDOWS), POOL_GROUP, POOL_GROUP), lambda i: (0, 0, 0)),
                  pl.BlockSpec((1, c), lambda i: (0, 0)),
                  pl.BlockSpec((nb, POOL_HALO, c), lambda i: (i // tiles_per_seq, 0, 0))],
        out_specs=[pl.BlockSpec((nb * tm, d), lambda i: (i, 0)),
                   pl.BlockSpec((nb * tm, c), lambda i: (i, 0)),
                   pl.BlockSpec((nb, POOL_STATE, c), lambda i: (i // tiles_per_seq, 0, 0))],
        out_shape=[jax.ShapeDtypeStruct((rows, d), BF16),
                   jax.ShapeDtypeStruct((rows, c), BF16),
                   jax.ShapeDtypeStruct((nseq, POOL_STATE, c), F32)],
        scratch_shapes=[pltpu.VMEM((d, c), BF16),
                        pltpu.VMEM((len(POOL_WINDOWS), POOL_GROUP, POOL_GROUP), BF16),
                        pltpu.VMEM((nb, POOL_HALO, c), F32)],
        compiler_params=_cparams(1),
        name="pool_mixer",
    )(x2, g.reshape(1, d), mod, mod, w_in, w_pool, pool_scale.reshape(1, c), st)
    return h2, ya, ns


def _conv_kernel(h_ref, wx_ref, wb_ref, wc_ref, cw_ref, st_ref, yb_ref, ns_ref,
                 wxbf, wbbf, wcbf, carry, *, nb, tm, tiles_per_seq):
    i = pl.program_id(1)
    t = i % tiles_per_seq
    tc = wxbf.shape[1]
    halo = CONV_HALO

    @pl.when(i == 0)
    def _():
        wxbf[...] = wx_ref[...].astype(BF16)
        wbbf[...] = wb_ref[...].astype(BF16)
        wcbf[...] = wc_ref[...].astype(BF16)

    @pl.when(t == 0)
    def _():
        carry[...] = st_ref[...]

    h = h_ref[...]
    xin = jnp.dot(h, wxbf[...], preferred_element_type=F32)
    gb = jnp.dot(h, wbbf[...], preferred_element_type=F32)
    gc = jnp.dot(h, wcbf[...], preferred_element_type=F32)
    z3 = (gc * xin).reshape(nb, tm, tc)
    ext3 = jnp.concatenate([carry[...], z3], axis=1)
    tail = ext3[:, tm:tm + halo, :]
    ns_ref[...] = tail[:, halo - (CONV_K - 1):, :]
    carry[...] = tail
    ext = ext3.reshape(nb * (halo + tm), tc)
    cw = cw_ref[...]
    conv = cw[0:1, :] * pltpu.roll(ext, 2, 0) + cw[1:2, :] * pltpu.roll(ext, 1, 0) + cw[2:3, :] * ext
    conv = conv.reshape(nb, halo + tm, tc)[:, halo:, :].reshape(nb * tm, tc)
    yb_ref[...] = (gb * conv).astype(BF16)


def _conv_mixer(h2, w_in, conv_w, state, nseq, nb, tm):
    rows, d = h2.shape
    tiles_per_seq = (rows // nseq) // tm
    seq_blocks = nseq // nb
    c = CONV_WIDTH
    tc = CONV_COL_TILE
    cb = c // tc
    base = POOL_WIDTH // tc
    if state is None:
        st = jnp.zeros((nseq, CONV_HALO, c), F32)
    else:
        st = jnp.pad(state, ((0, 0), (CONV_HALO - (CONV_K - 1), 0), (0, 0)))
    kern = functools.partial(_conv_kernel, nb=nb, tm=tm, tiles_per_seq=tiles_per_seq)
    yb, ns = pl.pallas_call(
        kern,
        grid=(cb, seq_blocks * tiles_per_seq),
        in_specs=[pl.BlockSpec((nb * tm, d), lambda j, i: (i, 0)),
                  pl.BlockSpec((d, tc), lambda j, i: (0, base + j)),
                  pl.BlockSpec((d, tc), lambda j, i: (0, base + cb + j)),
                  pl.BlockSpec((d, tc), lambda j, i: (0, base + 2 * cb + j)),
                  pl.BlockSpec((CONV_K, tc), lambda j, i: (0, j)),
                  pl.BlockSpec((nb, CONV_HALO, tc), lambda j, i: (i // tiles_per_seq, 0, j))],
        out_specs=[pl.BlockSpec((nb * tm, tc), lambda j, i: (i, j)),
                   pl.BlockSpec((nb, CONV_K - 1, tc), lambda j, i: (i // tiles_per_seq, 0, j))],
        out_shape=[jax.ShapeDtypeStruct((rows, c), BF16),
                   jax.ShapeDtypeStruct((nseq, CONV_K - 1, c), F32)],
        scratch_shapes=[pltpu.VMEM((d, tc), BF16)] * 3 + [pltpu.VMEM((nb, CONV_HALO, tc), F32)],
        compiler_params=_cparams(2),
        name="conv_mixer",
    )(h2, w_in, w_in, w_in, conv_w, st)
    return yb, ns


def _pack_bf16_pairs(v):
    c = v.shape[1] // 2
    return pltpu.bitcast(pltpu.pack_elementwise([v[:, :c], v[:, c:]], packed_dtype=BF16), U32)


def _store_token_tiles(ref, v):
    rows = v.shape[0]
    for j in range(V7X_SUBLANES):
        ref[pl.ds(j, rows, stride=V7X_SUBLANES), :] = v[:, j * V7X_LANES:(j + 1) * V7X_LANES]


def _load_token_tiles(ref):
    rows = ref.shape[0] // V7X_SUBLANES
    return jnp.concatenate([ref[pl.ds(j, rows, stride=V7X_SUBLANES), :] for j in range(V7X_SUBLANES)],
                           axis=-1)


def _unpack_pairs_f32(w):
    return tuple(pltpu.unpack_elementwise(w, index=k, packed_dtype=BF16, unpacked_dtype=F32) for k in range(2))


def _unpack_bf16_pairs(w):
    lo, hi = _unpack_pairs_f32(w)
    return lo.astype(BF16), hi.astype(BF16)


def _outproj_kernel(ya_ref, yb_ref, x_hbm, g1_ref, sc_ref, sh_ref, ng_ref, wo_ref,
                    x1_ref, hp_ref, hpt_ref, wobf, xring, xsems, *, tiles_per_seq, n_steps):
    i = pl.program_id(0)
    seq = i // tiles_per_seq
    depth, rt, _ = xring.shape

    def x_copy(j):
        j = jnp.asarray(j, I32)
        rows = pl.ds(pl.multiple_of(j * rt, rt), rt)
        return pltpu.make_async_copy(x_hbm.at[rows, :], xring.at[j % depth], xsems.at[j % depth])

    @pl.when(i == 0)
    def _():
        wobf[...] = wo_ref[...].astype(BF16)
        for j in range(min(depth - 1, n_steps)):
            x_copy(j).start()

    @pl.when(i + depth - 1 < n_steps)
    def _():
        x_copy(i + depth - 1).start()

    x_copy(i).wait()
    ycat = jnp.concatenate([ya_ref[...], yb_ref[...]], axis=-1)
    y = jnp.dot(ycat, wobf[...], preferred_element_type=F32)
    x1 = xring[i % depth] + _mod_rows(g1_ref, seq) * y
    x1_ref[...] = x1
    h2 = _rms(x1, ng_ref[...]) * (1.0 + _mod_rows(sc_ref, seq)) + _mod_rows(sh_ref, seq)
    packed = _pack_bf16_pairs(h2)
    hp_ref[...] = packed
    _store_token_tiles(hpt_ref, packed)


def _outproj(ya, yb, x2, mod, layer, ng, w_out, seq_rows):
    rows_all, d = x2.shape
    half = ya.shape[1]
    rt = ROW_TILE
    row_spec = lambda w: pl.BlockSpec((rt, w), lambda i: (i, 0))
    return pl.pallas_call(
        functools.partial(_outproj_kernel, tiles_per_seq=seq_rows // rt, n_steps=rows_all // rt),
        grid=(rows_all // rt,),
        in_specs=[row_spec(half), row_spec(half), pl.BlockSpec(memory_space=pl.ANY),
                  _mod_spec(mod, layer, 2), _mod_spec(mod, layer, 4), _mod_spec(mod, layer, 3),
                  pl.BlockSpec((1, d), lambda i: (0, 0)),
                  pl.BlockSpec((d, d), lambda i: (0, 0), pipeline_mode=pl.Buffered(1))],
        out_specs=[row_spec(d), row_spec(d // 2),
                   pl.BlockSpec((rt * V7X_SUBLANES, V7X_LANES), lambda i: (i, 0))],
        out_shape=[jax.ShapeDtypeStruct((rows_all, d), F32),
                   jax.ShapeDtypeStruct((rows_all, d // 2), U32),
                   jax.ShapeDtypeStruct((rows_all * V7X_SUBLANES, V7X_LANES), U32)],
        scratch_shapes=[pltpu.VMEM((d, d), BF16), pltpu.VMEM((XS_RING, rt, d), F32),
                        pltpu.SemaphoreType.DMA((XS_RING,))],
        compiler_params=_cparams(1),
        name="outproj",
    )(ya, yb, x2, mod, mod, mod, ng.reshape(1, d), w_out)


def _router_kernel(hpp_ref, hps_ref, wr_ref, br_ref, pos_ref, rw_ref, tab_ref,
                   cnt_acc, totals, starts, padded, before_ref, s_all, sel_all, *, nt_p, rows_s):
    ph = pl.program_id(0)
    t = pl.program_id(1)
    last = nt_p
    r = hpp_ref.shape[0]
    half = hpp_ref.shape[1]
    ne = N_EXPERTS
    sub = lax.broadcasted_iota(I32, (ne, V7X_LANES), 0)

    @pl.when(t == 0)
    def _():
        cnt_acc[...] = jnp.zeros_like(cnt_acc)

    @pl.when((ph == 0) & (t == 0))
    def _():
        starts[...] = jnp.zeros_like(starts)
        padded[...] = jnp.zeros_like(padded)

    @pl.when((ph == 1) & (t == 0))
    def _():
        pad = jnp.floor((totals[...] + (MOE_TILE - 1.0)) * (1.0 / MOE_TILE)) * MOE_TILE
        run = pad
        k = 1
        while k < ne:
            run = run + jnp.where(sub >= k, pltpu.roll(run, k, 0), 0.0)
            k *= 2
        padded[...] = pad
        starts[...] = run - pad

    is_s = t == last
    eid = lax.broadcasted_iota(I32, (ne, r), 0)
    n_valid = jnp.where(is_s, rows_s, r)
    tok = lax.broadcasted_iota(I32, (ne, r), 1)

    @pl.when(ph == 0)
    def _():
        w_s = jnp.concatenate([hps_ref[...], jnp.zeros((r - rows_s, half), U32)], axis=0)
        w = jnp.where(is_s, w_s, hpp_ref[...])
        lo, hi = _unpack_bf16_pairs(w)
        wr = wr_ref[...].astype(BF16)
        log_t = (lax.dot_general(wr[:, :half], lo, NT_DIMS, preferred_element_type=F32)
                 + lax.dot_general(wr[:, half:], hi, NT_DIMS, preferred_element_type=F32))

        s = jax.nn.sigmoid(log_t)
        sg = s + br_ref[...]
        within = eid % EXP_PER_GROUP
        grp = eid // EXP_PER_GROUP

        def group_rot(x, k):
            return jnp.where(within + k < EXP_PER_GROUP,
                             pltpu.roll(x, ne - k, 0), pltpu.roll(x, EXP_PER_GROUP - k, 0))

        rank = jnp.zeros((ne, r), I32)
        for k in range(1, EXP_PER_GROUP):
            mate = group_rot(sg, k)
            wrapped = within + k >= EXP_PER_GROUP
            ahead = (mate > sg) | (wrapped & (mate == sg))
            rank = rank + ahead.astype(I32)
        top2 = rank < TOP_K
        kept = jnp.where(top2, sg, 0.0)
        gscore = kept
        for k in range(1, EXP_PER_GROUP):
            gscore = gscore + group_rot(kept, k)
        win = None
        for k in range(1, N_EXPERT_GROUPS):
            other = pltpu.roll(gscore, EXP_PER_GROUP * k, 0)
            beats = (gscore > other) | ((grp < k) & (gscore == other))
            win = beats if win is None else (win & beats)
        picked_now = top2 & win & (tok < n_valid)
        s_all[t] = s
        sel_all[t] = picked_now.astype(F32)

    s = s_all[t]
    selb = sel_all[t]
    sel = selb > 0.5
    cnt_before = cnt_acc[...]
    cnt_new = cnt_before + jnp.sum(selb, axis=1, keepdims=True)
    cnt_acc[...] = cnt_new

    @pl.when((ph == 0) & (t == 0))
    def _():
        src = lax.broadcasted_iota(I32, (r, r), 0)
        dst = lax.broadcasted_iota(I32, (r, r), 1)
        before_ref[...] = (src < dst).astype(BF16)

    @pl.when((ph == 0) & (t == last))
    def _():
        totals[...] = cnt_new

    @pl.when(ph == 1)
    def _():
        picked = jnp.where(sel, s, 0.0)
        wsum = jnp.sum(picked, axis=0, keepdims=True)
        gate = picked / jnp.where(tok[0:1, :] < n_valid, wsum, 1.0)
        ranks = jnp.dot(selb.astype(BF16), before_ref[...], preferred_element_type=F32)
        slot = (starts[...][:, 0:1] + cnt_before[:, 0:1] + ranks).astype(I32)
        e_a = jnp.min(jnp.where(sel, eid, ne), axis=0, keepdims=True)
        e_b = jnp.max(jnp.where(sel, eid, -1), axis=0, keepdims=True)
        is_a = sel & (eid == e_a)
        is_b = sel & (eid == e_b)
        pos_a = jnp.sum(jnp.where(is_a, slot, 0), axis=0, keepdims=True)
        pos_b = jnp.sum(jnp.where(is_b, slot, 0), axis=0, keepdims=True)
        w_a = jnp.sum(jnp.where(is_a, gate, 0.0), axis=0, keepdims=True)
        w_b = jnp.sum(jnp.where(is_b, gate, 0.0), axis=0, keepdims=True)
        pos_ref[0] = jnp.concatenate([pos_a, pos_b], axis=0)
        wmat = jnp.concatenate([w_a, w_b, jnp.zeros((V7X_LANES - 2, r), F32)], axis=0)
        rw_ref[...] = wmat.T

    @pl.when((ph == 1) & (t == last))
    def _():
        ends = starts[...] + padded[...]
        lane = lax.broadcasted_iota(I32, (ne, V7X_LANES), 1)
        tile_start = (lane * MOE_TILE).astype(F32)
        te = jnp.sum((tile_start >= ends).astype(I32), axis=0, keepdims=True)
        valid = te < ne
        last_e = jnp.max(jnp.where(padded[...] > 0.0, sub, 0), axis=0, keepdims=True)
        te = jnp.where(valid, te, last_e)
        n_used = jnp.sum(valid.astype(I32), axis=1, keepdims=True) + jnp.zeros((1, V7X_LANES), I32)
        last_tile = jnp.where(padded[...] > 0.0, ends - MOE_TILE, -1.0).astype(I32)
        last_tile_row = jnp.sum(jnp.where(sub == lane, last_tile, 0), axis=0, keepdims=True)
        later = jnp.min(jnp.where((sub > te) & (padded[...] > 0.0), sub, ne), axis=0, keepdims=True)
        next_e = jnp.where(later < ne, later, -1)
        zero = jnp.zeros((1, V7X_LANES), I32)
        tab_ref[...] = jnp.concatenate([te, valid.astype(I32), last_tile_row, n_used, next_e,
                                        zero, zero, zero], axis=0)


def _router(hp_p, hp_s, w_router, b_router):
    n_p, half = hp_p.shape
    rows_s = hp_s.shape[0]
    r = ROUTER_TILE
    nt_p = n_p // r
    nt = nt_p + 1
    kern = functools.partial(_router_kernel, nt_p=nt_p, rows_s=rows_s)
    pos, rw, tab = pl.pallas_call(
        kern,
        grid=(2, nt),
        in_specs=[pl.BlockSpec((r, half), lambda p, t: (jnp.minimum(t, nt_p - 1) * (1 - p), 0)),
                  pl.BlockSpec((rows_s, half), lambda p, t: (0, 0)),
                  pl.BlockSpec((N_EXPERTS, 2 * half), lambda p, t: (0, 0)),
                  pl.BlockSpec((N_EXPERTS, 1), lambda p, t: (0, 0))],
        out_specs=[pl.BlockSpec((1, TOP_K, r), lambda p, t: (p * t, 0, 0)),
                   pl.BlockSpec((r, V7X_LANES), lambda p, t: (p * t, 0)),
                   pl.BlockSpec((V7X_SUBLANES, V7X_LANES), lambda p, t: (0, 0))],
        out_shape=[jax.ShapeDtypeStruct((nt, TOP_K, r), I32),
                   jax.ShapeDtypeStruct((nt * r, V7X_LANES), F32),
                   jax.ShapeDtypeStruct((V7X_SUBLANES, V7X_LANES), I32)],
        scratch_shapes=[pltpu.VMEM((N_EXPERTS, V7X_LANES), F32)] * 4 + [pltpu.VMEM((r, r), BF16)]
        + [pltpu.VMEM((nt, N_EXPERTS, r), F32)] * 2,
        compiler_params=_cparams(2),
        name="router",
    )(hp_p, hp_s, w_router.T, b_router.reshape(N_EXPERTS, 1))
    return pos.reshape(-1), rw, tab.reshape(-1)


def _pos_index(tok0):
    return (tok0 // ROUTER_TILE) * (TOP_K * ROUTER_TILE) + tok0 % ROUTER_TILE


def _tokens(ref, first, n=1):
    start = pl.multiple_of(first * V7X_SUBLANES, V7X_SUBLANES)
    return ref.at[pl.ds(start, n * V7X_SUBLANES), :]


def _dispatch_kernel(pos_ref, tab_ref, hpp_ref, hps_ref, xs_ref, zbuf, sem, *, n_p_steps):
    i = pl.program_id(0)
    rows = hpp_ref.shape[0] // V7X_SUBLANES

    @pl.when(i == 0)
    def _():
        zbuf[...] = jnp.zeros_like(zbuf)

        def fill(e):
            first = pl.multiple_of(tab_ref[TAB_LAST_TILE * V7X_LANES + e], MOE_TILE)
            return pltpu.make_async_copy(zbuf, _tokens(xs_ref, first, MOE_TILE), sem)

        for e in range(N_EXPERTS):
            @pl.when(tab_ref[TAB_LAST_TILE * V7X_LANES + e] >= 0)
            def _():
                fill(e).start()
        for e in range(N_EXPERTS):
            @pl.when(tab_ref[TAB_LAST_TILE * V7X_LANES + e] >= 0)
            def _():
                fill(e).wait()

        def tail(j):
            first = pl.multiple_of(j * MOE_TILE, MOE_TILE)
            return pltpu.make_async_copy(zbuf, _tokens(xs_ref, first, MOE_TILE), sem)

        def tail_start(j, carry):
            tail(j).start()
            return carry

        def tail_wait(j, carry):
            tail(j).wait()
            return carry

        n_used = tab_ref[TAB_NUSED * V7X_LANES]
        n_tiles = xs_ref.shape[0] // (MOE_TILE * V7X_SUBLANES)
        lax.fori_loop(n_used, n_tiles, tail_start, 0)
        lax.fori_loop(n_used, n_tiles, tail_wait, 0)

    def scatter(src_ref, tok0):
        n = src_ref.shape[0] // V7X_SUBLANES
        base = _pos_index(tok0)

        def row_copy(r, dst):
            return pltpu.make_async_copy(_tokens(src_ref, r), _tokens(xs_ref, dst), sem)

        def issue(r, carry):
            row_copy(r, pos_ref[base + r]).start()
            row_copy(r, pos_ref[base + ROUTER_TILE + r]).start(priority=1)
            return carry

        lax.fori_loop(0, n, issue, 0, unroll=8)
        block = pltpu.make_async_copy(src_ref, _tokens(xs_ref, 0, n), sem)
        for _ in range(TOP_K):
            block.wait()

    @pl.when(i < n_p_steps)
    def _():
        scatter(hpp_ref, i * rows)

    @pl.when(i == n_p_steps)
    def _():
        scatter(hps_ref, n_p_steps * rows)


def _dispatch(pos, tab, hpt_p, hpt_s, n_rows_sorted):
    sub = V7X_SUBLANES
    n_p_steps = hpt_p.shape[0] // (ROUTER_TILE * sub)
    kern = functools.partial(_dispatch_kernel, n_p_steps=n_p_steps)
    blk = (ROUTER_TILE * sub, V7X_LANES)
    return pl.pallas_call(
        kern,
        grid_spec=pltpu.PrefetchScalarGridSpec(
            num_scalar_prefetch=2,
            grid=(n_p_steps + 1,),
            in_specs=[pl.BlockSpec(blk, lambda i, p, t: (jnp.minimum(i, n_p_steps - 1), 0)),
                      pl.BlockSpec(hpt_s.shape, lambda i, p, t: (0, 0))],
            out_specs=pl.BlockSpec(memory_space=pl.ANY),
            scratch_shapes=[pltpu.VMEM((MOE_TILE * sub, V7X_LANES), U32), pltpu.SemaphoreType.DMA(())]),
        out_shape=jax.ShapeDtypeStruct((n_rows_sorted * sub, V7X_LANES), U32),
        compiler_params=_cparams(1),
        name="moe_dispatch",
    )(pos, tab, hpt_p, hpt_s)


def _experts_kernel(tab_ref, xs_hbm, wg_hbm, wu_hbm, wd_hbm, ys_ref,
                    wg32, wu32, wd32, wgbf, wubf, wdbf, slot_ref, sems, xbuf, xsems, *, layer):
    i = pl.program_id(0)
    expert = tab_ref[TAB_EXPERT * V7X_LANES + i]
    prev = tab_ref[TAB_EXPERT * V7X_LANES + jnp.maximum(i - 1, 0)]
    upcoming = tab_ref[TAB_NEXT * V7X_LANES + i]
    n_used = tab_ref[TAB_NUSED * V7X_LANES]
    changed = (i == 0) | (expert != prev)
    depth = xbuf.shape[0]

    def tile_copy(j):
        j = jnp.asarray(j, I32)
        return pltpu.make_async_copy(_tokens(xs_hbm, j * MOE_TILE, MOE_TILE), xbuf.at[j % depth],
                                     xsems.at[j % depth])

    @pl.when(i == 0)
    def _():
        for j in range(depth - 1):
            @pl.when(j < n_used)
            def _():
                tile_copy(j).start()

    @pl.when(i + depth - 1 < n_used)
    def _():
        tile_copy(i + depth - 1).start()

    @pl.when(i < n_used)
    def _():
        tile_copy(i).wait()

    xs_ref = xbuf.at[i % depth]

    def weight_copies(e, slot):
        return (pltpu.make_async_copy(wg_hbm.at[layer, e], wg32.at[slot], sems.at[0, slot]),
                pltpu.make_async_copy(wu_hbm.at[layer, e], wu32.at[slot], sems.at[1, slot]),
                pltpu.make_async_copy(wd_hbm.at[layer, e], wd32.at[slot], sems.at[2, slot]))

    @pl.when(i == 0)
    def _():
        slot_ref[0] = 0
        for cp in weight_copies(expert, 0):
            cp.start()

    @pl.when(changed & (i > 0))
    def _():
        slot_ref[0] = 1 - slot_ref[0]

    def mlp(wg, wu, wd):
        x = jnp.concatenate(_unpack_bf16_pairs(_load_token_tiles(xs_ref)), axis=-1)
        a = jnp.dot(x, wg, preferred_element_type=F32)
        b = jnp.dot(x, wu, preferred_element_type=F32)
        hid = (a * jax.nn.sigmoid(a)) * b
        y = jnp.dot(hid.astype(BF16), wd, preferred_element_type=F32)
        _store_token_tiles(ys_ref, _pack_bf16_pairs(y))

    for slot in range(2):
        @pl.when(changed & (slot_ref[0] == slot))
        def _():
            for cp in weight_copies(expert, slot):
                cp.wait()

            @pl.when(upcoming >= 0)
            def _():
                for cp in weight_copies(upcoming, 1 - slot):
                    cp.start(priority=1)

            wg = wg32[slot].astype(BF16)
            wu = wu32[slot].astype(BF16)
            wd = wd32[slot].astype(BF16)
            wgbf[...] = wg
            wubf[...] = wu
            wdbf[...] = wd
            mlp(wg, wu, wd)

    valid = tab_ref[TAB_VALID * V7X_LANES + i] > 0

    @pl.when(valid & jnp.logical_not(changed))
    def _():
        mlp(wgbf[...], wubf[...], wdbf[...])

    @pl.when(jnp.logical_not(valid))
    def _():
        ys_ref[...] = jnp.zeros_like(ys_ref)


def _experts(tab, xs, w_gate, w_up, w_down, layer):
    sub = V7X_SUBLANES
    _, _, d, f = w_gate.shape
    nt = xs.shape[0] // (MOE_TILE * sub)
    assert nt <= V7X_LANES and d == 2 * sub * V7X_LANES
    blk = (MOE_TILE * sub, V7X_LANES)

    hbm = pl.BlockSpec(memory_space=pl.ANY)
    return pl.pallas_call(
        functools.partial(_experts_kernel, layer=layer),
        grid_spec=pltpu.PrefetchScalarGridSpec(
            num_scalar_prefetch=1,
            grid=(nt,),
            in_specs=[hbm, hbm, hbm, hbm],
            out_specs=pl.BlockSpec(blk, lambda i, t: (i, 0)),
            scratch_shapes=[pltpu.VMEM((2, d, f), F32), pltpu.VMEM((2, d, f), F32), pltpu.VMEM((2, f, d), F32),
                            pltpu.VMEM((d, f), BF16), pltpu.VMEM((d, f), BF16), pltpu.VMEM((f, d), BF16),
                            pltpu.SMEM((1,), I32), pltpu.SemaphoreType.DMA((3, 2)),
                            pltpu.VMEM((XS_RING,) + blk, U32), pltpu.SemaphoreType.DMA((XS_RING,))]),
        out_shape=jax.ShapeDtypeStruct(xs.shape, U32),
        compiler_params=_cparams(1),
        name="moe_experts",
    )(tab, xs, w_gate, w_up, w_down)


def _combine_kernel(pos_ref, ys_ref, rw_ref, ng_ref, *rest, tiles_per_seq, n_p_steps, final):
    n_in = 4
    n_out = 1 if final else 2
    trunk_in = (rest[:n_in], rest[n_in:2 * n_in])
    outs = rest[2 * n_in:2 * n_in + 2 * n_out]
    trunk_out = (outs[:n_out], outs[n_out:])
    buf0, buf1, sems = rest[2 * n_in + 2 * n_out:]
    bufs = (buf0, buf1)
    i = pl.program_id(0)
    n_steps = n_p_steps + 1
    rows = rw_ref.shape[0]

    def gather(step, slot):
        base = _pos_index(step * rows)

        def issue(r, carry):
            for k in range(TOP_K):
                src = pos_ref[base + k * ROUTER_TILE + r]
                pltpu.make_async_copy(_tokens(ys_ref, src), _tokens(bufs[slot].at[k], r),
                                      sems.at[slot]).start(priority=k)
            return carry

        lax.fori_loop(0, rows, issue, 0, unroll=8)

    def drain(slot):
        for k in range(TOP_K):
            pltpu.make_async_copy(_tokens(ys_ref, 0, rows), bufs[slot].at[k], sems.at[slot]).wait()

    def finish(slot, trunk, seq):
        x1_ref, g2_ref, sc_ref, sh_ref = trunk_in[trunk]
        rw = rw_ref[...]
        lo_a, hi_a = _unpack_pairs_f32(_load_token_tiles(bufs[slot].at[0]))
        lo_b, hi_b = _unpack_pairs_f32(_load_token_tiles(bufs[slot].at[1]))
        w_a = rw[:, 0:1]
        w_b = rw[:, 1:2]
        moe = jnp.concatenate([w_a * lo_a + w_b * lo_b, w_a * hi_a + w_b * hi_b], axis=-1)
        x2 = x1_ref[...] + _mod_rows(g2_ref, seq) * moe
        if final:
            trunk_out[trunk][0][...] = _rms(x2, ng_ref[...])
        else:
            trunk_out[trunk][0][...] = x2
            trunk_out[trunk][1][...] = (_rms(x2, ng_ref[...]) * (1.0 + _mod_rows(sc_ref, seq))
                                        + _mod_rows(sh_ref, seq)).astype(BF16)

    @pl.when(i == 0)
    def _():
        gather(0, 0)

    for slot in range(2):
        @pl.when(i % 2 == slot)
        def _():
            @pl.when(i + 1 < n_steps)
            def _():
                gather(i + 1, 1 - slot)

            drain(slot)

            @pl.when(i < n_p_steps)
            def _():
                finish(slot, 0, i // tiles_per_seq)

            @pl.when(i == n_p_steps)
            def _():
                finish(slot, 1, 0)


def _combine(pos, ys, x1_p, x1_s, rw, mod_p, mod_s, layer, ng, seq_rows_p, final):
    n_p, d = x1_p.shape
    rt = ROW_TILE
    assert x1_s.shape[0] == rt and n_p % rt == 0 and ROUTER_TILE % rt == 0
    n_p_steps = n_p // rt
    kern = functools.partial(_combine_kernel, tiles_per_seq=seq_rows_p // rt, n_p_steps=n_p_steps, final=final)
    p_spec = lambda w: pl.BlockSpec((rt, w), lambda i, p: (jnp.minimum(i, n_p_steps - 1), 0))
    s_spec = lambda w: pl.BlockSpec((rt, w), lambda i, p: (0, 0))
    nxt = min(layer + 1, mod_p.shape[0] - 1)

    def trunk_specs(spec, mod):
        return [spec(d), _mod_spec(mod, layer, 5), _mod_spec(mod, nxt, 1), _mod_spec(mod, nxt, 0)]

    out_dtypes = [F32] if final else [F32, BF16]
    out_shape = ([jax.ShapeDtypeStruct((n_p, d), t) for t in out_dtypes]
                 + [jax.ShapeDtypeStruct((rt, d), t) for t in out_dtypes])
    out_specs = [p_spec(d)] * len(out_dtypes) + [s_spec(d)] * len(out_dtypes)
    res = pl.pallas_call(
        kern,
        grid_spec=pltpu.PrefetchScalarGridSpec(
            num_scalar_prefetch=1,
            grid=(n_p_steps + 1,),
            in_specs=[pl.BlockSpec(memory_space=pl.ANY),
                      pl.BlockSpec((rt, V7X_LANES), lambda i, p: (i, 0)),
                      pl.BlockSpec((1, d), lambda i, p: (0, 0))]
            + trunk_specs(p_spec, mod_p) + trunk_specs(s_spec, mod_s),
            out_specs=out_specs,
            scratch_shapes=[pltpu.VMEM((TOP_K, rt * V7X_SUBLANES, V7X_LANES), U32)] * 2
            + [pltpu.SemaphoreType.DMA((2,))]),
        out_shape=out_shape,
        compiler_params=_cparams(1),
        name="moe_combine_final" if final else "moe_combine",
    )(pos, ys, rw, ng.reshape(1, d), x1_p, mod_p, mod_p, mod_p, x1_s, mod_s, mod_s, mod_s)
    return res[:len(out_dtypes)], res[len(out_dtypes):]


def _moe(out_p, out_s, mod_p, mod_s, layer, ng, seq_rows_p, w_router, b_router, w_gate, w_up, w_down, final):
    x1_p, hp_p, hpt_p = out_p
    x1_s, hp_s, hpt_s = out_s
    n_p = x1_p.shape[0]
    n_tok = n_p + x1_s.shape[0]
    max_rows = TOP_K * n_tok + N_EXPERTS * (MOE_TILE - 1)
    n_rows_sorted = -(-max_rows // MOE_TILE) * MOE_TILE
    pos, rw, tab = _router(hp_p, hp_s, w_router, b_router)
    xs = _dispatch(pos, tab, hpt_p, hpt_s, n_rows_sorted)
    ys = _experts(tab, xs, w_gate, w_up, w_down, layer)
    return _combine(pos, ys, x1_p, x1_s, rw, mod_p, mod_s, layer, ng, seq_rows_p, final)


def _gmlp_kernel(h_ref, w_ref, lg_ref, lb_ref, ws_ref, bs_ref, yc_ref, *rest, ell, blk, emit_v):
    if emit_v:
        gv_ref, wbf, wsbf = rest
    else:
        wbf, wsbf = rest
    i = pl.program_id(0)
    rows = h_ref.shape[0]
    c = GM_WIDTH

    @pl.when(i == 0)
    def _():
        wbf[...] = w_ref[...].astype(BF16)
        r = lax.broadcasted_iota(I32, (ell, ell), 0)
        s = lax.broadcasted_iota(I32, (ell, ell), 1)
        keep = (r >= s) & ((r // blk) == (s // blk))
        rsel = (lax.broadcasted_iota(I32, (ell, CHUNK), 0) % blk
                == lax.broadcasted_iota(I32, (ell, CHUNK), 1)).astype(BF16)
        csel = (lax.broadcasted_iota(I32, (CHUNK, ell), 1) % blk
                == lax.broadcasted_iota(I32, (CHUNK, ell), 0)).astype(BF16)
        for g in range(GM_GROUPS):
            wchunk = ws_ref[g].astype(BF16)
            if blk == ell:
                full = wchunk
            else:
                rowsp = jnp.dot(rsel, wchunk, preferred_element_type=F32).astype(BF16)
                full = jnp.dot(rowsp, csel, preferred_element_type=F32).astype(BF16)
            wsbf[g] = jnp.where(keep, full, jnp.zeros_like(full))

    uv = jnp.dot(h_ref[...], wbf[...], preferred_element_type=F32)
    u = uv[:, :c]
    v = uv[:, c:]
    vc = v - jnp.mean(v, axis=-1, keepdims=True)
    vn = vc * lax.rsqrt(jnp.mean(vc * vc, axis=-1, keepdims=True) + EPS) * lg_ref[...] + lb_ref[...]
    if emit_v:
        gv_ref[...] = vn
    vb = vn.astype(BF16)
    bs = bs_ref[...]
    for ch in range(rows // ell):
        rs = slice(ch * ell, (ch + 1) * ell)
        outs = []
        for g in range(GM_GROUPS):
            cs = slice(g * GM_GROUP, (g + 1) * GM_GROUP)
            mixed = jnp.dot(wsbf[g], vb[rs, cs], preferred_element_type=F32)
            mixed = (mixed.reshape(ell // blk, blk, GM_GROUP) + bs[:blk, g:g + 1][None]).reshape(ell, GM_GROUP)
            outs.append(u[rs, cs] * mixed)
        yc_ref[rs, :] = jnp.concatenate(outs, axis=-1).astype(BF16)


def _gmlp_mixer(h2, w_in, ln_g, ln_b, ws, bs_t, ell, blk, emit_v):
    rows, d = h2.shape
    c = GM_WIDTH
    kern = functools.partial(_gmlp_kernel, ell=ell, blk=blk, emit_v=emit_v)
    rt = min(rows, MATMUL_TILE)
    out_specs = [pl.BlockSpec((rt, c), lambda i: (i, 0))]
    out_shape = [jax.ShapeDtypeStruct((rows, c), BF16)]
    if emit_v:
        out_specs.append(pl.BlockSpec((rt, c), lambda i: (i, 0)))
        out_shape.append(jax.ShapeDtypeStruct((rows, c), F32))
    return pl.pallas_call(
        kern,
        grid=(rows // rt,),
        in_specs=[pl.BlockSpec((rt, d), lambda i: (i, 0)),
                  pl.BlockSpec((d, 2 * c), lambda i: (0, 0), pipeline_mode=pl.Buffered(1)),
                  pl.BlockSpec((1, c), lambda i: (0, 0)),
                  pl.BlockSpec((1, c), lambda i: (0, 0)),
                  pl.BlockSpec((GM_GROUPS, CHUNK, CHUNK), lambda i: (0, 0, 0)),
                  pl.BlockSpec((CHUNK, GM_GROUPS), lambda i: (0, 0))],
        out_specs=out_specs,
        out_shape=out_shape,
        scratch_shapes=[pltpu.VMEM((d, 2 * c), BF16), pltpu.VMEM((GM_GROUPS, ell, ell), BF16)],
        compiler_params=_cparams(1),
        name="gmlp_mixer",
    )(h2, w_in, ln_g.reshape(1, c), ln_b.reshape(1, c), ws, bs_t)


PAIR_W = 2 * HEAD_DIM
PAIRS_PER_KV = N_HEADS // N_KV // 2
NT_DIMS = (((1,), (1,)), ((), ()))
SCORE_SCALE = HEAD_DIM ** -0.5
assert float(np.log2(SCORE_SCALE)).is_integer()


def _swa_project(i, h_ref, wq_ref, wkv_ref, wbf):
    nq = N_HEADS * HEAD_DIM

    @pl.when(i == 0)
    def _():
        wbf[:, :nq] = wq_ref[...].astype(BF16)
        wbf[:, nq:] = wkv_ref[...].astype(BF16)

    return jnp.dot(h_ref[...], wbf[...], preferred_element_type=F32)


def _pair_block_diag(a, a_swapped, hk, axis):
    dim_axis = 1 - axis
    low = lax.broadcasted_iota(I32, a.shape, dim_axis) < HEAD_DIM
    lo, hi = (a, a_swapped) if hk == 0 else (a_swapped, a)
    return jnp.concatenate([jnp.where(low, lo, 0.0), jnp.where(low, 0.0, hi)], axis=axis).astype(BF16)


def _stack_pairs(qkv, rs, hk, scale=None):
    p0 = hk * PAIRS_PER_KV
    q = jnp.concatenate([qkv[rs, (p0 + pp) * PAIR_W:(p0 + pp + 1) * PAIR_W]
                         for pp in range(PAIRS_PER_KV)], axis=0)
    return (q if scale is None else q * scale).astype(BF16)


def _swa_cached_kernel(h_ref, wq_ref, wkv_ref, kp_ref, vp_ref, bias_ref, sink_ref, yd_ref, k_ref, v_ref,
                       wbf, *, tq):
    i = pl.program_id(0)
    rows = h_ref.shape[0]
    nq = N_HEADS * HEAD_DIM
    nkv = N_KV * HEAD_DIM
    n_blocks = rows // tq
    qkv = _swa_project(i, h_ref, wq_ref, wkv_ref, wbf)
    k_new = qkv[:, nq:nq + nkv]
    v_new = qkv[:, nq + nkv:]
    k_ref[...] = k_new
    v_ref[...] = v_new
    pad = jnp.zeros((WINDOW - tq, nkv), F32)

    scores, vbds = [], []
    for blk in range(n_blocks):
        rs = slice(blk * tq, (blk + 1) * tq)
        kcat = jnp.concatenate([kp_ref[blk], k_new[rs], pad], axis=0)
        vcat = jnp.concatenate([vp_ref[blk], v_new[rs], pad], axis=0)
        kswap = pltpu.roll(kcat, HEAD_DIM, 1)
        vswap = pltpu.roll(vcat, HEAD_DIM, 1)
        per_head = []
        for hk in range(N_KV):
            kbd = _pair_block_diag(kcat, kswap, hk, 0)
            vbds.append(_pair_block_diag(vcat, vswap, hk, 0))
            s4 = lax.dot_general(_stack_pairs(qkv, rs, hk), kbd, NT_DIMS,
                                 preferred_element_type=F32) * (HEAD_DIM ** -0.5)
            for pp in range(PAIRS_PER_KV):
                for sub in range(2):
                    per_head.append(s4[pp * tq:(pp + 1) * tq, sub * 2 * WINDOW:(sub + 1) * 2 * WINDOW])
        scores.append(jnp.concatenate(per_head, axis=0))

    s_all = jnp.stack(scores, axis=0) + bias_ref[...][None]
    sink = sink_ref[...][None]
    m = jnp.maximum(jnp.max(s_all, axis=-1, keepdims=True), sink)
    pr = jnp.exp(s_all - m)
    pr = pr / (jnp.sum(pr, axis=-1, keepdims=True) + jnp.exp(sink - m))

    for blk in range(n_blocks):
        outs = []
        for hk in range(N_KV):
            p4 = []
            for pp in range(PAIRS_PER_KV):
                h0 = 2 * (hk * PAIRS_PER_KV + pp)
                p4.append(jnp.concatenate([pr[blk, h0 * tq:(h0 + 1) * tq, :],
                                           pr[blk, (h0 + 1) * tq:(h0 + 2) * tq, :]], axis=-1))
            o4 = jnp.dot(jnp.concatenate(p4, axis=0).astype(BF16), vbds[blk * N_KV + hk],
                         preferred_element_type=F32)
            outs.extend(o4[pp * tq:(pp + 1) * tq, :] for pp in range(PAIRS_PER_KV))
        yd_ref[blk * tq:(blk + 1) * tq, :] = jnp.concatenate(outs, axis=-1).astype(BF16)


def _swa_stream_kernel(h_ref, wq_ref, wkv_ref, bias_ref, sink_ref, yd_ref, k_ref, v_ref,
                       wbf, kprev, vprev_t, *, blocks_per_seq):
    i = pl.program_id(0)
    rows = h_ref.shape[0]
    nq = N_HEADS * HEAD_DIM
    nkv = N_KV * HEAD_DIM
    tq = WINDOW
    n_blocks = rows // tq

    @pl.when(i == 0)
    def _():
        kprev[...] = jnp.zeros_like(kprev)
        vprev_t[...] = jnp.zeros_like(vprev_t)

    qkv = _swa_project(i, h_ref, wq_ref, wkv_ref, wbf)
    k_new = qkv[:, nq:nq + nkv]
    v_new = qkv[:, nq + nkv:]
    k_ref[...] = k_new
    v_ref[...] = v_new
    v_new_t = v_new.T
    lanes = PAIRS_PER_KV * tq

    for blk in range(n_blocks):
        rs = slice(blk * tq, (blk + 1) * tq)
        first = ((i * n_blocks + blk) % blocks_per_seq == 0).astype(I32)
        k_cur = k_new[rs]
        v_cur_t = v_new_t[:, rs]
        kcat = jnp.concatenate([kprev[...], k_cur], axis=0)
        vcat_t = jnp.concatenate([vprev_t[...], v_cur_t], axis=1)
        kprev[...] = k_cur
        vprev_t[...] = v_cur_t
        kswap = pltpu.roll(kcat, HEAD_DIM, 1)
        vswap_t = pltpu.roll(vcat_t, HEAD_DIM, 0)
        outs = []
        for hk in range(N_KV):
            kbd = _pair_block_diag(kcat, kswap, hk, 0)
            vbd_t = _pair_block_diag(vcat_t, vswap_t, hk, 1)
            st = lax.dot_general(kbd, _stack_pairs(qkv, rs, hk, SCORE_SCALE), NT_DIMS,
                                 preferred_element_type=F32)
            s3 = st.reshape(2, 2 * WINDOW, lanes) + bias_ref[first, hk]
            sink = sink_ref[hk]
            m = jnp.maximum(jnp.max(s3, axis=1, keepdims=True), sink)
            pr = jnp.exp(s3 - m)
            inv = 1.0 / (jnp.sum(pr, axis=1, keepdims=True) + jnp.exp(sink - m))
            o_t = jnp.dot(vbd_t, pr.reshape(4 * WINDOW, lanes).astype(BF16),
                          preferred_element_type=F32)
            norm = jnp.concatenate([jnp.broadcast_to(inv[sub], (HEAD_DIM, lanes)) for sub in range(2)], axis=0)
            o4 = (o_t * norm).T
            outs.extend(o4[pp * tq:(pp + 1) * tq, :] for pp in range(PAIRS_PER_KV))
        yd_ref[rs, :] = jnp.concatenate(outs, axis=-1).astype(BF16)


def _swa_weight_specs(w_in, d):
    nq = N_HEADS * HEAD_DIM
    nkv = N_KV * HEAD_DIM
    nw = nq + 2 * nkv
    q_blk = (w_in.shape[1] - nw) // nq
    kv_blk = (w_in.shape[1] - 2 * nkv) // (2 * nkv)
    assert q_blk * nq + nw == w_in.shape[1] and kv_blk * 2 * nkv + 2 * nkv == w_in.shape[1]
    return [pl.BlockSpec((d, nq), lambda i: (0, q_blk)), pl.BlockSpec((d, 2 * nkv), lambda i: (0, kv_blk))]


def _swa_outputs(rows, rt):
    nq = N_HEADS * HEAD_DIM
    nkv = N_KV * HEAD_DIM
    specs = [pl.BlockSpec((rt, nq), lambda i: (i, 0)),
             pl.BlockSpec((rt, nkv), lambda i: (i, 0)),
             pl.BlockSpec((rt, nkv), lambda i: (i, 0))]
    shapes = [jax.ShapeDtypeStruct((rows, nq), BF16),
              jax.ShapeDtypeStruct((rows, nkv), F32),
              jax.ShapeDtypeStruct((rows, nkv), F32)]
    return specs, shapes


def _swa_cached_mixer(h2, w_in, k_cache, v_cache, bias, sinks, tq):
    rows, d = h2.shape
    nkv = N_KV * HEAD_DIM
    nw = N_HEADS * HEAD_DIM + 2 * nkv
    n_blocks = ROW_TILE // tq
    cache_spec = pl.BlockSpec((n_blocks, WINDOW, nkv), lambda i: (i, 0, 0))
    out_specs, out_shape = _swa_outputs(rows, ROW_TILE)
    return pl.pallas_call(
        functools.partial(_swa_cached_kernel, tq=tq),
        grid=(rows // ROW_TILE,),
        in_specs=[pl.BlockSpec((ROW_TILE, d), lambda i: (i, 0))] + _swa_weight_specs(w_in, d)
        + [cache_spec, cache_spec,
           pl.BlockSpec((N_HEADS * tq, 2 * WINDOW), lambda i: (0, 0)),
           pl.BlockSpec((N_HEADS * tq, 1), lambda i: (0, 0))],
        out_specs=out_specs,
        out_shape=out_shape,
        scratch_shapes=[pltpu.VMEM((d, nw), BF16)],
        compiler_params=_cparams(1),
        name="swa_cached",
    )(h2, w_in, w_in, k_cache, v_cache, bias, sinks)


def _swa_stream_mixer(h2, w_in, bias_t, sinks_t, blocks_per_seq):
    rows, d = h2.shape
    nkv = N_KV * HEAD_DIM
    nw = N_HEADS * HEAD_DIM + 2 * nkv
    lanes = PAIRS_PER_KV * WINDOW
    rt = min(rows, MATMUL_TILE)
    out_specs, out_shape = _swa_outputs(rows, rt)
    return pl.pallas_call(
        functools.partial(_swa_stream_kernel, blocks_per_seq=blocks_per_seq),
        grid=(rows // rt,),
        in_specs=[pl.BlockSpec((rt, d), lambda i: (i, 0))] + _swa_weight_specs(w_in, d)
        + [pl.BlockSpec((2, N_KV, 2, 2 * WINDOW, lanes), lambda i: (0, 0, 0, 0, 0)),
           pl.BlockSpec((N_KV, 2, 1, lanes), lambda i: (0, 0, 0, 0))],
        out_specs=out_specs,
        out_shape=out_shape,
        scratch_shapes=[pltpu.VMEM((d, nw), BF16), pltpu.VMEM((WINDOW, nkv), F32),
                        pltpu.VMEM((nkv, WINDOW), F32)],
        compiler_params=_cparams(1),
        name="swa_stream",
    )(h2, w_in, w_in, bias_t, sinks_t)


def _t5_bucket(dist):
    max_exact = N_BUCKETS // 2
    dd = np.maximum(dist, 1)
    large = max_exact + (np.log(dd / max_exact) / np.log(WINDOW / max_exact)
                         * (N_BUCKETS - max_exact)).astype(np.int64)
    large = np.minimum(large, N_BUCKETS - 1)
    return np.where(dist < max_exact, dist, large).astype(np.int32)


def _attention_bias(rel_bias):
    by_dist = jnp.take(rel_bias.astype(F32), _t5_bucket(np.arange(WINDOW)), axis=0).T
    neg = jnp.full((N_HEADS, WINDOW), NEG_INF, F32)
    line = jnp.concatenate([neg, by_dist[:, ::-1], neg[:, :WINDOW - 1]], axis=1)
    rows = line[:, None, :]
    span = 1
    while span < WINDOW:
        rows = jnp.concatenate([rows[:, :, span:], rows[:, :, :rows.shape[2] - span]], axis=1)
        span *= 2
    return rows


def kernel(x_prompt, x_sample, state_pool, state_conv, cache_swa_k, cache_swa_v, c_prompt, c_sample, w_ada, b_ada, norm_g, final_norm_g, w_in_even, w_out_even, w_pool, pool_scale, conv_w, w_in_odd, w_out_odd, gm_norm_g, gm_norm_b, gm_w_s, gm_b_s, attn_sinks, rel_bias, w_router, b_router, w_gate, w_up, w_down):
    d = D_MODEL
    bp, tp, _ = x_prompt.shape
    bs, ts, _ = x_sample.shape
    rows_s = bs * ts
    assert rows_s == ROW_TILE and tp % ROUTER_TILE == 0 and PAST_LEN % CHUNK == 0
    assert bp <= V7X_SUBLANES and CHUNK % ts == 0

    n_c = bp + bs
    c_pad = (-n_c) % V7X_SUBLANES
    c_all = jnp.concatenate([c_prompt, c_sample, jnp.zeros((c_pad, d), F32)], axis=0)
    mod_p = _adaln(c_all, w_ada, b_ada)
    mod_s = jnp.repeat(mod_p[:, bp:bp + bs], ts, axis=1)

    xp = x_prompt.reshape(bp * tp, d)
    xs_ = x_sample.reshape(rows_s, d)
    w_in0, w_in1 = w_in_even[0], w_in_odd[0]

    hp0, ya_p, pool_p = _pool_mixer(xp, norm_g[0, 0], mod_p, 0, w_in0, w_pool[0], pool_scale[0],
                                    None, bp, 1, MATMUL_TILE, 0)
    hs0, ya_s, pool_s = _pool_mixer(xs_, norm_g[0, 0], mod_s, 0, w_in0, w_pool[0], pool_scale[0],
                                    state_pool[0], bs, bs, ts, PAST_LEN)
    yb_p, conv_p = _conv_mixer(hp0, w_in0, conv_w[0], None, bp, 1, MATMUL_TILE)
    yb_s, conv_s = _conv_mixer(hs0, w_in0, conv_w[0], state_conv[0], bs, bs, ts)
    out_p = _outproj(ya_p, yb_p, xp, mod_p, 0, norm_g[0, 1], w_out_even[0], tp)
    out_s = _outproj(ya_s, yb_s, xs_, mod_s, 0, norm_g[0, 1], w_out_even[0], rows_s)
    (x2p, h1p), (x2s, h1s) = _moe(out_p, out_s, mod_p, mod_s, 0, norm_g[1, 0], tp,
                                  w_router, b_router, w_gate, w_up, w_down, final=False)

    bs_t = gm_b_s[0].T
    (yc_p,) = _gmlp_mixer(h1p, w_in1, gm_norm_g[0], gm_norm_b[0], gm_w_s[0], bs_t, CHUNK, CHUNK, False)
    yc_s, gv_s = _gmlp_mixer(h1s, w_in1, gm_norm_g[0], gm_norm_b[0], gm_w_s[0], bs_t, rows_s, ts, True)
    bias = _attention_bias(rel_bias)
    nkv = N_KV * HEAD_DIM
    bias_t = jnp.transpose(bias.reshape(N_KV, PAIRS_PER_KV, 2, WINDOW, 2 * WINDOW), (0, 2, 4, 1, 3))
    bias_t = bias_t.reshape(N_KV, 2, 2 * WINDOW, PAIRS_PER_KV * WINDOW)
    before_start = (np.arange(2 * WINDOW) < WINDOW)[None, None, :, None]
    bias_t = jnp.stack([bias_t, jnp.where(before_start, NEG_INF, bias_t)], axis=0)
    sinks_t = jnp.transpose(attn_sinks[0].reshape(N_KV, PAIRS_PER_KV, 2), (0, 2, 1))
    sinks_t = jnp.repeat(sinks_t, WINDOW, axis=-1).reshape(N_KV, 2, 1, PAIRS_PER_KV * WINDOW)
    yd_p, k_p, v_p = _swa_stream_mixer(h1p, w_in1, bias_t, sinks_t, tp // WINDOW)
    yd_s, k_s, v_s = _swa_cached_mixer(h1s, w_in1, cache_swa_k[0].reshape(bs, WINDOW, nkv),
                                       cache_swa_v[0].reshape(bs, WINDOW, nkv),
                                       bias[:, :ts, :].reshape(N_HEADS * ts, 2 * WINDOW),
                                       jnp.repeat(attn_sinks[0], ts).reshape(-1, 1), ts)
    out_p = _outproj(yc_p, yd_p, x2p, mod_p, 1, norm_g[1, 1], w_out_odd[0], tp)
    out_s = _outproj(yc_s, yd_s, x2s, mod_s, 1, norm_g[1, 1], w_out_odd[0], rows_s)
    (yp,), (ys_out,) = _moe(out_p, out_s, mod_p, mod_s, 1, final_norm_g, tp,
                            w_router, b_router, w_gate, w_up, w_down, final=True)

    k_p4 = k_p.reshape(bp, tp, nkv)[:, -WINDOW:].reshape(bp, WINDOW, N_KV, HEAD_DIM)
    v_p4 = v_p.reshape(bp, tp, nkv)[:, -WINDOW:].reshape(bp, WINDOW, N_KV, HEAD_DIM)
    k_s4 = jnp.concatenate([cache_swa_k[0], k_s.reshape(bs, ts, N_KV, HEAD_DIM)], axis=1)[:, -WINDOW:]
    v_s4 = jnp.concatenate([cache_swa_v[0], v_s.reshape(bs, ts, N_KV, HEAD_DIM)], axis=1)[:, -WINDOW:]
    return (yp.reshape(bp, tp, d), ys_out.reshape(bs, ts, d),
            pool_p[None], pool_s[None], conv_p[None], conv_s[None],
            k_p4[None], k_s4[None], v_p4[None], v_s4[None],
            gv_s.reshape(bs, ts, GM_WIDTH)[None])
```
